```python
import math
import jax, jax.numpy as jnp
from jax import lax
import numpy as np

D_MODEL = 1024
BATCH = 8
SEQ = 4096
DEPTH = 2

N_MIXERS = 2
POOL_WINDOWS = (2, 4, 8, 16)
N_POOL_GROUPS = len(POOL_WINDOWS)
POOL_GROUP_W = D_MODEL // N_POOL_GROUPS
CONV_WIDTH = 3
D_FF = int(math.ceil(8 * D_MODEL / 3 / 256) * 256)
RMS_EPS = 1e-6
N_POOL_LAYERS = (DEPTH + 1) // 2
N_CONV_LAYERS = DEPTH // 2

kernel_name = "hybrid_pool_shortconv_sandwich"


def rms_norm(x, g):
    xf = x.astype(jnp.float32)
    y = xf * lax.rsqrt(jnp.mean(xf * xf, axis=-1, keepdims=True) + RMS_EPS)
    return (y * g.astype(jnp.float32)).astype(x.dtype)


def pool_mixer(h, w_groups, scale):
    bsz, seq, d = h.shape
    hg = h.astype(jnp.float32).reshape(bsz, seq, N_POOL_GROUPS, POOL_GROUP_W)
    cs = jnp.cumsum(hg, axis=1)
    pos = jnp.arange(1, seq + 1, dtype=jnp.float32)
    outs = []
    for g, w in enumerate(POOL_WINDOWS):
        c = cs[:, :, g]
        lagged = jnp.pad(c, ((0, 0), (w, 0), (0, 0)))[:, :seq]
        mean = (c - lagged) / jnp.minimum(pos, float(w))[None, :, None]
        outs.append(mean - hg[:, :, g])
    pooled = jnp.stack(outs, axis=2).astype(h.dtype)
    mixed = jnp.einsum('bsgc,gcd->bsgd', pooled, w_groups).reshape(bsz, seq, d)
    return mixed * scale


def short_conv_mixer(h, w_in, taps, w_out):
    seq = h.shape[1]
    proj = jnp.einsum('bsd,de->bse', h, w_in)
    b_gate, c_gate, v = jnp.split(proj, 3, axis=-1)
    u = c_gate * v
    up = jnp.pad(u, ((0, 0), (CONV_WIDTH - 1, 0), (0, 0)))
    conv = up[:, 0:seq] * taps[0]
    for k in range(1, CONV_WIDTH):
        conv = conv + up[:, k:k + seq] * taps[k]
    return jnp.einsum('bsd,de->bse', b_gate * conv, w_out)


def swiglu_ffn(h, w_gate_up, w_down):
    gu = jnp.einsum('bsd,df->bsf', h, w_gate_up)
    gate, up = jnp.split(gu, 2, axis=-1)
    return jnp.einsum('bsf,fd->bsd', jax.nn.silu(gate) * up, w_down)


def _fwd_setup_inputs(seed: int = 0) -> dict:
    key = jax.random.key(seed)
    ks = jax.random.split(key, 10)
    f32 = jnp.float32
    x = jax.random.normal(ks[0], (BATCH, SEQ, D_MODEL), f32)
    norm_gains = 1.0 + 0.05 * jax.random.normal(ks[1], (DEPTH, 4, D_MODEL), f32)
    pool_w = jax.random.normal(ks[2], (N_POOL_LAYERS, N_POOL_GROUPS, POOL_GROUP_W, POOL_GROUP_W), f32) * POOL_GROUP_W ** -0.5
    pool_scale = 1.0 + 0.1 * jax.random.normal(ks[3], (N_POOL_LAYERS, D_MODEL), f32)
    conv_in_w = jax.random.normal(ks[4], (N_CONV_LAYERS, D_MODEL, 3 * D_MODEL), f32) * D_MODEL ** -0.5
    conv_w = jax.random.normal(ks[5], (N_CONV_LAYERS, CONV_WIDTH, D_MODEL), f32) * CONV_WIDTH ** -0.5
    conv_out_w = jax.random.normal(ks[6], (N_CONV_LAYERS, D_MODEL, D_MODEL), f32) * D_MODEL ** -0.5
    ffn_gate_up_w = jax.random.normal(ks[7], (DEPTH, D_MODEL, 2 * D_FF), f32) * D_MODEL ** -0.5
    ffn_down_w = jax.random.normal(ks[8], (DEPTH, D_FF, D_MODEL), f32) * D_FF ** -0.5
    return {"x": x, "norm_gains": norm_gains, "pool_w": pool_w, "pool_scale": pool_scale,
            "conv_in_w": conv_in_w, "conv_w": conv_w, "conv_out_w": conv_out_w,
            "ffn_gate_up_w": ffn_gate_up_w, "ffn_down_w": ffn_down_w}


def _fwd_reference(x, norm_gains, pool_w, pool_scale, conv_in_w, conv_w, conv_out_w, ffn_gate_up_w, ffn_down_w):
    h = x
    for i in range(DEPTH):
        j = i // N_MIXERS
        g = norm_gains[i]
        hn = rms_norm(h, g[0])
        if i % N_MIXERS == 0:
            mix = pool_mixer(hn, pool_w[j], pool_scale[j])
        else:
            mix = short_conv_mixer(hn, conv_in_w[j], conv_w[j], conv_out_w[j])
        h = h + rms_norm(mix, g[1])
        ff = swiglu_ffn(rms_norm(h, g[2]), ffn_gate_up_w[i], ffn_down_w[i])
        h = h + rms_norm(ff, g[3])
    return h


import jax as _jax
import jax.numpy as _jnp

TWIN_FORMAT = 'train_step'
FWD_PARAMS = ['x', 'norm_gains', 'pool_w', 'pool_scale', 'conv_in_w', 'conv_w', 'conv_out_w', 'ffn_gate_up_w', 'ffn_down_w']
TWIN_WEIGHTS = ['norm_gains', 'pool_w', 'pool_scale', 'conv_in_w', 'conv_w', 'conv_out_w', 'ffn_gate_up_w', 'ffn_down_w']
TWIN_DIFF_INPUT = 'x'
TWIN_INPUTS = ['x', 'norm_gains', 'pool_w', 'pool_scale', 'conv_in_w', 'conv_w', 'conv_out_w', 'ffn_gate_up_w', 'ffn_down_w', 'loss_target', 'm_norm_gains', 'm_pool_w', 'm_pool_scale', 'm_conv_in_w', 'm_conv_w', 'm_conv_out_w', 'm_ffn_gate_up_w', 'm_ffn_down_w', 'v_norm_gains', 'v_pool_w', 'v_pool_scale', 'v_conv_in_w', 'v_conv_w', 'v_conv_out_w', 'v_ffn_gate_up_w', 'v_ffn_down_w']
TWIN_OUTPUTS = ['loss', 'grad_x', 'grad_norm_gains', 'grad_pool_w', 'grad_pool_scale', 'grad_conv_in_w', 'grad_conv_w', 'grad_conv_out_w', 'grad_ffn_gate_up_w', 'grad_ffn_down_w', 'delta_norm_gains', 'delta_pool_w', 'delta_pool_scale', 'delta_conv_in_w', 'delta_conv_w', 'delta_conv_out_w', 'delta_ffn_gate_up_w', 'delta_ffn_down_w', 'new_m_norm_gains', 'new_m_pool_w', 'new_m_pool_scale', 'new_m_conv_in_w', 'new_m_conv_w', 'new_m_conv_out_w', 'new_m_ffn_gate_up_w', 'new_m_ffn_down_w', 'new_v_norm_gains', 'new_v_pool_w', 'new_v_pool_scale', 'new_v_conv_in_w', 'new_v_conv_w', 'new_v_conv_out_w', 'new_v_ffn_gate_up_w', 'new_v_ffn_down_w']
TWIN_LEAF_KINDS = {'loss': 'loss', 'grad_x': 'grad_x', 'grad_norm_gains': 'grad_w', 'grad_pool_w': 'grad_w', 'grad_pool_scale': 'grad_w', 'grad_conv_in_w': 'grad_w', 'grad_conv_w': 'grad_w', 'grad_conv_out_w': 'grad_w', 'grad_ffn_gate_up_w': 'grad_w', 'grad_ffn_down_w': 'grad_w', 'delta_norm_gains': 'delta_w', 'delta_pool_w': 'delta_w', 'delta_pool_scale': 'delta_w', 'delta_conv_in_w': 'delta_w', 'delta_conv_w': 'delta_w', 'delta_conv_out_w': 'delta_w', 'delta_ffn_gate_up_w': 'delta_w', 'delta_ffn_down_w': 'delta_w', 'new_m_norm_gains': 'new_m', 'new_m_pool_w': 'new_m', 'new_m_pool_scale': 'new_m', 'new_m_conv_in_w': 'new_m', 'new_m_conv_w': 'new_m', 'new_m_conv_out_w': 'new_m', 'new_m_ffn_gate_up_w': 'new_m', 'new_m_ffn_down_w': 'new_m', 'new_v_norm_gains': 'new_v', 'new_v_pool_w': 'new_v', 'new_v_pool_scale': 'new_v', 'new_v_conv_in_w': 'new_v', 'new_v_conv_w': 'new_v', 'new_v_conv_out_w': 'new_v', 'new_v_ffn_gate_up_w': 'new_v', 'new_v_ffn_down_w': 'new_v'}


def _forward(args):
    return _fwd_reference(*[args[k] for k in FWD_PARAMS])


def _output_shape():
    def fwd():
        inp = _fwd_setup_inputs(0)
        return _fwd_reference(*[inp[k] for k in FWD_PARAMS])
    out = _jax.eval_shape(fwd)
    return out.shape, out.dtype

N_MICROBATCH = 1
ADAM_LR = 0.001
ADAM_B1 = 0.9
ADAM_B2 = 0.999
ADAM_EPS = 1e-08
ADAM_WD = 0.01
ADAM_STEP = 10
PER_EXAMPLE_BATCH_AXIS = {'x': 0, 'loss_target': 0}
SHARED_INPUTS = []
_WEIGHT_DTYPES = {'norm_gains': _jnp.float32, 'pool_w': _jnp.float32, 'pool_scale': _jnp.float32, 'conv_in_w': _jnp.float32, 'conv_w': _jnp.float32, 'conv_out_w': _jnp.float32, 'ffn_gate_up_w': _jnp.float32, 'ffn_down_w': _jnp.float32}
MOMENT_SCALE = {'norm_gains': 2.289075e+01, 'pool_w': 1.718505e+00, 'pool_scale': 3.339379e+00, 'conv_in_w': 4.958598e-01, 'conv_w': 5.238209e-01, 'conv_out_w': 5.447258e-01, 'ffn_gate_up_w': 4.096937e-01, 'ffn_down_w': 7.747925e-01}


def _to_microbatches(a, axis):
    t = _jnp.moveaxis(a, axis, 0)
    t = t.reshape((N_MICROBATCH, t.shape[0] // N_MICROBATCH) + t.shape[1:])
    return _jnp.moveaxis(t, 1, axis + 1)


def setup_inputs(seed: int = 0) -> dict:
    inp = _fwd_setup_inputs(seed)
    key = _jax.random.fold_in(_jax.random.key(seed), 7919)
    shape, _ = _output_shape()
    out = dict(inp)
    out["loss_target"] = _jax.random.normal(_jax.random.fold_in(key, 0), shape, _jnp.float32)
    for i, name in enumerate(TWIN_WEIGHTS):
        w = inp[name].astype(_jnp.float32)
        if MOMENT_SCALE is None:
            s = _jnp.sqrt(_jnp.mean(_jnp.square(w)) + 1e-30)
        else:
            s = MOMENT_SCALE[name]
        km, kv = _jax.random.split(_jax.random.fold_in(key, i + 1))
        out[name] = w
        out["m_" + name] = s * _jax.random.normal(km, w.shape, _jnp.float32)
        out["v_" + name] = (s * s) * _jax.random.uniform(kv, w.shape, _jnp.float32, 0.5, 1.5)
    if N_MICROBATCH > 1:
        for name, axis in PER_EXAMPLE_BATCH_AXIS.items():
            out[name] = _to_microbatches(out[name], axis)
    return {'x': out['x'], 'norm_gains': out['norm_gains'], 'pool_w': out['pool_w'], 'pool_scale': out['pool_scale'], 'conv_in_w': out['conv_in_w'], 'conv_w': out['conv_w'], 'conv_out_w': out['conv_out_w'], 'ffn_gate_up_w': out['ffn_gate_up_w'], 'ffn_down_w': out['ffn_down_w'], 'loss_target': out['loss_target'], 'm_norm_gains': out['m_norm_gains'], 'm_pool_w': out['m_pool_w'], 'm_pool_scale': out['m_pool_scale'], 'm_conv_in_w': out['m_conv_in_w'], 'm_conv_w': out['m_conv_w'], 'm_conv_out_w': out['m_conv_out_w'], 'm_ffn_gate_up_w': out['m_ffn_gate_up_w'], 'm_ffn_down_w': out['m_ffn_down_w'], 'v_norm_gains': out['v_norm_gains'], 'v_pool_w': out['v_pool_w'], 'v_pool_scale': out['v_pool_scale'], 'v_conv_in_w': out['v_conv_in_w'], 'v_conv_w': out['v_conv_w'], 'v_conv_out_w': out['v_conv_out_w'], 'v_ffn_gate_up_w': out['v_ffn_gate_up_w'], 'v_ffn_down_w': out['v_ffn_down_w']}


def _loss(weights, diff, rest, loss_target):
    with _jax.named_scope("forward"):
        args = {**rest, TWIN_DIFF_INPUT: diff, **{k: w.astype(_WEIGHT_DTYPES[k]) for k, w in weights.items()}}
        y = _forward(args)
    with _jax.named_scope("loss_head"):
        err = _jnp.square(y.astype(_jnp.float32) - loss_target)
        return 0.5 * _jnp.sum(_jnp.mean(err, axis=-1)) if err.ndim else 0.5 * err


def _adamw(w, g, m, v):
    m = ADAM_B1 * m + (1.0 - ADAM_B1) * g
    v = ADAM_B2 * v + (1.0 - ADAM_B2) * _jnp.square(g)
    m_hat = m / (1.0 - ADAM_B1 ** ADAM_STEP)
    v_hat = v / (1.0 - ADAM_B2 ** ADAM_STEP)
    delta = -ADAM_LR * (m_hat / (_jnp.sqrt(v_hat) + ADAM_EPS) + ADAM_WD * w)
    return delta, m, v


def reference(x, norm_gains, pool_w, pool_scale, conv_in_w, conv_w, conv_out_w, ffn_gate_up_w, ffn_down_w, loss_target, m_norm_gains, m_pool_w, m_pool_scale, m_conv_in_w, m_conv_w, m_conv_out_w, m_ffn_gate_up_w, m_ffn_down_w, v_norm_gains, v_pool_w, v_pool_scale, v_conv_in_w, v_conv_w, v_conv_out_w, v_ffn_gate_up_w, v_ffn_down_w):
    given = dict(x=x, norm_gains=norm_gains, pool_w=pool_w, pool_scale=pool_scale, conv_in_w=conv_in_w, conv_w=conv_w, conv_out_w=conv_out_w, ffn_gate_up_w=ffn_gate_up_w, ffn_down_w=ffn_down_w, loss_target=loss_target, m_norm_gains=m_norm_gains, m_pool_w=m_pool_w, m_pool_scale=m_pool_scale, m_conv_in_w=m_conv_in_w, m_conv_w=m_conv_w, m_conv_out_w=m_conv_out_w, m_ffn_gate_up_w=m_ffn_gate_up_w, m_ffn_down_w=m_ffn_down_w, v_norm_gains=v_norm_gains, v_pool_w=v_pool_w, v_pool_scale=v_pool_scale, v_conv_in_w=v_conv_in_w, v_conv_w=v_conv_w, v_conv_out_w=v_conv_out_w, v_ffn_gate_up_w=v_ffn_gate_up_w, v_ffn_down_w=v_ffn_down_w)
    weights = {n: given[n] for n in TWIN_WEIGHTS}
    shared = {n: given[n] for n in SHARED_INPUTS}
    per_example = {n: given[n] for n in ['x']}
    grad_fn = _jax.value_and_grad(_loss, argnums=(0, 1))

    def one_microbatch(ex, loss_target):
        ex = dict(ex)
        diff = ex.pop(TWIN_DIFF_INPUT)
        return grad_fn(weights, diff, {**shared, **ex}, loss_target)

    if N_MICROBATCH == 1:
        loss, (grad_w, grad_x) = one_microbatch(per_example, given["loss_target"])
    else:
        def body(carry, xs):
            loss_sum, grad_sum = carry
            l_k, (gw_k, gx_k) = one_microbatch(xs[0], xs[1])
            with _jax.named_scope("update"):
                return (loss_sum + l_k, _jax.tree.map(_jnp.add, grad_sum, gw_k)), gx_k

        init = (_jnp.zeros((), _jnp.float32), _jax.tree.map(_jnp.zeros_like, weights))
        (loss, grad_w), grad_x = _jax.lax.scan(body, init, (per_example, given["loss_target"]))
    with _jax.named_scope("update"):
        delta_w, new_m, new_v = {}, {}, {}
        for n in TWIN_WEIGHTS:
            delta_w[n], new_m[n], new_v[n] = _adamw(weights[n], grad_w[n], given["m_" + n], given["v_" + n])
    return (loss, grad_x, *[grad_w[n] for n in TWIN_WEIGHTS], *[delta_w[n] for n in TWIN_WEIGHTS],
            *[new_m[n] for n in TWIN_WEIGHTS], *[new_v[n] for n in TWIN_WEIGHTS])
```

```python
import functools

import jax
import jax.numpy as jnp
from jax import lax
from jax.experimental import pallas as pl
from jax.experimental.pallas import tpu as pltpu

RMS_EPS = 1e-6
POOL_WINDOWS = (2, 4, 8, 16)
POOL_HALO = 16
CONV_HALO = 8
N_CHIPS = 4
N_DEV = 8
ADAM_LR = 0.001
ADAM_B1 = 0.9
ADAM_B2 = 0.999
ADAM_EPS = 1e-08
ADAM_WD = 0.01
ADAM_STEP = 10
VMEM_LIMIT = 56 * 2**20
DOWN_GRAD_ROWS = 256
MESH = pl.DeviceIdType.MESH
ANY = pl.BlockSpec(memory_space=pl.ANY)
BF16 = jnp.bfloat16
F32 = jnp.float32


def _params(*sem):
    return pltpu.CompilerParams(dimension_semantics=sem, vmem_limit_bytes=VMEM_LIMIT)


def _token_tile(t):
    return min(256, t)


def _rms(x):
    r = lax.rsqrt(jnp.mean(x * x, axis=-1, keepdims=True) + RMS_EPS)
    return x * r, r


def _rms_bwd(dy, xh, r, g):
    a = dy * g
    return r * (a - xh * jnp.mean(a * xh, axis=-1, keepdims=True))


def _dot(a, b):
    return jnp.dot(a, b, preferred_element_type=F32)


def _dot_nt(a, b):
    return lax.dot_general(a, b, (((1,), (1,)), ((), ())), preferred_element_type=F32)


def _dot_tn(a, b):
    return lax.dot_general(a, b, (((0,), (0,)), ((), ())), preferred_element_type=F32)


def _colsum(a):
    return jnp.sum(a, axis=0, keepdims=True)


def _resident(block, index_map):
    return pl.BlockSpec(block, index_map, pipeline_mode=pl.Buffered(1))


def _place():
    x, y, c = lax.axis_index("x"), lax.axis_index("y"), lax.axis_index("c")
    return x, y, c, 2 * x + y


def _dev(chip, core):
    return (chip // 2, chip % 2, core)


def _piece(ref, chip, half, half_major):
    return ref.at[half, chip] if half_major else ref.at[chip, half]


def _all_gather(shards, half_major, name):
    n = len(shards)

    def body(*refs):
        loc, out = refs[:n], refs[n:2 * n]
        own_sems, send_sems, recv_sems, fsend_sems, frecv_sems = refs[2 * n:]
        x, y, c, k = _place()
        owns, sends = [], []
        for a in range(n):
            hm = half_major[a]
            own = pltpu.make_async_copy(loc[a], out[a].at[:, k] if hm else out[a].at[k], own_sems.at[a])
            own.start()
            owns.append(own)
            for m in range(1, N_CHIPS):
                cp = pltpu.make_async_remote_copy(
                    src_ref=loc[a].at[c], dst_ref=_piece(out[a], k, c, hm),
                    send_sem=send_sems.at[a, m - 1], recv_sem=recv_sems.at[a, m - 1],
                    device_id=_dev(k ^ m, c), device_id_type=MESH)
                cp.start()
                sends.append(cp)
        fwds = []
        for a in range(n):
            hm = half_major[a]
            for m in range(1, N_CHIPS):
                got = _piece(out[a], k ^ m, c, hm)
                pltpu.make_async_remote_copy(
                    src_ref=got, dst_ref=got, send_sem=send_sems.at[a, m - 1], recv_sem=recv_sems.at[a, m - 1],
                    device_id=_dev(k ^ m, c), device_id_type=MESH).wait_recv()
                fw = pltpu.make_async_remote_copy(
                    src_ref=got, dst_ref=got, send_sem=fsend_sems.at[a, m - 1], recv_sem=frecv_sems.at[a, m - 1],
                    device_id=(x, y, 1 - c), device_id_type=MESH)
                fw.start()
                fwds.append(fw)
        for a in range(n):
            hm = half_major[a]
            for m in range(1, N_CHIPS):
                got = _piece(out[a], k ^ m, 1 - c, hm)
                pltpu.make_async_remote_copy(
                    src_ref=got, dst_ref=got, send_sem=fsend_sems.at[a, m - 1], recv_sem=frecv_sems.at[a, m - 1],
                    device_id=(x, y, 1 - c), device_id_type=MESH).wait_recv()
        for cp in sends + fwds:
            cp.wait_send()
        for own in owns:
            own.wait()

    out_shape = []
    for s, hm in zip(shards, half_major):
        _, r, cc = s.shape
        out_shape.append(jax.ShapeDtypeStruct((2, N_CHIPS, r, cc) if hm else (N_CHIPS, 2, r, cc), s.dtype))
    return pl.pallas_call(
        body, name=name, out_shape=out_shape, in_specs=[ANY] * n, out_specs=[ANY] * n,
        scratch_shapes=[pltpu.SemaphoreType.DMA((n,)), pltpu.SemaphoreType.DMA((n, 3)), pltpu.SemaphoreType.DMA((n, 3)),
                        pltpu.SemaphoreType.DMA((n, 3)), pltpu.SemaphoreType.DMA((n, 3))],
        compiler_params=pltpu.CompilerParams(has_side_effects=True),
    )(*shards)


def _sibling_exchange(grads, small):
    n = len(grads)

    def body(*refs):
        g, sm = refs[:n], refs[n]
        land, smg = refs[n + 1:2 * n + 1], refs[2 * n + 1]
        send_sems, recv_sems, own_sem, ssend_sems, srecv_sems = refs[2 * n + 2:]
        x, y, c, k = _place()
        me = 2 * k + c
        cps = []
        for a in range(n):
            cp = pltpu.make_async_remote_copy(
                src_ref=g[a].at[:, pl.ds(1 - c, 1)], dst_ref=land[a], send_sem=send_sems.at[a], recv_sem=recv_sems.at[a],
                device_id=(x, y, 1 - c), device_id_type=MESH)
            cp.start()
            cps.append(cp)
        own = pltpu.make_async_copy(sm, smg.at[me], own_sem)
        own.start()
        scs = []
        for m in range(1, N_DEV):
            peer = me ^ m
            sc = pltpu.make_async_remote_copy(
                src_ref=sm, dst_ref=smg.at[me], send_sem=ssend_sems.at[m - 1], recv_sem=srecv_sems.at[m - 1],
                device_id=(peer // 4, (peer // 2) % 2, peer % 2), device_id_type=MESH)
            sc.start()
            scs.append(sc)
        for cp in cps:
            cp.wait_recv()
        for m in range(1, N_DEV):
            peer = me ^ m
            pltpu.make_async_remote_copy(
                src_ref=sm, dst_ref=smg.at[peer], send_sem=ssend_sems.at[m - 1], recv_sem=srecv_sems.at[m - 1],
                device_id=(peer // 4, (peer // 2) % 2, peer % 2), device_id_type=MESH).wait_recv()
        for cp in cps + scs:
            cp.wait_send()
        own.wait()

    out_shape = [jax.ShapeDtypeStruct((N_CHIPS, 1) + a.shape[2:], a.dtype) for a in grads]
    out_shape.append(jax.ShapeDtypeStruct((N_DEV,) + small.shape, small.dtype))
    return pl.pallas_call(
        body, name="rs_sibling_exchange", out_shape=out_shape, in_specs=[ANY] * (n + 1), out_specs=[ANY] * (n + 1),
        scratch_shapes=[pltpu.SemaphoreType.DMA((n,)), pltpu.SemaphoreType.DMA((n,)), pltpu.SemaphoreType.DMA,
                        pltpu.SemaphoreType.DMA((N_DEV - 1,)), pltpu.SemaphoreType.DMA((N_DEV - 1,))],
        compiler_params=pltpu.CompilerParams(has_side_effects=True),
    )(*grads, small)


def _chip_exchange(parts):
    n = len(parts)

    def body(*refs):
        p, land = refs[:n], refs[n:2 * n]
        send_sems, recv_sems = refs[2 * n:]
        x, y, c, k = _place()
        cps = []
        for a in range(n):
            for m in range(1, N_CHIPS):
                cp = pltpu.make_async_remote_copy(
                    src_ref=p[a].at[k ^ m], dst_ref=land[a].at[m - 1], send_sem=send_sems.at[a, m - 1],
                    recv_sem=recv_sems.at[a, m - 1], device_id=_dev(k ^ m, c), device_id_type=MESH)
                cp.start()
                cps.append(cp)
        for cp in cps:
            cp.wait_recv()
        for cp in cps:
            cp.wait_send()

    out_shape = [jax.ShapeDtypeStruct((N_CHIPS - 1,) + a.shape[1:], a.dtype) for a in parts]
    return pl.pallas_call(
        body, name="rs_chip_exchange", out_shape=out_shape, in_specs=[ANY] * n, out_specs=[ANY] * n,
        scratch_shapes=[pltpu.SemaphoreType.DMA((n, 3)), pltpu.SemaphoreType.DMA((n, 3))],
        compiler_params=pltpu.CompilerParams(has_side_effects=True),
    )(*parts)


def _sibling_share(halves):
    n = len(halves)

    def body(*refs):
        h, out = refs[:n], refs[n:2 * n]
        own_sems, send_sems, recv_sems = refs[2 * n:]
        x, y, c, k = _place()
        cps, owns = [], []
        for a in range(n):
            own = pltpu.make_async_copy(h[a], out[a].at[c], own_sems.at[a])
            own.start()
            owns.append(own)
            cp = pltpu.make_async_remote_copy(
                src_ref=h[a], dst_ref=out[a].at[c], send_sem=send_sems.at[a], recv_sem=recv_sems.at[a],
                device_id=(x, y, 1 - c), device_id_type=MESH)
            cp.start()
            cps.append(cp)
        for a in range(n):
            pltpu.make_async_remote_copy(
                src_ref=h[a], dst_ref=out[a].at[1 - c], send_sem=send_sems.at[a], recv_sem=recv_sems.at[a],
                device_id=(x, y, 1 - c), device_id_type=MESH).wait_recv()
        for cp in cps:
            cp.wait_send()
        for own in owns:
            own.wait()

    out_shape = [jax.ShapeDtypeStruct((2,) + a.shape, a.dtype) for a in halves]
    return pl.pallas_call(
        body, name="rs_sibling_share", out_shape=out_shape, in_specs=[ANY] * n, out_specs=[ANY] * n,
        scratch_shapes=[pltpu.SemaphoreType.DMA((n,)), pltpu.SemaphoreType.DMA((n,)), pltpu.SemaphoreType.DMA((n,))],
        compiler_params=pltpu.CompilerParams(has_side_effects=True),
    )(*halves)


STREAM_BUDGET = 24 * 2**20


def _row_block(r, row_bytes):
    best = None
    for rb in range(16, r + 1, 16):
        if r % rb == 0 and rb * row_bytes <= STREAM_BUDGET:
            best = rb
    return best if best is not None else r


def _add_sibling(g, land, core):
    _, _, r, c = g.shape
    rb = _row_block(r, c * (3 * 2 * 2 + 2 * 4))

    def body(core_ref, g_ref, l_ref, o_ref):
        o_ref[...] = (g_ref[...].astype(F32) + l_ref[...].astype(F32)).astype(o_ref.dtype)

    return pl.pallas_call(
        body, name="rs_add_sibling", out_shape=jax.ShapeDtypeStruct((N_CHIPS, r, c), g.dtype),
        grid_spec=pltpu.PrefetchScalarGridSpec(
            num_scalar_prefetch=1, grid=(N_CHIPS, r // rb),
            in_specs=[pl.BlockSpec((None, None, rb, c), lambda j, i, core_ref: (j, core_ref[0], i, 0)),
                      pl.BlockSpec((None, None, rb, c), lambda j, i, core_ref: (j, 0, i, 0))],
            out_specs=pl.BlockSpec((None, rb, c), lambda j, i, core_ref: (j, i, 0))),
        compiler_params=_params("parallel", "parallel"),
    )(core, g, land)


def _add_chips(part, land, chip):
    _, r, c = part.shape
    rb = _row_block(r, c * (4 * 2 * 2 + 4 * 2 + 2 * 4))

    def body(chip_ref, p_ref, l_ref, o_ref):
        acc = p_ref[...].astype(F32)
        for m in range(N_CHIPS - 1):
            acc = acc + l_ref[m].astype(F32)
        o_ref[...] = acc

    return pl.pallas_call(
        body, name="rs_add_chips", out_shape=jax.ShapeDtypeStruct((r, c), F32),
        grid_spec=pltpu.PrefetchScalarGridSpec(
            num_scalar_prefetch=1, grid=(r // rb,),
            in_specs=[pl.BlockSpec((None, rb, c), lambda i, chip_ref: (chip_ref[0], i, 0)),
                      pl.BlockSpec((N_CHIPS - 1, rb, c), lambda i, chip_ref: (0, i, 0))],
            out_specs=pl.BlockSpec((rb, c), lambda i, chip_ref: (i, 0))),
        compiler_params=_params("parallel"),
    )(chip, part, land)


def _sum_small(smg):
    def body(s_ref, o_ref):
        acc = s_ref[0]
        for j in range(1, N_DEV):
            acc = acc + s_ref[j]
        o_ref[...] = acc

    return pl.pallas_call(body, name="rs_sum_small", out_shape=jax.ShapeDtypeStruct(smg.shape[1:], F32))(smg)


def _pool_windows(ext_ref, g, gw, tm, first_row):
    w = POOL_WINDOWS[g]
    slab = ext_ref[:, g * gw:(g + 1) * gw]
    p, k = slab, 1
    while k < w:
        p = p + pltpu.roll(p, k, 0)
        k *= 2
    t = first_row + lax.broadcasted_iota(jnp.int32, (tm, 1), 0)
    cnt = jnp.minimum(t + 1, w).astype(F32)
    return p[POOL_HALO:] / cnt - slab[POOL_HALO:], cnt


def _fwd_pool(x, small, scale, poolw):
    t, d = x.shape
    tm = _token_tile(t)
    gw = d // len(POOL_WINDOWS)

    def body(x_ref, sm_ref, sc_ref, w_ref, h_ref, ext_ref, mix_ref):
        i = pl.program_id(0)

        @pl.when(i == 0)
        def _():
            ext_ref[0:POOL_HALO, :] = jnp.zeros((POOL_HALO, d), F32)

        @pl.when(i > 0)
        def _():
            ext_ref[0:POOL_HALO, :] = ext_ref[tm:tm + POOL_HALO, :]

        xv = x_ref[...]
        xh, _ = _rms(xv)
        ext_ref[POOL_HALO:, :] = xh * sm_ref[0:1, :]
        for g in range(len(POOL_WINDOWS)):
            pooled, _ = _pool_windows(ext_ref, g, gw, tm, i * tm)
            cols = slice(g * gw, (g + 1) * gw)
            mix_ref[:, cols] = _dot(pooled.astype(BF16), w_ref[g]) * sc_ref[:, cols]
        mh, _ = _rms(mix_ref[...])
        h_ref[...] = xv + mh * sm_ref[1:2, :]

    return pl.pallas_call(
        body, name="fwd_pool", grid=(t // tm,), out_shape=jax.ShapeDtypeStruct((t, d), F32),
        in_specs=[pl.BlockSpec((tm, d), lambda i: (i, 0)), _resident(small.shape, lambda i: (0, 0)),
                  _resident(scale.shape, lambda i: (0, 0)), _resident(poolw.shape, lambda i: (0, 0, 0))],
        out_specs=pl.BlockSpec((tm, d), lambda i: (i, 0)),
        scratch_shapes=[pltpu.VMEM((POOL_HALO + tm, d), F32), pltpu.VMEM((tm, d), F32)],
        compiler_params=_params("arbitrary"),
    )(x, small, scale, poolw)


def _fwd_ffn(h, small, wgu, wd, layer):
    t, d = h.shape
    tm = _token_tile(t)
    fc = wgu.shape[-1]
    f = 2 * fc
    g_in, g_out = 4 * layer + 2, 4 * layer + 3

    def body(h_ref, sm_ref, wgu_ref, wd_ref, o_ref, gu_ref, ff_ref, n_ref):
        hv = h_ref[...]
        hh, _ = _rms(hv)
        n = (hh * sm_ref[g_in:g_in + 1, :]).astype(BF16)
        n_ref[...] = n
        ff = None
        for j in range(2):
            gate = _dot(n, wgu_ref[j])
            up = _dot(n, wgu_ref[2 + j])
            gu_ref[:, j * fc:(j + 1) * fc] = gate.astype(BF16)
            gu_ref[:, f + j * fc:f + (j + 1) * fc] = up.astype(BF16)
            act = (gate * jax.nn.sigmoid(gate) * up).astype(BF16)
            part = _dot(act, wd_ref[j * fc:(j + 1) * fc, :])
            ff = part if ff is None else ff + part
        ff_ref[...] = ff
        fh, _ = _rms(ff)
        o_ref[...] = hv + fh * sm_ref[g_out:g_out + 1, :]

    row = lambda i: (i, 0)
    return pl.pallas_call(
        body, name=f"fwd_ffn{layer}", grid=(t // tm,),
        out_shape=[jax.ShapeDtypeStruct((t, d), F32), jax.ShapeDtypeStruct((t, 2 * f), BF16),
                   jax.ShapeDtypeStruct((t, d), F32), jax.ShapeDtypeStruct((t, d), BF16)],
        in_specs=[pl.BlockSpec((tm, d), row), _resident(small.shape, lambda i: (0, 0)),
                  _resident((None, N_CHIPS, d, fc), lambda i: (layer, 0, 0, 0)),
                  _resident((None, f, d), lambda i: (layer, 0, 0))],
        out_specs=[pl.BlockSpec((tm, d), row), pl.BlockSpec((tm, 2 * f), row), pl.BlockSpec((tm, d), row),
                   pl.BlockSpec((tm, d), row)],
        compiler_params=_params("parallel"),
    )(h, small, wgu, wd)


def _conv_taps(uext_ref, taps, tm):
    full = uext_ref[...]
    return (full[CONV_HALO:] * taps[2] + pltpu.roll(full, 1, 0)[CONV_HALO:] * taps[1]
            + pltpu.roll(full, 2, 0)[CONV_HALO:] * taps[0])


def _fwd_conv(h, small, win, wout):
    t, d = h.shape
    tm = _token_tile(t)
    pc = win.shape[-1]

    def body(h_ref, sm_ref, win_ref, wout_ref, o_ref, proj_ref, y_ref, n_ref, pj_ref, uext_ref):
        i = pl.program_id(0)

        @pl.when(i == 0)
        def _():
            uext_ref[0:CONV_HALO, :] = jnp.zeros((CONV_HALO, d), F32)

        @pl.when(i > 0)
        def _():
            uext_ref[0:CONV_HALO, :] = uext_ref[tm:tm + CONV_HALO, :]

        hv = h_ref[...]
        hh, _ = _rms(hv)
        n = (hh * sm_ref[4:5, :]).astype(BF16)
        n_ref[...] = n
        for k in range(N_CHIPS):
            pj_ref[:, k * pc:(k + 1) * pc] = _dot(n, win_ref[k])
        proj_ref[...] = pj_ref[...].astype(BF16)
        uext_ref[CONV_HALO:, :] = pj_ref[:, d:2 * d] * pj_ref[:, 2 * d:]
        taps = [sm_ref[8 + j:9 + j, :] for j in range(3)]
        conv = _conv_taps(uext_ref, taps, tm)
        y = _dot((pj_ref[:, 0:d] * conv).astype(BF16), wout_ref[...])
        y_ref[...] = y
        yh, _ = _rms(y)
        o_ref[...] = hv + yh * sm_ref[5:6, :]

    row = lambda i: (i, 0)
    return pl.pallas_call(
        body, name="fwd_conv", grid=(t // tm,),
        out_shape=[jax.ShapeDtypeStruct((t, d), F32), jax.ShapeDtypeStruct((t, 3 * d), BF16),
                   jax.ShapeDtypeStruct((t, d), F32), jax.ShapeDtypeStruct((t, d), BF16)],
        in_specs=[pl.BlockSpec((tm, d), row), _resident(small.shape, lambda i: (0, 0)),
                  _resident(win.shape, lambda i: (0, 0, 0)), _resident(wout.shape, lambda i: (0, 0))],
        out_specs=[pl.BlockSpec((tm, d), row), pl.BlockSpec((tm, 3 * d), row), pl.BlockSpec((tm, d), row),
                   pl.BlockSpec((tm, d), row)],
        scratch_shapes=[pltpu.VMEM((tm, 3 * d), F32), pltpu.VMEM((CONV_HALO + tm, d), F32)],
        compiler_params=_params("arbitrary"),
    )(h, small, win, wout)


def _loss_grad(h, target):
    t, d = h.shape
    tm = _token_tile(t)
    steps = t // tm

    def body(h_ref, t_ref, g_ref, l_ref, acc_ref):
        i = pl.program_id(0)
        e = h_ref[...] - t_ref[...]
        g_ref[...] = e * (1.0 / d)

        @pl.when(i == 0)
        def _():
            acc_ref[...] = jnp.zeros_like(acc_ref)

        acc_ref[...] += _colsum(e * e)

        @pl.when(i == steps - 1)
        def _():
            l_ref[...] = jnp.full(l_ref.shape, 0.5 / d, F32) * jnp.sum(acc_ref[...])

    row = lambda i: (i, 0)
    return pl.pallas_call(
        body, name="loss_grad", grid=(steps,),
        out_shape=[jax.ShapeDtypeStruct((t, d), F32), jax.ShapeDtypeStruct((8, 128), F32)],
        in_specs=[pl.BlockSpec((tm, d), row), pl.BlockSpec((tm, d), row)],
        out_specs=[pl.BlockSpec((tm, d), row), pl.BlockSpec((8, 128), lambda i: (0, 0))],
        scratch_shapes=[pltpu.VMEM((1, d), F32)],
        compiler_params=_params("arbitrary"),
    )(h, target)


def _bwd_ffn(dh, h, ff, gu, small, wgu, wd, layer):
    t, d = h.shape
    tm = _token_tile(t)
    fc = wgu.shape[-1]
    f = 2 * fc
    g_in, g_out = 4 * layer + 2, 4 * layer + 3

    def body(dh_ref, h_ref, ff_ref, gu_ref, sm_ref, wgu_ref, wd_ref, o_ref, dgu_ref, dff_ref, act_ref, sg_ref):
        i = pl.program_id(0)

        @pl.when(i == 0)
        def _():
            sg_ref[...] = jnp.zeros_like(sg_ref)

        dy = dh_ref[...]
        fh, r3 = _rms(ff_ref[...])
        sg_ref[1:2, :] += _colsum(dy * fh)
        dff = _rms_bwd(dy, fh, r3, sm_ref[g_out:g_out + 1, :]).astype(BF16)
        dff_ref[...] = dff
        for j in range(2):
            dact = _dot_nt(dff, wd_ref[j * fc:(j + 1) * fc, :])
            gate = gu_ref[:, j * fc:(j + 1) * fc].astype(F32)
            up = gu_ref[:, f + j * fc:f + (j + 1) * fc].astype(F32)
            sig = jax.nn.sigmoid(gate)
            silu = gate * sig
            act_ref[:, j * fc:(j + 1) * fc] = (silu * up).astype(BF16)
            dgu_ref[:, j * fc:(j + 1) * fc] = (dact * up * (sig * (1.0 + gate * (1.0 - sig)))).astype(BF16)
            dgu_ref[:, f + j * fc:f + (j + 1) * fc] = (dact * silu).astype(BF16)
        dn = None
        for k in range(N_CHIPS):
            part = _dot_nt(dgu_ref[:, k * fc:(k + 1) * fc], wgu_ref[k])
            dn = part if dn is None else dn + part
        hh, r2 = _rms(h_ref[...])
        sg_ref[0:1, :] += _colsum(dn * hh)
        o_ref[...] = dy + _rms_bwd(dn, hh, r2, sm_ref[g_in:g_in + 1, :])

    row = lambda i: (i, 0)
    return pl.pallas_call(
        body, name=f"bwd_ffn{layer}", grid=(t // tm,),
        out_shape=[jax.ShapeDtypeStruct((t, d), F32), jax.ShapeDtypeStruct((t, 2 * f), BF16),
                   jax.ShapeDtypeStruct((t, d), BF16), jax.ShapeDtypeStruct((t, f), BF16),
                   jax.ShapeDtypeStruct((8, d), F32)],
        in_specs=[pl.BlockSpec((tm, d), row), pl.BlockSpec((tm, d), row), pl.BlockSpec((tm, d), row),
                  pl.BlockSpec((tm, 2 * f), row), _resident(small.shape, lambda i: (0, 0)),
                  _resident((None, N_CHIPS, d, fc), lambda i: (layer, 0, 0, 0)),
                  _resident((None, f, d), lambda i: (layer, 0, 0))],
        out_specs=[pl.BlockSpec((tm, d), row), pl.BlockSpec((tm, 2 * f), row), pl.BlockSpec((tm, d), row),
                   pl.BlockSpec((tm, f), row), pl.BlockSpec((8, d), lambda i: (0, 0))],
        compiler_params=_params("arbitrary"),
    )(dh, h, ff, gu, small, wgu, wd)


def _bwd_conv(dh, h, y, proj, small, win, wout):
    t, d = h.shape
    tm = _token_tile(t)
    steps = t // tm
    pc = win.shape[-1]
    halo_blocks = tm // 16

    def body(dh_ref, h_ref, y_ref, proj_ref, halo_ref, sm_ref, win_ref, wout_ref,
             o_ref, dproj_ref, dy_ref, bc_ref, sg_ref, uext_ref, dcext_ref, carry_ref):
        i = pl.program_id(0)
        tile = steps - 1 - i

        @pl.when(i == 0)
        def _():
            sg_ref[...] = jnp.zeros_like(sg_ref)
            carry_ref[...] = jnp.zeros_like(carry_ref)

        dy = dh_ref[...]
        yh, r1 = _rms(y_ref[...])
        sg_ref[1:2, :] += _colsum(dy * yh)
        dyv = _rms_bwd(dy, yh, r1, sm_ref[5:6, :]).astype(BF16)
        dy_ref[...] = dyv
        dbc = _dot_nt(dyv, wout_ref[...])
        b = proj_ref[:, 0:d].astype(F32)
        cg = proj_ref[:, d:2 * d].astype(F32)
        v = proj_ref[:, 2 * d:].astype(F32)
        halo = halo_ref[...].astype(F32)[16 - CONV_HALO:]
        uh = halo[:, d:2 * d] * halo[:, 2 * d:]
        uext_ref[0:CONV_HALO, :] = jnp.where(tile > 0, uh, jnp.zeros_like(uh))
        uext_ref[CONV_HALO:, :] = cg * v
        taps = [sm_ref[8 + j:9 + j, :] for j in range(3)]
        full = uext_ref[...]
        u0 = full[CONV_HALO:]
        u1 = pltpu.roll(full, 1, 0)[CONV_HALO:]
        u2 = pltpu.roll(full, 2, 0)[CONV_HALO:]
        conv = u0 * taps[2] + u1 * taps[1] + u2 * taps[0]
        bc_ref[...] = (b * conv).astype(BF16)
        dconv = dbc * b
        sg_ref[4:5, :] += _colsum(dconv * u0)
        sg_ref[3:4, :] += _colsum(dconv * u1)
        sg_ref[2:3, :] += _colsum(dconv * u2)
        dcext_ref[0:tm, :] = dconv
        dcext_ref[tm:, :] = carry_ref[...]
        carry_ref[...] = dconv[0:CONV_HALO]
        dfull = dcext_ref[...]
        n8 = tm + CONV_HALO
        du = (dfull[0:tm] * taps[2] + pltpu.roll(dfull, n8 - 1, 0)[0:tm] * taps[1]
              + pltpu.roll(dfull, n8 - 2, 0)[0:tm] * taps[0])
        dproj_ref[:, 0:d] = (dbc * conv).astype(BF16)
        dproj_ref[:, d:2 * d] = (du * v).astype(BF16)
        dproj_ref[:, 2 * d:] = (du * cg).astype(BF16)
        dn = None
        for k in range(N_CHIPS):
            part = _dot_nt(dproj_ref[:, k * pc:(k + 1) * pc], win_ref[k])
            dn = part if dn is None else dn + part
        hh, r0 = _rms(h_ref[...])
        sg_ref[0:1, :] += _colsum(dn * hh)
        o_ref[...] = dy + _rms_bwd(dn, hh, r0, sm_ref[4:5, :])

    rev = lambda i: (steps - 1 - i, 0)
    before = lambda i: (jnp.maximum((steps - 1 - i) * halo_blocks - 1, 0), 0)
    return pl.pallas_call(
        body, name="bwd_conv", grid=(steps,),
        out_shape=[jax.ShapeDtypeStruct((t, d), F32), jax.ShapeDtypeStruct((t, 3 * d), BF16),
                   jax.ShapeDtypeStruct((t, d), BF16), jax.ShapeDtypeStruct((t, d), BF16),
                   jax.ShapeDtypeStruct((8, d), F32)],
        in_specs=[pl.BlockSpec((tm, d), rev), pl.BlockSpec((tm, d), rev), pl.BlockSpec((tm, d), rev),
                  pl.BlockSpec((tm, 3 * d), rev), pl.BlockSpec((16, 3 * d), before),
                  _resident(small.shape, lambda i: (0, 0)), _resident(win.shape, lambda i: (0, 0, 0)),
                  _resident(wout.shape, lambda i: (0, 0))],
        out_specs=[pl.BlockSpec((tm, d), rev), pl.BlockSpec((tm, 3 * d), rev), pl.BlockSpec((tm, d), rev),
                   pl.BlockSpec((tm, d), rev), pl.BlockSpec((8, d), lambda i: (0, 0))],
        scratch_shapes=[pltpu.VMEM((CONV_HALO + tm, d), F32), pltpu.VMEM((tm + CONV_HALO, d), F32),
                        pltpu.VMEM((CONV_HALO, d), F32)],
        compiler_params=_params("arbitrary"),
    )(dh, h, y, proj, proj, small, win, wout)


def _bwd_pool(dh, x, small, scale, poolw):
    t, d = x.shape
    tm = _token_tile(t)
    steps = t // tm
    ng = len(POOL_WINDOWS)
    gw = d // ng
    halo_blocks = tm // POOL_HALO

    def body(dh_ref, x_ref, halo_ref, sm_ref, sc_ref, w_ref, o_ref, dw_ref, sg_ref,
             ext_ref, mix_ref, mm_ref, pb_ref, qext_ref, dhn_ref, carry_ref):
        i = pl.program_id(0)
        tile = steps - 1 - i

        @pl.when(i == 0)
        def _():
            sg_ref[...] = jnp.zeros_like(sg_ref)
            dw_ref[...] = jnp.zeros_like(dw_ref)
            carry_ref[...] = jnp.zeros_like(carry_ref)

        g0 = sm_ref[0:1, :]
        xv = x_ref[...]
        xh, r0 = _rms(xv)
        hx, _ = _rms(halo_ref[...])
        ext_ref[0:POOL_HALO, :] = jnp.where(tile > 0, hx * g0, jnp.zeros_like(hx))
        ext_ref[POOL_HALO:, :] = xh * g0
        for g in range(ng):
            pooled, _ = _pool_windows(ext_ref, g, gw, tm, tile * tm)
            cols = slice(g * gw, (g + 1) * gw)
            pb = pooled.astype(BF16)
            pb_ref[:, cols] = pb
            mm = _dot(pb, w_ref[g])
            mm_ref[:, cols] = mm
            mix_ref[:, cols] = mm * sc_ref[:, cols]
        dy = dh_ref[...]
        mh, r1 = _rms(mix_ref[...])
        sg_ref[1:2, :] += _colsum(dy * mh)
        dmix = _rms_bwd(dy, mh, r1, sm_ref[1:2, :])
        sg_ref[2:3, :] += _colsum(dmix * mm_ref[...])
        mix_ref[...] = dmix * sc_ref[...]
        n16 = tm + POOL_HALO
        for g in range(ng):
            w = POOL_WINDOWS[g]
            cols = slice(g * gw, (g + 1) * gw)
            dmm = mix_ref[:, cols].astype(BF16)
            dpooled = _dot_nt(dmm, w_ref[g])
            dw_ref[g] += _dot_tn(pb_ref[:, cols], dmm)
            trow = tile * tm + lax.broadcasted_iota(jnp.int32, (tm, 1), 0)
            q = dpooled / jnp.minimum(trow + 1, w).astype(F32)
            qext_ref[0:tm, cols] = q
            qext_ref[tm:, cols] = carry_ref[:, cols]
            carry_ref[:, cols] = q[0:POOL_HALO]
            p, k = qext_ref[:, cols], 1
            while k < w:
                p = p + pltpu.roll(p, n16 - k, 0)
                k *= 2
            dhn_ref[:, cols] = p[0:tm] - dpooled
        dhn = dhn_ref[...]
        sg_ref[0:1, :] += _colsum(dhn * xh)
        o_ref[...] = dy + _rms_bwd(dhn, xh, r0, g0)

    rev = lambda i: (steps - 1 - i, 0)
    before = lambda i: (jnp.maximum((steps - 1 - i) * halo_blocks - 1, 0), 0)
    return pl.pallas_call(
        body, name="bwd_pool", grid=(steps,),
        out_shape=[jax.ShapeDtypeStruct((t, d), F32), jax.ShapeDtypeStruct((ng, gw, gw), F32),
                   jax.ShapeDtypeStruct((8, d), F32)],
        in_specs=[pl.BlockSpec((tm, d), rev), pl.BlockSpec((tm, d), rev), pl.BlockSpec((POOL_HALO, d), before),
                  _resident(small.shape, lambda i: (0, 0)), _resident(scale.shape, lambda i: (0, 0)),
                  _resident(poolw.shape, lambda i: (0, 0, 0))],
        out_specs=[pl.BlockSpec((tm, d), rev), pl.BlockSpec((ng, gw, gw), lambda i: (0, 0, 0)),
                   pl.BlockSpec((8, d), lambda i: (0, 0))],
        scratch_shapes=[pltpu.VMEM((POOL_HALO + tm, d), F32), pltpu.VMEM((tm, d), F32), pltpu.VMEM((tm, d), F32),
                        pltpu.VMEM((tm, d), BF16), pltpu.VMEM((tm + POOL_HALO, d), F32), pltpu.VMEM((tm, d), F32),
                        pltpu.VMEM((POOL_HALO, d), F32)],
        compiler_params=_params("arbitrary"),
    )(dh, x, x, small, scale, poolw)


def _weight_grad(a, b, bm, bn, name):
    t, m = a.shape
    _, n = b.shape

    def body(a_ref, b_ref, o_ref):
        o_ref[...] = _dot_tn(a_ref[...], b_ref[...]).astype(o_ref.dtype)

    return pl.pallas_call(
        body, name=name, grid=(n // bn, m // bm),
        out_shape=jax.ShapeDtypeStruct((n // bn, m // bm, bm, bn), BF16),
        in_specs=[pl.BlockSpec((t, bm), lambda j, i: (0, i)), pl.BlockSpec((t, bn), lambda j, i: (0, j))],
        out_specs=pl.BlockSpec((None, None, bm, bn), lambda j, i: (j, i, 0, 0)),
        compiler_params=_params("parallel", "parallel"),
    )(a, b)


def _adamw(w, g, m, v, name):
    r, c = w.shape
    rb = _row_block(r, c * (7 * 4 * 2 + 4 * 4))
    bc1 = 1.0 - ADAM_B1 ** ADAM_STEP
    bc2 = 1.0 - ADAM_B2 ** ADAM_STEP

    def body(w_ref, g_ref, m_ref, v_ref, d_ref, nm_ref, nv_ref):
        gv = g_ref[...]
        nm = ADAM_B1 * m_ref[...] + (1.0 - ADAM_B1) * gv
        nv = ADAM_B2 * v_ref[...] + (1.0 - ADAM_B2) * (gv * gv)
        nm_ref[...] = nm
        nv_ref[...] = nv
        d_ref[...] = -ADAM_LR * ((nm / bc1) / (jnp.sqrt(nv / bc2) + ADAM_EPS) + ADAM_WD * w_ref[...])

    spec = pl.BlockSpec((rb, c), lambda i: (i, 0))
    return pl.pallas_call(
        body, name=name, grid=(r // rb,), out_shape=[jax.ShapeDtypeStruct((r, c), F32)] * 3,
        in_specs=[spec] * 4, out_specs=[spec] * 3, compiler_params=_params("parallel"),
    )(w, g, m, v)


def kernel(x, norm_gains, pool_w, pool_scale, conv_in_w, conv_w, conv_out_w, ffn_gate_up_w, ffn_down_w, loss_target, m_norm_gains, m_pool_w, m_pool_scale, m_conv_in_w, m_conv_w, m_conv_out_w, m_ffn_gate_up_w, m_ffn_down_w, v_norm_gains, v_pool_w, v_pool_scale, v_conv_in_w, v_conv_w, v_conv_out_w, v_ffn_gate_up_w, v_ffn_down_w):
    _, t, d = x.shape
    dq = d // N_CHIPS
    ng = len(POOL_WINDOWS)
    gw = d // ng
    fq = ffn_down_w.shape[1]
    f = N_CHIPS * fq
    fc = f // 2
    core = lax.axis_index("c")
    chip = 2 * lax.axis_index("x") + lax.axis_index("y")
    core_arr = jnp.reshape(core, (1,)).astype(jnp.int32)
    chip_arr = jnp.reshape(chip, (1,)).astype(jnp.int32)
    x2, target = x[0], loss_target[0]

    small_loc = jnp.concatenate(
        [norm_gains.reshape(8, dq), conv_w[0], jnp.zeros((5, dq), F32)], axis=0).reshape(2, 8, dq)
    wgu_f, wd_f, win_f, wout_f, pool_f, small_f = _all_gather(
        [ffn_gate_up_w.astype(BF16), ffn_down_w.astype(BF16), conv_in_w[0].astype(BF16).reshape(2, d // 2, -1),
         conv_out_w[0].astype(BF16).reshape(2, dq // 2, d), pool_w[0].astype(BF16).reshape(2, ng // 2 * (gw // N_CHIPS), gw),
         small_loc],
        [True, True, False, False, False, False], "ag_weights")
    wd_f = wd_f.reshape(2, f, d)
    win_f = win_f.reshape(N_CHIPS, d, -1)
    wout_f = wout_f.reshape(d, d)
    poolw = pool_f.reshape(N_CHIPS, ng, gw // N_CHIPS, gw).transpose(1, 0, 2, 3).reshape(ng, gw, gw)
    small = small_f.transpose(1, 2, 0, 3).reshape(16, d)

    h1 = _fwd_pool(x2, small, pool_scale, poolw)
    h2, gu0, ff0, n0 = _fwd_ffn(h1, small, wgu_f, wd_f, 0)
    h3, proj, y, nc = _fwd_conv(h2, small, win_f, wout_f)
    h4, gu1, ff1, n1 = _fwd_ffn(h3, small, wgu_f, wd_f, 1)
    dh4, loss_blk = _loss_grad(h4, target)
    loss = lax.psum(loss_blk[0, 0], ("x", "y", "c"))

    dh3, dgu1, dff1, act1, sg_f1 = _bwd_ffn(dh4, h3, ff1, gu1, small, wgu_f, wd_f, 1)
    g_gu1 = _weight_grad(n1, dgu1, d // 2, fc, "dw_gate_up1")
    g_d1 = _weight_grad(act1, dff1, DOWN_GRAD_ROWS, d, "dw_down1")
    dh2, dproj, dyv, bcv, sg_c = _bwd_conv(dh3, h2, y, proj, small, win_f, wout_f)
    g_in = _weight_grad(nc, dproj, d // 2, 3 * d // N_CHIPS, "dw_conv_in")
    g_out = _weight_grad(bcv, dyv, d // 2, d, "dw_conv_out")
    dh1, dgu0, dff0, act0, sg_f0 = _bwd_ffn(dh2, h1, ff0, gu0, small, wgu_f, wd_f, 0)
    g_gu0 = _weight_grad(n0, dgu0, d // 2, fc, "dw_gate_up0")
    g_d0 = _weight_grad(act0, dff0, DOWN_GRAD_ROWS, d, "dw_down0")
    grad_x, dpool, sg_p = _bwd_pool(dh1, x2, small, pool_scale, poolw)

    pieces = [
        g_gu0, g_gu1,
        g_d0.reshape(N_CHIPS, 2, fq // 2, d), g_d1.reshape(N_CHIPS, 2, fq // 2, d),
        g_in, g_out.reshape(N_CHIPS, 2, dq // 2, d),
        dpool.astype(BF16).reshape(2, ng // 2, N_CHIPS, gw // N_CHIPS, gw).transpose(2, 0, 1, 3, 4).reshape(
            N_CHIPS, 2, ng // 2 * (gw // N_CHIPS), gw),
    ]
    zero_row = jnp.zeros((1, d), F32)
    small_g = jnp.concatenate(
        [sg_p[0:2], sg_f0[0:2], sg_c[0:2], sg_f1[0:2], sg_c[2:5], sg_p[2:3]] + [zero_row] * 4, axis=0)

    *landed, small_all = _sibling_exchange(pieces, small_g)
    parts = [_add_sibling(g, l, core_arr) for g, l in zip(pieces, landed)]
    received = _chip_exchange(parts)
    halves = [_add_chips(p, r, chip_arr) for p, r in zip(parts, received)]
    full = _sibling_share(halves)
    small_sum = _sum_small(small_all)

    gg_gu = jnp.stack([full[0].reshape(d, -1), full[1].reshape(d, -1)])
    gg_d = jnp.stack([full[2].reshape(fq, d), full[3].reshape(fq, d)])
    gg_in = full[4].reshape(1, d, -1)
    gg_out = full[5].reshape(1, dq, d)
    gg_pool = full[6].reshape(1, ng, gw // N_CHIPS, gw)
    mine = lax.dynamic_slice_in_dim(small_sum, chip * dq, dq, axis=1)
    gg_gains = mine[0:8].reshape(2, 4, dq)
    gg_taps = mine[8:11].reshape(1, 3, dq)
    gg_scale = small_sum[11:12]

    grads = [gg_gains, gg_pool, gg_scale, gg_in, gg_taps, gg_out, gg_gu, gg_d]
    weights = [norm_gains, pool_w, pool_scale, conv_in_w, conv_w, conv_out_w, ffn_gate_up_w, ffn_down_w]
    ms = [m_norm_gains, m_pool_w, m_pool_scale, m_conv_in_w, m_conv_w, m_conv_out_w, m_ffn_gate_up_w, m_ffn_down_w]
    vs = [v_norm_gains, v_pool_w, v_pool_scale, v_conv_in_w, v_conv_w, v_conv_out_w, v_ffn_gate_up_w, v_ffn_down_w]
    names = ["gains", "pool_w", "pool_scale", "conv_in", "taps", "conv_out", "gate_up", "down"]
    deltas, new_ms, new_vs = [], [], []
    for w, g, m, v, nm in zip(weights, grads, ms, vs, names):
        flat = (-1, w.shape[-1])
        dl, m2, v2 = _adamw(w.reshape(flat), g.reshape(flat), m.reshape(flat), v.reshape(flat), "adamw_" + nm)
        deltas.append(dl.reshape(w.shape))
        new_ms.append(m2.reshape(w.shape))
        new_vs.append(v2.reshape(w.shape))
    return (loss, grad_x[None], *grads, *deltas, *new_ms, *new_vs)
```

```python
import functools

import jax
import jax.numpy as jnp
from jax import lax
from jax.experimental import pallas as pl
from jax.experimental.pallas import tpu as pltpu

RMS_EPS = 1e-6
POOL_WINDOWS = (2, 4, 8, 16)
POOL_HALO = 16
CONV_HALO = 8
N_CHIPS = 4
N_DEV = 8
ADAM_LR = 0.001
ADAM_B1 = 0.9
ADAM_B2 = 0.999
ADAM_EPS = 1e-08
ADAM_WD = 0.01
ADAM_STEP = 10
VMEM_LIMIT = 56 * 2**20
DOWN_GRAD_ROWS = 256
MESH = pl.DeviceIdType.MESH
ANY = pl.BlockSpec(memory_space=pl.ANY)
BF16 = jnp.bfloat16
F32 = jnp.float32


def _params(*sem):
    return pltpu.CompilerParams(dimension_semantics=sem, vmem_limit_bytes=VMEM_LIMIT)


def _token_tile(t):
    return min(256, t)


def _rms(x):
    r = lax.rsqrt(jnp.mean(x * x, axis=-1, keepdims=True) + RMS_EPS)
    return x * r, r


def _rms_bwd(dy, xh, r, g):
    a = dy * g
    return r * (a - xh * jnp.mean(a * xh, axis=-1, keepdims=True))


def _dot(a, b):
    return jnp.dot(a, b, preferred_element_type=F32)


def _dot_nt(a, b):
    return lax.dot_general(a, b, (((1,), (1,)), ((), ())), preferred_element_type=F32)


def _dot_tn(a, b):
    return lax.dot_general(a, b, (((0,), (0,)), ((), ())), preferred_element_type=F32)


def _colsum(a):
    return jnp.sum(a, axis=0, keepdims=True)


def _resident(block, index_map):
    return pl.BlockSpec(block, index_map, pipeline_mode=pl.Buffered(1))


def _place():
    x, y, c = lax.axis_index("x"), lax.axis_index("y"), lax.axis_index("c")
    return x, y, c, 2 * x + y


def _dev(chip, core):
    return (chip // 2, chip % 2, core)


def _piece(ref, chip, half, half_major):
    return ref.at[half, chip] if half_major else ref.at[chip, half]


def _all_gather(shards, half_major, name):
    n = len(shards)

    def body(*refs):
        loc, out = refs[:n], refs[n:2 * n]
        osend_sems, orecv_sems, send_sems, recv_sems, fsend_sems, frecv_sems = refs[2 * n:]
        x, y, c, k = _place()
        owns, sends = [], []
        for a in range(n):
            hm = half_major[a]
            own = pltpu.make_async_remote_copy(
                src_ref=loc[a], dst_ref=out[a].at[:, k] if hm else out[a].at[k],
                send_sem=osend_sems.at[a], recv_sem=orecv_sems.at[a], device_id=(x, y, 1 - c), device_id_type=MESH)
            own.start()
            owns.append(own)
            for m in range(1, N_CHIPS):
                cp = pltpu.make_async_remote_copy(
                    src_ref=loc[a].at[c], dst_ref=_piece(out[a], k, c, hm),
                    send_sem=send_sems.at[a, m - 1], recv_sem=recv_sems.at[a, m - 1],
                    device_id=_dev(k ^ m, c), device_id_type=MESH)
                cp.start()
                sends.append(cp)
        fwds = []
        for a in range(n):
            hm = half_major[a]
            for m in range(1, N_CHIPS):
                got = _piece(out[a], k ^ m, c, hm)
                pltpu.make_async_remote_copy(
                    src_ref=got, dst_ref=got, send_sem=send_sems.at[a, m - 1], recv_sem=recv_sems.at[a, m - 1],
                    device_id=_dev(k ^ m, c), device_id_type=MESH).wait_recv()
                fw = pltpu.make_async_remote_copy(
                    src_ref=got, dst_ref=got, send_sem=fsend_sems.at[a, m - 1], recv_sem=frecv_sems.at[a, m - 1],
                    device_id=(x, y, 1 - c), device_id_type=MESH)
                fw.start()
                fwds.append(fw)
        for a in range(n):
            hm = half_major[a]
            for m in range(1, N_CHIPS):
                got = _piece(out[a], k ^ m, 1 - c, hm)
                pltpu.make_async_remote_copy(
                    src_ref=got, dst_ref=got, send_sem=fsend_sems.at[a, m - 1], recv_sem=frecv_sems.at[a, m - 1],
                    device_id=(x, y, 1 - c), device_id_type=MESH).wait_recv()
        for own in owns:
            own.wait_recv()
        for cp in sends + fwds + owns:
            cp.wait_send()

    out_shape = []
    for s, hm in zip(shards, half_major):
        _, r, cc = s.shape
        out_shape.append(jax.ShapeDtypeStruct((2, N_CHIPS, r, cc) if hm else (N_CHIPS, 2, r, cc), s.dtype))
    return pl.pallas_call(
        body, name=name, out_shape=out_shape, in_specs=[ANY] * n, out_specs=[ANY] * n,
        scratch_shapes=[pltpu.SemaphoreType.DMA((n,)), pltpu.SemaphoreType.DMA((n,)),
                        pltpu.SemaphoreType.DMA((n, 3)), pltpu.SemaphoreType.DMA((n, 3)),
                        pltpu.SemaphoreType.DMA((n, 3)), pltpu.SemaphoreType.DMA((n, 3))],
        compiler_params=pltpu.CompilerParams(has_side_effects=True),
    )(*shards)


def _sibling_exchange(grads, small):
    n = len(grads)

    def body(*refs):
        g, sm = refs[:n], refs[n]
        land, smg = refs[n + 1:2 * n + 1], refs[2 * n + 1]
        send_sems, recv_sems, own_sem, ssend_sems, srecv_sems = refs[2 * n + 2:]
        x, y, c, k = _place()
        me = 2 * k + c
        cps = []
        for a in range(n):
            cp = pltpu.make_async_remote_copy(
                src_ref=g[a].at[:, pl.ds(1 - c, 1)], dst_ref=land[a], send_sem=send_sems.at[a], recv_sem=recv_sems.at[a],
                device_id=(x, y, 1 - c), device_id_type=MESH)
            cp.start()
            cps.append(cp)
        own = pltpu.make_async_copy(sm, smg.at[me], own_sem)
        own.start()
        scs = []
        for m in range(1, N_DEV):
            peer = me ^ m
            sc = pltpu.make_async_remote_copy(
                src_ref=sm, dst_ref=smg.at[me], send_sem=ssend_sems.at[m - 1], recv_sem=srecv_sems.at[m - 1],
                device_id=(peer // 4, (peer // 2) % 2, peer % 2), device_id_type=MESH)
            sc.start()
            scs.append(sc)
        for cp in cps:
            cp.wait_recv()
        for m in range(1, N_DEV):
            peer = me ^ m
            pltpu.make_async_remote_copy(
                src_ref=sm, dst_ref=smg.at[peer], send_sem=ssend_sems.at[m - 1], recv_sem=srecv_sems.at[m - 1],
                device_id=(peer // 4, (peer // 2) % 2, peer % 2), device_id_type=MESH).wait_recv()
        for cp in cps + scs:
            cp.wait_send()
        own.wait()

    out_shape = [jax.ShapeDtypeStruct((N_CHIPS, 1) + a.shape[2:], a.dtype) for a in grads]
    out_shape.append(jax.ShapeDtypeStruct((N_DEV,) + small.shape, small.dtype))
    return pl.pallas_call(
        body, name="rs_sibling_exchange", out_shape=out_shape, in_specs=[ANY] * (n + 1), out_specs=[ANY] * (n + 1),
        scratch_shapes=[pltpu.SemaphoreType.DMA((n,)), pltpu.SemaphoreType.DMA((n,)), pltpu.SemaphoreType.DMA,
                        pltpu.SemaphoreType.DMA((N_DEV - 1,)), pltpu.SemaphoreType.DMA((N_DEV - 1,))],
        compiler_params=pltpu.CompilerParams(has_side_effects=True),
    )(*grads, small)


def _chip_exchange(parts):
    n = len(parts)

    def body(*refs):
        p, land = refs[:n], refs[n:2 * n]
        send_sems, recv_sems = refs[2 * n:]
        x, y, c, k = _place()
        cps = []
        for a in range(n):
            for m in range(1, N_CHIPS):
                cp = pltpu.make_async_remote_copy(
                    src_ref=p[a].at[k ^ m], dst_ref=land[a].at[m - 1], send_sem=send_sems.at[a, m - 1],
                    recv_sem=recv_sems.at[a, m - 1], device_id=_dev(k ^ m, c), device_id_type=MESH)
                cp.start()
                cps.append(cp)
        for cp in cps:
            cp.wait_recv()
        for cp in cps:
            cp.wait_send()

    out_shape = [jax.ShapeDtypeStruct((N_CHIPS - 1,) + a.shape[1:], a.dtype) for a in parts]
    return pl.pallas_call(
        body, name="rs_chip_exchange", out_shape=out_shape, in_specs=[ANY] * n, out_specs=[ANY] * n,
        scratch_shapes=[pltpu.SemaphoreType.DMA((n, 3)), pltpu.SemaphoreType.DMA((n, 3))],
        compiler_params=pltpu.CompilerParams(has_side_effects=True),
    )(*parts)


def _sibling_share(halves):
    n = len(halves)

    def body(*refs):
        out = refs[n:2 * n]
        send_sems, recv_sems = refs[2 * n:]
        x, y, c, k = _place()
        cps = []
        for a in range(n):
            cp = pltpu.make_async_remote_copy(
                src_ref=out[a].at[c], dst_ref=out[a].at[c], send_sem=send_sems.at[a], recv_sem=recv_sems.at[a],
                device_id=(x, y, 1 - c), device_id_type=MESH)
            cp.start()
            cps.append(cp)
        for a in range(n):
            pltpu.make_async_remote_copy(
                src_ref=out[a].at[1 - c], dst_ref=out[a].at[1 - c], send_sem=send_sems.at[a], recv_sem=recv_sems.at[a],
                device_id=(x, y, 1 - c), device_id_type=MESH).wait_recv()
        for cp in cps:
            cp.wait_send()

    out_shape = [jax.ShapeDtypeStruct(a.shape, a.dtype) for a in halves]
    return pl.pallas_call(
        body, name="rs_sibling_share", out_shape=out_shape, in_specs=[ANY] * n, out_specs=[ANY] * n,
        input_output_aliases={a: a for a in range(n)},
        scratch_shapes=[pltpu.SemaphoreType.DMA((n,)), pltpu.SemaphoreType.DMA((n,))],
        compiler_params=pltpu.CompilerParams(has_side_effects=True),
    )(*halves)


STREAM_BUDGET = 24 * 2**20


def _row_block(r, row_bytes):
    best = None
    for rb in range(16, r + 1, 16):
        if r % rb == 0 and rb * row_bytes <= STREAM_BUDGET:
            best = rb
    return best if best is not None else r


def _add_sibling(g, land, core):
    _, _, r, c = g.shape
    rb = _row_block(r, c * (3 * 2 * 2 + 2 * 4))

    def body(core_ref, g_ref, l_ref, o_ref):
        o_ref[...] = (g_ref[...].astype(F32) + l_ref[...].astype(F32)).astype(o_ref.dtype)

    return pl.pallas_call(
        body, name="rs_add_sibling", out_shape=jax.ShapeDtypeStruct((N_CHIPS, r, c), g.dtype),
        grid_spec=pltpu.PrefetchScalarGridSpec(
            num_scalar_prefetch=1, grid=(N_CHIPS, r // rb),
            in_specs=[pl.BlockSpec((None, None, rb, c), lambda j, i, core_ref: (j, core_ref[0], i, 0)),
                      pl.BlockSpec((None, None, rb, c), lambda j, i, core_ref: (j, 0, i, 0))],
            out_specs=pl.BlockSpec((None, rb, c), lambda j, i, core_ref: (j, i, 0))),
        compiler_params=_params("parallel", "parallel"),
    )(core, g, land)


def _add_chips(part, land, where):
    _, r, c = part.shape
    rb = _row_block(r, c * (4 * 2 * 2 + 4 * 2 + 2 * 4))

    def body(where_ref, p_ref, l_ref, o_ref):
        acc = p_ref[...].astype(F32)
        for m in range(N_CHIPS - 1):
            acc = acc + l_ref[m].astype(F32)
        o_ref[...] = acc

    return pl.pallas_call(
        body, name="rs_add_chips", out_shape=jax.ShapeDtypeStruct((2, r, c), F32),
        grid_spec=pltpu.PrefetchScalarGridSpec(
            num_scalar_prefetch=1, grid=(r // rb,),
            in_specs=[pl.BlockSpec((None, rb, c), lambda i, where_ref: (where_ref[0], i, 0)),
                      pl.BlockSpec((N_CHIPS - 1, rb, c), lambda i, where_ref: (0, i, 0))],
            out_specs=pl.BlockSpec((None, rb, c), lambda i, where_ref: (where_ref[1], i, 0))),
        compiler_params=_params("parallel"),
    )(where, part, land)


def _sum_small(smg):
    def body(s_ref, o_ref):
        acc = s_ref[0]
        for j in range(1, N_DEV):
            acc = acc + s_ref[j]
        o_ref[...] = acc

    return pl.pallas_call(body, name="rs_sum_small", out_shape=jax.ShapeDtypeStruct(smg.shape[1:], F32))(smg)


def _pool_windows(ext_ref, g, gw, tm, first_row):
    w = POOL_WINDOWS[g]
    slab = ext_ref[:, g * gw:(g + 1) * gw]
    p, k = slab, 1
    while k < w:
        p = p + pltpu.roll(p, k, 0)
        k *= 2
    t = first_row + lax.broadcasted_iota(jnp.int32, (tm, 1), 0)
    cnt = jnp.minimum(t + 1, w).astype(F32)
    return p[POOL_HALO:] / cnt - slab[POOL_HALO:], cnt


def _fwd_pool(x, small, scale, poolw):
    t, d = x.shape
    tm = _token_tile(t)
    gw = d // len(POOL_WINDOWS)

    def body(x_ref, sm_ref, sc_ref, w_ref, h_ref, ext_ref, mix_ref):
        i = pl.program_id(0)

        @pl.when(i == 0)
        def _():
            ext_ref[0:POOL_HALO, :] = jnp.zeros((POOL_HALO, d), F32)

        @pl.when(i > 0)
        def _():
            ext_ref[0:POOL_HALO, :] = ext_ref[tm:tm + POOL_HALO, :]

        xv = x_ref[...]
        xh, _ = _rms(xv)
        ext_ref[POOL_HALO:, :] = xh * sm_ref[0:1, :]
        for g in range(len(POOL_WINDOWS)):
            pooled, _ = _pool_windows(ext_ref, g, gw, tm, i * tm)
            cols = slice(g * gw, (g + 1) * gw)
            mix_ref[:, cols] = _dot(pooled.astype(BF16), w_ref[g]) * sc_ref[:, cols]
        mh, _ = _rms(mix_ref[...])
        h_ref[...] = xv + mh * sm_ref[1:2, :]

    return pl.pallas_call(
        body, name="fwd_pool", grid=(t // tm,), out_shape=jax.ShapeDtypeStruct((t, d), F32),
        in_specs=[pl.BlockSpec((tm, d), lambda i: (i, 0)), _resident(small.shape, lambda i: (0, 0)),
                  _resident(scale.shape, lambda i: (0, 0)), _resident(poolw.shape, lambda i: (0, 0, 0))],
        out_specs=pl.BlockSpec((tm, d), lambda i: (i, 0)),
        scratch_shapes=[pltpu.VMEM((POOL_HALO + tm, d), F32), pltpu.VMEM((tm, d), F32)],
        compiler_params=_params("arbitrary"),
    )(x, small, scale, poolw)


def _fwd_ffn(h, small, wgu, wd, layer):
    t, d = h.shape
    tm = _token_tile(t)
    fc = wgu.shape[-1]
    f = 2 * fc
    g_in, g_out = 4 * layer + 2, 4 * layer + 3

    def body(h_ref, sm_ref, wgu_ref, wd_ref, o_ref, gu_ref, ff_ref, n_ref):
        hv = h_ref[...]
        hh, _ = _rms(hv)
        n = (hh * sm_ref[g_in:g_in + 1, :]).astype(BF16)
        n_ref[...] = n
        ff = None
        for j in range(2):
            gate = _dot(n, wgu_ref[j])
            up = _dot(n, wgu_ref[2 + j])
            gu_ref[:, j * fc:(j + 1) * fc] = gate.astype(BF16)
            gu_ref[:, f + j * fc:f + (j + 1) * fc] = up.astype(BF16)
            act = (gate * jax.nn.sigmoid(gate) * up).astype(BF16)
            part = _dot(act, wd_ref[j * fc:(j + 1) * fc, :])
            ff = part if ff is None else ff + part
        ff_ref[...] = ff
        fh, _ = _rms(ff)
        o_ref[...] = hv + fh * sm_ref[g_out:g_out + 1, :]

    row = lambda i: (i, 0)
    return pl.pallas_call(
        body, name=f"fwd_ffn{layer}", grid=(t // tm,),
        out_shape=[jax.ShapeDtypeStruct((t, d), F32), jax.ShapeDtypeStruct((t, 2 * f), BF16),
                   jax.ShapeDtypeStruct((t, d), F32), jax.ShapeDtypeStruct((t, d), BF16)],
        in_specs=[pl.BlockSpec((tm, d), row), _resident(small.shape, lambda i: (0, 0)),
                  _resident((None, N_CHIPS, d, fc), lambda i: (layer, 0, 0, 0)),
                  _resident((None, f, d), lambda i: (layer, 0, 0))],
        out_specs=[pl.BlockSpec((tm, d), row), pl.BlockSpec((tm, 2 * f), row), pl.BlockSpec((tm, d), row),
                   pl.BlockSpec((tm, d), row)],
        compiler_params=_params("parallel"),
    )(h, small, wgu, wd)


def _conv_taps(uext_ref, taps, tm):
    full = uext_ref[...]
    return (full[CONV_HALO:] * taps[2] + pltpu.roll(full, 1, 0)[CONV_HALO:] * taps[1]
            + pltpu.roll(full, 2, 0)[CONV_HALO:] * taps[0])


def _fwd_conv(h, small, win, wout):
    t, d = h.shape
    tm = _token_tile(t)
    pc = win.shape[-1]

    def body(h_ref, sm_ref, win_ref, wout_ref, o_ref, proj_ref, y_ref, n_ref, pj_ref, uext_ref):
        i = pl.program_id(0)

        @pl.when(i == 0)
        def _():
            uext_ref[0:CONV_HALO, :] = jnp.zeros((CONV_HALO, d), F32)

        @pl.when(i > 0)
        def _():
            uext_ref[0:CONV_HALO, :] = uext_ref[tm:tm + CONV_HALO, :]

        hv = h_ref[...]
        hh, _ = _rms(hv)
        n = (hh * sm_ref[4:5, :]).astype(BF16)
        n_ref[...] = n
        for k in range(N_CHIPS):
            pj_ref[:, k * pc:(k + 1) * pc] = _dot(n, win_ref[k])
        proj_ref[...] = pj_ref[...].astype(BF16)
        uext_ref[CONV_HALO:, :] = pj_ref[:, d:2 * d] * pj_ref[:, 2 * d:]
        taps = [sm_ref[8 + j:9 + j, :] for j in range(3)]
        conv = _conv_taps(uext_ref, taps, tm)
        y = _dot((pj_ref[:, 0:d] * conv).astype(BF16), wout_ref[...])
        y_ref[...] = y
        yh, _ = _rms(y)
        o_ref[...] = hv + yh * sm_ref[5:6, :]

    row = lambda i: (i, 0)
    return pl.pallas_call(
        body, name="fwd_conv", grid=(t // tm,),
        out_shape=[jax.ShapeDtypeStruct((t, d), F32), jax.ShapeDtypeStruct((t, 3 * d), BF16),
                   jax.ShapeDtypeStruct((t, d), F32), jax.ShapeDtypeStruct((t, d), BF16)],
        in_specs=[pl.BlockSpec((tm, d), row), _resident(small.shape, lambda i: (0, 0)),
                  _resident(win.shape, lambda i: (0, 0, 0)), _resident(wout.shape, lambda i: (0, 0))],
        out_specs=[pl.BlockSpec((tm, d), row), pl.BlockSpec((tm, 3 * d), row), pl.BlockSpec((tm, d), row),
                   pl.BlockSpec((tm, d), row)],
        scratch_shapes=[pltpu.VMEM((tm, 3 * d), F32), pltpu.VMEM((CONV_HALO + tm, d), F32)],
        compiler_params=_params("arbitrary"),
    )(h, small, win, wout)


def _loss_grad(h, target):
    t, d = h.shape
    tm = _token_tile(t)
    steps = t // tm

    def body(h_ref, t_ref, g_ref, l_ref, acc_ref):
        i = pl.program_id(0)
        e = h_ref[...] - t_ref[...]
        g_ref[...] = e * (1.0 / d)

        @pl.when(i == 0)
        def _():
            acc_ref[...] = jnp.zeros_like(acc_ref)

        acc_ref[...] += _colsum(e * e)

        @pl.when(i == steps - 1)
        def _():
            l_ref[...] = jnp.full(l_ref.shape, 0.5 / d, F32) * jnp.sum(acc_ref[...])

    row = lambda i: (i, 0)
    return pl.pallas_call(
        body, name="loss_grad", grid=(steps,),
        out_shape=[jax.ShapeDtypeStruct((t, d), F32), jax.ShapeDtypeStruct((8, 128), F32)],
        in_specs=[pl.BlockSpec((tm, d), row), pl.BlockSpec((tm, d), row)],
        out_specs=[pl.BlockSpec((tm, d), row), pl.BlockSpec((8, 128), lambda i: (0, 0))],
        scratch_shapes=[pltpu.VMEM((1, d), F32)],
        compiler_params=_params("arbitrary"),
    )(h, target)


def _bwd_ffn(dh, h, ff, gu, small, wgu, wd, layer):
    t, d = h.shape
    tm = _token_tile(t)
    fc = wgu.shape[-1]
    f = 2 * fc
    g_in, g_out = 4 * layer + 2, 4 * layer + 3

    def body(dh_ref, h_ref, ff_ref, gu_ref, sm_ref, wgu_ref, wd_ref, o_ref, dgu_ref, dff_ref, act_ref, sg_ref):
        i = pl.program_id(0)

        @pl.when(i == 0)
        def _():
            sg_ref[...] = jnp.zeros_like(sg_ref)

        dy = dh_ref[...]
        fh, r3 = _rms(ff_ref[...])
        sg_ref[1:2, :] += _colsum(dy * fh)
        dff = _rms_bwd(dy, fh, r3, sm_ref[g_out:g_out + 1, :]).astype(BF16)
        dff_ref[...] = dff
        for j in range(2):
            dact = _dot_nt(dff, wd_ref[j * fc:(j + 1) * fc, :])
            gate = gu_ref[:, j * fc:(j + 1) * fc].astype(F32)
            up = gu_ref[:, f + j * fc:f + (j + 1) * fc].astype(F32)
            sig = jax.nn.sigmoid(gate)
            silu = gate * sig
            act_ref[:, j * fc:(j + 1) * fc] = (silu * up).astype(BF16)
            dgu_ref[:, j * fc:(j + 1) * fc] = (dact * up * (sig * (1.0 + gate * (1.0 - sig)))).astype(BF16)
            dgu_ref[:, f + j * fc:f + (j + 1) * fc] = (dact * silu).astype(BF16)
        dn = None
        for k in range(N_CHIPS):
            part = _dot_nt(dgu_ref[:, k * fc:(k + 1) * fc], wgu_ref[k])
            dn = part if dn is None else dn + part
        hh, r2 = _rms(h_ref[...])
        sg_ref[0:1, :] += _colsum(dn * hh)
        o_ref[...] = dy + _rms_bwd(dn, hh, r2, sm_ref[g_in:g_in + 1, :])

    row = lambda i: (i, 0)
    return pl.pallas_call(
        body, name=f"bwd_ffn{layer}", grid=(t // tm,),
        out_shape=[jax.ShapeDtypeStruct((t, d), F32), jax.ShapeDtypeStruct((t, 2 * f), BF16),
                   jax.ShapeDtypeStruct((t, d), BF16), jax.ShapeDtypeStruct((t, f), BF16),
                   jax.ShapeDtypeStruct((8, d), F32)],
        in_specs=[pl.BlockSpec((tm, d), row), pl.BlockSpec((tm, d), row), pl.BlockSpec((tm, d), row),
                  pl.BlockSpec((tm, 2 * f), row), _resident(small.shape, lambda i: (0, 0)),
                  _resident((None, N_CHIPS, d, fc), lambda i: (layer, 0, 0, 0)),
                  _resident((None, f, d), lambda i: (layer, 0, 0))],
        out_specs=[pl.BlockSpec((tm, d), row), pl.BlockSpec((tm, 2 * f), row), pl.BlockSpec((tm, d), row),
                   pl.BlockSpec((tm, f), row), pl.BlockSpec((8, d), lambda i: (0, 0))],
        compiler_params=_params("arbitrary"),
    )(dh, h, ff, gu, small, wgu, wd)


def _bwd_conv(dh, h, y, proj, small, win, wout):
    t, d = h.shape
    tm = _token_tile(t)
    steps = t // tm
    pc = win.shape[-1]
    halo_blocks = tm // 16

    def body(dh_ref, h_ref, y_ref, proj_ref, halo_ref, sm_ref, win_ref, wout_ref,
             o_ref, dproj_ref, dy_ref, bc_ref, sg_ref, uext_ref, dcext_ref, carry_ref):
        i = pl.program_id(0)
        tile = steps - 1 - i

        @pl.when(i == 0)
        def _():
            sg_ref[...] = jnp.zeros_like(sg_ref)
            carry_ref[...] = jnp.zeros_like(carry_ref)

        dy = dh_ref[...]
        yh, r1 = _rms(y_ref[...])
        sg_ref[1:2, :] += _colsum(dy * yh)
        dyv = _rms_bwd(dy, yh, r1, sm_ref[5:6, :]).astype(BF16)
        dy_ref[...] = dyv
        dbc = _dot_nt(dyv, wout_ref[...])
        b = proj_ref[:, 0:d].astype(F32)
        cg = proj_ref[:, d:2 * d].astype(F32)
        v = proj_ref[:, 2 * d:].astype(F32)
        halo = halo_ref[...].astype(F32)[16 - CONV_HALO:]
        uh = halo[:, d:2 * d] * halo[:, 2 * d:]
        uext_ref[0:CONV_HALO, :] = jnp.where(tile > 0, uh, jnp.zeros_like(uh))
        uext_ref[CONV_HALO:, :] = cg * v
        taps = [sm_ref[8 + j:9 + j, :] for j in range(3)]
        full = uext_ref[...]
        u0 = full[CONV_HALO:]
        u1 = pltpu.roll(full, 1, 0)[CONV_HALO:]
        u2 = pltpu.roll(full, 2, 0)[CONV_HALO:]
        conv = u0 * taps[2] + u1 * taps[1] + u2 * taps[0]
        bc_ref[...] = (b * conv).astype(BF16)
        dconv = dbc * b
        sg_ref[4:5, :] += _colsum(dconv * u0)
        sg_ref[3:4, :] += _colsum(dconv * u1)
        sg_ref[2:3, :] += _colsum(dconv * u2)
        dcext_ref[0:tm, :] = dconv
        dcext_ref[tm:, :] = carry_ref[...]
        carry_ref[...] = dconv[0:CONV_HALO]
        dfull = dcext_ref[...]
        n8 = tm + CONV_HALO
        du = (dfull[0:tm] * taps[2] + pltpu.roll(dfull, n8 - 1, 0)[0:tm] * taps[1]
              + pltpu.roll(dfull, n8 - 2, 0)[0:tm] * taps[0])
        dproj_ref[:, 0:d] = (dbc * conv).astype(BF16)
        dproj_ref[:, d:2 * d] = (du * v).astype(BF16)
        dproj_ref[:, 2 * d:] = (du * cg).astype(BF16)
        dn = None
        for k in range(N_CHIPS):
            part = _dot_nt(dproj_ref[:, k * pc:(k + 1) * pc], win_ref[k])
            dn = part if dn is None else dn + part
        hh, r0 = _rms(h_ref[...])
        sg_ref[0:1, :] += _colsum(dn * hh)
        o_ref[...] = dy + _rms_bwd(dn, hh, r0, sm_ref[4:5, :])

    rev = lambda i: (steps - 1 - i, 0)
    before = lambda i: (jnp.maximum((steps - 1 - i) * halo_blocks - 1, 0), 0)
    return pl.pallas_call(
        body, name="bwd_conv", grid=(steps,),
        out_shape=[jax.ShapeDtypeStruct((t, d), F32), jax.ShapeDtypeStruct((t, 3 * d), BF16),
                   jax.ShapeDtypeStruct((t, d), BF16), jax.ShapeDtypeStruct((t, d), BF16),
                   jax.ShapeDtypeStruct((8, d), F32)],
        in_specs=[pl.BlockSpec((tm, d), rev), pl.BlockSpec((tm, d), rev), pl.BlockSpec((tm, d), rev),
                  pl.BlockSpec((tm, 3 * d), rev), pl.BlockSpec((16, 3 * d), before),
                  _resident(small.shape, lambda i: (0, 0)), _resident(win.shape, lambda i: (0, 0, 0)),
                  _resident(wout.shape, lambda i: (0, 0))],
        out_specs=[pl.BlockSpec((tm, d), rev), pl.BlockSpec((tm, 3 * d), rev), pl.BlockSpec((tm, d), rev),
                   pl.BlockSpec((tm, d), rev), pl.BlockSpec((8, d), lambda i: (0, 0))],
        scratch_shapes=[pltpu.VMEM((CONV_HALO + tm, d), F32), pltpu.VMEM((tm + CONV_HALO, d), F32),
                        pltpu.VMEM((CONV_HALO, d), F32)],
        compiler_params=_params("arbitrary"),
    )(dh, h, y, proj, proj, small, win, wout)


def _bwd_pool(dh, x, small, scale, poolw):
    t, d = x.shape
    tm = _token_tile(t)
    steps = t // tm
    ng = len(POOL_WINDOWS)
    gw = d // ng
    halo_blocks = tm // POOL_HALO

    def body(dh_ref, x_ref, halo_ref, sm_ref, sc_ref, w_ref, o_ref, dw_ref, sg_ref,
             ext_ref, mix_ref, mm_ref, pb_ref, qext_ref, dhn_ref, carry_ref):
        i = pl.program_id(0)
        tile = steps - 1 - i

        @pl.when(i == 0)
        def _():
            sg_ref[...] = jnp.zeros_like(sg_ref)
            dw_ref[...] = jnp.zeros_like(dw_ref)
            carry_ref[...] = jnp.zeros_like(carry_ref)

        g0 = sm_ref[0:1, :]
        xv = x_ref[...]
        xh, r0 = _rms(xv)
        hx, _ = _rms(halo_ref[...])
        ext_ref[0:POOL_HALO, :] = jnp.where(tile > 0, hx * g0, jnp.zeros_like(hx))
        ext_ref[POOL_HALO:, :] = xh * g0
        for g in range(ng):
            pooled, _ = _pool_windows(ext_ref, g, gw, tm, tile * tm)
            cols = slice(g * gw, (g + 1) * gw)
            pb = pooled.astype(BF16)
            pb_ref[:, cols] = pb
            mm = _dot(pb, w_ref[g])
            mm_ref[:, cols] = mm
            mix_ref[:, cols] = mm * sc_ref[:, cols]
        dy = dh_ref[...]
        mh, r1 = _rms(mix_ref[...])
        sg_ref[1:2, :] += _colsum(dy * mh)
        dmix = _rms_bwd(dy, mh, r1, sm_ref[1:2, :])
        sg_ref[2:3, :] += _colsum(dmix * mm_ref[...])
        mix_ref[...] = dmix * sc_ref[...]
        n16 = tm + POOL_HALO
        for g in range(ng):
            w = POOL_WINDOWS[g]
            cols = slice(g * gw, (g + 1) * gw)
            dmm = mix_ref[:, cols].astype(BF16)
            dpooled = _dot_nt(dmm, w_ref[g])
            dw_ref[g] += _dot_tn(pb_ref[:, cols], dmm)
            trow = tile * tm + lax.broadcasted_iota(jnp.int32, (tm, 1), 0)
            q = dpooled / jnp.minimum(trow + 1, w).astype(F32)
            qext_ref[0:tm, cols] = q
            qext_ref[tm:, cols] = carry_ref[:, cols]
            carry_ref[:, cols] = q[0:POOL_HALO]
            p, k = qext_ref[:, cols], 1
            while k < w:
                p = p + pltpu.roll(p, n16 - k, 0)
                k *= 2
            dhn_ref[:, cols] = p[0:tm] - dpooled
        dhn = dhn_ref[...]
        sg_ref[0:1, :] += _colsum(dhn * xh)
        o_ref[...] = dy + _rms_bwd(dhn, xh, r0, g0)

    rev = lambda i: (steps - 1 - i, 0)
    before = lambda i: (jnp.maximum((steps - 1 - i) * halo_blocks - 1, 0), 0)
    return pl.pallas_call(
        body, name="bwd_pool", grid=(steps,),
        out_shape=[jax.ShapeDtypeStruct((t, d), F32), jax.ShapeDtypeStruct((ng, gw, gw), F32),
                   jax.ShapeDtypeStruct((8, d), F32)],
        in_specs=[pl.BlockSpec((tm, d), rev), pl.BlockSpec((tm, d), rev), pl.BlockSpec((POOL_HALO, d), before),
                  _resident(small.shape, lambda i: (0, 0)), _resident(scale.shape, lambda i: (0, 0)),
                  _resident(poolw.shape, lambda i: (0, 0, 0))],
        out_specs=[pl.BlockSpec((tm, d), rev), pl.BlockSpec((ng, gw, gw), lambda i: (0, 0, 0)),
                   pl.BlockSpec((8, d), lambda i: (0, 0))],
        scratch_shapes=[pltpu.VMEM((POOL_HALO + tm, d), F32), pltpu.VMEM((tm, d), F32), pltpu.VMEM((tm, d), F32),
                        pltpu.VMEM((tm, d), BF16), pltpu.VMEM((tm + POOL_HALO, d), F32), pltpu.VMEM((tm, d), F32),
                        pltpu.VMEM((POOL_HALO, d), F32)],
        compiler_params=_params("arbitrary"),
    )(dh, x, x, small, scale, poolw)


def _weight_grad(a, b, bm, bn, name):
    t, m = a.shape
    _, n = b.shape

    def body(a_ref, b_ref, o_ref):
        o_ref[...] = _dot_tn(a_ref[...], b_ref[...]).astype(o_ref.dtype)

    return pl.pallas_call(
        body, name=name, grid=(n // bn, m // bm),
        out_shape=jax.ShapeDtypeStruct((n // bn, m // bm, bm, bn), BF16),
        in_specs=[pl.BlockSpec((t, bm), lambda j, i: (0, i)), pl.BlockSpec((t, bn), lambda j, i: (0, j))],
        out_specs=pl.BlockSpec((None, None, bm, bn), lambda j, i: (j, i, 0, 0)),
        compiler_params=_params("parallel", "parallel"),
    )(a, b)


def _adamw(w, g, m, v, name):
    r, c = w.shape
    rb = _row_block(r, c * (7 * 4 * 2 + 4 * 4))
    bc1 = 1.0 - ADAM_B1 ** ADAM_STEP
    bc2 = 1.0 - ADAM_B2 ** ADAM_STEP

    def body(w_ref, g_ref, m_ref, v_ref, d_ref, nm_ref, nv_ref):
        gv = g_ref[...]
        nm = ADAM_B1 * m_ref[...] + (1.0 - ADAM_B1) * gv
        nv = ADAM_B2 * v_ref[...] + (1.0 - ADAM_B2) * (gv * gv)
        nm_ref[...] = nm
        nv_ref[...] = nv
        d_ref[...] = -ADAM_LR * ((nm / bc1) / (jnp.sqrt(nv / bc2) + ADAM_EPS) + ADAM_WD * w_ref[...])

    spec = pl.BlockSpec((rb, c), lambda i: (i, 0))
    return pl.pallas_call(
        body, name=name, grid=(r // rb,), out_shape=[jax.ShapeDtypeStruct((r, c), F32)] * 3,
        in_specs=[spec] * 4, out_specs=[spec] * 3, compiler_params=_params("parallel"),
    )(w, g, m, v)


def kernel(x, norm_gains, pool_w, pool_scale, conv_in_w, conv_w, conv_out_w, ffn_gate_up_w, ffn_down_w, loss_target, m_norm_gains, m_pool_w, m_pool_scale, m_conv_in_w, m_conv_w, m_conv_out_w, m_ffn_gate_up_w, m_ffn_down_w, v_norm_gains, v_pool_w, v_pool_scale, v_conv_in_w, v_conv_w, v_conv_out_w, v_ffn_gate_up_w, v_ffn_down_w):
    _, t, d = x.shape
    dq = d // N_CHIPS
    ng = len(POOL_WINDOWS)
    gw = d // ng
    fq = ffn_down_w.shape[1]
    f = N_CHIPS * fq
    fc = f // 2
    core = lax.axis_index("c")
    chip = 2 * lax.axis_index("x") + lax.axis_index("y")
    core_arr = jnp.reshape(core, (1,)).astype(jnp.int32)
    where_arr = jnp.stack([chip, core]).astype(jnp.int32)
    x2, target = x[0], loss_target[0]

    small_loc = jnp.concatenate(
        [norm_gains.reshape(8, dq), conv_w[0], jnp.zeros((5, dq), F32)], axis=0).reshape(2, 8, dq)
    wgu_f, wd_f, win_f, wout_f, pool_f, small_f = _all_gather(
        [ffn_gate_up_w.astype(BF16), ffn_down_w.astype(BF16), conv_in_w[0].astype(BF16).reshape(2, d // 2, -1),
         conv_out_w[0].astype(BF16).reshape(2, dq // 2, d), pool_w[0].astype(BF16).reshape(2, ng // 2 * (gw // N_CHIPS), gw),
         small_loc],
        [True, True, False, False, False, False], "ag_weights")
    wd_f = wd_f.reshape(2, f, d)
    win_f = win_f.reshape(N_CHIPS, d, -1)
    wout_f = wout_f.reshape(d, d)
    poolw = pool_f.reshape(N_CHIPS, ng, gw // N_CHIPS, gw).transpose(1, 0, 2, 3).reshape(ng, gw, gw)
    small = small_f.transpose(1, 2, 0, 3).reshape(16, d)

    h1 = _fwd_pool(x2, small, pool_scale, poolw)
    h2, gu0, ff0, n0 = _fwd_ffn(h1, small, wgu_f, wd_f, 0)
    h3, proj, y, nc = _fwd_conv(h2, small, win_f, wout_f)
    h4, gu1, ff1, n1 = _fwd_ffn(h3, small, wgu_f, wd_f, 1)
    dh4, loss_blk = _loss_grad(h4, target)
    loss = lax.psum(loss_blk[0, 0], ("x", "y", "c"))

    dh3, dgu1, dff1, act1, sg_f1 = _bwd_ffn(dh4, h3, ff1, gu1, small, wgu_f, wd_f, 1)
    g_gu1 = _weight_grad(n1, dgu1, d // 2, fc, "dw_gate_up1")
    g_d1 = _weight_grad(act1, dff1, DOWN_GRAD_ROWS, d, "dw_down1")
    dh2, dproj, dyv, bcv, sg_c = _bwd_conv(dh3, h2, y, proj, small, win_f, wout_f)
    g_in = _weight_grad(nc, dproj, d // 2, 3 * d // N_CHIPS, "dw_conv_in")
    g_out = _weight_grad(bcv, dyv, d // 2, d, "dw_conv_out")
    dh1, dgu0, dff0, act0, sg_f0 = _bwd_ffn(dh2, h1, ff0, gu0, small, wgu_f, wd_f, 0)
    g_gu0 = _weight_grad(n0, dgu0, d // 2, fc, "dw_gate_up0")
    g_d0 = _weight_grad(act0, dff0, DOWN_GRAD_ROWS, d, "dw_down0")
    grad_x, dpool, sg_p = _bwd_pool(dh1, x2, small, pool_scale, poolw)

    pieces = [
        g_gu0, g_gu1,
        g_d0.reshape(N_CHIPS, 2, fq // 2, d), g_d1.reshape(N_CHIPS, 2, fq // 2, d),
        g_in, g_out.reshape(N_CHIPS, 2, dq // 2, d),
        dpool.astype(BF16).reshape(2, ng // 2, N_CHIPS, gw // N_CHIPS, gw).transpose(2, 0, 1, 3, 4).reshape(
            N_CHIPS, 2, ng // 2 * (gw // N_CHIPS), gw),
    ]
    zero_row = jnp.zeros((1, d), F32)
    small_g = jnp.concatenate(
        [sg_p[0:2], sg_f0[0:2], sg_c[0:2], sg_f1[0:2], sg_c[2:5], sg_p[2:3]] + [zero_row] * 4, axis=0)

    *landed, small_all = _sibling_exchange(pieces, small_g)
    parts = [_add_sibling(g, l, core_arr) for g, l in zip(pieces, landed)]
    received = _chip_exchange(parts)
    halves = [_add_chips(p, r, where_arr) for p, r in zip(parts, received)]
    full = _sibling_share(halves)
    small_sum = _sum_small(small_all)

    gg_gu = jnp.stack([full[0].reshape(d, -1), full[1].reshape(d, -1)])
    gg_d = jnp.stack([full[2].reshape(fq, d), full[3].reshape(fq, d)])
    gg_in = full[4].reshape(1, d, -1)
    gg_out = full[5].reshape(1, dq, d)
    gg_pool = full[6].reshape(1, ng, gw // N_CHIPS, gw)
    mine = lax.dynamic_slice_in_dim(small_sum, chip * dq, dq, axis=1)
    gg_gains = mine[0:8].reshape(2, 4, dq)
    gg_taps = mine[8:11].reshape(1, 3, dq)
    gg_scale = small_sum[11:12]

    grads = [gg_gains, gg_pool, gg_scale, gg_in, gg_taps, gg_out, gg_gu, gg_d]
    weights = [norm_gains, pool_w, pool_scale, conv_in_w, conv_w, conv_out_w, ffn_gate_up_w, ffn_down_w]
    ms = [m_norm_gains, m_pool_w, m_pool_scale, m_conv_in_w, m_conv_w, m_conv_out_w, m_ffn_gate_up_w, m_ffn_down_w]
    vs = [v_norm_gains, v_pool_w, v_pool_scale, v_conv_in_w, v_conv_w, v_conv_out_w, v_ffn_gate_up_w, v_ffn_down_w]
    names = ["gains", "pool_w", "pool_scale", "conv_in", "taps", "conv_out", "gate_up", "down"]
    deltas, new_ms, new_vs = [], [], []
    for w, g, m, v, nm in zip(weights, grads, ms, vs, names):
        flat = (-1, w.shape[-1])
        dl, m2, v2 = _adamw(w.reshape(flat), g.reshape(flat), m.reshape(flat), v.reshape(flat), "adamw_" + nm)
        deltas.append(dl.reshape(w.shape))
        new_ms.append(m2.reshape(w.shape))
        new_vs.append(v2.reshape(w.shape))
    return (loss, grad_x[None], *grads, *deltas, *new_ms, *new_vs)
```

```python
import jax
import jax.numpy as jnp
from jax import lax
from jax.experimental import pallas as pl
from jax.experimental.pallas import tpu as pltpu

RMS_EPS = 1e-6
POOL_WINDOWS = (2, 4, 8, 16)
POOL_HALO = 16
CONV_HALO = 8
N_CHIPS = 4
N_DEV = 8
ADAM_LR = 0.001
ADAM_B1 = 0.9
ADAM_B2 = 0.999
ADAM_EPS = 1e-08
ADAM_WD = 0.01
ADAM_STEP = 10
VMEM_LIMIT = 56 * 2**20
DOWN_GRAD_ROWS = 256
STREAM_BUDGET = 24 * 2**20
MESH = pl.DeviceIdType.MESH
ANY = pl.BlockSpec(memory_space=pl.ANY)
DMA = pltpu.SemaphoreType.DMA
BF16 = jnp.bfloat16
F32 = jnp.float32


def _token_tile(t):
    return min(256, t)


def _rms(x):
    r = lax.rsqrt(jnp.mean(x * x, axis=-1, keepdims=True) + RMS_EPS)
    return x * r, r


def _rms_bwd(dy, xh, r, g):
    a = dy * g
    return r * (a - xh * jnp.mean(a * xh, axis=-1, keepdims=True))


def _dot(a, b):
    return jnp.dot(a, b, preferred_element_type=F32)


def _dot_nt(a, b):
    return lax.dot_general(a, b, (((1,), (1,)), ((), ())), preferred_element_type=F32)


def _dot_tn(a, b):
    return lax.dot_general(a, b, (((0,), (0,)), ((), ())), preferred_element_type=F32)


def _colsum(a):
    return jnp.sum(a, axis=0, keepdims=True)


def _resident(block, index_map):
    return pl.BlockSpec(block, index_map, pipeline_mode=pl.Buffered(1))


def _row_block(r, row_bytes):
    best = None
    for rb in range(16, r + 1, 16):
        if r % rb == 0 and rb * row_bytes <= STREAM_BUDGET:
            best = rb
    return best if best is not None else r


def _place():
    x, y, c = lax.axis_index("x"), lax.axis_index("y"), lax.axis_index("c")
    return x, y, c, 2 * x + y


def _dev(chip, core):
    return (chip // 2, chip % 2, core)


def _remote(src, dst, send_sem, recv_sem, device):
    return pltpu.make_async_remote_copy(src_ref=src, dst_ref=dst, send_sem=send_sem, recv_sem=recv_sem,
                                        device_id=device, device_id_type=MESH)


class _Gather:
    def __init__(self, shards):
        n = len(shards)
        self.args = list(shards)
        self.out_shape = [jax.ShapeDtypeStruct((N_CHIPS,) + s.shape, s.dtype) for s in shards]
        self.sems = [DMA((n,)), DMA((n,)), DMA((n, 3)), DMA((n, 3)), DMA((n, 3)), DMA((n, 3))]

    def _own(self, loc, out, sems, a):
        x, y, c, k = _place()
        return _remote(loc[a], out[a].at[k], sems[0].at[a], sems[1].at[a], (x, y, 1 - c))

    def _ici(self, loc, out, sems, a, m, arrival):
        x, y, c, k = _place()
        dst = out[a].at[k ^ m, c] if arrival else out[a].at[k, c]
        return _remote(loc[a].at[c], dst, sems[2].at[a, m - 1], sems[3].at[a, m - 1], _dev(k ^ m, c))

    def _forward(self, out, sems, a, m, arrival):
        x, y, c, k = _place()
        got = out[a].at[k ^ m, 1 - c] if arrival else out[a].at[k ^ m, c]
        return _remote(got, got, sems[4].at[a, m - 1], sems[5].at[a, m - 1], (x, y, 1 - c))

    def start(self, loc, out, sems):
        for a in range(len(self.args)):
            for m in range(1, N_CHIPS):
                self._ici(loc, out, sems, a, m, False).start()
            self._own(loc, out, sems, a).start()

    def finish(self, loc, out, sems):
        n = len(self.args)
        for a in range(n):
            for m in range(1, N_CHIPS):
                self._ici(loc, out, sems, a, m, True).wait_recv()
                self._forward(out, sems, a, m, False).start()
        for a in range(n):
            for m in range(1, N_CHIPS):
                self._forward(out, sems, a, m, True).wait_recv()
            self._own(loc, out, sems, a).wait_recv()
        for a in range(n):
            for m in range(1, N_CHIPS):
                self._ici(loc, out, sems, a, m, False).wait_send()
                self._forward(out, sems, a, m, False).wait_send()
            self._own(loc, out, sems, a).wait_send()


class _ChipExchange:
    def __init__(self, parts):
        n = len(parts)
        self.args = list(parts)
        self.out_shape = [jax.ShapeDtypeStruct((N_CHIPS - 1,) + p.shape[1:], p.dtype) for p in parts]
        self.sems = [DMA((n, 3)), DMA((n, 3))]

    def _copy(self, p, land, sems, a, m):
        x, y, c, k = _place()
        return _remote(p[a].at[k ^ m], land[a].at[m - 1], sems[0].at[a, m - 1], sems[1].at[a, m - 1], _dev(k ^ m, c))

    def start(self, p, land, sems):
        for a in range(len(self.args)):
            for m in range(1, N_CHIPS):
                self._copy(p, land, sems, a, m).start()

    def finish(self, p, land, sems):
        for a in range(len(self.args)):
            for m in range(1, N_CHIPS):
                self._copy(p, land, sems, a, m).wait_recv()
        for a in range(len(self.args)):
            for m in range(1, N_CHIPS):
                self._copy(p, land, sems, a, m).wait_send()


def _hosted(body, comm, *, name, grid, in_specs, out_specs, out_shape, args, scratch_shapes=()):
    ni, no, ns = len(in_specs), len(out_shape), len(scratch_shapes)
    if comm is None:
        res = pl.pallas_call(
            body, name=name, grid=grid, in_specs=list(in_specs), out_specs=list(out_specs), out_shape=list(out_shape),
            scratch_shapes=list(scratch_shapes),
            compiler_params=pltpu.CompilerParams(dimension_semantics=("arbitrary",) * len(grid), vmem_limit_bytes=VMEM_LIMIT),
        )(*args)
        return list(res), []
    nc, nco = len(comm.args), len(comm.out_shape)

    def full(*refs):
        cin = refs[ni:ni + nc]
        outs = refs[ni + nc:ni + nc + no]
        cout = refs[ni + nc + no:ni + nc + no + nco]
        scratch = refs[ni + nc + no + nco:ni + nc + no + nco + ns]
        csems = refs[ni + nc + no + nco + ns:]
        first = _all_of([pl.program_id(ax) == 0 for ax in range(len(grid))])
        last = _all_of([pl.program_id(ax) == grid[ax] - 1 for ax in range(len(grid))])

        @pl.when(first)
        def _():
            comm.start(cin, cout, csems)

        body(*refs[:ni], *outs, *scratch)

        @pl.when(last)
        def _():
            comm.finish(cin, cout, csems)

    res = pl.pallas_call(
        full, name=name, grid=grid, in_specs=[*in_specs, *[ANY] * nc], out_specs=[*out_specs, *[ANY] * nco],
        out_shape=[*out_shape, *comm.out_shape], scratch_shapes=[*scratch_shapes, *comm.sems],
        compiler_params=pltpu.CompilerParams(dimension_semantics=("arbitrary",) * len(grid), vmem_limit_bytes=VMEM_LIMIT,
                                             has_side_effects=True),
    )(*args, *comm.args)
    return list(res[:no]), list(res[no:])


def _all_of(conds):
    out = conds[0]
    for c in conds[1:]:
        out = jnp.logical_and(out, c)
    return out


def _alone(comm, name):
    return _hosted(lambda: None, comm, name=name, grid=(1,), in_specs=[], out_specs=[], out_shape=[], args=[])[1]


def _sibling_exchange(grads, small, name):
    n = len(grads)
    ns = 0 if small is None else 1

    def body(*refs):
        g = refs[:n]
        land = refs[n + ns:2 * n + ns]
        send_sems, recv_sems, own_sem, ssend_sems, srecv_sems = refs[2 * n + 2 * ns:]
        x, y, c, k = _place()
        me = 2 * k + c
        cps = []
        for a in range(n):
            cp = _remote(g[a].at[:, pl.ds(1 - c, 1)], land[a], send_sems.at[a], recv_sems.at[a], (x, y, 1 - c))
            cp.start()
            cps.append(cp)
        if small is not None:
            sm, smg = refs[n], refs[2 * n + 1]
            peers = [me ^ m for m in range(1, N_DEV)]
            ids = [(p // 4, (p // 2) % 2, p % 2) for p in peers]
            own = pltpu.make_async_copy(sm, smg.at[me], own_sem)
            own.start()
            for m in range(1, N_DEV):
                cp = _remote(sm, smg.at[me], ssend_sems.at[m - 1], srecv_sems.at[m - 1], ids[m - 1])
                cp.start()
                cps.append(cp)
            for m in range(1, N_DEV):
                _remote(sm, smg.at[peers[m - 1]], ssend_sems.at[m - 1], srecv_sems.at[m - 1], ids[m - 1]).wait_recv()
            own.wait()
        for cp in cps[:n]:
            cp.wait_recv()
        for cp in cps:
            cp.wait_send()

    out_shape = [jax.ShapeDtypeStruct((N_CHIPS, 1) + a.shape[2:], a.dtype) for a in grads]
    ins = list(grads)
    if small is not None:
        out_shape.append(jax.ShapeDtypeStruct((N_DEV,) + small.shape, small.dtype))
        ins.append(small)
    return pl.pallas_call(
        body, name=name, out_shape=out_shape, in_specs=[ANY] * (n + ns), out_specs=[ANY] * (n + ns),
        scratch_shapes=[DMA((n,)), DMA((n,)), DMA, DMA((N_DEV - 1,)), DMA((N_DEV - 1,))],
        compiler_params=pltpu.CompilerParams(has_side_effects=True),
    )(*ins)


def _sibling_share(halves):
    n = len(halves)

    def body(*refs):
        out = refs[n:2 * n]
        send_sems, recv_sems = refs[2 * n:]
        x, y, c, k = _place()
        cps = []
        for a in range(n):
            cp = _remote(out[a].at[c], out[a].at[c], send_sems.at[a], recv_sems.at[a], (x, y, 1 - c))
            cp.start()
            cps.append(cp)
        for a in range(n):
            _remote(out[a].at[1 - c], out[a].at[1 - c], send_sems.at[a], recv_sems.at[a], (x, y, 1 - c)).wait_recv()
        for cp in cps:
            cp.wait_send()

    out_shape = [jax.ShapeDtypeStruct(a.shape, a.dtype) for a in halves]
    return pl.pallas_call(
        body, name="rs_sibling_share", out_shape=out_shape, in_specs=[ANY] * n, out_specs=[ANY] * n,
        input_output_aliases={a: a for a in range(n)}, scratch_shapes=[DMA((n,)), DMA((n,))],
        compiler_params=pltpu.CompilerParams(has_side_effects=True),
    )(*halves)


def _add_sibling(g, land, core):
    _, _, r, c = g.shape
    rb = _row_block(r, c * (3 * 2 * 2 + 2 * 4))

    def body(core_ref, g_ref, l_ref, o_ref):
        o_ref[...] = (g_ref[...].astype(F32) + l_ref[...].astype(F32)).astype(o_ref.dtype)

    return pl.pallas_call(
        body, name="rs_add_sibling", out_shape=jax.ShapeDtypeStruct((N_CHIPS, r, c), g.dtype),
        grid_spec=pltpu.PrefetchScalarGridSpec(
            num_scalar_prefetch=1, grid=(N_CHIPS, r // rb),
            in_specs=[pl.BlockSpec((None, None, rb, c), lambda j, i, core_ref: (j, core_ref[0], i, 0)),
                      pl.BlockSpec((None, None, rb, c), lambda j, i, core_ref: (j, 0, i, 0))],
            out_specs=pl.BlockSpec((None, rb, c), lambda j, i, core_ref: (j, i, 0))),
        compiler_params=pltpu.CompilerParams(dimension_semantics=("parallel", "parallel"), vmem_limit_bytes=VMEM_LIMIT),
    )(core, g, land)


def _add_chips(part, land, where):
    _, r, c = part.shape
    rb = _row_block(r, c * (4 * 2 * 2 + 4 * 2 + 2 * 4))

    def body(where_ref, p_ref, l_ref, o_ref):
        acc = p_ref[...].astype(F32)
        for m in range(N_CHIPS - 1):
            acc = acc + l_ref[m].astype(F32)
        o_ref[...] = acc

    return pl.pallas_call(
        body, name="rs_add_chips", out_shape=jax.ShapeDtypeStruct((2, r, c), F32),
        grid_spec=pltpu.PrefetchScalarGridSpec(
            num_scalar_prefetch=1, grid=(r // rb,),
            in_specs=[pl.BlockSpec((None, rb, c), lambda i, where_ref: (where_ref[0], i, 0)),
                      pl.BlockSpec((N_CHIPS - 1, rb, c), lambda i, where_ref: (0, i, 0))],
            out_specs=pl.BlockSpec((None, rb, c), lambda i, where_ref: (where_ref[1], i, 0))),
        compiler_params=pltpu.CompilerParams(dimension_semantics=("parallel",), vmem_limit_bytes=VMEM_LIMIT),
    )(where, part, land)


def _sum_small(smg):
    def body(s_ref, o_ref):
        acc = s_ref[0]
        for j in range(1, N_DEV):
            acc = acc + s_ref[j]
        o_ref[...] = acc

    return pl.pallas_call(body, name="rs_sum_small", out_shape=jax.ShapeDtypeStruct(smg.shape[1:], F32))(smg)


def _pool_windows(ext_ref, g, gw, tm, first_row):
    w = POOL_WINDOWS[g]
    slab = ext_ref[:, g * gw:(g + 1) * gw]
    p, k = slab, 1
    while k < w:
        p = p + pltpu.roll(p, k, 0)
        k *= 2
    t = first_row + lax.broadcasted_iota(jnp.int32, (tm, 1), 0)
    cnt = jnp.minimum(t + 1, w).astype(F32)
    return p[POOL_HALO:] / cnt - slab[POOL_HALO:]


def _fwd_pool(x, small, scale, poolw, comm=None):
    t, d = x.shape
    tm = _token_tile(t)
    gw = d // len(POOL_WINDOWS)

    def body(x_ref, sm_ref, sc_ref, w_ref, h_ref, ext_ref, mix_ref):
        i = pl.program_id(0)

        @pl.when(i == 0)
        def _():
            ext_ref[0:POOL_HALO, :] = jnp.zeros((POOL_HALO, d), F32)

        @pl.when(i > 0)
        def _():
            ext_ref[0:POOL_HALO, :] = ext_ref[tm:tm + POOL_HALO, :]

        xv = x_ref[...]
        xh, _ = _rms(xv)
        ext_ref[POOL_HALO:, :] = xh * sm_ref[0:1, :]
        for g in range(len(POOL_WINDOWS)):
            pooled = _pool_windows(ext_ref, g, gw, tm, i * tm)
            cols = slice(g * gw, (g + 1) * gw)
            mix_ref[:, cols] = _dot(pooled.astype(BF16), w_ref[g]) * sc_ref[:, cols]
        mh, _ = _rms(mix_ref[...])
        h_ref[...] = xv + mh * sm_ref[1:2, :]

    (h,), got = _hosted(
        body, comm, name="fwd_pool", grid=(t // tm,), out_shape=[jax.ShapeDtypeStruct((t, d), F32)],
        in_specs=[pl.BlockSpec((tm, d), lambda i: (i, 0)), _resident(small.shape, lambda i: (0, 0)),
                  _resident(scale.shape, lambda i: (0, 0)), _resident(poolw.shape, lambda i: (0, 0, 0))],
        out_specs=[pl.BlockSpec((tm, d), lambda i: (i, 0))],
        scratch_shapes=[pltpu.VMEM((POOL_HALO + tm, d), F32), pltpu.VMEM((tm, d), F32)],
        args=[x, small, scale, poolw])
    return h, got


def _fwd_ffn(h, small, wgu, wd, layer, comm=None):
    t, d = h.shape
    tm = _token_tile(t)
    fc = wgu.shape[-1]
    f = 2 * fc
    g_in, g_out = 4 * layer + 2, 4 * layer + 3

    def body(h_ref, sm_ref, wgu_ref, wd_ref, o_ref, gu_ref, ff_ref, n_ref):
        hv = h_ref[...]
        hh, _ = _rms(hv)
        n = (hh * sm_ref[g_in:g_in + 1, :]).astype(BF16)
        n_ref[...] = n
        ff = None
        for j in range(2):
            gate = _dot(n, wgu_ref[j])
            up = _dot(n, wgu_ref[2 + j])
            gu_ref[:, j * fc:(j + 1) * fc] = gate.astype(BF16)
            gu_ref[:, f + j * fc:f + (j + 1) * fc] = up.astype(BF16)
            act = (gate * jax.nn.sigmoid(gate) * up).astype(BF16)
            part = _dot(act, wd_ref[j * fc:(j + 1) * fc, :])
            ff = part if ff is None else ff + part
        ff_ref[...] = ff
        fh, _ = _rms(ff)
        o_ref[...] = hv + fh * sm_ref[g_out:g_out + 1, :]

    row = lambda i: (i, 0)
    return _hosted(
        body, comm, name=f"fwd_ffn{layer}", grid=(t // tm,),
        out_shape=[jax.ShapeDtypeStruct((t, d), F32), jax.ShapeDtypeStruct((t, 2 * f), BF16),
                   jax.ShapeDtypeStruct((t, d), F32), jax.ShapeDtypeStruct((t, d), BF16)],
        in_specs=[pl.BlockSpec((tm, d), row), _resident(small.shape, lambda i: (0, 0)),
                  _resident(wgu.shape, lambda i: (0, 0, 0)), _resident(wd.shape, lambda i: (0, 0))],
        out_specs=[pl.BlockSpec((tm, d), row), pl.BlockSpec((tm, 2 * f), row), pl.BlockSpec((tm, d), row),
                   pl.BlockSpec((tm, d), row)],
        args=[h, small, wgu, wd])


def _fwd_conv(h, small, win, wout, comm=None):
    t, d = h.shape
    tm = _token_tile(t)
    pc = win.shape[-1]

    def body(h_ref, sm_ref, win_ref, wout_ref, o_ref, proj_ref, y_ref, n_ref, pj_ref, uext_ref):
        i = pl.program_id(0)

        @pl.when(i == 0)
        def _():
            uext_ref[0:CONV_HALO, :] = jnp.zeros((CONV_HALO, d), F32)

        @pl.when(i > 0)
        def _():
            uext_ref[0:CONV_HALO, :] = uext_ref[tm:tm + CONV_HALO, :]

        hv = h_ref[...]
        hh, _ = _rms(hv)
        n = (hh * sm_ref[4:5, :]).astype(BF16)
        n_ref[...] = n
        for k in range(N_CHIPS):
            pj_ref[:, k * pc:(k + 1) * pc] = _dot(n, win_ref[k])
        proj_ref[...] = pj_ref[...].astype(BF16)
        uext_ref[CONV_HALO:, :] = pj_ref[:, d:2 * d] * pj_ref[:, 2 * d:]
        taps = [sm_ref[8 + j:9 + j, :] for j in range(3)]
        full = uext_ref[...]
        conv = (full[CONV_HALO:] * taps[2] + pltpu.roll(full, 1, 0)[CONV_HALO:] * taps[1]
                + pltpu.roll(full, 2, 0)[CONV_HALO:] * taps[0])
        y = _dot((pj_ref[:, 0:d] * conv).astype(BF16), wout_ref[...])
        y_ref[...] = y
        yh, _ = _rms(y)
        o_ref[...] = hv + yh * sm_ref[5:6, :]

    row = lambda i: (i, 0)
    return _hosted(
        body, comm, name="fwd_conv", grid=(t // tm,),
        out_shape=[jax.ShapeDtypeStruct((t, d), F32), jax.ShapeDtypeStruct((t, 3 * d), BF16),
                   jax.ShapeDtypeStruct((t, d), F32), jax.ShapeDtypeStruct((t, d), BF16)],
        in_specs=[pl.BlockSpec((tm, d), row), _resident(small.shape, lambda i: (0, 0)),
                  _resident(win.shape, lambda i: (0, 0, 0)), _resident(wout.shape, lambda i: (0, 0))],
        out_specs=[pl.BlockSpec((tm, d), row), pl.BlockSpec((tm, 3 * d), row), pl.BlockSpec((tm, d), row),
                   pl.BlockSpec((tm, d), row)],
        scratch_shapes=[pltpu.VMEM((tm, 3 * d), F32), pltpu.VMEM((CONV_HALO + tm, d), F32)],
        args=[h, small, win, wout])


def _loss_grad(h, target):
    t, d = h.shape
    tm = _token_tile(t)
    steps = t // tm

    def body(h_ref, t_ref, g_ref, l_ref, acc_ref):
        i = pl.program_id(0)
        e = h_ref[...] - t_ref[...]
        g_ref[...] = e * (1.0 / d)

        @pl.when(i == 0)
        def _():
            acc_ref[...] = jnp.zeros_like(acc_ref)

        acc_ref[...] += _colsum(e * e)

        @pl.when(i == steps - 1)
        def _():
            l_ref[...] = jnp.full(l_ref.shape, 0.5 / d, F32) * jnp.sum(acc_ref[...])

    row = lambda i: (i, 0)
    return _hosted(
        body, None, name="loss_grad", grid=(steps,),
        out_shape=[jax.ShapeDtypeStruct((t, d), F32), jax.ShapeDtypeStruct((8, 128), F32)],
        in_specs=[pl.BlockSpec((tm, d), row), pl.BlockSpec((tm, d), row)],
        out_specs=[pl.BlockSpec((tm, d), row), pl.BlockSpec((8, 128), lambda i: (0, 0))],
        scratch_shapes=[pltpu.VMEM((1, d), F32)], args=[h, target])[0]


def _bwd_ffn(dh, h, ff, gu, small, wgu, wd, layer, comm=None):
    t, d = h.shape
    tm = _token_tile(t)
    fc = wgu.shape[-1]
    f = 2 * fc
    g_in, g_out = 4 * layer + 2, 4 * layer + 3

    def body(dh_ref, h_ref, ff_ref, gu_ref, sm_ref, wgu_ref, wd_ref, o_ref, dgu_ref, dff_ref, act_ref, sg_ref):
        i = pl.program_id(0)

        @pl.when(i == 0)
        def _():
            sg_ref[...] = jnp.zeros_like(sg_ref)

        dy = dh_ref[...]
        fh, r3 = _rms(ff_ref[...])
        sg_ref[1:2, :] += _colsum(dy * fh)
        dff = _rms_bwd(dy, fh, r3, sm_ref[g_out:g_out + 1, :]).astype(BF16)
        dff_ref[...] = dff
        for j in range(2):
            dact = _dot_nt(dff, wd_ref[j * fc:(j + 1) * fc, :])
            gate = gu_ref[:, j * fc:(j + 1) * fc].astype(F32)
            up = gu_ref[:, f + j * fc:f + (j + 1) * fc].astype(F32)
            sig = jax.nn.sigmoid(gate)
            silu = gate * sig
            act_ref[:, j * fc:(j + 1) * fc] = (silu * up).astype(BF16)
            dgu_ref[:, j * fc:(j + 1) * fc] = (dact * up * (sig * (1.0 + gate * (1.0 - sig)))).astype(BF16)
            dgu_ref[:, f + j * fc:f + (j + 1) * fc] = (dact * silu).astype(BF16)
        dn = None
        for k in range(N_CHIPS):
            part = _dot_nt(dgu_ref[:, k * fc:(k + 1) * fc], wgu_ref[k])
            dn = part if dn is None else dn + part
        hh, r2 = _rms(h_ref[...])
        sg_ref[0:1, :] += _colsum(dn * hh)
        o_ref[...] = dy + _rms_bwd(dn, hh, r2, sm_ref[g_in:g_in + 1, :])

    row = lambda i: (i, 0)
    return _hosted(
        body, comm, name=f"bwd_ffn{layer}", grid=(t // tm,),
        out_shape=[jax.ShapeDtypeStruct((t, d), F32), jax.ShapeDtypeStruct((t, 2 * f), BF16),
                   jax.ShapeDtypeStruct((t, d), BF16), jax.ShapeDtypeStruct((t, f), BF16),
                   jax.ShapeDtypeStruct((8, d), F32)],
        in_specs=[pl.BlockSpec((tm, d), row), pl.BlockSpec((tm, d), row), pl.BlockSpec((tm, d), row),
                  pl.BlockSpec((tm, 2 * f), row), _resident(small.shape, lambda i: (0, 0)),
                  _resident(wgu.shape, lambda i: (0, 0, 0)), _resident(wd.shape, lambda i: (0, 0))],
        out_specs=[pl.BlockSpec((tm, d), row), pl.BlockSpec((tm, 2 * f), row), pl.BlockSpec((tm, d), row),
                   pl.BlockSpec((tm, f), row), pl.BlockSpec((8, d), lambda i: (0, 0))],
        args=[dh, h, ff, gu, small, wgu, wd])


def _bwd_conv(dh, h, y, proj, small, win, wout, comm=None):
    t, d = h.shape
    tm = _token_tile(t)
    steps = t // tm
    pc = win.shape[-1]
    halo_blocks = tm // 16

    def body(dh_ref, h_ref, y_ref, proj_ref, halo_ref, sm_ref, win_ref, wout_ref,
             o_ref, dproj_ref, dy_ref, bc_ref, sg_ref, uext_ref, dcext_ref, carry_ref):
        i = pl.program_id(0)
        tile = steps - 1 - i

        @pl.when(i == 0)
        def _():
            sg_ref[...] = jnp.zeros_like(sg_ref)
            carry_ref[...] = jnp.zeros_like(carry_ref)

        dy = dh_ref[...]
        yh, r1 = _rms(y_ref[...])
        sg_ref[1:2, :] += _colsum(dy * yh)
        dyv = _rms_bwd(dy, yh, r1, sm_ref[5:6, :]).astype(BF16)
        dy_ref[...] = dyv
        dbc = _dot_nt(dyv, wout_ref[...])
        b = proj_ref[:, 0:d].astype(F32)
        cg = proj_ref[:, d:2 * d].astype(F32)
        v = proj_ref[:, 2 * d:].astype(F32)
        halo = halo_ref[...].astype(F32)[16 - CONV_HALO:]
        uh = halo[:, d:2 * d] * halo[:, 2 * d:]
        uext_ref[0:CONV_HALO, :] = jnp.where(tile > 0, uh, jnp.zeros_like(uh))
        uext_ref[CONV_HALO:, :] = cg * v
        taps = [sm_ref[8 + j:9 + j, :] for j in range(3)]
        full = uext_ref[...]
        u0 = full[CONV_HALO:]
        u1 = pltpu.roll(full, 1, 0)[CONV_HALO:]
        u2 = pltpu.roll(full, 2, 0)[CONV_HALO:]
        conv = u0 * taps[2] + u1 * taps[1] + u2 * taps[0]
        bc_ref[...] = (b * conv).astype(BF16)
        dconv = dbc * b
        sg_ref[4:5, :] += _colsum(dconv * u0)
        sg_ref[3:4, :] += _colsum(dconv * u1)
        sg_ref[2:3, :] += _colsum(dconv * u2)
        dcext_ref[0:tm, :] = dconv
        dcext_ref[tm:, :] = carry_ref[...]
        carry_ref[...] = dconv[0:CONV_HALO]
        dfull = dcext_ref[...]
        n8 = tm + CONV_HALO
        du = (dfull[0:tm] * taps[2] + pltpu.roll(dfull, n8 - 1, 0)[0:tm] * taps[1]
              + pltpu.roll(dfull, n8 - 2, 0)[0:tm] * taps[0])
        dproj_ref[:, 0:d] = (dbc * conv).astype(BF16)
        dproj_ref[:, d:2 * d] = (du * v).astype(BF16)
        dproj_ref[:, 2 * d:] = (du * cg).astype(BF16)
        dn = None
        for k in range(N_CHIPS):
            part = _dot_nt(dproj_ref[:, k * pc:(k + 1) * pc], win_ref[k])
            dn = part if dn is None else dn + part
        hh, r0 = _rms(h_ref[...])
        sg_ref[0:1, :] += _colsum(dn * hh)
        o_ref[...] = dy + _rms_bwd(dn, hh, r0, sm_ref[4:5, :])

    rev = lambda i: (steps - 1 - i, 0)
    before = lambda i: (jnp.maximum((steps - 1 - i) * halo_blocks - 1, 0), 0)
    return _hosted(
        body, comm, name="bwd_conv", grid=(steps,),
        out_shape=[jax.ShapeDtypeStruct((t, d), F32), jax.ShapeDtypeStruct((t, 3 * d), BF16),
                   jax.ShapeDtypeStruct((t, d), BF16), jax.ShapeDtypeStruct((t, d), BF16),
                   jax.ShapeDtypeStruct((8, d), F32)],
        in_specs=[pl.BlockSpec((tm, d), rev), pl.BlockSpec((tm, d), rev), pl.BlockSpec((tm, d), rev),
                  pl.BlockSpec((tm, 3 * d), rev), pl.BlockSpec((16, 3 * d), before),
                  _resident(small.shape, lambda i: (0, 0)), _resident(win.shape, lambda i: (0, 0, 0)),
                  _resident(wout.shape, lambda i: (0, 0))],
        out_specs=[pl.BlockSpec((tm, d), rev), pl.BlockSpec((tm, 3 * d), rev), pl.BlockSpec((tm, d), rev),
                   pl.BlockSpec((tm, d), rev), pl.BlockSpec((8, d), lambda i: (0, 0))],
        scratch_shapes=[pltpu.VMEM((CONV_HALO + tm, d), F32), pltpu.VMEM((tm + CONV_HALO, d), F32),
                        pltpu.VMEM((CONV_HALO, d), F32)],
        args=[dh, h, y, proj, proj, small, win, wout])


def _bwd_pool(dh, x, small, scale, poolw, comm=None):
    t, d = x.shape
    tm = _token_tile(t)
    steps = t // tm
    ng = len(POOL_WINDOWS)
    gw = d // ng
    halo_blocks = tm // POOL_HALO

    def body(dh_ref, x_ref, halo_ref, sm_ref, sc_ref, w_ref, o_ref, dw_ref, sg_ref,
             ext_ref, mix_ref, mm_ref, pb_ref, qext_ref, dhn_ref, carry_ref):
        i = pl.program_id(0)
        tile = steps - 1 - i

        @pl.when(i == 0)
        def _():
            sg_ref[...] = jnp.zeros_like(sg_ref)
            dw_ref[...] = jnp.zeros_like(dw_ref)
            carry_ref[...] = jnp.zeros_like(carry_ref)

        g0 = sm_ref[0:1, :]
        xv = x_ref[...]
        xh, r0 = _rms(xv)
        hx, _ = _rms(halo_ref[...])
        ext_ref[0:POOL_HALO, :] = jnp.where(tile > 0, hx * g0, jnp.zeros_like(hx))
        ext_ref[POOL_HALO:, :] = xh * g0
        for g in range(ng):
            pooled = _pool_windows(ext_ref, g, gw, tm, tile * tm)
            cols = slice(g * gw, (g + 1) * gw)
            pb = pooled.astype(BF16)
            pb_ref[:, cols] = pb
            mm = _dot(pb, w_ref[g])
            mm_ref[:, cols] = mm
            mix_ref[:, cols] = mm * sc_ref[:, cols]
        dy = dh_ref[...]
        mh, r1 = _rms(mix_ref[...])
        sg_ref[1:2, :] += _colsum(dy * mh)
        dmix = _rms_bwd(dy, mh, r1, sm_ref[1:2, :])
        sg_ref[2:3, :] += _colsum(dmix * mm_ref[...])
        mix_ref[...] = dmix * sc_ref[...]
        n16 = tm + POOL_HALO
        for g in range(ng):
            w = POOL_WINDOWS[g]
            cols = slice(g * gw, (g + 1) * gw)
            dmm = mix_ref[:, cols].astype(BF16)
            dpooled = _dot_nt(dmm, w_ref[g])
            dw_ref[g] += _dot_tn(pb_ref[:, cols], dmm)
            trow = tile * tm + lax.broadcasted_iota(jnp.int32, (tm, 1), 0)
            q = dpooled / jnp.minimum(trow + 1, w).astype(F32)
            qext_ref[0:tm, cols] = q
            qext_ref[tm:, cols] = carry_ref[:, cols]
            carry_ref[:, cols] = q[0:POOL_HALO]
            p, k = qext_ref[:, cols], 1
            while k < w:
                p = p + pltpu.roll(p, n16 - k, 0)
                k *= 2
            dhn_ref[:, cols] = p[0:tm] - dpooled
        dhn = dhn_ref[...]
        sg_ref[0:1, :] += _colsum(dhn * xh)
        o_ref[...] = dy + _rms_bwd(dhn, xh, r0, g0)

    rev = lambda i: (steps - 1 - i, 0)
    before = lambda i: (jnp.maximum((steps - 1 - i) * halo_blocks - 1, 0), 0)
    return _hosted(
        body, comm, name="bwd_pool", grid=(steps,),
        out_shape=[jax.ShapeDtypeStruct((t, d), F32), jax.ShapeDtypeStruct((ng, gw, gw), F32),
                   jax.ShapeDtypeStruct((8, d), F32)],
        in_specs=[pl.BlockSpec((tm, d), rev), pl.BlockSpec((tm, d), rev), pl.BlockSpec((POOL_HALO, d), before),
                  _resident(small.shape, lambda i: (0, 0)), _resident(scale.shape, lambda i: (0, 0)),
                  _resident(poolw.shape, lambda i: (0, 0, 0))],
        out_specs=[pl.BlockSpec((tm, d), rev), pl.BlockSpec((ng, gw, gw), lambda i: (0, 0, 0)),
                   pl.BlockSpec((8, d), lambda i: (0, 0))],
        scratch_shapes=[pltpu.VMEM((POOL_HALO + tm, d), F32), pltpu.VMEM((tm, d), F32), pltpu.VMEM((tm, d), F32),
                        pltpu.VMEM((tm, d), BF16), pltpu.VMEM((tm + POOL_HALO, d), F32), pltpu.VMEM((tm, d), F32),
                        pltpu.VMEM((POOL_HALO, d), F32)],
        args=[dh, x, x, small, scale, poolw])


def _weight_grad(a, b, bm, bn, name, comm=None):
    t, m = a.shape
    _, n = b.shape

    def body(a_ref, b_ref, o_ref):
        o_ref[...] = _dot_tn(a_ref[...], b_ref[...]).astype(o_ref.dtype)

    (out,), got = _hosted(
        body, comm, name=name, grid=(n // bn, m // bm),
        out_shape=[jax.ShapeDtypeStruct((n // bn, m // bm, bm, bn), BF16)],
        in_specs=[pl.BlockSpec((t, bm), lambda j, i: (0, i)), pl.BlockSpec((t, bn), lambda j, i: (0, j))],
        out_specs=[pl.BlockSpec((None, None, bm, bn), lambda j, i: (j, i, 0, 0))],
        args=[a, b])
    return out, got


def _adamw(w, g, m, v, name):
    r, c = w.shape
    rb = _row_block(r, c * (7 * 4 * 2 + 4 * 4))
    bc1 = 1.0 - ADAM_B1 ** ADAM_STEP
    bc2 = 1.0 - ADAM_B2 ** ADAM_STEP

    def body(w_ref, g_ref, m_ref, v_ref, d_ref, nm_ref, nv_ref):
        gv = g_ref[...]
        nm = ADAM_B1 * m_ref[...] + (1.0 - ADAM_B1) * gv
        nv = ADAM_B2 * v_ref[...] + (1.0 - ADAM_B2) * (gv * gv)
        nm_ref[...] = nm
        nv_ref[...] = nv
        d_ref[...] = -ADAM_LR * ((nm / bc1) / (jnp.sqrt(nv / bc2) + ADAM_EPS) + ADAM_WD * w_ref[...])

    spec = pl.BlockSpec((rb, c), lambda i: (i, 0))
    return pl.pallas_call(
        body, name=name, grid=(r // rb,), out_shape=[jax.ShapeDtypeStruct((r, c), F32)] * 3,
        in_specs=[spec] * 4, out_specs=[spec] * 3,
        compiler_params=pltpu.CompilerParams(dimension_semantics=("parallel",), vmem_limit_bytes=VMEM_LIMIT),
    )(w, g, m, v)


def kernel(x, norm_gains, pool_w, pool_scale, conv_in_w, conv_w, conv_out_w, ffn_gate_up_w, ffn_down_w, loss_target, m_norm_gains, m_pool_w, m_pool_scale, m_conv_in_w, m_conv_w, m_conv_out_w, m_ffn_gate_up_w, m_ffn_down_w, v_norm_gains, v_pool_w, v_pool_scale, v_conv_in_w, v_conv_w, v_conv_out_w, v_ffn_gate_up_w, v_ffn_down_w):
    _, t, d = x.shape
    dq = d // N_CHIPS
    ng = len(POOL_WINDOWS)
    gw = d // ng
    fq = ffn_down_w.shape[1]
    f = N_CHIPS * fq
    fc = f // 2
    core = lax.axis_index("c")
    chip = 2 * lax.axis_index("x") + lax.axis_index("y")
    core_arr = jnp.reshape(core, (1,)).astype(jnp.int32)
    where_arr = jnp.stack([chip, core]).astype(jnp.int32)
    x2, target = x[0], loss_target[0]

    small_loc = jnp.concatenate(
        [norm_gains.reshape(8, dq), conv_w[0], jnp.zeros((5, dq), F32)], axis=0).reshape(2, 8, dq)
    pool_loc = pool_w[0].astype(BF16).reshape(2, ng // 2 * (gw // N_CHIPS), gw)
    wgu_loc = [ffn_gate_up_w[l].astype(BF16).reshape(2, d // 2, fc) for l in range(2)]
    wd_loc = [ffn_down_w[l].astype(BF16).reshape(2, fq // 2, d) for l in range(2)]
    win_loc = conv_in_w[0].astype(BF16).reshape(2, d // 2, -1)
    wout_loc = conv_out_w[0].astype(BF16).reshape(2, dq // 2, d)

    def ffn_weights(wgu_f, wd_f):
        return wgu_f.reshape(N_CHIPS, d, fc), wd_f.reshape(f, d)

    pool_f, small_f = _alone(_Gather([pool_loc, small_loc]), "ag_first")
    poolw = pool_f.reshape(N_CHIPS, ng, gw // N_CHIPS, gw).transpose(1, 0, 2, 3).reshape(ng, gw, gw)
    small = small_f.transpose(1, 2, 0, 3).reshape(16, d)
    h1, got = _fwd_pool(x2, small, pool_scale, poolw, _Gather([wgu_loc[0], wd_loc[0]]))
    wgu0, wd0 = ffn_weights(*got)
    (h2, gu0, ff0, n0), got = _fwd_ffn(h1, small, wgu0, wd0, 0, _Gather([win_loc, wout_loc, wgu_loc[1]]))
    win_f, wout_f = got[0].reshape(N_CHIPS, d, -1), got[1].reshape(d, d)
    (h3, proj, y, nc), got_d = _fwd_conv(h2, small, win_f, wout_f, _Gather([wd_loc[1]]))
    wgu1, wd1 = ffn_weights(got[2], got_d[0])
    (h4, gu1, ff1, n1), _ = _fwd_ffn(h3, small, wgu1, wd1, 1)
    dh4, loss_blk = _loss_grad(h4, target)
    loss = lax.psum(loss_blk[0, 0], ("x", "y", "c"))

    def sibling_sum(pieces, name, small_g=None):
        landed = _sibling_exchange(pieces, small_g, name)
        return [_add_sibling(g, l, core_arr) for g, l in zip(pieces, landed)], landed[len(pieces):]

    (dh3, dgu1, dff1, act1, sg_f1), _ = _bwd_ffn(dh4, h3, ff1, gu1, small, wgu1, wd1, 1)
    g_gu1, _ = _weight_grad(n1, dgu1, d // 2, fc, "dw_gate_up1")
    g_d1, _ = _weight_grad(act1, dff1, DOWN_GRAD_ROWS, d, "dw_down1")
    parts_f1, _ = sibling_sum([g_gu1, g_d1.reshape(N_CHIPS, 2, fq // 2, d)], "rs_sibling_ffn1")
    (dh2, dproj, dyv, bcv, sg_c), recv_f1 = _bwd_conv(dh3, h2, y, proj, small, win_f, wout_f, _ChipExchange(parts_f1))
    g_in, _ = _weight_grad(nc, dproj, d // 2, 3 * d // N_CHIPS, "dw_conv_in")
    g_out, _ = _weight_grad(bcv, dyv, d // 2, d, "dw_conv_out")
    parts_c, _ = sibling_sum([g_in, g_out.reshape(N_CHIPS, 2, dq // 2, d)], "rs_sibling_conv")
    (dh1, dgu0, dff0, act0, sg_f0), recv_c = _bwd_ffn(dh2, h1, ff0, gu0, small, wgu0, wd0, 0, _ChipExchange(parts_c))
    g_gu0, _ = _weight_grad(n0, dgu0, d // 2, fc, "dw_gate_up0")
    parts_gu0, _ = sibling_sum([g_gu0], "rs_sibling_gate_up0")
    g_d0, recv_gu0 = _weight_grad(act0, dff0, DOWN_GRAD_ROWS, d, "dw_down0", _ChipExchange(parts_gu0))
    parts_d0, _ = sibling_sum([g_d0.reshape(N_CHIPS, 2, fq // 2, d)], "rs_sibling_down0")
    (grad_x, dpool, sg_p), recv_d0 = _bwd_pool(dh1, x2, small, pool_scale, poolw, _ChipExchange(parts_d0))
    g_pool = dpool.astype(BF16).reshape(2, ng // 2, N_CHIPS, gw // N_CHIPS, gw).transpose(2, 0, 1, 3, 4).reshape(
        N_CHIPS, 2, ng // 2 * (gw // N_CHIPS), gw)
    zero_row = jnp.zeros((1, d), F32)
    small_g = jnp.concatenate(
        [sg_p[0:2], sg_f0[0:2], sg_c[0:2], sg_f1[0:2], sg_c[2:5], sg_p[2:3]] + [zero_row] * 4, axis=0)
    parts_p, (small_all,) = sibling_sum([g_pool], "rs_sibling_pool", small_g)
    recv_p = _alone(_ChipExchange(parts_p), "rs_chips_pool")

    parts = [parts_gu0[0], parts_f1[0], parts_d0[0], parts_f1[1], parts_c[0], parts_c[1], parts_p[0]]
    received = [recv_gu0[0], recv_f1[0], recv_d0[0], recv_f1[1], recv_c[0], recv_c[1], recv_p[0]]
    halves = [_add_chips(p, r, where_arr) for p, r in zip(parts, received)]
    full = _sibling_share(halves)
    small_sum = _sum_small(small_all)

    gg_gu = jnp.stack([full[0].reshape(d, -1), full[1].reshape(d, -1)])
    gg_d = jnp.stack([full[2].reshape(fq, d), full[3].reshape(fq, d)])
    gg_in = full[4].reshape(1, d, -1)
    gg_out = full[5].reshape(1, dq, d)
    gg_pool = full[6].reshape(1, ng, gw // N_CHIPS, gw)
    mine = lax.dynamic_slice_in_dim(small_sum, chip * dq, dq, axis=1)
    gg_gains = mine[0:8].reshape(2, 4, dq)
    gg_taps = mine[8:11].reshape(1, 3, dq)
    gg_scale = small_sum[11:12]

    grads = [gg_gains, gg_pool, gg_scale, gg_in, gg_taps, gg_out, gg_gu, gg_d]
    weights = [norm_gains, pool_w, pool_scale, conv_in_w, conv_w, conv_out_w, ffn_gate_up_w, ffn_down_w]
    ms = [m_norm_gains, m_pool_w, m_pool_scale, m_conv_in_w, m_conv_w, m_conv_out_w, m_ffn_gate_up_w, m_ffn_down_w]
    vs = [v_norm_gains, v_pool_w, v_pool_scale, v_conv_in_w, v_conv_w, v_conv_out_w, v_ffn_gate_up_w, v_ffn_down_w]
    names = ["gains", "pool_w", "pool_scale", "conv_in", "taps", "conv_out", "gate_up", "down"]
    deltas, new_ms, new_vs = [], [], []
    for w, g, m, v, nm in zip(weights, grads, ms, vs, names):
        flat = (-1, w.shape[-1])
        dl, m2, v2 = _adamw(w.reshape(flat), g.reshape(flat), m.reshape(flat), v.reshape(flat), "adamw_" + nm)
        deltas.append(dl.reshape(w.shape))
        new_ms.append(m2.reshape(w.shape))
        new_vs.append(v2.reshape(w.shape))
    return (loss, grad_x[None], *grads, *deltas, *new_ms, *new_vs)
```

```python
import jax
import jax.numpy as jnp
from jax import lax
from jax.experimental import pallas as pl
from jax.experimental.pallas import tpu as pltpu

RMS_EPS = 1e-6
POOL_WINDOWS = (2, 4, 8, 16)
POOL_HALO = 16
CONV_HALO = 8
N_CHIPS = 4
N_DEV = 8
ADAM_LR = 0.001
ADAM_B1 = 0.9
ADAM_B2 = 0.999
ADAM_EPS = 1e-08
ADAM_WD = 0.01
ADAM_STEP = 10
VMEM_LIMIT = 56 * 2**20
DOWN_GRAD_ROWS = 256
STREAM_BUDGET = 24 * 2**20
MESH = pl.DeviceIdType.MESH
ANY = pl.BlockSpec(memory_space=pl.ANY)
DMA = pltpu.SemaphoreType.DMA
BF16 = jnp.bfloat16
F32 = jnp.float32


def _token_tile(t):
    return min(256, t)


def _rms(x):
    r = lax.rsqrt(jnp.mean(x * x, axis=-1, keepdims=True) + RMS_EPS)
    return x * r, r


def _rms_bwd(dy, xh, r, g):
    a = dy * g
    return r * (a - xh * jnp.mean(a * xh, axis=-1, keepdims=True))


def _dot(a, b):
    return jnp.dot(a, b, preferred_element_type=F32)


def _dot_nt(a, b):
    return lax.dot_general(a, b, (((1,), (1,)), ((), ())), preferred_element_type=F32)


def _dot_tn(a, b):
    return lax.dot_general(a, b, (((0,), (0,)), ((), ())), preferred_element_type=F32)


def _colsum(a):
    return jnp.sum(a, axis=0, keepdims=True)


def _resident(block, index_map):
    return pl.BlockSpec(block, index_map, pipeline_mode=pl.Buffered(1))


def _row_block(r, row_bytes):
    best = None
    for rb in range(16, r + 1, 16):
        if r % rb == 0 and rb * row_bytes <= STREAM_BUDGET:
            best = rb
    return best if best is not None else r


def _place():
    x, y, c = lax.axis_index("x"), lax.axis_index("y"), lax.axis_index("c")
    return x, y, c, 2 * x + y


def _dev(chip, core):
    return (chip // 2, chip % 2, core)


def _remote(src, dst, send_sem, recv_sem, device):
    return pltpu.make_async_remote_copy(src_ref=src, dst_ref=dst, send_sem=send_sem, recv_sem=recv_sem,
                                        device_id=device, device_id_type=MESH)


class _Gather:
    def __init__(self, shards):
        n = len(shards)
        self.args = [s for s, _ in shards]
        self.layers = [l for _, l in shards]
        self.out_shape = [jax.ShapeDtypeStruct((N_CHIPS,) + s.shape[1:], s.dtype) for s in self.args]
        self.sems = [DMA((n,)), DMA((n,)), DMA((n, 3)), DMA((n, 3)), DMA((n, 3)), DMA((n, 3))]

    def _own(self, loc, out, sems, a):
        x, y, c, k = _place()
        return _remote(loc[a].at[self.layers[a]], out[a].at[k], sems[0].at[a], sems[1].at[a], (x, y, 1 - c))

    def _ici(self, loc, out, sems, a, m, arrival):
        x, y, c, k = _place()
        dst = out[a].at[k ^ m, c] if arrival else out[a].at[k, c]
        return _remote(loc[a].at[self.layers[a], c], dst, sems[2].at[a, m - 1], sems[3].at[a, m - 1], _dev(k ^ m, c))

    def _forward(self, out, sems, a, m, arrival):
        x, y, c, k = _place()
        got = out[a].at[k ^ m, 1 - c] if arrival else out[a].at[k ^ m, c]
        return _remote(got, got, sems[4].at[a, m - 1], sems[5].at[a, m - 1], (x, y, 1 - c))

    def start(self, loc, out, sems):
        for a in range(len(self.args)):
            for m in range(1, N_CHIPS):
                self._ici(loc, out, sems, a, m, False).start()
            self._own(loc, out, sems, a).start()

    def finish(self, loc, out, sems):
        n = len(self.args)
        for a in range(n):
            for m in range(1, N_CHIPS):
                self._ici(loc, out, sems, a, m, True).wait_recv()
                self._forward(out, sems, a, m, False).start()
        for a in range(n):
            for m in range(1, N_CHIPS):
                self._forward(out, sems, a, m, True).wait_recv()
            self._own(loc, out, sems, a).wait_recv()
        for a in range(n):
            for m in range(1, N_CHIPS):
                self._ici(loc, out, sems, a, m, False).wait_send()
                self._forward(out, sems, a, m, False).wait_send()
            self._own(loc, out, sems, a).wait_send()


class _ChipExchange:
    def __init__(self, parts):
        n = len(parts)
        self.args = list(parts)
        self.out_shape = [jax.ShapeDtypeStruct((N_CHIPS - 1,) + p.shape[1:], p.dtype) for p in parts]
        self.sems = [DMA((n, 3)), DMA((n, 3))]

    def _copy(self, p, land, sems, a, m):
        x, y, c, k = _place()
        return _remote(p[a].at[k ^ m], land[a].at[m - 1], sems[0].at[a, m - 1], sems[1].at[a, m - 1], _dev(k ^ m, c))

    def start(self, p, land, sems):
        for a in range(len(self.args)):
            for m in range(1, N_CHIPS):
                self._copy(p, land, sems, a, m).start()

    def finish(self, p, land, sems):
        for a in range(len(self.args)):
            for m in range(1, N_CHIPS):
                self._copy(p, land, sems, a, m).wait_recv()
        for a in range(len(self.args)):
            for m in range(1, N_CHIPS):
                self._copy(p, land, sems, a, m).wait_send()


def _hosted(body, comm, *, name, grid, in_specs, out_specs, out_shape, args, scratch_shapes=()):
    ni, no, ns = len(in_specs), len(out_shape), len(scratch_shapes)
    if comm is None:
        res = pl.pallas_call(
            body, name=name, grid=grid, in_specs=list(in_specs), out_specs=list(out_specs), out_shape=list(out_shape),
            scratch_shapes=list(scratch_shapes),
            compiler_params=pltpu.CompilerParams(dimension_semantics=("arbitrary",) * len(grid), vmem_limit_bytes=VMEM_LIMIT),
        )(*args)
        return list(res), []
    nc, nco = len(comm.args), len(comm.out_shape)

    def full(*refs):
        cin = refs[ni:ni + nc]
        outs = refs[ni + nc:ni + nc + no]
        cout = refs[ni + nc + no:ni + nc + no + nco]
        scratch = refs[ni + nc + no + nco:ni + nc + no + nco + ns]
        csems = refs[ni + nc + no + nco + ns:]
        first = _all_of([pl.program_id(ax) == 0 for ax in range(len(grid))])
        last = _all_of([pl.program_id(ax) == grid[ax] - 1 for ax in range(len(grid))])

        @pl.when(first)
        def _():
            comm.start(cin, cout, csems)

        body(*refs[:ni], *outs, *scratch)

        @pl.when(last)
        def _():
            comm.finish(cin, cout, csems)

    res = pl.pallas_call(
        full, name=name, grid=grid, in_specs=[*in_specs, *[ANY] * nc], out_specs=[*out_specs, *[ANY] * nco],
        out_shape=[*out_shape, *comm.out_shape], scratch_shapes=[*scratch_shapes, *comm.sems],
        compiler_params=pltpu.CompilerParams(dimension_semantics=("arbitrary",) * len(grid), vmem_limit_bytes=VMEM_LIMIT,
                                             has_side_effects=True),
    )(*args, *comm.args)
    return list(res[:no]), list(res[no:])


def _all_of(conds):
    out = conds[0]
    for c in conds[1:]:
        out = jnp.logical_and(out, c)
    return out


def _alone(comm, name):
    return _hosted(lambda: None, comm, name=name, grid=(1,), in_specs=[], out_specs=[], out_shape=[], args=[])[1]


def _sibling_exchange(grads, small, name):
    n = len(grads)
    ns = 0 if small is None else 1

    def body(*refs):
        g = refs[:n]
        land = refs[n + ns:2 * n + ns]
        send_sems, recv_sems, own_sem, ssend_sems, srecv_sems = refs[2 * n + 2 * ns:]
        x, y, c, k = _place()
        me = 2 * k + c
        cps = []
        for a in range(n):
            cp = _remote(g[a].at[:, pl.ds(1 - c, 1)], land[a], send_sems.at[a], recv_sems.at[a], (x, y, 1 - c))
            cp.start()
            cps.append(cp)
        if small is not None:
            sm, smg = refs[n], refs[2 * n + 1]
            peers = [me ^ m for m in range(1, N_DEV)]
            ids = [(p // 4, (p // 2) % 2, p % 2) for p in peers]
            own = pltpu.make_async_copy(sm, smg.at[me], own_sem)
            own.start()
            for m in range(1, N_DEV):
                cp = _remote(sm, smg.at[me], ssend_sems.at[m - 1], srecv_sems.at[m - 1], ids[m - 1])
                cp.start()
                cps.append(cp)
            for m in range(1, N_DEV):
                _remote(sm, smg.at[peers[m - 1]], ssend_sems.at[m - 1], srecv_sems.at[m - 1], ids[m - 1]).wait_recv()
            own.wait()
        for cp in cps[:n]:
            cp.wait_recv()
        for cp in cps:
            cp.wait_send()

    out_shape = [jax.ShapeDtypeStruct((N_CHIPS, 1) + a.shape[2:], a.dtype) for a in grads]
    ins = list(grads)
    if small is not None:
        out_shape.append(jax.ShapeDtypeStruct((N_DEV,) + small.shape, small.dtype))
        ins.append(small)
    return pl.pallas_call(
        body, name=name, out_shape=out_shape, in_specs=[ANY] * (n + ns), out_specs=[ANY] * (n + ns),
        scratch_shapes=[DMA((n,)), DMA((n,)), DMA, DMA((N_DEV - 1,)), DMA((N_DEV - 1,))],
        compiler_params=pltpu.CompilerParams(has_side_effects=True),
    )(*ins)


def _sibling_share(halves):
    n = len(halves)

    def body(*refs):
        out = refs[n:2 * n]
        send_sems, recv_sems = refs[2 * n:]
        x, y, c, k = _place()
        cps = []
        for a in range(n):
            mine = out[a].at[:, pl.ds(c, 1)]
            cp = _remote(mine, mine, send_sems.at[a], recv_sems.at[a], (x, y, 1 - c))
            cp.start()
            cps.append(cp)
        for a in range(n):
            theirs = out[a].at[:, pl.ds(1 - c, 1)]
            _remote(theirs, theirs, send_sems.at[a], recv_sems.at[a], (x, y, 1 - c)).wait_recv()
        for cp in cps:
            cp.wait_send()

    out_shape = [jax.ShapeDtypeStruct(a.shape, a.dtype) for a in halves]
    return pl.pallas_call(
        body, name="rs_sibling_share", out_shape=out_shape, in_specs=[ANY] * n, out_specs=[ANY] * n,
        input_output_aliases={a: a for a in range(n)}, scratch_shapes=[DMA((n,)), DMA((n,))],
        compiler_params=pltpu.CompilerParams(has_side_effects=True),
    )(*halves)


def _add_sibling(g, land, core):
    _, _, r, c = g.shape
    rb = _row_block(r, c * (3 * 2 * 2 + 2 * 4))

    def body(core_ref, g_ref, l_ref, o_ref):
        o_ref[...] = (g_ref[...].astype(F32) + l_ref[...].astype(F32)).astype(o_ref.dtype)

    return pl.pallas_call(
        body, name="rs_add_sibling", out_shape=jax.ShapeDtypeStruct((N_CHIPS, r, c), g.dtype),
        grid_spec=pltpu.PrefetchScalarGridSpec(
            num_scalar_prefetch=1, grid=(N_CHIPS, r // rb),
            in_specs=[pl.BlockSpec((None, None, rb, c), lambda j, i, core_ref: (j, core_ref[0], i, 0)),
                      pl.BlockSpec((None, None, rb, c), lambda j, i, core_ref: (j, 0, i, 0))],
            out_specs=pl.BlockSpec((None, rb, c), lambda j, i, core_ref: (j, i, 0))),
        compiler_params=pltpu.CompilerParams(dimension_semantics=("parallel", "parallel"), vmem_limit_bytes=VMEM_LIMIT),
    )(core, g, land)


def _add_chips(part, land, where, layer=0, n_layers=1, into=None):
    _, r, c = part.shape
    rb = _row_block(r, c * (4 * 2 * 2 + 4 * 2 + 2 * 4))

    def body(where_ref, p_ref, l_ref, *rest):
        acc = p_ref[...].astype(F32)
        for m in range(N_CHIPS - 1):
            acc = acc + l_ref[m].astype(F32)
        rest[-1][...] = acc

    in_specs = [pl.BlockSpec((None, rb, c), lambda i, where_ref: (where_ref[0], i, 0)),
                pl.BlockSpec((N_CHIPS - 1, rb, c), lambda i, where_ref: (0, i, 0))]
    args = [where, part, land]
    if into is not None:
        in_specs.append(ANY)
        args.append(into)
    return pl.pallas_call(
        body, name="rs_add_chips", out_shape=jax.ShapeDtypeStruct((n_layers, 2, r, c), F32),
        grid_spec=pltpu.PrefetchScalarGridSpec(
            num_scalar_prefetch=1, grid=(r // rb,), in_specs=in_specs,
            out_specs=pl.BlockSpec((None, None, rb, c), lambda i, where_ref: (layer, where_ref[1], i, 0))),
        input_output_aliases={} if into is None else {3: 0},
        compiler_params=pltpu.CompilerParams(dimension_semantics=("parallel",), vmem_limit_bytes=VMEM_LIMIT),
    )(*args)


def _sum_small(smg):
    def body(s_ref, o_ref):
        acc = s_ref[0]
        for j in range(1, N_DEV):
            acc = acc + s_ref[j]
        o_ref[...] = acc

    return pl.pallas_call(body, name="rs_sum_small", out_shape=jax.ShapeDtypeStruct(smg.shape[1:], F32))(smg)


def _pool_windows(ext_ref, g, gw, tm, first_row):
    w = POOL_WINDOWS[g]
    slab = ext_ref[:, g * gw:(g + 1) * gw]
    p, k = slab, 1
    while k < w:
        p = p + pltpu.roll(p, k, 0)
        k *= 2
    t = first_row + lax.broadcasted_iota(jnp.int32, (tm, 1), 0)
    cnt = jnp.minimum(t + 1, w).astype(F32)
    return p[POOL_HALO:] / cnt - slab[POOL_HALO:]


def _fwd_pool(x, small, scale, poolw, comm=None):
    t, d = x.shape
    tm = _token_tile(t)
    gw = d // len(POOL_WINDOWS)

    def body(x_ref, sm_ref, sc_ref, w_ref, h_ref, ext_ref, mix_ref):
        i = pl.program_id(0)

        @pl.when(i == 0)
        def _():
            ext_ref[0:POOL_HALO, :] = jnp.zeros((POOL_HALO, d), F32)

        @pl.when(i > 0)
        def _():
            ext_ref[0:POOL_HALO, :] = ext_ref[tm:tm + POOL_HALO, :]

        xv = x_ref[...]
        xh, _ = _rms(xv)
        ext_ref[POOL_HALO:, :] = xh * sm_ref[0:1, :]
        for g in range(len(POOL_WINDOWS)):
            pooled = _pool_windows(ext_ref, g, gw, tm, i * tm)
            cols = slice(g * gw, (g + 1) * gw)
            mix_ref[:, cols] = _dot(pooled.astype(BF16), w_ref[g]) * sc_ref[:, cols]
        mh, _ = _rms(mix_ref[...])
        h_ref[...] = xv + mh * sm_ref[1:2, :]

    (h,), got = _hosted(
        body, comm, name="fwd_pool", grid=(t // tm,), out_shape=[jax.ShapeDtypeStruct((t, d), F32)],
        in_specs=[pl.BlockSpec((tm, d), lambda i: (i, 0)), _resident(small.shape, lambda i: (0, 0)),
                  _resident(scale.shape, lambda i: (0, 0)), _resident(poolw.shape, lambda i: (0, 0, 0))],
        out_specs=[pl.BlockSpec((tm, d), lambda i: (i, 0))],
        scratch_shapes=[pltpu.VMEM((POOL_HALO + tm, d), F32), pltpu.VMEM((tm, d), F32)],
        args=[x, small, scale, poolw])
    return h, got


def _fwd_ffn(h, small, wgu, wd, layer, comm=None, target=None):
    t, d = h.shape
    tm = _token_tile(t)
    steps = t // tm
    fc = wgu.shape[-1]
    f = 2 * fc
    g_in, g_out = 4 * layer + 2, 4 * layer + 3
    with_loss = target is not None

    def body(h_ref, *refs):
        if with_loss:
            t_ref, sm_ref, wgu_ref, wd_ref, o_ref, gu_ref, ff_ref, n_ref, l_ref, acc_ref = refs
        else:
            sm_ref, wgu_ref, wd_ref, o_ref, gu_ref, ff_ref, n_ref = refs
        hv = h_ref[...]
        hh, _ = _rms(hv)
        n = (hh * sm_ref[g_in:g_in + 1, :]).astype(BF16)
        n_ref[...] = n
        ff = None
        for j in range(2):
            gate = _dot(n, wgu_ref[j])
            up = _dot(n, wgu_ref[2 + j])
            gu_ref[:, j * fc:(j + 1) * fc] = gate.astype(BF16)
            gu_ref[:, f + j * fc:f + (j + 1) * fc] = up.astype(BF16)
            act = (gate * jax.nn.sigmoid(gate) * up).astype(BF16)
            part = _dot(act, wd_ref[j * fc:(j + 1) * fc, :])
            ff = part if ff is None else ff + part
        ff_ref[...] = ff
        fh, _ = _rms(ff)
        out = hv + fh * sm_ref[g_out:g_out + 1, :]
        if not with_loss:
            o_ref[...] = out
            return
        i = pl.program_id(0)
        e = out - t_ref[...]
        o_ref[...] = e * (1.0 / d)

        @pl.when(i == 0)
        def _():
            acc_ref[...] = jnp.zeros_like(acc_ref)

        acc_ref[...] += _colsum(e * e)

        @pl.when(i == steps - 1)
        def _():
            l_ref[...] = jnp.full(l_ref.shape, 0.5 / d, F32) * jnp.sum(acc_ref[...])

    row = lambda i: (i, 0)
    out_shape = [jax.ShapeDtypeStruct((t, d), F32), jax.ShapeDtypeStruct((t, 2 * f), BF16),
                 jax.ShapeDtypeStruct((t, d), F32), jax.ShapeDtypeStruct((t, d), BF16)]
    out_specs = [pl.BlockSpec((tm, d), row), pl.BlockSpec((tm, 2 * f), row), pl.BlockSpec((tm, d), row),
                 pl.BlockSpec((tm, d), row)]
    weight_specs = [_resident(small.shape, lambda i: (0, 0)), _resident(wgu.shape, lambda i: (0, 0, 0)),
                    _resident(wd.shape, lambda i: (0, 0))]
    if with_loss:
        return _hosted(
            body, comm, name=f"fwd_ffn{layer}_loss", grid=(steps,),
            out_shape=out_shape + [jax.ShapeDtypeStruct((8, 128), F32)],
            in_specs=[pl.BlockSpec((tm, d), row), pl.BlockSpec((tm, d), row)] + weight_specs,
            out_specs=out_specs + [pl.BlockSpec((8, 128), lambda i: (0, 0))],
            scratch_shapes=[pltpu.VMEM((1, d), F32)], args=[h, target, small, wgu, wd])
    return _hosted(
        body, comm, name=f"fwd_ffn{layer}", grid=(steps,), out_shape=out_shape,
        in_specs=[pl.BlockSpec((tm, d), row)] + weight_specs, out_specs=out_specs, args=[h, small, wgu, wd])


def _fwd_conv(h, small, win, wout, comm=None):
    t, d = h.shape
    tm = _token_tile(t)
    pc = win.shape[-1]

    def body(h_ref, sm_ref, win_ref, wout_ref, o_ref, proj_ref, y_ref, n_ref, pj_ref, uext_ref):
        i = pl.program_id(0)

        @pl.when(i == 0)
        def _():
            uext_ref[0:CONV_HALO, :] = jnp.zeros((CONV_HALO, d), F32)

        @pl.when(i > 0)
        def _():
            uext_ref[0:CONV_HALO, :] = uext_ref[tm:tm + CONV_HALO, :]

        hv = h_ref[...]
        hh, _ = _rms(hv)
        n = (hh * sm_ref[4:5, :]).astype(BF16)
        n_ref[...] = n
        for k in range(N_CHIPS):
            pj_ref[:, k * pc:(k + 1) * pc] = _dot(n, win_ref[k])
        proj_ref[...] = pj_ref[...].astype(BF16)
        uext_ref[CONV_HALO:, :] = pj_ref[:, d:2 * d] * pj_ref[:, 2 * d:]
        taps = [sm_ref[8 + j:9 + j, :] for j in range(3)]
        full = uext_ref[...]
        conv = (full[CONV_HALO:] * taps[2] + pltpu.roll(full, 1, 0)[CONV_HALO:] * taps[1]
                + pltpu.roll(full, 2, 0)[CONV_HALO:] * taps[0])
        y = _dot((pj_ref[:, 0:d] * conv).astype(BF16), wout_ref[...])
        y_ref[...] = y
        yh, _ = _rms(y)
        o_ref[...] = hv + yh * sm_ref[5:6, :]

    row = lambda i: (i, 0)
    return _hosted(
        body, comm, name="fwd_conv", grid=(t // tm,),
        out_shape=[jax.ShapeDtypeStruct((t, d), F32), jax.ShapeDtypeStruct((t, 3 * d), BF16),
                   jax.ShapeDtypeStruct((t, d), F32), jax.ShapeDtypeStruct((t, d), BF16)],
        in_specs=[pl.BlockSpec((tm, d), row), _resident(small.shape, lambda i: (0, 0)),
                  _resident(win.shape, lambda i: (0, 0, 0)), _resident(wout.shape, lambda i: (0, 0))],
        out_specs=[pl.BlockSpec((tm, d), row), pl.BlockSpec((tm, 3 * d), row), pl.BlockSpec((tm, d), row),
                   pl.BlockSpec((tm, d), row)],
        scratch_shapes=[pltpu.VMEM((tm, 3 * d), F32), pltpu.VMEM((CONV_HALO + tm, d), F32)],
        args=[h, small, win, wout])


def _bwd_ffn(dh, h, ff, gu, small, wgu, wd, layer, comm=None):
    t, d = h.shape
    tm = _token_tile(t)
    fc = wgu.shape[-1]
    f = 2 * fc
    g_in, g_out = 4 * layer + 2, 4 * layer + 3

    def body(dh_ref, h_ref, ff_ref, gu_ref, sm_ref, wgu_ref, wd_ref, o_ref, dgu_ref, dff_ref, act_ref, sg_ref):
        i = pl.program_id(0)

        @pl.when(i == 0)
        def _():
            sg_ref[...] = jnp.zeros_like(sg_ref)

        dy = dh_ref[...]
        fh, r3 = _rms(ff_ref[...])
        sg_ref[1:2, :] += _colsum(dy * fh)
        dff = _rms_bwd(dy, fh, r3, sm_ref[g_out:g_out + 1, :]).astype(BF16)
        dff_ref[...] = dff
        for j in range(2):
            dact = _dot_nt(dff, wd_ref[j * fc:(j + 1) * fc, :])
            gate = gu_ref[:, j * fc:(j + 1) * fc].astype(F32)
            up = gu_ref[:, f + j * fc:f + (j + 1) * fc].astype(F32)
            sig = jax.nn.sigmoid(gate)
            silu = gate * sig
            act_ref[:, j * fc:(j + 1) * fc] = (silu * up).astype(BF16)
            dgu_ref[:, j * fc:(j + 1) * fc] = (dact * up * (sig * (1.0 + gate * (1.0 - sig)))).astype(BF16)
            dgu_ref[:, f + j * fc:f + (j + 1) * fc] = (dact * silu).astype(BF16)
        dn = None
        for k in range(N_CHIPS):
            part = _dot_nt(dgu_ref[:, k * fc:(k + 1) * fc], wgu_ref[k])
            dn = part if dn is None else dn + part
        hh, r2 = _rms(h_ref[...])
        sg_ref[0:1, :] += _colsum(dn * hh)
        o_ref[...] = dy + _rms_bwd(dn, hh, r2, sm_ref[g_in:g_in + 1, :])

    row = lambda i: (i, 0)
    return _hosted(
        body, comm, name=f"bwd_ffn{layer}", grid=(t // tm,),
        out_shape=[jax.ShapeDtypeStruct((t, d), F32), jax.ShapeDtypeStruct((t, 2 * f), BF16),
                   jax.ShapeDtypeStruct((t, d), BF16), jax.ShapeDtypeStruct((t, f), BF16),
                   jax.ShapeDtypeStruct((8, d), F32)],
        in_specs=[pl.BlockSpec((tm, d), row), pl.BlockSpec((tm, d), row), pl.BlockSpec((tm, d), row),
                  pl.BlockSpec((tm, 2 * f), row), _resident(small.shape, lambda i: (0, 0)),
                  _resident(wgu.shape, lambda i: (0, 0, 0)), _resident(wd.shape, lambda i: (0, 0))],
        out_specs=[pl.BlockSpec((tm, d), row), pl.BlockSpec((tm, 2 * f), row), pl.BlockSpec((tm, d), row),
                   pl.BlockSpec((tm, f), row), pl.BlockSpec((8, d), lambda i: (0, 0))],
        args=[dh, h, ff, gu, small, wgu, wd])


def _bwd_conv(dh, h, y, proj, small, win, wout, comm=None):
    t, d = h.shape
    tm = _token_tile(t)
    steps = t // tm
    pc = win.shape[-1]
    halo_blocks = tm // 16

    def body(dh_ref, h_ref, y_ref, proj_ref, halo_ref, sm_ref, win_ref, wout_ref,
             o_ref, dproj_ref, dy_ref, bc_ref, sg_ref, uext_ref, dcext_ref, carry_ref):
        i = pl.program_id(0)
        tile = steps - 1 - i

        @pl.when(i == 0)
        def _():
            sg_ref[...] = jnp.zeros_like(sg_ref)
            carry_ref[...] = jnp.zeros_like(carry_ref)

        dy = dh_ref[...]
        yh, r1 = _rms(y_ref[...])
        sg_ref[1:2, :] += _colsum(dy * yh)
        dyv = _rms_bwd(dy, yh, r1, sm_ref[5:6, :]).astype(BF16)
        dy_ref[...] = dyv
        dbc = _dot_nt(dyv, wout_ref[...])
        b = proj_ref[:, 0:d].astype(F32)
        cg = proj_ref[:, d:2 * d].astype(F32)
        v = proj_ref[:, 2 * d:].astype(F32)
        halo = halo_ref[...].astype(F32)[16 - CONV_HALO:]
        uh = halo[:, d:2 * d] * halo[:, 2 * d:]
        uext_ref[0:CONV_HALO, :] = jnp.where(tile > 0, uh, jnp.zeros_like(uh))
        uext_ref[CONV_HALO:, :] = cg * v
        taps = [sm_ref[8 + j:9 + j, :] for j in range(3)]
        full = uext_ref[...]
        u0 = full[CONV_HALO:]
        u1 = pltpu.roll(full, 1, 0)[CONV_HALO:]
        u2 = pltpu.roll(full, 2, 0)[CONV_HALO:]
        conv = u0 * taps[2] + u1 * taps[1] + u2 * taps[0]
        bc_ref[...] = (b * conv).astype(BF16)
        dconv = dbc * b
        sg_ref[4:5, :] += _colsum(dconv * u0)
        sg_ref[3:4, :] += _colsum(dconv * u1)
        sg_ref[2:3, :] += _colsum(dconv * u2)
        dcext_ref[0:tm, :] = dconv
        dcext_ref[tm:, :] = carry_ref[...]
        carry_ref[...] = dconv[0:CONV_HALO]
        dfull = dcext_ref[...]
        n8 = tm + CONV_HALO
        du = (dfull[0:tm] * taps[2] + pltpu.roll(dfull, n8 - 1, 0)[0:tm] * taps[1]
              + pltpu.roll(dfull, n8 - 2, 0)[0:tm] * taps[0])
        dproj_ref[:, 0:d] = (dbc * conv).astype(BF16)
        dproj_ref[:, d:2 * d] = (du * v).astype(BF16)
        dproj_ref[:, 2 * d:] = (du * cg).astype(BF16)
        dn = None
        for k in range(N_CHIPS):
            part = _dot_nt(dproj_ref[:, k * pc:(k + 1) * pc], win_ref[k])
            dn = part if dn is None else dn + part
        hh, r0 = _rms(h_ref[...])
        sg_ref[0:1, :] += _colsum(dn * hh)
        o_ref[...] = dy + _rms_bwd(dn, hh, r0, sm_ref[4:5, :])

    rev = lambda i: (steps - 1 - i, 0)
    before = lambda i: (jnp.maximum((steps - 1 - i) * halo_blocks - 1, 0), 0)
    return _hosted(
        body, comm, name="bwd_conv", grid=(steps,),
        out_shape=[jax.ShapeDtypeStruct((t, d), F32), jax.ShapeDtypeStruct((t, 3 * d), BF16),
                   jax.ShapeDtypeStruct((t, d), BF16), jax.ShapeDtypeStruct((t, d), BF16),
                   jax.ShapeDtypeStruct((8, d), F32)],
        in_specs=[pl.BlockSpec((tm, d), rev), pl.BlockSpec((tm, d), rev), pl.BlockSpec((tm, d), rev),
                  pl.BlockSpec((tm, 3 * d), rev), pl.BlockSpec((16, 3 * d), before),
                  _resident(small.shape, lambda i: (0, 0)), _resident(win.shape, lambda i: (0, 0, 0)),
                  _resident(wout.shape, lambda i: (0, 0))],
        out_specs=[pl.BlockSpec((tm, d), rev), pl.BlockSpec((tm, 3 * d), rev), pl.BlockSpec((tm, d), rev),
                   pl.BlockSpec((tm, d), rev), pl.BlockSpec((8, d), lambda i: (0, 0))],
        scratch_shapes=[pltpu.VMEM((CONV_HALO + tm, d), F32), pltpu.VMEM((tm + CONV_HALO, d), F32),
                        pltpu.VMEM((CONV_HALO, d), F32)],
        args=[dh, h, y, proj, proj, small, win, wout])


def _bwd_pool(dh, x, small, scale, poolw, comm=None):
    t, d = x.shape
    tm = _token_tile(t)
    steps = t // tm
    ng = len(POOL_WINDOWS)
    gw = d // ng
    halo_blocks = tm // POOL_HALO

    def body(dh_ref, x_ref, halo_ref, sm_ref, sc_ref, w_ref, o_ref, dw_ref, sg_ref,
             ext_ref, mix_ref, mm_ref, pb_ref, qext_ref, dhn_ref, carry_ref):
        i = pl.program_id(0)
        tile = steps - 1 - i

        @pl.when(i == 0)
        def _():
            sg_ref[...] = jnp.zeros_like(sg_ref)
            dw_ref[...] = jnp.zeros_like(dw_ref)
            carry_ref[...] = jnp.zeros_like(carry_ref)

        g0 = sm_ref[0:1, :]
        xv = x_ref[...]
        xh, r0 = _rms(xv)
        hx, _ = _rms(halo_ref[...])
        ext_ref[0:POOL_HALO, :] = jnp.where(tile > 0, hx * g0, jnp.zeros_like(hx))
        ext_ref[POOL_HALO:, :] = xh * g0
        for g in range(ng):
            pooled = _pool_windows(ext_ref, g, gw, tm, tile * tm)
            cols = slice(g * gw, (g + 1) * gw)
            pb = pooled.astype(BF16)
            pb_ref[:, cols] = pb
            mm = _dot(pb, w_ref[g])
            mm_ref[:, cols] = mm
            mix_ref[:, cols] = mm * sc_ref[:, cols]
        dy = dh_ref[...]
        mh, r1 = _rms(mix_ref[...])
        sg_ref[1:2, :] += _colsum(dy * mh)
        dmix = _rms_bwd(dy, mh, r1, sm_ref[1:2, :])
        sg_ref[2:3, :] += _colsum(dmix * mm_ref[...])
        mix_ref[...] = dmix * sc_ref[...]
        n16 = tm + POOL_HALO
        for g in range(ng):
            w = POOL_WINDOWS[g]
            cols = slice(g * gw, (g + 1) * gw)
            dmm = mix_ref[:, cols].astype(BF16)
            dpooled = _dot_nt(dmm, w_ref[g])
            dw_ref[g] += _dot_tn(pb_ref[:, cols], dmm)
            trow = tile * tm + lax.broadcasted_iota(jnp.int32, (tm, 1), 0)
            q = dpooled / jnp.minimum(trow + 1, w).astype(F32)
            qext_ref[0:tm, cols] = q
            qext_ref[tm:, cols] = carry_ref[:, cols]
            carry_ref[:, cols] = q[0:POOL_HALO]
            p, k = qext_ref[:, cols], 1
            while k < w:
                p = p + pltpu.roll(p, n16 - k, 0)
                k *= 2
            dhn_ref[:, cols] = p[0:tm] - dpooled
        dhn = dhn_ref[...]
        sg_ref[0:1, :] += _colsum(dhn * xh)
        o_ref[...] = dy + _rms_bwd(dhn, xh, r0, g0)

    rev = lambda i: (steps - 1 - i, 0)
    before = lambda i: (jnp.maximum((steps - 1 - i) * halo_blocks - 1, 0), 0)
    return _hosted(
        body, comm, name="bwd_pool", grid=(steps,),
        out_shape=[jax.ShapeDtypeStruct((t, d), F32), jax.ShapeDtypeStruct((ng, gw, gw), F32),
                   jax.ShapeDtypeStruct((8, d), F32)],
        in_specs=[pl.BlockSpec((tm, d), rev), pl.BlockSpec((tm, d), rev), pl.BlockSpec((POOL_HALO, d), before),
                  _resident(small.shape, lambda i: (0, 0)), _resident(scale.shape, lambda i: (0, 0)),
                  _resident(poolw.shape, lambda i: (0, 0, 0))],
        out_specs=[pl.BlockSpec((tm, d), rev), pl.BlockSpec((ng, gw, gw), lambda i: (0, 0, 0)),
                   pl.BlockSpec((8, d), lambda i: (0, 0))],
        scratch_shapes=[pltpu.VMEM((POOL_HALO + tm, d), F32), pltpu.VMEM((tm, d), F32), pltpu.VMEM((tm, d), F32),
                        pltpu.VMEM((tm, d), BF16), pltpu.VMEM((tm + POOL_HALO, d), F32), pltpu.VMEM((tm, d), F32),
                        pltpu.VMEM((POOL_HALO, d), F32)],
        args=[dh, x, x, small, scale, poolw])


def _weight_grad(a, b, bm, bn, name, comm=None):
    t, m = a.shape
    _, n = b.shape

    def body(a_ref, b_ref, o_ref):
        o_ref[...] = _dot_tn(a_ref[...], b_ref[...]).astype(o_ref.dtype)

    (out,), got = _hosted(
        body, comm, name=name, grid=(n // bn, m // bm),
        out_shape=[jax.ShapeDtypeStruct((n // bn, m // bm, bm, bn), BF16)],
        in_specs=[pl.BlockSpec((t, bm), lambda j, i: (0, i)), pl.BlockSpec((t, bn), lambda j, i: (0, j))],
        out_specs=[pl.BlockSpec((None, None, bm, bn), lambda j, i: (j, i, 0, 0))],
        args=[a, b])
    return out, got


def _adamw(w, g, m, v, name):
    r, c = w.shape
    rb = _row_block(r, c * (7 * 4 * 2 + 4 * 4))
    bc1 = 1.0 - ADAM_B1 ** ADAM_STEP
    bc2 = 1.0 - ADAM_B2 ** ADAM_STEP

    def body(w_ref, g_ref, m_ref, v_ref, d_ref, nm_ref, nv_ref):
        gv = g_ref[...]
        nm = ADAM_B1 * m_ref[...] + (1.0 - ADAM_B1) * gv
        nv = ADAM_B2 * v_ref[...] + (1.0 - ADAM_B2) * (gv * gv)
        nm_ref[...] = nm
        nv_ref[...] = nv
        d_ref[...] = -ADAM_LR * ((nm / bc1) / (jnp.sqrt(nv / bc2) + ADAM_EPS) + ADAM_WD * w_ref[...])

    spec = pl.BlockSpec((rb, c), lambda i: (i, 0))
    return pl.pallas_call(
        body, name=name, grid=(r // rb,), out_shape=[jax.ShapeDtypeStruct((r, c), F32)] * 3,
        in_specs=[spec] * 4, out_specs=[spec] * 3,
        compiler_params=pltpu.CompilerParams(dimension_semantics=("parallel",), vmem_limit_bytes=VMEM_LIMIT),
    )(w, g, m, v)


def kernel(x, norm_gains, pool_w, pool_scale, conv_in_w, conv_w, conv_out_w, ffn_gate_up_w, ffn_down_w, loss_target, m_norm_gains, m_pool_w, m_pool_scale, m_conv_in_w, m_conv_w, m_conv_out_w, m_ffn_gate_up_w, m_ffn_down_w, v_norm_gains, v_pool_w, v_pool_scale, v_conv_in_w, v_conv_w, v_conv_out_w, v_ffn_gate_up_w, v_ffn_down_w):
    _, t, d = x.shape
    dq = d // N_CHIPS
    ng = len(POOL_WINDOWS)
    gw = d // ng
    fq = ffn_down_w.shape[1]
    f = N_CHIPS * fq
    fc = f // 2
    core = lax.axis_index("c")
    chip = 2 * lax.axis_index("x") + lax.axis_index("y")
    core_arr = jnp.reshape(core, (1,)).astype(jnp.int32)
    where_arr = jnp.stack([chip, core]).astype(jnp.int32)
    x2, target = x[0], loss_target[0]

    small_loc = jnp.concatenate(
        [norm_gains.reshape(8, dq), conv_w[0], jnp.zeros((5, dq), F32)], axis=0).reshape(1, 2, 8, dq)
    pool_loc = pool_w.astype(BF16).reshape(1, 2, ng // 2 * (gw // N_CHIPS), gw)
    wgu_loc = ffn_gate_up_w.astype(BF16).reshape(2, 2, d // 2, fc)
    wd_loc = ffn_down_w.astype(BF16).reshape(2, 2, fq // 2, d)
    win_loc = conv_in_w.astype(BF16).reshape(1, 2, d // 2, -1)
    wout_loc = conv_out_w.astype(BF16).reshape(1, 2, dq // 2, d)

    def ffn_weights(wgu_f, wd_f):
        return wgu_f.reshape(N_CHIPS, d, fc), wd_f.reshape(f, d)

    pool_f, small_f = _alone(_Gather([(pool_loc, 0), (small_loc, 0)]), "ag_first")
    poolw = pool_f.reshape(N_CHIPS, ng, gw // N_CHIPS, gw).transpose(1, 0, 2, 3).reshape(ng, gw, gw)
    small = small_f.transpose(1, 2, 0, 3).reshape(16, d)
    h1, got = _fwd_pool(x2, small, pool_scale, poolw, _Gather([(wgu_loc, 0), (wd_loc, 0)]))
    wgu0, wd0 = ffn_weights(*got)
    (h2, gu0, ff0, n0), got = _fwd_ffn(h1, small, wgu0, wd0, 0, _Gather([(win_loc, 0), (wout_loc, 0), (wgu_loc, 1)]))
    win_f, wout_f = got[0].reshape(N_CHIPS, d, -1), got[1].reshape(d, d)
    (h3, proj, y, nc), got_d = _fwd_conv(h2, small, win_f, wout_f, _Gather([(wd_loc, 1)]))
    wgu1, wd1 = ffn_weights(got[2], got_d[0])
    (dh4, gu1, ff1, n1, loss_blk), _ = _fwd_ffn(h3, small, wgu1, wd1, 1, target=target)

    def sibling_sum(pieces, name, small_g=None):
        landed = _sibling_exchange(pieces, small_g, name)
        return [_add_sibling(g, l, core_arr) for g, l in zip(pieces, landed)], landed[len(pieces):]

    (dh3, dgu1, dff1, act1, sg_f1), _ = _bwd_ffn(dh4, h3, ff1, gu1, small, wgu1, wd1, 1)
    g_d1, _ = _weight_grad(act1, dff1, DOWN_GRAD_ROWS, d, "dw_down1")
    parts_d1, _ = sibling_sum([g_d1.reshape(N_CHIPS, 2, fq // 2, d)], "rs_sibling_down1")
    g_gu1, recv_d1 = _weight_grad(n1, dgu1, d // 2, fc, "dw_gate_up1", _ChipExchange(parts_d1))
    parts_gu1, _ = sibling_sum([g_gu1], "rs_sibling_gate_up1")
    (dh2, dproj, dyv, bcv, sg_c), recv_gu1 = _bwd_conv(dh3, h2, y, proj, small, win_f, wout_f, _ChipExchange(parts_gu1))
    g_in, _ = _weight_grad(nc, dproj, d // 2, 3 * d // N_CHIPS, "dw_conv_in")
    g_out, _ = _weight_grad(bcv, dyv, d // 2, d, "dw_conv_out")
    parts_c, _ = sibling_sum([g_in, g_out.reshape(N_CHIPS, 2, dq // 2, d)], "rs_sibling_conv")
    (dh1, dgu0, dff0, act0, sg_f0), recv_c = _bwd_ffn(dh2, h1, ff0, gu0, small, wgu0, wd0, 0, _ChipExchange(parts_c))
    g_d0, _ = _weight_grad(act0, dff0, DOWN_GRAD_ROWS, d, "dw_down0")
    parts_d0, _ = sibling_sum([g_d0.reshape(N_CHIPS, 2, fq // 2, d)], "rs_sibling_down0")
    g_gu0, recv_d0 = _weight_grad(n0, dgu0, d // 2, fc, "dw_gate_up0", _ChipExchange(parts_d0))
    parts_gu0, _ = sibling_sum([g_gu0], "rs_sibling_gate_up0")
    (grad_x, dpool, sg_p), recv_gu0 = _bwd_pool(dh1, x2, small, pool_scale, poolw, _ChipExchange(parts_gu0))
    g_pool = dpool.astype(BF16).reshape(2, ng // 2, N_CHIPS, gw // N_CHIPS, gw).transpose(2, 0, 1, 3, 4).reshape(
        N_CHIPS, 2, ng // 2 * (gw // N_CHIPS), gw)
    small_g = jnp.concatenate(
        [sg_p[0:2], sg_f0[0:2], sg_c[0:2], sg_f1[0:2], sg_c[2:5], sg_p[2:3],
         jnp.broadcast_to(loss_blk[0:1, 0:1], (1, d)), jnp.zeros((3, d), F32)], axis=0)
    parts_p, (small_all,) = sibling_sum([g_pool], "rs_sibling_pool", small_g)
    recv_p = _alone(_ChipExchange(parts_p), "rs_chips_pool")

    gs_gu = _add_chips(parts_gu1[0], recv_gu1[0], where_arr, 1, 2)
    gs_gu = _add_chips(parts_gu0[0], recv_gu0[0], where_arr, 0, 2, gs_gu)
    gs_d = _add_chips(parts_d1[0], recv_d1[0], where_arr, 1, 2)
    gs_d = _add_chips(parts_d0[0], recv_d0[0], where_arr, 0, 2, gs_d)
    gs_in = _add_chips(parts_c[0], recv_c[0], where_arr)
    gs_out = _add_chips(parts_c[1], recv_c[1], where_arr)
    gs_pool = _add_chips(parts_p[0], recv_p[0], where_arr)
    full = _sibling_share([gs_gu, gs_d, gs_in, gs_out, gs_pool])
    small_sum = _sum_small(small_all)
    loss = small_sum[12, 0]

    gg_gu = full[0].reshape(2, d, fc)
    gg_d = full[1].reshape(2, fq, d)
    gg_in = full[2].reshape(1, d, -1)
    gg_out = full[3].reshape(1, dq, d)
    gg_pool = full[4].reshape(1, ng, gw // N_CHIPS, gw)
    mine = lax.dynamic_slice_in_dim(small_sum, chip * dq, dq, axis=1)
    gg_gains = mine[0:8].reshape(2, 4, dq)
    gg_taps = mine[8:11].reshape(1, 3, dq)
    gg_scale = small_sum[11:12]

    grads = [gg_gains, gg_pool, gg_scale, gg_in, gg_taps, gg_out, gg_gu, gg_d]
    weights = [norm_gains, pool_w, pool_scale, conv_in_w, conv_w, conv_out_w, ffn_gate_up_w, ffn_down_w]
    ms = [m_norm_gains, m_pool_w, m_pool_scale, m_conv_in_w, m_conv_w, m_conv_out_w, m_ffn_gate_up_w, m_ffn_down_w]
    vs = [v_norm_gains, v_pool_w, v_pool_scale, v_conv_in_w, v_conv_w, v_conv_out_w, v_ffn_gate_up_w, v_ffn_down_w]
    names = ["gains", "pool_w", "pool_scale", "conv_in", "taps", "conv_out", "gate_up", "down"]
    deltas, new_ms, new_vs = [], [], []
    for w, g, m, v, nm in zip(weights, grads, ms, vs, names):
        flat = (-1, w.shape[-1])
        dl, m2, v2 = _adamw(w.reshape(flat), g.reshape(flat), m.reshape(flat), v.reshape(flat), "adamw_" + nm)
        deltas.append(dl.reshape(w.shape))
        new_ms.append(m2.reshape(w.shape))
        new_vs.append(v2.reshape(w.shape))
    return (loss, grad_x[None], *grads, *deltas, *new_ms, *new_vs)
```

```python
import jax
import jax.numpy as jnp
from jax import lax
from jax.experimental import pallas as pl
from jax.experimental.pallas import tpu as pltpu

RMS_EPS = 1e-6
POOL_WINDOWS = (2, 4, 8, 16)
POOL_HALO = 16
CONV_HALO = 8
N_CHIPS = 4
N_DEV = 8
ADAM_LR = 0.001
ADAM_B1 = 0.9
ADAM_B2 = 0.999
ADAM_EPS = 1e-08
ADAM_WD = 0.01
ADAM_STEP = 10
VMEM_LIMIT = 56 * 2**20
DOWN_GRAD_ROWS = 256
STREAM_BUDGET = 24 * 2**20
MESH = pl.DeviceIdType.MESH
ANY = pl.BlockSpec(memory_space=pl.ANY)
DMA = pltpu.SemaphoreType.DMA
BF16 = jnp.bfloat16
F32 = jnp.float32


def _token_tile(t, rows=512):
    return min(rows, t)


def _rms(x):
    r = lax.rsqrt(jnp.mean(x * x, axis=-1, keepdims=True) + RMS_EPS)
    return x * r, r


def _rms_bwd(dy, xh, r, g):
    a = dy * g
    return r * (a - xh * jnp.mean(a * xh, axis=-1, keepdims=True))


def _dot(a, b):
    return jnp.dot(a, b, preferred_element_type=F32)


def _dot_nt(a, b):
    return lax.dot_general(a, b, (((1,), (1,)), ((), ())), preferred_element_type=F32)


def _dot_tn(a, b):
    return lax.dot_general(a, b, (((0,), (0,)), ((), ())), preferred_element_type=F32)


def _colsum(a):
    return jnp.sum(a, axis=0, keepdims=True)


def _resident(block, index_map):
    return pl.BlockSpec(block, index_map, pipeline_mode=pl.Buffered(1))


def _row_block(r, row_bytes):
    best = None
    for rb in range(16, r + 1, 16):
        if r % rb == 0 and rb * row_bytes <= STREAM_BUDGET:
            best = rb
    return best if best is not None else r


def _place():
    x, y, c = lax.axis_index("x"), lax.axis_index("y"), lax.axis_index("c")
    return x, y, c, 2 * x + y


def _dev(chip, core):
    return (chip // 2, chip % 2, core)


def _remote(src, dst, send_sem, recv_sem, device):
    return pltpu.make_async_remote_copy(src_ref=src, dst_ref=dst, send_sem=send_sem, recv_sem=recv_sem,
                                        device_id=device, device_id_type=MESH)


class _Gather:
    def __init__(self, shards):
        n = len(shards)
        self.args = [s for s, _ in shards]
        self.layers = [l for _, l in shards]
        self.out_shape = [jax.ShapeDtypeStruct((N_CHIPS,) + s.shape[1:], s.dtype) for s in self.args]
        self.sems = [DMA((n,)), DMA((n,)), DMA((n, 3)), DMA((n, 3)), DMA((n, 3)), DMA((n, 3))]

    def _own(self, loc, out, sems, a):
        x, y, c, k = _place()
        return _remote(loc[a].at[self.layers[a]], out[a].at[k], sems[0].at[a], sems[1].at[a], (x, y, 1 - c))

    def _ici(self, loc, out, sems, a, m, arrival):
        x, y, c, k = _place()
        dst = out[a].at[k ^ m, c] if arrival else out[a].at[k, c]
        return _remote(loc[a].at[self.layers[a], c], dst, sems[2].at[a, m - 1], sems[3].at[a, m - 1], _dev(k ^ m, c))

    def _forward(self, out, sems, a, m, arrival):
        x, y, c, k = _place()
        got = out[a].at[k ^ m, 1 - c] if arrival else out[a].at[k ^ m, c]
        return _remote(got, got, sems[4].at[a, m - 1], sems[5].at[a, m - 1], (x, y, 1 - c))

    def start(self, loc, out, sems):
        for a in range(len(self.args)):
            for m in range(1, N_CHIPS):
                self._ici(loc, out, sems, a, m, False).start()
            self._own(loc, out, sems, a).start()

    def finish(self, loc, out, sems):
        n = len(self.args)
        for a in range(n):
            for m in range(1, N_CHIPS):
                self._ici(loc, out, sems, a, m, True).wait_recv()
                self._forward(out, sems, a, m, False).start()
        for a in range(n):
            for m in range(1, N_CHIPS):
                self._forward(out, sems, a, m, True).wait_recv()
            self._own(loc, out, sems, a).wait_recv()
        for a in range(n):
            for m in range(1, N_CHIPS):
                self._ici(loc, out, sems, a, m, False).wait_send()
                self._forward(out, sems, a, m, False).wait_send()
            self._own(loc, out, sems, a).wait_send()


class _ChipExchange:
    def __init__(self, parts):
        n = len(parts)
        self.args = list(parts)
        self.out_shape = [jax.ShapeDtypeStruct((N_CHIPS - 1,) + p.shape[1:], p.dtype) for p in parts]
        self.sems = [DMA((n, 3)), DMA((n, 3))]

    def _copy(self, p, land, sems, a, m):
        x, y, c, k = _place()
        return _remote(p[a].at[k ^ m], land[a].at[m - 1], sems[0].at[a, m - 1], sems[1].at[a, m - 1], _dev(k ^ m, c))

    def start(self, p, land, sems):
        for a in range(len(self.args)):
            for m in range(1, N_CHIPS):
                self._copy(p, land, sems, a, m).start()

    def finish(self, p, land, sems):
        for a in range(len(self.args)):
            for m in range(1, N_CHIPS):
                self._copy(p, land, sems, a, m).wait_recv()
        for a in range(len(self.args)):
            for m in range(1, N_CHIPS):
                self._copy(p, land, sems, a, m).wait_send()


def _hosted(body, comm, *, name, grid, in_specs, out_specs, out_shape, args, scratch_shapes=()):
    ni, no, ns = len(in_specs), len(out_shape), len(scratch_shapes)
    if comm is None:
        res = pl.pallas_call(
            body, name=name, grid=grid, in_specs=list(in_specs), out_specs=list(out_specs), out_shape=list(out_shape),
            scratch_shapes=list(scratch_shapes),
            compiler_params=pltpu.CompilerParams(dimension_semantics=("arbitrary",) * len(grid), vmem_limit_bytes=VMEM_LIMIT),
        )(*args)
        return list(res), []
    nc, nco = len(comm.args), len(comm.out_shape)

    def full(*refs):
        cin = refs[ni:ni + nc]
        outs = refs[ni + nc:ni + nc + no]
        cout = refs[ni + nc + no:ni + nc + no + nco]
        scratch = refs[ni + nc + no + nco:ni + nc + no + nco + ns]
        csems = refs[ni + nc + no + nco + ns:]
        first = _all_of([pl.program_id(ax) == 0 for ax in range(len(grid))])
        last = _all_of([pl.program_id(ax) == grid[ax] - 1 for ax in range(len(grid))])

        @pl.when(first)
        def _():
            comm.start(cin, cout, csems)

        body(*refs[:ni], *outs, *scratch)

        @pl.when(last)
        def _():
            comm.finish(cin, cout, csems)

    res = pl.pallas_call(
        full, name=name, grid=grid, in_specs=[*in_specs, *[ANY] * nc], out_specs=[*out_specs, *[ANY] * nco],
        out_shape=[*out_shape, *comm.out_shape], scratch_shapes=[*scratch_shapes, *comm.sems],
        compiler_params=pltpu.CompilerParams(dimension_semantics=("arbitrary",) * len(grid), vmem_limit_bytes=VMEM_LIMIT,
                                             has_side_effects=True),
    )(*args, *comm.args)
    return list(res[:no]), list(res[no:])


def _all_of(conds):
    out = conds[0]
    for c in conds[1:]:
        out = jnp.logical_and(out, c)
    return out


def _alone(comm, name):
    return _hosted(lambda: None, comm, name=name, grid=(1,), in_specs=[], out_specs=[], out_shape=[], args=[])[1]


def _sibling_exchange(grads, small, name):
    n = len(grads)
    ns = 0 if small is None else 1

    def body(*refs):
        g = refs[:n]
        land = refs[n + ns:2 * n + ns]
        send_sems, recv_sems, own_sem, ssend_sems, srecv_sems = refs[2 * n + 2 * ns:]
        x, y, c, k = _place()
        me = 2 * k + c
        cps = []
        for a in range(n):
            cp = _remote(g[a].at[:, pl.ds(1 - c, 1)], land[a], send_sems.at[a], recv_sems.at[a], (x, y, 1 - c))
            cp.start()
            cps.append(cp)
        if small is not None:
            sm, smg = refs[n], refs[2 * n + 1]
            peers = [me ^ m for m in range(1, N_DEV)]
            ids = [(p // 4, (p // 2) % 2, p % 2) for p in peers]
            own = pltpu.make_async_copy(sm, smg.at[me], own_sem)
            own.start()
            for m in range(1, N_DEV):
                cp = _remote(sm, smg.at[me], ssend_sems.at[m - 1], srecv_sems.at[m - 1], ids[m - 1])
                cp.start()
                cps.append(cp)
            for m in range(1, N_DEV):
                _remote(sm, smg.at[peers[m - 1]], ssend_sems.at[m - 1], srecv_sems.at[m - 1], ids[m - 1]).wait_recv()
            own.wait()
        for cp in cps[:n]:
            cp.wait_recv()
        for cp in cps:
            cp.wait_send()

    out_shape = [jax.ShapeDtypeStruct((N_CHIPS, 1) + a.shape[2:], a.dtype) for a in grads]
    ins = list(grads)
    if small is not None:
        out_shape.append(jax.ShapeDtypeStruct((N_DEV,) + small.shape, small.dtype))
        ins.append(small)
    return pl.pallas_call(
        body, name=name, out_shape=out_shape, in_specs=[ANY] * (n + ns), out_specs=[ANY] * (n + ns),
        scratch_shapes=[DMA((n,)), DMA((n,)), DMA, DMA((N_DEV - 1,)), DMA((N_DEV - 1,))],
        compiler_params=pltpu.CompilerParams(has_side_effects=True),
    )(*ins)


def _sibling_share(halves):
    n = len(halves)

    def body(*refs):
        out = refs[n:2 * n]
        send_sems, recv_sems = refs[2 * n:]
        x, y, c, k = _place()
        cps = []
        for a in range(n):
            mine = out[a].at[:, pl.ds(c, 1)]
            cp = _remote(mine, mine, send_sems.at[a], recv_sems.at[a], (x, y, 1 - c))
            cp.start()
            cps.append(cp)
        for a in range(n):
            theirs = out[a].at[:, pl.ds(1 - c, 1)]
            _remote(theirs, theirs, send_sems.at[a], recv_sems.at[a], (x, y, 1 - c)).wait_recv()
        for cp in cps:
            cp.wait_send()

    out_shape = [jax.ShapeDtypeStruct(a.shape, a.dtype) for a in halves]
    return pl.pallas_call(
        body, name="rs_sibling_share", out_shape=out_shape, in_specs=[ANY] * n, out_specs=[ANY] * n,
        input_output_aliases={a: a for a in range(n)}, scratch_shapes=[DMA((n,)), DMA((n,))],
        compiler_params=pltpu.CompilerParams(has_side_effects=True),
    )(*halves)


def _add_sibling(g, land, core):
    _, _, r, c = g.shape
    rb = _row_block(r, c * (3 * 2 * 2 + 2 * 4))

    def body(core_ref, g_ref, l_ref, o_ref):
        o_ref[...] = (g_ref[...].astype(F32) + l_ref[...].astype(F32)).astype(o_ref.dtype)

    return pl.pallas_call(
        body, name="rs_add_sibling", out_shape=jax.ShapeDtypeStruct((N_CHIPS, r, c), g.dtype),
        grid_spec=pltpu.PrefetchScalarGridSpec(
            num_scalar_prefetch=1, grid=(N_CHIPS, r // rb),
            in_specs=[pl.BlockSpec((None, None, rb, c), lambda j, i, core_ref: (j, core_ref[0], i, 0)),
                      pl.BlockSpec((None, None, rb, c), lambda j, i, core_ref: (j, 0, i, 0))],
            out_specs=pl.BlockSpec((None, rb, c), lambda j, i, core_ref: (j, i, 0))),
        compiler_params=pltpu.CompilerParams(dimension_semantics=("parallel", "parallel"), vmem_limit_bytes=VMEM_LIMIT),
    )(core, g, land)


def _add_chips(part, land, where, layer=0, n_layers=1, into=None):
    _, r, c = part.shape
    rb = _row_block(r, c * (4 * 2 * 2 + 4 * 2 + 2 * 4))

    def body(where_ref, p_ref, l_ref, *rest):
        acc = p_ref[...].astype(F32)
        for m in range(N_CHIPS - 1):
            acc = acc + l_ref[m].astype(F32)
        rest[-1][...] = acc

    in_specs = [pl.BlockSpec((None, rb, c), lambda i, where_ref: (where_ref[0], i, 0)),
                pl.BlockSpec((N_CHIPS - 1, rb, c), lambda i, where_ref: (0, i, 0))]
    args = [where, part, land]
    if into is not None:
        in_specs.append(ANY)
        args.append(into)
    return pl.pallas_call(
        body, name="rs_add_chips", out_shape=jax.ShapeDtypeStruct((n_layers, 2, r, c), F32),
        grid_spec=pltpu.PrefetchScalarGridSpec(
            num_scalar_prefetch=1, grid=(r // rb,), in_specs=in_specs,
            out_specs=pl.BlockSpec((None, None, rb, c), lambda i, where_ref: (layer, where_ref[1], i, 0))),
        input_output_aliases={} if into is None else {3: 0},
        compiler_params=pltpu.CompilerParams(dimension_semantics=("parallel",), vmem_limit_bytes=VMEM_LIMIT),
    )(*args)


def _sum_small(smg):
    def body(s_ref, o_ref):
        acc = s_ref[0]
        for j in range(1, N_DEV):
            acc = acc + s_ref[j]
        o_ref[...] = acc

    return pl.pallas_call(body, name="rs_sum_small", out_shape=jax.ShapeDtypeStruct(smg.shape[1:], F32))(smg)


def _pool_windows(ext_ref, g, gw, tm, first_row):
    w = POOL_WINDOWS[g]
    slab = ext_ref[:, g * gw:(g + 1) * gw]
    p, k = slab, 1
    while k < w:
        p = p + pltpu.roll(p, k, 0)
        k *= 2
    t = first_row + lax.broadcasted_iota(jnp.int32, (tm, 1), 0)
    cnt = jnp.minimum(t + 1, w).astype(F32)
    return p[POOL_HALO:] / cnt - slab[POOL_HALO:]


def _fwd_pool(x, small, scale, poolw, comm=None):
    t, d = x.shape
    tm = _token_tile(t)
    gw = d // len(POOL_WINDOWS)

    def body(x_ref, sm_ref, sc_ref, w_ref, h_ref, ext_ref, mix_ref):
        i = pl.program_id(0)

        @pl.when(i == 0)
        def _():
            ext_ref[0:POOL_HALO, :] = jnp.zeros((POOL_HALO, d), F32)

        @pl.when(i > 0)
        def _():
            ext_ref[0:POOL_HALO, :] = ext_ref[tm:tm + POOL_HALO, :]

        xv = x_ref[...]
        xh, _ = _rms(xv)
        ext_ref[POOL_HALO:, :] = xh * sm_ref[0:1, :]
        for g in range(len(POOL_WINDOWS)):
            pooled = _pool_windows(ext_ref, g, gw, tm, i * tm)
            cols = slice(g * gw, (g + 1) * gw)
            mix_ref[:, cols] = _dot(pooled.astype(BF16), w_ref[g]) * sc_ref[:, cols]
        mh, _ = _rms(mix_ref[...])
        h_ref[...] = xv + mh * sm_ref[1:2, :]

    (h,), got = _hosted(
        body, comm, name="fwd_pool", grid=(t // tm,), out_shape=[jax.ShapeDtypeStruct((t, d), F32)],
        in_specs=[pl.BlockSpec((tm, d), lambda i: (i, 0)), _resident(small.shape, lambda i: (0, 0)),
                  _resident(scale.shape, lambda i: (0, 0)), _resident(poolw.shape, lambda i: (0, 0, 0))],
        out_specs=[pl.BlockSpec((tm, d), lambda i: (i, 0))],
        scratch_shapes=[pltpu.VMEM((POOL_HALO + tm, d), F32), pltpu.VMEM((tm, d), F32)],
        args=[x, small, scale, poolw])
    return h, got


def _fwd_ffn(h, small, wgu, wd, layer, comm=None, target=None):
    t, d = h.shape
    tm = _token_tile(t)
    steps = t // tm
    fc = wgu.shape[-1]
    f = 2 * fc
    g_in, g_out = 4 * layer + 2, 4 * layer + 3
    with_loss = target is not None

    def body(h_ref, *refs):
        if with_loss:
            t_ref, sm_ref, wgu_ref, wd_ref, o_ref, gu_ref, ff_ref, n_ref, l_ref, acc_ref = refs
        else:
            sm_ref, wgu_ref, wd_ref, o_ref, gu_ref, ff_ref, n_ref = refs
        hv = h_ref[...]
        hh, _ = _rms(hv)
        n = (hh * sm_ref[g_in:g_in + 1, :]).astype(BF16)
        n_ref[...] = n
        ff = None
        for j in range(2):
            gate = _dot(n, wgu_ref[j])
            up = _dot(n, wgu_ref[2 + j])
            gu_ref[:, j * fc:(j + 1) * fc] = gate.astype(BF16)
            gu_ref[:, f + j * fc:f + (j + 1) * fc] = up.astype(BF16)
            act = (gate * jax.nn.sigmoid(gate) * up).astype(BF16)
            part = _dot(act, wd_ref[j * fc:(j + 1) * fc, :])
            ff = part if ff is None else ff + part
        ff_ref[...] = ff
        fh, _ = _rms(ff)
        out = hv + fh * sm_ref[g_out:g_out + 1, :]
        if not with_loss:
            o_ref[...] = out
            return
        i = pl.program_id(0)
        e = out - t_ref[...]
        o_ref[...] = e * (1.0 / d)

        @pl.when(i == 0)
        def _():
            acc_ref[...] = jnp.zeros_like(acc_ref)

        acc_ref[...] += _colsum(e * e)

        @pl.when(i == steps - 1)
        def _():
            l_ref[...] = jnp.full(l_ref.shape, 0.5 / d, F32) * jnp.sum(acc_ref[...])

    row = lambda i: (i, 0)
    out_shape = [jax.ShapeDtypeStruct((t, d), F32), jax.ShapeDtypeStruct((t, 2 * f), BF16),
                 jax.ShapeDtypeStruct((t, d), F32), jax.ShapeDtypeStruct((t, d), BF16)]
    out_specs = [pl.BlockSpec((tm, d), row), pl.BlockSpec((tm, 2 * f), row), pl.BlockSpec((tm, d), row),
                 pl.BlockSpec((tm, d), row)]
    weight_specs = [_resident(small.shape, lambda i: (0, 0)), _resident(wgu.shape, lambda i: (0, 0, 0)),
                    _resident(wd.shape, lambda i: (0, 0))]
    if with_loss:
        return _hosted(
            body, comm, name=f"fwd_ffn{layer}_loss", grid=(steps,),
            out_shape=out_shape + [jax.ShapeDtypeStruct((8, 128), F32)],
            in_specs=[pl.BlockSpec((tm, d), row), pl.BlockSpec((tm, d), row)] + weight_specs,
            out_specs=out_specs + [pl.BlockSpec((8, 128), lambda i: (0, 0))],
            scratch_shapes=[pltpu.VMEM((1, d), F32)], args=[h, target, small, wgu, wd])
    return _hosted(
        body, comm, name=f"fwd_ffn{layer}", grid=(steps,), out_shape=out_shape,
        in_specs=[pl.BlockSpec((tm, d), row)] + weight_specs, out_specs=out_specs, args=[h, small, wgu, wd])


def _fwd_conv(h, small, win, wout, comm=None):
    t, d = h.shape
    tm = _token_tile(t)
    pc = win.shape[-1]

    def body(h_ref, sm_ref, win_ref, wout_ref, o_ref, proj_ref, y_ref, n_ref, pj_ref, uext_ref):
        i = pl.program_id(0)

        @pl.when(i == 0)
        def _():
            uext_ref[0:CONV_HALO, :] = jnp.zeros((CONV_HALO, d), F32)

        @pl.when(i > 0)
        def _():
            uext_ref[0:CONV_HALO, :] = uext_ref[tm:tm + CONV_HALO, :]

        hv = h_ref[...]
        hh, _ = _rms(hv)
        n = (hh * sm_ref[4:5, :]).astype(BF16)
        n_ref[...] = n
        for k in range(N_CHIPS):
            pj_ref[:, k * pc:(k + 1) * pc] = _dot(n, win_ref[k])
        proj_ref[...] = pj_ref[...].astype(BF16)
        uext_ref[CONV_HALO:, :] = pj_ref[:, d:2 * d] * pj_ref[:, 2 * d:]
        taps = [sm_ref[8 + j:9 + j, :] for j in range(3)]
        full = uext_ref[...]
        conv = (full[CONV_HALO:] * taps[2] + pltpu.roll(full, 1, 0)[CONV_HALO:] * taps[1]
                + pltpu.roll(full, 2, 0)[CONV_HALO:] * taps[0])
        y = _dot((pj_ref[:, 0:d] * conv).astype(BF16), wout_ref[...])
        y_ref[...] = y
        yh, _ = _rms(y)
        o_ref[...] = hv + yh * sm_ref[5:6, :]

    row = lambda i: (i, 0)
    return _hosted(
        body, comm, name="fwd_conv", grid=(t // tm,),
        out_shape=[jax.ShapeDtypeStruct((t, d), F32), jax.ShapeDtypeStruct((t, 3 * d), BF16),
                   jax.ShapeDtypeStruct((t, d), F32), jax.ShapeDtypeStruct((t, d), BF16)],
        in_specs=[pl.BlockSpec((tm, d), row), _resident(small.shape, lambda i: (0, 0)),
                  _resident(win.shape, lambda i: (0, 0, 0)), _resident(wout.shape, lambda i: (0, 0))],
        out_specs=[pl.BlockSpec((tm, d), row), pl.BlockSpec((tm, 3 * d), row), pl.BlockSpec((tm, d), row),
                   pl.BlockSpec((tm, d), row)],
        scratch_shapes=[pltpu.VMEM((tm, 3 * d), F32), pltpu.VMEM((CONV_HALO + tm, d), F32)],
        args=[h, small, win, wout])


def _bwd_ffn(dh, h, ff, gu, small, wgu, wd, layer, comm=None):
    t, d = h.shape
    tm = _token_tile(t, 256)
    fc = wgu.shape[-1]
    f = 2 * fc
    g_in, g_out = 4 * layer + 2, 4 * layer + 3

    def body(dh_ref, h_ref, ff_ref, gu_ref, sm_ref, wgu_ref, wd_ref, o_ref, dgu_ref, dff_ref, act_ref, sg_ref):
        i = pl.program_id(0)

        @pl.when(i == 0)
        def _():
            sg_ref[...] = jnp.zeros_like(sg_ref)

        dy = dh_ref[...]
        fh, r3 = _rms(ff_ref[...])
        sg_ref[1:2, :] += _colsum(dy * fh)
        dff = _rms_bwd(dy, fh, r3, sm_ref[g_out:g_out + 1, :]).astype(BF16)
        dff_ref[...] = dff
        for j in range(2):
            dact = _dot_nt(dff, wd_ref[j * fc:(j + 1) * fc, :])
            gate = gu_ref[:, j * fc:(j + 1) * fc].astype(F32)
            up = gu_ref[:, f + j * fc:f + (j + 1) * fc].astype(F32)
            sig = jax.nn.sigmoid(gate)
            silu = gate * sig
            act_ref[:, j * fc:(j + 1) * fc] = (silu * up).astype(BF16)
            dgu_ref[:, j * fc:(j + 1) * fc] = (dact * up * (sig * (1.0 + gate * (1.0 - sig)))).astype(BF16)
            dgu_ref[:, f + j * fc:f + (j + 1) * fc] = (dact * silu).astype(BF16)
        dn = None
        for k in range(N_CHIPS):
            part = _dot_nt(dgu_ref[:, k * fc:(k + 1) * fc], wgu_ref[k])
            dn = part if dn is None else dn + part
        hh, r2 = _rms(h_ref[...])
        sg_ref[0:1, :] += _colsum(dn * hh)
        o_ref[...] = dy + _rms_bwd(dn, hh, r2, sm_ref[g_in:g_in + 1, :])

    row = lambda i: (i, 0)
    return _hosted(
        body, comm, name=f"bwd_ffn{layer}", grid=(t // tm,),
        out_shape=[jax.ShapeDtypeStruct((t, d), F32), jax.ShapeDtypeStruct((t, 2 * f), BF16),
                   jax.ShapeDtypeStruct((t, d), BF16), jax.ShapeDtypeStruct((t, f), BF16),
                   jax.ShapeDtypeStruct((8, d), F32)],
        in_specs=[pl.BlockSpec((tm, d), row), pl.BlockSpec((tm, d), row), pl.BlockSpec((tm, d), row),
                  pl.BlockSpec((tm, 2 * f), row), _resident(small.shape, lambda i: (0, 0)),
                  _resident(wgu.shape, lambda i: (0, 0, 0)), _resident(wd.shape, lambda i: (0, 0))],
        out_specs=[pl.BlockSpec((tm, d), row), pl.BlockSpec((tm, 2 * f), row), pl.BlockSpec((tm, d), row),
                   pl.BlockSpec((tm, f), row), pl.BlockSpec((8, d), lambda i: (0, 0))],
        args=[dh, h, ff, gu, small, wgu, wd])


def _bwd_conv(dh, h, y, proj, small, win, wout, comm=None):
    t, d = h.shape
    tm = _token_tile(t)
    steps = t // tm
    pc = win.shape[-1]
    halo_blocks = tm // 16

    def body(dh_ref, h_ref, y_ref, proj_ref, halo_ref, sm_ref, win_ref, wout_ref,
             o_ref, dproj_ref, dy_ref, bc_ref, sg_ref, uext_ref, dcext_ref, carry_ref):
        i = pl.program_id(0)
        tile = steps - 1 - i

        @pl.when(i == 0)
        def _():
            sg_ref[...] = jnp.zeros_like(sg_ref)
            carry_ref[...] = jnp.zeros_like(carry_ref)

        dy = dh_ref[...]
        yh, r1 = _rms(y_ref[...])
        sg_ref[1:2, :] += _colsum(dy * yh)
        dyv = _rms_bwd(dy, yh, r1, sm_ref[5:6, :]).astype(BF16)
        dy_ref[...] = dyv
        dbc = _dot_nt(dyv, wout_ref[...])
        b = proj_ref[:, 0:d].astype(F32)
        cg = proj_ref[:, d:2 * d].astype(F32)
        v = proj_ref[:, 2 * d:].astype(F32)
        halo = halo_ref[...].astype(F32)[16 - CONV_HALO:]
        uh = halo[:, d:2 * d] * halo[:, 2 * d:]
        uext_ref[0:CONV_HALO, :] = jnp.where(tile > 0, uh, jnp.zeros_like(uh))
        uext_ref[CONV_HALO:, :] = cg * v
        taps = [sm_ref[8 + j:9 + j, :] for j in range(3)]
        full = uext_ref[...]
        u0 = full[CONV_HALO:]
        u1 = pltpu.roll(full, 1, 0)[CONV_HALO:]
        u2 = pltpu.roll(full, 2, 0)[CONV_HALO:]
        conv = u0 * taps[2] + u1 * taps[1] + u2 * taps[0]
        bc_ref[...] = (b * conv).astype(BF16)
        dconv = dbc * b
        sg_ref[4:5, :] += _colsum(dconv * u0)
        sg_ref[3:4, :] += _colsum(dconv * u1)
        sg_ref[2:3, :] += _colsum(dconv * u2)
        dcext_ref[0:tm, :] = dconv
        dcext_ref[tm:, :] = carry_ref[...]
        carry_ref[...] = dconv[0:CONV_HALO]
        dfull = dcext_ref[...]
        n8 = tm + CONV_HALO
        du = (dfull[0:tm] * taps[2] + pltpu.roll(dfull, n8 - 1, 0)[0:tm] * taps[1]
              + pltpu.roll(dfull, n8 - 2, 0)[0:tm] * taps[0])
        dproj_ref[:, 0:d] = (dbc * conv).astype(BF16)
        dproj_ref[:, d:2 * d] = (du * v).astype(BF16)
        dproj_ref[:, 2 * d:] = (du * cg).astype(BF16)
        dn = None
        for k in range(N_CHIPS):
            part = _dot_nt(dproj_ref[:, k * pc:(k + 1) * pc], win_ref[k])
            dn = part if dn is None else dn + part
        hh, r0 = _rms(h_ref[...])
        sg_ref[0:1, :] += _colsum(dn * hh)
        o_ref[...] = dy + _rms_bwd(dn, hh, r0, sm_ref[4:5, :])

    rev = lambda i: (steps - 1 - i, 0)
    before = lambda i: (jnp.maximum((steps - 1 - i) * halo_blocks - 1, 0), 0)
    return _hosted(
        body, comm, name="bwd_conv", grid=(steps,),
        out_shape=[jax.ShapeDtypeStruct((t, d), F32), jax.ShapeDtypeStruct((t, 3 * d), BF16),
                   jax.ShapeDtypeStruct((t, d), BF16), jax.ShapeDtypeStruct((t, d), BF16),
                   jax.ShapeDtypeStruct((8, d), F32)],
        in_specs=[pl.BlockSpec((tm, d), rev), pl.BlockSpec((tm, d), rev), pl.BlockSpec((tm, d), rev),
                  pl.BlockSpec((tm, 3 * d), rev), pl.BlockSpec((16, 3 * d), before),
                  _resident(small.shape, lambda i: (0, 0)), _resident(win.shape, lambda i: (0, 0, 0)),
                  _resident(wout.shape, lambda i: (0, 0))],
        out_specs=[pl.BlockSpec((tm, d), rev), pl.BlockSpec((tm, 3 * d), rev), pl.BlockSpec((tm, d), rev),
                   pl.BlockSpec((tm, d), rev), pl.BlockSpec((8, d), lambda i: (0, 0))],
        scratch_shapes=[pltpu.VMEM((CONV_HALO + tm, d), F32), pltpu.VMEM((tm + CONV_HALO, d), F32),
                        pltpu.VMEM((CONV_HALO, d), F32)],
        args=[dh, h, y, proj, proj, small, win, wout])


def _bwd_pool(dh, x, small, scale, poolw, comm=None):
    t, d = x.shape
    tm = _token_tile(t)
    steps = t // tm
    ng = len(POOL_WINDOWS)
    gw = d // ng
    halo_blocks = tm // POOL_HALO

    def body(dh_ref, x_ref, halo_ref, sm_ref, sc_ref, w_ref, o_ref, dw_ref, sg_ref,
             ext_ref, mix_ref, mm_ref, pb_ref, qext_ref, dhn_ref, carry_ref):
        i = pl.program_id(0)
        tile = steps - 1 - i

        @pl.when(i == 0)
        def _():
            sg_ref[...] = jnp.zeros_like(sg_ref)
            dw_ref[...] = jnp.zeros_like(dw_ref)
            carry_ref[...] = jnp.zeros_like(carry_ref)

        g0 = sm_ref[0:1, :]
        xv = x_ref[...]
        xh, r0 = _rms(xv)
        hx, _ = _rms(halo_ref[...])
        ext_ref[0:POOL_HALO, :] = jnp.where(tile > 0, hx * g0, jnp.zeros_like(hx))
        ext_ref[POOL_HALO:, :] = xh * g0
        for g in range(ng):
            pooled = _pool_windows(ext_ref, g, gw, tm, tile * tm)
            cols = slice(g * gw, (g + 1) * gw)
            pb = pooled.astype(BF16)
            pb_ref[:, cols] = pb
            mm = _dot(pb, w_ref[g])
            mm_ref[:, cols] = mm
            mix_ref[:, cols] = mm * sc_ref[:, cols]
        dy = dh_ref[...]
        mh, r1 = _rms(mix_ref[...])
        sg_ref[1:2, :] += _colsum(dy * mh)
        dmix = _rms_bwd(dy, mh, r1, sm_ref[1:2, :])
        sg_ref[2:3, :] += _colsum(dmix * mm_ref[...])
        mix_ref[...] = dmix * sc_ref[...]
        n16 = tm + POOL_HALO
        for g in range(ng):
            w = POOL_WINDOWS[g]
            cols = slice(g * gw, (g + 1) * gw)
            dmm = mix_ref[:, cols].astype(BF16)
            dpooled = _dot_nt(dmm, w_ref[g])
            dw_ref[g] += _dot_tn(pb_ref[:, cols], dmm)
            trow = tile * tm + lax.broadcasted_iota(jnp.int32, (tm, 1), 0)
            q = dpooled / jnp.minimum(trow + 1, w).astype(F32)
            qext_ref[0:tm, cols] = q
            qext_ref[tm:, cols] = carry_ref[:, cols]
            carry_ref[:, cols] = q[0:POOL_HALO]
            p, k = qext_ref[:, cols], 1
            while k < w:
                p = p + pltpu.roll(p, n16 - k, 0)
                k *= 2
            dhn_ref[:, cols] = p[0:tm] - dpooled
        dhn = dhn_ref[...]
        sg_ref[0:1, :] += _colsum(dhn * xh)
        o_ref[...] = dy + _rms_bwd(dhn, xh, r0, g0)

    rev = lambda i: (steps - 1 - i, 0)
    before = lambda i: (jnp.maximum((steps - 1 - i) * halo_blocks - 1, 0), 0)
    return _hosted(
        body, comm, name="bwd_pool", grid=(steps,),
        out_shape=[jax.ShapeDtypeStruct((t, d), F32), jax.ShapeDtypeStruct((ng, gw, gw), F32),
                   jax.ShapeDtypeStruct((8, d), F32)],
        in_specs=[pl.BlockSpec((tm, d), rev), pl.BlockSpec((tm, d), rev), pl.BlockSpec((POOL_HALO, d), before),
                  _resident(small.shape, lambda i: (0, 0)), _resident(scale.shape, lambda i: (0, 0)),
                  _resident(poolw.shape, lambda i: (0, 0, 0))],
        out_specs=[pl.BlockSpec((tm, d), rev), pl.BlockSpec((ng, gw, gw), lambda i: (0, 0, 0)),
                   pl.BlockSpec((8, d), lambda i: (0, 0))],
        scratch_shapes=[pltpu.VMEM((POOL_HALO + tm, d), F32), pltpu.VMEM((tm, d), F32), pltpu.VMEM((tm, d), F32),
                        pltpu.VMEM((tm, d), BF16), pltpu.VMEM((tm + POOL_HALO, d), F32), pltpu.VMEM((tm, d), F32),
                        pltpu.VMEM((POOL_HALO, d), F32)],
        args=[dh, x, x, small, scale, poolw])


def _weight_grad(a, b, bm, bn, name, comm=None):
    t, m = a.shape
    _, n = b.shape

    def body(a_ref, b_ref, o_ref):
        o_ref[...] = _dot_tn(a_ref[...], b_ref[...]).astype(o_ref.dtype)

    (out,), got = _hosted(
        body, comm, name=name, grid=(n // bn, m // bm),
        out_shape=[jax.ShapeDtypeStruct((n // bn, m // bm, bm, bn), BF16)],
        in_specs=[pl.BlockSpec((t, bm), lambda j, i: (0, i)), pl.BlockSpec((t, bn), lambda j, i: (0, j))],
        out_specs=[pl.BlockSpec((None, None, bm, bn), lambda j, i: (j, i, 0, 0))],
        args=[a, b])
    return out, got


def _adamw(w, g, m, v, name):
    r, c = w.shape
    rb = _row_block(r, c * (7 * 4 * 2 + 4 * 4))
    bc1 = 1.0 - ADAM_B1 ** ADAM_STEP
    bc2 = 1.0 - ADAM_B2 ** ADAM_STEP

    def body(w_ref, g_ref, m_ref, v_ref, d_ref, nm_ref, nv_ref):
        gv = g_ref[...]
        nm = ADAM_B1 * m_ref[...] + (1.0 - ADAM_B1) * gv
        nv = ADAM_B2 * v_ref[...] + (1.0 - ADAM_B2) * (gv * gv)
        nm_ref[...] = nm
        nv_ref[...] = nv
        d_ref[...] = -ADAM_LR * ((nm / bc1) / (jnp.sqrt(nv / bc2) + ADAM_EPS) + ADAM_WD * w_ref[...])

    spec = pl.BlockSpec((rb, c), lambda i: (i, 0))
    return pl.pallas_call(
        body, name=name, grid=(r // rb,), out_shape=[jax.ShapeDtypeStruct((r, c), F32)] * 3,
        in_specs=[spec] * 4, out_specs=[spec] * 3,
        compiler_params=pltpu.CompilerParams(dimension_semantics=("parallel",), vmem_limit_bytes=VMEM_LIMIT),
    )(w, g, m, v)


def kernel(x, norm_gains, pool_w, pool_scale, conv_in_w, conv_w, conv_out_w, ffn_gate_up_w, ffn_down_w, loss_target, m_norm_gains, m_pool_w, m_pool_scale, m_conv_in_w, m_conv_w, m_conv_out_w, m_ffn_gate_up_w, m_ffn_down_w, v_norm_gains, v_pool_w, v_pool_scale, v_conv_in_w, v_conv_w, v_conv_out_w, v_ffn_gate_up_w, v_ffn_down_w):
    _, t, d = x.shape
    dq = d // N_CHIPS
    ng = len(POOL_WINDOWS)
    gw = d // ng
    fq = ffn_down_w.shape[1]
    f = N_CHIPS * fq
    fc = f // 2
    core = lax.axis_index("c")
    chip = 2 * lax.axis_index("x") + lax.axis_index("y")
    core_arr = jnp.reshape(core, (1,)).astype(jnp.int32)
    where_arr = jnp.stack([chip, core]).astype(jnp.int32)
    x2, target = x[0], loss_target[0]

    small_loc = jnp.concatenate(
        [norm_gains.reshape(8, dq), conv_w[0], jnp.zeros((5, dq), F32)], axis=0).reshape(1, 2, 8, dq)
    pool_loc = pool_w.astype(BF16).reshape(1, 2, ng // 2 * (gw // N_CHIPS), gw)
    wgu_loc = ffn_gate_up_w.astype(BF16).reshape(2, 2, d // 2, fc)
    wd_loc = ffn_down_w.astype(BF16).reshape(2, 2, fq // 2, d)
    win_loc = conv_in_w.astype(BF16).reshape(1, 2, d // 2, -1)
    wout_loc = conv_out_w.astype(BF16).reshape(1, 2, dq // 2, d)

    def ffn_weights(wgu_f, wd_f):
        return wgu_f.reshape(N_CHIPS, d, fc), wd_f.reshape(f, d)

    pool_f, small_f = _alone(_Gather([(pool_loc, 0), (small_loc, 0)]), "ag_first")
    poolw = pool_f.reshape(N_CHIPS, ng, gw // N_CHIPS, gw).transpose(1, 0, 2, 3).reshape(ng, gw, gw)
    small = small_f.transpose(1, 2, 0, 3).reshape(16, d)
    h1, got = _fwd_pool(x2, small, pool_scale, poolw, _Gather([(wgu_loc, 0), (wd_loc, 0)]))
    wgu0, wd0 = ffn_weights(*got)
    (h2, gu0, ff0, n0), got = _fwd_ffn(h1, small, wgu0, wd0, 0, _Gather([(win_loc, 0), (wout_loc, 0), (wgu_loc, 1)]))
    win_f, wout_f = got[0].reshape(N_CHIPS, d, -1), got[1].reshape(d, d)
    (h3, proj, y, nc), got_d = _fwd_conv(h2, small, win_f, wout_f, _Gather([(wd_loc, 1)]))
    wgu1, wd1 = ffn_weights(got[2], got_d[0])
    (dh4, gu1, ff1, n1, loss_blk), _ = _fwd_ffn(h3, small, wgu1, wd1, 1, target=target)

    def sibling_sum(pieces, name, small_g=None):
        landed = _sibling_exchange(pieces, small_g, name)
        return [_add_sibling(g, l, core_arr) for g, l in zip(pieces, landed)], landed[len(pieces):]

    (dh3, dgu1, dff1, act1, sg_f1), _ = _bwd_ffn(dh4, h3, ff1, gu1, small, wgu1, wd1, 1)
    g_d1, _ = _weight_grad(act1, dff1, DOWN_GRAD_ROWS, d, "dw_down1")
    parts_d1, _ = sibling_sum([g_d1.reshape(N_CHIPS, 2, fq // 2, d)], "rs_sibling_down1")
    g_gu1, recv_d1 = _weight_grad(n1, dgu1, d // 2, fc, "dw_gate_up1", _ChipExchange(parts_d1))
    parts_gu1, _ = sibling_sum([g_gu1], "rs_sibling_gate_up1")
    (dh2, dproj, dyv, bcv, sg_c), recv_gu1 = _bwd_conv(dh3, h2, y, proj, small, win_f, wout_f, _ChipExchange(parts_gu1))
    g_in, _ = _weight_grad(nc, dproj, d // 2, 3 * d // N_CHIPS, "dw_conv_in")
    g_out, _ = _weight_grad(bcv, dyv, d // 2, d, "dw_conv_out")
    parts_c, _ = sibling_sum([g_in, g_out.reshape(N_CHIPS, 2, dq // 2, d)], "rs_sibling_conv")
    (dh1, dgu0, dff0, act0, sg_f0), recv_c = _bwd_ffn(dh2, h1, ff0, gu0, small, wgu0, wd0, 0, _ChipExchange(parts_c))
    g_d0, _ = _weight_grad(act0, dff0, DOWN_GRAD_ROWS, d, "dw_down0")
    parts_d0, _ = sibling_sum([g_d0.reshape(N_CHIPS, 2, fq // 2, d)], "rs_sibling_down0")
    g_gu0, recv_d0 = _weight_grad(n0, dgu0, d // 2, fc, "dw_gate_up0", _ChipExchange(parts_d0))
    parts_gu0, _ = sibling_sum([g_gu0], "rs_sibling_gate_up0")
    (grad_x, dpool, sg_p), recv_gu0 = _bwd_pool(dh1, x2, small, pool_scale, poolw, _ChipExchange(parts_gu0))
    g_pool = dpool.astype(BF16).reshape(2, ng // 2, N_CHIPS, gw // N_CHIPS, gw).transpose(2, 0, 1, 3, 4).reshape(
        N_CHIPS, 2, ng // 2 * (gw // N_CHIPS), gw)
    small_g = jnp.concatenate(
        [sg_p[0:2], sg_f0[0:2], sg_c[0:2], sg_f1[0:2], sg_c[2:5], sg_p[2:3],
         jnp.broadcast_to(loss_blk[0:1, 0:1], (1, d)), jnp.zeros((3, d), F32)], axis=0)
    parts_p, (small_all,) = sibling_sum([g_pool], "rs_sibling_pool", small_g)
    recv_p = _alone(_ChipExchange(parts_p), "rs_chips_pool")

    gs_gu = _add_chips(parts_gu1[0], recv_gu1[0], where_arr, 1, 2)
    gs_gu = _add_chips(parts_gu0[0], recv_gu0[0], where_arr, 0, 2, gs_gu)
    gs_d = _add_chips(parts_d1[0], recv_d1[0], where_arr, 1, 2)
    gs_d = _add_chips(parts_d0[0], recv_d0[0], where_arr, 0, 2, gs_d)
    gs_in = _add_chips(parts_c[0], recv_c[0], where_arr)
    gs_out = _add_chips(parts_c[1], recv_c[1], where_arr)
    gs_pool = _add_chips(parts_p[0], recv_p[0], where_arr)
    full = _sibling_share([gs_gu, gs_d, gs_in, gs_out, gs_pool])
    small_sum = _sum_small(small_all)
    loss = small_sum[12, 0]

    gg_gu = full[0].reshape(2, d, fc)
    gg_d = full[1].reshape(2, fq, d)
    gg_in = full[2].reshape(1, d, -1)
    gg_out = full[3].reshape(1, dq, d)
    gg_pool = full[4].reshape(1, ng, gw // N_CHIPS, gw)
    mine = lax.dynamic_slice_in_dim(small_sum, chip * dq, dq, axis=1)
    gg_gains = mine[0:8].reshape(2, 4, dq)
    gg_taps = mine[8:11].reshape(1, 3, dq)
    gg_scale = small_sum[11:12]

    grads = [gg_gains, gg_pool, gg_scale, gg_in, gg_taps, gg_out, gg_gu, gg_d]
    weights = [norm_gains, pool_w, pool_scale, conv_in_w, conv_w, conv_out_w, ffn_gate_up_w, ffn_down_w]
    ms = [m_norm_gains, m_pool_w, m_pool_scale, m_conv_in_w, m_conv_w, m_conv_out_w, m_ffn_gate_up_w, m_ffn_down_w]
    vs = [v_norm_gains, v_pool_w, v_pool_scale, v_conv_in_w, v_conv_w, v_conv_out_w, v_ffn_gate_up_w, v_ffn_down_w]
    names = ["gains", "pool_w", "pool_scale", "conv_in", "taps", "conv_out", "gate_up", "down"]
    deltas, new_ms, new_vs = [], [], []
    for w, g, m, v, nm in zip(weights, grads, ms, vs, names):
        flat = (-1, w.shape[-1])
        dl, m2, v2 = _adamw(w.reshape(flat), g.reshape(flat), m.reshape(flat), v.reshape(flat), "adamw_" + nm)
        deltas.append(dl.reshape(w.shape))
        new_ms.append(m2.reshape(w.shape))
        new_vs.append(v2.reshape(w.shape))
    return (loss, grad_x[None], *grads, *deltas, *new_ms, *new_vs)
```

```python
import jax
import jax.numpy as jnp
from jax import lax
from jax.experimental import pallas as pl
from jax.experimental.pallas import tpu as pltpu

RMS_EPS = 1e-6
POOL_WINDOWS = (2, 4, 8, 16)
POOL_HALO = 16
CONV_HALO = 8
N_CHIPS = 4
N_DEV = 8
ADAM_LR = 0.001
ADAM_B1 = 0.9
ADAM_B2 = 0.999
ADAM_EPS = 1e-08
ADAM_WD = 0.01
ADAM_STEP = 10
VMEM_LIMIT = 56 * 2**20
STREAM_BUDGET = 24 * 2**20
MESH = pl.DeviceIdType.MESH
ANY = pl.BlockSpec(memory_space=pl.ANY)
DMA = pltpu.SemaphoreType.DMA
BF16 = jnp.bfloat16
F32 = jnp.float32


def _token_tile(t, rows=512):
    return min(rows, t)


def _rms(x):
    r = lax.rsqrt(jnp.mean(x * x, axis=-1, keepdims=True) + RMS_EPS)
    return x * r, r


def _rms_bwd(dy, xh, r, g):
    a = dy * g
    return r * (a - xh * jnp.mean(a * xh, axis=-1, keepdims=True))


def _dot(a, b):
    return jnp.dot(a, b, preferred_element_type=F32)


def _dot_nt(a, b):
    return lax.dot_general(a, b, (((1,), (1,)), ((), ())), preferred_element_type=F32)


def _dot_tn(a, b):
    return lax.dot_general(a, b, (((0,), (0,)), ((), ())), preferred_element_type=F32)


def _colsum(a):
    return jnp.sum(a, axis=0, keepdims=True)


def _resident(block, index_map):
    return pl.BlockSpec(block, index_map, pipeline_mode=pl.Buffered(1))


def _row_block(r, row_bytes):
    best = None
    for rb in range(16, r + 1, 16):
        if r % rb == 0 and rb * row_bytes <= STREAM_BUDGET:
            best = rb
    return best if best is not None else r


def _place():
    x, y, c = lax.axis_index("x"), lax.axis_index("y"), lax.axis_index("c")
    return x, y, c, 2 * x + y


def _dev(chip, core):
    return (chip // 2, chip % 2, core)


def _remote(src, dst, send_sem, recv_sem, device):
    return pltpu.make_async_remote_copy(src_ref=src, dst_ref=dst, send_sem=send_sem, recv_sem=recv_sem,
                                        device_id=device, device_id_type=MESH)


class _Gather:
    def __init__(self, shards):
        n = len(shards)
        self.args = [s for s, _ in shards]
        self.layers = [l for _, l in shards]
        self.out_shape = [jax.ShapeDtypeStruct((N_CHIPS,) + s.shape[1:], s.dtype) for s in self.args]
        self.sems = [DMA((n,)), DMA((n,)), DMA((n, 3)), DMA((n, 3)), DMA((n, 3)), DMA((n, 3))]

    def _own(self, loc, out, sems, a):
        x, y, c, k = _place()
        return _remote(loc[a].at[self.layers[a]], out[a].at[k], sems[0].at[a], sems[1].at[a], (x, y, 1 - c))

    def _ici(self, loc, out, sems, a, m, arrival):
        x, y, c, k = _place()
        dst = out[a].at[k ^ m, c] if arrival else out[a].at[k, c]
        return _remote(loc[a].at[self.layers[a], c], dst, sems[2].at[a, m - 1], sems[3].at[a, m - 1], _dev(k ^ m, c))

    def _forward(self, out, sems, a, m, arrival):
        x, y, c, k = _place()
        got = out[a].at[k ^ m, 1 - c] if arrival else out[a].at[k ^ m, c]
        return _remote(got, got, sems[4].at[a, m - 1], sems[5].at[a, m - 1], (x, y, 1 - c))

    def start(self, loc, out, sems):
        for a in range(len(self.args)):
            for m in range(1, N_CHIPS):
                self._ici(loc, out, sems, a, m, False).start()
            self._own(loc, out, sems, a).start()

    def finish(self, loc, out, sems):
        n = len(self.args)
        for a in range(n):
            for m in range(1, N_CHIPS):
                self._ici(loc, out, sems, a, m, True).wait_recv()
                self._forward(out, sems, a, m, False).start()
        for a in range(n):
            for m in range(1, N_CHIPS):
                self._forward(out, sems, a, m, True).wait_recv()
            self._own(loc, out, sems, a).wait_recv()
        for a in range(n):
            for m in range(1, N_CHIPS):
                self._ici(loc, out, sems, a, m, False).wait_send()
                self._forward(out, sems, a, m, False).wait_send()
            self._own(loc, out, sems, a).wait_send()


class _ChipExchange:
    def __init__(self, parts):
        n = len(parts)
        self.args = list(parts)
        self.out_shape = [jax.ShapeDtypeStruct((N_CHIPS - 1,) + p.shape[1:], p.dtype) for p in parts]
        self.sems = [DMA((n, 3)), DMA((n, 3))]

    def _copy(self, p, land, sems, a, m):
        x, y, c, k = _place()
        return _remote(p[a].at[k ^ m], land[a].at[m - 1], sems[0].at[a, m - 1], sems[1].at[a, m - 1], _dev(k ^ m, c))

    def start(self, p, land, sems):
        for a in range(len(self.args)):
            for m in range(1, N_CHIPS):
                self._copy(p, land, sems, a, m).start()

    def finish(self, p, land, sems):
        for a in range(len(self.args)):
            for m in range(1, N_CHIPS):
                self._copy(p, land, sems, a, m).wait_recv()
        for a in range(len(self.args)):
            for m in range(1, N_CHIPS):
                self._copy(p, land, sems, a, m).wait_send()


def _hosted(body, comm, *, name, grid, in_specs, out_specs, out_shape, args, scratch_shapes=()):
    ni, no, ns = len(in_specs), len(out_shape), len(scratch_shapes)
    if comm is None:
        res = pl.pallas_call(
            body, name=name, grid=grid, in_specs=list(in_specs), out_specs=list(out_specs), out_shape=list(out_shape),
            scratch_shapes=list(scratch_shapes),
            compiler_params=pltpu.CompilerParams(dimension_semantics=("arbitrary",) * len(grid), vmem_limit_bytes=VMEM_LIMIT),
        )(*args)
        return list(res), []
    nc, nco = len(comm.args), len(comm.out_shape)

    def full(*refs):
        cin = refs[ni:ni + nc]
        outs = refs[ni + nc:ni + nc + no]
        cout = refs[ni + nc + no:ni + nc + no + nco]
        scratch = refs[ni + nc + no + nco:ni + nc + no + nco + ns]
        csems = refs[ni + nc + no + nco + ns:]
        first = _all_of([pl.program_id(ax) == 0 for ax in range(len(grid))])
        last = _all_of([pl.program_id(ax) == grid[ax] - 1 for ax in range(len(grid))])

        @pl.when(first)
        def _():
            comm.start(cin, cout, csems)

        body(*refs[:ni], *outs, *scratch)

        @pl.when(last)
        def _():
            comm.finish(cin, cout, csems)

    res = pl.pallas_call(
        full, name=name, grid=grid, in_specs=[*in_specs, *[ANY] * nc], out_specs=[*out_specs, *[ANY] * nco],
        out_shape=[*out_shape, *comm.out_shape], scratch_shapes=[*scratch_shapes, *comm.sems],
        compiler_params=pltpu.CompilerParams(dimension_semantics=("arbitrary",) * len(grid), vmem_limit_bytes=VMEM_LIMIT,
                                             has_side_effects=True),
    )(*args, *comm.args)
    return list(res[:no]), list(res[no:])


def _all_of(conds):
    out = conds[0]
    for c in conds[1:]:
        out = jnp.logical_and(out, c)
    return out


def _alone(comm, name):
    return _hosted(lambda: None, comm, name=name, grid=(1,), in_specs=[], out_specs=[], out_shape=[], args=[])[1]


def _sibling_exchange(grads, small, name):
    n = len(grads)
    ns = 0 if small is None else 1

    def body(*refs):
        g = refs[:n]
        land = refs[n + ns:2 * n + ns]
        send_sems, recv_sems, own_sem, ssend_sems, srecv_sems = refs[2 * n + 2 * ns:]
        x, y, c, k = _place()
        me = 2 * k + c
        cps = []
        for a in range(n):
            cp = _remote(g[a].at[:, pl.ds(1 - c, 1)], land[a], send_sems.at[a], recv_sems.at[a], (x, y, 1 - c))
            cp.start()
            cps.append(cp)
        if small is not None:
            sm, smg = refs[n], refs[2 * n + 1]
            peers = [me ^ m for m in range(1, N_DEV)]
            ids = [(p // 4, (p // 2) % 2, p % 2) for p in peers]
            own = pltpu.make_async_copy(sm, smg.at[me], own_sem)
            own.start()
            for m in range(1, N_DEV):
                cp = _remote(sm, smg.at[me], ssend_sems.at[m - 1], srecv_sems.at[m - 1], ids[m - 1])
                cp.start()
                cps.append(cp)
            for m in range(1, N_DEV):
                _remote(sm, smg.at[peers[m - 1]], ssend_sems.at[m - 1], srecv_sems.at[m - 1], ids[m - 1]).wait_recv()
            own.wait()
        for cp in cps[:n]:
            cp.wait_recv()
        for cp in cps:
            cp.wait_send()

    out_shape = [jax.ShapeDtypeStruct((N_CHIPS, 1) + a.shape[2:], a.dtype) for a in grads]
    ins = list(grads)
    if small is not None:
        out_shape.append(jax.ShapeDtypeStruct((N_DEV,) + small.shape, small.dtype))
        ins.append(small)
    return pl.pallas_call(
        body, name=name, out_shape=out_shape, in_specs=[ANY] * (n + ns), out_specs=[ANY] * (n + ns),
        scratch_shapes=[DMA((n,)), DMA((n,)), DMA, DMA((N_DEV - 1,)), DMA((N_DEV - 1,))],
        compiler_params=pltpu.CompilerParams(has_side_effects=True),
    )(*ins)


def _sibling_share(halves, col_half):
    n = len(halves)

    def body(*refs):
        out = refs[n:2 * n]
        send_sems, recv_sems = refs[2 * n:]
        x, y, c, k = _place()

        def half(a, core):
            if not col_half[a]:
                return out[a].at[:, pl.ds(core, 1)]
            cols = out[a].shape[-1] // 2
            return out[a].at[:, :, pl.ds(pl.multiple_of(core * cols, cols), cols)]

        cps = []
        for a in range(n):
            cp = _remote(half(a, c), half(a, c), send_sems.at[a], recv_sems.at[a], (x, y, 1 - c))
            cp.start()
            cps.append(cp)
        for a in range(n):
            _remote(half(a, 1 - c), half(a, 1 - c), send_sems.at[a], recv_sems.at[a], (x, y, 1 - c)).wait_recv()
        for cp in cps:
            cp.wait_send()

    out_shape = [jax.ShapeDtypeStruct(a.shape, a.dtype) for a in halves]
    return pl.pallas_call(
        body, name="rs_sibling_share", out_shape=out_shape, in_specs=[ANY] * n, out_specs=[ANY] * n,
        input_output_aliases={a: a for a in range(n)}, scratch_shapes=[DMA((n,)), DMA((n,))],
        compiler_params=pltpu.CompilerParams(has_side_effects=True),
    )(*halves)


def _add_sibling(g, land, core):
    _, _, r, c = g.shape
    rb = _row_block(r, c * (3 * 2 * 2 + 2 * 4))

    def body(core_ref, g_ref, l_ref, o_ref):
        o_ref[...] = (g_ref[...].astype(F32) + l_ref[...].astype(F32)).astype(o_ref.dtype)

    return pl.pallas_call(
        body, name="rs_add_sibling", out_shape=jax.ShapeDtypeStruct((N_CHIPS, r, c), g.dtype),
        grid_spec=pltpu.PrefetchScalarGridSpec(
            num_scalar_prefetch=1, grid=(N_CHIPS, r // rb),
            in_specs=[pl.BlockSpec((None, None, rb, c), lambda j, i, core_ref: (j, core_ref[0], i, 0)),
                      pl.BlockSpec((None, None, rb, c), lambda j, i, core_ref: (j, 0, i, 0))],
            out_specs=pl.BlockSpec((None, rb, c), lambda j, i, core_ref: (j, i, 0))),
        compiler_params=pltpu.CompilerParams(dimension_semantics=("parallel", "parallel"), vmem_limit_bytes=VMEM_LIMIT),
    )(core, g, land)


def _add_chips(part, land, where, layer=0, n_layers=1, into=None, col_half=False):
    _, r, c = part.shape
    rb = _row_block(r, c * (4 * 2 * 2 + 4 * 2 + 2 * 4))

    def body(where_ref, p_ref, l_ref, *rest):
        acc = p_ref[...].astype(F32)
        for m in range(N_CHIPS - 1):
            acc = acc + l_ref[m].astype(F32)
        rest[-1][...] = acc

    in_specs = [pl.BlockSpec((None, rb, c), lambda i, where_ref: (where_ref[0], i, 0)),
                pl.BlockSpec((N_CHIPS - 1, rb, c), lambda i, where_ref: (0, i, 0))]
    args = [where, part, land]
    if into is not None:
        in_specs.append(ANY)
        args.append(into)
    if col_half:
        out_shape = jax.ShapeDtypeStruct((n_layers, r, 2 * c), F32)
        out_spec = pl.BlockSpec((None, rb, c), lambda i, where_ref: (layer, i, where_ref[1]))
    else:
        out_shape = jax.ShapeDtypeStruct((n_layers, 2, r, c), F32)
        out_spec = pl.BlockSpec((None, None, rb, c), lambda i, where_ref: (layer, where_ref[1], i, 0))
    return pl.pallas_call(
        body, name="rs_add_chips", out_shape=out_shape,
        grid_spec=pltpu.PrefetchScalarGridSpec(
            num_scalar_prefetch=1, grid=(r // rb,), in_specs=in_specs, out_specs=out_spec),
        input_output_aliases={} if into is None else {3: 0},
        compiler_params=pltpu.CompilerParams(dimension_semantics=("parallel",), vmem_limit_bytes=VMEM_LIMIT),
    )(*args)


def _sum_small(smg):
    def body(s_ref, o_ref):
        acc = s_ref[0]
        for j in range(1, N_DEV):
            acc = acc + s_ref[j]
        o_ref[...] = acc

    return pl.pallas_call(body, name="rs_sum_small", out_shape=jax.ShapeDtypeStruct(smg.shape[1:], F32))(smg)


def _pool_windows(ext_ref, g, gw, tm, first_row):
    w = POOL_WINDOWS[g]
    slab = ext_ref[:, g * gw:(g + 1) * gw]
    p, k = slab, 1
    while k < w:
        p = p + pltpu.roll(p, k, 0)
        k *= 2
    t = first_row + lax.broadcasted_iota(jnp.int32, (tm, 1), 0)
    cnt = jnp.minimum(t + 1, w).astype(F32)
    return p[POOL_HALO:] / cnt - slab[POOL_HALO:]


def _fwd_pool(x, small, scale, poolw, comm=None):
    t, d = x.shape
    tm = _token_tile(t)
    gw = d // len(POOL_WINDOWS)

    def body(x_ref, sm_ref, sc_ref, w_ref, h_ref, ext_ref, mix_ref):
        i = pl.program_id(0)

        @pl.when(i == 0)
        def _():
            ext_ref[0:POOL_HALO, :] = jnp.zeros((POOL_HALO, d), F32)

        @pl.when(i > 0)
        def _():
            ext_ref[0:POOL_HALO, :] = ext_ref[tm:tm + POOL_HALO, :]

        xv = x_ref[...]
        xh, _ = _rms(xv)
        ext_ref[POOL_HALO:, :] = xh * sm_ref[0:1, :]
        for g in range(len(POOL_WINDOWS)):
            pooled = _pool_windows(ext_ref, g, gw, tm, i * tm)
            cols = slice(g * gw, (g + 1) * gw)
            mix_ref[:, cols] = _dot(pooled.astype(BF16), w_ref[g]) * sc_ref[:, cols]
        mh, _ = _rms(mix_ref[...])
        h_ref[...] = xv + mh * sm_ref[1:2, :]

    (h,), got = _hosted(
        body, comm, name="fwd_pool", grid=(t // tm,), out_shape=[jax.ShapeDtypeStruct((t, d), F32)],
        in_specs=[pl.BlockSpec((tm, d), lambda i: (i, 0)), _resident(small.shape, lambda i: (0, 0)),
                  _resident(scale.shape, lambda i: (0, 0)), _resident(poolw.shape, lambda i: (0, 0, 0))],
        out_specs=[pl.BlockSpec((tm, d), lambda i: (i, 0))],
        scratch_shapes=[pltpu.VMEM((POOL_HALO + tm, d), F32), pltpu.VMEM((tm, d), F32)],
        args=[x, small, scale, poolw])
    return h, got


def _fwd_ffn(h, small, wgu, wd, layer, comm=None, target=None):
    t, d = h.shape
    tm = _token_tile(t)
    steps = t // tm
    fc = wgu.shape[-1]
    f = 2 * fc
    g_in, g_out = 4 * layer + 2, 4 * layer + 3
    with_loss = target is not None

    def body(h_ref, *refs):
        if with_loss:
            t_ref, sm_ref, wgu_ref, wd_ref, o_ref, gu_ref, ff_ref, n_ref, l_ref, acc_ref = refs
        else:
            sm_ref, wgu_ref, wd_ref, o_ref, gu_ref, ff_ref, n_ref = refs
        hv = h_ref[...]
        hh, _ = _rms(hv)
        n = (hh * sm_ref[g_in:g_in + 1, :]).astype(BF16)
        n_ref[...] = n
        ff = None
        for j in range(2):
            gate = _dot(n, wgu_ref[j])
            up = _dot(n, wgu_ref[2 + j])
            gu_ref[:, j * fc:(j + 1) * fc] = gate.astype(BF16)
            gu_ref[:, f + j * fc:f + (j + 1) * fc] = up.astype(BF16)
            act = (gate * jax.nn.sigmoid(gate) * up).astype(BF16)
            part = _dot(act, wd_ref[j * fc:(j + 1) * fc, :])
            ff = part if ff is None else ff + part
        ff_ref[...] = ff
        fh, _ = _rms(ff)
        out = hv + fh * sm_ref[g_out:g_out + 1, :]
        if not with_loss:
            o_ref[...] = out
            return
        i = pl.program_id(0)
        e = out - t_ref[...]
        o_ref[...] = e * (1.0 / d)

        @pl.when(i == 0)
        def _():
            acc_ref[...] = jnp.zeros_like(acc_ref)

        acc_ref[...] += _colsum(e * e)

        @pl.when(i == steps - 1)
        def _():
            l_ref[...] = jnp.full(l_ref.shape, 0.5 / d, F32) * jnp.sum(acc_ref[...])

    row = lambda i: (i, 0)
    out_shape = [jax.ShapeDtypeStruct((t, d), F32), jax.ShapeDtypeStruct((t, 2 * f), BF16),
                 jax.ShapeDtypeStruct((t, d), F32), jax.ShapeDtypeStruct((t, d), BF16)]
    out_specs = [pl.BlockSpec((tm, d), row), pl.BlockSpec((tm, 2 * f), row), pl.BlockSpec((tm, d), row),
                 pl.BlockSpec((tm, d), row)]
    weight_specs = [_resident(small.shape, lambda i: (0, 0)), _resident(wgu.shape, lambda i: (0, 0, 0)),
                    _resident(wd.shape, lambda i: (0, 0))]
    if with_loss:
        return _hosted(
            body, comm, name=f"fwd_ffn{layer}_loss", grid=(steps,),
            out_shape=out_shape + [jax.ShapeDtypeStruct((8, 128), F32)],
            in_specs=[pl.BlockSpec((tm, d), row), pl.BlockSpec((tm, d), row)] + weight_specs,
            out_specs=out_specs + [pl.BlockSpec((8, 128), lambda i: (0, 0))],
            scratch_shapes=[pltpu.VMEM((1, d), F32)], args=[h, target, small, wgu, wd])
    return _hosted(
        body, comm, name=f"fwd_ffn{layer}", grid=(steps,), out_shape=out_shape,
        in_specs=[pl.BlockSpec((tm, d), row)] + weight_specs, out_specs=out_specs, args=[h, small, wgu, wd])


def _fwd_conv(h, small, win, wout, comm=None):
    t, d = h.shape
    tm = _token_tile(t)
    pc = win.shape[-1]

    def body(h_ref, sm_ref, win_ref, wout_ref, o_ref, proj_ref, y_ref, n_ref, pj_ref, uext_ref):
        i = pl.program_id(0)

        @pl.when(i == 0)
        def _():
            uext_ref[0:CONV_HALO, :] = jnp.zeros((CONV_HALO, d), F32)

        @pl.when(i > 0)
        def _():
            uext_ref[0:CONV_HALO, :] = uext_ref[tm:tm + CONV_HALO, :]

        hv = h_ref[...]
        hh, _ = _rms(hv)
        n = (hh * sm_ref[4:5, :]).astype(BF16)
        n_ref[...] = n
        for k in range(N_CHIPS):
            pj_ref[:, k * pc:(k + 1) * pc] = _dot(n, win_ref[k])
        proj_ref[...] = pj_ref[...].astype(BF16)
        uext_ref[CONV_HALO:, :] = pj_ref[:, d:2 * d] * pj_ref[:, 2 * d:]
        taps = [sm_ref[8 + j:9 + j, :] for j in range(3)]
        full = uext_ref[...]
        conv = (full[CONV_HALO:] * taps[2] + pltpu.roll(full, 1, 0)[CONV_HALO:] * taps[1]
                + pltpu.roll(full, 2, 0)[CONV_HALO:] * taps[0])
        y = _dot((pj_ref[:, 0:d] * conv).astype(BF16), wout_ref[...])
        y_ref[...] = y
        yh, _ = _rms(y)
        o_ref[...] = hv + yh * sm_ref[5:6, :]

    row = lambda i: (i, 0)
    return _hosted(
        body, comm, name="fwd_conv", grid=(t // tm,),
        out_shape=[jax.ShapeDtypeStruct((t, d), F32), jax.ShapeDtypeStruct((t, 3 * d), BF16),
                   jax.ShapeDtypeStruct((t, d), F32), jax.ShapeDtypeStruct((t, d), BF16)],
        in_specs=[pl.BlockSpec((tm, d), row), _resident(small.shape, lambda i: (0, 0)),
                  _resident(win.shape, lambda i: (0, 0, 0)), _resident(wout.shape, lambda i: (0, 0))],
        out_specs=[pl.BlockSpec((tm, d), row), pl.BlockSpec((tm, 3 * d), row), pl.BlockSpec((tm, d), row),
                   pl.BlockSpec((tm, d), row)],
        scratch_shapes=[pltpu.VMEM((tm, 3 * d), F32), pltpu.VMEM((CONV_HALO + tm, d), F32)],
        args=[h, small, win, wout])


def _bwd_ffn(dh, h, ff, gu, small, wgu, wd, layer, comm=None):
    t, d = h.shape
    tm = _token_tile(t, 256)
    fc = wgu.shape[-1]
    f = 2 * fc
    g_in, g_out = 4 * layer + 2, 4 * layer + 3

    def body(dh_ref, h_ref, ff_ref, gu_ref, sm_ref, wgu_ref, wd_ref, o_ref, dgu_ref, dff_ref, act_ref, sg_ref):
        i = pl.program_id(0)

        @pl.when(i == 0)
        def _():
            sg_ref[...] = jnp.zeros_like(sg_ref)

        dy = dh_ref[...]
        fh, r3 = _rms(ff_ref[...])
        sg_ref[1:2, :] += _colsum(dy * fh)
        dff = _rms_bwd(dy, fh, r3, sm_ref[g_out:g_out + 1, :]).astype(BF16)
        dff_ref[...] = dff
        for j in range(2):
            dact = _dot_nt(dff, wd_ref[j * fc:(j + 1) * fc, :])
            gate = gu_ref[:, j * fc:(j + 1) * fc].astype(F32)
            up = gu_ref[:, f + j * fc:f + (j + 1) * fc].astype(F32)
            sig = jax.nn.sigmoid(gate)
            silu = gate * sig
            act_ref[:, j * fc:(j + 1) * fc] = (silu * up).astype(BF16)
            dgu_ref[:, j * fc:(j + 1) * fc] = (dact * up * (sig * (1.0 + gate * (1.0 - sig)))).astype(BF16)
            dgu_ref[:, f + j * fc:f + (j + 1) * fc] = (dact * silu).astype(BF16)
        dn = None
        for k in range(N_CHIPS):
            part = _dot_nt(dgu_ref[:, k * fc:(k + 1) * fc], wgu_ref[k])
            dn = part if dn is None else dn + part
        hh, r2 = _rms(h_ref[...])
        sg_ref[0:1, :] += _colsum(dn * hh)
        o_ref[...] = dy + _rms_bwd(dn, hh, r2, sm_ref[g_in:g_in + 1, :])

    row = lambda i: (i, 0)
    return _hosted(
        body, comm, name=f"bwd_ffn{layer}", grid=(t // tm,),
        out_shape=[jax.ShapeDtypeStruct((t, d), F32), jax.ShapeDtypeStruct((t, 2 * f), BF16),
                   jax.ShapeDtypeStruct((t, d), BF16), jax.ShapeDtypeStruct((t, f), BF16),
                   jax.ShapeDtypeStruct((8, d), F32)],
        in_specs=[pl.BlockSpec((tm, d), row), pl.BlockSpec((tm, d), row), pl.BlockSpec((tm, d), row),
                  pl.BlockSpec((tm, 2 * f), row), _resident(small.shape, lambda i: (0, 0)),
                  _resident(wgu.shape, lambda i: (0, 0, 0)), _resident(wd.shape, lambda i: (0, 0))],
        out_specs=[pl.BlockSpec((tm, d), row), pl.BlockSpec((tm, 2 * f), row), pl.BlockSpec((tm, d), row),
                   pl.BlockSpec((tm, f), row), pl.BlockSpec((8, d), lambda i: (0, 0))],
        args=[dh, h, ff, gu, small, wgu, wd])


def _bwd_conv(dh, h, y, proj, small, win, wout, comm=None):
    t, d = h.shape
    tm = _token_tile(t)
    steps = t // tm
    pc = win.shape[-1]
    halo_blocks = tm // 16

    def body(dh_ref, h_ref, y_ref, proj_ref, halo_ref, sm_ref, win_ref, wout_ref,
             o_ref, dproj_ref, dy_ref, bc_ref, sg_ref, uext_ref, dcext_ref, carry_ref):
        i = pl.program_id(0)
        tile = steps - 1 - i

        @pl.when(i == 0)
        def _():
            sg_ref[...] = jnp.zeros_like(sg_ref)
            carry_ref[...] = jnp.zeros_like(carry_ref)

        dy = dh_ref[...]
        yh, r1 = _rms(y_ref[...])
        sg_ref[1:2, :] += _colsum(dy * yh)
        dyv = _rms_bwd(dy, yh, r1, sm_ref[5:6, :]).astype(BF16)
        dy_ref[...] = dyv
        dbc = _dot_nt(dyv, wout_ref[...])
        b = proj_ref[:, 0:d].astype(F32)
        cg = proj_ref[:, d:2 * d].astype(F32)
        v = proj_ref[:, 2 * d:].astype(F32)
        halo = halo_ref[...].astype(F32)[16 - CONV_HALO:]
        uh = halo[:, d:2 * d] * halo[:, 2 * d:]
        uext_ref[0:CONV_HALO, :] = jnp.where(tile > 0, uh, jnp.zeros_like(uh))
        uext_ref[CONV_HALO:, :] = cg * v
        taps = [sm_ref[8 + j:9 + j, :] for j in range(3)]
        full = uext_ref[...]
        u0 = full[CONV_HALO:]
        u1 = pltpu.roll(full, 1, 0)[CONV_HALO:]
        u2 = pltpu.roll(full, 2, 0)[CONV_HALO:]
        conv = u0 * taps[2] + u1 * taps[1] + u2 * taps[0]
        bc_ref[...] = (b * conv).astype(BF16)
        dconv = dbc * b
        sg_ref[4:5, :] += _colsum(dconv * u0)
        sg_ref[3:4, :] += _colsum(dconv * u1)
        sg_ref[2:3, :] += _colsum(dconv * u2)
        dcext_ref[0:tm, :] = dconv
        dcext_ref[tm:, :] = carry_ref[...]
        carry_ref[...] = dconv[0:CONV_HALO]
        dfull = dcext_ref[...]
        n8 = tm + CONV_HALO
        du = (dfull[0:tm] * taps[2] + pltpu.roll(dfull, n8 - 1, 0)[0:tm] * taps[1]
              + pltpu.roll(dfull, n8 - 2, 0)[0:tm] * taps[0])
        dproj_ref[:, 0:d] = (dbc * conv).astype(BF16)
        dproj_ref[:, d:2 * d] = (du * v).astype(BF16)
        dproj_ref[:, 2 * d:] = (du * cg).astype(BF16)
        dn = None
        for k in range(N_CHIPS):
            part = _dot_nt(dproj_ref[:, k * pc:(k + 1) * pc], win_ref[k])
            dn = part if dn is None else dn + part
        hh, r0 = _rms(h_ref[...])
        sg_ref[0:1, :] += _colsum(dn * hh)
        o_ref[...] = dy + _rms_bwd(dn, hh, r0, sm_ref[4:5, :])

    rev = lambda i: (steps - 1 - i, 0)
    before = lambda i: (jnp.maximum((steps - 1 - i) * halo_blocks - 1, 0), 0)
    return _hosted(
        body, comm, name="bwd_conv", grid=(steps,),
        out_shape=[jax.ShapeDtypeStruct((t, d), F32), jax.ShapeDtypeStruct((t, 3 * d), BF16),
                   jax.ShapeDtypeStruct((t, d), BF16), jax.ShapeDtypeStruct((t, d), BF16),
                   jax.ShapeDtypeStruct((8, d), F32)],
        in_specs=[pl.BlockSpec((tm, d), rev), pl.BlockSpec((tm, d), rev), pl.BlockSpec((tm, d), rev),
                  pl.BlockSpec((tm, 3 * d), rev), pl.BlockSpec((16, 3 * d), before),
                  _resident(small.shape, lambda i: (0, 0)), _resident(win.shape, lambda i: (0, 0, 0)),
                  _resident(wout.shape, lambda i: (0, 0))],
        out_specs=[pl.BlockSpec((tm, d), rev), pl.BlockSpec((tm, 3 * d), rev), pl.BlockSpec((tm, d), rev),
                   pl.BlockSpec((tm, d), rev), pl.BlockSpec((8, d), lambda i: (0, 0))],
        scratch_shapes=[pltpu.VMEM((CONV_HALO + tm, d), F32), pltpu.VMEM((tm + CONV_HALO, d), F32),
                        pltpu.VMEM((CONV_HALO, d), F32)],
        args=[dh, h, y, proj, proj, small, win, wout])


def _bwd_pool(dh, x, small, scale, poolw, comm=None):
    t, d = x.shape
    tm = _token_tile(t)
    steps = t // tm
    ng = len(POOL_WINDOWS)
    gw = d // ng
    halo_blocks = tm // POOL_HALO

    def body(dh_ref, x_ref, halo_ref, sm_ref, sc_ref, w_ref, o_ref, dw_ref, sg_ref,
             ext_ref, mix_ref, mm_ref, pb_ref, qext_ref, dhn_ref, carry_ref):
        i = pl.program_id(0)
        tile = steps - 1 - i

        @pl.when(i == 0)
        def _():
            sg_ref[...] = jnp.zeros_like(sg_ref)
            dw_ref[...] = jnp.zeros_like(dw_ref)
            carry_ref[...] = jnp.zeros_like(carry_ref)

        g0 = sm_ref[0:1, :]
        xv = x_ref[...]
        xh, r0 = _rms(xv)
        hx, _ = _rms(halo_ref[...])
        ext_ref[0:POOL_HALO, :] = jnp.where(tile > 0, hx * g0, jnp.zeros_like(hx))
        ext_ref[POOL_HALO:, :] = xh * g0
        for g in range(ng):
            pooled = _pool_windows(ext_ref, g, gw, tm, tile * tm)
            cols = slice(g * gw, (g + 1) * gw)
            pb = pooled.astype(BF16)
            pb_ref[:, cols] = pb
            mm = _dot(pb, w_ref[g])
            mm_ref[:, cols] = mm
            mix_ref[:, cols] = mm * sc_ref[:, cols]
        dy = dh_ref[...]
        mh, r1 = _rms(mix_ref[...])
        sg_ref[1:2, :] += _colsum(dy * mh)
        dmix = _rms_bwd(dy, mh, r1, sm_ref[1:2, :])
        sg_ref[2:3, :] += _colsum(dmix * mm_ref[...])
        mix_ref[...] = dmix * sc_ref[...]
        n16 = tm + POOL_HALO
        for g in range(ng):
            w = POOL_WINDOWS[g]
            cols = slice(g * gw, (g + 1) * gw)
            dmm = mix_ref[:, cols].astype(BF16)
            dpooled = _dot_nt(dmm, w_ref[g])
            dw_ref[g] += _dot_tn(pb_ref[:, cols], dmm)
            trow = tile * tm + lax.broadcasted_iota(jnp.int32, (tm, 1), 0)
            q = dpooled / jnp.minimum(trow + 1, w).astype(F32)
            qext_ref[0:tm, cols] = q
            qext_ref[tm:, cols] = carry_ref[:, cols]
            carry_ref[:, cols] = q[0:POOL_HALO]
            p, k = qext_ref[:, cols], 1
            while k < w:
                p = p + pltpu.roll(p, n16 - k, 0)
                k *= 2
            dhn_ref[:, cols] = p[0:tm] - dpooled
        dhn = dhn_ref[...]
        sg_ref[0:1, :] += _colsum(dhn * xh)
        o_ref[...] = dy + _rms_bwd(dhn, xh, r0, g0)

    rev = lambda i: (steps - 1 - i, 0)
    before = lambda i: (jnp.maximum((steps - 1 - i) * halo_blocks - 1, 0), 0)
    return _hosted(
        body, comm, name="bwd_pool", grid=(steps,),
        out_shape=[jax.ShapeDtypeStruct((t, d), F32), jax.ShapeDtypeStruct((ng, gw, gw), F32),
                   jax.ShapeDtypeStruct((8, d), F32)],
        in_specs=[pl.BlockSpec((tm, d), rev), pl.BlockSpec((tm, d), rev), pl.BlockSpec((POOL_HALO, d), before),
                  _resident(small.shape, lambda i: (0, 0)), _resident(scale.shape, lambda i: (0, 0)),
                  _resident(poolw.shape, lambda i: (0, 0, 0))],
        out_specs=[pl.BlockSpec((tm, d), rev), pl.BlockSpec((ng, gw, gw), lambda i: (0, 0, 0)),
                   pl.BlockSpec((8, d), lambda i: (0, 0))],
        scratch_shapes=[pltpu.VMEM((POOL_HALO + tm, d), F32), pltpu.VMEM((tm, d), F32), pltpu.VMEM((tm, d), F32),
                        pltpu.VMEM((tm, d), BF16), pltpu.VMEM((tm + POOL_HALO, d), F32), pltpu.VMEM((tm, d), F32),
                        pltpu.VMEM((POOL_HALO, d), F32)],
        args=[dh, x, x, small, scale, poolw])


def _weight_grad(a, b, bm, bn, half_on, name, comm=None):
    t, m = a.shape
    _, n = b.shape
    if half_on == "a":
        a_cols, b_cols = 2 * bm, bn
    else:
        a_cols, b_cols = bm, 2 * bn
    steps = max(m // a_cols, n // b_cols)

    def spec(cols, total):
        if cols == total:
            return _resident((t, cols), lambda p, j: (0, 0))
        return pl.BlockSpec((t, cols), lambda p, j: (0, j))

    def tile(a_ref, b_ref, half):
        if half_on == "a":
            return _dot_tn(a_ref[:, half * bm:(half + 1) * bm], b_ref[...])
        return _dot_tn(a_ref[...], b_ref[:, half * bn:(half + 1) * bn])

    def body(a_ref, b_ref, parts_ref, land_ref, acc_ref, stage_ref, got_ref, send_sems, recv_sems, got_sem):
        p, j = pl.program_id(0), pl.program_id(1)
        x, y, c, _ = _place()
        half = jnp.where(p == 0, 1 - c, c)
        for hv in range(2):
            @pl.when(half == hv)
            def _():
                acc_ref[...] = tile(a_ref, b_ref, hv)

        def send(jj):
            return _remote(stage_ref.at[jj % 2], land_ref.at[jj], send_sems.at[jj], recv_sems.at[jj], (x, y, 1 - c))

        @pl.when(p == 0)
        def _():
            @pl.when(j >= 2)
            def _():
                send(j - 2).wait_send()

            stage_ref[j % 2] = acc_ref[...].astype(BF16)
            send(j).start()

        @pl.when(p == 1)
        def _():
            @pl.when(j == 0)
            def _():
                for jj in range(max(steps - 2, 0), steps):
                    send(jj).wait_send()

            send(j).wait_recv()
            fetch = pltpu.make_async_copy(land_ref.at[j], got_ref, got_sem)
            fetch.start()
            fetch.wait()
            parts_ref[...] = (acc_ref[...] + got_ref[...].astype(F32)).astype(BF16)

    (parts, _), got = _hosted(
        body, comm, name=name, grid=(2, steps),
        out_shape=[jax.ShapeDtypeStruct((steps, bm, bn), BF16), jax.ShapeDtypeStruct((steps, bm, bn), BF16)],
        in_specs=[spec(a_cols, m), spec(b_cols, n)],
        out_specs=[pl.BlockSpec((None, bm, bn), lambda p, j: (p * j, 0, 0)), ANY],
        scratch_shapes=[pltpu.VMEM((bm, bn), F32), pltpu.VMEM((2, bm, bn), BF16), pltpu.VMEM((bm, bn), BF16),
                        DMA((steps,)), DMA((steps,)), DMA],
        args=[a, b])
    return parts, got


def _adamw(w, g, m, v, name):
    r, c = w.shape
    rb = _row_block(r, c * (7 * 4 * 2 + 4 * 4))
    bc1 = 1.0 - ADAM_B1 ** ADAM_STEP
    bc2 = 1.0 - ADAM_B2 ** ADAM_STEP

    def body(w_ref, g_ref, m_ref, v_ref, d_ref, nm_ref, nv_ref):
        gv = g_ref[...]
        nm = ADAM_B1 * m_ref[...] + (1.0 - ADAM_B1) * gv
        nv = ADAM_B2 * v_ref[...] + (1.0 - ADAM_B2) * (gv * gv)
        nm_ref[...] = nm
        nv_ref[...] = nv
        d_ref[...] = -ADAM_LR * ((nm / bc1) / (jnp.sqrt(nv / bc2) + ADAM_EPS) + ADAM_WD * w_ref[...])

    spec = pl.BlockSpec((rb, c), lambda i: (i, 0))
    return pl.pallas_call(
        body, name=name, grid=(r // rb,), out_shape=[jax.ShapeDtypeStruct((r, c), F32)] * 3,
        in_specs=[spec] * 4, out_specs=[spec] * 3,
        compiler_params=pltpu.CompilerParams(dimension_semantics=("parallel",), vmem_limit_bytes=VMEM_LIMIT),
    )(w, g, m, v)


def kernel(x, norm_gains, pool_w, pool_scale, conv_in_w, conv_w, conv_out_w, ffn_gate_up_w, ffn_down_w, loss_target, m_norm_gains, m_pool_w, m_pool_scale, m_conv_in_w, m_conv_w, m_conv_out_w, m_ffn_gate_up_w, m_ffn_down_w, v_norm_gains, v_pool_w, v_pool_scale, v_conv_in_w, v_conv_w, v_conv_out_w, v_ffn_gate_up_w, v_ffn_down_w):
    _, t, d = x.shape
    dq = d // N_CHIPS
    ng = len(POOL_WINDOWS)
    gw = d // ng
    fq = ffn_down_w.shape[1]
    f = N_CHIPS * fq
    fc = f // 2
    core = lax.axis_index("c")
    chip = 2 * lax.axis_index("x") + lax.axis_index("y")
    core_arr = jnp.reshape(core, (1,)).astype(jnp.int32)
    where_arr = jnp.stack([chip, core]).astype(jnp.int32)
    x2, target = x[0], loss_target[0]

    small_loc = jnp.concatenate(
        [norm_gains.reshape(8, dq), conv_w[0], jnp.zeros((5, dq), F32)], axis=0).reshape(1, 2, 8, dq)
    pool_loc = pool_w.astype(BF16).reshape(1, 2, ng // 2 * (gw // N_CHIPS), gw)
    wgu_loc = ffn_gate_up_w.astype(BF16).reshape(2, 2, d // 2, fc)
    wd_loc = ffn_down_w.astype(BF16).reshape(2, 2, fq // 2, d)
    win_loc = conv_in_w.astype(BF16).reshape(1, 2, d // 2, -1)
    wout_loc = conv_out_w.astype(BF16).reshape(1, 2, dq // 2, d)

    def ffn_weights(wgu_f, wd_f):
        return wgu_f.reshape(N_CHIPS, d, fc), wd_f.reshape(f, d)

    pool_f, small_f = _alone(_Gather([(pool_loc, 0), (small_loc, 0)]), "ag_first")
    poolw = pool_f.reshape(N_CHIPS, ng, gw // N_CHIPS, gw).transpose(1, 0, 2, 3).reshape(ng, gw, gw)
    small = small_f.transpose(1, 2, 0, 3).reshape(16, d)
    h1, got = _fwd_pool(x2, small, pool_scale, poolw, _Gather([(wgu_loc, 0), (wd_loc, 0)]))
    wgu0, wd0 = ffn_weights(*got)
    (h2, gu0, ff0, n0), got = _fwd_ffn(h1, small, wgu0, wd0, 0, _Gather([(win_loc, 0), (wout_loc, 0), (wgu_loc, 1)]))
    win_f, wout_f = got[0].reshape(N_CHIPS, d, -1), got[1].reshape(d, d)
    (h3, proj, y, nc), got_d = _fwd_conv(h2, small, win_f, wout_f, _Gather([(wd_loc, 1)]))
    wgu1, wd1 = ffn_weights(got[2], got_d[0])
    (dh4, gu1, ff1, n1, loss_blk), _ = _fwd_ffn(h3, small, wgu1, wd1, 1, target=target)

    (dh3, dgu1, dff1, act1, sg_f1), _ = _bwd_ffn(dh4, h3, ff1, gu1, small, wgu1, wd1, 1)
    parts_d1, _ = _weight_grad(act1, dff1, fc, d // 2, "b", "dw_down1")
    parts_d1 = parts_d1.reshape(N_CHIPS, fq, d // 2)
    parts_gu1, recv_d1 = _weight_grad(n1, dgu1, d // 2, fc, "a", "dw_gate_up1", _ChipExchange([parts_d1]))
    (dh2, dproj, dyv, bcv, sg_c), recv_gu1 = _bwd_conv(dh3, h2, y, proj, small, win_f, wout_f, _ChipExchange([parts_gu1]))
    parts_in, _ = _weight_grad(nc, dproj, d // 2, 3 * d // N_CHIPS, "a", "dw_conv_in")
    parts_out, _ = _weight_grad(bcv, dyv, dq // 2, d, "a", "dw_conv_out")
    (dh1, dgu0, dff0, act0, sg_f0), recv_c = _bwd_ffn(dh2, h1, ff0, gu0, small, wgu0, wd0, 0,
                                                       _ChipExchange([parts_in, parts_out]))
    parts_d0, _ = _weight_grad(act0, dff0, fc, d // 2, "b", "dw_down0")
    parts_d0 = parts_d0.reshape(N_CHIPS, fq, d // 2)
    parts_gu0, recv_d0 = _weight_grad(n0, dgu0, d // 2, fc, "a", "dw_gate_up0", _ChipExchange([parts_d0]))
    (grad_x, dpool, sg_p), recv_gu0 = _bwd_pool(dh1, x2, small, pool_scale, poolw, _ChipExchange([parts_gu0]))
    g_pool = dpool.astype(BF16).reshape(2, ng // 2, N_CHIPS, gw // N_CHIPS, gw).transpose(2, 0, 1, 3, 4).reshape(
        N_CHIPS, 2, ng // 2 * (gw // N_CHIPS), gw)
    small_g = jnp.concatenate(
        [sg_p[0:2], sg_f0[0:2], sg_c[0:2], sg_f1[0:2], sg_c[2:5], sg_p[2:3],
         jnp.broadcast_to(loss_blk[0:1, 0:1], (1, d)), jnp.zeros((3, d), F32)], axis=0)
    land_p, small_all = _sibling_exchange([g_pool], small_g, "rs_sibling_pool")
    parts_p = _add_sibling(g_pool, land_p, core_arr)
    recv_p = _alone(_ChipExchange([parts_p]), "rs_chips_pool")

    gs_gu = _add_chips(parts_gu1, recv_gu1[0], where_arr, 1, 2)
    gs_gu = _add_chips(parts_gu0, recv_gu0[0], where_arr, 0, 2, gs_gu)
    gs_d = _add_chips(parts_d1, recv_d1[0], where_arr, 1, 2, col_half=True)
    gs_d = _add_chips(parts_d0, recv_d0[0], where_arr, 0, 2, gs_d, col_half=True)
    gs_in = _add_chips(parts_in, recv_c[0], where_arr)
    gs_out = _add_chips(parts_out, recv_c[1], where_arr)
    gs_pool = _add_chips(parts_p, recv_p[0], where_arr)
    full = _sibling_share([gs_gu, gs_d, gs_in, gs_out, gs_pool], [False, True, False, False, False])
    small_sum = _sum_small(small_all)
    loss = small_sum[12, 0]

    gg_gu = full[0].reshape(2, d, fc)
    gg_d = full[1].reshape(2, fq, d)
    gg_in = full[2].reshape(1, d, -1)
    gg_out = full[3].reshape(1, dq, d)
    gg_pool = full[4].reshape(1, ng, gw // N_CHIPS, gw)
    mine = lax.dynamic_slice_in_dim(small_sum, chip * dq, dq, axis=1)
    gg_gains = mine[0:8].reshape(2, 4, dq)
    gg_taps = mine[8:11].reshape(1, 3, dq)
    gg_scale = small_sum[11:12]

    grads = [gg_gains, gg_pool, gg_scale, gg_in, gg_taps, gg_out, gg_gu, gg_d]
    weights = [norm_gains, pool_w, pool_scale, conv_in_w, conv_w, conv_out_w, ffn_gate_up_w, ffn_down_w]
    ms = [m_norm_gains, m_pool_w, m_pool_scale, m_conv_in_w, m_conv_w, m_conv_out_w, m_ffn_gate_up_w, m_ffn_down_w]
    vs = [v_norm_gains, v_pool_w, v_pool_scale, v_conv_in_w, v_conv_w, v_conv_out_w, v_ffn_gate_up_w, v_ffn_down_w]
    names = ["gains", "pool_w", "pool_scale", "conv_in", "taps", "conv_out", "gate_up", "down"]
    deltas, new_ms, new_vs = [], [], []
    for w, g, m, v, nm in zip(weights, grads, ms, vs, names):
        flat = (-1, w.shape[-1])
        dl, m2, v2 = _adamw(w.reshape(flat), g.reshape(flat), m.reshape(flat), v.reshape(flat), "adamw_" + nm)
        deltas.append(dl.reshape(w.shape))
        new_ms.append(m2.reshape(w.shape))
        new_vs.append(v2.reshape(w.shape))
    return (loss, grad_x[None], *grads, *deltas, *new_ms, *new_vs)
```

```python
import jax
import jax.numpy as jnp
from jax import lax
from jax.experimental import pallas as pl
from jax.experimental.pallas import tpu as pltpu

RMS_EPS = 1e-6
POOL_WINDOWS = (2, 4, 8, 16)
POOL_HALO = 16
CONV_HALO = 8
N_CHIPS = 4
N_DEV = 8
ADAM_LR = 0.001
ADAM_B1 = 0.9
ADAM_B2 = 0.999
ADAM_EPS = 1e-08
ADAM_WD = 0.01
ADAM_STEP = 10
VMEM_LIMIT = 56 * 2**20
STREAM_BUDGET = 24 * 2**20
MESH = pl.DeviceIdType.MESH
ANY = pl.BlockSpec(memory_space=pl.ANY)
DMA = pltpu.SemaphoreType.DMA
BF16 = jnp.bfloat16
F32 = jnp.float32


def _token_tile(t, rows=512):
    return min(rows, t)


def _rms(x):
    r = lax.rsqrt(jnp.mean(x * x, axis=-1, keepdims=True) + RMS_EPS)
    return x * r, r


def _rms_bwd(dy, xh, r, g):
    a = dy * g
    return r * (a - xh * jnp.mean(a * xh, axis=-1, keepdims=True))


def _dot(a, b):
    return jnp.dot(a, b, preferred_element_type=F32)


def _dot_nt(a, b):
    return lax.dot_general(a, b, (((1,), (1,)), ((), ())), preferred_element_type=F32)


def _dot_tn(a, b):
    return lax.dot_general(a, b, (((0,), (0,)), ((), ())), preferred_element_type=F32)


def _colsum(a):
    return jnp.sum(a, axis=0, keepdims=True)


def _resident(block, index_map):
    return pl.BlockSpec(block, index_map, pipeline_mode=pl.Buffered(1))


def _row_block(r, row_bytes):
    best = None
    for rb in range(16, r + 1, 16):
        if r % rb == 0 and rb * row_bytes <= STREAM_BUDGET:
            best = rb
    return best if best is not None else r


def _place():
    x, y, c = lax.axis_index("x"), lax.axis_index("y"), lax.axis_index("c")
    return x, y, c, 2 * x + y


def _dev(chip, core):
    return (chip // 2, chip % 2, core)


def _remote(src, dst, send_sem, recv_sem, device):
    return pltpu.make_async_remote_copy(src_ref=src, dst_ref=dst, send_sem=send_sem, recv_sem=recv_sem,
                                        device_id=device, device_id_type=MESH)


class _Gather:
    def __init__(self, shards):
        n = len(shards)
        self.args = [s for s, _ in shards]
        self.layers = [l for _, l in shards]
        self.out_shape = [jax.ShapeDtypeStruct((N_CHIPS,) + s.shape[1:], s.dtype) for s in self.args]
        self.sems = [DMA((n,)), DMA((n,)), DMA((n, 3)), DMA((n, 3)), DMA((n, 3)), DMA((n, 3))]

    def _own(self, loc, out, sems, a):
        x, y, c, k = _place()
        return _remote(loc[a].at[self.layers[a]], out[a].at[k], sems[0].at[a], sems[1].at[a], (x, y, 1 - c))

    def _ici(self, loc, out, sems, a, m, arrival):
        x, y, c, k = _place()
        dst = out[a].at[k ^ m, c] if arrival else out[a].at[k, c]
        return _remote(loc[a].at[self.layers[a], c], dst, sems[2].at[a, m - 1], sems[3].at[a, m - 1], _dev(k ^ m, c))

    def _forward(self, out, sems, a, m, arrival):
        x, y, c, k = _place()
        got = out[a].at[k ^ m, 1 - c] if arrival else out[a].at[k ^ m, c]
        return _remote(got, got, sems[4].at[a, m - 1], sems[5].at[a, m - 1], (x, y, 1 - c))

    def start(self, loc, out, sems):
        for a in range(len(self.args)):
            for m in range(1, N_CHIPS):
                self._ici(loc, out, sems, a, m, False).start()
            self._own(loc, out, sems, a).start()

    def finish(self, loc, out, sems):
        n = len(self.args)
        for a in range(n):
            for m in range(1, N_CHIPS):
                self._ici(loc, out, sems, a, m, True).wait_recv()
                self._forward(out, sems, a, m, False).start()
        for a in range(n):
            for m in range(1, N_CHIPS):
                self._forward(out, sems, a, m, True).wait_recv()
            self._own(loc, out, sems, a).wait_recv()
        for a in range(n):
            for m in range(1, N_CHIPS):
                self._ici(loc, out, sems, a, m, False).wait_send()
                self._forward(out, sems, a, m, False).wait_send()
            self._own(loc, out, sems, a).wait_send()


class _ChipExchange:
    def __init__(self, parts):
        n = len(parts)
        self.args = list(parts)
        self.out_shape = [jax.ShapeDtypeStruct((N_CHIPS - 1,) + p.shape[1:], p.dtype) for p in parts]
        self.sems = [DMA((n, 3)), DMA((n, 3))]

    def _copy(self, p, land, sems, a, m):
        x, y, c, k = _place()
        return _remote(p[a].at[k ^ m], land[a].at[m - 1], sems[0].at[a, m - 1], sems[1].at[a, m - 1], _dev(k ^ m, c))

    def start(self, p, land, sems):
        for a in range(len(self.args)):
            for m in range(1, N_CHIPS):
                self._copy(p, land, sems, a, m).start()

    def finish(self, p, land, sems):
        for a in range(len(self.args)):
            for m in range(1, N_CHIPS):
                self._copy(p, land, sems, a, m).wait_recv()
        for a in range(len(self.args)):
            for m in range(1, N_CHIPS):
                self._copy(p, land, sems, a, m).wait_send()


def _hosted(body, comm, *, name, grid, in_specs, out_specs, out_shape, args, scratch_shapes=()):
    ni, no, ns = len(in_specs), len(out_shape), len(scratch_shapes)
    if comm is None:
        res = pl.pallas_call(
            body, name=name, grid=grid, in_specs=list(in_specs), out_specs=list(out_specs), out_shape=list(out_shape),
            scratch_shapes=list(scratch_shapes),
            compiler_params=pltpu.CompilerParams(dimension_semantics=("arbitrary",) * len(grid), vmem_limit_bytes=VMEM_LIMIT),
        )(*args)
        return list(res), []
    nc, nco = len(comm.args), len(comm.out_shape)

    def full(*refs):
        cin = refs[ni:ni + nc]
        outs = refs[ni + nc:ni + nc + no]
        cout = refs[ni + nc + no:ni + nc + no + nco]
        scratch = refs[ni + nc + no + nco:ni + nc + no + nco + ns]
        csems = refs[ni + nc + no + nco + ns:]
        first = _all_of([pl.program_id(ax) == 0 for ax in range(len(grid))])
        last = _all_of([pl.program_id(ax) == grid[ax] - 1 for ax in range(len(grid))])

        @pl.when(first)
        def _():
            comm.start(cin, cout, csems)

        body(*refs[:ni], *outs, *scratch)

        @pl.when(last)
        def _():
            comm.finish(cin, cout, csems)

    res = pl.pallas_call(
        full, name=name, grid=grid, in_specs=[*in_specs, *[ANY] * nc], out_specs=[*out_specs, *[ANY] * nco],
        out_shape=[*out_shape, *comm.out_shape], scratch_shapes=[*scratch_shapes, *comm.sems],
        compiler_params=pltpu.CompilerParams(dimension_semantics=("arbitrary",) * len(grid), vmem_limit_bytes=VMEM_LIMIT,
                                             has_side_effects=True),
    )(*args, *comm.args)
    return list(res[:no]), list(res[no:])


def _all_of(conds):
    out = conds[0]
    for c in conds[1:]:
        out = jnp.logical_and(out, c)
    return out


def _alone(comm, name):
    return _hosted(lambda: None, comm, name=name, grid=(1,), in_specs=[], out_specs=[], out_shape=[], args=[])[1]


def _sibling_exchange(grads, small, name):
    n = len(grads)
    ns = 0 if small is None else 1

    def body(*refs):
        g = refs[:n]
        land = refs[n + ns:2 * n + ns]
        send_sems, recv_sems, own_sem, ssend_sems, srecv_sems = refs[2 * n + 2 * ns:]
        x, y, c, k = _place()
        me = 2 * k + c
        cps = []
        for a in range(n):
            cp = _remote(g[a].at[:, pl.ds(1 - c, 1)], land[a], send_sems.at[a], recv_sems.at[a], (x, y, 1 - c))
            cp.start()
            cps.append(cp)
        if small is not None:
            sm, smg = refs[n], refs[2 * n + 1]
            peers = [me ^ m for m in range(1, N_DEV)]
            ids = [(p // 4, (p // 2) % 2, p % 2) for p in peers]
            own = pltpu.make_async_copy(sm, smg.at[me], own_sem)
            own.start()
            for m in range(1, N_DEV):
                cp = _remote(sm, smg.at[me], ssend_sems.at[m - 1], srecv_sems.at[m - 1], ids[m - 1])
                cp.start()
                cps.append(cp)
            for m in range(1, N_DEV):
                _remote(sm, smg.at[peers[m - 1]], ssend_sems.at[m - 1], srecv_sems.at[m - 1], ids[m - 1]).wait_recv()
            own.wait()
        for cp in cps[:n]:
            cp.wait_recv()
        for cp in cps:
            cp.wait_send()

    out_shape = [jax.ShapeDtypeStruct((N_CHIPS, 1) + a.shape[2:], a.dtype) for a in grads]
    ins = list(grads)
    if small is not None:
        out_shape.append(jax.ShapeDtypeStruct((N_DEV,) + small.shape, small.dtype))
        ins.append(small)
    return pl.pallas_call(
        body, name=name, out_shape=out_shape, in_specs=[ANY] * (n + ns), out_specs=[ANY] * (n + ns),
        scratch_shapes=[DMA((n,)), DMA((n,)), DMA, DMA((N_DEV - 1,)), DMA((N_DEV - 1,))],
        compiler_params=pltpu.CompilerParams(has_side_effects=True),
    )(*ins)


def _sibling_share(halves, col_half):
    n = len(halves)

    def body(*refs):
        out = refs[n:2 * n]
        send_sems, recv_sems = refs[2 * n:]
        x, y, c, k = _place()

        def half(a, core):
            if not col_half[a]:
                return out[a].at[:, pl.ds(core, 1)]
            cols = out[a].shape[-1] // 2
            return out[a].at[:, :, pl.ds(pl.multiple_of(core * cols, cols), cols)]

        cps = []
        for a in range(n):
            cp = _remote(half(a, c), half(a, c), send_sems.at[a], recv_sems.at[a], (x, y, 1 - c))
            cp.start()
            cps.append(cp)
        for a in range(n):
            _remote(half(a, 1 - c), half(a, 1 - c), send_sems.at[a], recv_sems.at[a], (x, y, 1 - c)).wait_recv()
        for cp in cps:
            cp.wait_send()

    out_shape = [jax.ShapeDtypeStruct(a.shape, a.dtype) for a in halves]
    return pl.pallas_call(
        body, name="rs_sibling_share", out_shape=out_shape, in_specs=[ANY] * n, out_specs=[ANY] * n,
        input_output_aliases={a: a for a in range(n)}, scratch_shapes=[DMA((n,)), DMA((n,))],
        compiler_params=pltpu.CompilerParams(has_side_effects=True),
    )(*halves)


def _add_sibling(g, land, core):
    _, _, r, c = g.shape
    rb = _row_block(r, c * (3 * 2 * 2 + 2 * 4))

    def body(core_ref, g_ref, l_ref, o_ref):
        o_ref[...] = (g_ref[...].astype(F32) + l_ref[...].astype(F32)).astype(o_ref.dtype)

    return pl.pallas_call(
        body, name="rs_add_sibling", out_shape=jax.ShapeDtypeStruct((N_CHIPS, r, c), g.dtype),
        grid_spec=pltpu.PrefetchScalarGridSpec(
            num_scalar_prefetch=1, grid=(N_CHIPS, r // rb),
            in_specs=[pl.BlockSpec((None, None, rb, c), lambda j, i, core_ref: (j, core_ref[0], i, 0)),
                      pl.BlockSpec((None, None, rb, c), lambda j, i, core_ref: (j, 0, i, 0))],
            out_specs=pl.BlockSpec((None, rb, c), lambda j, i, core_ref: (j, i, 0))),
        compiler_params=pltpu.CompilerParams(dimension_semantics=("parallel", "parallel"), vmem_limit_bytes=VMEM_LIMIT),
    )(core, g, land)


def _add_chips(part, land, where, layer=0, n_layers=1, into=None, col_half=False, rows=(0, 1)):
    _, r, c = part.shape
    sub, n_sub = rows
    rb = _row_block(r, c * (4 * 2 * 2 + 4 * 2 + 2 * 4))

    def body(where_ref, p_ref, l_ref, *rest):
        acc = p_ref[...].astype(F32)
        for m in range(N_CHIPS - 1):
            acc = acc + l_ref[m].astype(F32)
        rest[-1][...] = acc

    in_specs = [pl.BlockSpec((None, rb, c), lambda i, where_ref: (where_ref[0], i, 0)),
                pl.BlockSpec((N_CHIPS - 1, rb, c), lambda i, where_ref: (0, i, 0))]
    args = [where, part, land]
    if into is not None:
        in_specs.append(ANY)
        args.append(into)
    if col_half:
        out_shape = jax.ShapeDtypeStruct((n_layers, r, 2 * c), F32)
        out_spec = pl.BlockSpec((None, rb, c), lambda i, where_ref: (layer, i, where_ref[1]))
    else:
        out_shape = jax.ShapeDtypeStruct((n_layers, 2, n_sub * r, c), F32)
        out_spec = pl.BlockSpec((None, None, rb, c), lambda i, where_ref: (layer, where_ref[1], sub * (r // rb) + i, 0))
    return pl.pallas_call(
        body, name="rs_add_chips", out_shape=out_shape,
        grid_spec=pltpu.PrefetchScalarGridSpec(
            num_scalar_prefetch=1, grid=(r // rb,), in_specs=in_specs, out_specs=out_spec),
        input_output_aliases={} if into is None else {3: 0},
        compiler_params=pltpu.CompilerParams(dimension_semantics=("parallel",), vmem_limit_bytes=VMEM_LIMIT),
    )(*args)


def _sum_small(smg):
    def body(s_ref, o_ref):
        acc = s_ref[0]
        for j in range(1, N_DEV):
            acc = acc + s_ref[j]
        o_ref[...] = acc

    return pl.pallas_call(body, name="rs_sum_small", out_shape=jax.ShapeDtypeStruct(smg.shape[1:], F32))(smg)


def _pool_windows(ext_ref, g, gw, tm, first_row):
    w = POOL_WINDOWS[g]
    slab = ext_ref[:, g * gw:(g + 1) * gw]
    p, k = slab, 1
    while k < w:
        p = p + pltpu.roll(p, k, 0)
        k *= 2
    t = first_row + lax.broadcasted_iota(jnp.int32, (tm, 1), 0)
    cnt = jnp.minimum(t + 1, w).astype(F32)
    return p[POOL_HALO:] / cnt - slab[POOL_HALO:]


def _fwd_pool(x, small, scale, poolw, comm=None):
    t, d = x.shape
    tm = _token_tile(t)
    gw = d // len(POOL_WINDOWS)

    def body(x_ref, sm_ref, sc_ref, w_ref, h_ref, ext_ref, mix_ref):
        i = pl.program_id(0)

        @pl.when(i == 0)
        def _():
            ext_ref[0:POOL_HALO, :] = jnp.zeros((POOL_HALO, d), F32)

        @pl.when(i > 0)
        def _():
            ext_ref[0:POOL_HALO, :] = ext_ref[tm:tm + POOL_HALO, :]

        xv = x_ref[...]
        xh, _ = _rms(xv)
        ext_ref[POOL_HALO:, :] = xh * sm_ref[0:1, :]
        for g in range(len(POOL_WINDOWS)):
            pooled = _pool_windows(ext_ref, g, gw, tm, i * tm)
            cols = slice(g * gw, (g + 1) * gw)
            mix_ref[:, cols] = _dot(pooled.astype(BF16), w_ref[g]) * sc_ref[:, cols]
        mh, _ = _rms(mix_ref[...])
        h_ref[...] = xv + mh * sm_ref[1:2, :]

    (h,), got = _hosted(
        body, comm, name="fwd_pool", grid=(t // tm,), out_shape=[jax.ShapeDtypeStruct((t, d), F32)],
        in_specs=[pl.BlockSpec((tm, d), lambda i: (i, 0)), _resident(small.shape, lambda i: (0, 0)),
                  _resident(scale.shape, lambda i: (0, 0)), _resident(poolw.shape, lambda i: (0, 0, 0))],
        out_specs=[pl.BlockSpec((tm, d), lambda i: (i, 0))],
        scratch_shapes=[pltpu.VMEM((POOL_HALO + tm, d), F32), pltpu.VMEM((tm, d), F32)],
        args=[x, small, scale, poolw])
    return h, got


def _fwd_ffn(h, small, wgu, wd, layer, comm=None, target=None):
    t, d = h.shape
    tm = _token_tile(t)
    steps = t // tm
    fc = wgu.shape[-1]
    f = 2 * fc
    g_in, g_out = 4 * layer + 2, 4 * layer + 3
    with_loss = target is not None

    def body(h_ref, *refs):
        if with_loss:
            t_ref, sm_ref, wgu_ref, wd_ref, o_ref, gu_ref, ff_ref, n_ref, l_ref, acc_ref = refs
        else:
            sm_ref, wgu_ref, wd_ref, o_ref, gu_ref, ff_ref, n_ref = refs
        hv = h_ref[...]
        hh, _ = _rms(hv)
        n = (hh * sm_ref[g_in:g_in + 1, :]).astype(BF16)
        n_ref[...] = n
        ff = None
        for j in range(2):
            gate = _dot(n, wgu_ref[j])
            up = _dot(n, wgu_ref[2 + j])
            gu_ref[:, j * fc:(j + 1) * fc] = gate.astype(BF16)
            gu_ref[:, f + j * fc:f + (j + 1) * fc] = up.astype(BF16)
            act = (gate * jax.nn.sigmoid(gate) * up).astype(BF16)
            part = _dot(act, wd_ref[j * fc:(j + 1) * fc, :])
            ff = part if ff is None else ff + part
        ff_ref[...] = ff
        fh, _ = _rms(ff)
        out = hv + fh * sm_ref[g_out:g_out + 1, :]
        if not with_loss:
            o_ref[...] = out
            return
        i = pl.program_id(0)
        e = out - t_ref[...]
        o_ref[...] = e * (1.0 / d)

        @pl.when(i == 0)
        def _():
            acc_ref[...] = jnp.zeros_like(acc_ref)

        acc_ref[...] += _colsum(e * e)

        @pl.when(i == steps - 1)
        def _():
            l_ref[...] = jnp.full(l_ref.shape, 0.5 / d, F32) * jnp.sum(acc_ref[...])

    row = lambda i: (i, 0)
    out_shape = [jax.ShapeDtypeStruct((t, d), F32), jax.ShapeDtypeStruct((t, 2 * f), BF16),
                 jax.ShapeDtypeStruct((t, d), F32), jax.ShapeDtypeStruct((t, d), BF16)]
    out_specs = [pl.BlockSpec((tm, d), row), pl.BlockSpec((tm, 2 * f), row), pl.BlockSpec((tm, d), row),
                 pl.BlockSpec((tm, d), row)]
    weight_specs = [_resident(small.shape, lambda i: (0, 0)), _resident(wgu.shape, lambda i: (0, 0, 0)),
                    _resident(wd.shape, lambda i: (0, 0))]
    if with_loss:
        return _hosted(
            body, comm, name=f"fwd_ffn{layer}_loss", grid=(steps,),
            out_shape=out_shape + [jax.ShapeDtypeStruct((8, 128), F32)],
            in_specs=[pl.BlockSpec((tm, d), row), pl.BlockSpec((tm, d), row)] + weight_specs,
            out_specs=out_specs + [pl.BlockSpec((8, 128), lambda i: (0, 0))],
            scratch_shapes=[pltpu.VMEM((1, d), F32)], args=[h, target, small, wgu, wd])
    return _hosted(
        body, comm, name=f"fwd_ffn{layer}", grid=(steps,), out_shape=out_shape,
        in_specs=[pl.BlockSpec((tm, d), row)] + weight_specs, out_specs=out_specs, args=[h, small, wgu, wd])


def _fwd_conv(h, small, win, wout, comm=None):
    t, d = h.shape
    tm = _token_tile(t)
    pc = win.shape[-1]

    def body(h_ref, sm_ref, win_ref, wout_ref, o_ref, proj_ref, y_ref, n_ref, pj_ref, uext_ref):
        i = pl.program_id(0)

        @pl.when(i == 0)
        def _():
            uext_ref[0:CONV_HALO, :] = jnp.zeros((CONV_HALO, d), F32)

        @pl.when(i > 0)
        def _():
            uext_ref[0:CONV_HALO, :] = uext_ref[tm:tm + CONV_HALO, :]

        hv = h_ref[...]
        hh, _ = _rms(hv)
        n = (hh * sm_ref[4:5, :]).astype(BF16)
        n_ref[...] = n
        for k in range(N_CHIPS):
            pj_ref[:, k * pc:(k + 1) * pc] = _dot(n, win_ref[k])
        proj_ref[...] = pj_ref[...].astype(BF16)
        uext_ref[CONV_HALO:, :] = pj_ref[:, d:2 * d] * pj_ref[:, 2 * d:]
        taps = [sm_ref[8 + j:9 + j, :] for j in range(3)]
        full = uext_ref[...]
        conv = (full[CONV_HALO:] * taps[2] + pltpu.roll(full, 1, 0)[CONV_HALO:] * taps[1]
                + pltpu.roll(full, 2, 0)[CONV_HALO:] * taps[0])
        y = _dot((pj_ref[:, 0:d] * conv).astype(BF16), wout_ref[...])
        y_ref[...] = y
        yh, _ = _rms(y)
        o_ref[...] = hv + yh * sm_ref[5:6, :]

    row = lambda i: (i, 0)
    return _hosted(
        body, comm, name="fwd_conv", grid=(t // tm,),
        out_shape=[jax.ShapeDtypeStruct((t, d), F32), jax.ShapeDtypeStruct((t, 3 * d), BF16),
                   jax.ShapeDtypeStruct((t, d), F32), jax.ShapeDtypeStruct((t, d), BF16)],
        in_specs=[pl.BlockSpec((tm, d), row), _resident(small.shape, lambda i: (0, 0)),
                  _resident(win.shape, lambda i: (0, 0, 0)), _resident(wout.shape, lambda i: (0, 0))],
        out_specs=[pl.BlockSpec((tm, d), row), pl.BlockSpec((tm, 3 * d), row), pl.BlockSpec((tm, d), row),
                   pl.BlockSpec((tm, d), row)],
        scratch_shapes=[pltpu.VMEM((tm, 3 * d), F32), pltpu.VMEM((CONV_HALO + tm, d), F32)],
        args=[h, small, win, wout])


def _bwd_ffn(dh, h, ff, gu, small, wgu, wd, layer, comm=None):
    t, d = h.shape
    tm = _token_tile(t, 256)
    fc = wgu.shape[-1]
    f = 2 * fc
    g_in, g_out = 4 * layer + 2, 4 * layer + 3

    def body(dh_ref, h_ref, ff_ref, gu_ref, sm_ref, wgu_ref, wd_ref, o_ref, dgu_ref, dff_ref, act_ref, sg_ref):
        i = pl.program_id(0)

        @pl.when(i == 0)
        def _():
            sg_ref[...] = jnp.zeros_like(sg_ref)

        dy = dh_ref[...]
        fh, r3 = _rms(ff_ref[...])
        sg_ref[1:2, :] += _colsum(dy * fh)
        dff = _rms_bwd(dy, fh, r3, sm_ref[g_out:g_out + 1, :]).astype(BF16)
        dff_ref[...] = dff
        for j in range(2):
            dact = _dot_nt(dff, wd_ref[j * fc:(j + 1) * fc, :])
            gate = gu_ref[:, j * fc:(j + 1) * fc].astype(F32)
            up = gu_ref[:, f + j * fc:f + (j + 1) * fc].astype(F32)
            sig = jax.nn.sigmoid(gate)
            silu = gate * sig
            act_ref[:, j * fc:(j + 1) * fc] = (silu * up).astype(BF16)
            dgu_ref[:, j * fc:(j + 1) * fc] = (dact * up * (sig * (1.0 + gate * (1.0 - sig)))).astype(BF16)
            dgu_ref[:, f + j * fc:f + (j + 1) * fc] = (dact * silu).astype(BF16)
        dn = None
        for k in range(N_CHIPS):
            part = _dot_nt(dgu_ref[:, k * fc:(k + 1) * fc], wgu_ref[k])
            dn = part if dn is None else dn + part
        hh, r2 = _rms(h_ref[...])
        sg_ref[0:1, :] += _colsum(dn * hh)
        o_ref[...] = dy + _rms_bwd(dn, hh, r2, sm_ref[g_in:g_in + 1, :])

    row = lambda i: (i, 0)
    return _hosted(
        body, comm, name=f"bwd_ffn{layer}", grid=(t // tm,),
        out_shape=[jax.ShapeDtypeStruct((t, d), F32), jax.ShapeDtypeStruct((t, 2 * f), BF16),
                   jax.ShapeDtypeStruct((t, d), BF16), jax.ShapeDtypeStruct((t, f), BF16),
                   jax.ShapeDtypeStruct((8, d), F32)],
        in_specs=[pl.BlockSpec((tm, d), row), pl.BlockSpec((tm, d), row), pl.BlockSpec((tm, d), row),
                  pl.BlockSpec((tm, 2 * f), row), _resident(small.shape, lambda i: (0, 0)),
                  _resident(wgu.shape, lambda i: (0, 0, 0)), _resident(wd.shape, lambda i: (0, 0))],
        out_specs=[pl.BlockSpec((tm, d), row), pl.BlockSpec((tm, 2 * f), row), pl.BlockSpec((tm, d), row),
                   pl.BlockSpec((tm, f), row), pl.BlockSpec((8, d), lambda i: (0, 0))],
        args=[dh, h, ff, gu, small, wgu, wd])


def _bwd_conv(dh, h, y, proj, small, win, wout, comm=None):
    t, d = h.shape
    tm = _token_tile(t)
    steps = t // tm
    pc = win.shape[-1]
    halo_blocks = tm // 16

    def body(dh_ref, h_ref, y_ref, proj_ref, halo_ref, sm_ref, win_ref, wout_ref,
             o_ref, dproj_ref, dy_ref, bc_ref, sg_ref, uext_ref, dcext_ref, carry_ref):
        i = pl.program_id(0)
        tile = steps - 1 - i

        @pl.when(i == 0)
        def _():
            sg_ref[...] = jnp.zeros_like(sg_ref)
            carry_ref[...] = jnp.zeros_like(carry_ref)

        dy = dh_ref[...]
        yh, r1 = _rms(y_ref[...])
        sg_ref[1:2, :] += _colsum(dy * yh)
        dyv = _rms_bwd(dy, yh, r1, sm_ref[5:6, :]).astype(BF16)
        dy_ref[...] = dyv
        dbc = _dot_nt(dyv, wout_ref[...])
        b = proj_ref[:, 0:d].astype(F32)
        cg = proj_ref[:, d:2 * d].astype(F32)
        v = proj_ref[:, 2 * d:].astype(F32)
        halo = halo_ref[...].astype(F32)[16 - CONV_HALO:]
        uh = halo[:, d:2 * d] * halo[:, 2 * d:]
        uext_ref[0:CONV_HALO, :] = jnp.where(tile > 0, uh, jnp.zeros_like(uh))
        uext_ref[CONV_HALO:, :] = cg * v
        taps = [sm_ref[8 + j:9 + j, :] for j in range(3)]
        full = uext_ref[...]
        u0 = full[CONV_HALO:]
        u1 = pltpu.roll(full, 1, 0)[CONV_HALO:]
        u2 = pltpu.roll(full, 2, 0)[CONV_HALO:]
        conv = u0 * taps[2] + u1 * taps[1] + u2 * taps[0]
        bc_ref[...] = (b * conv).astype(BF16)
        dconv = dbc * b
        sg_ref[4:5, :] += _colsum(dconv * u0)
        sg_ref[3:4, :] += _colsum(dconv * u1)
        sg_ref[2:3, :] += _colsum(dconv * u2)
        dcext_ref[0:tm, :] = dconv
        dcext_ref[tm:, :] = carry_ref[...]
        carry_ref[...] = dconv[0:CONV_HALO]
        dfull = dcext_ref[...]
        n8 = tm + CONV_HALO
        du = (dfull[0:tm] * taps[2] + pltpu.roll(dfull, n8 - 1, 0)[0:tm] * taps[1]
              + pltpu.roll(dfull, n8 - 2, 0)[0:tm] * taps[0])
        dproj_ref[:, 0:d] = (dbc * conv).astype(BF16)
        dproj_ref[:, d:2 * d] = (du * v).astype(BF16)
        dproj_ref[:, 2 * d:] = (du * cg).astype(BF16)
        dn = None
        for k in range(N_CHIPS):
            part = _dot_nt(dproj_ref[:, k * pc:(k + 1) * pc], win_ref[k])
            dn = part if dn is None else dn + part
        hh, r0 = _rms(h_ref[...])
        sg_ref[0:1, :] += _colsum(dn * hh)
        o_ref[...] = dy + _rms_bwd(dn, hh, r0, sm_ref[4:5, :])

    rev = lambda i: (steps - 1 - i, 0)
    before = lambda i: (jnp.maximum((steps - 1 - i) * halo_blocks - 1, 0), 0)
    return _hosted(
        body, comm, name="bwd_conv", grid=(steps,),
        out_shape=[jax.ShapeDtypeStruct((t, d), F32), jax.ShapeDtypeStruct((t, 3 * d), BF16),
                   jax.ShapeDtypeStruct((t, d), BF16), jax.ShapeDtypeStruct((t, d), BF16),
                   jax.ShapeDtypeStruct((8, d), F32)],
        in_specs=[pl.BlockSpec((tm, d), rev), pl.BlockSpec((tm, d), rev), pl.BlockSpec((tm, d), rev),
                  pl.BlockSpec((tm, 3 * d), rev), pl.BlockSpec((16, 3 * d), before),
                  _resident(small.shape, lambda i: (0, 0)), _resident(win.shape, lambda i: (0, 0, 0)),
                  _resident(wout.shape, lambda i: (0, 0))],
        out_specs=[pl.BlockSpec((tm, d), rev), pl.BlockSpec((tm, 3 * d), rev), pl.BlockSpec((tm, d), rev),
                   pl.BlockSpec((tm, d), rev), pl.BlockSpec((8, d), lambda i: (0, 0))],
        scratch_shapes=[pltpu.VMEM((CONV_HALO + tm, d), F32), pltpu.VMEM((tm + CONV_HALO, d), F32),
                        pltpu.VMEM((CONV_HALO, d), F32)],
        args=[dh, h, y, proj, proj, small, win, wout])


def _bwd_pool(dh, x, small, scale, poolw, comm=None):
    t, d = x.shape
    tm = _token_tile(t)
    steps = t // tm
    ng = len(POOL_WINDOWS)
    gw = d // ng
    halo_blocks = tm // POOL_HALO

    def body(dh_ref, x_ref, halo_ref, sm_ref, sc_ref, w_ref, o_ref, dw_ref, sg_ref,
             ext_ref, mix_ref, mm_ref, pb_ref, qext_ref, dhn_ref, carry_ref):
        i = pl.program_id(0)
        tile = steps - 1 - i

        @pl.when(i == 0)
        def _():
            sg_ref[...] = jnp.zeros_like(sg_ref)
            dw_ref[...] = jnp.zeros_like(dw_ref)
            carry_ref[...] = jnp.zeros_like(carry_ref)

        g0 = sm_ref[0:1, :]
        xv = x_ref[...]
        xh, r0 = _rms(xv)
        hx, _ = _rms(halo_ref[...])
        ext_ref[0:POOL_HALO, :] = jnp.where(tile > 0, hx * g0, jnp.zeros_like(hx))
        ext_ref[POOL_HALO:, :] = xh * g0
        for g in range(ng):
            pooled = _pool_windows(ext_ref, g, gw, tm, tile * tm)
            cols = slice(g * gw, (g + 1) * gw)
            pb = pooled.astype(BF16)
            pb_ref[:, cols] = pb
            mm = _dot(pb, w_ref[g])
            mm_ref[:, cols] = mm
            mix_ref[:, cols] = mm * sc_ref[:, cols]
        dy = dh_ref[...]
        mh, r1 = _rms(mix_ref[...])
        sg_ref[1:2, :] += _colsum(dy * mh)
        dmix = _rms_bwd(dy, mh, r1, sm_ref[1:2, :])
        sg_ref[2:3, :] += _colsum(dmix * mm_ref[...])
        mix_ref[...] = dmix * sc_ref[...]
        n16 = tm + POOL_HALO
        for g in range(ng):
            w = POOL_WINDOWS[g]
            cols = slice(g * gw, (g + 1) * gw)
            dmm = mix_ref[:, cols].astype(BF16)
            dpooled = _dot_nt(dmm, w_ref[g])
            dw_ref[g] += _dot_tn(pb_ref[:, cols], dmm)
            trow = tile * tm + lax.broadcasted_iota(jnp.int32, (tm, 1), 0)
            q = dpooled / jnp.minimum(trow + 1, w).astype(F32)
            qext_ref[0:tm, cols] = q
            qext_ref[tm:, cols] = carry_ref[:, cols]
            carry_ref[:, cols] = q[0:POOL_HALO]
            p, k = qext_ref[:, cols], 1
            while k < w:
                p = p + pltpu.roll(p, n16 - k, 0)
                k *= 2
            dhn_ref[:, cols] = p[0:tm] - dpooled
        dhn = dhn_ref[...]
        sg_ref[0:1, :] += _colsum(dhn * xh)
        o_ref[...] = dy + _rms_bwd(dhn, xh, r0, g0)

    rev = lambda i: (steps - 1 - i, 0)
    before = lambda i: (jnp.maximum((steps - 1 - i) * halo_blocks - 1, 0), 0)
    return _hosted(
        body, comm, name="bwd_pool", grid=(steps,),
        out_shape=[jax.ShapeDtypeStruct((t, d), F32), jax.ShapeDtypeStruct((ng, gw, gw), F32),
                   jax.ShapeDtypeStruct((8, d), F32)],
        in_specs=[pl.BlockSpec((tm, d), rev), pl.BlockSpec((tm, d), rev), pl.BlockSpec((POOL_HALO, d), before),
                  _resident(small.shape, lambda i: (0, 0)), _resident(scale.shape, lambda i: (0, 0)),
                  _resident(poolw.shape, lambda i: (0, 0, 0))],
        out_specs=[pl.BlockSpec((tm, d), rev), pl.BlockSpec((ng, gw, gw), lambda i: (0, 0, 0)),
                   pl.BlockSpec((8, d), lambda i: (0, 0))],
        scratch_shapes=[pltpu.VMEM((POOL_HALO + tm, d), F32), pltpu.VMEM((tm, d), F32), pltpu.VMEM((tm, d), F32),
                        pltpu.VMEM((tm, d), BF16), pltpu.VMEM((tm + POOL_HALO, d), F32), pltpu.VMEM((tm, d), F32),
                        pltpu.VMEM((POOL_HALO, d), F32)],
        args=[dh, x, x, small, scale, poolw])


def _weight_grad(a, b, bm, bn, half_on, name, comm=None, rows=(0, 1)):
    t, m = a.shape
    _, n = b.shape
    if half_on == "a":
        a_cols, b_cols = 2 * bm, bn
    else:
        a_cols, b_cols = bm, 2 * bn
    steps = max(m // a_cols, n // b_cols)
    sub, n_sub = rows
    tr = bm // n_sub

    def spec(cols, total):
        if cols == total:
            return _resident((t, cols), lambda p, j: (0, 0))
        return pl.BlockSpec((t, cols), lambda p, j: (0, j))

    def tile(a_ref, b_ref, half):
        if half_on == "a":
            first = half * bm + sub * tr
            return _dot_tn(a_ref[:, first:first + tr], b_ref[...])
        return _dot_tn(a_ref[...], b_ref[:, half * bn:(half + 1) * bn])

    def body(a_ref, b_ref, parts_ref, land_ref, acc_ref, stage_ref, got_ref, send_sems, recv_sems, got_sem):
        p, j = pl.program_id(0), pl.program_id(1)
        x, y, c, _ = _place()
        half = jnp.where(p == 0, 1 - c, c)

        def send(jj):
            return _remote(stage_ref.at[jj % 2], land_ref.at[jj], send_sems.at[jj], recv_sems.at[jj], (x, y, 1 - c))

        def fetch():
            return pltpu.make_async_copy(land_ref.at[j], got_ref, got_sem)

        @pl.when(p == 1)
        def _():
            @pl.when(j == 0)
            def _():
                for jj in range(max(steps - 2, 0), steps):
                    send(jj).wait_send()

            send(j).wait_recv()
            fetch().start()

        for hv in range(2):
            @pl.when(half == hv)
            def _():
                acc_ref[...] = tile(a_ref, b_ref, hv)

        @pl.when(p == 0)
        def _():
            @pl.when(j >= 2)
            def _():
                send(j - 2).wait_send()

            stage_ref[j % 2] = acc_ref[...].astype(BF16)
            send(j).start()

        @pl.when(p == 1)
        def _():
            fetch().wait()
            parts_ref[...] = (acc_ref[...] + got_ref[...].astype(F32)).astype(BF16)

    (parts, _), got = _hosted(
        body, comm, name=name, grid=(2, steps),
        out_shape=[jax.ShapeDtypeStruct((steps, tr, bn), BF16), jax.ShapeDtypeStruct((steps, tr, bn), BF16)],
        in_specs=[spec(a_cols, m), spec(b_cols, n)],
        out_specs=[pl.BlockSpec((None, tr, bn), lambda p, j: (p * j, 0, 0)), ANY],
        scratch_shapes=[pltpu.VMEM((tr, bn), F32), pltpu.VMEM((2, tr, bn), BF16), pltpu.VMEM((tr, bn), BF16),
                        DMA((steps,)), DMA((steps,)), DMA],
        args=[a, b])
    return parts, got


def _adamw(w, g, m, v, name):
    r, c = w.shape
    rb = _row_block(r, c * (8 * 4 * 2 + 4 * 4))
    bc1 = 1.0 - ADAM_B1 ** ADAM_STEP
    bc2 = 1.0 - ADAM_B2 ** ADAM_STEP

    def body(w_ref, g_ref, m_ref, v_ref, d_ref, nm_ref, nv_ref, go_ref):
        gv = g_ref[...]
        go_ref[...] = gv
        nm = ADAM_B1 * m_ref[...] + (1.0 - ADAM_B1) * gv
        nv = ADAM_B2 * v_ref[...] + (1.0 - ADAM_B2) * (gv * gv)
        nm_ref[...] = nm
        nv_ref[...] = nv
        d_ref[...] = -ADAM_LR * ((nm / bc1) / (jnp.sqrt(nv / bc2) + ADAM_EPS) + ADAM_WD * w_ref[...])

    spec = pl.BlockSpec((rb, c), lambda i: (i, 0))
    return pl.pallas_call(
        body, name=name, grid=(r // rb,), out_shape=[jax.ShapeDtypeStruct((r, c), F32)] * 4,
        in_specs=[spec] * 4, out_specs=[spec] * 4,
        compiler_params=pltpu.CompilerParams(dimension_semantics=("parallel",), vmem_limit_bytes=VMEM_LIMIT),
    )(w, g, m, v)


def kernel(x, norm_gains, pool_w, pool_scale, conv_in_w, conv_w, conv_out_w, ffn_gate_up_w, ffn_down_w, loss_target, m_norm_gains, m_pool_w, m_pool_scale, m_conv_in_w, m_conv_w, m_conv_out_w, m_ffn_gate_up_w, m_ffn_down_w, v_norm_gains, v_pool_w, v_pool_scale, v_conv_in_w, v_conv_w, v_conv_out_w, v_ffn_gate_up_w, v_ffn_down_w):
    _, t, d = x.shape
    dq = d // N_CHIPS
    ng = len(POOL_WINDOWS)
    gw = d // ng
    fq = ffn_down_w.shape[1]
    f = N_CHIPS * fq
    fc = f // 2
    core = lax.axis_index("c")
    chip = 2 * lax.axis_index("x") + lax.axis_index("y")
    core_arr = jnp.reshape(core, (1,)).astype(jnp.int32)
    where_arr = jnp.stack([chip, core]).astype(jnp.int32)
    x2, target = x[0], loss_target[0]

    small_loc = jnp.concatenate(
        [norm_gains.reshape(8, dq), conv_w[0], jnp.zeros((5, dq), F32)], axis=0).reshape(1, 2, 8, dq)
    pool_loc = pool_w.astype(BF16).reshape(1, 2, ng // 2 * (gw // N_CHIPS), gw)
    wgu_loc = ffn_gate_up_w.astype(BF16).reshape(2, 2, d // 2, fc)
    wd_loc = ffn_down_w.astype(BF16).reshape(2, 2, fq // 2, d)
    win_loc = conv_in_w.astype(BF16).reshape(1, 2, d // 2, -1)
    wout_loc = conv_out_w.astype(BF16).reshape(1, 2, dq // 2, d)

    def ffn_weights(wgu_f, wd_f):
        return wgu_f.reshape(N_CHIPS, d, fc), wd_f.reshape(f, d)

    pool_f, small_f = _alone(_Gather([(pool_loc, 0), (small_loc, 0)]), "ag_first")
    poolw = pool_f.reshape(N_CHIPS, ng, gw // N_CHIPS, gw).transpose(1, 0, 2, 3).reshape(ng, gw, gw)
    small = small_f.transpose(1, 2, 0, 3).reshape(16, d)
    h1, got = _fwd_pool(x2, small, pool_scale, poolw, _Gather([(wgu_loc, 0), (wd_loc, 0)]))
    wgu0, wd0 = ffn_weights(*got)
    (h2, gu0, ff0, n0), got = _fwd_ffn(h1, small, wgu0, wd0, 0, _Gather([(win_loc, 0), (wout_loc, 0), (wgu_loc, 1)]))
    win_f, wout_f = got[0].reshape(N_CHIPS, d, -1), got[1].reshape(d, d)
    (h3, proj, y, nc), got_d = _fwd_conv(h2, small, win_f, wout_f, _Gather([(wd_loc, 1)]))
    wgu1, wd1 = ffn_weights(got[2], got_d[0])
    (dh4, gu1, ff1, n1, loss_blk), _ = _fwd_ffn(h3, small, wgu1, wd1, 1, target=target)

    (dh3, dgu1, dff1, act1, sg_f1), _ = _bwd_ffn(dh4, h3, ff1, gu1, small, wgu1, wd1, 1)
    parts_d1, _ = _weight_grad(act1, dff1, fc, d // 2, "b", "dw_down1")
    parts_d1 = parts_d1.reshape(N_CHIPS, fq, d // 2)
    parts_gu1, recv_d1 = _weight_grad(n1, dgu1, d // 2, fc, "a", "dw_gate_up1", _ChipExchange([parts_d1]))
    (dh2, dproj, dyv, bcv, sg_c), recv_gu1 = _bwd_conv(dh3, h2, y, proj, small, win_f, wout_f, _ChipExchange([parts_gu1]))
    parts_in, _ = _weight_grad(nc, dproj, d // 2, 3 * d // N_CHIPS, "a", "dw_conv_in")
    (dh1, dgu0, dff0, act0, sg_f0), recv_in = _bwd_ffn(dh2, h1, ff0, gu0, small, wgu0, wd0, 0, _ChipExchange([parts_in]))
    (grad_x, dpool, sg_p), _ = _bwd_pool(dh1, x2, small, pool_scale, poolw)
    parts_d0, _ = _weight_grad(act0, dff0, fc, d // 2, "b", "dw_down0")
    parts_d0 = parts_d0.reshape(N_CHIPS, fq, d // 2)
    parts_gu0a, recv_d0 = _weight_grad(n0, dgu0, d // 2, fc, "a", "dw_gate_up0a", _ChipExchange([parts_d0]), (0, 2))
    parts_gu0b, recv_gu0a = _weight_grad(n0, dgu0, d // 2, fc, "a", "dw_gate_up0b", _ChipExchange([parts_gu0a]), (1, 2))
    parts_out, recv_gu0b = _weight_grad(bcv, dyv, dq // 2, d, "a", "dw_conv_out", _ChipExchange([parts_gu0b]))
    g_pool = dpool.astype(BF16).reshape(2, ng // 2, N_CHIPS, gw // N_CHIPS, gw).transpose(2, 0, 1, 3, 4).reshape(
        N_CHIPS, 2, ng // 2 * (gw // N_CHIPS), gw)
    small_g = jnp.concatenate(
        [sg_p[0:2], sg_f0[0:2], sg_c[0:2], sg_f1[0:2], sg_c[2:5], sg_p[2:3],
         jnp.broadcast_to(loss_blk[0:1, 0:1], (1, d)), jnp.zeros((3, d), F32)], axis=0)
    land_p, small_all = _sibling_exchange([g_pool], small_g, "rs_sibling_pool")
    parts_p = _add_sibling(g_pool, land_p, core_arr)
    recv_out, recv_p = _alone(_ChipExchange([parts_out, parts_p]), "rs_chips_last")

    gs_gu = _add_chips(parts_gu1, recv_gu1[0], where_arr, 1, 2)
    gs_gu = _add_chips(parts_gu0a, recv_gu0a[0], where_arr, 0, 2, gs_gu, rows=(0, 2))
    gs_gu = _add_chips(parts_gu0b, recv_gu0b[0], where_arr, 0, 2, gs_gu, rows=(1, 2))
    gs_d = _add_chips(parts_d1, recv_d1[0], where_arr, 1, 2, col_half=True)
    gs_d = _add_chips(parts_d0, recv_d0[0], where_arr, 0, 2, gs_d, col_half=True)
    gs_in = _add_chips(parts_in, recv_in[0], where_arr)
    gs_out = _add_chips(parts_out, recv_out, where_arr)
    gs_pool = _add_chips(parts_p, recv_p, where_arr)
    full = _sibling_share([gs_gu, gs_d, gs_in, gs_out, gs_pool], [False, True, False, False, False])
    small_sum = _sum_small(small_all)
    loss = small_sum[12, 0]

    gg_gu = full[0].reshape(2, d, fc)
    gg_d = full[1].reshape(2, fq, d)
    gg_in = full[2].reshape(1, d, -1)
    gg_out = full[3].reshape(1, dq, d)
    gg_pool = full[4].reshape(1, ng, gw // N_CHIPS, gw)
    mine = lax.dynamic_slice_in_dim(small_sum, chip * dq, dq, axis=1)
    gg_gains = mine[0:8].reshape(2, 4, dq)
    gg_taps = mine[8:11].reshape(1, 3, dq)
    gg_scale = small_sum[11:12]

    grads = [gg_gains, gg_pool, gg_scale, gg_in, gg_taps, gg_out, gg_gu, gg_d]
    weights = [norm_gains, pool_w, pool_scale, conv_in_w, conv_w, conv_out_w, ffn_gate_up_w, ffn_down_w]
    ms = [m_norm_gains, m_pool_w, m_pool_scale, m_conv_in_w, m_conv_w, m_conv_out_w, m_ffn_gate_up_w, m_ffn_down_w]
    vs = [v_norm_gains, v_pool_w, v_pool_scale, v_conv_in_w, v_conv_w, v_conv_out_w, v_ffn_gate_up_w, v_ffn_down_w]
    names = ["gains", "pool_w", "pool_scale", "conv_in", "taps", "conv_out", "gate_up", "down"]
    deltas, new_ms, new_vs, grads_out = [], [], [], []
    for w, g, m, v, nm in zip(weights, grads, ms, vs, names):
        flat = (-1, w.shape[-1])
        dl, m2, v2, g2 = _adamw(w.reshape(flat), g.reshape(flat), m.reshape(flat), v.reshape(flat), "adamw_" + nm)
        deltas.append(dl.reshape(w.shape))
        new_ms.append(m2.reshape(w.shape))
        new_vs.append(v2.reshape(w.shape))
        grads_out.append(g2.reshape(w.shape))
    return (loss, grad_x[None], *grads_out, *deltas, *new_ms, *new_vs)
```

```python
import jax
import jax.numpy as jnp
from jax import lax
from jax.experimental import pallas as pl
from jax.experimental.pallas import tpu as pltpu

RMS_EPS = 1e-6
POOL_WINDOWS = (2, 4, 8, 16)
POOL_HALO = 16
CONV_HALO = 8
N_CHIPS = 4
N_DEV = 8
ADAM_LR = 0.001
ADAM_B1 = 0.9
ADAM_B2 = 0.999
ADAM_EPS = 1e-08
ADAM_WD = 0.01
ADAM_STEP = 10
VMEM_LIMIT = 56 * 2**20
STREAM_BUDGET = 24 * 2**20
MESH = pl.DeviceIdType.MESH
ANY = pl.BlockSpec(memory_space=pl.ANY)
DMA = pltpu.SemaphoreType.DMA
BF16 = jnp.bfloat16
F32 = jnp.float32


def _token_tile(t, rows=512):
    return min(rows, t)


def _rms(x):
    r = lax.rsqrt(jnp.mean(x * x, axis=-1, keepdims=True) + RMS_EPS)
    return x * r, r


def _rms_bwd(dy, xh, r, g):
    a = dy * g
    return r * (a - xh * jnp.mean(a * xh, axis=-1, keepdims=True))


def _dot(a, b):
    return jnp.dot(a, b, preferred_element_type=F32)


def _dot_nt(a, b):
    return lax.dot_general(a, b, (((1,), (1,)), ((), ())), preferred_element_type=F32)


def _dot_tn(a, b):
    return lax.dot_general(a, b, (((0,), (0,)), ((), ())), preferred_element_type=F32)


def _colsum(a):
    return jnp.sum(a, axis=0, keepdims=True)


def _resident(block, index_map):
    return pl.BlockSpec(block, index_map, pipeline_mode=pl.Buffered(1))


def _row_block(r, row_bytes):
    best = None
    for rb in range(16, r + 1, 16):
        if r % rb == 0 and rb * row_bytes <= STREAM_BUDGET:
            best = rb
    return best if best is not None else r


def _place():
    x, y, c = lax.axis_index("x"), lax.axis_index("y"), lax.axis_index("c")
    return x, y, c, 2 * x + y


def _dev(chip, core):
    return (chip // 2, chip % 2, core)


def _remote(src, dst, send_sem, recv_sem, device):
    return pltpu.make_async_remote_copy(src_ref=src, dst_ref=dst, send_sem=send_sem, recv_sem=recv_sem,
                                        device_id=device, device_id_type=MESH)


class _Gather:
    def __init__(self, shards):
        n = len(shards)
        self.args = [s for s, _ in shards]
        self.layers = [l for _, l in shards]
        self.out_shape = [jax.ShapeDtypeStruct((N_CHIPS,) + s.shape[1:], s.dtype) for s in self.args]
        self.sems = [DMA((n,)), DMA((n,)), DMA((n, 3)), DMA((n, 3)), DMA((n, 3)), DMA((n, 3))]

    def _own(self, loc, out, sems, a):
        x, y, c, k = _place()
        return _remote(loc[a].at[self.layers[a]], out[a].at[k], sems[0].at[a], sems[1].at[a], (x, y, 1 - c))

    def _ici(self, loc, out, sems, a, m, arrival):
        x, y, c, k = _place()
        dst = out[a].at[k ^ m, c] if arrival else out[a].at[k, c]
        return _remote(loc[a].at[self.layers[a], c], dst, sems[2].at[a, m - 1], sems[3].at[a, m - 1], _dev(k ^ m, c))

    def _forward(self, out, sems, a, m, arrival):
        x, y, c, k = _place()
        got = out[a].at[k ^ m, 1 - c] if arrival else out[a].at[k ^ m, c]
        return _remote(got, got, sems[4].at[a, m - 1], sems[5].at[a, m - 1], (x, y, 1 - c))

    def start(self, loc, out, sems):
        for a in range(len(self.args)):
            for m in range(1, N_CHIPS):
                self._ici(loc, out, sems, a, m, False).start()
            self._own(loc, out, sems, a).start()

    def finish(self, loc, out, sems):
        n = len(self.args)
        for a in range(n):
            for m in range(1, N_CHIPS):
                self._ici(loc, out, sems, a, m, True).wait_recv()
                self._forward(out, sems, a, m, False).start()
        for a in range(n):
            for m in range(1, N_CHIPS):
                self._forward(out, sems, a, m, True).wait_recv()
            self._own(loc, out, sems, a).wait_recv()
        for a in range(n):
            for m in range(1, N_CHIPS):
                self._ici(loc, out, sems, a, m, False).wait_send()
                self._forward(out, sems, a, m, False).wait_send()
            self._own(loc, out, sems, a).wait_send()


class _ChipExchange:
    def __init__(self, parts):
        n = len(parts)
        self.args = list(parts)
        self.out_shape = [jax.ShapeDtypeStruct((N_CHIPS - 1,) + p.shape[1:], p.dtype) for p in parts]
        self.sems = [DMA((n, 3)), DMA((n, 3))]

    def _copy(self, p, land, sems, a, m):
        x, y, c, k = _place()
        return _remote(p[a].at[k ^ m], land[a].at[m - 1], sems[0].at[a, m - 1], sems[1].at[a, m - 1], _dev(k ^ m, c))

    def start(self, p, land, sems):
        for a in range(len(self.args)):
            for m in range(1, N_CHIPS):
                self._copy(p, land, sems, a, m).start()

    def finish(self, p, land, sems):
        for a in range(len(self.args)):
            for m in range(1, N_CHIPS):
                self._copy(p, land, sems, a, m).wait_recv()
        for a in range(len(self.args)):
            for m in range(1, N_CHIPS):
                self._copy(p, land, sems, a, m).wait_send()


def _hosted(body, comm, *, name, grid, in_specs, out_specs, out_shape, args, scratch_shapes=()):
    ni, no, ns = len(in_specs), len(out_shape), len(scratch_shapes)
    if comm is None:
        res = pl.pallas_call(
            body, name=name, grid=grid, in_specs=list(in_specs), out_specs=list(out_specs), out_shape=list(out_shape),
            scratch_shapes=list(scratch_shapes),
            compiler_params=pltpu.CompilerParams(dimension_semantics=("arbitrary",) * len(grid), vmem_limit_bytes=VMEM_LIMIT),
        )(*args)
        return list(res), []
    nc, nco = len(comm.args), len(comm.out_shape)

    def full(*refs):
        cin = refs[ni:ni + nc]
        outs = refs[ni + nc:ni + nc + no]
        cout = refs[ni + nc + no:ni + nc + no + nco]
        scratch = refs[ni + nc + no + nco:ni + nc + no + nco + ns]
        csems = refs[ni + nc + no + nco + ns:]
        first = _all_of([pl.program_id(ax) == 0 for ax in range(len(grid))])
        last = _all_of([pl.program_id(ax) == grid[ax] - 1 for ax in range(len(grid))])

        @pl.when(first)
        def _():
            comm.start(cin, cout, csems)

        body(*refs[:ni], *outs, *scratch)

        @pl.when(last)
        def _():
            comm.finish(cin, cout, csems)

    res = pl.pallas_call(
        full, name=name, grid=grid, in_specs=[*in_specs, *[ANY] * nc], out_specs=[*out_specs, *[ANY] * nco],
        out_shape=[*out_shape, *comm.out_shape], scratch_shapes=[*scratch_shapes, *comm.sems],
        compiler_params=pltpu.CompilerParams(dimension_semantics=("arbitrary",) * len(grid), vmem_limit_bytes=VMEM_LIMIT,
                                             has_side_effects=True),
    )(*args, *comm.args)
    return list(res[:no]), list(res[no:])


def _all_of(conds):
    out = conds[0]
    for c in conds[1:]:
        out = jnp.logical_and(out, c)
    return out


def _alone(comm, name):
    return _hosted(lambda: None, comm, name=name, grid=(1,), in_specs=[], out_specs=[], out_shape=[], args=[])[1]


def _sibling_exchange(grads, small, name):
    n = len(grads)
    ns = 0 if small is None else 1

    def body(*refs):
        g = refs[:n]
        land = refs[n + ns:2 * n + ns]
        send_sems, recv_sems, own_sem, ssend_sems, srecv_sems = refs[2 * n + 2 * ns:]
        x, y, c, k = _place()
        me = 2 * k + c
        cps = []
        for a in range(n):
            cp = _remote(g[a].at[:, pl.ds(1 - c, 1)], land[a], send_sems.at[a], recv_sems.at[a], (x, y, 1 - c))
            cp.start()
            cps.append(cp)
        if small is not None:
            sm, smg = refs[n], refs[2 * n + 1]
            peers = [me ^ m for m in range(1, N_DEV)]
            ids = [(p // 4, (p // 2) % 2, p % 2) for p in peers]
            own = pltpu.make_async_copy(sm, smg.at[me], own_sem)
            own.start()
            for m in range(1, N_DEV):
                cp = _remote(sm, smg.at[me], ssend_sems.at[m - 1], srecv_sems.at[m - 1], ids[m - 1])
                cp.start()
                cps.append(cp)
            for m in range(1, N_DEV):
                _remote(sm, smg.at[peers[m - 1]], ssend_sems.at[m - 1], srecv_sems.at[m - 1], ids[m - 1]).wait_recv()
            own.wait()
        for cp in cps[:n]:
            cp.wait_recv()
        for cp in cps:
            cp.wait_send()

    out_shape = [jax.ShapeDtypeStruct((N_CHIPS, 1) + a.shape[2:], a.dtype) for a in grads]
    ins = list(grads)
    if small is not None:
        out_shape.append(jax.ShapeDtypeStruct((N_DEV,) + small.shape, small.dtype))
        ins.append(small)
    return pl.pallas_call(
        body, name=name, out_shape=out_shape, in_specs=[ANY] * (n + ns), out_specs=[ANY] * (n + ns),
        scratch_shapes=[DMA((n,)), DMA((n,)), DMA, DMA((N_DEV - 1,)), DMA((N_DEV - 1,))],
        compiler_params=pltpu.CompilerParams(has_side_effects=True),
    )(*ins)


def _sibling_share(halves, col_half):
    n = len(halves)

    def body(*refs):
        out = refs[n:2 * n]
        send_sems, recv_sems = refs[2 * n:]
        x, y, c, k = _place()

        def half(a, core):
            if not col_half[a]:
                return out[a].at[:, pl.ds(core, 1)]
            cols = out[a].shape[-1] // 2
            return out[a].at[:, :, pl.ds(pl.multiple_of(core * cols, cols), cols)]

        cps = []
        for a in range(n):
            cp = _remote(half(a, c), half(a, c), send_sems.at[a], recv_sems.at[a], (x, y, 1 - c))
            cp.start()
            cps.append(cp)
        for a in range(n):
            _remote(half(a, 1 - c), half(a, 1 - c), send_sems.at[a], recv_sems.at[a], (x, y, 1 - c)).wait_recv()
        for cp in cps:
            cp.wait_send()

    out_shape = [jax.ShapeDtypeStruct(a.shape, a.dtype) for a in halves]
    return pl.pallas_call(
        body, name="rs_sibling_share", out_shape=out_shape, in_specs=[ANY] * n, out_specs=[ANY] * n,
        input_output_aliases={a: a for a in range(n)}, scratch_shapes=[DMA((n,)), DMA((n,))],
        compiler_params=pltpu.CompilerParams(has_side_effects=True),
    )(*halves)


def _add_sibling(g, land, core):
    _, _, r, c = g.shape
    rb = _row_block(r, c * (3 * 2 * 2 + 2 * 4))

    def body(core_ref, g_ref, l_ref, o_ref):
        o_ref[...] = (g_ref[...].astype(F32) + l_ref[...].astype(F32)).astype(o_ref.dtype)

    return pl.pallas_call(
        body, name="rs_add_sibling", out_shape=jax.ShapeDtypeStruct((N_CHIPS, r, c), g.dtype),
        grid_spec=pltpu.PrefetchScalarGridSpec(
            num_scalar_prefetch=1, grid=(N_CHIPS, r // rb),
            in_specs=[pl.BlockSpec((None, None, rb, c), lambda j, i, core_ref: (j, core_ref[0], i, 0)),
                      pl.BlockSpec((None, None, rb, c), lambda j, i, core_ref: (j, 0, i, 0))],
            out_specs=pl.BlockSpec((None, rb, c), lambda j, i, core_ref: (j, i, 0))),
        compiler_params=pltpu.CompilerParams(dimension_semantics=("parallel", "parallel"), vmem_limit_bytes=VMEM_LIMIT),
    )(core, g, land)


def _add_chips(part, land, where, layer=0, n_layers=1, into=None, col_half=False, rows=(0, 1)):
    _, r, c = part.shape
    sub, n_sub = rows
    rb = _row_block(r, c * (4 * 2 * 2 + 4 * 2 + 2 * 4))

    def body(where_ref, p_ref, l_ref, *rest):
        acc = p_ref[...].astype(F32)
        for m in range(N_CHIPS - 1):
            acc = acc + l_ref[m].astype(F32)
        rest[-1][...] = acc

    in_specs = [pl.BlockSpec((None, rb, c), lambda i, where_ref: (where_ref[0], i, 0)),
                pl.BlockSpec((N_CHIPS - 1, rb, c), lambda i, where_ref: (0, i, 0))]
    args = [where, part, land]
    if into is not None:
        in_specs.append(ANY)
        args.append(into)
    if col_half:
        out_shape = jax.ShapeDtypeStruct((n_layers, r, 2 * c), F32)
        out_spec = pl.BlockSpec((None, rb, c), lambda i, where_ref: (layer, i, where_ref[1]))
    else:
        out_shape = jax.ShapeDtypeStruct((n_layers, 2, n_sub * r, c), F32)
        out_spec = pl.BlockSpec((None, None, rb, c), lambda i, where_ref: (layer, where_ref[1], sub * (r // rb) + i, 0))
    return pl.pallas_call(
        body, name="rs_add_chips", out_shape=out_shape,
        grid_spec=pltpu.PrefetchScalarGridSpec(
            num_scalar_prefetch=1, grid=(r // rb,), in_specs=in_specs, out_specs=out_spec),
        input_output_aliases={} if into is None else {3: 0},
        compiler_params=pltpu.CompilerParams(dimension_semantics=("parallel",), vmem_limit_bytes=VMEM_LIMIT),
    )(*args)


def _sum_small(smg):
    def body(s_ref, o_ref):
        acc = s_ref[0]
        for j in range(1, N_DEV):
            acc = acc + s_ref[j]
        o_ref[...] = acc

    return pl.pallas_call(body, name="rs_sum_small", out_shape=jax.ShapeDtypeStruct(smg.shape[1:], F32))(smg)


def _pool_windows(ext_ref, g, gw, tm, first_row):
    w = POOL_WINDOWS[g]
    slab = ext_ref[:, g * gw:(g + 1) * gw]
    p, k = slab, 1
    while k < w:
        p = p + pltpu.roll(p, k, 0)
        k *= 2
    t = first_row + lax.broadcasted_iota(jnp.int32, (tm, 1), 0)
    cnt = jnp.minimum(t + 1, w).astype(F32)
    return p[POOL_HALO:] / cnt - slab[POOL_HALO:]


def _fwd_pool(x, small, scale, poolw, comm=None):
    t, d = x.shape
    tm = _token_tile(t)
    gw = d // len(POOL_WINDOWS)

    def body(x_ref, sm_ref, sc_ref, w_ref, h_ref, ext_ref, mix_ref):
        i = pl.program_id(0)

        @pl.when(i == 0)
        def _():
            ext_ref[0:POOL_HALO, :] = jnp.zeros((POOL_HALO, d), F32)

        @pl.when(i > 0)
        def _():
            ext_ref[0:POOL_HALO, :] = ext_ref[tm:tm + POOL_HALO, :]

        xv = x_ref[...]
        xh, _ = _rms(xv)
        ext_ref[POOL_HALO:, :] = xh * sm_ref[0:1, :]
        for g in range(len(POOL_WINDOWS)):
            pooled = _pool_windows(ext_ref, g, gw, tm, i * tm)
            cols = slice(g * gw, (g + 1) * gw)
            mix_ref[:, cols] = _dot(pooled.astype(BF16), w_ref[g]) * sc_ref[:, cols]
        mh, _ = _rms(mix_ref[...])
        h_ref[...] = xv + mh * sm_ref[1:2, :]

    (h,), got = _hosted(
        body, comm, name="fwd_pool", grid=(t // tm,), out_shape=[jax.ShapeDtypeStruct((t, d), F32)],
        in_specs=[pl.BlockSpec((tm, d), lambda i: (i, 0)), _resident(small.shape, lambda i: (0, 0)),
                  _resident(scale.shape, lambda i: (0, 0)), _resident(poolw.shape, lambda i: (0, 0, 0))],
        out_specs=[pl.BlockSpec((tm, d), lambda i: (i, 0))],
        scratch_shapes=[pltpu.VMEM((POOL_HALO + tm, d), F32), pltpu.VMEM((tm, d), F32)],
        args=[x, small, scale, poolw])
    return h, got


def _fwd_ffn(h, small, wgu, wd, layer, comm=None, target=None):
    t, d = h.shape
    tm = _token_tile(t)
    steps = t // tm
    fc = wgu.shape[-1]
    f = 2 * fc
    g_in, g_out = 4 * layer + 2, 4 * layer + 3
    with_loss = target is not None

    def body(h_ref, *refs):
        if with_loss:
            t_ref, sm_ref, wgu_ref, wd_ref, o_ref, gu_ref, ff_ref, n_ref, l_ref, acc_ref = refs
        else:
            sm_ref, wgu_ref, wd_ref, o_ref, gu_ref, ff_ref, n_ref = refs
        hv = h_ref[...]
        hh, _ = _rms(hv)
        n = (hh * sm_ref[g_in:g_in + 1, :]).astype(BF16)
        n_ref[...] = n
        ff = None
        for j in range(2):
            gate = _dot(n, wgu_ref[j])
            up = _dot(n, wgu_ref[2 + j])
            gu_ref[:, j * fc:(j + 1) * fc] = gate.astype(BF16)
            gu_ref[:, f + j * fc:f + (j + 1) * fc] = up.astype(BF16)
            act = (gate * jax.nn.sigmoid(gate) * up).astype(BF16)
            part = _dot(act, wd_ref[j * fc:(j + 1) * fc, :])
            ff = part if ff is None else ff + part
        ff_ref[...] = ff
        fh, _ = _rms(ff)
        out = hv + fh * sm_ref[g_out:g_out + 1, :]
        if not with_loss:
            o_ref[...] = out
            return
        i = pl.program_id(0)
        e = out - t_ref[...]
        o_ref[...] = e * (1.0 / d)

        @pl.when(i == 0)
        def _():
            acc_ref[...] = jnp.zeros_like(acc_ref)

        acc_ref[...] += _colsum(e * e)

        @pl.when(i == steps - 1)
        def _():
            l_ref[...] = jnp.full(l_ref.shape, 0.5 / d, F32) * jnp.sum(acc_ref[...])

    row = lambda i: (i, 0)
    out_shape = [jax.ShapeDtypeStruct((t, d), F32), jax.ShapeDtypeStruct((t, 2 * f), BF16),
                 jax.ShapeDtypeStruct((t, d), F32), jax.ShapeDtypeStruct((t, d), BF16)]
    out_specs = [pl.BlockSpec((tm, d), row), pl.BlockSpec((tm, 2 * f), row), pl.BlockSpec((tm, d), row),
                 pl.BlockSpec((tm, d), row)]
    weight_specs = [_resident(small.shape, lambda i: (0, 0)), _resident(wgu.shape, lambda i: (0, 0, 0)),
                    _resident(wd.shape, lambda i: (0, 0))]
    if with_loss:
        return _hosted(
            body, comm, name=f"fwd_ffn{layer}_loss", grid=(steps,),
            out_shape=out_shape + [jax.ShapeDtypeStruct((8, 128), F32)],
            in_specs=[pl.BlockSpec((tm, d), row), pl.BlockSpec((tm, d), row)] + weight_specs,
            out_specs=out_specs + [pl.BlockSpec((8, 128), lambda i: (0, 0))],
            scratch_shapes=[pltpu.VMEM((1, d), F32)], args=[h, target, small, wgu, wd])
    return _hosted(
        body, comm, name=f"fwd_ffn{layer}", grid=(steps,), out_shape=out_shape,
        in_specs=[pl.BlockSpec((tm, d), row)] + weight_specs, out_specs=out_specs, args=[h, small, wgu, wd])


def _fwd_conv(h, small, win, wout, comm=None):
    t, d = h.shape
    tm = _token_tile(t)
    pc = win.shape[-1]

    def body(h_ref, sm_ref, win_ref, wout_ref, o_ref, proj_ref, y_ref, n_ref, pj_ref, uext_ref):
        i = pl.program_id(0)

        @pl.when(i == 0)
        def _():
            uext_ref[0:CONV_HALO, :] = jnp.zeros((CONV_HALO, d), F32)

        @pl.when(i > 0)
        def _():
            uext_ref[0:CONV_HALO, :] = uext_ref[tm:tm + CONV_HALO, :]

        hv = h_ref[...]
        hh, _ = _rms(hv)
        n = (hh * sm_ref[4:5, :]).astype(BF16)
        n_ref[...] = n
        for k in range(N_CHIPS):
            pj_ref[:, k * pc:(k + 1) * pc] = _dot(n, win_ref[k])
        proj_ref[...] = pj_ref[...].astype(BF16)
        uext_ref[CONV_HALO:, :] = pj_ref[:, d:2 * d] * pj_ref[:, 2 * d:]
        taps = [sm_ref[8 + j:9 + j, :] for j in range(3)]
        full = uext_ref[...]
        conv = (full[CONV_HALO:] * taps[2] + pltpu.roll(full, 1, 0)[CONV_HALO:] * taps[1]
                + pltpu.roll(full, 2, 0)[CONV_HALO:] * taps[0])
        y = _dot((pj_ref[:, 0:d] * conv).astype(BF16), wout_ref[...])
        y_ref[...] = y
        yh, _ = _rms(y)
        o_ref[...] = hv + yh * sm_ref[5:6, :]

    row = lambda i: (i, 0)
    return _hosted(
        body, comm, name="fwd_conv", grid=(t // tm,),
        out_shape=[jax.ShapeDtypeStruct((t, d), F32), jax.ShapeDtypeStruct((t, 3 * d), BF16),
                   jax.ShapeDtypeStruct((t, d), F32), jax.ShapeDtypeStruct((t, d), BF16)],
        in_specs=[pl.BlockSpec((tm, d), row), _resident(small.shape, lambda i: (0, 0)),
                  _resident(win.shape, lambda i: (0, 0, 0)), _resident(wout.shape, lambda i: (0, 0))],
        out_specs=[pl.BlockSpec((tm, d), row), pl.BlockSpec((tm, 3 * d), row), pl.BlockSpec((tm, d), row),
                   pl.BlockSpec((tm, d), row)],
        scratch_shapes=[pltpu.VMEM((tm, 3 * d), F32), pltpu.VMEM((CONV_HALO + tm, d), F32)],
        args=[h, small, win, wout])


def _bwd_ffn(dh, h, ff, gu, small, wgu, wd, layer, comm=None):
    t, d = h.shape
    tm = _token_tile(t, 256)
    fc = wgu.shape[-1]
    f = 2 * fc
    g_in, g_out = 4 * layer + 2, 4 * layer + 3

    def body(dh_ref, h_ref, ff_ref, gu_ref, sm_ref, wgu_ref, wd_ref, o_ref, dgu_ref, dff_ref, act_ref, sg_ref):
        i = pl.program_id(0)

        @pl.when(i == 0)
        def _():
            sg_ref[...] = jnp.zeros_like(sg_ref)

        dy = dh_ref[...]
        fh, r3 = _rms(ff_ref[...])
        sg_ref[1:2, :] += _colsum(dy * fh)
        dff = _rms_bwd(dy, fh, r3, sm_ref[g_out:g_out + 1, :]).astype(BF16)
        dff_ref[...] = dff
        for j in range(2):
            dact = _dot_nt(dff, wd_ref[j * fc:(j + 1) * fc, :])
            gate = gu_ref[:, j * fc:(j + 1) * fc].astype(F32)
            up = gu_ref[:, f + j * fc:f + (j + 1) * fc].astype(F32)
            sig = jax.nn.sigmoid(gate)
            silu = gate * sig
            act_ref[:, j * fc:(j + 1) * fc] = (silu * up).astype(BF16)
            dgu_ref[:, j * fc:(j + 1) * fc] = (dact * up * (sig * (1.0 + gate * (1.0 - sig)))).astype(BF16)
            dgu_ref[:, f + j * fc:f + (j + 1) * fc] = (dact * silu).astype(BF16)
        dn = None
        for k in range(N_CHIPS):
            part = _dot_nt(dgu_ref[:, k * fc:(k + 1) * fc], wgu_ref[k])
            dn = part if dn is None else dn + part
        hh, r2 = _rms(h_ref[...])
        sg_ref[0:1, :] += _colsum(dn * hh)
        o_ref[...] = dy + _rms_bwd(dn, hh, r2, sm_ref[g_in:g_in + 1, :])

    row = lambda i: (i, 0)
    return _hosted(
        body, comm, name=f"bwd_ffn{layer}", grid=(t // tm,),
        out_shape=[jax.ShapeDtypeStruct((t, d), F32), jax.ShapeDtypeStruct((t, 2 * f), BF16),
                   jax.ShapeDtypeStruct((t, d), BF16), jax.ShapeDtypeStruct((t, f), BF16),
                   jax.ShapeDtypeStruct((8, d), F32)],
        in_specs=[pl.BlockSpec((tm, d), row), pl.BlockSpec((tm, d), row), pl.BlockSpec((tm, d), row),
                  pl.BlockSpec((tm, 2 * f), row), _resident(small.shape, lambda i: (0, 0)),
                  _resident(wgu.shape, lambda i: (0, 0, 0)), _resident(wd.shape, lambda i: (0, 0))],
        out_specs=[pl.BlockSpec((tm, d), row), pl.BlockSpec((tm, 2 * f), row), pl.BlockSpec((tm, d), row),
                   pl.BlockSpec((tm, f), row), pl.BlockSpec((8, d), lambda i: (0, 0))],
        args=[dh, h, ff, gu, small, wgu, wd])


def _bwd_conv(dh, h, y, proj, small, win, wout, comm=None):
    t, d = h.shape
    tm = _token_tile(t)
    steps = t // tm
    pc = win.shape[-1]
    halo_blocks = tm // 16

    def body(dh_ref, h_ref, y_ref, proj_ref, halo_ref, sm_ref, win_ref, wout_ref,
             o_ref, dproj_ref, dy_ref, bc_ref, sg_ref, uext_ref, dcext_ref, carry_ref):
        i = pl.program_id(0)
        tile = steps - 1 - i

        @pl.when(i == 0)
        def _():
            sg_ref[...] = jnp.zeros_like(sg_ref)
            carry_ref[...] = jnp.zeros_like(carry_ref)

        dy = dh_ref[...]
        yh, r1 = _rms(y_ref[...])
        sg_ref[1:2, :] += _colsum(dy * yh)
        dyv = _rms_bwd(dy, yh, r1, sm_ref[5:6, :]).astype(BF16)
        dy_ref[...] = dyv
        dbc = _dot_nt(dyv, wout_ref[...])
        b = proj_ref[:, 0:d].astype(F32)
        cg = proj_ref[:, d:2 * d].astype(F32)
        v = proj_ref[:, 2 * d:].astype(F32)
        halo = halo_ref[...].astype(F32)[16 - CONV_HALO:]
        uh = halo[:, d:2 * d] * halo[:, 2 * d:]
        uext_ref[0:CONV_HALO, :] = jnp.where(tile > 0, uh, jnp.zeros_like(uh))
        uext_ref[CONV_HALO:, :] = cg * v
        taps = [sm_ref[8 + j:9 + j, :] for j in range(3)]
        full = uext_ref[...]
        u0 = full[CONV_HALO:]
        u1 = pltpu.roll(full, 1, 0)[CONV_HALO:]
        u2 = pltpu.roll(full, 2, 0)[CONV_HALO:]
        conv = u0 * taps[2] + u1 * taps[1] + u2 * taps[0]
        bc_ref[...] = (b * conv).astype(BF16)
        dconv = dbc * b
        sg_ref[4:5, :] += _colsum(dconv * u0)
        sg_ref[3:4, :] += _colsum(dconv * u1)
        sg_ref[2:3, :] += _colsum(dconv * u2)
        dcext_ref[0:tm, :] = dconv
        dcext_ref[tm:, :] = carry_ref[...]
        carry_ref[...] = dconv[0:CONV_HALO]
        dfull = dcext_ref[...]
        n8 = tm + CONV_HALO
        du = (dfull[0:tm] * taps[2] + pltpu.roll(dfull, n8 - 1, 0)[0:tm] * taps[1]
              + pltpu.roll(dfull, n8 - 2, 0)[0:tm] * taps[0])
        dproj_ref[:, 0:d] = (dbc * conv).astype(BF16)
        dproj_ref[:, d:2 * d] = (du * v).astype(BF16)
        dproj_ref[:, 2 * d:] = (du * cg).astype(BF16)
        dn = None
        for k in range(N_CHIPS):
            part = _dot_nt(dproj_ref[:, k * pc:(k + 1) * pc], win_ref[k])
            dn = part if dn is None else dn + part
        hh, r0 = _rms(h_ref[...])
        sg_ref[0:1, :] += _colsum(dn * hh)
        o_ref[...] = dy + _rms_bwd(dn, hh, r0, sm_ref[4:5, :])

    rev = lambda i: (steps - 1 - i, 0)
    before = lambda i: (jnp.maximum((steps - 1 - i) * halo_blocks - 1, 0), 0)
    return _hosted(
        body, comm, name="bwd_conv", grid=(steps,),
        out_shape=[jax.ShapeDtypeStruct((t, d), F32), jax.ShapeDtypeStruct((t, 3 * d), BF16),
                   jax.ShapeDtypeStruct((t, d), BF16), jax.ShapeDtypeStruct((t, d), BF16),
                   jax.ShapeDtypeStruct((8, d), F32)],
        in_specs=[pl.BlockSpec((tm, d), rev), pl.BlockSpec((tm, d), rev), pl.BlockSpec((tm, d), rev),
                  pl.BlockSpec((tm, 3 * d), rev), pl.BlockSpec((16, 3 * d), before),
                  _resident(small.shape, lambda i: (0, 0)), _resident(win.shape, lambda i: (0, 0, 0)),
                  _resident(wout.shape, lambda i: (0, 0))],
        out_specs=[pl.BlockSpec((tm, d), rev), pl.BlockSpec((tm, 3 * d), rev), pl.BlockSpec((tm, d), rev),
                   pl.BlockSpec((tm, d), rev), pl.BlockSpec((8, d), lambda i: (0, 0))],
        scratch_shapes=[pltpu.VMEM((CONV_HALO + tm, d), F32), pltpu.VMEM((tm + CONV_HALO, d), F32),
                        pltpu.VMEM((CONV_HALO, d), F32)],
        args=[dh, h, y, proj, proj, small, win, wout])


def _bwd_pool(dh, x, small, scale, poolw, comm=None):
    t, d = x.shape
    tm = _token_tile(t)
    steps = t // tm
    ng = len(POOL_WINDOWS)
    gw = d // ng
    halo_blocks = tm // POOL_HALO

    def body(dh_ref, x_ref, halo_ref, sm_ref, sc_ref, w_ref, o_ref, dw_ref, sg_ref,
             ext_ref, mix_ref, mm_ref, pb_ref, qext_ref, dhn_ref, carry_ref):
        i = pl.program_id(0)
        tile = steps - 1 - i

        @pl.when(i == 0)
        def _():
            sg_ref[...] = jnp.zeros_like(sg_ref)
            dw_ref[...] = jnp.zeros_like(dw_ref)
            carry_ref[...] = jnp.zeros_like(carry_ref)

        g0 = sm_ref[0:1, :]
        xv = x_ref[...]
        xh, r0 = _rms(xv)
        hx, _ = _rms(halo_ref[...])
        ext_ref[0:POOL_HALO, :] = jnp.where(tile > 0, hx * g0, jnp.zeros_like(hx))
        ext_ref[POOL_HALO:, :] = xh * g0
        for g in range(ng):
            pooled = _pool_windows(ext_ref, g, gw, tm, tile * tm)
            cols = slice(g * gw, (g + 1) * gw)
            pb = pooled.astype(BF16)
            pb_ref[:, cols] = pb
            mm = _dot(pb, w_ref[g])
            mm_ref[:, cols] = mm
            mix_ref[:, cols] = mm * sc_ref[:, cols]
        dy = dh_ref[...]
        mh, r1 = _rms(mix_ref[...])
        sg_ref[1:2, :] += _colsum(dy * mh)
        dmix = _rms_bwd(dy, mh, r1, sm_ref[1:2, :])
        sg_ref[2:3, :] += _colsum(dmix * mm_ref[...])
        mix_ref[...] = dmix * sc_ref[...]
        n16 = tm + POOL_HALO
        for g in range(ng):
            w = POOL_WINDOWS[g]
            cols = slice(g * gw, (g + 1) * gw)
            dmm = mix_ref[:, cols].astype(BF16)
            dpooled = _dot_nt(dmm, w_ref[g])
            dw_ref[g] += _dot_tn(pb_ref[:, cols], dmm)
            trow = tile * tm + lax.broadcasted_iota(jnp.int32, (tm, 1), 0)
            q = dpooled / jnp.minimum(trow + 1, w).astype(F32)
            qext_ref[0:tm, cols] = q
            qext_ref[tm:, cols] = carry_ref[:, cols]
            carry_ref[:, cols] = q[0:POOL_HALO]
            p, k = qext_ref[:, cols], 1
            while k < w:
                p = p + pltpu.roll(p, n16 - k, 0)
                k *= 2
            dhn_ref[:, cols] = p[0:tm] - dpooled
        dhn = dhn_ref[...]
        sg_ref[0:1, :] += _colsum(dhn * xh)
        o_ref[...] = dy + _rms_bwd(dhn, xh, r0, g0)

    rev = lambda i: (steps - 1 - i, 0)
    before = lambda i: (jnp.maximum((steps - 1 - i) * halo_blocks - 1, 0), 0)
    return _hosted(
        body, comm, name="bwd_pool", grid=(steps,),
        out_shape=[jax.ShapeDtypeStruct((t, d), F32), jax.ShapeDtypeStruct((ng, gw, gw), F32),
                   jax.ShapeDtypeStruct((8, d), F32)],
        in_specs=[pl.BlockSpec((tm, d), rev), pl.BlockSpec((tm, d), rev), pl.BlockSpec((POOL_HALO, d), before),
                  _resident(small.shape, lambda i: (0, 0)), _resident(scale.shape, lambda i: (0, 0)),
                  _resident(poolw.shape, lambda i: (0, 0, 0))],
        out_specs=[pl.BlockSpec((tm, d), rev), pl.BlockSpec((ng, gw, gw), lambda i: (0, 0, 0)),
                   pl.BlockSpec((8, d), lambda i: (0, 0))],
        scratch_shapes=[pltpu.VMEM((POOL_HALO + tm, d), F32), pltpu.VMEM((tm, d), F32), pltpu.VMEM((tm, d), F32),
                        pltpu.VMEM((tm, d), BF16), pltpu.VMEM((tm + POOL_HALO, d), F32), pltpu.VMEM((tm, d), F32),
                        pltpu.VMEM((POOL_HALO, d), F32)],
        args=[dh, x, x, small, scale, poolw])


def _weight_grad(a, b, bm, bn, half_on, name, comm=None, rows=(0, 1)):
    t, m = a.shape
    _, n = b.shape
    if half_on == "a":
        a_cols, b_cols = 2 * bm, bn
    else:
        a_cols, b_cols = bm, 2 * bn
    steps = max(m // a_cols, n // b_cols)
    sub, n_sub = rows
    tr = bm // n_sub

    def spec(cols, total):
        if cols == total:
            return _resident((t, cols), lambda s: (0, 0))
        return pl.BlockSpec((t, cols), lambda s: (0, jnp.minimum(s, steps - 1)))

    def tile(a_ref, b_ref, half):
        if half_on == "a":
            first = half * bm + sub * tr
            return _dot_tn(a_ref[:, first:first + tr], b_ref[...])
        return _dot_tn(a_ref[...], b_ref[:, half * bn:(half + 1) * bn])

    def body(a_ref, b_ref, parts_ref, land_ref, own_ref, stage_ref, got_ref, send_sems, recv_sems, got_sem):
        s = pl.program_id(0)
        x, y, c, _ = _place()

        def send(j):
            return _remote(stage_ref.at[j % 2], land_ref.at[j], send_sems.at[j], recv_sems.at[j], (x, y, 1 - c))

        def fetch(j):
            return pltpu.make_async_copy(land_ref.at[j], got_ref, got_sem)

        @pl.when(s >= 1)
        def _():
            send(s - 1).wait_recv()
            fetch(s - 1).start()

        @pl.when(s < steps)
        def _():
            @pl.when(s >= 2)
            def _():
                send(s - 2).wait_send()

            for core in range(2):
                @pl.when(c == core)
                def _():
                    stage_ref[s % 2] = tile(a_ref, b_ref, 1 - core).astype(BF16)
                    own_ref[s % 2] = tile(a_ref, b_ref, core)

            send(s).start()

        @pl.when(s >= 1)
        def _():
            fetch(s - 1).wait()
            parts_ref[...] = (own_ref[(s - 1) % 2] + got_ref[...].astype(F32)).astype(BF16)

        @pl.when(s == steps)
        def _():
            for j in range(max(steps - 2, 0), steps):
                send(j).wait_send()

    (parts, _), got = _hosted(
        body, comm, name=name, grid=(steps + 1,),
        out_shape=[jax.ShapeDtypeStruct((steps, tr, bn), BF16), jax.ShapeDtypeStruct((steps, tr, bn), BF16)],
        in_specs=[spec(a_cols, m), spec(b_cols, n)],
        out_specs=[pl.BlockSpec((None, tr, bn), lambda s: (jnp.maximum(s - 1, 0), 0, 0)), ANY],
        scratch_shapes=[pltpu.VMEM((2, tr, bn), F32), pltpu.VMEM((2, tr, bn), BF16), pltpu.VMEM((tr, bn), BF16),
                        DMA((steps,)), DMA((steps,)), DMA],
        args=[a, b])
    return parts, got


def _adamw(w, g, m, v, name):
    r, c = w.shape
    rb = _row_block(r, c * (8 * 4 * 2 + 4 * 4))
    bc1 = 1.0 - ADAM_B1 ** ADAM_STEP
    bc2 = 1.0 - ADAM_B2 ** ADAM_STEP

    def body(w_ref, g_ref, m_ref, v_ref, d_ref, nm_ref, nv_ref, go_ref):
        gv = g_ref[...]
        go_ref[...] = gv
        nm = ADAM_B1 * m_ref[...] + (1.0 - ADAM_B1) * gv
        nv = ADAM_B2 * v_ref[...] + (1.0 - ADAM_B2) * (gv * gv)
        nm_ref[...] = nm
        nv_ref[...] = nv
        d_ref[...] = -ADAM_LR * ((nm / bc1) / (jnp.sqrt(nv / bc2) + ADAM_EPS) + ADAM_WD * w_ref[...])

    spec = pl.BlockSpec((rb, c), lambda i: (i, 0))
    return pl.pallas_call(
        body, name=name, grid=(r // rb,), out_shape=[jax.ShapeDtypeStruct((r, c), F32)] * 4,
        in_specs=[spec] * 4, out_specs=[spec] * 4,
        compiler_params=pltpu.CompilerParams(dimension_semantics=("parallel",), vmem_limit_bytes=VMEM_LIMIT),
    )(w, g, m, v)


def kernel(x, norm_gains, pool_w, pool_scale, conv_in_w, conv_w, conv_out_w, ffn_gate_up_w, ffn_down_w, loss_target, m_norm_gains, m_pool_w, m_pool_scale, m_conv_in_w, m_conv_w, m_conv_out_w, m_ffn_gate_up_w, m_ffn_down_w, v_norm_gains, v_pool_w, v_pool_scale, v_conv_in_w, v_conv_w, v_conv_out_w, v_ffn_gate_up_w, v_ffn_down_w):
    _, t, d = x.shape
    dq = d // N_CHIPS
    ng = len(POOL_WINDOWS)
    gw = d // ng
    fq = ffn_down_w.shape[1]
    f = N_CHIPS * fq
    fc = f // 2
    core = lax.axis_index("c")
    chip = 2 * lax.axis_index("x") + lax.axis_index("y")
    core_arr = jnp.reshape(core, (1,)).astype(jnp.int32)
    where_arr = jnp.stack([chip, core]).astype(jnp.int32)
    x2, target = x[0], loss_target[0]

    small_loc = jnp.concatenate(
        [norm_gains.reshape(8, dq), conv_w[0], jnp.zeros((5, dq), F32)], axis=0).reshape(1, 2, 8, dq)
    pool_loc = pool_w.astype(BF16).reshape(1, 2, ng // 2 * (gw // N_CHIPS), gw)
    wgu_loc = ffn_gate_up_w.astype(BF16).reshape(2, 2, d // 2, fc)
    wd_loc = ffn_down_w.astype(BF16).reshape(2, 2, fq // 2, d)
    win_loc = conv_in_w.astype(BF16).reshape(1, 2, d // 2, -1)
    wout_loc = conv_out_w.astype(BF16).reshape(1, 2, dq // 2, d)

    def ffn_weights(wgu_f, wd_f):
        return wgu_f.reshape(N_CHIPS, d, fc), wd_f.reshape(f, d)

    pool_f, small_f = _alone(_Gather([(pool_loc, 0), (small_loc, 0)]), "ag_first")
    poolw = pool_f.reshape(N_CHIPS, ng, gw // N_CHIPS, gw).transpose(1, 0, 2, 3).reshape(ng, gw, gw)
    small = small_f.transpose(1, 2, 0, 3).reshape(16, d)
    h1, got = _fwd_pool(x2, small, pool_scale, poolw, _Gather([(wgu_loc, 0), (wd_loc, 0)]))
    wgu0, wd0 = ffn_weights(*got)
    (h2, gu0, ff0, n0), got = _fwd_ffn(h1, small, wgu0, wd0, 0, _Gather([(win_loc, 0), (wout_loc, 0), (wgu_loc, 1)]))
    win_f, wout_f = got[0].reshape(N_CHIPS, d, -1), got[1].reshape(d, d)
    (h3, proj, y, nc), got_d = _fwd_conv(h2, small, win_f, wout_f, _Gather([(wd_loc, 1)]))
    wgu1, wd1 = ffn_weights(got[2], got_d[0])
    (dh4, gu1, ff1, n1, loss_blk), _ = _fwd_ffn(h3, small, wgu1, wd1, 1, target=target)

    (dh3, dgu1, dff1, act1, sg_f1), _ = _bwd_ffn(dh4, h3, ff1, gu1, small, wgu1, wd1, 1)
    parts_d1, _ = _weight_grad(act1, dff1, fc, d // 2, "b", "dw_down1")
    parts_d1 = parts_d1.reshape(N_CHIPS, fq, d // 2)
    parts_gu1, recv_d1 = _weight_grad(n1, dgu1, d // 2, fc, "a", "dw_gate_up1", _ChipExchange([parts_d1]))
    (dh2, dproj, dyv, bcv, sg_c), recv_gu1 = _bwd_conv(dh3, h2, y, proj, small, win_f, wout_f, _ChipExchange([parts_gu1]))
    parts_in, _ = _weight_grad(nc, dproj, d // 2, 3 * d // N_CHIPS, "a", "dw_conv_in")
    (dh1, dgu0, dff0, act0, sg_f0), recv_in = _bwd_ffn(dh2, h1, ff0, gu0, small, wgu0, wd0, 0, _ChipExchange([parts_in]))
    (grad_x, dpool, sg_p), _ = _bwd_pool(dh1, x2, small, pool_scale, poolw)
    parts_d0, _ = _weight_grad(act0, dff0, fc, d // 2, "b", "dw_down0")
    parts_d0 = parts_d0.reshape(N_CHIPS, fq, d // 2)
    parts_gu0a, recv_d0 = _weight_grad(n0, dgu0, d // 2, fc, "a", "dw_gate_up0a", _ChipExchange([parts_d0]), (0, 2))
    parts_gu0b, recv_gu0a = _weight_grad(n0, dgu0, d // 2, fc, "a", "dw_gate_up0b", _ChipExchange([parts_gu0a]), (1, 2))
    parts_out, recv_gu0b = _weight_grad(bcv, dyv, dq // 2, d, "a", "dw_conv_out", _ChipExchange([parts_gu0b]))
    g_pool = dpool.astype(BF16).reshape(2, ng // 2, N_CHIPS, gw // N_CHIPS, gw).transpose(2, 0, 1, 3, 4).reshape(
        N_CHIPS, 2, ng // 2 * (gw // N_CHIPS), gw)
    small_g = jnp.concatenate(
        [sg_p[0:2], sg_f0[0:2], sg_c[0:2], sg_f1[0:2], sg_c[2:5], sg_p[2:3],
         jnp.broadcast_to(loss_blk[0:1, 0:1], (1, d)), jnp.zeros((3, d), F32)], axis=0)
    land_p, small_all = _sibling_exchange([g_pool], small_g, "rs_sibling_pool")
    parts_p = _add_sibling(g_pool, land_p, core_arr)
    recv_out, recv_p = _alone(_ChipExchange([parts_out, parts_p]), "rs_chips_last")

    gs_gu = _add_chips(parts_gu1, recv_gu1[0], where_arr, 1, 2)
    gs_gu = _add_chips(parts_gu0a, recv_gu0a[0], where_arr, 0, 2, gs_gu, rows=(0, 2))
    gs_gu = _add_chips(parts_gu0b, recv_gu0b[0], where_arr, 0, 2, gs_gu, rows=(1, 2))
    gs_d = _add_chips(parts_d1, recv_d1[0], where_arr, 1, 2, col_half=True)
    gs_d = _add_chips(parts_d0, recv_d0[0], where_arr, 0, 2, gs_d, col_half=True)
    gs_in = _add_chips(parts_in, recv_in[0], where_arr)
    gs_out = _add_chips(parts_out, recv_out, where_arr)
    gs_pool = _add_chips(parts_p, recv_p, where_arr)
    full = _sibling_share([gs_gu, gs_d, gs_in, gs_out, gs_pool], [False, True, False, False, False])
    small_sum = _sum_small(small_all)
    loss = small_sum[12, 0]

    gg_gu = full[0].reshape(2, d, fc)
    gg_d = full[1].reshape(2, fq, d)
    gg_in = full[2].reshape(1, d, -1)
    gg_out = full[3].reshape(1, dq, d)
    gg_pool = full[4].reshape(1, ng, gw // N_CHIPS, gw)
    mine = lax.dynamic_slice_in_dim(small_sum, chip * dq, dq, axis=1)
    gg_gains = mine[0:8].reshape(2, 4, dq)
    gg_taps = mine[8:11].reshape(1, 3, dq)
    gg_scale = small_sum[11:12]

    grads = [gg_gains, gg_pool, gg_scale, gg_in, gg_taps, gg_out, gg_gu, gg_d]
    weights = [norm_gains, pool_w, pool_scale, conv_in_w, conv_w, conv_out_w, ffn_gate_up_w, ffn_down_w]
    ms = [m_norm_gains, m_pool_w, m_pool_scale, m_conv_in_w, m_conv_w, m_conv_out_w, m_ffn_gate_up_w, m_ffn_down_w]
    vs = [v_norm_gains, v_pool_w, v_pool_scale, v_conv_in_w, v_conv_w, v_conv_out_w, v_ffn_gate_up_w, v_ffn_down_w]
    names = ["gains", "pool_w", "pool_scale", "conv_in", "taps", "conv_out", "gate_up", "down"]
    deltas, new_ms, new_vs, grads_out = [], [], [], []
    for w, g, m, v, nm in zip(weights, grads, ms, vs, names):
        flat = (-1, w.shape[-1])
        dl, m2, v2, g2 = _adamw(w.reshape(flat), g.reshape(flat), m.reshape(flat), v.reshape(flat), "adamw_" + nm)
        deltas.append(dl.reshape(w.shape))
        new_ms.append(m2.reshape(w.shape))
        new_vs.append(v2.reshape(w.shape))
        grads_out.append(g2.reshape(w.shape))
    return (loss, grad_x[None], *grads_out, *deltas, *new_ms, *new_vs)
```

```python
import jax
import jax.numpy as jnp
from jax import lax
from jax.experimental import pallas as pl
from jax.experimental.pallas import tpu as pltpu

RMS_EPS = 1e-6
POOL_WINDOWS = (2, 4, 8, 16)
POOL_HALO = 16
CONV_HALO = 8
N_CHIPS = 4
N_DEV = 8
ADAM_LR = 0.001
ADAM_B1 = 0.9
ADAM_B2 = 0.999
ADAM_EPS = 1e-08
ADAM_WD = 0.01
ADAM_STEP = 10
VMEM_LIMIT = 56 * 2**20
STREAM_BUDGET = 24 * 2**20
MESH = pl.DeviceIdType.MESH
ANY = pl.BlockSpec(memory_space=pl.ANY)
DMA = pltpu.SemaphoreType.DMA
BF16 = jnp.bfloat16
F32 = jnp.float32


def _token_tile(t, rows=512):
    return min(rows, t)


def _rms(x):
    r = lax.rsqrt(jnp.mean(x * x, axis=-1, keepdims=True) + RMS_EPS)
    return x * r, r


def _rms_bwd(dy, xh, r, g):
    a = dy * g
    return r * (a - xh * jnp.mean(a * xh, axis=-1, keepdims=True))


def _dot(a, b):
    return jnp.dot(a, b, preferred_element_type=F32)


def _dot_nt(a, b):
    return lax.dot_general(a, b, (((1,), (1,)), ((), ())), preferred_element_type=F32)


def _dot_tn(a, b):
    return lax.dot_general(a, b, (((0,), (0,)), ((), ())), preferred_element_type=F32)


def _colsum(a):
    return jnp.sum(a, axis=0, keepdims=True)


def _resident(block, index_map):
    return pl.BlockSpec(block, index_map, pipeline_mode=pl.Buffered(1))


def _row_block(r, row_bytes):
    best = None
    for rb in range(16, r + 1, 16):
        if r % rb == 0 and rb * row_bytes <= STREAM_BUDGET:
            best = rb
    return best if best is not None else r


def _place():
    x, y, c = lax.axis_index("x"), lax.axis_index("y"), lax.axis_index("c")
    return x, y, c, 2 * x + y


def _dev(chip, core):
    return (chip // 2, chip % 2, core)


def _remote(src, dst, send_sem, recv_sem, device):
    return pltpu.make_async_remote_copy(src_ref=src, dst_ref=dst, send_sem=send_sem, recv_sem=recv_sem,
                                        device_id=device, device_id_type=MESH)


class _Gather:
    def __init__(self, shards):
        n = len(shards)
        self.args = [s for s, _ in shards]
        self.layers = [l for _, l in shards]
        self.out_shape = [jax.ShapeDtypeStruct((N_CHIPS,) + s.shape[1:], s.dtype) for s in self.args]
        self.sems = [DMA((n,)), DMA((n,)), DMA((n, 3)), DMA((n, 3)), DMA((n, 3)), DMA((n, 3))]

    def _own(self, loc, out, sems, a):
        x, y, c, k = _place()
        return _remote(loc[a].at[self.layers[a]], out[a].at[k], sems[0].at[a], sems[1].at[a], (x, y, 1 - c))

    def _ici(self, loc, out, sems, a, m, arrival):
        x, y, c, k = _place()
        dst = out[a].at[k ^ m, c] if arrival else out[a].at[k, c]
        return _remote(loc[a].at[self.layers[a], c], dst, sems[2].at[a, m - 1], sems[3].at[a, m - 1], _dev(k ^ m, c))

    def _forward(self, out, sems, a, m, arrival):
        x, y, c, k = _place()
        got = out[a].at[k ^ m, 1 - c] if arrival else out[a].at[k ^ m, c]
        return _remote(got, got, sems[4].at[a, m - 1], sems[5].at[a, m - 1], (x, y, 1 - c))

    def start(self, loc, out, sems):
        for a in range(len(self.args)):
            for m in range(1, N_CHIPS):
                self._ici(loc, out, sems, a, m, False).start()
            self._own(loc, out, sems, a).start()

    def finish(self, loc, out, sems):
        n = len(self.args)
        for a in range(n):
            for m in range(1, N_CHIPS):
                self._ici(loc, out, sems, a, m, True).wait_recv()
                self._forward(out, sems, a, m, False).start()
        for a in range(n):
            for m in range(1, N_CHIPS):
                self._forward(out, sems, a, m, True).wait_recv()
            self._own(loc, out, sems, a).wait_recv()
        for a in range(n):
            for m in range(1, N_CHIPS):
                self._ici(loc, out, sems, a, m, False).wait_send()
                self._forward(out, sems, a, m, False).wait_send()
            self._own(loc, out, sems, a).wait_send()


class _ChipExchange:
    def __init__(self, parts):
        n = len(parts)
        self.args = list(parts)
        self.out_shape = [jax.ShapeDtypeStruct((N_CHIPS - 1,) + p.shape[1:], p.dtype) for p in parts]
        self.sems = [DMA((n, 3)), DMA((n, 3))]

    def _copy(self, p, land, sems, a, m):
        x, y, c, k = _place()
        return _remote(p[a].at[k ^ m], land[a].at[m - 1], sems[0].at[a, m - 1], sems[1].at[a, m - 1], _dev(k ^ m, c))

    def start(self, p, land, sems):
        for a in range(len(self.args)):
            for m in range(1, N_CHIPS):
                self._copy(p, land, sems, a, m).start()

    def finish(self, p, land, sems):
        for a in range(len(self.args)):
            for m in range(1, N_CHIPS):
                self._copy(p, land, sems, a, m).wait_recv()
        for a in range(len(self.args)):
            for m in range(1, N_CHIPS):
                self._copy(p, land, sems, a, m).wait_send()


def _hosted(body, comm, *, name, grid, in_specs, out_specs, out_shape, args, scratch_shapes=()):
    ni, no, ns = len(in_specs), len(out_shape), len(scratch_shapes)
    if comm is None:
        res = pl.pallas_call(
            body, name=name, grid=grid, in_specs=list(in_specs), out_specs=list(out_specs), out_shape=list(out_shape),
            scratch_shapes=list(scratch_shapes),
            compiler_params=pltpu.CompilerParams(dimension_semantics=("arbitrary",) * len(grid), vmem_limit_bytes=VMEM_LIMIT),
        )(*args)
        return list(res), []
    nc, nco = len(comm.args), len(comm.out_shape)

    def full(*refs):
        cin = refs[ni:ni + nc]
        outs = refs[ni + nc:ni + nc + no]
        cout = refs[ni + nc + no:ni + nc + no + nco]
        scratch = refs[ni + nc + no + nco:ni + nc + no + nco + ns]
        csems = refs[ni + nc + no + nco + ns:]
        first = _all_of([pl.program_id(ax) == 0 for ax in range(len(grid))])
        last = _all_of([pl.program_id(ax) == grid[ax] - 1 for ax in range(len(grid))])

        @pl.when(first)
        def _():
            comm.start(cin, cout, csems)

        body(*refs[:ni], *outs, *scratch)

        @pl.when(last)
        def _():
            comm.finish(cin, cout, csems)

    res = pl.pallas_call(
        full, name=name, grid=grid, in_specs=[*in_specs, *[ANY] * nc], out_specs=[*out_specs, *[ANY] * nco],
        out_shape=[*out_shape, *comm.out_shape], scratch_shapes=[*scratch_shapes, *comm.sems],
        compiler_params=pltpu.CompilerParams(dimension_semantics=("arbitrary",) * len(grid), vmem_limit_bytes=VMEM_LIMIT,
                                             has_side_effects=True),
    )(*args, *comm.args)
    return list(res[:no]), list(res[no:])


def _all_of(conds):
    out = conds[0]
    for c in conds[1:]:
        out = jnp.logical_and(out, c)
    return out


def _alone(comm, name):
    return _hosted(lambda: None, comm, name=name, grid=(1,), in_specs=[], out_specs=[], out_shape=[], args=[])[1]


HBM = pl.BlockSpec(memory_space=pltpu.HBM)
SEM = pl.BlockSpec(memory_space=pltpu.SEMAPHORE)
DATAFLOW = pltpu.SideEffectType.DATAFLOW_SIDE_EFFECTING


class _SplitGather:
    PER_ARRAY = 8

    def __init__(self, shards):
        self.plan = _Gather(shards)
        self.n = len(shards)

    @staticmethod
    def _tables(sems_of):
        class Table:
            def __init__(self, pick):
                self.pick = pick

            @property
            def at(self):
                return self

            def __getitem__(self, idx):
                return self.pick(idx)

        return [Table(lambda a: sems_of[a][0]), Table(lambda a: sems_of[a][1]),
                Table(lambda am: sems_of[am[0]][2 + am[1]]), Table(lambda am: sems_of[am[0]][5 + am[1]])]

    def start(self):
        n, plan, per = self.n, self.plan, self.PER_ARRAY

        def body(*refs):
            loc, land = refs[:n], refs[n:2 * n]
            sems_of = {a: refs[2 * n + per * a:2 * n + per * (a + 1)] for a in range(n)}
            plan.start(loc, land, self._tables(sems_of))
            refs[-1][...] = jnp.zeros_like(refs[-1])

        lands = [pltpu.with_memory_space_constraint(lax.empty(o.shape, o.dtype), pltpu.HBM) for o in plan.out_shape]
        locs = [pltpu.with_memory_space_constraint(a, pltpu.HBM) for a in plan.args]
        res = pl.pallas_call(
            body, name="ag_start",
            out_shape=[*[DMA(())] * (per * n),
                       *[pltpu.HBM(a.shape, a.dtype) for a in plan.args],
                       *[pltpu.HBM(o.shape, o.dtype) for o in plan.out_shape],
                       jax.ShapeDtypeStruct((8, 128), F32)],
            in_specs=[HBM] * (2 * n),
            out_specs=[SEM] * (per * n) + [HBM] * (2 * n) + [pl.BlockSpec(memory_space=pltpu.VMEM)],
            input_output_aliases={i: per * n + i for i in range(2 * n)},
            compiler_params=pltpu.CompilerParams(has_side_effects=DATAFLOW),
        )(*locs, *lands)
        self.sems = {a: list(res[per * a:per * (a + 1)]) for a in range(n)}
        self.locs = list(res[per * n:per * n + n])
        self.lands = list(res[per * n + n:per * n + 2 * n])
        self.token = res[-1]

    def wait(self, idxs, after, name):
        plan, g, per = self.plan, len(idxs), self.PER_ARRAY

        def body(*refs):
            loc = {a: refs[j] for j, a in enumerate(idxs)}
            land = {a: refs[g + j] for j, a in enumerate(idxs)}
            sems = self._tables({a: refs[2 * g + per * j:2 * g + per * (j + 1)] for j, a in enumerate(idxs)})
            for a in idxs:
                for m in range(1, N_CHIPS):
                    plan._ici(loc, land, sems, a, m, True).wait_recv()
                    plan._ici(loc, land, sems, a, m, False).wait_send()
                plan._own(loc, land, sems, a).wait_recv()
                plan._own(loc, land, sems, a).wait_send()

        res = pl.pallas_call(
            body, name=name,
            out_shape=[pltpu.HBM(self.lands[a].shape, self.lands[a].dtype) for a in idxs],
            in_specs=[HBM] * (2 * g) + [SEM] * (per * g) + [pl.BlockSpec(memory_space=pl.ANY)], out_specs=[HBM] * g,
            input_output_aliases={g + j: j for j in range(g)},
            compiler_params=pltpu.CompilerParams(has_side_effects=DATAFLOW),
        )(*[self.locs[a] for a in idxs], *[self.lands[a] for a in idxs],
          *[s for a in idxs for s in self.sems[a]], after)
        return list(res)


def _pass_on(lands, name):
    n = len(lands)

    def body(*refs):
        out = refs[n:2 * n]
        send_sems, recv_sems = refs[2 * n:]
        x, y, c, k = _place()
        cps = []
        for a in range(n):
            for m in range(1, N_CHIPS):
                got = out[a].at[k ^ m, c]
                cp = _remote(got, got, send_sems.at[a, m - 1], recv_sems.at[a, m - 1], (x, y, 1 - c))
                cp.start()
                cps.append(cp)
        for a in range(n):
            for m in range(1, N_CHIPS):
                theirs = out[a].at[k ^ m, 1 - c]
                _remote(theirs, theirs, send_sems.at[a, m - 1], recv_sems.at[a, m - 1], (x, y, 1 - c)).wait_recv()
        for cp in cps:
            cp.wait_send()

    return pl.pallas_call(
        body, name=name, out_shape=[jax.ShapeDtypeStruct(a.shape, a.dtype) for a in lands],
        in_specs=[ANY] * n, out_specs=[ANY] * n, input_output_aliases={a: a for a in range(n)},
        scratch_shapes=[DMA((n, 3)), DMA((n, 3))], compiler_params=pltpu.CompilerParams(has_side_effects=True),
    )(*lands)


def _sibling_exchange(grads, small, name):
    n = len(grads)
    ns = 0 if small is None else 1

    def body(*refs):
        g = refs[:n]
        land = refs[n + ns:2 * n + ns]
        send_sems, recv_sems, own_sem, ssend_sems, srecv_sems = refs[2 * n + 2 * ns:]
        x, y, c, k = _place()
        me = 2 * k + c
        cps = []
        for a in range(n):
            cp = _remote(g[a].at[:, pl.ds(1 - c, 1)], land[a], send_sems.at[a], recv_sems.at[a], (x, y, 1 - c))
            cp.start()
            cps.append(cp)
        if small is not None:
            sm, smg = refs[n], refs[2 * n + 1]
            peers = [me ^ m for m in range(1, N_DEV)]
            ids = [(p // 4, (p // 2) % 2, p % 2) for p in peers]
            own = pltpu.make_async_copy(sm, smg.at[me], own_sem)
            own.start()
            for m in range(1, N_DEV):
                cp = _remote(sm, smg.at[me], ssend_sems.at[m - 1], srecv_sems.at[m - 1], ids[m - 1])
                cp.start()
                cps.append(cp)
            for m in range(1, N_DEV):
                _remote(sm, smg.at[peers[m - 1]], ssend_sems.at[m - 1], srecv_sems.at[m - 1], ids[m - 1]).wait_recv()
            own.wait()
        for cp in cps[:n]:
            cp.wait_recv()
        for cp in cps:
            cp.wait_send()

    out_shape = [jax.ShapeDtypeStruct((N_CHIPS, 1) + a.shape[2:], a.dtype) for a in grads]
    ins = list(grads)
    if small is not None:
        out_shape.append(jax.ShapeDtypeStruct((N_DEV,) + small.shape, small.dtype))
        ins.append(small)
    return pl.pallas_call(
        body, name=name, out_shape=out_shape, in_specs=[ANY] * (n + ns), out_specs=[ANY] * (n + ns),
        scratch_shapes=[DMA((n,)), DMA((n,)), DMA, DMA((N_DEV - 1,)), DMA((N_DEV - 1,))],
        compiler_params=pltpu.CompilerParams(has_side_effects=True),
    )(*ins)


def _sibling_share(halves, col_half):
    n = len(halves)

    def body(*refs):
        out = refs[n:2 * n]
        send_sems, recv_sems = refs[2 * n:]
        x, y, c, k = _place()

        def half(a, core):
            if not col_half[a]:
                return out[a].at[:, pl.ds(core, 1)]
            cols = out[a].shape[-1] // 2
            return out[a].at[:, :, pl.ds(pl.multiple_of(core * cols, cols), cols)]

        cps = []
        for a in range(n):
            cp = _remote(half(a, c), half(a, c), send_sems.at[a], recv_sems.at[a], (x, y, 1 - c))
            cp.start()
            cps.append(cp)
        for a in range(n):
            _remote(half(a, 1 - c), half(a, 1 - c), send_sems.at[a], recv_sems.at[a], (x, y, 1 - c)).wait_recv()
        for cp in cps:
            cp.wait_send()

    out_shape = [jax.ShapeDtypeStruct(a.shape, a.dtype) for a in halves]
    return pl.pallas_call(
        body, name="rs_sibling_share", out_shape=out_shape, in_specs=[ANY] * n, out_specs=[ANY] * n,
        input_output_aliases={a: a for a in range(n)}, scratch_shapes=[DMA((n,)), DMA((n,))],
        compiler_params=pltpu.CompilerParams(has_side_effects=True),
    )(*halves)


def _add_sibling(g, land, core):
    _, _, r, c = g.shape
    rb = _row_block(r, c * (3 * 2 * 2 + 2 * 4))

    def body(core_ref, g_ref, l_ref, o_ref):
        o_ref[...] = (g_ref[...].astype(F32) + l_ref[...].astype(F32)).astype(o_ref.dtype)

    return pl.pallas_call(
        body, name="rs_add_sibling", out_shape=jax.ShapeDtypeStruct((N_CHIPS, r, c), g.dtype),
        grid_spec=pltpu.PrefetchScalarGridSpec(
            num_scalar_prefetch=1, grid=(N_CHIPS, r // rb),
            in_specs=[pl.BlockSpec((None, None, rb, c), lambda j, i, core_ref: (j, core_ref[0], i, 0)),
                      pl.BlockSpec((None, None, rb, c), lambda j, i, core_ref: (j, 0, i, 0))],
            out_specs=pl.BlockSpec((None, rb, c), lambda j, i, core_ref: (j, i, 0))),
        compiler_params=pltpu.CompilerParams(dimension_semantics=("parallel", "parallel"), vmem_limit_bytes=VMEM_LIMIT),
    )(core, g, land)


def _add_chips(part, land, where, layer=0, n_layers=1, into=None, col_half=False, rows=(0, 1)):
    _, r, c = part.shape
    sub, n_sub = rows
    rb = _row_block(r, c * (4 * 2 * 2 + 4 * 2 + 2 * 4))

    def body(where_ref, p_ref, l_ref, *rest):
        acc = p_ref[...].astype(F32)
        for m in range(N_CHIPS - 1):
            acc = acc + l_ref[m].astype(F32)
        rest[-1][...] = acc

    in_specs = [pl.BlockSpec((None, rb, c), lambda i, where_ref: (where_ref[0], i, 0)),
                pl.BlockSpec((N_CHIPS - 1, rb, c), lambda i, where_ref: (0, i, 0))]
    args = [where, part, land]
    if into is not None:
        in_specs.append(ANY)
        args.append(into)
    if col_half:
        out_shape = jax.ShapeDtypeStruct((n_layers, r, 2 * c), F32)
        out_spec = pl.BlockSpec((None, rb, c), lambda i, where_ref: (layer, i, where_ref[1]))
    else:
        out_shape = jax.ShapeDtypeStruct((n_layers, 2, n_sub * r, c), F32)
        out_spec = pl.BlockSpec((None, None, rb, c), lambda i, where_ref: (layer, where_ref[1], sub * (r // rb) + i, 0))
    return pl.pallas_call(
        body, name="rs_add_chips", out_shape=out_shape,
        grid_spec=pltpu.PrefetchScalarGridSpec(
            num_scalar_prefetch=1, grid=(r // rb,), in_specs=in_specs, out_specs=out_spec),
        input_output_aliases={} if into is None else {3: 0},
        compiler_params=pltpu.CompilerParams(dimension_semantics=("parallel",), vmem_limit_bytes=VMEM_LIMIT),
    )(*args)


def _sum_small(smg):
    def body(s_ref, o_ref):
        acc = s_ref[0]
        for j in range(1, N_DEV):
            acc = acc + s_ref[j]
        o_ref[...] = acc

    return pl.pallas_call(body, name="rs_sum_small", out_shape=jax.ShapeDtypeStruct(smg.shape[1:], F32))(smg)


def _pool_windows(ext_ref, g, gw, tm, first_row):
    w = POOL_WINDOWS[g]
    slab = ext_ref[:, g * gw:(g + 1) * gw]
    p, k = slab, 1
    while k < w:
        p = p + pltpu.roll(p, k, 0)
        k *= 2
    t = first_row + lax.broadcasted_iota(jnp.int32, (tm, 1), 0)
    cnt = jnp.minimum(t + 1, w).astype(F32)
    return p[POOL_HALO:] / cnt - slab[POOL_HALO:]


def _fwd_pool(x, small, scale, poolw, comm=None):
    t, d = x.shape
    tm = _token_tile(t)
    gw = d // len(POOL_WINDOWS)

    def body(x_ref, sm_ref, sc_ref, w_ref, h_ref, ext_ref, mix_ref):
        i = pl.program_id(0)

        @pl.when(i == 0)
        def _():
            ext_ref[0:POOL_HALO, :] = jnp.zeros((POOL_HALO, d), F32)

        @pl.when(i > 0)
        def _():
            ext_ref[0:POOL_HALO, :] = ext_ref[tm:tm + POOL_HALO, :]

        xv = x_ref[...]
        xh, _ = _rms(xv)
        ext_ref[POOL_HALO:, :] = xh * sm_ref[0:1, :]
        for g in range(len(POOL_WINDOWS)):
            pooled = _pool_windows(ext_ref, g, gw, tm, i * tm)
            cols = slice(g * gw, (g + 1) * gw)
            mix_ref[:, cols] = _dot(pooled.astype(BF16), w_ref[g]) * sc_ref[:, cols]
        mh, _ = _rms(mix_ref[...])
        h_ref[...] = xv + mh * sm_ref[1:2, :]

    (h,), got = _hosted(
        body, comm, name="fwd_pool", grid=(t // tm,), out_shape=[jax.ShapeDtypeStruct((t, d), F32)],
        in_specs=[pl.BlockSpec((tm, d), lambda i: (i, 0)), _resident(small.shape, lambda i: (0, 0)),
                  _resident(scale.shape, lambda i: (0, 0)), _resident(poolw.shape, lambda i: (0, 0, 0))],
        out_specs=[pl.BlockSpec((tm, d), lambda i: (i, 0))],
        scratch_shapes=[pltpu.VMEM((POOL_HALO + tm, d), F32), pltpu.VMEM((tm, d), F32)],
        args=[x, small, scale, poolw])
    return h, got


def _fwd_ffn(h, small, wgu, wd, layer, comm=None, target=None):
    t, d = h.shape
    tm = _token_tile(t)
    steps = t // tm
    fc = wgu.shape[-1]
    f = 2 * fc
    g_in, g_out = 4 * layer + 2, 4 * layer + 3
    with_loss = target is not None

    def body(h_ref, *refs):
        if with_loss:
            t_ref, sm_ref, wgu_ref, wd_ref, o_ref, gu_ref, ff_ref, n_ref, l_ref, acc_ref = refs
        else:
            sm_ref, wgu_ref, wd_ref, o_ref, gu_ref, ff_ref, n_ref = refs
        hv = h_ref[...]
        hh, _ = _rms(hv)
        n = (hh * sm_ref[g_in:g_in + 1, :]).astype(BF16)
        n_ref[...] = n
        ff = None
        for j in range(2):
            gate = _dot(n, wgu_ref[j])
            up = _dot(n, wgu_ref[2 + j])
            gu_ref[:, j * fc:(j + 1) * fc] = gate.astype(BF16)
            gu_ref[:, f + j * fc:f + (j + 1) * fc] = up.astype(BF16)
            act = (gate * jax.nn.sigmoid(gate) * up).astype(BF16)
            part = _dot(act, wd_ref[j * fc:(j + 1) * fc, :])
            ff = part if ff is None else ff + part
        ff_ref[...] = ff
        fh, _ = _rms(ff)
        out = hv + fh * sm_ref[g_out:g_out + 1, :]
        if not with_loss:
            o_ref[...] = out
            return
        i = pl.program_id(0)
        e = out - t_ref[...]
        o_ref[...] = e * (1.0 / d)

        @pl.when(i == 0)
        def _():
            acc_ref[...] = jnp.zeros_like(acc_ref)

        acc_ref[...] += _colsum(e * e)

        @pl.when(i == steps - 1)
        def _():
            l_ref[...] = jnp.full(l_ref.shape, 0.5 / d, F32) * jnp.sum(acc_ref[...])

    row = lambda i: (i, 0)
    out_shape = [jax.ShapeDtypeStruct((t, d), F32), jax.ShapeDtypeStruct((t, 2 * f), BF16),
                 jax.ShapeDtypeStruct((t, d), F32), jax.ShapeDtypeStruct((t, d), BF16)]
    out_specs = [pl.BlockSpec((tm, d), row), pl.BlockSpec((tm, 2 * f), row), pl.BlockSpec((tm, d), row),
                 pl.BlockSpec((tm, d), row)]
    weight_specs = [_resident(small.shape, lambda i: (0, 0)), _resident(wgu.shape, lambda i: (0, 0, 0)),
                    _resident(wd.shape, lambda i: (0, 0))]
    if with_loss:
        return _hosted(
            body, comm, name=f"fwd_ffn{layer}_loss", grid=(steps,),
            out_shape=out_shape + [jax.ShapeDtypeStruct((8, 128), F32)],
            in_specs=[pl.BlockSpec((tm, d), row), pl.BlockSpec((tm, d), row)] + weight_specs,
            out_specs=out_specs + [pl.BlockSpec((8, 128), lambda i: (0, 0))],
            scratch_shapes=[pltpu.VMEM((1, d), F32)], args=[h, target, small, wgu, wd])
    return _hosted(
        body, comm, name=f"fwd_ffn{layer}", grid=(steps,), out_shape=out_shape,
        in_specs=[pl.BlockSpec((tm, d), row)] + weight_specs, out_specs=out_specs, args=[h, small, wgu, wd])


def _fwd_conv(h, small, win, wout, comm=None):
    t, d = h.shape
    tm = _token_tile(t)
    pc = win.shape[-1]

    def body(h_ref, sm_ref, win_ref, wout_ref, o_ref, proj_ref, y_ref, n_ref, pj_ref, uext_ref):
        i = pl.program_id(0)

        @pl.when(i == 0)
        def _():
            uext_ref[0:CONV_HALO, :] = jnp.zeros((CONV_HALO, d), F32)

        @pl.when(i > 0)
        def _():
            uext_ref[0:CONV_HALO, :] = uext_ref[tm:tm + CONV_HALO, :]

        hv = h_ref[...]
        hh, _ = _rms(hv)
        n = (hh * sm_ref[4:5, :]).astype(BF16)
        n_ref[...] = n
        for k in range(N_CHIPS):
            pj_ref[:, k * pc:(k + 1) * pc] = _dot(n, win_ref[k])
        proj_ref[...] = pj_ref[...].astype(BF16)
        uext_ref[CONV_HALO:, :] = pj_ref[:, d:2 * d] * pj_ref[:, 2 * d:]
        taps = [sm_ref[8 + j:9 + j, :] for j in range(3)]
        full = uext_ref[...]
        conv = (full[CONV_HALO:] * taps[2] + pltpu.roll(full, 1, 0)[CONV_HALO:] * taps[1]
                + pltpu.roll(full, 2, 0)[CONV_HALO:] * taps[0])
        y = _dot((pj_ref[:, 0:d] * conv).astype(BF16), wout_ref[...])
        y_ref[...] = y
        yh, _ = _rms(y)
        o_ref[...] = hv + yh * sm_ref[5:6, :]

    row = lambda i: (i, 0)
    return _hosted(
        body, comm, name="fwd_conv", grid=(t // tm,),
        out_shape=[jax.ShapeDtypeStruct((t, d), F32), jax.ShapeDtypeStruct((t, 3 * d), BF16),
                   jax.ShapeDtypeStruct((t, d), F32), jax.ShapeDtypeStruct((t, d), BF16)],
        in_specs=[pl.BlockSpec((tm, d), row), _resident(small.shape, lambda i: (0, 0)),
                  _resident(win.shape, lambda i: (0, 0, 0)), _resident(wout.shape, lambda i: (0, 0))],
        out_specs=[pl.BlockSpec((tm, d), row), pl.BlockSpec((tm, 3 * d), row), pl.BlockSpec((tm, d), row),
                   pl.BlockSpec((tm, d), row)],
        scratch_shapes=[pltpu.VMEM((tm, 3 * d), F32), pltpu.VMEM((CONV_HALO + tm, d), F32)],
        args=[h, small, win, wout])


def _bwd_ffn(dh, h, ff, gu, small, wgu, wd, layer, comm=None):
    t, d = h.shape
    tm = _token_tile(t, 256)
    fc = wgu.shape[-1]
    f = 2 * fc
    g_in, g_out = 4 * layer + 2, 4 * layer + 3

    def body(dh_ref, h_ref, ff_ref, gu_ref, sm_ref, wgu_ref, wd_ref, o_ref, dgu_ref, dff_ref, act_ref, sg_ref):
        i = pl.program_id(0)

        @pl.when(i == 0)
        def _():
            sg_ref[...] = jnp.zeros_like(sg_ref)

        dy = dh_ref[...]
        fh, r3 = _rms(ff_ref[...])
        sg_ref[1:2, :] += _colsum(dy * fh)
        dff = _rms_bwd(dy, fh, r3, sm_ref[g_out:g_out + 1, :]).astype(BF16)
        dff_ref[...] = dff
        for j in range(2):
            dact = _dot_nt(dff, wd_ref[j * fc:(j + 1) * fc, :])
            gate = gu_ref[:, j * fc:(j + 1) * fc].astype(F32)
            up = gu_ref[:, f + j * fc:f + (j + 1) * fc].astype(F32)
            sig = jax.nn.sigmoid(gate)
            silu = gate * sig
            act_ref[:, j * fc:(j + 1) * fc] = (silu * up).astype(BF16)
            dgu_ref[:, j * fc:(j + 1) * fc] = (dact * up * (sig * (1.0 + gate * (1.0 - sig)))).astype(BF16)
            dgu_ref[:, f + j * fc:f + (j + 1) * fc] = (dact * silu).astype(BF16)
        dn = None
        for k in range(N_CHIPS):
            part = _dot_nt(dgu_ref[:, k * fc:(k + 1) * fc], wgu_ref[k])
            dn = part if dn is None else dn + part
        hh, r2 = _rms(h_ref[...])
        sg_ref[0:1, :] += _colsum(dn * hh)
        o_ref[...] = dy + _rms_bwd(dn, hh, r2, sm_ref[g_in:g_in + 1, :])

    row = lambda i: (i, 0)
    return _hosted(
        body, comm, name=f"bwd_ffn{layer}", grid=(t // tm,),
        out_shape=[jax.ShapeDtypeStruct((t, d), F32), jax.ShapeDtypeStruct((t, 2 * f), BF16),
                   jax.ShapeDtypeStruct((t, d), BF16), jax.ShapeDtypeStruct((t, f), BF16),
                   jax.ShapeDtypeStruct((8, d), F32)],
        in_specs=[pl.BlockSpec((tm, d), row), pl.BlockSpec((tm, d), row), pl.BlockSpec((tm, d), row),
                  pl.BlockSpec((tm, 2 * f), row), _resident(small.shape, lambda i: (0, 0)),
                  _resident(wgu.shape, lambda i: (0, 0, 0)), _resident(wd.shape, lambda i: (0, 0))],
        out_specs=[pl.BlockSpec((tm, d), row), pl.BlockSpec((tm, 2 * f), row), pl.BlockSpec((tm, d), row),
                   pl.BlockSpec((tm, f), row), pl.BlockSpec((8, d), lambda i: (0, 0))],
        args=[dh, h, ff, gu, small, wgu, wd])


def _bwd_conv(dh, h, y, proj, small, win, wout, comm=None):
    t, d = h.shape
    tm = _token_tile(t)
    steps = t // tm
    pc = win.shape[-1]
    halo_blocks = tm // 16

    def body(dh_ref, h_ref, y_ref, proj_ref, halo_ref, sm_ref, win_ref, wout_ref,
             o_ref, dproj_ref, dy_ref, bc_ref, sg_ref, uext_ref, dcext_ref, carry_ref):
        i = pl.program_id(0)
        tile = steps - 1 - i

        @pl.when(i == 0)
        def _():
            sg_ref[...] = jnp.zeros_like(sg_ref)
            carry_ref[...] = jnp.zeros_like(carry_ref)

        dy = dh_ref[...]
        yh, r1 = _rms(y_ref[...])
        sg_ref[1:2, :] += _colsum(dy * yh)
        dyv = _rms_bwd(dy, yh, r1, sm_ref[5:6, :]).astype(BF16)
        dy_ref[...] = dyv
        dbc = _dot_nt(dyv, wout_ref[...])
        b = proj_ref[:, 0:d].astype(F32)
        cg = proj_ref[:, d:2 * d].astype(F32)
        v = proj_ref[:, 2 * d:].astype(F32)
        halo = halo_ref[...].astype(F32)[16 - CONV_HALO:]
        uh = halo[:, d:2 * d] * halo[:, 2 * d:]
        uext_ref[0:CONV_HALO, :] = jnp.where(tile > 0, uh, jnp.zeros_like(uh))
        uext_ref[CONV_HALO:, :] = cg * v
        taps = [sm_ref[8 + j:9 + j, :] for j in range(3)]
        full = uext_ref[...]
        u0 = full[CONV_HALO:]
        u1 = pltpu.roll(full, 1, 0)[CONV_HALO:]
        u2 = pltpu.roll(full, 2, 0)[CONV_HALO:]
        conv = u0 * taps[2] + u1 * taps[1] + u2 * taps[0]
        bc_ref[...] = (b * conv).astype(BF16)
        dconv = dbc * b
        sg_ref[4:5, :] += _colsum(dconv * u0)
        sg_ref[3:4, :] += _colsum(dconv * u1)
        sg_ref[2:3, :] += _colsum(dconv * u2)
        dcext_ref[0:tm, :] = dconv
        dcext_ref[tm:, :] = carry_ref[...]
        carry_ref[...] = dconv[0:CONV_HALO]
        dfull = dcext_ref[...]
        n8 = tm + CONV_HALO
        du = (dfull[0:tm] * taps[2] + pltpu.roll(dfull, n8 - 1, 0)[0:tm] * taps[1]
              + pltpu.roll(dfull, n8 - 2, 0)[0:tm] * taps[0])
        dproj_ref[:, 0:d] = (dbc * conv).astype(BF16)
        dproj_ref[:, d:2 * d] = (du * v).astype(BF16)
        dproj_ref[:, 2 * d:] = (du * cg).astype(BF16)
        dn = None
        for k in range(N_CHIPS):
            part = _dot_nt(dproj_ref[:, k * pc:(k + 1) * pc], win_ref[k])
            dn = part if dn is None else dn + part
        hh, r0 = _rms(h_ref[...])
        sg_ref[0:1, :] += _colsum(dn * hh)
        o_ref[...] = dy + _rms_bwd(dn, hh, r0, sm_ref[4:5, :])

    rev = lambda i: (steps - 1 - i, 0)
    before = lambda i: (jnp.maximum((steps - 1 - i) * halo_blocks - 1, 0), 0)
    return _hosted(
        body, comm, name="bwd_conv", grid=(steps,),
        out_shape=[jax.ShapeDtypeStruct((t, d), F32), jax.ShapeDtypeStruct((t, 3 * d), BF16),
                   jax.ShapeDtypeStruct((t, d), BF16), jax.ShapeDtypeStruct((t, d), BF16),
                   jax.ShapeDtypeStruct((8, d), F32)],
        in_specs=[pl.BlockSpec((tm, d), rev), pl.BlockSpec((tm, d), rev), pl.BlockSpec((tm, d), rev),
                  pl.BlockSpec((tm, 3 * d), rev), pl.BlockSpec((16, 3 * d), before),
                  _resident(small.shape, lambda i: (0, 0)), _resident(win.shape, lambda i: (0, 0, 0)),
                  _resident(wout.shape, lambda i: (0, 0))],
        out_specs=[pl.BlockSpec((tm, d), rev), pl.BlockSpec((tm, 3 * d), rev), pl.BlockSpec((tm, d), rev),
                   pl.BlockSpec((tm, d), rev), pl.BlockSpec((8, d), lambda i: (0, 0))],
        scratch_shapes=[pltpu.VMEM((CONV_HALO + tm, d), F32), pltpu.VMEM((tm + CONV_HALO, d), F32),
                        pltpu.VMEM((CONV_HALO, d), F32)],
        args=[dh, h, y, proj, proj, small, win, wout])


def _bwd_pool(dh, x, small, scale, poolw, comm=None):
    t, d = x.shape
    tm = _token_tile(t)
    steps = t // tm
    ng = len(POOL_WINDOWS)
    gw = d // ng
    halo_blocks = tm // POOL_HALO

    def body(dh_ref, x_ref, halo_ref, sm_ref, sc_ref, w_ref, o_ref, dw_ref, sg_ref,
             ext_ref, mix_ref, mm_ref, pb_ref, qext_ref, dhn_ref, carry_ref):
        i = pl.program_id(0)
        tile = steps - 1 - i

        @pl.when(i == 0)
        def _():
            sg_ref[...] = jnp.zeros_like(sg_ref)
            dw_ref[...] = jnp.zeros_like(dw_ref)
            carry_ref[...] = jnp.zeros_like(carry_ref)

        g0 = sm_ref[0:1, :]
        xv = x_ref[...]
        xh, r0 = _rms(xv)
        hx, _ = _rms(halo_ref[...])
        ext_ref[0:POOL_HALO, :] = jnp.where(tile > 0, hx * g0, jnp.zeros_like(hx))
        ext_ref[POOL_HALO:, :] = xh * g0
        for g in range(ng):
            pooled = _pool_windows(ext_ref, g, gw, tm, tile * tm)
            cols = slice(g * gw, (g + 1) * gw)
            pb = pooled.astype(BF16)
            pb_ref[:, cols] = pb
            mm = _dot(pb, w_ref[g])
            mm_ref[:, cols] = mm
            mix_ref[:, cols] = mm * sc_ref[:, cols]
        dy = dh_ref[...]
        mh, r1 = _rms(mix_ref[...])
        sg_ref[1:2, :] += _colsum(dy * mh)
        dmix = _rms_bwd(dy, mh, r1, sm_ref[1:2, :])
        sg_ref[2:3, :] += _colsum(dmix * mm_ref[...])
        mix_ref[...] = dmix * sc_ref[...]
        n16 = tm + POOL_HALO
        for g in range(ng):
            w = POOL_WINDOWS[g]
            cols = slice(g * gw, (g + 1) * gw)
            dmm = mix_ref[:, cols].astype(BF16)
            dpooled = _dot_nt(dmm, w_ref[g])
            dw_ref[g] += _dot_tn(pb_ref[:, cols], dmm)
            trow = tile * tm + lax.broadcasted_iota(jnp.int32, (tm, 1), 0)
            q = dpooled / jnp.minimum(trow + 1, w).astype(F32)
            qext_ref[0:tm, cols] = q
            qext_ref[tm:, cols] = carry_ref[:, cols]
            carry_ref[:, cols] = q[0:POOL_HALO]
            p, k = qext_ref[:, cols], 1
            while k < w:
                p = p + pltpu.roll(p, n16 - k, 0)
                k *= 2
            dhn_ref[:, cols] = p[0:tm] - dpooled
        dhn = dhn_ref[...]
        sg_ref[0:1, :] += _colsum(dhn * xh)
        o_ref[...] = dy + _rms_bwd(dhn, xh, r0, g0)

    rev = lambda i: (steps - 1 - i, 0)
    before = lambda i: (jnp.maximum((steps - 1 - i) * halo_blocks - 1, 0), 0)
    return _hosted(
        body, comm, name="bwd_pool", grid=(steps,),
        out_shape=[jax.ShapeDtypeStruct((t, d), F32), jax.ShapeDtypeStruct((ng, gw, gw), F32),
                   jax.ShapeDtypeStruct((8, d), F32)],
        in_specs=[pl.BlockSpec((tm, d), rev), pl.BlockSpec((tm, d), rev), pl.BlockSpec((POOL_HALO, d), before),
                  _resident(small.shape, lambda i: (0, 0)), _resident(scale.shape, lambda i: (0, 0)),
                  _resident(poolw.shape, lambda i: (0, 0, 0))],
        out_specs=[pl.BlockSpec((tm, d), rev), pl.BlockSpec((ng, gw, gw), lambda i: (0, 0, 0)),
                   pl.BlockSpec((8, d), lambda i: (0, 0))],
        scratch_shapes=[pltpu.VMEM((POOL_HALO + tm, d), F32), pltpu.VMEM((tm, d), F32), pltpu.VMEM((tm, d), F32),
                        pltpu.VMEM((tm, d), BF16), pltpu.VMEM((tm + POOL_HALO, d), F32), pltpu.VMEM((tm, d), F32),
                        pltpu.VMEM((POOL_HALO, d), F32)],
        args=[dh, x, x, small, scale, poolw])


def _weight_grad(a, b, bm, bn, half_on, name, comm=None, rows=(0, 1)):
    t, m = a.shape
    _, n = b.shape
    if half_on == "a":
        a_cols, b_cols = 2 * bm, bn
    else:
        a_cols, b_cols = bm, 2 * bn
    steps = max(m // a_cols, n // b_cols)
    sub, n_sub = rows
    tr = bm // n_sub

    def spec(cols, total):
        if cols == total:
            return _resident((t, cols), lambda p, j: (0, 0))
        return pl.BlockSpec((t, cols), lambda p, j: (0, j))

    def tile(a_ref, b_ref, half):
        if half_on == "a":
            first = half * bm + sub * tr
            return _dot_tn(a_ref[:, first:first + tr], b_ref[...])
        return _dot_tn(a_ref[...], b_ref[:, half * bn:(half + 1) * bn])

    def body(a_ref, b_ref, parts_ref, land_ref, acc_ref, stage_ref, got_ref, send_sems, recv_sems, got_sem):
        p, j = pl.program_id(0), pl.program_id(1)
        x, y, c, _ = _place()
        half = jnp.where(p == 0, 1 - c, c)

        def send(jj):
            return _remote(stage_ref.at[jj % 2], land_ref.at[jj], send_sems.at[jj], recv_sems.at[jj], (x, y, 1 - c))

        def fetch():
            return pltpu.make_async_copy(land_ref.at[j], got_ref, got_sem)

        @pl.when(p == 1)
        def _():
            @pl.when(j == 0)
            def _():
                for jj in range(max(steps - 2, 0), steps):
                    send(jj).wait_send()

            send(j).wait_recv()
            fetch().start()

        for hv in range(2):
            @pl.when(half == hv)
            def _():
                acc_ref[...] = tile(a_ref, b_ref, hv)

        @pl.when(p == 0)
        def _():
            @pl.when(j >= 2)
            def _():
                send(j - 2).wait_send()

            stage_ref[j % 2] = acc_ref[...].astype(BF16)
            send(j).start()

        @pl.when(p == 1)
        def _():
            fetch().wait()
            parts_ref[...] = (acc_ref[...] + got_ref[...].astype(F32)).astype(BF16)

    (parts, _), got = _hosted(
        body, comm, name=name, grid=(2, steps),
        out_shape=[jax.ShapeDtypeStruct((steps, tr, bn), BF16), jax.ShapeDtypeStruct((steps, tr, bn), BF16)],
        in_specs=[spec(a_cols, m), spec(b_cols, n)],
        out_specs=[pl.BlockSpec((None, tr, bn), lambda p, j: (p * j, 0, 0)), ANY],
        scratch_shapes=[pltpu.VMEM((tr, bn), F32), pltpu.VMEM((2, tr, bn), BF16), pltpu.VMEM((tr, bn), BF16),
                        DMA((steps,)), DMA((steps,)), DMA],
        args=[a, b])
    return parts, got


def _adamw(w, g, m, v, name):
    r, c = w.shape
    rb = _row_block(r, c * (8 * 4 * 2 + 4 * 4))
    bc1 = 1.0 - ADAM_B1 ** ADAM_STEP
    bc2 = 1.0 - ADAM_B2 ** ADAM_STEP

    def body(w_ref, g_ref, m_ref, v_ref, d_ref, nm_ref, nv_ref, go_ref):
        gv = g_ref[...]
        go_ref[...] = gv
        nm = ADAM_B1 * m_ref[...] + (1.0 - ADAM_B1) * gv
        nv = ADAM_B2 * v_ref[...] + (1.0 - ADAM_B2) * (gv * gv)
        nm_ref[...] = nm
        nv_ref[...] = nv
        d_ref[...] = -ADAM_LR * ((nm / bc1) / (jnp.sqrt(nv / bc2) + ADAM_EPS) + ADAM_WD * w_ref[...])

    spec = pl.BlockSpec((rb, c), lambda i: (i, 0))
    return pl.pallas_call(
        body, name=name, grid=(r // rb,), out_shape=[jax.ShapeDtypeStruct((r, c), F32)] * 4,
        in_specs=[spec] * 4, out_specs=[spec] * 4,
        compiler_params=pltpu.CompilerParams(dimension_semantics=("parallel",), vmem_limit_bytes=VMEM_LIMIT),
    )(w, g, m, v)


def kernel(x, norm_gains, pool_w, pool_scale, conv_in_w, conv_w, conv_out_w, ffn_gate_up_w, ffn_down_w, loss_target, m_norm_gains, m_pool_w, m_pool_scale, m_conv_in_w, m_conv_w, m_conv_out_w, m_ffn_gate_up_w, m_ffn_down_w, v_norm_gains, v_pool_w, v_pool_scale, v_conv_in_w, v_conv_w, v_conv_out_w, v_ffn_gate_up_w, v_ffn_down_w):
    _, t, d = x.shape
    dq = d // N_CHIPS
    ng = len(POOL_WINDOWS)
    gw = d // ng
    fq = ffn_down_w.shape[1]
    f = N_CHIPS * fq
    fc = f // 2
    core = lax.axis_index("c")
    chip = 2 * lax.axis_index("x") + lax.axis_index("y")
    core_arr = jnp.reshape(core, (1,)).astype(jnp.int32)
    where_arr = jnp.stack([chip, core]).astype(jnp.int32)
    x2, target = x[0], loss_target[0]

    small_loc = jnp.concatenate(
        [norm_gains.reshape(8, dq), conv_w[0], jnp.zeros((5, dq), F32)], axis=0).reshape(1, 2, 8, dq)
    pool_loc = pool_w.astype(BF16).reshape(1, 2, ng // 2 * (gw // N_CHIPS), gw)
    wgu_loc = ffn_gate_up_w.astype(BF16).reshape(2, 2, d // 2, fc)
    wd_loc = ffn_down_w.astype(BF16).reshape(2, 2, fq // 2, d)
    win_loc = conv_in_w.astype(BF16).reshape(1, 2, d // 2, -1)
    wout_loc = conv_out_w.astype(BF16).reshape(1, 2, dq // 2, d)

    def ffn_weights(wgu_f, wd_f):
        return wgu_f.reshape(N_CHIPS, d, fc), wd_f.reshape(f, d)

    ag = _SplitGather([(pool_loc, 0), (small_loc, 0), (wgu_loc, 0), (wd_loc, 0), (win_loc, 0), (wout_loc, 0),
                       (wgu_loc, 1), (wd_loc, 1)])
    ag.start()
    pool_f, small_f = _pass_on(ag.wait([0, 1], ag.token, "ag_wait_first"), "ag_pass_first")
    poolw = pool_f.reshape(N_CHIPS, ng, gw // N_CHIPS, gw).transpose(1, 0, 2, 3).reshape(ng, gw, gw)
    small = small_f.transpose(1, 2, 0, 3).reshape(16, d)
    h1, _ = _fwd_pool(x2, small, pool_scale, poolw)
    wgu0, wd0 = ffn_weights(*_pass_on(ag.wait([2, 3], h1, "ag_wait_ffn0"), "ag_pass_ffn0"))
    (h2, gu0, ff0, n0), _ = _fwd_ffn(h1, small, wgu0, wd0, 0)
    win_f, wout_f = _pass_on(ag.wait([4, 5], h2, "ag_wait_conv"), "ag_pass_conv")
    win_f, wout_f = win_f.reshape(N_CHIPS, d, -1), wout_f.reshape(d, d)
    (h3, proj, y, nc), _ = _fwd_conv(h2, small, win_f, wout_f)
    wgu1, wd1 = ffn_weights(*_pass_on(ag.wait([6, 7], h3, "ag_wait_ffn1"), "ag_pass_ffn1"))
    (dh4, gu1, ff1, n1, loss_blk), _ = _fwd_ffn(h3, small, wgu1, wd1, 1, target=target)

    (dh3, dgu1, dff1, act1, sg_f1), _ = _bwd_ffn(dh4, h3, ff1, gu1, small, wgu1, wd1, 1)
    parts_d1, _ = _weight_grad(act1, dff1, fc, d // 2, "b", "dw_down1")
    parts_d1 = parts_d1.reshape(N_CHIPS, fq, d // 2)
    parts_gu1, recv_d1 = _weight_grad(n1, dgu1, d // 2, fc, "a", "dw_gate_up1", _ChipExchange([parts_d1]))
    (dh2, dproj, dyv, bcv, sg_c), recv_gu1 = _bwd_conv(dh3, h2, y, proj, small, win_f, wout_f, _ChipExchange([parts_gu1]))
    parts_in, _ = _weight_grad(nc, dproj, d // 2, 3 * d // N_CHIPS, "a", "dw_conv_in")
    (dh1, dgu0, dff0, act0, sg_f0), recv_in = _bwd_ffn(dh2, h1, ff0, gu0, small, wgu0, wd0, 0, _ChipExchange([parts_in]))
    (grad_x, dpool, sg_p), _ = _bwd_pool(dh1, x2, small, pool_scale, poolw)
    parts_d0, _ = _weight_grad(act0, dff0, fc, d // 2, "b", "dw_down0")
    parts_d0 = parts_d0.reshape(N_CHIPS, fq, d // 2)
    parts_gu0a, recv_d0 = _weight_grad(n0, dgu0, d // 2, fc, "a", "dw_gate_up0a", _ChipExchange([parts_d0]), (0, 2))
    parts_gu0b, recv_gu0a = _weight_grad(n0, dgu0, d // 2, fc, "a", "dw_gate_up0b", _ChipExchange([parts_gu0a]), (1, 2))
    parts_out, recv_gu0b = _weight_grad(bcv, dyv, dq // 2, d, "a", "dw_conv_out", _ChipExchange([parts_gu0b]))
    g_pool = dpool.astype(BF16).reshape(2, ng // 2, N_CHIPS, gw // N_CHIPS, gw).transpose(2, 0, 1, 3, 4).reshape(
        N_CHIPS, 2, ng // 2 * (gw // N_CHIPS), gw)
    small_g = jnp.concatenate(
        [sg_p[0:2], sg_f0[0:2], sg_c[0:2], sg_f1[0:2], sg_c[2:5], sg_p[2:3],
         jnp.broadcast_to(loss_blk[0:1, 0:1], (1, d)), jnp.zeros((3, d), F32)], axis=0)
    land_p, small_all = _sibling_exchange([g_pool], small_g, "rs_sibling_pool")
    parts_p = _add_sibling(g_pool, land_p, core_arr)
    recv_out, recv_p = _alone(_ChipExchange([parts_out, parts_p]), "rs_chips_last")

    gs_gu = _add_chips(parts_gu1, recv_gu1[0], where_arr, 1, 2)
    gs_gu = _add_chips(parts_gu0a, recv_gu0a[0], where_arr, 0, 2, gs_gu, rows=(0, 2))
    gs_gu = _add_chips(parts_gu0b, recv_gu0b[0], where_arr, 0, 2, gs_gu, rows=(1, 2))
    gs_d = _add_chips(parts_d1, recv_d1[0], where_arr, 1, 2, col_half=True)
    gs_d = _add_chips(parts_d0, recv_d0[0], where_arr, 0, 2, gs_d, col_half=True)
    gs_in = _add_chips(parts_in, recv_in[0], where_arr)
    gs_out = _add_chips(parts_out, recv_out, where_arr)
    gs_pool = _add_chips(parts_p, recv_p, where_arr)
    full = _sibling_share([gs_gu, gs_d, gs_in, gs_out, gs_pool], [False, True, False, False, False])
    small_sum = _sum_small(small_all)
    loss = small_sum[12, 0]

    gg_gu = full[0].reshape(2, d, fc)
    gg_d = full[1].reshape(2, fq, d)
    gg_in = full[2].reshape(1, d, -1)
    gg_out = full[3].reshape(1, dq, d)
    gg_pool = full[4].reshape(1, ng, gw // N_CHIPS, gw)
    mine = lax.dynamic_slice_in_dim(small_sum, chip * dq, dq, axis=1)
    gg_gains = mine[0:8].reshape(2, 4, dq)
    gg_taps = mine[8:11].reshape(1, 3, dq)
    gg_scale = small_sum[11:12]

    grads = [gg_gains, gg_pool, gg_scale, gg_in, gg_taps, gg_out, gg_gu, gg_d]
    weights = [norm_gains, pool_w, pool_scale, conv_in_w, conv_w, conv_out_w, ffn_gate_up_w, ffn_down_w]
    ms = [m_norm_gains, m_pool_w, m_pool_scale, m_conv_in_w, m_conv_w, m_conv_out_w, m_ffn_gate_up_w, m_ffn_down_w]
    vs = [v_norm_gains, v_pool_w, v_pool_scale, v_conv_in_w, v_conv_w, v_conv_out_w, v_ffn_gate_up_w, v_ffn_down_w]
    names = ["gains", "pool_w", "pool_scale", "conv_in", "taps", "conv_out", "gate_up", "down"]
    deltas, new_ms, new_vs, grads_out = [], [], [], []
    for w, g, m, v, nm in zip(weights, grads, ms, vs, names):
        flat = (-1, w.shape[-1])
        dl, m2, v2, g2 = _adamw(w.reshape(flat), g.reshape(flat), m.reshape(flat), v.reshape(flat), "adamw_" + nm)
        deltas.append(dl.reshape(w.shape))
        new_ms.append(m2.reshape(w.shape))
        new_vs.append(v2.reshape(w.shape))
        grads_out.append(g2.reshape(w.shape))
    return (loss, grad_x[None], *grads_out, *deltas, *new_ms, *new_vs)
```

```python
import jax
import jax.numpy as jnp
from jax import lax
from jax.experimental import pallas as pl
from jax.experimental.pallas import tpu as pltpu

RMS_EPS = 1e-6
POOL_WINDOWS = (2, 4, 8, 16)
POOL_HALO = 16
CONV_HALO = 8
N_CHIPS = 4
N_DEV = 8
ADAM_LR = 0.001
ADAM_B1 = 0.9
ADAM_B2 = 0.999
ADAM_EPS = 1e-08
ADAM_WD = 0.01
ADAM_STEP = 10
VMEM_LIMIT = 56 * 2**20
STREAM_BUDGET = 24 * 2**20
MESH = pl.DeviceIdType.MESH
ANY = pl.BlockSpec(memory_space=pl.ANY)
DMA = pltpu.SemaphoreType.DMA
BF16 = jnp.bfloat16
F32 = jnp.float32


def _token_tile(t, rows=512):
    return min(rows, t)


def _rms(x):
    r = lax.rsqrt(jnp.mean(x * x, axis=-1, keepdims=True) + RMS_EPS)
    return x * r, r


def _rms_bwd(dy, xh, r, g):
    a = dy * g
    return r * (a - xh * jnp.mean(a * xh, axis=-1, keepdims=True))


def _dot(a, b):
    return jnp.dot(a, b, preferred_element_type=F32)


def _dot_nt(a, b):
    return lax.dot_general(a, b, (((1,), (1,)), ((), ())), preferred_element_type=F32)


def _dot_tn(a, b):
    return lax.dot_general(a, b, (((0,), (0,)), ((), ())), preferred_element_type=F32)


def _colsum(a):
    return jnp.sum(a, axis=0, keepdims=True)


def _resident(block, index_map):
    return pl.BlockSpec(block, index_map, pipeline_mode=pl.Buffered(1))


def _row_block(r, row_bytes):
    best = None
    for rb in range(16, r + 1, 16):
        if r % rb == 0 and rb * row_bytes <= STREAM_BUDGET:
            best = rb
    return best if best is not None else r


def _place():
    x, y, c = lax.axis_index("x"), lax.axis_index("y"), lax.axis_index("c")
    return x, y, c, 2 * x + y


def _dev(chip, core):
    return (chip // 2, chip % 2, core)


def _remote(src, dst, send_sem, recv_sem, device):
    return pltpu.make_async_remote_copy(src_ref=src, dst_ref=dst, send_sem=send_sem, recv_sem=recv_sem,
                                        device_id=device, device_id_type=MESH)


class _Gather:
    def __init__(self, shards):
        n = len(shards)
        self.args = [s for s, _ in shards]
        self.layers = [l for _, l in shards]
        self.out_shape = [jax.ShapeDtypeStruct((N_CHIPS,) + s.shape[1:], s.dtype) for s in self.args]
        self.sems = [DMA((n,)), DMA((n,)), DMA((n, 3)), DMA((n, 3)), DMA((n, 3)), DMA((n, 3))]

    def _own(self, loc, out, sems, a):
        x, y, c, k = _place()
        return _remote(loc[a].at[self.layers[a]], out[a].at[k], sems[0].at[a], sems[1].at[a], (x, y, 1 - c))

    def _ici(self, loc, out, sems, a, m, arrival):
        x, y, c, k = _place()
        dst = out[a].at[k ^ m, c] if arrival else out[a].at[k, c]
        return _remote(loc[a].at[self.layers[a], c], dst, sems[2].at[a, m - 1], sems[3].at[a, m - 1], _dev(k ^ m, c))

    def _forward(self, out, sems, a, m, arrival):
        x, y, c, k = _place()
        got = out[a].at[k ^ m, 1 - c] if arrival else out[a].at[k ^ m, c]
        return _remote(got, got, sems[4].at[a, m - 1], sems[5].at[a, m - 1], (x, y, 1 - c))

    def start(self, loc, out, sems):
        for a in range(len(self.args)):
            for m in range(1, N_CHIPS):
                self._ici(loc, out, sems, a, m, False).start()
            self._own(loc, out, sems, a).start()

    def finish(self, loc, out, sems):
        n = len(self.args)
        for a in range(n):
            for m in range(1, N_CHIPS):
                self._ici(loc, out, sems, a, m, True).wait_recv()
                self._forward(out, sems, a, m, False).start()
        for a in range(n):
            for m in range(1, N_CHIPS):
                self._forward(out, sems, a, m, True).wait_recv()
            self._own(loc, out, sems, a).wait_recv()
        for a in range(n):
            for m in range(1, N_CHIPS):
                self._ici(loc, out, sems, a, m, False).wait_send()
                self._forward(out, sems, a, m, False).wait_send()
            self._own(loc, out, sems, a).wait_send()


class _ChipExchange:
    def __init__(self, parts):
        n = len(parts)
        self.args = list(parts)
        self.out_shape = [jax.ShapeDtypeStruct((N_CHIPS - 1,) + p.shape[1:], p.dtype) for p in parts]
        self.sems = [DMA((n, 3)), DMA((n, 3))]

    def _copy(self, p, land, sems, a, m):
        x, y, c, k = _place()
        return _remote(p[a].at[k ^ m], land[a].at[m - 1], sems[0].at[a, m - 1], sems[1].at[a, m - 1], _dev(k ^ m, c))

    def start(self, p, land, sems):
        for a in range(len(self.args)):
            for m in range(1, N_CHIPS):
                self._copy(p, land, sems, a, m).start()

    def finish(self, p, land, sems):
        for a in range(len(self.args)):
            for m in range(1, N_CHIPS):
                self._copy(p, land, sems, a, m).wait_recv()
        for a in range(len(self.args)):
            for m in range(1, N_CHIPS):
                self._copy(p, land, sems, a, m).wait_send()


def _hosted(body, comm, *, name, grid, in_specs, out_specs, out_shape, args, scratch_shapes=()):
    ni, no, ns = len(in_specs), len(out_shape), len(scratch_shapes)
    if comm is None:
        res = pl.pallas_call(
            body, name=name, grid=grid, in_specs=list(in_specs), out_specs=list(out_specs), out_shape=list(out_shape),
            scratch_shapes=list(scratch_shapes),
            compiler_params=pltpu.CompilerParams(dimension_semantics=("arbitrary",) * len(grid), vmem_limit_bytes=VMEM_LIMIT),
        )(*args)
        return list(res), []
    nc, nco = len(comm.args), len(comm.out_shape)

    def full(*refs):
        cin = refs[ni:ni + nc]
        outs = refs[ni + nc:ni + nc + no]
        cout = refs[ni + nc + no:ni + nc + no + nco]
        scratch = refs[ni + nc + no + nco:ni + nc + no + nco + ns]
        csems = refs[ni + nc + no + nco + ns:]
        first = _all_of([pl.program_id(ax) == 0 for ax in range(len(grid))])
        last = _all_of([pl.program_id(ax) == grid[ax] - 1 for ax in range(len(grid))])

        @pl.when(first)
        def _():
            comm.start(cin, cout, csems)

        body(*refs[:ni], *outs, *scratch)

        @pl.when(last)
        def _():
            comm.finish(cin, cout, csems)

    res = pl.pallas_call(
        full, name=name, grid=grid, in_specs=[*in_specs, *[ANY] * nc], out_specs=[*out_specs, *[ANY] * nco],
        out_shape=[*out_shape, *comm.out_shape], scratch_shapes=[*scratch_shapes, *comm.sems],
        compiler_params=pltpu.CompilerParams(dimension_semantics=("arbitrary",) * len(grid), vmem_limit_bytes=VMEM_LIMIT,
                                             has_side_effects=True),
    )(*args, *comm.args)
    return list(res[:no]), list(res[no:])


def _all_of(conds):
    out = conds[0]
    for c in conds[1:]:
        out = jnp.logical_and(out, c)
    return out


class _After:
    def __init__(self, token):
        self.args, self.out_shape, self.sems = [token], [], []

    def start(self, *_):
        pass

    def finish(self, *_):
        pass


def _alone(comm, name):
    return _hosted(lambda: None, comm, name=name, grid=(1,), in_specs=[], out_specs=[], out_shape=[], args=[])[1]


HBM = pl.BlockSpec(memory_space=pltpu.HBM)
SEM = pl.BlockSpec(memory_space=pltpu.SEMAPHORE)
DATAFLOW = pltpu.SideEffectType.DATAFLOW_SIDE_EFFECTING


class _SplitGather:
    PER_ARRAY = 8

    def __init__(self, shards):
        self.plan = _Gather(shards)
        self.n = len(shards)

    @staticmethod
    def _tables(sems_of):
        class Table:
            def __init__(self, pick):
                self.pick = pick

            @property
            def at(self):
                return self

            def __getitem__(self, idx):
                return self.pick(idx)

        return [Table(lambda a: sems_of[a][0]), Table(lambda a: sems_of[a][1]),
                Table(lambda am: sems_of[am[0]][2 + am[1]]), Table(lambda am: sems_of[am[0]][5 + am[1]])]

    def start(self):
        n, plan, per = self.n, self.plan, self.PER_ARRAY

        def body(*refs):
            loc, land = refs[:n], refs[n:2 * n]
            sems_of = {a: refs[2 * n + per * a:2 * n + per * (a + 1)] for a in range(n)}
            plan.start(loc, land, self._tables(sems_of))
            refs[-1][...] = jnp.zeros_like(refs[-1])

        lands = [pltpu.with_memory_space_constraint(lax.empty(o.shape, o.dtype), pltpu.HBM) for o in plan.out_shape]
        locs = [pltpu.with_memory_space_constraint(a, pltpu.HBM) for a in plan.args]
        res = pl.pallas_call(
            body, name="ag_start",
            out_shape=[*[DMA(())] * (per * n),
                       *[pltpu.HBM(a.shape, a.dtype) for a in plan.args],
                       *[pltpu.HBM(o.shape, o.dtype) for o in plan.out_shape],
                       jax.ShapeDtypeStruct((8, 128), F32)],
            in_specs=[HBM] * (2 * n),
            out_specs=[SEM] * (per * n) + [HBM] * (2 * n) + [pl.BlockSpec(memory_space=pltpu.VMEM)],
            input_output_aliases={i: per * n + i for i in range(2 * n)},
            compiler_params=pltpu.CompilerParams(has_side_effects=DATAFLOW),
        )(*locs, *lands)
        self.sems = {a: list(res[per * a:per * (a + 1)]) for a in range(n)}
        self.locs = list(res[per * n:per * n + n])
        self.lands = list(res[per * n + n:per * n + 2 * n])
        self.token = res[-1]

    def wait(self, idxs, after, name):
        plan, g, per = self.plan, len(idxs), self.PER_ARRAY

        def body(*refs):
            loc = {a: refs[j] for j, a in enumerate(idxs)}
            land = {a: refs[g + j] for j, a in enumerate(idxs)}
            sems = self._tables({a: refs[2 * g + per * j:2 * g + per * (j + 1)] for j, a in enumerate(idxs)})
            for a in idxs:
                for m in range(1, N_CHIPS):
                    plan._ici(loc, land, sems, a, m, True).wait_recv()
                    plan._ici(loc, land, sems, a, m, False).wait_send()
                plan._own(loc, land, sems, a).wait_recv()
                plan._own(loc, land, sems, a).wait_send()

        res = pl.pallas_call(
            body, name=name,
            out_shape=[pltpu.HBM(self.lands[a].shape, self.lands[a].dtype) for a in idxs],
            in_specs=[HBM] * (2 * g) + [SEM] * (per * g) + [pl.BlockSpec(memory_space=pl.ANY)], out_specs=[HBM] * g,
            input_output_aliases={g + j: j for j in range(g)},
            compiler_params=pltpu.CompilerParams(has_side_effects=DATAFLOW),
        )(*[self.locs[a] for a in idxs], *[self.lands[a] for a in idxs],
          *[s for a in idxs for s in self.sems[a]], after)
        return list(res)


class _SplitExchange:
    PER_ARRAY = 6

    def __init__(self, parts):
        self.plan = _ChipExchange(parts)
        self.n = len(parts)

    @staticmethod
    def _tables(sems_of):
        class Table:
            def __init__(self, pick):
                self.pick = pick

            @property
            def at(self):
                return self

            def __getitem__(self, am):
                return self.pick(am)

        return [Table(lambda am: sems_of[am[0]][am[1]]), Table(lambda am: sems_of[am[0]][3 + am[1]])]

    def start(self, name):
        n, plan, per = self.n, self.plan, self.PER_ARRAY

        def body(*refs):
            p, land = refs[:n], refs[n:2 * n]
            sems_of = {a: refs[2 * n + per * a:2 * n + per * (a + 1)] for a in range(n)}
            plan.start(p, land, self._tables(sems_of))
            refs[-1][...] = jnp.zeros_like(refs[-1])

        lands = [pltpu.with_memory_space_constraint(lax.empty(o.shape, o.dtype), pltpu.HBM) for o in plan.out_shape]
        parts = [pltpu.with_memory_space_constraint(a, pltpu.HBM) for a in plan.args]
        res = pl.pallas_call(
            body, name=name,
            out_shape=[*[DMA(())] * (per * n),
                       *[pltpu.HBM(a.shape, a.dtype) for a in plan.args],
                       *[pltpu.HBM(o.shape, o.dtype) for o in plan.out_shape],
                       jax.ShapeDtypeStruct((8, 128), F32)],
            in_specs=[HBM] * (2 * n),
            out_specs=[SEM] * (per * n) + [HBM] * (2 * n) + [pl.BlockSpec(memory_space=pltpu.VMEM)],
            input_output_aliases={i: per * n + i for i in range(2 * n)},
            compiler_params=pltpu.CompilerParams(has_side_effects=DATAFLOW),
        )(*parts, *lands)
        self.sems = list(res[:per * n])
        self.parts = list(res[per * n:per * n + n])
        self.lands = list(res[per * n + n:per * n + 2 * n])
        return res[-1]

    def wait(self, after, name):
        n, plan, per = self.n, self.plan, self.PER_ARRAY

        def body(*refs):
            p, land = refs[:n], refs[n:2 * n]
            sems_of = {a: refs[2 * n + per * a:2 * n + per * (a + 1)] for a in range(n)}
            plan.finish(p, land, self._tables(sems_of))

        res = pl.pallas_call(
            body, name=name,
            out_shape=[*[pltpu.HBM(a.shape, a.dtype) for a in self.parts], *[pltpu.HBM(a.shape, a.dtype) for a in self.lands]],
            in_specs=[HBM] * (2 * n) + [SEM] * (per * n) + [pl.BlockSpec(memory_space=pl.ANY)], out_specs=[HBM] * (2 * n),
            input_output_aliases={i: i for i in range(2 * n)},
            compiler_params=pltpu.CompilerParams(has_side_effects=DATAFLOW),
        )(*self.parts, *self.lands, *self.sems, after)
        return list(res[:n]), list(res[n:])


def _pass_on(lands, name):
    n = len(lands)

    def body(*refs):
        out = refs[n:2 * n]
        send_sems, recv_sems = refs[2 * n:]
        x, y, c, k = _place()
        cps = []
        for a in range(n):
            for m in range(1, N_CHIPS):
                got = out[a].at[k ^ m, c]
                cp = _remote(got, got, send_sems.at[a, m - 1], recv_sems.at[a, m - 1], (x, y, 1 - c))
                cp.start()
                cps.append(cp)
        for a in range(n):
            for m in range(1, N_CHIPS):
                theirs = out[a].at[k ^ m, 1 - c]
                _remote(theirs, theirs, send_sems.at[a, m - 1], recv_sems.at[a, m - 1], (x, y, 1 - c)).wait_recv()
        for cp in cps:
            cp.wait_send()

    return pl.pallas_call(
        body, name=name, out_shape=[jax.ShapeDtypeStruct(a.shape, a.dtype) for a in lands],
        in_specs=[ANY] * n, out_specs=[ANY] * n, input_output_aliases={a: a for a in range(n)},
        scratch_shapes=[DMA((n, 3)), DMA((n, 3))], compiler_params=pltpu.CompilerParams(has_side_effects=True),
    )(*lands)


def _sibling_exchange(grads, small, name):
    n = len(grads)
    ns = 0 if small is None else 1

    def body(*refs):
        g = refs[:n]
        land = refs[n + ns:2 * n + ns]
        send_sems, recv_sems, own_sem, ssend_sems, srecv_sems = refs[2 * n + 2 * ns:]
        x, y, c, k = _place()
        me = 2 * k + c
        cps = []
        for a in range(n):
            cp = _remote(g[a].at[:, pl.ds(1 - c, 1)], land[a], send_sems.at[a], recv_sems.at[a], (x, y, 1 - c))
            cp.start()
            cps.append(cp)
        if small is not None:
            sm, smg = refs[n], refs[2 * n + 1]
            peers = [me ^ m for m in range(1, N_DEV)]
            ids = [(p // 4, (p // 2) % 2, p % 2) for p in peers]
            own = pltpu.make_async_copy(sm, smg.at[me], own_sem)
            own.start()
            for m in range(1, N_DEV):
                cp = _remote(sm, smg.at[me], ssend_sems.at[m - 1], srecv_sems.at[m - 1], ids[m - 1])
                cp.start()
                cps.append(cp)
            for m in range(1, N_DEV):
                _remote(sm, smg.at[peers[m - 1]], ssend_sems.at[m - 1], srecv_sems.at[m - 1], ids[m - 1]).wait_recv()
            own.wait()
        for cp in cps[:n]:
            cp.wait_recv()
        for cp in cps:
            cp.wait_send()

    out_shape = [jax.ShapeDtypeStruct((N_CHIPS, 1) + a.shape[2:], a.dtype) for a in grads]
    ins = list(grads)
    if small is not None:
        out_shape.append(jax.ShapeDtypeStruct((N_DEV,) + small.shape, small.dtype))
        ins.append(small)
    return pl.pallas_call(
        body, name=name, out_shape=out_shape, in_specs=[ANY] * (n + ns), out_specs=[ANY] * (n + ns),
        scratch_shapes=[DMA((n,)), DMA((n,)), DMA, DMA((N_DEV - 1,)), DMA((N_DEV - 1,))],
        compiler_params=pltpu.CompilerParams(has_side_effects=True),
    )(*ins)


def _sibling_share(halves, col_half, name):
    n = len(halves)

    def body(*refs):
        out = refs[n:2 * n]
        send_sems, recv_sems = refs[2 * n:]
        x, y, c, k = _place()

        def half(a, core):
            if not col_half[a]:
                return out[a].at[:, pl.ds(core, 1)]
            cols = out[a].shape[-1] // 2
            return out[a].at[:, :, pl.ds(pl.multiple_of(core * cols, cols), cols)]

        cps = []
        for a in range(n):
            cp = _remote(half(a, c), half(a, c), send_sems.at[a], recv_sems.at[a], (x, y, 1 - c))
            cp.start()
            cps.append(cp)
        for a in range(n):
            _remote(half(a, 1 - c), half(a, 1 - c), send_sems.at[a], recv_sems.at[a], (x, y, 1 - c)).wait_recv()
        for cp in cps:
            cp.wait_send()

    out_shape = [jax.ShapeDtypeStruct(a.shape, a.dtype) for a in halves]
    return pl.pallas_call(
        body, name=name, out_shape=out_shape, in_specs=[ANY] * n, out_specs=[ANY] * n,
        input_output_aliases={a: a for a in range(n)}, scratch_shapes=[DMA((n,)), DMA((n,))],
        compiler_params=pltpu.CompilerParams(has_side_effects=True),
    )(*halves)


def _add_sibling(g, land, core):
    _, _, r, c = g.shape
    rb = _row_block(r, c * (3 * 2 * 2 + 2 * 4))

    def body(core_ref, g_ref, l_ref, o_ref):
        o_ref[...] = (g_ref[...].astype(F32) + l_ref[...].astype(F32)).astype(o_ref.dtype)

    return pl.pallas_call(
        body, name="rs_add_sibling", out_shape=jax.ShapeDtypeStruct((N_CHIPS, r, c), g.dtype),
        grid_spec=pltpu.PrefetchScalarGridSpec(
            num_scalar_prefetch=1, grid=(N_CHIPS, r // rb),
            in_specs=[pl.BlockSpec((None, None, rb, c), lambda j, i, core_ref: (j, core_ref[0], i, 0)),
                      pl.BlockSpec((None, None, rb, c), lambda j, i, core_ref: (j, 0, i, 0))],
            out_specs=pl.BlockSpec((None, rb, c), lambda j, i, core_ref: (j, i, 0))),
        compiler_params=pltpu.CompilerParams(dimension_semantics=("parallel", "parallel"), vmem_limit_bytes=VMEM_LIMIT),
    )(core, g, land)


def _add_chips(part, land, where, layer=0, n_layers=1, into=None, col_half=False, rows=(0, 1)):
    _, r, c = part.shape
    sub, n_sub = rows
    rb = _row_block(r, c * (4 * 2 * 2 + 4 * 2 + 2 * 4))

    def body(where_ref, p_ref, l_ref, *rest):
        acc = p_ref[...].astype(F32)
        for m in range(N_CHIPS - 1):
            acc = acc + l_ref[m].astype(F32)
        rest[-1][...] = acc

    in_specs = [pl.BlockSpec((None, rb, c), lambda i, where_ref: (where_ref[0], i, 0)),
                pl.BlockSpec((N_CHIPS - 1, rb, c), lambda i, where_ref: (0, i, 0))]
    args = [where, part, land]
    if into is not None:
        in_specs.append(ANY)
        args.append(into)
    if col_half:
        out_shape = jax.ShapeDtypeStruct((n_layers, r, 2 * c), F32)
        out_spec = pl.BlockSpec((None, rb, c), lambda i, where_ref: (layer, i, where_ref[1]))
    else:
        out_shape = jax.ShapeDtypeStruct((n_layers, 2, n_sub * r, c), F32)
        out_spec = pl.BlockSpec((None, None, rb, c), lambda i, where_ref: (layer, where_ref[1], sub * (r // rb) + i, 0))
    return pl.pallas_call(
        body, name="rs_add_chips", out_shape=out_shape,
        grid_spec=pltpu.PrefetchScalarGridSpec(
            num_scalar_prefetch=1, grid=(r // rb,), in_specs=in_specs, out_specs=out_spec),
        input_output_aliases={} if into is None else {3: 0},
        compiler_params=pltpu.CompilerParams(dimension_semantics=("parallel",), vmem_limit_bytes=VMEM_LIMIT),
    )(*args)


def _sum_small(smg):
    def body(s_ref, o_ref):
        acc = s_ref[0]
        for j in range(1, N_DEV):
            acc = acc + s_ref[j]
        o_ref[...] = acc

    return pl.pallas_call(body, name="rs_sum_small", out_shape=jax.ShapeDtypeStruct(smg.shape[1:], F32))(smg)


def _pool_windows(ext_ref, g, gw, tm, first_row):
    w = POOL_WINDOWS[g]
    slab = ext_ref[:, g * gw:(g + 1) * gw]
    p, k = slab, 1
    while k < w:
        p = p + pltpu.roll(p, k, 0)
        k *= 2
    t = first_row + lax.broadcasted_iota(jnp.int32, (tm, 1), 0)
    cnt = jnp.minimum(t + 1, w).astype(F32)
    return p[POOL_HALO:] / cnt - slab[POOL_HALO:]


def _fwd_pool(x, small, scale, poolw, comm=None):
    t, d = x.shape
    tm = _token_tile(t)
    gw = d // len(POOL_WINDOWS)

    def body(x_ref, sm_ref, sc_ref, w_ref, h_ref, ext_ref, mix_ref):
        i = pl.program_id(0)

        @pl.when(i == 0)
        def _():
            ext_ref[0:POOL_HALO, :] = jnp.zeros((POOL_HALO, d), F32)

        @pl.when(i > 0)
        def _():
            ext_ref[0:POOL_HALO, :] = ext_ref[tm:tm + POOL_HALO, :]

        xv = x_ref[...]
        xh, _ = _rms(xv)
        ext_ref[POOL_HALO:, :] = xh * sm_ref[0:1, :]
        for g in range(len(POOL_WINDOWS)):
            pooled = _pool_windows(ext_ref, g, gw, tm, i * tm)
            cols = slice(g * gw, (g + 1) * gw)
            mix_ref[:, cols] = _dot(pooled.astype(BF16), w_ref[g]) * sc_ref[:, cols]
        mh, _ = _rms(mix_ref[...])
        h_ref[...] = xv + mh * sm_ref[1:2, :]

    (h,), got = _hosted(
        body, comm, name="fwd_pool", grid=(t // tm,), out_shape=[jax.ShapeDtypeStruct((t, d), F32)],
        in_specs=[pl.BlockSpec((tm, d), lambda i: (i, 0)), _resident(small.shape, lambda i: (0, 0)),
                  _resident(scale.shape, lambda i: (0, 0)), _resident(poolw.shape, lambda i: (0, 0, 0))],
        out_specs=[pl.BlockSpec((tm, d), lambda i: (i, 0))],
        scratch_shapes=[pltpu.VMEM((POOL_HALO + tm, d), F32), pltpu.VMEM((tm, d), F32)],
        args=[x, small, scale, poolw])
    return h, got


def _fwd_ffn(h, small, wgu, wd, layer, comm=None, target=None):
    t, d = h.shape
    tm = _token_tile(t)
    steps = t // tm
    fc = wgu.shape[-1]
    f = 2 * fc
    g_in, g_out = 4 * layer + 2, 4 * layer + 3
    with_loss = target is not None

    def body(h_ref, *refs):
        if with_loss:
            t_ref, sm_ref, wgu_ref, wd_ref, o_ref, gu_ref, ff_ref, n_ref, l_ref, acc_ref = refs
        else:
            sm_ref, wgu_ref, wd_ref, o_ref, gu_ref, ff_ref, n_ref = refs
        hv = h_ref[...]
        hh, _ = _rms(hv)
        n = (hh * sm_ref[g_in:g_in + 1, :]).astype(BF16)
        n_ref[...] = n
        ff = None
        for j in range(2):
            gate = _dot(n, wgu_ref[j])
            up = _dot(n, wgu_ref[2 + j])
            gu_ref[:, j * fc:(j + 1) * fc] = gate.astype(BF16)
            gu_ref[:, f + j * fc:f + (j + 1) * fc] = up.astype(BF16)
            act = (gate * jax.nn.sigmoid(gate) * up).astype(BF16)
            part = _dot(act, wd_ref[j * fc:(j + 1) * fc, :])
            ff = part if ff is None else ff + part
        ff_ref[...] = ff
        fh, _ = _rms(ff)
        out = hv + fh * sm_ref[g_out:g_out + 1, :]
        if not with_loss:
            o_ref[...] = out
            return
        i = pl.program_id(0)
        e = out - t_ref[...]
        o_ref[...] = e * (1.0 / d)

        @pl.when(i == 0)
        def _():
            acc_ref[...] = jnp.zeros_like(acc_ref)

        acc_ref[...] += _colsum(e * e)

        @pl.when(i == steps - 1)
        def _():
            l_ref[...] = jnp.full(l_ref.shape, 0.5 / d, F32) * jnp.sum(acc_ref[...])

    row = lambda i: (i, 0)
    out_shape = [jax.ShapeDtypeStruct((t, d), F32), jax.ShapeDtypeStruct((t, 2 * f), BF16),
                 jax.ShapeDtypeStruct((t, d), F32), jax.ShapeDtypeStruct((t, d), BF16)]
    out_specs = [pl.BlockSpec((tm, d), row), pl.BlockSpec((tm, 2 * f), row), pl.BlockSpec((tm, d), row),
                 pl.BlockSpec((tm, d), row)]
    weight_specs = [_resident(small.shape, lambda i: (0, 0)), _resident(wgu.shape, lambda i: (0, 0, 0)),
                    _resident(wd.shape, lambda i: (0, 0))]
    if with_loss:
        return _hosted(
            body, comm, name=f"fwd_ffn{layer}_loss", grid=(steps,),
            out_shape=out_shape + [jax.ShapeDtypeStruct((8, 128), F32)],
            in_specs=[pl.BlockSpec((tm, d), row), pl.BlockSpec((tm, d), row)] + weight_specs,
            out_specs=out_specs + [pl.BlockSpec((8, 128), lambda i: (0, 0))],
            scratch_shapes=[pltpu.VMEM((1, d), F32)], args=[h, target, small, wgu, wd])
    return _hosted(
        body, comm, name=f"fwd_ffn{layer}", grid=(steps,), out_shape=out_shape,
        in_specs=[pl.BlockSpec((tm, d), row)] + weight_specs, out_specs=out_specs, args=[h, small, wgu, wd])


def _fwd_conv(h, small, win, wout, comm=None):
    t, d = h.shape
    tm = _token_tile(t)
    pc = win.shape[-1]

    def body(h_ref, sm_ref, win_ref, wout_ref, o_ref, proj_ref, y_ref, n_ref, pj_ref, uext_ref):
        i = pl.program_id(0)

        @pl.when(i == 0)
        def _():
            uext_ref[0:CONV_HALO, :] = jnp.zeros((CONV_HALO, d), F32)

        @pl.when(i > 0)
        def _():
            uext_ref[0:CONV_HALO, :] = uext_ref[tm:tm + CONV_HALO, :]

        hv = h_ref[...]
        hh, _ = _rms(hv)
        n = (hh * sm_ref[4:5, :]).astype(BF16)
        n_ref[...] = n
        for k in range(N_CHIPS):
            pj_ref[:, k * pc:(k + 1) * pc] = _dot(n, win_ref[k])
        proj_ref[...] = pj_ref[...].astype(BF16)
        uext_ref[CONV_HALO:, :] = pj_ref[:, d:2 * d] * pj_ref[:, 2 * d:]
        taps = [sm_ref[8 + j:9 + j, :] for j in range(3)]
        full = uext_ref[...]
        conv = (full[CONV_HALO:] * taps[2] + pltpu.roll(full, 1, 0)[CONV_HALO:] * taps[1]
                + pltpu.roll(full, 2, 0)[CONV_HALO:] * taps[0])
        y = _dot((pj_ref[:, 0:d] * conv).astype(BF16), wout_ref[...])
        y_ref[...] = y
        yh, _ = _rms(y)
        o_ref[...] = hv + yh * sm_ref[5:6, :]

    row = lambda i: (i, 0)
    return _hosted(
        body, comm, name="fwd_conv", grid=(t // tm,),
        out_shape=[jax.ShapeDtypeStruct((t, d), F32), jax.ShapeDtypeStruct((t, 3 * d), BF16),
                   jax.ShapeDtypeStruct((t, d), F32), jax.ShapeDtypeStruct((t, d), BF16)],
        in_specs=[pl.BlockSpec((tm, d), row), _resident(small.shape, lambda i: (0, 0)),
                  _resident(win.shape, lambda i: (0, 0, 0)), _resident(wout.shape, lambda i: (0, 0))],
        out_specs=[pl.BlockSpec((tm, d), row), pl.BlockSpec((tm, 3 * d), row), pl.BlockSpec((tm, d), row),
                   pl.BlockSpec((tm, d), row)],
        scratch_shapes=[pltpu.VMEM((tm, 3 * d), F32), pltpu.VMEM((CONV_HALO + tm, d), F32)],
        args=[h, small, win, wout])


def _bwd_ffn(dh, h, ff, gu, small, wgu, wd, layer, comm=None):
    t, d = h.shape
    tm = _token_tile(t, 256)
    fc = wgu.shape[-1]
    f = 2 * fc
    g_in, g_out = 4 * layer + 2, 4 * layer + 3

    def body(dh_ref, h_ref, ff_ref, gu_ref, sm_ref, wgu_ref, wd_ref, o_ref, dgu_ref, dff_ref, act_ref, sg_ref):
        i = pl.program_id(0)

        @pl.when(i == 0)
        def _():
            sg_ref[...] = jnp.zeros_like(sg_ref)

        dy = dh_ref[...]
        fh, r3 = _rms(ff_ref[...])
        sg_ref[1:2, :] += _colsum(dy * fh)
        dff = _rms_bwd(dy, fh, r3, sm_ref[g_out:g_out + 1, :]).astype(BF16)
        dff_ref[...] = dff
        for j in range(2):
            dact = _dot_nt(dff, wd_ref[j * fc:(j + 1) * fc, :])
            gate = gu_ref[:, j * fc:(j + 1) * fc].astype(F32)
            up = gu_ref[:, f + j * fc:f + (j + 1) * fc].astype(F32)
            sig = jax.nn.sigmoid(gate)
            silu = gate * sig
            act_ref[:, j * fc:(j + 1) * fc] = (silu * up).astype(BF16)
            dgu_ref[:, j * fc:(j + 1) * fc] = (dact * up * (sig * (1.0 + gate * (1.0 - sig)))).astype(BF16)
            dgu_ref[:, f + j * fc:f + (j + 1) * fc] = (dact * silu).astype(BF16)
        dn = None
        for k in range(N_CHIPS):
            part = _dot_nt(dgu_ref[:, k * fc:(k + 1) * fc], wgu_ref[k])
            dn = part if dn is None else dn + part
        hh, r2 = _rms(h_ref[...])
        sg_ref[0:1, :] += _colsum(dn * hh)
        o_ref[...] = dy + _rms_bwd(dn, hh, r2, sm_ref[g_in:g_in + 1, :])

    row = lambda i: (i, 0)
    return _hosted(
        body, comm, name=f"bwd_ffn{layer}", grid=(t // tm,),
        out_shape=[jax.ShapeDtypeStruct((t, d), F32), jax.ShapeDtypeStruct((t, 2 * f), BF16),
                   jax.ShapeDtypeStruct((t, d), BF16), jax.ShapeDtypeStruct((t, f), BF16),
                   jax.ShapeDtypeStruct((8, d), F32)],
        in_specs=[pl.BlockSpec((tm, d), row), pl.BlockSpec((tm, d), row), pl.BlockSpec((tm, d), row),
                  pl.BlockSpec((tm, 2 * f), row), _resident(small.shape, lambda i: (0, 0)),
                  _resident(wgu.shape, lambda i: (0, 0, 0)), _resident(wd.shape, lambda i: (0, 0))],
        out_specs=[pl.BlockSpec((tm, d), row), pl.BlockSpec((tm, 2 * f), row), pl.BlockSpec((tm, d), row),
                   pl.BlockSpec((tm, f), row), pl.BlockSpec((8, d), lambda i: (0, 0))],
        args=[dh, h, ff, gu, small, wgu, wd])


def _bwd_conv(dh, h, y, proj, small, win, wout, comm=None):
    t, d = h.shape
    tm = _token_tile(t)
    steps = t // tm
    pc = win.shape[-1]
    halo_blocks = tm // 16

    def body(dh_ref, h_ref, y_ref, proj_ref, halo_ref, sm_ref, win_ref, wout_ref,
             o_ref, dproj_ref, dy_ref, bc_ref, sg_ref, uext_ref, dcext_ref, carry_ref):
        i = pl.program_id(0)
        tile = steps - 1 - i

        @pl.when(i == 0)
        def _():
            sg_ref[...] = jnp.zeros_like(sg_ref)
            carry_ref[...] = jnp.zeros_like(carry_ref)

        dy = dh_ref[...]
        yh, r1 = _rms(y_ref[...])
        sg_ref[1:2, :] += _colsum(dy * yh)
        dyv = _rms_bwd(dy, yh, r1, sm_ref[5:6, :]).astype(BF16)
        dy_ref[...] = dyv
        dbc = _dot_nt(dyv, wout_ref[...])
        b = proj_ref[:, 0:d].astype(F32)
        cg = proj_ref[:, d:2 * d].astype(F32)
        v = proj_ref[:, 2 * d:].astype(F32)
        halo = halo_ref[...].astype(F32)[16 - CONV_HALO:]
        uh = halo[:, d:2 * d] * halo[:, 2 * d:]
        uext_ref[0:CONV_HALO, :] = jnp.where(tile > 0, uh, jnp.zeros_like(uh))
        uext_ref[CONV_HALO:, :] = cg * v
        taps = [sm_ref[8 + j:9 + j, :] for j in range(3)]
        full = uext_ref[...]
        u0 = full[CONV_HALO:]
        u1 = pltpu.roll(full, 1, 0)[CONV_HALO:]
        u2 = pltpu.roll(full, 2, 0)[CONV_HALO:]
        conv = u0 * taps[2] + u1 * taps[1] + u2 * taps[0]
        bc_ref[...] = (b * conv).astype(BF16)
        dconv = dbc * b
        sg_ref[4:5, :] += _colsum(dconv * u0)
        sg_ref[3:4, :] += _colsum(dconv * u1)
        sg_ref[2:3, :] += _colsum(dconv * u2)
        dcext_ref[0:tm, :] = dconv
        dcext_ref[tm:, :] = carry_ref[...]
        carry_ref[...] = dconv[0:CONV_HALO]
        dfull = dcext_ref[...]
        n8 = tm + CONV_HALO
        du = (dfull[0:tm] * taps[2] + pltpu.roll(dfull, n8 - 1, 0)[0:tm] * taps[1]
              + pltpu.roll(dfull, n8 - 2, 0)[0:tm] * taps[0])
        dproj_ref[:, 0:d] = (dbc * conv).astype(BF16)
        dproj_ref[:, d:2 * d] = (du * v).astype(BF16)
        dproj_ref[:, 2 * d:] = (du * cg).astype(BF16)
        dn = None
        for k in range(N_CHIPS):
            part = _dot_nt(dproj_ref[:, k * pc:(k + 1) * pc], win_ref[k])
            dn = part if dn is None else dn + part
        hh, r0 = _rms(h_ref[...])
        sg_ref[0:1, :] += _colsum(dn * hh)
        o_ref[...] = dy + _rms_bwd(dn, hh, r0, sm_ref[4:5, :])

    rev = lambda i: (steps - 1 - i, 0)
    before = lambda i: (jnp.maximum((steps - 1 - i) * halo_blocks - 1, 0), 0)
    return _hosted(
        body, comm, name="bwd_conv", grid=(steps,),
        out_shape=[jax.ShapeDtypeStruct((t, d), F32), jax.ShapeDtypeStruct((t, 3 * d), BF16),
                   jax.ShapeDtypeStruct((t, d), BF16), jax.ShapeDtypeStruct((t, d), BF16),
                   jax.ShapeDtypeStruct((8, d), F32)],
        in_specs=[pl.BlockSpec((tm, d), rev), pl.BlockSpec((tm, d), rev), pl.BlockSpec((tm, d), rev),
                  pl.BlockSpec((tm, 3 * d), rev), pl.BlockSpec((16, 3 * d), before),
                  _resident(small.shape, lambda i: (0, 0)), _resident(win.shape, lambda i: (0, 0, 0)),
                  _resident(wout.shape, lambda i: (0, 0))],
        out_specs=[pl.BlockSpec((tm, d), rev), pl.BlockSpec((tm, 3 * d), rev), pl.BlockSpec((tm, d), rev),
                   pl.BlockSpec((tm, d), rev), pl.BlockSpec((8, d), lambda i: (0, 0))],
        scratch_shapes=[pltpu.VMEM((CONV_HALO + tm, d), F32), pltpu.VMEM((tm + CONV_HALO, d), F32),
                        pltpu.VMEM((CONV_HALO, d), F32)],
        args=[dh, h, y, proj, proj, small, win, wout])


def _bwd_pool(dh, x, small, scale, poolw, comm=None):
    t, d = x.shape
    tm = _token_tile(t)
    steps = t // tm
    ng = len(POOL_WINDOWS)
    gw = d // ng
    halo_blocks = tm // POOL_HALO

    def body(dh_ref, x_ref, halo_ref, sm_ref, sc_ref, w_ref, o_ref, dw_ref, sg_ref,
             ext_ref, mix_ref, mm_ref, pb_ref, qext_ref, dhn_ref, carry_ref):
        i = pl.program_id(0)
        tile = steps - 1 - i

        @pl.when(i == 0)
        def _():
            sg_ref[...] = jnp.zeros_like(sg_ref)
            dw_ref[...] = jnp.zeros_like(dw_ref)
            carry_ref[...] = jnp.zeros_like(carry_ref)

        g0 = sm_ref[0:1, :]
        xv = x_ref[...]
        xh, r0 = _rms(xv)
        hx, _ = _rms(halo_ref[...])
        ext_ref[0:POOL_HALO, :] = jnp.where(tile > 0, hx * g0, jnp.zeros_like(hx))
        ext_ref[POOL_HALO:, :] = xh * g0
        for g in range(ng):
            pooled = _pool_windows(ext_ref, g, gw, tm, tile * tm)
            cols = slice(g * gw, (g + 1) * gw)
            pb = pooled.astype(BF16)
            pb_ref[:, cols] = pb
            mm = _dot(pb, w_ref[g])
            mm_ref[:, cols] = mm
            mix_ref[:, cols] = mm * sc_ref[:, cols]
        dy = dh_ref[...]
        mh, r1 = _rms(mix_ref[...])
        sg_ref[1:2, :] += _colsum(dy * mh)
        dmix = _rms_bwd(dy, mh, r1, sm_ref[1:2, :])
        sg_ref[2:3, :] += _colsum(dmix * mm_ref[...])
        mix_ref[...] = dmix * sc_ref[...]
        n16 = tm + POOL_HALO
        for g in range(ng):
            w = POOL_WINDOWS[g]
            cols = slice(g * gw, (g + 1) * gw)
            dmm = mix_ref[:, cols].astype(BF16)
            dpooled = _dot_nt(dmm, w_ref[g])
            dw_ref[g] += _dot_tn(pb_ref[:, cols], dmm)
            trow = tile * tm + lax.broadcasted_iota(jnp.int32, (tm, 1), 0)
            q = dpooled / jnp.minimum(trow + 1, w).astype(F32)
            qext_ref[0:tm, cols] = q
            qext_ref[tm:, cols] = carry_ref[:, cols]
            carry_ref[:, cols] = q[0:POOL_HALO]
            p, k = qext_ref[:, cols], 1
            while k < w:
                p = p + pltpu.roll(p, n16 - k, 0)
                k *= 2
            dhn_ref[:, cols] = p[0:tm] - dpooled
        dhn = dhn_ref[...]
        sg_ref[0:1, :] += _colsum(dhn * xh)
        o_ref[...] = dy + _rms_bwd(dhn, xh, r0, g0)

    rev = lambda i: (steps - 1 - i, 0)
    before = lambda i: (jnp.maximum((steps - 1 - i) * halo_blocks - 1, 0), 0)
    return _hosted(
        body, comm, name="bwd_pool", grid=(steps,),
        out_shape=[jax.ShapeDtypeStruct((t, d), F32), jax.ShapeDtypeStruct((ng, gw, gw), F32),
                   jax.ShapeDtypeStruct((8, d), F32)],
        in_specs=[pl.BlockSpec((tm, d), rev), pl.BlockSpec((tm, d), rev), pl.BlockSpec((POOL_HALO, d), before),
                  _resident(small.shape, lambda i: (0, 0)), _resident(scale.shape, lambda i: (0, 0)),
                  _resident(poolw.shape, lambda i: (0, 0, 0))],
        out_specs=[pl.BlockSpec((tm, d), rev), pl.BlockSpec((ng, gw, gw), lambda i: (0, 0, 0)),
                   pl.BlockSpec((8, d), lambda i: (0, 0))],
        scratch_shapes=[pltpu.VMEM((POOL_HALO + tm, d), F32), pltpu.VMEM((tm, d), F32), pltpu.VMEM((tm, d), F32),
                        pltpu.VMEM((tm, d), BF16), pltpu.VMEM((tm + POOL_HALO, d), F32), pltpu.VMEM((tm, d), F32),
                        pltpu.VMEM((POOL_HALO, d), F32)],
        args=[dh, x, x, small, scale, poolw])


def _weight_grad(a, b, bm, bn, half_on, name, comm=None, rows=(0, 1)):
    t, m = a.shape
    _, n = b.shape
    if half_on == "a":
        a_cols, b_cols = 2 * bm, bn
    else:
        a_cols, b_cols = bm, 2 * bn
    steps = max(m // a_cols, n // b_cols)
    sub, n_sub = rows
    tr = bm // n_sub

    def spec(cols, total):
        if cols == total:
            return _resident((t, cols), lambda p, j: (0, 0))
        return pl.BlockSpec((t, cols), lambda p, j: (0, j))

    def tile(a_ref, b_ref, half):
        if half_on == "a":
            first = half * bm + sub * tr
            return _dot_tn(a_ref[:, first:first + tr], b_ref[...])
        return _dot_tn(a_ref[...], b_ref[:, half * bn:(half + 1) * bn])

    def body(a_ref, b_ref, parts_ref, land_ref, acc_ref, stage_ref, got_ref, send_sems, recv_sems, got_sem):
        p, j = pl.program_id(0), pl.program_id(1)
        x, y, c, _ = _place()
        half = jnp.where(p == 0, 1 - c, c)

        def send(jj):
            return _remote(stage_ref.at[jj % 2], land_ref.at[jj], send_sems.at[jj], recv_sems.at[jj], (x, y, 1 - c))

        def fetch():
            return pltpu.make_async_copy(land_ref.at[j], got_ref, got_sem)

        @pl.when(p == 1)
        def _():
            @pl.when(j == 0)
            def _():
                for jj in range(max(steps - 2, 0), steps):
                    send(jj).wait_send()

            send(j).wait_recv()
            fetch().start()

        for hv in range(2):
            @pl.when(half == hv)
            def _():
                acc_ref[...] = tile(a_ref, b_ref, hv)

        @pl.when(p == 0)
        def _():
            @pl.when(j >= 2)
            def _():
                send(j - 2).wait_send()

            stage_ref[j % 2] = acc_ref[...].astype(BF16)
            send(j).start()

        @pl.when(p == 1)
        def _():
            fetch().wait()
            parts_ref[...] = (acc_ref[...] + got_ref[...].astype(F32)).astype(BF16)

    (parts, _), got = _hosted(
        body, comm, name=name, grid=(2, steps),
        out_shape=[jax.ShapeDtypeStruct((steps, tr, bn), BF16), jax.ShapeDtypeStruct((steps, tr, bn), BF16)],
        in_specs=[spec(a_cols, m), spec(b_cols, n)],
        out_specs=[pl.BlockSpec((None, tr, bn), lambda p, j: (p * j, 0, 0)), ANY],
        scratch_shapes=[pltpu.VMEM((tr, bn), F32), pltpu.VMEM((2, tr, bn), BF16), pltpu.VMEM((tr, bn), BF16),
                        DMA((steps,)), DMA((steps,)), DMA],
        args=[a, b])
    return parts, got


def _adamw(w, g, m, v, name):
    r, c = w.shape
    rb = _row_block(r, c * (8 * 4 * 2 + 4 * 4))
    bc1 = 1.0 - ADAM_B1 ** ADAM_STEP
    bc2 = 1.0 - ADAM_B2 ** ADAM_STEP

    def body(w_ref, g_ref, m_ref, v_ref, d_ref, nm_ref, nv_ref, go_ref):
        gv = g_ref[...]
        go_ref[...] = gv
        nm = ADAM_B1 * m_ref[...] + (1.0 - ADAM_B1) * gv
        nv = ADAM_B2 * v_ref[...] + (1.0 - ADAM_B2) * (gv * gv)
        nm_ref[...] = nm
        nv_ref[...] = nv
        d_ref[...] = -ADAM_LR * ((nm / bc1) / (jnp.sqrt(nv / bc2) + ADAM_EPS) + ADAM_WD * w_ref[...])

    spec = pl.BlockSpec((rb, c), lambda i: (i, 0))
    return pl.pallas_call(
        body, name=name, grid=(r // rb,), out_shape=[jax.ShapeDtypeStruct((r, c), F32)] * 4,
        in_specs=[spec] * 4, out_specs=[spec] * 4,
        compiler_params=pltpu.CompilerParams(dimension_semantics=("parallel",), vmem_limit_bytes=VMEM_LIMIT),
    )(w, g, m, v)


def kernel(x, norm_gains, pool_w, pool_scale, conv_in_w, conv_w, conv_out_w, ffn_gate_up_w, ffn_down_w, loss_target, m_norm_gains, m_pool_w, m_pool_scale, m_conv_in_w, m_conv_w, m_conv_out_w, m_ffn_gate_up_w, m_ffn_down_w, v_norm_gains, v_pool_w, v_pool_scale, v_conv_in_w, v_conv_w, v_conv_out_w, v_ffn_gate_up_w, v_ffn_down_w):
    _, t, d = x.shape
    dq = d // N_CHIPS
    ng = len(POOL_WINDOWS)
    gw = d // ng
    fq = ffn_down_w.shape[1]
    f = N_CHIPS * fq
    fc = f // 2
    core = lax.axis_index("c")
    chip = 2 * lax.axis_index("x") + lax.axis_index("y")
    core_arr = jnp.reshape(core, (1,)).astype(jnp.int32)
    where_arr = jnp.stack([chip, core]).astype(jnp.int32)
    x2, target = x[0], loss_target[0]

    small_loc = jnp.concatenate(
        [norm_gains.reshape(8, dq), conv_w[0], jnp.zeros((5, dq), F32)], axis=0).reshape(1, 2, 8, dq)
    pool_loc = pool_w.astype(BF16).reshape(1, 2, ng // 2 * (gw // N_CHIPS), gw)
    wgu_loc = ffn_gate_up_w.astype(BF16).reshape(2, 2, d // 2, fc)
    wd_loc = ffn_down_w.astype(BF16).reshape(2, 2, fq // 2, d)
    win_loc = conv_in_w.astype(BF16).reshape(1, 2, d // 2, -1)
    wout_loc = conv_out_w.astype(BF16).reshape(1, 2, dq // 2, d)

    def ffn_weights(wgu_f, wd_f):
        return wgu_f.reshape(N_CHIPS, d, fc), wd_f.reshape(f, d)

    ag = _SplitGather([(pool_loc, 0), (small_loc, 0), (wgu_loc, 0), (wd_loc, 0), (win_loc, 0), (wout_loc, 0),
                       (wgu_loc, 1), (wd_loc, 1)])
    ag.start()
    pool_f, small_f = _pass_on(ag.wait([0, 1], ag.token, "ag_wait_first"), "ag_pass_first")
    poolw = pool_f.reshape(N_CHIPS, ng, gw // N_CHIPS, gw).transpose(1, 0, 2, 3).reshape(ng, gw, gw)
    small = small_f.transpose(1, 2, 0, 3).reshape(16, d)
    h1, _ = _fwd_pool(x2, small, pool_scale, poolw)
    wgu0, wd0 = ffn_weights(*_pass_on(ag.wait([2, 3], h1, "ag_wait_ffn0"), "ag_pass_ffn0"))
    (h2, gu0, ff0, n0), _ = _fwd_ffn(h1, small, wgu0, wd0, 0)
    win_f, wout_f = _pass_on(ag.wait([4, 5], h2, "ag_wait_conv"), "ag_pass_conv")
    win_f, wout_f = win_f.reshape(N_CHIPS, d, -1), wout_f.reshape(d, d)
    (h3, proj, y, nc), _ = _fwd_conv(h2, small, win_f, wout_f)
    wgu1, wd1 = ffn_weights(*_pass_on(ag.wait([6, 7], h3, "ag_wait_ffn1"), "ag_pass_ffn1"))
    (dh4, gu1, ff1, n1, loss_blk), _ = _fwd_ffn(h3, small, wgu1, wd1, 1, target=target)

    (dh3, dgu1, dff1, act1, sg_f1), _ = _bwd_ffn(dh4, h3, ff1, gu1, small, wgu1, wd1, 1)
    parts_d1, _ = _weight_grad(act1, dff1, fc, d // 2, "b", "dw_down1")
    parts_gu1, _ = _weight_grad(n1, dgu1, d // 2, fc, "a", "dw_gate_up1")
    ex_ffn1 = _SplitExchange([parts_d1.reshape(N_CHIPS, fq, d // 2), parts_gu1])
    started = ex_ffn1.start("rs_start_ffn1")
    (dh2, dproj, dyv, bcv, sg_c), _ = _bwd_conv(dh3, h2, y, proj, small, win_f, wout_f, _After(started))
    parts_in, _ = _weight_grad(nc, dproj, d // 2, 3 * d // N_CHIPS, "a", "dw_conv_in")
    parts_out, _ = _weight_grad(bcv, dyv, dq // 2, d, "a", "dw_conv_out")
    ex_conv = _SplitExchange([parts_in, parts_out])
    started = ex_conv.start("rs_start_conv")
    (dh1, dgu0, dff0, act0, sg_f0), _ = _bwd_ffn(dh2, h1, ff0, gu0, small, wgu0, wd0, 0, _After(started))
    parts_d0, _ = _weight_grad(act0, dff0, fc, d // 2, "b", "dw_down0")
    parts_gu0, _ = _weight_grad(n0, dgu0, d // 2, fc, "a", "dw_gate_up0")
    ex_ffn0 = _SplitExchange([parts_d0.reshape(N_CHIPS, fq, d // 2), parts_gu0])
    started = ex_ffn0.start("rs_start_ffn0")
    (grad_x, dpool, sg_p), _ = _bwd_pool(dh1, x2, small, pool_scale, poolw, _After(started))
    g_pool = dpool.astype(BF16).reshape(2, ng // 2, N_CHIPS, gw // N_CHIPS, gw).transpose(2, 0, 1, 3, 4).reshape(
        N_CHIPS, 2, ng // 2 * (gw // N_CHIPS), gw)
    small_g = jnp.concatenate(
        [sg_p[0:2], sg_f0[0:2], sg_c[0:2], sg_f1[0:2], sg_c[2:5], sg_p[2:3],
         jnp.broadcast_to(loss_blk[0:1, 0:1], (1, d)), jnp.zeros((3, d), F32)], axis=0)
    land_p, small_all = _sibling_exchange([g_pool], small_g, "rs_sibling_pool")
    ex_pool = _SplitExchange([_add_sibling(g_pool, land_p, core_arr)])
    started = ex_pool.start("rs_start_pool")

    def update(w, g, m, v, name):
        flat = (-1, w.shape[-1])
        dl, m2, v2, g2 = _adamw(w.reshape(flat), g.reshape(flat), m.reshape(flat), v.reshape(flat), "adamw_" + name)
        return tuple(o.reshape(w.shape) for o in (g2, dl, m2, v2))

    (parts_d1, parts_gu1), (recv_d1, recv_gu1) = ex_ffn1.wait(started, "rs_wait_ffn1")
    (parts_in, parts_out), (recv_in, recv_out) = ex_conv.wait(started, "rs_wait_conv")
    gs_gu = _add_chips(parts_gu1, recv_gu1, where_arr, 1, 2)
    gs_d = _add_chips(parts_d1, recv_d1, where_arr, 1, 2, col_half=True)
    gs_in = _add_chips(parts_in, recv_in, where_arr)
    gs_out = _add_chips(parts_out, recv_out, where_arr)
    full_in, full_out = _sibling_share([gs_in, gs_out], [False, False], "rs_share_conv")
    up_in = update(conv_in_w, full_in.reshape(1, d, -1), m_conv_in_w, v_conv_in_w, "conv_in")
    up_out = update(conv_out_w, full_out.reshape(1, dq, d), m_conv_out_w, v_conv_out_w, "conv_out")
    (parts_d0, parts_gu0), (recv_d0, recv_gu0) = ex_ffn0.wait(up_out[1], "rs_wait_ffn0")
    (parts_p,), (recv_p,) = ex_pool.wait(up_out[1], "rs_wait_pool")
    gs_gu = _add_chips(parts_gu0, recv_gu0, where_arr, 0, 2, gs_gu)
    gs_d = _add_chips(parts_d0, recv_d0, where_arr, 0, 2, gs_d, col_half=True)
    gs_pool = _add_chips(parts_p, recv_p, where_arr)
    full_gu, full_d, full_pool = _sibling_share([gs_gu, gs_d, gs_pool], [False, True, False], "rs_share_ffn")
    small_sum = _sum_small(small_all)
    loss = small_sum[12, 0]
    mine = lax.dynamic_slice_in_dim(small_sum, chip * dq, dq, axis=1)

    ups = [
        update(norm_gains, mine[0:8].reshape(2, 4, dq), m_norm_gains, v_norm_gains, "gains"),
        update(pool_w, full_pool.reshape(1, ng, gw // N_CHIPS, gw), m_pool_w, v_pool_w, "pool_w"),
        update(pool_scale, small_sum[11:12], m_pool_scale, v_pool_scale, "pool_scale"),
        up_in,
        update(conv_w, mine[8:11].reshape(1, 3, dq), m_conv_w, v_conv_w, "taps"),
        up_out,
        update(ffn_gate_up_w, full_gu.reshape(2, d, fc), m_ffn_gate_up_w, v_ffn_gate_up_w, "gate_up"),
        update(ffn_down_w, full_d.reshape(2, fq, d), m_ffn_down_w, v_ffn_down_w, "down"),
    ]
    grads_out, deltas, new_ms, new_vs = (list(col) for col in zip(*ups))
    return (loss, grad_x[None], *grads_out, *deltas, *new_ms, *new_vs)
```

```python
import jax
import jax.numpy as jnp
from jax import lax
from jax.experimental import pallas as pl
from jax.experimental.pallas import tpu as pltpu

RMS_EPS = 1e-6
POOL_WINDOWS = (2, 4, 8, 16)
POOL_HALO = 16
CONV_HALO = 8
N_CHIPS = 4
N_DEV = 8
ADAM_LR = 0.001
ADAM_B1 = 0.9
ADAM_B2 = 0.999
ADAM_EPS = 1e-08
ADAM_WD = 0.01
ADAM_STEP = 10
VMEM_LIMIT = 56 * 2**20
STREAM_BUDGET = 24 * 2**20
MESH = pl.DeviceIdType.MESH
ANY = pl.BlockSpec(memory_space=pl.ANY)
DMA = pltpu.SemaphoreType.DMA
BF16 = jnp.bfloat16
F32 = jnp.float32


def _token_tile(t, rows=512):
    return min(rows, t)


def _rms(x):
    r = lax.rsqrt(jnp.mean(x * x, axis=-1, keepdims=True) + RMS_EPS)
    return x * r, r


def _rms_bwd(dy, xh, r, g):
    a = dy * g
    return r * (a - xh * jnp.mean(a * xh, axis=-1, keepdims=True))


def _dot(a, b):
    return jnp.dot(a, b, preferred_element_type=F32)


def _dot_nt(a, b):
    return lax.dot_general(a, b, (((1,), (1,)), ((), ())), preferred_element_type=F32)


def _dot_tn(a, b):
    return lax.dot_general(a, b, (((0,), (0,)), ((), ())), preferred_element_type=F32)


def _colsum(a):
    return jnp.sum(a, axis=0, keepdims=True)


def _resident(block, index_map):
    return pl.BlockSpec(block, index_map, pipeline_mode=pl.Buffered(1))


def _row_block(r, row_bytes):
    best = None
    for rb in range(16, r + 1, 16):
        if r % rb == 0 and rb * row_bytes <= STREAM_BUDGET:
            best = rb
    return best if best is not None else r


def _place():
    x, y, c = lax.axis_index("x"), lax.axis_index("y"), lax.axis_index("c")
    return x, y, c, 2 * x + y


def _dev(chip, core):
    return (chip // 2, chip % 2, core)


def _remote(src, dst, send_sem, recv_sem, device):
    return pltpu.make_async_remote_copy(src_ref=src, dst_ref=dst, send_sem=send_sem, recv_sem=recv_sem,
                                        device_id=device, device_id_type=MESH)


class _Gather:
    def __init__(self, shards):
        n = len(shards)
        self.args = [s for s, _ in shards]
        self.layers = [l for _, l in shards]
        self.out_shape = [jax.ShapeDtypeStruct((N_CHIPS,) + s.shape[1:], s.dtype) for s in self.args]
        self.sems = [DMA((n,)), DMA((n,)), DMA((n, 3)), DMA((n, 3)), DMA((n, 3)), DMA((n, 3))]

    def _own(self, loc, out, sems, a):
        x, y, c, k = _place()
        return _remote(loc[a].at[self.layers[a]], out[a].at[k], sems[0].at[a], sems[1].at[a], (x, y, 1 - c))

    def _ici(self, loc, out, sems, a, m, arrival):
        x, y, c, k = _place()
        dst = out[a].at[k ^ m, c] if arrival else out[a].at[k, c]
        return _remote(loc[a].at[self.layers[a], c], dst, sems[2].at[a, m - 1], sems[3].at[a, m - 1], _dev(k ^ m, c))

    def _forward(self, out, sems, a, m, arrival):
        x, y, c, k = _place()
        got = out[a].at[k ^ m, 1 - c] if arrival else out[a].at[k ^ m, c]
        return _remote(got, got, sems[4].at[a, m - 1], sems[5].at[a, m - 1], (x, y, 1 - c))

    def start(self, loc, out, sems):
        for a in range(len(self.args)):
            for m in range(1, N_CHIPS):
                self._ici(loc, out, sems, a, m, False).start()
            self._own(loc, out, sems, a).start()

    def finish(self, loc, out, sems):
        n = len(self.args)
        for a in range(n):
            for m in range(1, N_CHIPS):
                self._ici(loc, out, sems, a, m, True).wait_recv()
                self._forward(out, sems, a, m, False).start()
        for a in range(n):
            for m in range(1, N_CHIPS):
                self._forward(out, sems, a, m, True).wait_recv()
            self._own(loc, out, sems, a).wait_recv()
        for a in range(n):
            for m in range(1, N_CHIPS):
                self._ici(loc, out, sems, a, m, False).wait_send()
                self._forward(out, sems, a, m, False).wait_send()
            self._own(loc, out, sems, a).wait_send()


class _ChipExchange:
    def __init__(self, parts):
        n = len(parts)
        self.args = list(parts)
        self.out_shape = [jax.ShapeDtypeStruct((N_CHIPS - 1,) + p.shape[1:], p.dtype) for p in parts]
        self.sems = [DMA((n, 3)), DMA((n, 3))]

    def _copy(self, p, land, sems, a, m):
        x, y, c, k = _place()
        return _remote(p[a].at[k ^ m], land[a].at[m - 1], sems[0].at[a, m - 1], sems[1].at[a, m - 1], _dev(k ^ m, c))

    def start(self, p, land, sems):
        for a in range(len(self.args)):
            for m in range(1, N_CHIPS):
                self._copy(p, land, sems, a, m).start()

    def finish(self, p, land, sems):
        for a in range(len(self.args)):
            for m in range(1, N_CHIPS):
                self._copy(p, land, sems, a, m).wait_recv()
        for a in range(len(self.args)):
            for m in range(1, N_CHIPS):
                self._copy(p, land, sems, a, m).wait_send()


def _hosted(body, comm, *, name, grid, in_specs, out_specs, out_shape, args, scratch_shapes=()):
    ni, no, ns = len(in_specs), len(out_shape), len(scratch_shapes)
    if comm is None:
        res = pl.pallas_call(
            body, name=name, grid=grid, in_specs=list(in_specs), out_specs=list(out_specs), out_shape=list(out_shape),
            scratch_shapes=list(scratch_shapes),
            compiler_params=pltpu.CompilerParams(dimension_semantics=("arbitrary",) * len(grid), vmem_limit_bytes=VMEM_LIMIT),
        )(*args)
        return list(res), []
    nc, nco = len(comm.args), len(comm.out_shape)

    def full(*refs):
        cin = refs[ni:ni + nc]
        outs = refs[ni + nc:ni + nc + no]
        cout = refs[ni + nc + no:ni + nc + no + nco]
        scratch = refs[ni + nc + no + nco:ni + nc + no + nco + ns]
        csems = refs[ni + nc + no + nco + ns:]
        first = _all_of([pl.program_id(ax) == 0 for ax in range(len(grid))])
        last = _all_of([pl.program_id(ax) == grid[ax] - 1 for ax in range(len(grid))])

        @pl.when(first)
        def _():
            comm.start(cin, cout, csems)

        body(*refs[:ni], *outs, *scratch)

        @pl.when(last)
        def _():
            comm.finish(cin, cout, csems)

    res = pl.pallas_call(
        full, name=name, grid=grid, in_specs=[*in_specs, *[ANY] * nc], out_specs=[*out_specs, *[ANY] * nco],
        out_shape=[*out_shape, *comm.out_shape], scratch_shapes=[*scratch_shapes, *comm.sems],
        compiler_params=pltpu.CompilerParams(dimension_semantics=("arbitrary",) * len(grid), vmem_limit_bytes=VMEM_LIMIT,
                                             has_side_effects=True),
    )(*args, *comm.args)
    return list(res[:no]), list(res[no:])


def _all_of(conds):
    out = conds[0]
    for c in conds[1:]:
        out = jnp.logical_and(out, c)
    return out


class _After:
    def __init__(self, token):
        self.args, self.out_shape, self.sems = [token], [], []

    def start(self, *_):
        pass

    def finish(self, *_):
        pass


def _alone(comm, name):
    return _hosted(lambda: None, comm, name=name, grid=(1,), in_specs=[], out_specs=[], out_shape=[], args=[])[1]


HBM = pl.BlockSpec(memory_space=pltpu.HBM)
SEM = pl.BlockSpec(memory_space=pltpu.SEMAPHORE)
DATAFLOW = pltpu.SideEffectType.DATAFLOW_SIDE_EFFECTING


class _SplitGather:
    PER_ARRAY = 8

    def __init__(self, shards):
        self.plan = _Gather(shards)
        self.n = len(shards)

    @staticmethod
    def _tables(sems_of):
        class Table:
            def __init__(self, pick):
                self.pick = pick

            @property
            def at(self):
                return self

            def __getitem__(self, idx):
                return self.pick(idx)

        return [Table(lambda a: sems_of[a][0]), Table(lambda a: sems_of[a][1]),
                Table(lambda am: sems_of[am[0]][2 + am[1]]), Table(lambda am: sems_of[am[0]][5 + am[1]])]

    def start(self):
        n, plan, per = self.n, self.plan, self.PER_ARRAY

        def body(*refs):
            loc, land = refs[:n], refs[n:2 * n]
            sems_of = {a: refs[2 * n + per * a:2 * n + per * (a + 1)] for a in range(n)}
            plan.start(loc, land, self._tables(sems_of))
            refs[-1][...] = jnp.zeros_like(refs[-1])

        lands = [pltpu.with_memory_space_constraint(lax.empty(o.shape, o.dtype), pltpu.HBM) for o in plan.out_shape]
        locs = [pltpu.with_memory_space_constraint(a, pltpu.HBM) for a in plan.args]
        res = pl.pallas_call(
            body, name="ag_start",
            out_shape=[*[DMA(())] * (per * n),
                       *[pltpu.HBM(a.shape, a.dtype) for a in plan.args],
                       *[pltpu.HBM(o.shape, o.dtype) for o in plan.out_shape],
                       jax.ShapeDtypeStruct((8, 128), F32)],
            in_specs=[HBM] * (2 * n),
            out_specs=[SEM] * (per * n) + [HBM] * (2 * n) + [pl.BlockSpec(memory_space=pltpu.VMEM)],
            input_output_aliases={i: per * n + i for i in range(2 * n)},
            compiler_params=pltpu.CompilerParams(has_side_effects=DATAFLOW),
        )(*locs, *lands)
        self.sems = {a: list(res[per * a:per * (a + 1)]) for a in range(n)}
        self.locs = list(res[per * n:per * n + n])
        self.lands = list(res[per * n + n:per * n + 2 * n])
        self.token = res[-1]

    def wait(self, idxs, after, name):
        plan, g, per = self.plan, len(idxs), self.PER_ARRAY

        def body(*refs):
            loc = {a: refs[j] for j, a in enumerate(idxs)}
            land = {a: refs[g + j] for j, a in enumerate(idxs)}
            sems = self._tables({a: refs[2 * g + per * j:2 * g + per * (j + 1)] for j, a in enumerate(idxs)})
            for a in idxs:
                for m in range(1, N_CHIPS):
                    plan._ici(loc, land, sems, a, m, True).wait_recv()
                    plan._ici(loc, land, sems, a, m, False).wait_send()
                plan._own(loc, land, sems, a).wait_recv()
                plan._own(loc, land, sems, a).wait_send()

        res = pl.pallas_call(
            body, name=name,
            out_shape=[pltpu.HBM(self.lands[a].shape, self.lands[a].dtype) for a in idxs],
            in_specs=[HBM] * (2 * g) + [SEM] * (per * g) + [pl.BlockSpec(memory_space=pl.ANY)], out_specs=[HBM] * g,
            input_output_aliases={g + j: j for j in range(g)},
            compiler_params=pltpu.CompilerParams(has_side_effects=DATAFLOW),
        )(*[self.locs[a] for a in idxs], *[self.lands[a] for a in idxs],
          *[s for a in idxs for s in self.sems[a]], after)
        return list(res)


class _SplitExchange:
    PER_ARRAY = 6

    def __init__(self, parts):
        self.plan = _ChipExchange(parts)
        self.n = len(parts)

    @staticmethod
    def _tables(sems_of):
        class Table:
            def __init__(self, pick):
                self.pick = pick

            @property
            def at(self):
                return self

            def __getitem__(self, am):
                return self.pick(am)

        return [Table(lambda am: sems_of[am[0]][am[1]]), Table(lambda am: sems_of[am[0]][3 + am[1]])]

    def start(self, name):
        n, plan, per = self.n, self.plan, self.PER_ARRAY

        def body(*refs):
            p, land = refs[:n], refs[n:2 * n]
            sems_of = {a: refs[2 * n + per * a:2 * n + per * (a + 1)] for a in range(n)}
            plan.start(p, land, self._tables(sems_of))
            refs[-1][...] = jnp.zeros_like(refs[-1])

        lands = [pltpu.with_memory_space_constraint(lax.empty(o.shape, o.dtype), pltpu.HBM) for o in plan.out_shape]
        parts = [pltpu.with_memory_space_constraint(a, pltpu.HBM) for a in plan.args]
        res = pl.pallas_call(
            body, name=name,
            out_shape=[*[DMA(())] * (per * n),
                       *[pltpu.HBM(a.shape, a.dtype) for a in plan.args],
                       *[pltpu.HBM(o.shape, o.dtype) for o in plan.out_shape],
                       jax.ShapeDtypeStruct((8, 128), F32)],
            in_specs=[HBM] * (2 * n),
            out_specs=[SEM] * (per * n) + [HBM] * (2 * n) + [pl.BlockSpec(memory_space=pltpu.VMEM)],
            input_output_aliases={i: per * n + i for i in range(2 * n)},
            compiler_params=pltpu.CompilerParams(has_side_effects=DATAFLOW),
        )(*parts, *lands)
        self.sems = list(res[:per * n])
        self.parts = list(res[per * n:per * n + n])
        self.lands = list(res[per * n + n:per * n + 2 * n])
        return res[-1]

    def wait(self, after, name):
        n, plan, per = self.n, self.plan, self.PER_ARRAY

        def body(*refs):
            p, land = refs[:n], refs[n:2 * n]
            sems_of = {a: refs[2 * n + per * a:2 * n + per * (a + 1)] for a in range(n)}
            plan.finish(p, land, self._tables(sems_of))

        res = pl.pallas_call(
            body, name=name,
            out_shape=[*[pltpu.HBM(a.shape, a.dtype) for a in self.parts], *[pltpu.HBM(a.shape, a.dtype) for a in self.lands]],
            in_specs=[HBM] * (2 * n) + [SEM] * (per * n) + [pl.BlockSpec(memory_space=pl.ANY)], out_specs=[HBM] * (2 * n),
            input_output_aliases={i: i for i in range(2 * n)},
            compiler_params=pltpu.CompilerParams(has_side_effects=DATAFLOW),
        )(*self.parts, *self.lands, *self.sems, after)
        return list(res[:n]), list(res[n:])


PASS_ON_BARRIER = 1


def _sibling_barrier():
    x, y, c, _ = _place()
    barrier = pltpu.get_barrier_semaphore()
    pl.semaphore_signal(barrier, inc=1, device_id=(x, y, 1 - c), device_id_type=MESH)
    pl.semaphore_wait(barrier, 1)


def _pass_on(lands, name):
    n = len(lands)

    def body(*refs):
        out = refs[n:2 * n]
        send_sems, recv_sems = refs[2 * n:]
        x, y, c, k = _place()
        _sibling_barrier()
        cps = []
        for a in range(n):
            for m in range(1, N_CHIPS):
                got = out[a].at[k ^ m, c]
                cp = _remote(got, got, send_sems.at[a, m - 1], recv_sems.at[a, m - 1], (x, y, 1 - c))
                cp.start()
                cps.append(cp)
        for a in range(n):
            for m in range(1, N_CHIPS):
                theirs = out[a].at[k ^ m, 1 - c]
                _remote(theirs, theirs, send_sems.at[a, m - 1], recv_sems.at[a, m - 1], (x, y, 1 - c)).wait_recv()
        for cp in cps:
            cp.wait_send()

    return pl.pallas_call(
        body, name=name, out_shape=[jax.ShapeDtypeStruct(a.shape, a.dtype) for a in lands],
        in_specs=[ANY] * n, out_specs=[ANY] * n, input_output_aliases={a: a for a in range(n)},
        scratch_shapes=[DMA((n, 3)), DMA((n, 3))],
        compiler_params=pltpu.CompilerParams(has_side_effects=True, collective_id=PASS_ON_BARRIER),
    )(*lands)


def _sibling_exchange(grads, small, name, after=None):
    n = len(grads)
    ns = 0 if small is None else 1
    na = 0 if after is None else 1

    def body(*refs):
        g = refs[:n]
        land = refs[n + ns + na:2 * n + ns + na]
        send_sems, recv_sems, own_sem, ssend_sems, srecv_sems = refs[2 * n + 2 * ns + na:]
        x, y, c, k = _place()
        me = 2 * k + c
        cps = []
        for a in range(n):
            cp = _remote(g[a].at[:, pl.ds(1 - c, 1)], land[a], send_sems.at[a], recv_sems.at[a], (x, y, 1 - c))
            cp.start()
            cps.append(cp)
        if small is not None:
            sm, smg = refs[n], refs[2 * n + 1 + na]
            peers = [me ^ m for m in range(1, N_DEV)]
            ids = [(p // 4, (p // 2) % 2, p % 2) for p in peers]
            own = pltpu.make_async_copy(sm, smg.at[me], own_sem)
            own.start()
            for m in range(1, N_DEV):
                cp = _remote(sm, smg.at[me], ssend_sems.at[m - 1], srecv_sems.at[m - 1], ids[m - 1])
                cp.start()
                cps.append(cp)
            for m in range(1, N_DEV):
                _remote(sm, smg.at[peers[m - 1]], ssend_sems.at[m - 1], srecv_sems.at[m - 1], ids[m - 1]).wait_recv()
            own.wait()
        for cp in cps[:n]:
            cp.wait_recv()
        for cp in cps:
            cp.wait_send()

    out_shape = [jax.ShapeDtypeStruct((N_CHIPS, 1) + a.shape[2:], a.dtype) for a in grads]
    ins = list(grads)
    if small is not None:
        out_shape.append(jax.ShapeDtypeStruct((N_DEV,) + small.shape, small.dtype))
        ins.append(small)
    if after is not None:
        ins.append(after)
    return pl.pallas_call(
        body, name=name, out_shape=out_shape, in_specs=[ANY] * (n + ns + na), out_specs=[ANY] * (n + ns),
        scratch_shapes=[DMA((max(n, 1),)), DMA((max(n, 1),)), DMA, DMA((N_DEV - 1,)), DMA((N_DEV - 1,))],
        compiler_params=pltpu.CompilerParams(has_side_effects=True),
    )(*ins)


def _sibling_share(halves, col_half, name):
    n = len(halves)

    def body(*refs):
        out = refs[n:2 * n]
        send_sems, recv_sems = refs[2 * n:]
        x, y, c, k = _place()

        def half(a, core):
            if not col_half[a]:
                return out[a].at[:, pl.ds(core, 1)]
            cols = out[a].shape[-1] // 2
            return out[a].at[:, :, pl.ds(pl.multiple_of(core * cols, cols), cols)]

        cps = []
        for a in range(n):
            cp = _remote(half(a, c), half(a, c), send_sems.at[a], recv_sems.at[a], (x, y, 1 - c))
            cp.start()
            cps.append(cp)
        for a in range(n):
            _remote(half(a, 1 - c), half(a, 1 - c), send_sems.at[a], recv_sems.at[a], (x, y, 1 - c)).wait_recv()
        for cp in cps:
            cp.wait_send()

    out_shape = [jax.ShapeDtypeStruct(a.shape, a.dtype) for a in halves]
    return pl.pallas_call(
        body, name=name, out_shape=out_shape, in_specs=[ANY] * n, out_specs=[ANY] * n,
        input_output_aliases={a: a for a in range(n)}, scratch_shapes=[DMA((n,)), DMA((n,))],
        compiler_params=pltpu.CompilerParams(has_side_effects=True),
    )(*halves)


def _add_sibling(g, land, core):
    _, _, r, c = g.shape
    rb = _row_block(r, c * (3 * 2 * 2 + 2 * 4))

    def body(core_ref, g_ref, l_ref, o_ref):
        o_ref[...] = (g_ref[...].astype(F32) + l_ref[...].astype(F32)).astype(o_ref.dtype)

    return pl.pallas_call(
        body, name="rs_add_sibling", out_shape=jax.ShapeDtypeStruct((N_CHIPS, r, c), g.dtype),
        grid_spec=pltpu.PrefetchScalarGridSpec(
            num_scalar_prefetch=1, grid=(N_CHIPS, r // rb),
            in_specs=[pl.BlockSpec((None, None, rb, c), lambda j, i, core_ref: (j, core_ref[0], i, 0)),
                      pl.BlockSpec((None, None, rb, c), lambda j, i, core_ref: (j, 0, i, 0))],
            out_specs=pl.BlockSpec((None, rb, c), lambda j, i, core_ref: (j, i, 0))),
        compiler_params=pltpu.CompilerParams(dimension_semantics=("parallel", "parallel"), vmem_limit_bytes=VMEM_LIMIT),
    )(core, g, land)


def _add_chips(part, land, where, layer=0, n_layers=1, into=None, col_half=False, rows=(0, 1)):
    _, r, c = part.shape
    sub, n_sub = rows
    rb = _row_block(r, c * (4 * 2 * 2 + 4 * 2 + 2 * 4))

    def body(where_ref, p_ref, l_ref, *rest):
        acc = p_ref[...].astype(F32)
        for m in range(N_CHIPS - 1):
            acc = acc + l_ref[m].astype(F32)
        rest[-1][...] = acc

    in_specs = [pl.BlockSpec((None, rb, c), lambda i, where_ref: (where_ref[0], i, 0)),
                pl.BlockSpec((N_CHIPS - 1, rb, c), lambda i, where_ref: (0, i, 0))]
    args = [where, part, land]
    if into is not None:
        in_specs.append(ANY)
        args.append(into)
    if col_half:
        out_shape = jax.ShapeDtypeStruct((n_layers, r, 2 * c), F32)
        out_spec = pl.BlockSpec((None, rb, c), lambda i, where_ref: (layer, i, where_ref[1]))
    else:
        out_shape = jax.ShapeDtypeStruct((n_layers, 2, n_sub * r, c), F32)
        out_spec = pl.BlockSpec((None, None, rb, c), lambda i, where_ref: (layer, where_ref[1], sub * (r // rb) + i, 0))
    return pl.pallas_call(
        body, name="rs_add_chips", out_shape=out_shape,
        grid_spec=pltpu.PrefetchScalarGridSpec(
            num_scalar_prefetch=1, grid=(r // rb,), in_specs=in_specs, out_specs=out_spec),
        input_output_aliases={} if into is None else {3: 0},
        compiler_params=pltpu.CompilerParams(dimension_semantics=("parallel",), vmem_limit_bytes=VMEM_LIMIT),
    )(*args)


def _sum_small(smg):
    def body(s_ref, o_ref):
        acc = s_ref[0]
        for j in range(1, N_DEV):
            acc = acc + s_ref[j]
        o_ref[...] = acc

    return pl.pallas_call(body, name="rs_sum_small", out_shape=jax.ShapeDtypeStruct(smg.shape[1:], F32))(smg)


def _pool_windows(ext_ref, g, gw, tm, first_row):
    w = POOL_WINDOWS[g]
    slab = ext_ref[:, g * gw:(g + 1) * gw]
    p, k = slab, 1
    while k < w:
        p = p + pltpu.roll(p, k, 0)
        k *= 2
    t = first_row + lax.broadcasted_iota(jnp.int32, (tm, 1), 0)
    cnt = jnp.minimum(t + 1, w).astype(F32)
    return p[POOL_HALO:] / cnt - slab[POOL_HALO:]


def _fwd_pool(x, small, scale, poolw, comm=None):
    t, d = x.shape
    tm = _token_tile(t)
    gw = d // len(POOL_WINDOWS)

    def body(x_ref, sm_ref, sc_ref, w_ref, h_ref, ext_ref, mix_ref):
        i = pl.program_id(0)

        @pl.when(i == 0)
        def _():
            ext_ref[0:POOL_HALO, :] = jnp.zeros((POOL_HALO, d), F32)

        @pl.when(i > 0)
        def _():
            ext_ref[0:POOL_HALO, :] = ext_ref[tm:tm + POOL_HALO, :]

        xv = x_ref[...]
        xh, _ = _rms(xv)
        ext_ref[POOL_HALO:, :] = xh * sm_ref[0:1, :]
        for g in range(len(POOL_WINDOWS)):
            pooled = _pool_windows(ext_ref, g, gw, tm, i * tm)
            cols = slice(g * gw, (g + 1) * gw)
            mix_ref[:, cols] = _dot(pooled.astype(BF16), w_ref[g]) * sc_ref[:, cols]
        mh, _ = _rms(mix_ref[...])
        h_ref[...] = xv + mh * sm_ref[1:2, :]

    (h,), got = _hosted(
        body, comm, name="fwd_pool", grid=(t // tm,), out_shape=[jax.ShapeDtypeStruct((t, d), F32)],
        in_specs=[pl.BlockSpec((tm, d), lambda i: (i, 0)), _resident(small.shape, lambda i: (0, 0)),
                  _resident(scale.shape, lambda i: (0, 0)), _resident(poolw.shape, lambda i: (0, 0, 0))],
        out_specs=[pl.BlockSpec((tm, d), lambda i: (i, 0))],
        scratch_shapes=[pltpu.VMEM((POOL_HALO + tm, d), F32), pltpu.VMEM((tm, d), F32)],
        args=[x, small, scale, poolw])
    return h, got


def _fwd_ffn(h, small, wgu, wd, layer, comm=None, target=None):
    t, d = h.shape
    tm = _token_tile(t)
    steps = t // tm
    fc = wgu.shape[-1]
    f = 2 * fc
    g_in, g_out = 4 * layer + 2, 4 * layer + 3
    with_loss = target is not None

    def body(h_ref, *refs):
        if with_loss:
            t_ref, sm_ref, wgu_ref, wd_ref, o_ref, gu_ref, ff_ref, n_ref, l_ref, acc_ref = refs
        else:
            sm_ref, wgu_ref, wd_ref, o_ref, gu_ref, ff_ref, n_ref = refs
        hv = h_ref[...]
        hh, _ = _rms(hv)
        n = (hh * sm_ref[g_in:g_in + 1, :]).astype(BF16)
        n_ref[...] = n
        ff = None
        for j in range(2):
            gate = _dot(n, wgu_ref[j])
            up = _dot(n, wgu_ref[2 + j])
            gu_ref[:, j * fc:(j + 1) * fc] = gate.astype(BF16)
            gu_ref[:, f + j * fc:f + (j + 1) * fc] = up.astype(BF16)
            act = (gate * jax.nn.sigmoid(gate) * up).astype(BF16)
            part = _dot(act, wd_ref[j * fc:(j + 1) * fc, :])
            ff = part if ff is None else ff + part
        ff_ref[...] = ff
        fh, _ = _rms(ff)
        out = hv + fh * sm_ref[g_out:g_out + 1, :]
        if not with_loss:
            o_ref[...] = out
            return
        i = pl.program_id(0)
        e = out - t_ref[...]
        o_ref[...] = e * (1.0 / d)

        @pl.when(i == 0)
        def _():
            acc_ref[...] = jnp.zeros_like(acc_ref)

        acc_ref[...] += _colsum(e * e)

        @pl.when(i == steps - 1)
        def _():
            l_ref[...] = jnp.full(l_ref.shape, 0.5 / d, F32) * jnp.sum(acc_ref[...])

    row = lambda i: (i, 0)
    out_shape = [jax.ShapeDtypeStruct((t, d), F32), jax.ShapeDtypeStruct((t, 2 * f), BF16),
                 jax.ShapeDtypeStruct((t, d), F32), jax.ShapeDtypeStruct((t, d), BF16)]
    out_specs = [pl.BlockSpec((tm, d), row), pl.BlockSpec((tm, 2 * f), row), pl.BlockSpec((tm, d), row),
                 pl.BlockSpec((tm, d), row)]
    weight_specs = [_resident(small.shape, lambda i: (0, 0)), _resident(wgu.shape, lambda i: (0, 0, 0)),
                    _resident(wd.shape, lambda i: (0, 0))]
    if with_loss:
        return _hosted(
            body, comm, name=f"fwd_ffn{layer}_loss", grid=(steps,),
            out_shape=out_shape + [jax.ShapeDtypeStruct((8, 128), F32)],
            in_specs=[pl.BlockSpec((tm, d), row), pl.BlockSpec((tm, d), row)] + weight_specs,
            out_specs=out_specs + [pl.BlockSpec((8, 128), lambda i: (0, 0))],
            scratch_shapes=[pltpu.VMEM((1, d), F32)], args=[h, target, small, wgu, wd])
    return _hosted(
        body, comm, name=f"fwd_ffn{layer}", grid=(steps,), out_shape=out_shape,
        in_specs=[pl.BlockSpec((tm, d), row)] + weight_specs, out_specs=out_specs, args=[h, small, wgu, wd])


def _fwd_conv(h, small, win, wout, comm=None):
    t, d = h.shape
    tm = _token_tile(t)
    pc = win.shape[-1]

    def body(h_ref, sm_ref, win_ref, wout_ref, o_ref, proj_ref, y_ref, n_ref, pj_ref, uext_ref):
        i = pl.program_id(0)

        @pl.when(i == 0)
        def _():
            uext_ref[0:CONV_HALO, :] = jnp.zeros((CONV_HALO, d), F32)

        @pl.when(i > 0)
        def _():
            uext_ref[0:CONV_HALO, :] = uext_ref[tm:tm + CONV_HALO, :]

        hv = h_ref[...]
        hh, _ = _rms(hv)
        n = (hh * sm_ref[4:5, :]).astype(BF16)
        n_ref[...] = n
        for k in range(N_CHIPS):
            pj_ref[:, k * pc:(k + 1) * pc] = _dot(n, win_ref[k])
        proj_ref[...] = pj_ref[...].astype(BF16)
        uext_ref[CONV_HALO:, :] = pj_ref[:, d:2 * d] * pj_ref[:, 2 * d:]
        taps = [sm_ref[8 + j:9 + j, :] for j in range(3)]
        full = uext_ref[...]
        conv = (full[CONV_HALO:] * taps[2] + pltpu.roll(full, 1, 0)[CONV_HALO:] * taps[1]
                + pltpu.roll(full, 2, 0)[CONV_HALO:] * taps[0])
        y = _dot((pj_ref[:, 0:d] * conv).astype(BF16), wout_ref[...])
        y_ref[...] = y
        yh, _ = _rms(y)
        o_ref[...] = hv + yh * sm_ref[5:6, :]

    row = lambda i: (i, 0)
    return _hosted(
        body, comm, name="fwd_conv", grid=(t // tm,),
        out_shape=[jax.ShapeDtypeStruct((t, d), F32), jax.ShapeDtypeStruct((t, 3 * d), BF16),
                   jax.ShapeDtypeStruct((t, d), F32), jax.ShapeDtypeStruct((t, d), BF16)],
        in_specs=[pl.BlockSpec((tm, d), row), _resident(small.shape, lambda i: (0, 0)),
                  _resident(win.shape, lambda i: (0, 0, 0)), _resident(wout.shape, lambda i: (0, 0))],
        out_specs=[pl.BlockSpec((tm, d), row), pl.BlockSpec((tm, 3 * d), row), pl.BlockSpec((tm, d), row),
                   pl.BlockSpec((tm, d), row)],
        scratch_shapes=[pltpu.VMEM((tm, 3 * d), F32), pltpu.VMEM((CONV_HALO + tm, d), F32)],
        args=[h, small, win, wout])


def _bwd_ffn(dh, h, ff, gu, small, wgu, wd, layer, comm=None):
    t, d = h.shape
    tm = _token_tile(t, 256)
    fc = wgu.shape[-1]
    f = 2 * fc
    g_in, g_out = 4 * layer + 2, 4 * layer + 3

    def body(dh_ref, h_ref, ff_ref, gu_ref, sm_ref, wgu_ref, wd_ref, o_ref, dgu_ref, dff_ref, act_ref, sg_ref):
        i = pl.program_id(0)

        @pl.when(i == 0)
        def _():
            sg_ref[...] = jnp.zeros_like(sg_ref)

        dy = dh_ref[...]
        fh, r3 = _rms(ff_ref[...])
        sg_ref[1:2, :] += _colsum(dy * fh)
        dff = _rms_bwd(dy, fh, r3, sm_ref[g_out:g_out + 1, :]).astype(BF16)
        dff_ref[...] = dff
        for j in range(2):
            dact = _dot_nt(dff, wd_ref[j * fc:(j + 1) * fc, :])
            gate = gu_ref[:, j * fc:(j + 1) * fc].astype(F32)
            up = gu_ref[:, f + j * fc:f + (j + 1) * fc].astype(F32)
            sig = jax.nn.sigmoid(gate)
            silu = gate * sig
            act_ref[:, j * fc:(j + 1) * fc] = (silu * up).astype(BF16)
            dgu_ref[:, j * fc:(j + 1) * fc] = (dact * up * (sig * (1.0 + gate * (1.0 - sig)))).astype(BF16)
            dgu_ref[:, f + j * fc:f + (j + 1) * fc] = (dact * silu).astype(BF16)
        dn = None
        for k in range(N_CHIPS):
            part = _dot_nt(dgu_ref[:, k * fc:(k + 1) * fc], wgu_ref[k])
            dn = part if dn is None else dn + part
        hh, r2 = _rms(h_ref[...])
        sg_ref[0:1, :] += _colsum(dn * hh)
        o_ref[...] = dy + _rms_bwd(dn, hh, r2, sm_ref[g_in:g_in + 1, :])

    row = lambda i: (i, 0)
    return _hosted(
        body, comm, name=f"bwd_ffn{layer}", grid=(t // tm,),
        out_shape=[jax.ShapeDtypeStruct((t, d), F32), jax.ShapeDtypeStruct((t, 2 * f), BF16),
                   jax.ShapeDtypeStruct((t, d), BF16), jax.ShapeDtypeStruct((t, f), BF16),
                   jax.ShapeDtypeStruct((8, d), F32)],
        in_specs=[pl.BlockSpec((tm, d), row), pl.BlockSpec((tm, d), row), pl.BlockSpec((tm, d), row),
                  pl.BlockSpec((tm, 2 * f), row), _resident(small.shape, lambda i: (0, 0)),
                  _resident(wgu.shape, lambda i: (0, 0, 0)), _resident(wd.shape, lambda i: (0, 0))],
        out_specs=[pl.BlockSpec((tm, d), row), pl.BlockSpec((tm, 2 * f), row), pl.BlockSpec((tm, d), row),
                   pl.BlockSpec((tm, f), row), pl.BlockSpec((8, d), lambda i: (0, 0))],
        args=[dh, h, ff, gu, small, wgu, wd])


def _bwd_conv(dh, h, y, proj, small, win, wout, comm=None):
    t, d = h.shape
    tm = _token_tile(t)
    steps = t // tm
    pc = win.shape[-1]
    halo_blocks = tm // 16

    def body(dh_ref, h_ref, y_ref, proj_ref, halo_ref, sm_ref, win_ref, wout_ref,
             o_ref, dproj_ref, dy_ref, bc_ref, sg_ref, uext_ref, dcext_ref, carry_ref):
        i = pl.program_id(0)
        tile = steps - 1 - i

        @pl.when(i == 0)
        def _():
            sg_ref[...] = jnp.zeros_like(sg_ref)
            carry_ref[...] = jnp.zeros_like(carry_ref)

        dy = dh_ref[...]
        yh, r1 = _rms(y_ref[...])
        sg_ref[1:2, :] += _colsum(dy * yh)
        dyv = _rms_bwd(dy, yh, r1, sm_ref[5:6, :]).astype(BF16)
        dy_ref[...] = dyv
        dbc = _dot_nt(dyv, wout_ref[...])
        b = proj_ref[:, 0:d].astype(F32)
        cg = proj_ref[:, d:2 * d].astype(F32)
        v = proj_ref[:, 2 * d:].astype(F32)
        halo = halo_ref[...].astype(F32)[16 - CONV_HALO:]
        uh = halo[:, d:2 * d] * halo[:, 2 * d:]
        uext_ref[0:CONV_HALO, :] = jnp.where(tile > 0, uh, jnp.zeros_like(uh))
        uext_ref[CONV_HALO:, :] = cg * v
        taps = [sm_ref[8 + j:9 + j, :] for j in range(3)]
        full = uext_ref[...]
        u0 = full[CONV_HALO:]
        u1 = pltpu.roll(full, 1, 0)[CONV_HALO:]
        u2 = pltpu.roll(full, 2, 0)[CONV_HALO:]
        conv = u0 * taps[2] + u1 * taps[1] + u2 * taps[0]
        bc_ref[...] = (b * conv).astype(BF16)
        dconv = dbc * b
        sg_ref[4:5, :] += _colsum(dconv * u0)
        sg_ref[3:4, :] += _colsum(dconv * u1)
        sg_ref[2:3, :] += _colsum(dconv * u2)
        dcext_ref[0:tm, :] = dconv
        dcext_ref[tm:, :] = carry_ref[...]
        carry_ref[...] = dconv[0:CONV_HALO]
        dfull = dcext_ref[...]
        n8 = tm + CONV_HALO
        du = (dfull[0:tm] * taps[2] + pltpu.roll(dfull, n8 - 1, 0)[0:tm] * taps[1]
              + pltpu.roll(dfull, n8 - 2, 0)[0:tm] * taps[0])
        dproj_ref[:, 0:d] = (dbc * conv).astype(BF16)
        dproj_ref[:, d:2 * d] = (du * v).astype(BF16)
        dproj_ref[:, 2 * d:] = (du * cg).astype(BF16)
        dn = None
        for k in range(N_CHIPS):
            part = _dot_nt(dproj_ref[:, k * pc:(k + 1) * pc], win_ref[k])
            dn = part if dn is None else dn + part
        hh, r0 = _rms(h_ref[...])
        sg_ref[0:1, :] += _colsum(dn * hh)
        o_ref[...] = dy + _rms_bwd(dn, hh, r0, sm_ref[4:5, :])

    rev = lambda i: (steps - 1 - i, 0)
    before = lambda i: (jnp.maximum((steps - 1 - i) * halo_blocks - 1, 0), 0)
    return _hosted(
        body, comm, name="bwd_conv", grid=(steps,),
        out_shape=[jax.ShapeDtypeStruct((t, d), F32), jax.ShapeDtypeStruct((t, 3 * d), BF16),
                   jax.ShapeDtypeStruct((t, d), BF16), jax.ShapeDtypeStruct((t, d), BF16),
                   jax.ShapeDtypeStruct((8, d), F32)],
        in_specs=[pl.BlockSpec((tm, d), rev), pl.BlockSpec((tm, d), rev), pl.BlockSpec((tm, d), rev),
                  pl.BlockSpec((tm, 3 * d), rev), pl.BlockSpec((16, 3 * d), before),
                  _resident(small.shape, lambda i: (0, 0)), _resident(win.shape, lambda i: (0, 0, 0)),
                  _resident(wout.shape, lambda i: (0, 0))],
        out_specs=[pl.BlockSpec((tm, d), rev), pl.BlockSpec((tm, 3 * d), rev), pl.BlockSpec((tm, d), rev),
                   pl.BlockSpec((tm, d), rev), pl.BlockSpec((8, d), lambda i: (0, 0))],
        scratch_shapes=[pltpu.VMEM((CONV_HALO + tm, d), F32), pltpu.VMEM((tm + CONV_HALO, d), F32),
                        pltpu.VMEM((CONV_HALO, d), F32)],
        args=[dh, h, y, proj, proj, small, win, wout])


def _bwd_pool(dh, x, small, scale, poolw, comm=None):
    t, d = x.shape
    tm = _token_tile(t)
    steps = t // tm
    ng = len(POOL_WINDOWS)
    gw = d // ng
    halo_blocks = tm // POOL_HALO

    def body(dh_ref, x_ref, halo_ref, sm_ref, sc_ref, w_ref, o_ref, dw_ref, sg_ref,
             ext_ref, mix_ref, mm_ref, pb_ref, qext_ref, dhn_ref, carry_ref):
        i = pl.program_id(0)
        tile = steps - 1 - i

        @pl.when(i == 0)
        def _():
            sg_ref[...] = jnp.zeros_like(sg_ref)
            dw_ref[...] = jnp.zeros_like(dw_ref)
            carry_ref[...] = jnp.zeros_like(carry_ref)

        g0 = sm_ref[0:1, :]
        xv = x_ref[...]
        xh, r0 = _rms(xv)
        hx, _ = _rms(halo_ref[...])
        ext_ref[0:POOL_HALO, :] = jnp.where(tile > 0, hx * g0, jnp.zeros_like(hx))
        ext_ref[POOL_HALO:, :] = xh * g0
        for g in range(ng):
            pooled = _pool_windows(ext_ref, g, gw, tm, tile * tm)
            cols = slice(g * gw, (g + 1) * gw)
            pb = pooled.astype(BF16)
            pb_ref[:, cols] = pb
            mm = _dot(pb, w_ref[g])
            mm_ref[:, cols] = mm
            mix_ref[:, cols] = mm * sc_ref[:, cols]
        dy = dh_ref[...]
        mh, r1 = _rms(mix_ref[...])
        sg_ref[1:2, :] += _colsum(dy * mh)
        dmix = _rms_bwd(dy, mh, r1, sm_ref[1:2, :])
        sg_ref[2:3, :] += _colsum(dmix * mm_ref[...])
        mix_ref[...] = dmix * sc_ref[...]
        n16 = tm + POOL_HALO
        for g in range(ng):
            w = POOL_WINDOWS[g]
            cols = slice(g * gw, (g + 1) * gw)
            dmm = mix_ref[:, cols].astype(BF16)
            dpooled = _dot_nt(dmm, w_ref[g])
            dw_ref[g] += _dot_tn(pb_ref[:, cols], dmm)
            trow = tile * tm + lax.broadcasted_iota(jnp.int32, (tm, 1), 0)
            q = dpooled / jnp.minimum(trow + 1, w).astype(F32)
            qext_ref[0:tm, cols] = q
            qext_ref[tm:, cols] = carry_ref[:, cols]
            carry_ref[:, cols] = q[0:POOL_HALO]
            p, k = qext_ref[:, cols], 1
            while k < w:
                p = p + pltpu.roll(p, n16 - k, 0)
                k *= 2
            dhn_ref[:, cols] = p[0:tm] - dpooled
        dhn = dhn_ref[...]
        sg_ref[0:1, :] += _colsum(dhn * xh)
        o_ref[...] = dy + _rms_bwd(dhn, xh, r0, g0)

    rev = lambda i: (steps - 1 - i, 0)
    before = lambda i: (jnp.maximum((steps - 1 - i) * halo_blocks - 1, 0), 0)
    return _hosted(
        body, comm, name="bwd_pool", grid=(steps,),
        out_shape=[jax.ShapeDtypeStruct((t, d), F32), jax.ShapeDtypeStruct((ng, gw, gw), F32),
                   jax.ShapeDtypeStruct((8, d), F32)],
        in_specs=[pl.BlockSpec((tm, d), rev), pl.BlockSpec((tm, d), rev), pl.BlockSpec((POOL_HALO, d), before),
                  _resident(small.shape, lambda i: (0, 0)), _resident(scale.shape, lambda i: (0, 0)),
                  _resident(poolw.shape, lambda i: (0, 0, 0))],
        out_specs=[pl.BlockSpec((tm, d), rev), pl.BlockSpec((ng, gw, gw), lambda i: (0, 0, 0)),
                   pl.BlockSpec((8, d), lambda i: (0, 0))],
        scratch_shapes=[pltpu.VMEM((POOL_HALO + tm, d), F32), pltpu.VMEM((tm, d), F32), pltpu.VMEM((tm, d), F32),
                        pltpu.VMEM((tm, d), BF16), pltpu.VMEM((tm + POOL_HALO, d), F32), pltpu.VMEM((tm, d), F32),
                        pltpu.VMEM((POOL_HALO, d), F32)],
        args=[dh, x, x, small, scale, poolw])


def _weight_grad(a, b, bm, bn, half_on, name, comm=None, rows=(0, 1)):
    t, m = a.shape
    _, n = b.shape
    if half_on == "a":
        a_cols, b_cols = 2 * bm, bn
    else:
        a_cols, b_cols = bm, 2 * bn
    steps = max(m // a_cols, n // b_cols)
    sub, n_sub = rows
    tr = bm // n_sub

    def spec(cols, total):
        if cols == total:
            return _resident((t, cols), lambda p, j: (0, 0))
        return pl.BlockSpec((t, cols), lambda p, j: (0, j))

    def tile(a_ref, b_ref, half):
        if half_on == "a":
            first = half * bm + sub * tr
            return _dot_tn(a_ref[:, first:first + tr], b_ref[...])
        return _dot_tn(a_ref[...], b_ref[:, half * bn:(half + 1) * bn])

    def body(a_ref, b_ref, parts_ref, land_ref, acc_ref, stage_ref, got_ref, send_sems, recv_sems, got_sem):
        p, j = pl.program_id(0), pl.program_id(1)
        x, y, c, _ = _place()
        half = jnp.where(p == 0, 1 - c, c)

        def send(jj):
            return _remote(stage_ref.at[jj % 2], land_ref.at[jj], send_sems.at[jj], recv_sems.at[jj], (x, y, 1 - c))

        def fetch():
            return pltpu.make_async_copy(land_ref.at[j], got_ref, got_sem)

        @pl.when(p == 1)
        def _():
            @pl.when(j == 0)
            def _():
                for jj in range(max(steps - 2, 0), steps):
                    send(jj).wait_send()

            send(j).wait_recv()
            fetch().start()

        for hv in range(2):
            @pl.when(half == hv)
            def _():
                acc_ref[...] = tile(a_ref, b_ref, hv)

        @pl.when(p == 0)
        def _():
            @pl.when(j >= 2)
            def _():
                send(j - 2).wait_send()

            stage_ref[j % 2] = acc_ref[...].astype(BF16)
            send(j).start()

        @pl.when(p == 1)
        def _():
            fetch().wait()
            parts_ref[...] = (acc_ref[...] + got_ref[...].astype(F32)).astype(BF16)

    (parts, _), got = _hosted(
        body, comm, name=name, grid=(2, steps),
        out_shape=[jax.ShapeDtypeStruct((steps, tr, bn), BF16), jax.ShapeDtypeStruct((steps, tr, bn), BF16)],
        in_specs=[spec(a_cols, m), spec(b_cols, n)],
        out_specs=[pl.BlockSpec((None, tr, bn), lambda p, j: (p * j, 0, 0)), ANY],
        scratch_shapes=[pltpu.VMEM((tr, bn), F32), pltpu.VMEM((2, tr, bn), BF16), pltpu.VMEM((tr, bn), BF16),
                        DMA((steps,)), DMA((steps,)), DMA],
        args=[a, b])
    return parts, got


def _adamw(w, g, m, v, name):
    r, c = w.shape
    rb = _row_block(r, c * (8 * 4 * 2 + 4 * 4))
    bc1 = 1.0 - ADAM_B1 ** ADAM_STEP
    bc2 = 1.0 - ADAM_B2 ** ADAM_STEP

    def body(w_ref, g_ref, m_ref, v_ref, d_ref, nm_ref, nv_ref, go_ref):
        gv = g_ref[...]
        go_ref[...] = gv
        nm = ADAM_B1 * m_ref[...] + (1.0 - ADAM_B1) * gv
        nv = ADAM_B2 * v_ref[...] + (1.0 - ADAM_B2) * (gv * gv)
        nm_ref[...] = nm
        nv_ref[...] = nv
        d_ref[...] = -ADAM_LR * ((nm / bc1) / (jnp.sqrt(nv / bc2) + ADAM_EPS) + ADAM_WD * w_ref[...])

    spec = pl.BlockSpec((rb, c), lambda i: (i, 0))
    return pl.pallas_call(
        body, name=name, grid=(r // rb,), out_shape=[jax.ShapeDtypeStruct((r, c), F32)] * 4,
        in_specs=[spec] * 4, out_specs=[spec] * 4,
        compiler_params=pltpu.CompilerParams(dimension_semantics=("parallel",), vmem_limit_bytes=VMEM_LIMIT),
    )(w, g, m, v)


def kernel(x, norm_gains, pool_w, pool_scale, conv_in_w, conv_w, conv_out_w, ffn_gate_up_w, ffn_down_w, loss_target, m_norm_gains, m_pool_w, m_pool_scale, m_conv_in_w, m_conv_w, m_conv_out_w, m_ffn_gate_up_w, m_ffn_down_w, v_norm_gains, v_pool_w, v_pool_scale, v_conv_in_w, v_conv_w, v_conv_out_w, v_ffn_gate_up_w, v_ffn_down_w):
    _, t, d = x.shape
    dq = d // N_CHIPS
    ng = len(POOL_WINDOWS)
    gw = d // ng
    fq = ffn_down_w.shape[1]
    f = N_CHIPS * fq
    fc = f // 2
    core = lax.axis_index("c")
    chip = 2 * lax.axis_index("x") + lax.axis_index("y")
    core_arr = jnp.reshape(core, (1,)).astype(jnp.int32)
    where_arr = jnp.stack([chip, core]).astype(jnp.int32)
    x2, target = x[0], loss_target[0]

    small_loc = jnp.concatenate(
        [norm_gains.reshape(8, dq), conv_w[0], jnp.zeros((5, dq), F32)], axis=0).reshape(1, 2, 8, dq)
    pool_loc = pool_w.astype(BF16).reshape(1, 2, ng // 2 * (gw // N_CHIPS), gw)
    wgu_loc = ffn_gate_up_w.astype(BF16).reshape(2, 2, d // 2, fc)
    wd_loc = ffn_down_w.astype(BF16).reshape(2, 2, fq // 2, d)
    win_loc = conv_in_w.astype(BF16).reshape(1, 2, d // 2, -1)
    wout_loc = conv_out_w.astype(BF16).reshape(1, 2, dq // 2, d)

    def ffn_weights(wgu_f, wd_f):
        return wgu_f.reshape(N_CHIPS, d, fc), wd_f.reshape(f, d)

    ag = _SplitGather([(pool_loc, 0), (small_loc, 0), (wgu_loc, 0), (wd_loc, 0), (win_loc, 0), (wout_loc, 0),
                       (wgu_loc, 1), (wd_loc, 1)])
    ag.start()
    pool_f, small_f = _pass_on(ag.wait([0, 1], ag.token, "ag_wait_first"), "ag_pass_first")
    poolw = pool_f.reshape(N_CHIPS, ng, gw // N_CHIPS, gw).transpose(1, 0, 2, 3).reshape(ng, gw, gw)
    small = small_f.transpose(1, 2, 0, 3).reshape(16, d)
    h1, _ = _fwd_pool(x2, small, pool_scale, poolw)
    wgu0, wd0 = ffn_weights(*_pass_on(ag.wait([2, 3], h1, "ag_wait_ffn0"), "ag_pass_ffn0"))
    (h2, gu0, ff0, n0), _ = _fwd_ffn(h1, small, wgu0, wd0, 0)
    win_f, wout_f = _pass_on(ag.wait([4, 5], h2, "ag_wait_conv"), "ag_pass_conv")
    win_f, wout_f = win_f.reshape(N_CHIPS, d, -1), wout_f.reshape(d, d)
    (h3, proj, y, nc), _ = _fwd_conv(h2, small, win_f, wout_f)
    wgu1, wd1 = ffn_weights(*_pass_on(ag.wait([6, 7], h3, "ag_wait_ffn1"), "ag_pass_ffn1"))
    (dh4, gu1, ff1, n1, loss_blk), _ = _fwd_ffn(h3, small, wgu1, wd1, 1, target=target)

    (dh3, dgu1, dff1, act1, sg_f1), _ = _bwd_ffn(dh4, h3, ff1, gu1, small, wgu1, wd1, 1)
    parts_d1, _ = _weight_grad(act1, dff1, fc, d // 2, "b", "dw_down1")
    parts_gu1, _ = _weight_grad(n1, dgu1, d // 2, fc, "a", "dw_gate_up1")
    ex_ffn1 = _SplitExchange([parts_d1.reshape(N_CHIPS, fq, d // 2), parts_gu1])
    started = ex_ffn1.start("rs_start_ffn1")
    (dh2, dproj, dyv, bcv, sg_c), _ = _bwd_conv(dh3, h2, y, proj, small, win_f, wout_f, _After(started))
    parts_in, _ = _weight_grad(nc, dproj, d // 2, 3 * d // N_CHIPS, "a", "dw_conv_in")
    parts_out, _ = _weight_grad(bcv, dyv, dq // 2, d, "a", "dw_conv_out")
    ex_conv = _SplitExchange([parts_in, parts_out])
    started = ex_conv.start("rs_start_conv")
    (dh1, dgu0, dff0, act0, sg_f0), _ = _bwd_ffn(dh2, h1, ff0, gu0, small, wgu0, wd0, 0, _After(started))
    parts_d0, _ = _weight_grad(act0, dff0, fc, d // 2, "b", "dw_down0")
    parts_gu0, _ = _weight_grad(n0, dgu0, d // 2, fc, "a", "dw_gate_up0")
    ex_ffn0 = _SplitExchange([parts_d0.reshape(N_CHIPS, fq, d // 2), parts_gu0])
    started = ex_ffn0.start("rs_start_ffn0")
    (grad_x, dpool, sg_p), _ = _bwd_pool(dh1, x2, small, pool_scale, poolw, _After(started))
    g_pool = dpool.astype(BF16).reshape(2, ng // 2, N_CHIPS, gw // N_CHIPS, gw).transpose(2, 0, 1, 3, 4).reshape(
        N_CHIPS, 2, ng // 2 * (gw // N_CHIPS), gw)
    small_g = jnp.concatenate(
        [sg_p[0:2], sg_f0[0:2], sg_c[0:2], sg_f1[0:2], sg_c[2:5], sg_p[2:3],
         jnp.broadcast_to(loss_blk[0:1, 0:1], (1, d)), jnp.zeros((3, d), F32)], axis=0)
    (land_p,) = _sibling_exchange([g_pool], None, "rs_sibling_pool")
    ex_pool = _SplitExchange([_add_sibling(g_pool, land_p, core_arr)])
    started = ex_pool.start("rs_start_pool")

    def update(w, g, m, v, name):
        flat = (-1, w.shape[-1])
        dl, m2, v2, g2 = _adamw(w.reshape(flat), g.reshape(flat), m.reshape(flat), v.reshape(flat), "adamw_" + name)
        return tuple(o.reshape(w.shape) for o in (g2, dl, m2, v2))

    (parts_d1, parts_gu1), (recv_d1, recv_gu1) = ex_ffn1.wait(started, "rs_wait_ffn1")
    (parts_in, parts_out), (recv_in, recv_out) = ex_conv.wait(started, "rs_wait_conv")
    gs_gu = _add_chips(parts_gu1, recv_gu1, where_arr, 1, 2)
    gs_d = _add_chips(parts_d1, recv_d1, where_arr, 1, 2, col_half=True)
    gs_in = _add_chips(parts_in, recv_in, where_arr)
    gs_out = _add_chips(parts_out, recv_out, where_arr)
    full_in, full_out = _sibling_share([gs_in, gs_out], [False, False], "rs_share_conv")
    up_in = update(conv_in_w, full_in.reshape(1, d, -1), m_conv_in_w, v_conv_in_w, "conv_in")
    up_out = update(conv_out_w, full_out.reshape(1, dq, d), m_conv_out_w, v_conv_out_w, "conv_out")
    (parts_d0, parts_gu0), (recv_d0, recv_gu0) = ex_ffn0.wait(up_out[1], "rs_wait_ffn0")
    (parts_p,), (recv_p,) = ex_pool.wait(up_out[1], "rs_wait_pool")
    gs_gu = _add_chips(parts_gu0, recv_gu0, where_arr, 0, 2, gs_gu)
    gs_d = _add_chips(parts_d0, recv_d0, where_arr, 0, 2, gs_d, col_half=True)
    gs_pool = _add_chips(parts_p, recv_p, where_arr)
    full_gu, full_d, full_pool = _sibling_share([gs_gu, gs_d, gs_pool], [False, True, False], "rs_share_ffn")
    (small_all,) = _sibling_exchange([], small_g, "rs_small_gather", after=recv_gu0)
    small_sum = _sum_small(small_all)
    loss = small_sum[12, 0]
    mine = lax.dynamic_slice_in_dim(small_sum, chip * dq, dq, axis=1)

    ups = [
        update(norm_gains, mine[0:8].reshape(2, 4, dq), m_norm_gains, v_norm_gains, "gains"),
        update(pool_w, full_pool.reshape(1, ng, gw // N_CHIPS, gw), m_pool_w, v_pool_w, "pool_w"),
        update(pool_scale, small_sum[11:12], m_pool_scale, v_pool_scale, "pool_scale"),
        up_in,
        update(conv_w, mine[8:11].reshape(1, 3, dq), m_conv_w, v_conv_w, "taps"),
        up_out,
        update(ffn_gate_up_w, full_gu.reshape(2, d, fc), m_ffn_gate_up_w, v_ffn_gate_up_w, "gate_up"),
        update(ffn_down_w, full_d.reshape(2, fq, d), m_ffn_down_w, v_ffn_down_w, "down"),
    ]
    grads_out, deltas, new_ms, new_vs = (list(col) for col in zip(*ups))
    return (loss, grad_x[None], *grads_out, *deltas, *new_ms, *new_vs)
```

```python
import jax
import jax.numpy as jnp
from jax import lax
from jax.experimental import pallas as pl
from jax.experimental.pallas import tpu as pltpu

RMS_EPS = 1e-6
POOL_WINDOWS = (2, 4, 8, 16)
POOL_HALO = 16
CONV_HALO = 8
N_CHIPS = 4
N_DEV = 8
ADAM_LR = 0.001
ADAM_B1 = 0.9
ADAM_B2 = 0.999
ADAM_EPS = 1e-08
ADAM_WD = 0.01
ADAM_STEP = 10
VMEM_LIMIT = 56 * 2**20
STREAM_BUDGET = 24 * 2**20
MESH = pl.DeviceIdType.MESH
ANY = pl.BlockSpec(memory_space=pl.ANY)
DMA = pltpu.SemaphoreType.DMA
BF16 = jnp.bfloat16
F32 = jnp.float32


def _token_tile(t, rows=512):
    return min(rows, t)


def _rms(x):
    r = lax.rsqrt(jnp.mean(x * x, axis=-1, keepdims=True) + RMS_EPS)
    return x * r, r


def _rms_bwd(dy, xh, r, g):
    a = dy * g
    return r * (a - xh * jnp.mean(a * xh, axis=-1, keepdims=True))


def _dot(a, b):
    return jnp.dot(a, b, preferred_element_type=F32)


def _dot_nt(a, b):
    return lax.dot_general(a, b, (((1,), (1,)), ((), ())), preferred_element_type=F32)


def _dot_tn(a, b):
    return lax.dot_general(a, b, (((0,), (0,)), ((), ())), preferred_element_type=F32)


def _colsum(a):
    return jnp.sum(a, axis=0, keepdims=True)


def _resident(block, index_map):
    return pl.BlockSpec(block, index_map, pipeline_mode=pl.Buffered(1))


def _row_block(r, row_bytes):
    best = None
    for rb in range(16, r + 1, 16):
        if r % rb == 0 and rb * row_bytes <= STREAM_BUDGET:
            best = rb
    return best if best is not None else r


def _place():
    x, y, c = lax.axis_index("x"), lax.axis_index("y"), lax.axis_index("c")
    return x, y, c, 2 * x + y


def _dev(chip, core):
    return (chip // 2, chip % 2, core)


def _remote(src, dst, send_sem, recv_sem, device):
    return pltpu.make_async_remote_copy(src_ref=src, dst_ref=dst, send_sem=send_sem, recv_sem=recv_sem,
                                        device_id=device, device_id_type=MESH)


class _Gather:
    def __init__(self, shards):
        n = len(shards)
        self.args = [s for s, _ in shards]
        self.layers = [l for _, l in shards]
        self.out_shape = [jax.ShapeDtypeStruct((N_CHIPS,) + s.shape[1:], s.dtype) for s in self.args]
        self.sems = [DMA((n,)), DMA((n,)), DMA((n, 3)), DMA((n, 3)), DMA((n, 3)), DMA((n, 3))]

    def _own(self, loc, out, sems, a):
        x, y, c, k = _place()
        return _remote(loc[a].at[self.layers[a]], out[a].at[k], sems[0].at[a], sems[1].at[a], (x, y, 1 - c))

    def _ici(self, loc, out, sems, a, m, arrival):
        x, y, c, k = _place()
        dst = out[a].at[k ^ m, c] if arrival else out[a].at[k, c]
        return _remote(loc[a].at[self.layers[a], c], dst, sems[2].at[a, m - 1], sems[3].at[a, m - 1], _dev(k ^ m, c))

    def _forward(self, out, sems, a, m, arrival):
        x, y, c, k = _place()
        got = out[a].at[k ^ m, 1 - c] if arrival else out[a].at[k ^ m, c]
        return _remote(got, got, sems[4].at[a, m - 1], sems[5].at[a, m - 1], (x, y, 1 - c))

    def start(self, loc, out, sems):
        for a in range(len(self.args)):
            for m in range(1, N_CHIPS):
                self._ici(loc, out, sems, a, m, False).start()
            self._own(loc, out, sems, a).start()

    def finish(self, loc, out, sems):
        n = len(self.args)
        for a in range(n):
            for m in range(1, N_CHIPS):
                self._ici(loc, out, sems, a, m, True).wait_recv()
                self._forward(out, sems, a, m, False).start()
        for a in range(n):
            for m in range(1, N_CHIPS):
                self._forward(out, sems, a, m, True).wait_recv()
            self._own(loc, out, sems, a).wait_recv()
        for a in range(n):
            for m in range(1, N_CHIPS):
                self._ici(loc, out, sems, a, m, False).wait_send()
                self._forward(out, sems, a, m, False).wait_send()
            self._own(loc, out, sems, a).wait_send()


class _ChipExchange:
    def __init__(self, parts):
        n = len(parts)
        self.args = list(parts)
        self.out_shape = [jax.ShapeDtypeStruct((N_CHIPS - 1,) + p.shape[1:], p.dtype) for p in parts]
        self.sems = [DMA((n, 3)), DMA((n, 3))]

    def _copy(self, p, land, sems, a, m):
        x, y, c, k = _place()
        return _remote(p[a].at[k ^ m], land[a].at[m - 1], sems[0].at[a, m - 1], sems[1].at[a, m - 1], _dev(k ^ m, c))

    def start(self, p, land, sems):
        for a in range(len(self.args)):
            for m in range(1, N_CHIPS):
                self._copy(p, land, sems, a, m).start()

    def finish(self, p, land, sems):
        for a in range(len(self.args)):
            for m in range(1, N_CHIPS):
                self._copy(p, land, sems, a, m).wait_recv()
        for a in range(len(self.args)):
            for m in range(1, N_CHIPS):
                self._copy(p, land, sems, a, m).wait_send()


def _hosted(body, comm, *, name, grid, in_specs, out_specs, out_shape, args, scratch_shapes=()):
    ni, no, ns = len(in_specs), len(out_shape), len(scratch_shapes)
    if comm is None:
        res = pl.pallas_call(
            body, name=name, grid=grid, in_specs=list(in_specs), out_specs=list(out_specs), out_shape=list(out_shape),
            scratch_shapes=list(scratch_shapes),
            compiler_params=pltpu.CompilerParams(dimension_semantics=("arbitrary",) * len(grid), vmem_limit_bytes=VMEM_LIMIT),
        )(*args)
        return list(res), []
    nc, nco = len(comm.args), len(comm.out_shape)

    def full(*refs):
        cin = refs[ni:ni + nc]
        outs = refs[ni + nc:ni + nc + no]
        cout = refs[ni + nc + no:ni + nc + no + nco]
        scratch = refs[ni + nc + no + nco:ni + nc + no + nco + ns]
        csems = refs[ni + nc + no + nco + ns:]
        first = _all_of([pl.program_id(ax) == 0 for ax in range(len(grid))])
        last = _all_of([pl.program_id(ax) == grid[ax] - 1 for ax in range(len(grid))])

        @pl.when(first)
        def _():
            comm.start(cin, cout, csems)

        body(*refs[:ni], *outs, *scratch)

        @pl.when(last)
        def _():
            comm.finish(cin, cout, csems)

    res = pl.pallas_call(
        full, name=name, grid=grid, in_specs=[*in_specs, *[ANY] * nc], out_specs=[*out_specs, *[ANY] * nco],
        out_shape=[*out_shape, *comm.out_shape], scratch_shapes=[*scratch_shapes, *comm.sems],
        compiler_params=pltpu.CompilerParams(dimension_semantics=("arbitrary",) * len(grid), vmem_limit_bytes=VMEM_LIMIT,
                                             has_side_effects=True),
    )(*args, *comm.args)
    return list(res[:no]), list(res[no:])


def _all_of(conds):
    out = conds[0]
    for c in conds[1:]:
        out = jnp.logical_and(out, c)
    return out


class _After:
    def __init__(self, token):
        self.args, self.out_shape, self.sems = [token], [], []

    def start(self, *_):
        pass

    def finish(self, *_):
        pass


def _alone(comm, name):
    return _hosted(lambda: None, comm, name=name, grid=(1,), in_specs=[], out_specs=[], out_shape=[], args=[])[1]


HBM = pl.BlockSpec(memory_space=pltpu.HBM)
SEM = pl.BlockSpec(memory_space=pltpu.SEMAPHORE)
DATAFLOW = pltpu.SideEffectType.DATAFLOW_SIDE_EFFECTING


class _SplitGather:
    PER_ARRAY = 8

    def __init__(self, shards):
        self.plan = _Gather(shards)
        self.n = len(shards)

    @staticmethod
    def _tables(sems_of):
        class Table:
            def __init__(self, pick):
                self.pick = pick

            @property
            def at(self):
                return self

            def __getitem__(self, idx):
                return self.pick(idx)

        return [Table(lambda a: sems_of[a][0]), Table(lambda a: sems_of[a][1]),
                Table(lambda am: sems_of[am[0]][2 + am[1]]), Table(lambda am: sems_of[am[0]][5 + am[1]])]

    def start(self, name):
        n, plan, per = self.n, self.plan, self.PER_ARRAY

        def body(*refs):
            loc, land = refs[:n], refs[n:2 * n]
            sems_of = {a: refs[2 * n + per * a:2 * n + per * (a + 1)] for a in range(n)}
            plan.start(loc, land, self._tables(sems_of))
            refs[-1][...] = jnp.zeros_like(refs[-1])

        lands = [pltpu.with_memory_space_constraint(lax.empty(o.shape, o.dtype), pltpu.HBM) for o in plan.out_shape]
        locs = [pltpu.with_memory_space_constraint(a, pltpu.HBM) for a in plan.args]
        res = pl.pallas_call(
            body, name=name,
            out_shape=[*[DMA(())] * (per * n),
                       *[pltpu.HBM(a.shape, a.dtype) for a in plan.args],
                       *[pltpu.HBM(o.shape, o.dtype) for o in plan.out_shape],
                       jax.ShapeDtypeStruct((8, 128), F32)],
            in_specs=[HBM] * (2 * n),
            out_specs=[SEM] * (per * n) + [HBM] * (2 * n) + [pl.BlockSpec(memory_space=pltpu.VMEM)],
            input_output_aliases={i: per * n + i for i in range(2 * n)},
            compiler_params=pltpu.CompilerParams(has_side_effects=DATAFLOW),
        )(*locs, *lands)
        self.sems = {a: list(res[per * a:per * (a + 1)]) for a in range(n)}
        self.locs = list(res[per * n:per * n + n])
        self.lands = list(res[per * n + n:per * n + 2 * n])
        self.token = res[-1]

    def wait(self, idxs, after, name):
        plan, g, per = self.plan, len(idxs), self.PER_ARRAY

        def body(*refs):
            loc = {a: refs[j] for j, a in enumerate(idxs)}
            land = {a: refs[g + j] for j, a in enumerate(idxs)}
            sems = self._tables({a: refs[2 * g + per * j:2 * g + per * (j + 1)] for j, a in enumerate(idxs)})
            for a in idxs:
                for m in range(1, N_CHIPS):
                    plan._ici(loc, land, sems, a, m, True).wait_recv()
                    plan._ici(loc, land, sems, a, m, False).wait_send()
                plan._own(loc, land, sems, a).wait_recv()
                plan._own(loc, land, sems, a).wait_send()

        res = pl.pallas_call(
            body, name=name,
            out_shape=[pltpu.HBM(self.lands[a].shape, self.lands[a].dtype) for a in idxs],
            in_specs=[HBM] * (2 * g) + [SEM] * (per * g) + [pl.BlockSpec(memory_space=pl.ANY)], out_specs=[HBM] * g,
            input_output_aliases={g + j: j for j in range(g)},
            compiler_params=pltpu.CompilerParams(has_side_effects=DATAFLOW),
        )(*[self.locs[a] for a in idxs], *[self.lands[a] for a in idxs],
          *[s for a in idxs for s in self.sems[a]], after)
        return list(res)


class _SplitExchange:
    PER_ARRAY = 6

    def __init__(self, parts):
        self.plan = _ChipExchange(parts)
        self.n = len(parts)

    @staticmethod
    def _tables(sems_of):
        class Table:
            def __init__(self, pick):
                self.pick = pick

            @property
            def at(self):
                return self

            def __getitem__(self, am):
                return self.pick(am)

        return [Table(lambda am: sems_of[am[0]][am[1]]), Table(lambda am: sems_of[am[0]][3 + am[1]])]

    def start(self, name):
        n, plan, per = self.n, self.plan, self.PER_ARRAY

        def body(*refs):
            p, land = refs[:n], refs[n:2 * n]
            sems_of = {a: refs[2 * n + per * a:2 * n + per * (a + 1)] for a in range(n)}
            plan.start(p, land, self._tables(sems_of))
            refs[-1][...] = jnp.zeros_like(refs[-1])

        lands = [pltpu.with_memory_space_constraint(lax.empty(o.shape, o.dtype), pltpu.HBM) for o in plan.out_shape]
        parts = [pltpu.with_memory_space_constraint(a, pltpu.HBM) for a in plan.args]
        res = pl.pallas_call(
            body, name=name,
            out_shape=[*[DMA(())] * (per * n),
                       *[pltpu.HBM(a.shape, a.dtype) for a in plan.args],
                       *[pltpu.HBM(o.shape, o.dtype) for o in plan.out_shape],
                       jax.ShapeDtypeStruct((8, 128), F32)],
            in_specs=[HBM] * (2 * n),
            out_specs=[SEM] * (per * n) + [HBM] * (2 * n) + [pl.BlockSpec(memory_space=pltpu.VMEM)],
            input_output_aliases={i: per * n + i for i in range(2 * n)},
            compiler_params=pltpu.CompilerParams(has_side_effects=DATAFLOW),
        )(*parts, *lands)
        self.sems = list(res[:per * n])
        self.parts = list(res[per * n:per * n + n])
        self.lands = list(res[per * n + n:per * n + 2 * n])
        return res[-1]

    def wait(self, after, name):
        n, plan, per = self.n, self.plan, self.PER_ARRAY

        def body(*refs):
            p, land = refs[:n], refs[n:2 * n]
            sems_of = {a: refs[2 * n + per * a:2 * n + per * (a + 1)] for a in range(n)}
            plan.finish(p, land, self._tables(sems_of))

        res = pl.pallas_call(
            body, name=name,
            out_shape=[*[pltpu.HBM(a.shape, a.dtype) for a in self.parts], *[pltpu.HBM(a.shape, a.dtype) for a in self.lands]],
            in_specs=[HBM] * (2 * n) + [SEM] * (per * n) + [pl.BlockSpec(memory_space=pl.ANY)] * len(after),
            out_specs=[HBM] * (2 * n), input_output_aliases={i: i for i in range(2 * n)},
            compiler_params=pltpu.CompilerParams(has_side_effects=DATAFLOW),
        )(*self.parts, *self.lands, *self.sems, *after)
        return list(res[:n]), list(res[n:])


PASS_ON_BARRIER = 1


def _sibling_barrier():
    x, y, c, _ = _place()
    barrier = pltpu.get_barrier_semaphore()
    pl.semaphore_signal(barrier, inc=1, device_id=(x, y, 1 - c), device_id_type=MESH)
    pl.semaphore_wait(barrier, 1)


def _pass_on(lands, name):
    n = len(lands)

    def body(*refs):
        out = refs[n:2 * n]
        send_sems, recv_sems = refs[2 * n:]
        x, y, c, k = _place()
        _sibling_barrier()
        cps = []
        for a in range(n):
            for m in range(1, N_CHIPS):
                got = out[a].at[k ^ m, c]
                cp = _remote(got, got, send_sems.at[a, m - 1], recv_sems.at[a, m - 1], (x, y, 1 - c))
                cp.start()
                cps.append(cp)
        for a in range(n):
            for m in range(1, N_CHIPS):
                theirs = out[a].at[k ^ m, 1 - c]
                _remote(theirs, theirs, send_sems.at[a, m - 1], recv_sems.at[a, m - 1], (x, y, 1 - c)).wait_recv()
        for cp in cps:
            cp.wait_send()

    return pl.pallas_call(
        body, name=name, out_shape=[jax.ShapeDtypeStruct(a.shape, a.dtype) for a in lands],
        in_specs=[ANY] * n, out_specs=[ANY] * n, input_output_aliases={a: a for a in range(n)},
        scratch_shapes=[DMA((n, 3)), DMA((n, 3))],
        compiler_params=pltpu.CompilerParams(has_side_effects=True, collective_id=PASS_ON_BARRIER),
    )(*lands)


def _sibling_exchange(grads, small, name, after=None):
    n = len(grads)
    ns = 0 if small is None else 1
    na = 0 if after is None else 1

    def body(*refs):
        g = refs[:n]
        land = refs[n + ns + na:2 * n + ns + na]
        send_sems, recv_sems, own_sem, ssend_sems, srecv_sems = refs[2 * n + 2 * ns + na:]
        x, y, c, k = _place()
        me = 2 * k + c
        cps = []
        for a in range(n):
            cp = _remote(g[a].at[:, pl.ds(1 - c, 1)], land[a], send_sems.at[a], recv_sems.at[a], (x, y, 1 - c))
            cp.start()
            cps.append(cp)
        if small is not None:
            sm, smg = refs[n], refs[2 * n + 1 + na]
            peers = [me ^ m for m in range(1, N_DEV)]
            ids = [(p // 4, (p // 2) % 2, p % 2) for p in peers]
            own = pltpu.make_async_copy(sm, smg.at[me], own_sem)
            own.start()
            for m in range(1, N_DEV):
                cp = _remote(sm, smg.at[me], ssend_sems.at[m - 1], srecv_sems.at[m - 1], ids[m - 1])
                cp.start()
                cps.append(cp)
            for m in range(1, N_DEV):
                _remote(sm, smg.at[peers[m - 1]], ssend_sems.at[m - 1], srecv_sems.at[m - 1], ids[m - 1]).wait_recv()
            own.wait()
        for cp in cps[:n]:
            cp.wait_recv()
        for cp in cps:
            cp.wait_send()

    out_shape = [jax.ShapeDtypeStruct((N_CHIPS, 1) + a.shape[2:], a.dtype) for a in grads]
    ins = list(grads)
    if small is not None:
        out_shape.append(jax.ShapeDtypeStruct((N_DEV,) + small.shape, small.dtype))
        ins.append(small)
    if after is not None:
        ins.append(after)
    return pl.pallas_call(
        body, name=name, out_shape=out_shape, in_specs=[ANY] * (n + ns + na), out_specs=[ANY] * (n + ns),
        scratch_shapes=[DMA((max(n, 1),)), DMA((max(n, 1),)), DMA, DMA((N_DEV - 1,)), DMA((N_DEV - 1,))],
        compiler_params=pltpu.CompilerParams(has_side_effects=True),
    )(*ins)


def _sibling_share(halves, col_half, name, after=()):
    n, na = len(halves), len(after)

    def body(*refs):
        out = refs[n + na:2 * n + na]
        send_sems, recv_sems = refs[2 * n + na:]
        x, y, c, k = _place()

        def half(a, core):
            if not col_half[a]:
                return out[a].at[:, pl.ds(core, 1)]
            cols = out[a].shape[-1] // 2
            return out[a].at[:, :, pl.ds(pl.multiple_of(core * cols, cols), cols)]

        cps = []
        for a in range(n):
            cp = _remote(half(a, c), half(a, c), send_sems.at[a], recv_sems.at[a], (x, y, 1 - c))
            cp.start()
            cps.append(cp)
        for a in range(n):
            _remote(half(a, 1 - c), half(a, 1 - c), send_sems.at[a], recv_sems.at[a], (x, y, 1 - c)).wait_recv()
        for cp in cps:
            cp.wait_send()

    out_shape = [jax.ShapeDtypeStruct(a.shape, a.dtype) for a in halves]
    return pl.pallas_call(
        body, name=name, out_shape=out_shape, in_specs=[ANY] * (n + na), out_specs=[ANY] * n,
        input_output_aliases={a: a for a in range(n)}, scratch_shapes=[DMA((n,)), DMA((n,))],
        compiler_params=pltpu.CompilerParams(has_side_effects=True),
    )(*halves, *after)


def _add_sibling(g, land, core):
    _, _, r, c = g.shape
    rb = _row_block(r, c * (3 * 2 * 2 + 2 * 4))

    def body(core_ref, g_ref, l_ref, o_ref):
        o_ref[...] = (g_ref[...].astype(F32) + l_ref[...].astype(F32)).astype(o_ref.dtype)

    return pl.pallas_call(
        body, name="rs_add_sibling", out_shape=jax.ShapeDtypeStruct((N_CHIPS, r, c), g.dtype),
        grid_spec=pltpu.PrefetchScalarGridSpec(
            num_scalar_prefetch=1, grid=(N_CHIPS, r // rb),
            in_specs=[pl.BlockSpec((None, None, rb, c), lambda j, i, core_ref: (j, core_ref[0], i, 0)),
                      pl.BlockSpec((None, None, rb, c), lambda j, i, core_ref: (j, 0, i, 0))],
            out_specs=pl.BlockSpec((None, rb, c), lambda j, i, core_ref: (j, i, 0))),
        compiler_params=pltpu.CompilerParams(dimension_semantics=("parallel", "parallel"), vmem_limit_bytes=VMEM_LIMIT),
    )(core, g, land)


def _add_chips(part, land, where, layer=0, n_layers=1, into=None, col_half=False, rows=(0, 1)):
    _, r, c = part.shape
    sub, n_sub = rows
    rb = _row_block(r, c * (4 * 2 * 2 + 4 * 2 + 2 * 4))

    def body(where_ref, p_ref, l_ref, *rest):
        acc = p_ref[...].astype(F32)
        for m in range(N_CHIPS - 1):
            acc = acc + l_ref[m].astype(F32)
        rest[-1][...] = acc

    in_specs = [pl.BlockSpec((None, rb, c), lambda i, where_ref: (where_ref[0], i, 0)),
                pl.BlockSpec((N_CHIPS - 1, rb, c), lambda i, where_ref: (0, i, 0))]
    args = [where, part, land]
    if into is not None:
        in_specs.append(ANY)
        args.append(into)
    if col_half:
        out_shape = jax.ShapeDtypeStruct((n_layers, r, 2 * c), F32)
        out_spec = pl.BlockSpec((None, rb, c), lambda i, where_ref: (layer, i, where_ref[1]))
    else:
        out_shape = jax.ShapeDtypeStruct((n_layers, 2, n_sub * r, c), F32)
        out_spec = pl.BlockSpec((None, None, rb, c), lambda i, where_ref: (layer, where_ref[1], sub * (r // rb) + i, 0))
    return pl.pallas_call(
        body, name="rs_add_chips", out_shape=out_shape,
        grid_spec=pltpu.PrefetchScalarGridSpec(
            num_scalar_prefetch=1, grid=(r // rb,), in_specs=in_specs, out_specs=out_spec),
        input_output_aliases={} if into is None else {3: 0},
        compiler_params=pltpu.CompilerParams(dimension_semantics=("parallel",), vmem_limit_bytes=VMEM_LIMIT),
    )(*args)


def _sum_small(smg):
    def body(s_ref, o_ref):
        acc = s_ref[0]
        for j in range(1, N_DEV):
            acc = acc + s_ref[j]
        o_ref[...] = acc

    return pl.pallas_call(body, name="rs_sum_small", out_shape=jax.ShapeDtypeStruct(smg.shape[1:], F32))(smg)


def _pool_windows(ext_ref, g, gw, tm, first_row):
    w = POOL_WINDOWS[g]
    slab = ext_ref[:, g * gw:(g + 1) * gw]
    p, k = slab, 1
    while k < w:
        p = p + pltpu.roll(p, k, 0)
        k *= 2
    t = first_row + lax.broadcasted_iota(jnp.int32, (tm, 1), 0)
    cnt = jnp.minimum(t + 1, w).astype(F32)
    return p[POOL_HALO:] / cnt - slab[POOL_HALO:]


def _fwd_pool(x, small, scale, poolw, comm=None):
    t, d = x.shape
    tm = _token_tile(t)
    gw = d // len(POOL_WINDOWS)

    def body(x_ref, sm_ref, sc_ref, w_ref, h_ref, ext_ref, mix_ref):
        i = pl.program_id(0)

        @pl.when(i == 0)
        def _():
            ext_ref[0:POOL_HALO, :] = jnp.zeros((POOL_HALO, d), F32)

        @pl.when(i > 0)
        def _():
            ext_ref[0:POOL_HALO, :] = ext_ref[tm:tm + POOL_HALO, :]

        xv = x_ref[...]
        xh, _ = _rms(xv)
        ext_ref[POOL_HALO:, :] = xh * sm_ref[0:1, :]
        for g in range(len(POOL_WINDOWS)):
            pooled = _pool_windows(ext_ref, g, gw, tm, i * tm)
            cols = slice(g * gw, (g + 1) * gw)
            mix_ref[:, cols] = _dot(pooled.astype(BF16), w_ref[g]) * sc_ref[:, cols]
        mh, _ = _rms(mix_ref[...])
        h_ref[...] = xv + mh * sm_ref[1:2, :]

    (h,), got = _hosted(
        body, comm, name="fwd_pool", grid=(t // tm,), out_shape=[jax.ShapeDtypeStruct((t, d), F32)],
        in_specs=[pl.BlockSpec((tm, d), lambda i: (i, 0)), _resident(small.shape, lambda i: (0, 0)),
                  _resident(scale.shape, lambda i: (0, 0)), _resident(poolw.shape, lambda i: (0, 0, 0))],
        out_specs=[pl.BlockSpec((tm, d), lambda i: (i, 0))],
        scratch_shapes=[pltpu.VMEM((POOL_HALO + tm, d), F32), pltpu.VMEM((tm, d), F32)],
        args=[x, small, scale, poolw])
    return h, got


def _fwd_ffn(h, small, wgu, wd, layer, comm=None, target=None):
    t, d = h.shape
    tm = _token_tile(t)
    steps = t // tm
    fc = wgu.shape[-1]
    f = 2 * fc
    g_in, g_out = 4 * layer + 2, 4 * layer + 3
    with_loss = target is not None

    def body(h_ref, *refs):
        if with_loss:
            t_ref, sm_ref, wgu_ref, wd_ref, o_ref, gu_ref, ff_ref, n_ref, l_ref, acc_ref = refs
        else:
            sm_ref, wgu_ref, wd_ref, o_ref, gu_ref, ff_ref, n_ref = refs
        hv = h_ref[...]
        hh, _ = _rms(hv)
        n = (hh * sm_ref[g_in:g_in + 1, :]).astype(BF16)
        n_ref[...] = n
        ff = None
        for j in range(2):
            gate = _dot(n, wgu_ref[j])
            up = _dot(n, wgu_ref[2 + j])
            gu_ref[:, j * fc:(j + 1) * fc] = gate.astype(BF16)
            gu_ref[:, f + j * fc:f + (j + 1) * fc] = up.astype(BF16)
            act = (gate * jax.nn.sigmoid(gate) * up).astype(BF16)
            part = _dot(act, wd_ref[j * fc:(j + 1) * fc, :])
            ff = part if ff is None else ff + part
        ff_ref[...] = ff
        fh, _ = _rms(ff)
        out = hv + fh * sm_ref[g_out:g_out + 1, :]
        if not with_loss:
            o_ref[...] = out
            return
        i = pl.program_id(0)
        e = out - t_ref[...]
        o_ref[...] = e * (1.0 / d)

        @pl.when(i == 0)
        def _():
            acc_ref[...] = jnp.zeros_like(acc_ref)

        acc_ref[...] += _colsum(e * e)

        @pl.when(i == steps - 1)
        def _():
            l_ref[...] = jnp.full(l_ref.shape, 0.5 / d, F32) * jnp.sum(acc_ref[...])

    row = lambda i: (i, 0)
    out_shape = [jax.ShapeDtypeStruct((t, d), F32), jax.ShapeDtypeStruct((t, 2 * f), BF16),
                 jax.ShapeDtypeStruct((t, d), F32), jax.ShapeDtypeStruct((t, d), BF16)]
    out_specs = [pl.BlockSpec((tm, d), row), pl.BlockSpec((tm, 2 * f), row), pl.BlockSpec((tm, d), row),
                 pl.BlockSpec((tm, d), row)]
    weight_specs = [_resident(small.shape, lambda i: (0, 0)), _resident(wgu.shape, lambda i: (0, 0, 0)),
                    _resident(wd.shape, lambda i: (0, 0))]
    if with_loss:
        return _hosted(
            body, comm, name=f"fwd_ffn{layer}_loss", grid=(steps,),
            out_shape=out_shape + [jax.ShapeDtypeStruct((8, 128), F32)],
            in_specs=[pl.BlockSpec((tm, d), row), pl.BlockSpec((tm, d), row)] + weight_specs,
            out_specs=out_specs + [pl.BlockSpec((8, 128), lambda i: (0, 0))],
            scratch_shapes=[pltpu.VMEM((1, d), F32)], args=[h, target, small, wgu, wd])
    return _hosted(
        body, comm, name=f"fwd_ffn{layer}", grid=(steps,), out_shape=out_shape,
        in_specs=[pl.BlockSpec((tm, d), row)] + weight_specs, out_specs=out_specs, args=[h, small, wgu, wd])


def _fwd_conv(h, small, win, wout, comm=None):
    t, d = h.shape
    tm = _token_tile(t)
    pc = win.shape[-1]

    def body(h_ref, sm_ref, win_ref, wout_ref, o_ref, proj_ref, y_ref, n_ref, pj_ref, uext_ref):
        i = pl.program_id(0)

        @pl.when(i == 0)
        def _():
            uext_ref[0:CONV_HALO, :] = jnp.zeros((CONV_HALO, d), F32)

        @pl.when(i > 0)
        def _():
            uext_ref[0:CONV_HALO, :] = uext_ref[tm:tm + CONV_HALO, :]

        hv = h_ref[...]
        hh, _ = _rms(hv)
        n = (hh * sm_ref[4:5, :]).astype(BF16)
        n_ref[...] = n
        for k in range(N_CHIPS):
            pj_ref[:, k * pc:(k + 1) * pc] = _dot(n, win_ref[k])
        proj_ref[...] = pj_ref[...].astype(BF16)
        uext_ref[CONV_HALO:, :] = pj_ref[:, d:2 * d] * pj_ref[:, 2 * d:]
        taps = [sm_ref[8 + j:9 + j, :] for j in range(3)]
        full = uext_ref[...]
        conv = (full[CONV_HALO:] * taps[2] + pltpu.roll(full, 1, 0)[CONV_HALO:] * taps[1]
                + pltpu.roll(full, 2, 0)[CONV_HALO:] * taps[0])
        y = _dot((pj_ref[:, 0:d] * conv).astype(BF16), wout_ref[...])
        y_ref[...] = y
        yh, _ = _rms(y)
        o_ref[...] = hv + yh * sm_ref[5:6, :]

    row = lambda i: (i, 0)
    return _hosted(
        body, comm, name="fwd_conv", grid=(t // tm,),
        out_shape=[jax.ShapeDtypeStruct((t, d), F32), jax.ShapeDtypeStruct((t, 3 * d), BF16),
                   jax.ShapeDtypeStruct((t, d), F32), jax.ShapeDtypeStruct((t, d), BF16)],
        in_specs=[pl.BlockSpec((tm, d), row), _resident(small.shape, lambda i: (0, 0)),
                  _resident(win.shape, lambda i: (0, 0, 0)), _resident(wout.shape, lambda i: (0, 0))],
        out_specs=[pl.BlockSpec((tm, d), row), pl.BlockSpec((tm, 3 * d), row), pl.BlockSpec((tm, d), row),
                   pl.BlockSpec((tm, d), row)],
        scratch_shapes=[pltpu.VMEM((tm, 3 * d), F32), pltpu.VMEM((CONV_HALO + tm, d), F32)],
        args=[h, small, win, wout])


def _bwd_ffn(dh, h, ff, gu, small, wgu, wd, layer, comm=None):
    t, d = h.shape
    tm = _token_tile(t, 256)
    fc = wgu.shape[-1]
    f = 2 * fc
    g_in, g_out = 4 * layer + 2, 4 * layer + 3

    def body(dh_ref, h_ref, ff_ref, gu_ref, sm_ref, wgu_ref, wd_ref, o_ref, dgu_ref, dff_ref, act_ref, sg_ref):
        i = pl.program_id(0)

        @pl.when(i == 0)
        def _():
            sg_ref[...] = jnp.zeros_like(sg_ref)

        dy = dh_ref[...]
        fh, r3 = _rms(ff_ref[...])
        sg_ref[1:2, :] += _colsum(dy * fh)
        dff = _rms_bwd(dy, fh, r3, sm_ref[g_out:g_out + 1, :]).astype(BF16)
        dff_ref[...] = dff
        for j in range(2):
            dact = _dot_nt(dff, wd_ref[j * fc:(j + 1) * fc, :])
            gate = gu_ref[:, j * fc:(j + 1) * fc].astype(F32)
            up = gu_ref[:, f + j * fc:f + (j + 1) * fc].astype(F32)
            sig = jax.nn.sigmoid(gate)
            silu = gate * sig
            act_ref[:, j * fc:(j + 1) * fc] = (silu * up).astype(BF16)
            dgu_ref[:, j * fc:(j + 1) * fc] = (dact * up * (sig * (1.0 + gate * (1.0 - sig)))).astype(BF16)
            dgu_ref[:, f + j * fc:f + (j + 1) * fc] = (dact * silu).astype(BF16)
        dn = None
        for k in range(N_CHIPS):
            part = _dot_nt(dgu_ref[:, k * fc:(k + 1) * fc], wgu_ref[k])
            dn = part if dn is None else dn + part
        hh, r2 = _rms(h_ref[...])
        sg_ref[0:1, :] += _colsum(dn * hh)
        o_ref[...] = dy + _rms_bwd(dn, hh, r2, sm_ref[g_in:g_in + 1, :])

    row = lambda i: (i, 0)
    return _hosted(
        body, comm, name=f"bwd_ffn{layer}", grid=(t // tm,),
        out_shape=[jax.ShapeDtypeStruct((t, d), F32), jax.ShapeDtypeStruct((t, 2 * f), BF16),
                   jax.ShapeDtypeStruct((t, d), BF16), jax.ShapeDtypeStruct((t, f), BF16),
                   jax.ShapeDtypeStruct((8, d), F32)],
        in_specs=[pl.BlockSpec((tm, d), row), pl.BlockSpec((tm, d), row), pl.BlockSpec((tm, d), row),
                  pl.BlockSpec((tm, 2 * f), row), _resident(small.shape, lambda i: (0, 0)),
                  _resident(wgu.shape, lambda i: (0, 0, 0)), _resident(wd.shape, lambda i: (0, 0))],
        out_specs=[pl.BlockSpec((tm, d), row), pl.BlockSpec((tm, 2 * f), row), pl.BlockSpec((tm, d), row),
                   pl.BlockSpec((tm, f), row), pl.BlockSpec((8, d), lambda i: (0, 0))],
        args=[dh, h, ff, gu, small, wgu, wd])


def _bwd_conv(dh, h, y, proj, small, win, wout, comm=None):
    t, d = h.shape
    tm = _token_tile(t)
    steps = t // tm
    pc = win.shape[-1]
    halo_blocks = tm // 16

    def body(dh_ref, h_ref, y_ref, proj_ref, halo_ref, sm_ref, win_ref, wout_ref,
             o_ref, dproj_ref, dy_ref, bc_ref, sg_ref, uext_ref, dcext_ref, carry_ref):
        i = pl.program_id(0)
        tile = steps - 1 - i

        @pl.when(i == 0)
        def _():
            sg_ref[...] = jnp.zeros_like(sg_ref)
            carry_ref[...] = jnp.zeros_like(carry_ref)

        dy = dh_ref[...]
        yh, r1 = _rms(y_ref[...])
        sg_ref[1:2, :] += _colsum(dy * yh)
        dyv = _rms_bwd(dy, yh, r1, sm_ref[5:6, :]).astype(BF16)
        dy_ref[...] = dyv
        dbc = _dot_nt(dyv, wout_ref[...])
        b = proj_ref[:, 0:d].astype(F32)
        cg = proj_ref[:, d:2 * d].astype(F32)
        v = proj_ref[:, 2 * d:].astype(F32)
        halo = halo_ref[...].astype(F32)[16 - CONV_HALO:]
        uh = halo[:, d:2 * d] * halo[:, 2 * d:]
        uext_ref[0:CONV_HALO, :] = jnp.where(tile > 0, uh, jnp.zeros_like(uh))
        uext_ref[CONV_HALO:, :] = cg * v
        taps = [sm_ref[8 + j:9 + j, :] for j in range(3)]
        full = uext_ref[...]
        u0 = full[CONV_HALO:]
        u1 = pltpu.roll(full, 1, 0)[CONV_HALO:]
        u2 = pltpu.roll(full, 2, 0)[CONV_HALO:]
        conv = u0 * taps[2] + u1 * taps[1] + u2 * taps[0]
        bc_ref[...] = (b * conv).astype(BF16)
        dconv = dbc * b
        sg_ref[4:5, :] += _colsum(dconv * u0)
        sg_ref[3:4, :] += _colsum(dconv * u1)
        sg_ref[2:3, :] += _colsum(dconv * u2)
        dcext_ref[0:tm, :] = dconv
        dcext_ref[tm:, :] = carry_ref[...]
        carry_ref[...] = dconv[0:CONV_HALO]
        dfull = dcext_ref[...]
        n8 = tm + CONV_HALO
        du = (dfull[0:tm] * taps[2] + pltpu.roll(dfull, n8 - 1, 0)[0:tm] * taps[1]
              + pltpu.roll(dfull, n8 - 2, 0)[0:tm] * taps[0])
        dproj_ref[:, 0:d] = (dbc * conv).astype(BF16)
        dproj_ref[:, d:2 * d] = (du * v).astype(BF16)
        dproj_ref[:, 2 * d:] = (du * cg).astype(BF16)
        dn = None
        for k in range(N_CHIPS):
            part = _dot_nt(dproj_ref[:, k * pc:(k + 1) * pc], win_ref[k])
            dn = part if dn is None else dn + part
        hh, r0 = _rms(h_ref[...])
        sg_ref[0:1, :] += _colsum(dn * hh)
        o_ref[...] = dy + _rms_bwd(dn, hh, r0, sm_ref[4:5, :])

    rev = lambda i: (steps - 1 - i, 0)
    before = lambda i: (jnp.maximum((steps - 1 - i) * halo_blocks - 1, 0), 0)
    return _hosted(
        body, comm, name="bwd_conv", grid=(steps,),
        out_shape=[jax.ShapeDtypeStruct((t, d), F32), jax.ShapeDtypeStruct((t, 3 * d), BF16),
                   jax.ShapeDtypeStruct((t, d), BF16), jax.ShapeDtypeStruct((t, d), BF16),
                   jax.ShapeDtypeStruct((8, d), F32)],
        in_specs=[pl.BlockSpec((tm, d), rev), pl.BlockSpec((tm, d), rev), pl.BlockSpec((tm, d), rev),
                  pl.BlockSpec((tm, 3 * d), rev), pl.BlockSpec((16, 3 * d), before),
                  _resident(small.shape, lambda i: (0, 0)), _resident(win.shape, lambda i: (0, 0, 0)),
                  _resident(wout.shape, lambda i: (0, 0))],
        out_specs=[pl.BlockSpec((tm, d), rev), pl.BlockSpec((tm, 3 * d), rev), pl.BlockSpec((tm, d), rev),
                   pl.BlockSpec((tm, d), rev), pl.BlockSpec((8, d), lambda i: (0, 0))],
        scratch_shapes=[pltpu.VMEM((CONV_HALO + tm, d), F32), pltpu.VMEM((tm + CONV_HALO, d), F32),
                        pltpu.VMEM((CONV_HALO, d), F32)],
        args=[dh, h, y, proj, proj, small, win, wout])


def _bwd_pool(dh, x, small, scale, poolw, comm=None):
    t, d = x.shape
    tm = _token_tile(t)
    steps = t // tm
    ng = len(POOL_WINDOWS)
    gw = d // ng
    halo_blocks = tm // POOL_HALO

    def body(dh_ref, x_ref, halo_ref, sm_ref, sc_ref, w_ref, o_ref, dw_ref, sg_ref,
             ext_ref, mix_ref, mm_ref, pb_ref, qext_ref, dhn_ref, carry_ref):
        i = pl.program_id(0)
        tile = steps - 1 - i

        @pl.when(i == 0)
        def _():
            sg_ref[...] = jnp.zeros_like(sg_ref)
            dw_ref[...] = jnp.zeros_like(dw_ref)
            carry_ref[...] = jnp.zeros_like(carry_ref)

        g0 = sm_ref[0:1, :]
        xv = x_ref[...]
        xh, r0 = _rms(xv)
        hx, _ = _rms(halo_ref[...])
        ext_ref[0:POOL_HALO, :] = jnp.where(tile > 0, hx * g0, jnp.zeros_like(hx))
        ext_ref[POOL_HALO:, :] = xh * g0
        for g in range(ng):
            pooled = _pool_windows(ext_ref, g, gw, tm, tile * tm)
            cols = slice(g * gw, (g + 1) * gw)
            pb = pooled.astype(BF16)
            pb_ref[:, cols] = pb
            mm = _dot(pb, w_ref[g])
            mm_ref[:, cols] = mm
            mix_ref[:, cols] = mm * sc_ref[:, cols]
        dy = dh_ref[...]
        mh, r1 = _rms(mix_ref[...])
        sg_ref[1:2, :] += _colsum(dy * mh)
        dmix = _rms_bwd(dy, mh, r1, sm_ref[1:2, :])
        sg_ref[2:3, :] += _colsum(dmix * mm_ref[...])
        mix_ref[...] = dmix * sc_ref[...]
        n16 = tm + POOL_HALO
        for g in range(ng):
            w = POOL_WINDOWS[g]
            cols = slice(g * gw, (g + 1) * gw)
            dmm = mix_ref[:, cols].astype(BF16)
            dpooled = _dot_nt(dmm, w_ref[g])
            dw_ref[g] += _dot_tn(pb_ref[:, cols], dmm)
            trow = tile * tm + lax.broadcasted_iota(jnp.int32, (tm, 1), 0)
            q = dpooled / jnp.minimum(trow + 1, w).astype(F32)
            qext_ref[0:tm, cols] = q
            qext_ref[tm:, cols] = carry_ref[:, cols]
            carry_ref[:, cols] = q[0:POOL_HALO]
            p, k = qext_ref[:, cols], 1
            while k < w:
                p = p + pltpu.roll(p, n16 - k, 0)
                k *= 2
            dhn_ref[:, cols] = p[0:tm] - dpooled
        dhn = dhn_ref[...]
        sg_ref[0:1, :] += _colsum(dhn * xh)
        o_ref[...] = dy + _rms_bwd(dhn, xh, r0, g0)

    rev = lambda i: (steps - 1 - i, 0)
    before = lambda i: (jnp.maximum((steps - 1 - i) * halo_blocks - 1, 0), 0)
    return _hosted(
        body, comm, name="bwd_pool", grid=(steps,),
        out_shape=[jax.ShapeDtypeStruct((t, d), F32), jax.ShapeDtypeStruct((ng, gw, gw), F32),
                   jax.ShapeDtypeStruct((8, d), F32)],
        in_specs=[pl.BlockSpec((tm, d), rev), pl.BlockSpec((tm, d), rev), pl.BlockSpec((POOL_HALO, d), before),
                  _resident(small.shape, lambda i: (0, 0)), _resident(scale.shape, lambda i: (0, 0)),
                  _resident(poolw.shape, lambda i: (0, 0, 0))],
        out_specs=[pl.BlockSpec((tm, d), rev), pl.BlockSpec((ng, gw, gw), lambda i: (0, 0, 0)),
                   pl.BlockSpec((8, d), lambda i: (0, 0))],
        scratch_shapes=[pltpu.VMEM((POOL_HALO + tm, d), F32), pltpu.VMEM((tm, d), F32), pltpu.VMEM((tm, d), F32),
                        pltpu.VMEM((tm, d), BF16), pltpu.VMEM((tm + POOL_HALO, d), F32), pltpu.VMEM((tm, d), F32),
                        pltpu.VMEM((POOL_HALO, d), F32)],
        args=[dh, x, x, small, scale, poolw])


def _weight_grad(a, b, bm, bn, half_on, name, comm=None, rows=(0, 1)):
    t, m = a.shape
    _, n = b.shape
    if half_on == "a":
        a_cols, b_cols = 2 * bm, bn
    else:
        a_cols, b_cols = bm, 2 * bn
    steps = max(m // a_cols, n // b_cols)
    sub, n_sub = rows
    tr = bm // n_sub

    def spec(cols, total):
        if cols == total:
            return _resident((t, cols), lambda p, j: (0, 0))
        return pl.BlockSpec((t, cols), lambda p, j: (0, j))

    def tile(a_ref, b_ref, half):
        if half_on == "a":
            first = half * bm + sub * tr
            return _dot_tn(a_ref[:, first:first + tr], b_ref[...])
        return _dot_tn(a_ref[...], b_ref[:, half * bn:(half + 1) * bn])

    def body(a_ref, b_ref, parts_ref, land_ref, acc_ref, stage_ref, got_ref, send_sems, recv_sems, got_sem):
        p, j = pl.program_id(0), pl.program_id(1)
        x, y, c, _ = _place()
        half = jnp.where(p == 0, 1 - c, c)

        def send(jj):
            return _remote(stage_ref.at[jj % 2], land_ref.at[jj], send_sems.at[jj], recv_sems.at[jj], (x, y, 1 - c))

        def fetch():
            return pltpu.make_async_copy(land_ref.at[j], got_ref, got_sem)

        @pl.when(p == 1)
        def _():
            @pl.when(j == 0)
            def _():
                for jj in range(max(steps - 2, 0), steps):
                    send(jj).wait_send()

            send(j).wait_recv()
            fetch().start()

        for hv in range(2):
            @pl.when(half == hv)
            def _():
                acc_ref[...] = tile(a_ref, b_ref, hv)

        @pl.when(p == 0)
        def _():
            @pl.when(j >= 2)
            def _():
                send(j - 2).wait_send()

            stage_ref[j % 2] = acc_ref[...].astype(BF16)
            send(j).start()

        @pl.when(p == 1)
        def _():
            fetch().wait()
            parts_ref[...] = (acc_ref[...] + got_ref[...].astype(F32)).astype(BF16)

    (parts, _), got = _hosted(
        body, comm, name=name, grid=(2, steps),
        out_shape=[jax.ShapeDtypeStruct((steps, tr, bn), BF16), jax.ShapeDtypeStruct((steps, tr, bn), BF16)],
        in_specs=[spec(a_cols, m), spec(b_cols, n)],
        out_specs=[pl.BlockSpec((None, tr, bn), lambda p, j: (p * j, 0, 0)), ANY],
        scratch_shapes=[pltpu.VMEM((tr, bn), F32), pltpu.VMEM((2, tr, bn), BF16), pltpu.VMEM((tr, bn), BF16),
                        DMA((steps,)), DMA((steps,)), DMA],
        args=[a, b])
    return parts, got


def _adamw(w, g, m, v, name):
    r, c = w.shape
    rb = _row_block(r, c * (8 * 4 * 2 + 4 * 4))
    bc1 = 1.0 - ADAM_B1 ** ADAM_STEP
    bc2 = 1.0 - ADAM_B2 ** ADAM_STEP

    def body(w_ref, g_ref, m_ref, v_ref, d_ref, nm_ref, nv_ref, go_ref):
        gv = g_ref[...]
        go_ref[...] = gv
        nm = ADAM_B1 * m_ref[...] + (1.0 - ADAM_B1) * gv
        nv = ADAM_B2 * v_ref[...] + (1.0 - ADAM_B2) * (gv * gv)
        nm_ref[...] = nm
        nv_ref[...] = nv
        d_ref[...] = -ADAM_LR * ((nm / bc1) / (jnp.sqrt(nv / bc2) + ADAM_EPS) + ADAM_WD * w_ref[...])

    spec = pl.BlockSpec((rb, c), lambda i: (i, 0))
    return pl.pallas_call(
        body, name=name, grid=(r // rb,), out_shape=[jax.ShapeDtypeStruct((r, c), F32)] * 4,
        in_specs=[spec] * 4, out_specs=[spec] * 4,
        compiler_params=pltpu.CompilerParams(dimension_semantics=("parallel",), vmem_limit_bytes=VMEM_LIMIT),
    )(w, g, m, v)


def kernel(x, norm_gains, pool_w, pool_scale, conv_in_w, conv_w, conv_out_w, ffn_gate_up_w, ffn_down_w, loss_target, m_norm_gains, m_pool_w, m_pool_scale, m_conv_in_w, m_conv_w, m_conv_out_w, m_ffn_gate_up_w, m_ffn_down_w, v_norm_gains, v_pool_w, v_pool_scale, v_conv_in_w, v_conv_w, v_conv_out_w, v_ffn_gate_up_w, v_ffn_down_w):
    _, t, d = x.shape
    dq = d // N_CHIPS
    ng = len(POOL_WINDOWS)
    gw = d // ng
    fq = ffn_down_w.shape[1]
    f = N_CHIPS * fq
    fc = f // 2
    core = lax.axis_index("c")
    chip = 2 * lax.axis_index("x") + lax.axis_index("y")
    core_arr = jnp.reshape(core, (1,)).astype(jnp.int32)
    where_arr = jnp.stack([chip, core]).astype(jnp.int32)
    x2, target = x[0], loss_target[0]

    small_loc = jnp.concatenate(
        [norm_gains.reshape(8, dq), conv_w[0], jnp.zeros((5, dq), F32)], axis=0).reshape(1, 2, 8, dq)
    pool_loc = pool_w.astype(BF16).reshape(1, 2, ng // 2 * (gw // N_CHIPS), gw)
    wgu_loc = ffn_gate_up_w.astype(BF16).reshape(2, 2, d // 2, fc)
    wd_loc = ffn_down_w.astype(BF16).reshape(2, 2, fq // 2, d)
    win_loc = conv_in_w.astype(BF16).reshape(1, 2, d // 2, -1)
    wout_loc = conv_out_w.astype(BF16).reshape(1, 2, dq // 2, d)

    def ffn_weights(wgu_f, wd_f):
        return wgu_f.reshape(N_CHIPS, d, fc), wd_f.reshape(f, d)

    ag0 = _SplitGather([(pool_loc, 0), (small_loc, 0), (wgu_loc, 0), (wd_loc, 0)])
    ag0.start("ag_start_layer0")
    ag1 = _SplitGather([(win_loc, 0), (wout_loc, 0), (wgu_loc, 1), (wd_loc, 1)])
    ag1.start("ag_start_layer1")
    pool_f, small_f = _pass_on(ag0.wait([0, 1], ag1.token, "ag_wait_first"), "ag_pass_first")
    poolw = pool_f.reshape(N_CHIPS, ng, gw // N_CHIPS, gw).transpose(1, 0, 2, 3).reshape(ng, gw, gw)
    small = small_f.transpose(1, 2, 0, 3).reshape(16, d)
    h1, _ = _fwd_pool(x2, small, pool_scale, poolw)
    wgu0, wd0 = ffn_weights(*_pass_on(ag0.wait([2, 3], h1, "ag_wait_ffn0"), "ag_pass_ffn0"))
    (h2, gu0, ff0, n0), _ = _fwd_ffn(h1, small, wgu0, wd0, 0)
    win_f, wout_f = _pass_on(ag1.wait([0, 1], h2, "ag_wait_conv"), "ag_pass_conv")
    win_f, wout_f = win_f.reshape(N_CHIPS, d, -1), wout_f.reshape(d, d)
    (h3, proj, y, nc), _ = _fwd_conv(h2, small, win_f, wout_f)
    wgu1, wd1 = ffn_weights(*_pass_on(ag1.wait([2, 3], h3, "ag_wait_ffn1"), "ag_pass_ffn1"))
    (dh4, gu1, ff1, n1, loss_blk), _ = _fwd_ffn(h3, small, wgu1, wd1, 1, target=target)

    (dh3, dgu1, dff1, act1, sg_f1), _ = _bwd_ffn(dh4, h3, ff1, gu1, small, wgu1, wd1, 1)
    parts_d1, _ = _weight_grad(act1, dff1, fc, d // 2, "b", "dw_down1")
    parts_gu1, _ = _weight_grad(n1, dgu1, d // 2, fc, "a", "dw_gate_up1")
    ex_ffn1 = _SplitExchange([parts_d1.reshape(N_CHIPS, fq, d // 2), parts_gu1])
    started = ex_ffn1.start("rs_start_ffn1")
    (dh2, dproj, dyv, bcv, sg_c), _ = _bwd_conv(dh3, h2, y, proj, small, win_f, wout_f, _After(started))
    parts_in, _ = _weight_grad(nc, dproj, d // 2, 3 * d // N_CHIPS, "a", "dw_conv_in")
    parts_out, _ = _weight_grad(bcv, dyv, dq // 2, d, "a", "dw_conv_out")
    ex_conv = _SplitExchange([parts_in, parts_out])
    started = ex_conv.start("rs_start_conv")
    (dh1, dgu0, dff0, act0, sg_f0), _ = _bwd_ffn(dh2, h1, ff0, gu0, small, wgu0, wd0, 0, _After(started))
    parts_d0, _ = _weight_grad(act0, dff0, fc, d // 2, "b", "dw_down0")
    parts_gu0, _ = _weight_grad(n0, dgu0, d // 2, fc, "a", "dw_gate_up0")
    ex_ffn0 = _SplitExchange([parts_d0.reshape(N_CHIPS, fq, d // 2), parts_gu0])
    started = ex_ffn0.start("rs_start_ffn0")
    (grad_x, dpool, sg_p), _ = _bwd_pool(dh1, x2, small, pool_scale, poolw, _After(started))
    g_pool = dpool.astype(BF16).reshape(2, ng // 2, N_CHIPS, gw // N_CHIPS, gw).transpose(2, 0, 1, 3, 4).reshape(
        N_CHIPS, 2, ng // 2 * (gw // N_CHIPS), gw)
    small_g = jnp.concatenate(
        [sg_p[0:2], sg_f0[0:2], sg_c[0:2], sg_f1[0:2], sg_c[2:5], sg_p[2:3],
         jnp.broadcast_to(loss_blk[0:1, 0:1], (1, d)), jnp.zeros((3, d), F32)], axis=0)
    (land_p,) = _sibling_exchange([g_pool], None, "rs_sibling_pool")
    ex_pool = _SplitExchange([_add_sibling(g_pool, land_p, core_arr)])
    started = ex_pool.start("rs_start_pool")

    def update(w, g, m, v, name):
        flat = (-1, w.shape[-1])
        dl, m2, v2, g2 = _adamw(w.reshape(flat), g.reshape(flat), m.reshape(flat), v.reshape(flat), "adamw_" + name)
        return tuple(o.reshape(w.shape) for o in (g2, dl, m2, v2))

    (parts_d1, parts_gu1), (recv_d1, recv_gu1) = ex_ffn1.wait([started], "rs_wait_ffn1")
    (parts_in, parts_out), (recv_in, recv_out) = ex_conv.wait([started], "rs_wait_conv")
    gs_gu = _add_chips(parts_gu1, recv_gu1, where_arr, 1, 2)
    gs_d = _add_chips(parts_d1, recv_d1, where_arr, 1, 2, col_half=True)
    gs_in = _add_chips(parts_in, recv_in, where_arr)
    gs_out = _add_chips(parts_out, recv_out, where_arr)
    full_in, full_out = _sibling_share([gs_in, gs_out], [False, False], "rs_share_conv")
    up_in = update(conv_in_w, full_in.reshape(1, d, -1), m_conv_in_w, v_conv_in_w, "conv_in")
    up_out = update(conv_out_w, full_out.reshape(1, dq, d), m_conv_out_w, v_conv_out_w, "conv_out")
    done_first = [up_in[1], up_out[1], gs_gu, gs_d]
    (parts_d0, parts_gu0), (recv_d0, recv_gu0) = ex_ffn0.wait(done_first, "rs_wait_ffn0")
    (parts_p,), (recv_p,) = ex_pool.wait(done_first, "rs_wait_pool")
    (small_all,) = _sibling_exchange([], small_g, "rs_small_gather", after=recv_gu0)
    gs_gu = _add_chips(parts_gu0, recv_gu0, where_arr, 0, 2, gs_gu)
    gs_d = _add_chips(parts_d0, recv_d0, where_arr, 0, 2, gs_d, col_half=True)
    gs_pool = _add_chips(parts_p, recv_p, where_arr)
    full_gu, full_d, full_pool = _sibling_share([gs_gu, gs_d, gs_pool], [False, True, False], "rs_share_ffn", [small_all])
    small_sum = _sum_small(small_all)
    loss = small_sum[12, 0]
    mine = lax.dynamic_slice_in_dim(small_sum, chip * dq, dq, axis=1)

    ups = [
        update(norm_gains, mine[0:8].reshape(2, 4, dq), m_norm_gains, v_norm_gains, "gains"),
        update(pool_w, full_pool.reshape(1, ng, gw // N_CHIPS, gw), m_pool_w, v_pool_w, "pool_w"),
        update(pool_scale, small_sum[11:12], m_pool_scale, v_pool_scale, "pool_scale"),
        up_in,
        update(conv_w, mine[8:11].reshape(1, 3, dq), m_conv_w, v_conv_w, "taps"),
        up_out,
        update(ffn_gate_up_w, full_gu.reshape(2, d, fc), m_ffn_gate_up_w, v_ffn_gate_up_w, "gate_up"),
        update(ffn_down_w, full_d.reshape(2, fq, d), m_ffn_down_w, v_ffn_down_w, "down"),
    ]
    grads_out, deltas, new_ms, new_vs = (list(col) for col in zip(*ups))
    return (loss, grad_x[None], *grads_out, *deltas, *new_ms, *new_vs)
```

```python
import jax
import jax.numpy as jnp
from jax import lax
from jax.experimental import pallas as pl
from jax.experimental.pallas import tpu as pltpu

RMS_EPS = 1e-6
POOL_WINDOWS = (2, 4, 8, 16)
POOL_HALO = 16
CONV_HALO = 8
N_CHIPS = 4
N_DEV = 8
ADAM_LR = 0.001
ADAM_B1 = 0.9
ADAM_B2 = 0.999
ADAM_EPS = 1e-08
ADAM_WD = 0.01
ADAM_STEP = 10
VMEM_LIMIT = 56 * 2**20
STREAM_BUDGET = 24 * 2**20
MESH = pl.DeviceIdType.MESH
ANY = pl.BlockSpec(memory_space=pl.ANY)
DMA = pltpu.SemaphoreType.DMA
BF16 = jnp.bfloat16
F32 = jnp.float32


def _token_tile(t, rows=512):
    return min(rows, t)


def _rms(x):
    r = lax.rsqrt(jnp.mean(x * x, axis=-1, keepdims=True) + RMS_EPS)
    return x * r, r


def _rms_bwd(dy, xh, r, g):
    a = dy * g
    return r * (a - xh * jnp.mean(a * xh, axis=-1, keepdims=True))


def _dot(a, b):
    return jnp.dot(a, b, preferred_element_type=F32)


def _dot_nt(a, b):
    return lax.dot_general(a, b, (((1,), (1,)), ((), ())), preferred_element_type=F32)


def _dot_tn(a, b):
    return lax.dot_general(a, b, (((0,), (0,)), ((), ())), preferred_element_type=F32)


def _colsum(a):
    return jnp.sum(a, axis=0, keepdims=True)


def _resident(block, index_map):
    return pl.BlockSpec(block, index_map, pipeline_mode=pl.Buffered(1))


def _row_block(r, row_bytes):
    best = None
    for rb in range(16, r + 1, 16):
        if r % rb == 0 and rb * row_bytes <= STREAM_BUDGET:
            best = rb
    return best if best is not None else r


def _place():
    x, y, c = lax.axis_index("x"), lax.axis_index("y"), lax.axis_index("c")
    return x, y, c, 2 * x + y


def _dev(chip, core):
    return (chip // 2, chip % 2, core)


def _remote(src, dst, send_sem, recv_sem, device):
    return pltpu.make_async_remote_copy(src_ref=src, dst_ref=dst, send_sem=send_sem, recv_sem=recv_sem,
                                        device_id=device, device_id_type=MESH)


class _Gather:
    def __init__(self, shards):
        n = len(shards)
        self.args = [s for s, _ in shards]
        self.layers = [l for _, l in shards]
        self.out_shape = [jax.ShapeDtypeStruct((N_CHIPS,) + s.shape[1:], s.dtype) for s in self.args]
        self.sems = [DMA((n,)), DMA((n,)), DMA((n, 3)), DMA((n, 3)), DMA((n, 3)), DMA((n, 3))]

    def _own(self, loc, out, sems, a):
        x, y, c, k = _place()
        return _remote(loc[a].at[self.layers[a]], out[a].at[k], sems[0].at[a], sems[1].at[a], (x, y, 1 - c))

    def _ici(self, loc, out, sems, a, m, arrival):
        x, y, c, k = _place()
        dst = out[a].at[k ^ m, c] if arrival else out[a].at[k, c]
        return _remote(loc[a].at[self.layers[a], c], dst, sems[2].at[a, m - 1], sems[3].at[a, m - 1], _dev(k ^ m, c))

    def _forward(self, out, sems, a, m, arrival):
        x, y, c, k = _place()
        got = out[a].at[k ^ m, 1 - c] if arrival else out[a].at[k ^ m, c]
        return _remote(got, got, sems[4].at[a, m - 1], sems[5].at[a, m - 1], (x, y, 1 - c))

    def start(self, loc, out, sems):
        for a in range(len(self.args)):
            for m in range(1, N_CHIPS):
                self._ici(loc, out, sems, a, m, False).start()
            self._own(loc, out, sems, a).start()

    def finish(self, loc, out, sems):
        n = len(self.args)
        for a in range(n):
            for m in range(1, N_CHIPS):
                self._ici(loc, out, sems, a, m, True).wait_recv()
                self._forward(out, sems, a, m, False).start()
        for a in range(n):
            for m in range(1, N_CHIPS):
                self._forward(out, sems, a, m, True).wait_recv()
            self._own(loc, out, sems, a).wait_recv()
        for a in range(n):
            for m in range(1, N_CHIPS):
                self._ici(loc, out, sems, a, m, False).wait_send()
                self._forward(out, sems, a, m, False).wait_send()
            self._own(loc, out, sems, a).wait_send()


class _ChipExchange:
    def __init__(self, parts):
        n = len(parts)
        self.args = list(parts)
        self.out_shape = [jax.ShapeDtypeStruct((N_CHIPS - 1,) + p.shape[1:], p.dtype) for p in parts]
        self.sems = [DMA((n, 3)), DMA((n, 3))]

    def _copy(self, p, land, sems, a, m):
        x, y, c, k = _place()
        return _remote(p[a].at[k ^ m], land[a].at[m - 1], sems[0].at[a, m - 1], sems[1].at[a, m - 1], _dev(k ^ m, c))

    def start(self, p, land, sems):
        for a in range(len(self.args)):
            for m in range(1, N_CHIPS):
                self._copy(p, land, sems, a, m).start()

    def finish(self, p, land, sems):
        for a in range(len(self.args)):
            for m in range(1, N_CHIPS):
                self._copy(p, land, sems, a, m).wait_recv()
        for a in range(len(self.args)):
            for m in range(1, N_CHIPS):
                self._copy(p, land, sems, a, m).wait_send()


def _hosted(body, comm, *, name, grid, in_specs, out_specs, out_shape, args, scratch_shapes=()):
    ni, no, ns = len(in_specs), len(out_shape), len(scratch_shapes)
    if comm is None:
        res = pl.pallas_call(
            body, name=name, grid=grid, in_specs=list(in_specs), out_specs=list(out_specs), out_shape=list(out_shape),
            scratch_shapes=list(scratch_shapes),
            compiler_params=pltpu.CompilerParams(dimension_semantics=("arbitrary",) * len(grid), vmem_limit_bytes=VMEM_LIMIT),
        )(*args)
        return list(res), []
    nc, nco = len(comm.args), len(comm.out_shape)

    def full(*refs):
        cin = refs[ni:ni + nc]
        outs = refs[ni + nc:ni + nc + no]
        cout = refs[ni + nc + no:ni + nc + no + nco]
        scratch = refs[ni + nc + no + nco:ni + nc + no + nco + ns]
        csems = refs[ni + nc + no + nco + ns:]
        first = _all_of([pl.program_id(ax) == 0 for ax in range(len(grid))])
        last = _all_of([pl.program_id(ax) == grid[ax] - 1 for ax in range(len(grid))])

        @pl.when(first)
        def _():
            comm.start(cin, cout, csems)

        body(*refs[:ni], *outs, *scratch)

        @pl.when(last)
        def _():
            comm.finish(cin, cout, csems)

    res = pl.pallas_call(
        full, name=name, grid=grid, in_specs=[*in_specs, *[ANY] * nc], out_specs=[*out_specs, *[ANY] * nco],
        out_shape=[*out_shape, *comm.out_shape], scratch_shapes=[*scratch_shapes, *comm.sems],
        compiler_params=pltpu.CompilerParams(dimension_semantics=("arbitrary",) * len(grid), vmem_limit_bytes=VMEM_LIMIT,
                                             has_side_effects=True),
    )(*args, *comm.args)
    return list(res[:no]), list(res[no:])


def _all_of(conds):
    out = conds[0]
    for c in conds[1:]:
        out = jnp.logical_and(out, c)
    return out


class _After:
    def __init__(self, token):
        self.args, self.out_shape, self.sems = [token], [], []

    def start(self, *_):
        pass

    def finish(self, *_):
        pass


def _alone(comm, name):
    return _hosted(lambda: None, comm, name=name, grid=(1,), in_specs=[], out_specs=[], out_shape=[], args=[])[1]


HBM = pl.BlockSpec(memory_space=pltpu.HBM)
SEM = pl.BlockSpec(memory_space=pltpu.SEMAPHORE)
DATAFLOW = pltpu.SideEffectType.DATAFLOW_SIDE_EFFECTING


class _SplitGather:
    PER_ARRAY = 8

    def __init__(self, shards):
        self.plan = _Gather(shards)
        self.n = len(shards)

    @staticmethod
    def _tables(sems_of):
        class Table:
            def __init__(self, pick):
                self.pick = pick

            @property
            def at(self):
                return self

            def __getitem__(self, idx):
                return self.pick(idx)

        return [Table(lambda a: sems_of[a][0]), Table(lambda a: sems_of[a][1]),
                Table(lambda am: sems_of[am[0]][2 + am[1]]), Table(lambda am: sems_of[am[0]][5 + am[1]])]

    def start(self, name):
        n, plan, per = self.n, self.plan, self.PER_ARRAY

        def body(*refs):
            loc, land = refs[:n], refs[n:2 * n]
            sems_of = {a: refs[2 * n + per * a:2 * n + per * (a + 1)] for a in range(n)}
            plan.start(loc, land, self._tables(sems_of))
            refs[-1][...] = jnp.zeros_like(refs[-1])

        lands = [pltpu.with_memory_space_constraint(lax.empty(o.shape, o.dtype), pltpu.HBM) for o in plan.out_shape]
        locs = [pltpu.with_memory_space_constraint(a, pltpu.HBM) for a in plan.args]
        res = pl.pallas_call(
            body, name=name,
            out_shape=[*[DMA(())] * (per * n),
                       *[pltpu.HBM(o.shape, o.dtype) for o in plan.out_shape],
                       jax.ShapeDtypeStruct((8, 128), F32)],
            in_specs=[HBM] * (2 * n),
            out_specs=[SEM] * (per * n) + [HBM] * n + [pl.BlockSpec(memory_space=pltpu.VMEM)],
            input_output_aliases={n + i: per * n + i for i in range(n)},
            compiler_params=pltpu.CompilerParams(has_side_effects=DATAFLOW),
        )(*locs, *lands)
        self.sems = {a: list(res[per * a:per * (a + 1)]) for a in range(n)}
        self.locs = locs
        self.lands = list(res[per * n:per * n + n])
        self.token = res[-1]

    def wait(self, idxs, after, name):
        plan, g, per = self.plan, len(idxs), self.PER_ARRAY

        def body(*refs):
            loc = {a: refs[j] for j, a in enumerate(idxs)}
            land = {a: refs[g + j] for j, a in enumerate(idxs)}
            sems = self._tables({a: refs[2 * g + per * j:2 * g + per * (j + 1)] for j, a in enumerate(idxs)})
            for a in idxs:
                for m in range(1, N_CHIPS):
                    plan._ici(loc, land, sems, a, m, True).wait_recv()
                    plan._ici(loc, land, sems, a, m, False).wait_send()
                plan._own(loc, land, sems, a).wait_recv()
                plan._own(loc, land, sems, a).wait_send()

        res = pl.pallas_call(
            body, name=name,
            out_shape=[pltpu.HBM(self.lands[a].shape, self.lands[a].dtype) for a in idxs],
            in_specs=[HBM] * (2 * g) + [SEM] * (per * g) + [pl.BlockSpec(memory_space=pl.ANY)], out_specs=[HBM] * g,
            input_output_aliases={g + j: j for j in range(g)},
            compiler_params=pltpu.CompilerParams(has_side_effects=DATAFLOW),
        )(*[self.locs[a] for a in idxs], *[self.lands[a] for a in idxs],
          *[s for a in idxs for s in self.sems[a]], after)
        return list(res)


class _SplitExchange:
    PER_ARRAY = 6

    def __init__(self, parts):
        self.plan = _ChipExchange(parts)
        self.n = len(parts)

    @staticmethod
    def _tables(sems_of):
        class Table:
            def __init__(self, pick):
                self.pick = pick

            @property
            def at(self):
                return self

            def __getitem__(self, am):
                return self.pick(am)

        return [Table(lambda am: sems_of[am[0]][am[1]]), Table(lambda am: sems_of[am[0]][3 + am[1]])]

    def start(self, name):
        n, plan, per = self.n, self.plan, self.PER_ARRAY

        def body(*refs):
            p, land = refs[:n], refs[n:2 * n]
            sems_of = {a: refs[2 * n + per * a:2 * n + per * (a + 1)] for a in range(n)}
            plan.start(p, land, self._tables(sems_of))
            refs[-1][...] = jnp.zeros_like(refs[-1])

        lands = [pltpu.with_memory_space_constraint(lax.empty(o.shape, o.dtype), pltpu.HBM) for o in plan.out_shape]
        parts = [pltpu.with_memory_space_constraint(a, pltpu.HBM) for a in plan.args]
        res = pl.pallas_call(
            body, name=name,
            out_shape=[*[DMA(())] * (per * n),
                       *[pltpu.HBM(a.shape, a.dtype) for a in plan.args],
                       *[pltpu.HBM(o.shape, o.dtype) for o in plan.out_shape],
                       jax.ShapeDtypeStruct((8, 128), F32)],
            in_specs=[HBM] * (2 * n),
            out_specs=[SEM] * (per * n) + [HBM] * (2 * n) + [pl.BlockSpec(memory_space=pltpu.VMEM)],
            input_output_aliases={i: per * n + i for i in range(2 * n)},
            compiler_params=pltpu.CompilerParams(has_side_effects=DATAFLOW),
        )(*parts, *lands)
        self.sems = list(res[:per * n])
        self.parts = list(res[per * n:per * n + n])
        self.lands = list(res[per * n + n:per * n + 2 * n])
        return res[-1]

    def wait(self, after, name):
        n, plan, per = self.n, self.plan, self.PER_ARRAY

        def body(*refs):
            p, land = refs[:n], refs[n:2 * n]
            sems_of = {a: refs[2 * n + per * a:2 * n + per * (a + 1)] for a in range(n)}
            plan.finish(p, land, self._tables(sems_of))

        res = pl.pallas_call(
            body, name=name,
            out_shape=[*[pltpu.HBM(a.shape, a.dtype) for a in self.parts], *[pltpu.HBM(a.shape, a.dtype) for a in self.lands]],
            in_specs=[HBM] * (2 * n) + [SEM] * (per * n) + [pl.BlockSpec(memory_space=pl.ANY)] * len(after),
            out_specs=[HBM] * (2 * n), input_output_aliases={i: i for i in range(2 * n)},
            compiler_params=pltpu.CompilerParams(has_side_effects=DATAFLOW),
        )(*self.parts, *self.lands, *self.sems, *after)
        return list(res[:n]), list(res[n:])


PASS_ON_BARRIER = 1


def _sibling_barrier():
    x, y, c, _ = _place()
    barrier = pltpu.get_barrier_semaphore()
    pl.semaphore_signal(barrier, inc=1, device_id=(x, y, 1 - c), device_id_type=MESH)
    pl.semaphore_wait(barrier, 1)


def _pass_on(lands, name):
    n = len(lands)

    def body(*refs):
        out = refs[n:2 * n]
        send_sems, recv_sems = refs[2 * n:]
        x, y, c, k = _place()
        _sibling_barrier()
        cps = []
        for a in range(n):
            for m in range(1, N_CHIPS):
                got = out[a].at[k ^ m, c]
                cp = _remote(got, got, send_sems.at[a, m - 1], recv_sems.at[a, m - 1], (x, y, 1 - c))
                cp.start()
                cps.append(cp)
        for a in range(n):
            for m in range(1, N_CHIPS):
                theirs = out[a].at[k ^ m, 1 - c]
                _remote(theirs, theirs, send_sems.at[a, m - 1], recv_sems.at[a, m - 1], (x, y, 1 - c)).wait_recv()
        for cp in cps:
            cp.wait_send()

    return pl.pallas_call(
        body, name=name, out_shape=[jax.ShapeDtypeStruct(a.shape, a.dtype) for a in lands],
        in_specs=[ANY] * n, out_specs=[ANY] * n, input_output_aliases={a: a for a in range(n)},
        scratch_shapes=[DMA((n, 3)), DMA((n, 3))],
        compiler_params=pltpu.CompilerParams(has_side_effects=True, collective_id=PASS_ON_BARRIER),
    )(*lands)


def _sibling_exchange(grads, small, name, after=None):
    n = len(grads)
    ns = 0 if small is None else 1
    na = 0 if after is None else 1

    def body(*refs):
        g = refs[:n]
        land = refs[n + ns + na:2 * n + ns + na]
        send_sems, recv_sems, own_sem, ssend_sems, srecv_sems = refs[2 * n + 2 * ns + na:]
        x, y, c, k = _place()
        me = 2 * k + c
        cps = []
        for a in range(n):
            cp = _remote(g[a].at[:, pl.ds(1 - c, 1)], land[a], send_sems.at[a], recv_sems.at[a], (x, y, 1 - c))
            cp.start()
            cps.append(cp)
        if small is not None:
            sm, smg = refs[n], refs[2 * n + 1 + na]
            peers = [me ^ m for m in range(1, N_DEV)]
            ids = [(p // 4, (p // 2) % 2, p % 2) for p in peers]
            own = pltpu.make_async_copy(sm, smg.at[me], own_sem)
            own.start()
            for m in range(1, N_DEV):
                cp = _remote(sm, smg.at[me], ssend_sems.at[m - 1], srecv_sems.at[m - 1], ids[m - 1])
                cp.start()
                cps.append(cp)
            for m in range(1, N_DEV):
                _remote(sm, smg.at[peers[m - 1]], ssend_sems.at[m - 1], srecv_sems.at[m - 1], ids[m - 1]).wait_recv()
            own.wait()
        for cp in cps[:n]:
            cp.wait_recv()
        for cp in cps:
            cp.wait_send()

    out_shape = [jax.ShapeDtypeStruct((N_CHIPS, 1) + a.shape[2:], a.dtype) for a in grads]
    ins = list(grads)
    if small is not None:
        out_shape.append(jax.ShapeDtypeStruct((N_DEV,) + small.shape, small.dtype))
        ins.append(small)
    if after is not None:
        ins.append(after)
    return pl.pallas_call(
        body, name=name, out_shape=out_shape, in_specs=[ANY] * (n + ns + na), out_specs=[ANY] * (n + ns),
        scratch_shapes=[DMA((max(n, 1),)), DMA((max(n, 1),)), DMA, DMA((N_DEV - 1,)), DMA((N_DEV - 1,))],
        compiler_params=pltpu.CompilerParams(has_side_effects=True),
    )(*ins)


def _sibling_share(halves, col_half, name, after=()):
    n, na = len(halves), len(after)

    def body(*refs):
        out = refs[n + na:2 * n + na]
        send_sems, recv_sems = refs[2 * n + na:]
        x, y, c, k = _place()

        def half(a, core):
            if not col_half[a]:
                return out[a].at[:, pl.ds(core, 1)]
            cols = out[a].shape[-1] // 2
            return out[a].at[:, :, pl.ds(pl.multiple_of(core * cols, cols), cols)]

        cps = []
        for a in range(n):
            cp = _remote(half(a, c), half(a, c), send_sems.at[a], recv_sems.at[a], (x, y, 1 - c))
            cp.start()
            cps.append(cp)
        for a in range(n):
            _remote(half(a, 1 - c), half(a, 1 - c), send_sems.at[a], recv_sems.at[a], (x, y, 1 - c)).wait_recv()
        for cp in cps:
            cp.wait_send()

    out_shape = [jax.ShapeDtypeStruct(a.shape, a.dtype) for a in halves]
    return pl.pallas_call(
        body, name=name, out_shape=out_shape, in_specs=[ANY] * (n + na), out_specs=[ANY] * n,
        input_output_aliases={a: a for a in range(n)}, scratch_shapes=[DMA((n,)), DMA((n,))],
        compiler_params=pltpu.CompilerParams(has_side_effects=True),
    )(*halves, *after)


def _add_sibling(g, land, core):
    _, _, r, c = g.shape
    rb = _row_block(r, c * (3 * 2 * 2 + 2 * 4))

    def body(core_ref, g_ref, l_ref, o_ref):
        o_ref[...] = (g_ref[...].astype(F32) + l_ref[...].astype(F32)).astype(o_ref.dtype)

    return pl.pallas_call(
        body, name="rs_add_sibling", out_shape=jax.ShapeDtypeStruct((N_CHIPS, r, c), g.dtype),
        grid_spec=pltpu.PrefetchScalarGridSpec(
            num_scalar_prefetch=1, grid=(N_CHIPS, r // rb),
            in_specs=[pl.BlockSpec((None, None, rb, c), lambda j, i, core_ref: (j, core_ref[0], i, 0)),
                      pl.BlockSpec((None, None, rb, c), lambda j, i, core_ref: (j, 0, i, 0))],
            out_specs=pl.BlockSpec((None, rb, c), lambda j, i, core_ref: (j, i, 0))),
        compiler_params=pltpu.CompilerParams(dimension_semantics=("parallel", "parallel"), vmem_limit_bytes=VMEM_LIMIT),
    )(core, g, land)


def _add_chips(part, land, where, layer=0, n_layers=1, into=None, col_half=False, rows=(0, 1)):
    _, r, c = part.shape
    sub, n_sub = rows
    rb = _row_block(r, c * (4 * 2 * 2 + 4 * 2 + 2 * 4))

    def body(where_ref, p_ref, l_ref, *rest):
        acc = p_ref[...].astype(F32)
        for m in range(N_CHIPS - 1):
            acc = acc + l_ref[m].astype(F32)
        rest[-1][...] = acc

    in_specs = [pl.BlockSpec((None, rb, c), lambda i, where_ref: (where_ref[0], i, 0)),
                pl.BlockSpec((N_CHIPS - 1, rb, c), lambda i, where_ref: (0, i, 0))]
    args = [where, part, land]
    if into is not None:
        in_specs.append(ANY)
        args.append(into)
    if col_half:
        out_shape = jax.ShapeDtypeStruct((n_layers, r, 2 * c), F32)
        out_spec = pl.BlockSpec((None, rb, c), lambda i, where_ref: (layer, i, where_ref[1]))
    else:
        out_shape = jax.ShapeDtypeStruct((n_layers, 2, n_sub * r, c), F32)
        out_spec = pl.BlockSpec((None, None, rb, c), lambda i, where_ref: (layer, where_ref[1], sub * (r // rb) + i, 0))
    return pl.pallas_call(
        body, name="rs_add_chips", out_shape=out_shape,
        grid_spec=pltpu.PrefetchScalarGridSpec(
            num_scalar_prefetch=1, grid=(r // rb,), in_specs=in_specs, out_specs=out_spec),
        input_output_aliases={} if into is None else {3: 0},
        compiler_params=pltpu.CompilerParams(dimension_semantics=("parallel",), vmem_limit_bytes=VMEM_LIMIT),
    )(*args)


def _sum_small(smg):
    def body(s_ref, o_ref):
        acc = s_ref[0]
        for j in range(1, N_DEV):
            acc = acc + s_ref[j]
        o_ref[...] = acc

    return pl.pallas_call(body, name="rs_sum_small", out_shape=jax.ShapeDtypeStruct(smg.shape[1:], F32))(smg)


def _pool_windows(ext_ref, g, gw, tm, first_row):
    w = POOL_WINDOWS[g]
    slab = ext_ref[:, g * gw:(g + 1) * gw]
    p, k = slab, 1
    while k < w:
        p = p + pltpu.roll(p, k, 0)
        k *= 2
    t = first_row + lax.broadcasted_iota(jnp.int32, (tm, 1), 0)
    cnt = jnp.minimum(t + 1, w).astype(F32)
    return p[POOL_HALO:] / cnt - slab[POOL_HALO:]


def _fwd_pool(x, small, scale, poolw, comm=None):
    t, d = x.shape
    tm = _token_tile(t)
    gw = d // len(POOL_WINDOWS)

    def body(x_ref, sm_ref, sc_ref, w_ref, h_ref, ext_ref, mix_ref):
        i = pl.program_id(0)

        @pl.when(i == 0)
        def _():
            ext_ref[0:POOL_HALO, :] = jnp.zeros((POOL_HALO, d), F32)

        @pl.when(i > 0)
        def _():
            ext_ref[0:POOL_HALO, :] = ext_ref[tm:tm + POOL_HALO, :]

        xv = x_ref[...]
        xh, _ = _rms(xv)
        ext_ref[POOL_HALO:, :] = xh * sm_ref[0:1, :]
        for g in range(len(POOL_WINDOWS)):
            pooled = _pool_windows(ext_ref, g, gw, tm, i * tm)
            cols = slice(g * gw, (g + 1) * gw)
            mix_ref[:, cols] = _dot(pooled.astype(BF16), w_ref[g]) * sc_ref[:, cols]
        mh, _ = _rms(mix_ref[...])
        h_ref[...] = xv + mh * sm_ref[1:2, :]

    (h,), got = _hosted(
        body, comm, name="fwd_pool", grid=(t // tm,), out_shape=[jax.ShapeDtypeStruct((t, d), F32)],
        in_specs=[pl.BlockSpec((tm, d), lambda i: (i, 0)), _resident(small.shape, lambda i: (0, 0)),
                  _resident(scale.shape, lambda i: (0, 0)), _resident(poolw.shape, lambda i: (0, 0, 0))],
        out_specs=[pl.BlockSpec((tm, d), lambda i: (i, 0))],
        scratch_shapes=[pltpu.VMEM((POOL_HALO + tm, d), F32), pltpu.VMEM((tm, d), F32)],
        args=[x, small, scale, poolw])
    return h, got


def _fwd_ffn(h, small, wgu, wd, layer, comm=None, target=None):
    t, d = h.shape
    tm = _token_tile(t)
    steps = t // tm
    fc = wgu.shape[-1]
    f = 2 * fc
    g_in, g_out = 4 * layer + 2, 4 * layer + 3
    with_loss = target is not None

    def body(h_ref, *refs):
        if with_loss:
            t_ref, sm_ref, wgu_ref, wd_ref, o_ref, gu_ref, ff_ref, n_ref, l_ref, acc_ref = refs
        else:
            sm_ref, wgu_ref, wd_ref, o_ref, gu_ref, ff_ref, n_ref = refs
        hv = h_ref[...]
        hh, _ = _rms(hv)
        n = (hh * sm_ref[g_in:g_in + 1, :]).astype(BF16)
        n_ref[...] = n
        ff = None
        for j in range(2):
            gate = _dot(n, wgu_ref[j])
            up = _dot(n, wgu_ref[2 + j])
            gu_ref[:, j * fc:(j + 1) * fc] = gate.astype(BF16)
            gu_ref[:, f + j * fc:f + (j + 1) * fc] = up.astype(BF16)
            act = (gate * jax.nn.sigmoid(gate) * up).astype(BF16)
            part = _dot(act, wd_ref[j * fc:(j + 1) * fc, :])
            ff = part if ff is None else ff + part
        ff_ref[...] = ff
        fh, _ = _rms(ff)
        out = hv + fh * sm_ref[g_out:g_out + 1, :]
        if not with_loss:
            o_ref[...] = out
            return
        i = pl.program_id(0)
        e = out - t_ref[...]
        o_ref[...] = e * (1.0 / d)

        @pl.when(i == 0)
        def _():
            acc_ref[...] = jnp.zeros_like(acc_ref)

        acc_ref[...] += _colsum(e * e)

        @pl.when(i == steps - 1)
        def _():
            l_ref[...] = jnp.full(l_ref.shape, 0.5 / d, F32) * jnp.sum(acc_ref[...])

    row = lambda i: (i, 0)
    out_shape = [jax.ShapeDtypeStruct((t, d), F32), jax.ShapeDtypeStruct((t, 2 * f), BF16),
                 jax.ShapeDtypeStruct((t, d), F32), jax.ShapeDtypeStruct((t, d), BF16)]
    out_specs = [pl.BlockSpec((tm, d), row), pl.BlockSpec((tm, 2 * f), row), pl.BlockSpec((tm, d), row),
                 pl.BlockSpec((tm, d), row)]
    weight_specs = [_resident(small.shape, lambda i: (0, 0)), _resident(wgu.shape, lambda i: (0, 0, 0)),
                    _resident(wd.shape, lambda i: (0, 0))]
    if with_loss:
        return _hosted(
            body, comm, name=f"fwd_ffn{layer}_loss", grid=(steps,),
            out_shape=out_shape + [jax.ShapeDtypeStruct((8, 128), F32)],
            in_specs=[pl.BlockSpec((tm, d), row), pl.BlockSpec((tm, d), row)] + weight_specs,
            out_specs=out_specs + [pl.BlockSpec((8, 128), lambda i: (0, 0))],
            scratch_shapes=[pltpu.VMEM((1, d), F32)], args=[h, target, small, wgu, wd])
    return _hosted(
        body, comm, name=f"fwd_ffn{layer}", grid=(steps,), out_shape=out_shape,
        in_specs=[pl.BlockSpec((tm, d), row)] + weight_specs, out_specs=out_specs, args=[h, small, wgu, wd])


def _fwd_conv(h, small, win, wout, comm=None):
    t, d = h.shape
    tm = _token_tile(t)
    pc = win.shape[-1]

    def body(h_ref, sm_ref, win_ref, wout_ref, o_ref, proj_ref, y_ref, n_ref, pj_ref, uext_ref):
        i = pl.program_id(0)

        @pl.when(i == 0)
        def _():
            uext_ref[0:CONV_HALO, :] = jnp.zeros((CONV_HALO, d), F32)

        @pl.when(i > 0)
        def _():
            uext_ref[0:CONV_HALO, :] = uext_ref[tm:tm + CONV_HALO, :]

        hv = h_ref[...]
        hh, _ = _rms(hv)
        n = (hh * sm_ref[4:5, :]).astype(BF16)
        n_ref[...] = n
        for k in range(N_CHIPS):
            pj_ref[:, k * pc:(k + 1) * pc] = _dot(n, win_ref[k])
        proj_ref[...] = pj_ref[...].astype(BF16)
        uext_ref[CONV_HALO:, :] = pj_ref[:, d:2 * d] * pj_ref[:, 2 * d:]
        taps = [sm_ref[8 + j:9 + j, :] for j in range(3)]
        full = uext_ref[...]
        conv = (full[CONV_HALO:] * taps[2] + pltpu.roll(full, 1, 0)[CONV_HALO:] * taps[1]
                + pltpu.roll(full, 2, 0)[CONV_HALO:] * taps[0])
        y = _dot((pj_ref[:, 0:d] * conv).astype(BF16), wout_ref[...])
        y_ref[...] = y
        yh, _ = _rms(y)
        o_ref[...] = hv + yh * sm_ref[5:6, :]

    row = lambda i: (i, 0)
    return _hosted(
        body, comm, name="fwd_conv", grid=(t // tm,),
        out_shape=[jax.ShapeDtypeStruct((t, d), F32), jax.ShapeDtypeStruct((t, 3 * d), BF16),
                   jax.ShapeDtypeStruct((t, d), F32), jax.ShapeDtypeStruct((t, d), BF16)],
        in_specs=[pl.BlockSpec((tm, d), row), _resident(small.shape, lambda i: (0, 0)),
                  _resident(win.shape, lambda i: (0, 0, 0)), _resident(wout.shape, lambda i: (0, 0))],
        out_specs=[pl.BlockSpec((tm, d), row), pl.BlockSpec((tm, 3 * d), row), pl.BlockSpec((tm, d), row),
                   pl.BlockSpec((tm, d), row)],
        scratch_shapes=[pltpu.VMEM((tm, 3 * d), F32), pltpu.VMEM((CONV_HALO + tm, d), F32)],
        args=[h, small, win, wout])


def _bwd_ffn(dh, h, ff, gu, small, wgu, wd, layer, comm=None):
    t, d = h.shape
    tm = _token_tile(t, 256)
    fc = wgu.shape[-1]
    f = 2 * fc
    g_in, g_out = 4 * layer + 2, 4 * layer + 3

    def body(dh_ref, h_ref, ff_ref, gu_ref, sm_ref, wgu_ref, wd_ref, o_ref, dgu_ref, dff_ref, act_ref, sg_ref):
        i = pl.program_id(0)

        @pl.when(i == 0)
        def _():
            sg_ref[...] = jnp.zeros_like(sg_ref)

        dy = dh_ref[...]
        fh, r3 = _rms(ff_ref[...])
        sg_ref[1:2, :] += _colsum(dy * fh)
        dff = _rms_bwd(dy, fh, r3, sm_ref[g_out:g_out + 1, :]).astype(BF16)
        dff_ref[...] = dff
        for j in range(2):
            dact = _dot_nt(dff, wd_ref[j * fc:(j + 1) * fc, :])
            gate = gu_ref[:, j * fc:(j + 1) * fc].astype(F32)
            up = gu_ref[:, f + j * fc:f + (j + 1) * fc].astype(F32)
            sig = jax.nn.sigmoid(gate)
            silu = gate * sig
            act_ref[:, j * fc:(j + 1) * fc] = (silu * up).astype(BF16)
            dgu_ref[:, j * fc:(j + 1) * fc] = (dact * up * (sig * (1.0 + gate * (1.0 - sig)))).astype(BF16)
            dgu_ref[:, f + j * fc:f + (j + 1) * fc] = (dact * silu).astype(BF16)
        dn = None
        for k in range(N_CHIPS):
            part = _dot_nt(dgu_ref[:, k * fc:(k + 1) * fc], wgu_ref[k])
            dn = part if dn is None else dn + part
        hh, r2 = _rms(h_ref[...])
        sg_ref[0:1, :] += _colsum(dn * hh)
        o_ref[...] = dy + _rms_bwd(dn, hh, r2, sm_ref[g_in:g_in + 1, :])

    row = lambda i: (i, 0)
    return _hosted(
        body, comm, name=f"bwd_ffn{layer}", grid=(t // tm,),
        out_shape=[jax.ShapeDtypeStruct((t, d), F32), jax.ShapeDtypeStruct((t, 2 * f), BF16),
                   jax.ShapeDtypeStruct((t, d), BF16), jax.ShapeDtypeStruct((t, f), BF16),
                   jax.ShapeDtypeStruct((8, d), F32)],
        in_specs=[pl.BlockSpec((tm, d), row), pl.BlockSpec((tm, d), row), pl.BlockSpec((tm, d), row),
                  pl.BlockSpec((tm, 2 * f), row), _resident(small.shape, lambda i: (0, 0)),
                  _resident(wgu.shape, lambda i: (0, 0, 0)), _resident(wd.shape, lambda i: (0, 0))],
        out_specs=[pl.BlockSpec((tm, d), row), pl.BlockSpec((tm, 2 * f), row), pl.BlockSpec((tm, d), row),
                   pl.BlockSpec((tm, f), row), pl.BlockSpec((8, d), lambda i: (0, 0))],
        args=[dh, h, ff, gu, small, wgu, wd])


def _bwd_conv(dh, h, y, proj, small, win, wout, comm=None):
    t, d = h.shape
    tm = _token_tile(t)
    steps = t // tm
    pc = win.shape[-1]
    halo_blocks = tm // 16

    def body(dh_ref, h_ref, y_ref, proj_ref, halo_ref, sm_ref, win_ref, wout_ref,
             o_ref, dproj_ref, dy_ref, bc_ref, sg_ref, uext_ref, dcext_ref, carry_ref):
        i = pl.program_id(0)
        tile = steps - 1 - i

        @pl.when(i == 0)
        def _():
            sg_ref[...] = jnp.zeros_like(sg_ref)
            carry_ref[...] = jnp.zeros_like(carry_ref)

        dy = dh_ref[...]
        yh, r1 = _rms(y_ref[...])
        sg_ref[1:2, :] += _colsum(dy * yh)
        dyv = _rms_bwd(dy, yh, r1, sm_ref[5:6, :]).astype(BF16)
        dy_ref[...] = dyv
        dbc = _dot_nt(dyv, wout_ref[...])
        b = proj_ref[:, 0:d].astype(F32)
        cg = proj_ref[:, d:2 * d].astype(F32)
        v = proj_ref[:, 2 * d:].astype(F32)
        halo = halo_ref[...].astype(F32)[16 - CONV_HALO:]
        uh = halo[:, d:2 * d] * halo[:, 2 * d:]
        uext_ref[0:CONV_HALO, :] = jnp.where(tile > 0, uh, jnp.zeros_like(uh))
        uext_ref[CONV_HALO:, :] = cg * v
        taps = [sm_ref[8 + j:9 + j, :] for j in range(3)]
        full = uext_ref[...]
        u0 = full[CONV_HALO:]
        u1 = pltpu.roll(full, 1, 0)[CONV_HALO:]
        u2 = pltpu.roll(full, 2, 0)[CONV_HALO:]
        conv = u0 * taps[2] + u1 * taps[1] + u2 * taps[0]
        bc_ref[...] = (b * conv).astype(BF16)
        dconv = dbc * b
        sg_ref[4:5, :] += _colsum(dconv * u0)
        sg_ref[3:4, :] += _colsum(dconv * u1)
        sg_ref[2:3, :] += _colsum(dconv * u2)
        dcext_ref[0:tm, :] = dconv
        dcext_ref[tm:, :] = carry_ref[...]
        carry_ref[...] = dconv[0:CONV_HALO]
        dfull = dcext_ref[...]
        n8 = tm + CONV_HALO
        du = (dfull[0:tm] * taps[2] + pltpu.roll(dfull, n8 - 1, 0)[0:tm] * taps[1]
              + pltpu.roll(dfull, n8 - 2, 0)[0:tm] * taps[0])
        dproj_ref[:, 0:d] = (dbc * conv).astype(BF16)
        dproj_ref[:, d:2 * d] = (du * v).astype(BF16)
        dproj_ref[:, 2 * d:] = (du * cg).astype(BF16)
        dn = None
        for k in range(N_CHIPS):
            part = _dot_nt(dproj_ref[:, k * pc:(k + 1) * pc], win_ref[k])
            dn = part if dn is None else dn + part
        hh, r0 = _rms(h_ref[...])
        sg_ref[0:1, :] += _colsum(dn * hh)
        o_ref[...] = dy + _rms_bwd(dn, hh, r0, sm_ref[4:5, :])

    rev = lambda i: (steps - 1 - i, 0)
    before = lambda i: (jnp.maximum((steps - 1 - i) * halo_blocks - 1, 0), 0)
    return _hosted(
        body, comm, name="bwd_conv", grid=(steps,),
        out_shape=[jax.ShapeDtypeStruct((t, d), F32), jax.ShapeDtypeStruct((t, 3 * d), BF16),
                   jax.ShapeDtypeStruct((t, d), BF16), jax.ShapeDtypeStruct((t, d), BF16),
                   jax.ShapeDtypeStruct((8, d), F32)],
        in_specs=[pl.BlockSpec((tm, d), rev), pl.BlockSpec((tm, d), rev), pl.BlockSpec((tm, d), rev),
                  pl.BlockSpec((tm, 3 * d), rev), pl.BlockSpec((16, 3 * d), before),
                  _resident(small.shape, lambda i: (0, 0)), _resident(win.shape, lambda i: (0, 0, 0)),
                  _resident(wout.shape, lambda i: (0, 0))],
        out_specs=[pl.BlockSpec((tm, d), rev), pl.BlockSpec((tm, 3 * d), rev), pl.BlockSpec((tm, d), rev),
                   pl.BlockSpec((tm, d), rev), pl.BlockSpec((8, d), lambda i: (0, 0))],
        scratch_shapes=[pltpu.VMEM((CONV_HALO + tm, d), F32), pltpu.VMEM((tm + CONV_HALO, d), F32),
                        pltpu.VMEM((CONV_HALO, d), F32)],
        args=[dh, h, y, proj, proj, small, win, wout])


def _bwd_pool(dh, x, small, scale, poolw, comm=None):
    t, d = x.shape
    tm = _token_tile(t)
    steps = t // tm
    ng = len(POOL_WINDOWS)
    gw = d // ng
    halo_blocks = tm // POOL_HALO

    def body(dh_ref, x_ref, halo_ref, sm_ref, sc_ref, w_ref, o_ref, dw_ref, sg_ref,
             ext_ref, mix_ref, mm_ref, pb_ref, qext_ref, dhn_ref, carry_ref):
        i = pl.program_id(0)
        tile = steps - 1 - i

        @pl.when(i == 0)
        def _():
            sg_ref[...] = jnp.zeros_like(sg_ref)
            dw_ref[...] = jnp.zeros_like(dw_ref)
            carry_ref[...] = jnp.zeros_like(carry_ref)

        g0 = sm_ref[0:1, :]
        xv = x_ref[...]
        xh, r0 = _rms(xv)
        hx, _ = _rms(halo_ref[...])
        ext_ref[0:POOL_HALO, :] = jnp.where(tile > 0, hx * g0, jnp.zeros_like(hx))
        ext_ref[POOL_HALO:, :] = xh * g0
        for g in range(ng):
            pooled = _pool_windows(ext_ref, g, gw, tm, tile * tm)
            cols = slice(g * gw, (g + 1) * gw)
            pb = pooled.astype(BF16)
            pb_ref[:, cols] = pb
            mm = _dot(pb, w_ref[g])
            mm_ref[:, cols] = mm
            mix_ref[:, cols] = mm * sc_ref[:, cols]
        dy = dh_ref[...]
        mh, r1 = _rms(mix_ref[...])
        sg_ref[1:2, :] += _colsum(dy * mh)
        dmix = _rms_bwd(dy, mh, r1, sm_ref[1:2, :])
        sg_ref[2:3, :] += _colsum(dmix * mm_ref[...])
        mix_ref[...] = dmix * sc_ref[...]
        n16 = tm + POOL_HALO
        for g in range(ng):
            w = POOL_WINDOWS[g]
            cols = slice(g * gw, (g + 1) * gw)
            dmm = mix_ref[:, cols].astype(BF16)
            dpooled = _dot_nt(dmm, w_ref[g])
            dw_ref[g] += _dot_tn(pb_ref[:, cols], dmm)
            trow = tile * tm + lax.broadcasted_iota(jnp.int32, (tm, 1), 0)
            q = dpooled / jnp.minimum(trow + 1, w).astype(F32)
            qext_ref[0:tm, cols] = q
            qext_ref[tm:, cols] = carry_ref[:, cols]
            carry_ref[:, cols] = q[0:POOL_HALO]
            p, k = qext_ref[:, cols], 1
            while k < w:
                p = p + pltpu.roll(p, n16 - k, 0)
                k *= 2
            dhn_ref[:, cols] = p[0:tm] - dpooled
        dhn = dhn_ref[...]
        sg_ref[0:1, :] += _colsum(dhn * xh)
        o_ref[...] = dy + _rms_bwd(dhn, xh, r0, g0)

    rev = lambda i: (steps - 1 - i, 0)
    before = lambda i: (jnp.maximum((steps - 1 - i) * halo_blocks - 1, 0), 0)
    return _hosted(
        body, comm, name="bwd_pool", grid=(steps,),
        out_shape=[jax.ShapeDtypeStruct((t, d), F32), jax.ShapeDtypeStruct((ng, gw, gw), F32),
                   jax.ShapeDtypeStruct((8, d), F32)],
        in_specs=[pl.BlockSpec((tm, d), rev), pl.BlockSpec((tm, d), rev), pl.BlockSpec((POOL_HALO, d), before),
                  _resident(small.shape, lambda i: (0, 0)), _resident(scale.shape, lambda i: (0, 0)),
                  _resident(poolw.shape, lambda i: (0, 0, 0))],
        out_specs=[pl.BlockSpec((tm, d), rev), pl.BlockSpec((ng, gw, gw), lambda i: (0, 0, 0)),
                   pl.BlockSpec((8, d), lambda i: (0, 0))],
        scratch_shapes=[pltpu.VMEM((POOL_HALO + tm, d), F32), pltpu.VMEM((tm, d), F32), pltpu.VMEM((tm, d), F32),
                        pltpu.VMEM((tm, d), BF16), pltpu.VMEM((tm + POOL_HALO, d), F32), pltpu.VMEM((tm, d), F32),
                        pltpu.VMEM((POOL_HALO, d), F32)],
        args=[dh, x, x, small, scale, poolw])


def _weight_grad(a, b, bm, bn, half_on, name, comm=None, rows=(0, 1)):
    t, m = a.shape
    _, n = b.shape
    if half_on == "a":
        a_cols, b_cols = 2 * bm, bn
    else:
        a_cols, b_cols = bm, 2 * bn
    steps = max(m // a_cols, n // b_cols)
    sub, n_sub = rows
    tr = bm // n_sub

    def spec(cols, total):
        if cols == total:
            return _resident((t, cols), lambda p, j: (0, 0))
        return pl.BlockSpec((t, cols), lambda p, j: (0, j))

    def tile(a_ref, b_ref, half):
        if half_on == "a":
            first = half * bm + sub * tr
            return _dot_tn(a_ref[:, first:first + tr], b_ref[...])
        return _dot_tn(a_ref[...], b_ref[:, half * bn:(half + 1) * bn])

    def body(a_ref, b_ref, parts_ref, land_ref, acc_ref, stage_ref, got_ref, send_sems, recv_sems, got_sem):
        p, j = pl.program_id(0), pl.program_id(1)
        x, y, c, _ = _place()
        half = jnp.where(p == 0, 1 - c, c)

        def send(jj):
            return _remote(stage_ref.at[jj % 2], land_ref.at[jj], send_sems.at[jj], recv_sems.at[jj], (x, y, 1 - c))

        def fetch():
            return pltpu.make_async_copy(land_ref.at[j], got_ref, got_sem)

        @pl.when(p == 1)
        def _():
            @pl.when(j == 0)
            def _():
                for jj in range(max(steps - 2, 0), steps):
                    send(jj).wait_send()

            send(j).wait_recv()
            fetch().start()

        for hv in range(2):
            @pl.when(half == hv)
            def _():
                acc_ref[...] = tile(a_ref, b_ref, hv)

        @pl.when(p == 0)
        def _():
            @pl.when(j >= 2)
            def _():
                send(j - 2).wait_send()

            stage_ref[j % 2] = acc_ref[...].astype(BF16)
            send(j).start()

        @pl.when(p == 1)
        def _():
            fetch().wait()
            parts_ref[...] = (acc_ref[...] + got_ref[...].astype(F32)).astype(BF16)

    (parts, _), got = _hosted(
        body, comm, name=name, grid=(2, steps),
        out_shape=[jax.ShapeDtypeStruct((steps, tr, bn), BF16), jax.ShapeDtypeStruct((steps, tr, bn), BF16)],
        in_specs=[spec(a_cols, m), spec(b_cols, n)],
        out_specs=[pl.BlockSpec((None, tr, bn), lambda p, j: (p * j, 0, 0)), ANY],
        scratch_shapes=[pltpu.VMEM((tr, bn), F32), pltpu.VMEM((2, tr, bn), BF16), pltpu.VMEM((tr, bn), BF16),
                        DMA((steps,)), DMA((steps,)), DMA],
        args=[a, b])
    return parts, got


def _adamw_math(w, g, m, v):
    bc1 = 1.0 - ADAM_B1 ** ADAM_STEP
    bc2 = 1.0 - ADAM_B2 ** ADAM_STEP
    nm = ADAM_B1 * m + (1.0 - ADAM_B1) * g
    nv = ADAM_B2 * v + (1.0 - ADAM_B2) * (g * g)
    return -ADAM_LR * ((nm / bc1) / (jnp.sqrt(nv / bc2) + ADAM_EPS) + ADAM_WD * w), nm, nv


def _adamw_small(small_sum, where, gains, taps, scale):
    dq = gains[0].shape[-1]
    d = small_sum.shape[-1]

    def body(where_ref, mine_ref, all_ref, gw, gm, gv, tw, tm_, tv, sw, sm, sv,
             gg, gd, gnm, gnv, tg, td, tnm, tnv, sg, sd, snm, snv):
        for layer in range(gw.shape[0]):
            g = mine_ref[4 * layer:4 * layer + 4, :]
            gg[layer] = g
            gd[layer], gnm[layer], gnv[layer] = _adamw_math(gw[layer], g, gm[layer], gv[layer])
        g = mine_ref[8:8 + tw.shape[1], :]
        tg[0] = g
        td[0], tnm[0], tnv[0] = _adamw_math(tw[0], g, tm_[0], tv[0])
        g = all_ref[11:12, :]
        sg[...] = g
        sd[...], snm[...], snv[...] = _adamw_math(sw[...], g, sm[...], sv[...])

    full = lambda a: pl.BlockSpec(a.shape, lambda i, where_ref: (0,) * a.ndim)
    params = [*gains, *taps, *scale]
    outs = [gains[0]] * 4 + [taps[0]] * 4 + [scale[0]] * 4
    res = pl.pallas_call(
        body, name="adamw_small", out_shape=[jax.ShapeDtypeStruct(a.shape, F32) for a in outs],
        grid_spec=pltpu.PrefetchScalarGridSpec(
            num_scalar_prefetch=1, grid=(1,),
            in_specs=[pl.BlockSpec((16, dq), lambda i, where_ref: (0, where_ref[0])), pl.BlockSpec((16, d), lambda i, where_ref: (0, 0)),
                      *[full(a) for a in params]],
            out_specs=[full(a) for a in outs]),
    )(where, small_sum, small_sum, *params)
    return tuple(res[0:4]), tuple(res[4:8]), tuple(res[8:12])


def _adamw(w, g, m, v, name):
    r, c = w.shape
    rb = _row_block(r, c * (8 * 4 * 2 + 4 * 4))

    def body(w_ref, g_ref, m_ref, v_ref, d_ref, nm_ref, nv_ref, go_ref):
        gv = g_ref[...]
        go_ref[...] = gv
        d_ref[...], nm_ref[...], nv_ref[...] = _adamw_math(w_ref[...], gv, m_ref[...], v_ref[...])

    spec = pl.BlockSpec((rb, c), lambda i: (i, 0))
    return pl.pallas_call(
        body, name=name, grid=(r // rb,), out_shape=[jax.ShapeDtypeStruct((r, c), F32)] * 4,
        in_specs=[spec] * 4, out_specs=[spec] * 4,
        compiler_params=pltpu.CompilerParams(dimension_semantics=("parallel",), vmem_limit_bytes=VMEM_LIMIT),
    )(w, g, m, v)


def kernel(x, norm_gains, pool_w, pool_scale, conv_in_w, conv_w, conv_out_w, ffn_gate_up_w, ffn_down_w, loss_target, m_norm_gains, m_pool_w, m_pool_scale, m_conv_in_w, m_conv_w, m_conv_out_w, m_ffn_gate_up_w, m_ffn_down_w, v_norm_gains, v_pool_w, v_pool_scale, v_conv_in_w, v_conv_w, v_conv_out_w, v_ffn_gate_up_w, v_ffn_down_w):
    _, t, d = x.shape
    dq = d // N_CHIPS
    ng = len(POOL_WINDOWS)
    gw = d // ng
    fq = ffn_down_w.shape[1]
    f = N_CHIPS * fq
    fc = f // 2
    core = lax.axis_index("c")
    chip = 2 * lax.axis_index("x") + lax.axis_index("y")
    core_arr = jnp.reshape(core, (1,)).astype(jnp.int32)
    where_arr = jnp.stack([chip, core]).astype(jnp.int32)
    x2, target = x[0], loss_target[0]

    small_loc = jnp.concatenate(
        [norm_gains.reshape(8, dq), conv_w[0], jnp.zeros((5, dq), F32)], axis=0).reshape(1, 2, 8, dq)
    pool_loc = pool_w.astype(BF16).reshape(1, 2, ng // 2 * (gw // N_CHIPS), gw)
    wgu_loc = [ffn_gate_up_w[l:l + 1].astype(BF16).reshape(1, 2, d // 2, fc) for l in range(2)]
    wd_loc = [ffn_down_w[l:l + 1].astype(BF16).reshape(1, 2, fq // 2, d) for l in range(2)]
    win_loc = conv_in_w.astype(BF16).reshape(1, 2, d // 2, -1)
    wout_loc = conv_out_w.astype(BF16).reshape(1, 2, dq // 2, d)

    def ffn_weights(wgu_f, wd_f):
        return wgu_f.reshape(N_CHIPS, d, fc), wd_f.reshape(f, d)

    ag0 = _SplitGather([(pool_loc, 0), (small_loc, 0), (wgu_loc[0], 0), (wd_loc[0], 0)])
    ag0.start("ag_start_layer0")
    ag1 = _SplitGather([(win_loc, 0), (wout_loc, 0), (wgu_loc[1], 0), (wd_loc[1], 0)])
    ag1.start("ag_start_layer1")
    pool_f, small_f = _pass_on(ag0.wait([0, 1], ag1.token, "ag_wait_first"), "ag_pass_first")
    poolw = pool_f.reshape(N_CHIPS, ng, gw // N_CHIPS, gw).transpose(1, 0, 2, 3).reshape(ng, gw, gw)
    small = small_f.transpose(1, 2, 0, 3).reshape(16, d)
    h1, _ = _fwd_pool(x2, small, pool_scale, poolw)
    wgu0, wd0 = ffn_weights(*_pass_on(ag0.wait([2, 3], h1, "ag_wait_ffn0"), "ag_pass_ffn0"))
    (h2, gu0, ff0, n0), _ = _fwd_ffn(h1, small, wgu0, wd0, 0)
    win_f, wout_f = _pass_on(ag1.wait([0, 1], h2, "ag_wait_conv"), "ag_pass_conv")
    win_f, wout_f = win_f.reshape(N_CHIPS, d, -1), wout_f.reshape(d, d)
    (h3, proj, y, nc), _ = _fwd_conv(h2, small, win_f, wout_f)
    wgu1, wd1 = ffn_weights(*_pass_on(ag1.wait([2, 3], h3, "ag_wait_ffn1"), "ag_pass_ffn1"))
    (dh4, gu1, ff1, n1, loss_blk), _ = _fwd_ffn(h3, small, wgu1, wd1, 1, target=target)

    (dh3, dgu1, dff1, act1, sg_f1), _ = _bwd_ffn(dh4, h3, ff1, gu1, small, wgu1, wd1, 1)
    parts_d1, _ = _weight_grad(act1, dff1, fc, d // 2, "b", "dw_down1")
    parts_gu1, _ = _weight_grad(n1, dgu1, d // 2, fc, "a", "dw_gate_up1")
    ex_ffn1 = _SplitExchange([parts_d1.reshape(N_CHIPS, fq, d // 2), parts_gu1])
    started = ex_ffn1.start("rs_start_ffn1")
    (dh2, dproj, dyv, bcv, sg_c), _ = _bwd_conv(dh3, h2, y, proj, small, win_f, wout_f, _After(started))
    parts_in, _ = _weight_grad(nc, dproj, d // 2, 3 * d // N_CHIPS, "a", "dw_conv_in")
    parts_out, _ = _weight_grad(bcv, dyv, dq // 2, d, "a", "dw_conv_out")
    ex_conv = _SplitExchange([parts_in, parts_out])
    started = ex_conv.start("rs_start_conv")
    (dh1, dgu0, dff0, act0, sg_f0), _ = _bwd_ffn(dh2, h1, ff0, gu0, small, wgu0, wd0, 0, _After(started))
    parts_d0, _ = _weight_grad(act0, dff0, fc, d // 2, "b", "dw_down0")
    parts_gu0, _ = _weight_grad(n0, dgu0, d // 2, fc, "a", "dw_gate_up0")
    ex_ffn0 = _SplitExchange([parts_d0.reshape(N_CHIPS, fq, d // 2), parts_gu0])
    started = ex_ffn0.start("rs_start_ffn0")
    (grad_x, dpool, sg_p), _ = _bwd_pool(dh1, x2, small, pool_scale, poolw, _After(started))
    g_pool = dpool.astype(BF16).reshape(2, ng // 2, N_CHIPS, gw // N_CHIPS, gw).transpose(2, 0, 1, 3, 4).reshape(
        N_CHIPS, 2, ng // 2 * (gw // N_CHIPS), gw)
    small_g = jnp.concatenate(
        [sg_p[0:2], sg_f0[0:2], sg_c[0:2], sg_f1[0:2], sg_c[2:5], sg_p[2:3],
         jnp.broadcast_to(loss_blk[0:1, 0:1], (1, d)), jnp.zeros((3, d), F32)], axis=0)
    (land_p,) = _sibling_exchange([g_pool], None, "rs_sibling_pool")
    ex_pool = _SplitExchange([_add_sibling(g_pool, land_p, core_arr)])
    started = ex_pool.start("rs_start_pool")

    def update(w, g, m, v, name):
        flat = (-1, w.shape[-1])
        dl, m2, v2, g2 = _adamw(w.reshape(flat), g.reshape(flat), m.reshape(flat), v.reshape(flat), "adamw_" + name)
        return tuple(o.reshape(w.shape) for o in (g2, dl, m2, v2))

    (parts_d1, parts_gu1), (recv_d1, recv_gu1) = ex_ffn1.wait([started], "rs_wait_ffn1")
    (parts_in, parts_out), (recv_in, recv_out) = ex_conv.wait([started], "rs_wait_conv")
    gs_gu = _add_chips(parts_gu1, recv_gu1, where_arr, 1, 2)
    gs_d = _add_chips(parts_d1, recv_d1, where_arr, 1, 2, col_half=True)
    gs_in = _add_chips(parts_in, recv_in, where_arr)
    gs_out = _add_chips(parts_out, recv_out, where_arr)
    full_in, full_out = _sibling_share([gs_in, gs_out], [False, False], "rs_share_conv")
    up_in = update(conv_in_w, full_in.reshape(1, d, -1), m_conv_in_w, v_conv_in_w, "conv_in")
    up_out = update(conv_out_w, full_out.reshape(1, dq, d), m_conv_out_w, v_conv_out_w, "conv_out")
    done_first = [up_in[1], up_out[1], gs_gu, gs_d]
    (parts_d0, parts_gu0), (recv_d0, recv_gu0) = ex_ffn0.wait(done_first, "rs_wait_ffn0")
    (parts_p,), (recv_p,) = ex_pool.wait(done_first, "rs_wait_pool")
    (small_all,) = _sibling_exchange([], small_g, "rs_small_gather", after=recv_gu0)
    gs_gu = _add_chips(parts_gu0, recv_gu0, where_arr, 0, 2, gs_gu)
    gs_d = _add_chips(parts_d0, recv_d0, where_arr, 0, 2, gs_d, col_half=True)
    gs_pool = _add_chips(parts_p, recv_p, where_arr)
    full_gu, full_d, full_pool = _sibling_share([gs_gu, gs_d, gs_pool], [False, True, False], "rs_share_ffn", [small_all])
    small_sum = _sum_small(small_all)
    loss = small_sum[12, 0]
    up_gains, up_taps, up_scale = _adamw_small(
        small_sum, where_arr, (norm_gains, m_norm_gains, v_norm_gains), (conv_w, m_conv_w, v_conv_w),
        (pool_scale, m_pool_scale, v_pool_scale))

    ups = [
        up_gains,
        update(pool_w, full_pool.reshape(1, ng, gw // N_CHIPS, gw), m_pool_w, v_pool_w, "pool_w"),
        up_scale,
        up_in,
        up_taps,
        up_out,
        update(ffn_gate_up_w, full_gu.reshape(2, d, fc), m_ffn_gate_up_w, v_ffn_gate_up_w, "gate_up"),
        update(ffn_down_w, full_d.reshape(2, fq, d), m_ffn_down_w, v_ffn_down_w, "down"),
    ]
    grads_out, deltas, new_ms, new_vs = (list(col) for col in zip(*ups))
    return (loss, grad_x[None], *grads_out, *deltas, *new_ms, *new_vs)
```

```python
import jax
import jax.numpy as jnp
from jax import lax
from jax.experimental import pallas as pl
from jax.experimental.pallas import tpu as pltpu

RMS_EPS = 1e-6
POOL_WINDOWS = (2, 4, 8, 16)
POOL_HALO = 16
CONV_HALO = 8
N_CHIPS = 4
N_DEV = 8
ADAM_LR = 0.001
ADAM_B1 = 0.9
ADAM_B2 = 0.999
ADAM_EPS = 1e-08
ADAM_WD = 0.01
ADAM_STEP = 10
VMEM_LIMIT = 56 * 2**20
STREAM_BUDGET = 24 * 2**20
MESH = pl.DeviceIdType.MESH
ANY = pl.BlockSpec(memory_space=pl.ANY)
DMA = pltpu.SemaphoreType.DMA
BF16 = jnp.bfloat16
F32 = jnp.float32


def _token_tile(t, rows=512):
    return min(rows, t)


def _rms(x):
    r = lax.rsqrt(jnp.mean(x * x, axis=-1, keepdims=True) + RMS_EPS)
    return x * r, r


def _rms_bwd(dy, xh, r, g):
    a = dy * g
    return r * (a - xh * jnp.mean(a * xh, axis=-1, keepdims=True))


def _dot(a, b):
    return jnp.dot(a, b, preferred_element_type=F32)


def _dot_nt(a, b):
    return lax.dot_general(a, b, (((1,), (1,)), ((), ())), preferred_element_type=F32)


def _dot_tn(a, b):
    return lax.dot_general(a, b, (((0,), (0,)), ((), ())), preferred_element_type=F32)


def _colsum(a):
    return jnp.sum(a, axis=0, keepdims=True)


def _resident(block, index_map):
    return pl.BlockSpec(block, index_map, pipeline_mode=pl.Buffered(1))


def _row_block(r, row_bytes):
    best = None
    for rb in range(16, r + 1, 16):
        if r % rb == 0 and rb * row_bytes <= STREAM_BUDGET:
            best = rb
    return best if best is not None else r


def _place():
    x, y, c = lax.axis_index("x"), lax.axis_index("y"), lax.axis_index("c")
    return x, y, c, 2 * x + y


def _dev(chip, core):
    return (chip // 2, chip % 2, core)


def _remote(src, dst, send_sem, recv_sem, device):
    return pltpu.make_async_remote_copy(src_ref=src, dst_ref=dst, send_sem=send_sem, recv_sem=recv_sem,
                                        device_id=device, device_id_type=MESH)


class _Gather:
    def __init__(self, shards):
        n = len(shards)
        self.args = [s for s, _ in shards]
        self.layers = [l for _, l in shards]
        self.out_shape = [jax.ShapeDtypeStruct((N_CHIPS,) + s.shape[1:], s.dtype) for s in self.args]
        self.sems = [DMA((n,)), DMA((n,)), DMA((n, 3)), DMA((n, 3)), DMA((n, 3)), DMA((n, 3))]

    def _own(self, loc, out, sems, a):
        x, y, c, k = _place()
        return _remote(loc[a].at[self.layers[a]], out[a].at[k], sems[0].at[a], sems[1].at[a], (x, y, 1 - c))

    def _ici(self, loc, out, sems, a, m, arrival):
        x, y, c, k = _place()
        dst = out[a].at[k ^ m, c] if arrival else out[a].at[k, c]
        return _remote(loc[a].at[self.layers[a], c], dst, sems[2].at[a, m - 1], sems[3].at[a, m - 1], _dev(k ^ m, c))

    def _forward(self, out, sems, a, m, arrival):
        x, y, c, k = _place()
        got = out[a].at[k ^ m, 1 - c] if arrival else out[a].at[k ^ m, c]
        return _remote(got, got, sems[4].at[a, m - 1], sems[5].at[a, m - 1], (x, y, 1 - c))

    def start(self, loc, out, sems):
        for a in range(len(self.args)):
            for m in range(1, N_CHIPS):
                self._ici(loc, out, sems, a, m, False).start()
            self._own(loc, out, sems, a).start()

    def finish(self, loc, out, sems):
        n = len(self.args)
        for a in range(n):
            for m in range(1, N_CHIPS):
                self._ici(loc, out, sems, a, m, True).wait_recv()
                self._forward(out, sems, a, m, False).start()
        for a in range(n):
            for m in range(1, N_CHIPS):
                self._forward(out, sems, a, m, True).wait_recv()
            self._own(loc, out, sems, a).wait_recv()
        for a in range(n):
            for m in range(1, N_CHIPS):
                self._ici(loc, out, sems, a, m, False).wait_send()
                self._forward(out, sems, a, m, False).wait_send()
            self._own(loc, out, sems, a).wait_send()


class _ChipExchange:
    def __init__(self, parts):
        n = len(parts)
        self.args = list(parts)
        self.out_shape = [jax.ShapeDtypeStruct((N_CHIPS - 1,) + p.shape[1:], p.dtype) for p in parts]
        self.sems = [DMA((n, 3)), DMA((n, 3))]

    def _copy(self, p, land, sems, a, m):
        x, y, c, k = _place()
        return _remote(p[a].at[k ^ m], land[a].at[m - 1], sems[0].at[a, m - 1], sems[1].at[a, m - 1], _dev(k ^ m, c))

    def start(self, p, land, sems):
        for a in range(len(self.args)):
            for m in range(1, N_CHIPS):
                self._copy(p, land, sems, a, m).start()

    def finish(self, p, land, sems):
        for a in range(len(self.args)):
            for m in range(1, N_CHIPS):
                self._copy(p, land, sems, a, m).wait_recv()
        for a in range(len(self.args)):
            for m in range(1, N_CHIPS):
                self._copy(p, land, sems, a, m).wait_send()


def _hosted(body, comm, *, name, grid, in_specs, out_specs, out_shape, args, scratch_shapes=()):
    ni, no, ns = len(in_specs), len(out_shape), len(scratch_shapes)
    if comm is None:
        res = pl.pallas_call(
            body, name=name, grid=grid, in_specs=list(in_specs), out_specs=list(out_specs), out_shape=list(out_shape),
            scratch_shapes=list(scratch_shapes),
            compiler_params=pltpu.CompilerParams(dimension_semantics=("arbitrary",) * len(grid), vmem_limit_bytes=VMEM_LIMIT),
        )(*args)
        return list(res), []
    nc, nco = len(comm.args), len(comm.out_shape)

    def full(*refs):
        cin = refs[ni:ni + nc]
        outs = refs[ni + nc:ni + nc + no]
        cout = refs[ni + nc + no:ni + nc + no + nco]
        scratch = refs[ni + nc + no + nco:ni + nc + no + nco + ns]
        csems = refs[ni + nc + no + nco + ns:]
        first = _all_of([pl.program_id(ax) == 0 for ax in range(len(grid))])
        last = _all_of([pl.program_id(ax) == grid[ax] - 1 for ax in range(len(grid))])

        @pl.when(first)
        def _():
            comm.start(cin, cout, csems)

        body(*refs[:ni], *outs, *scratch)

        @pl.when(last)
        def _():
            comm.finish(cin, cout, csems)

    res = pl.pallas_call(
        full, name=name, grid=grid, in_specs=[*in_specs, *[ANY] * nc], out_specs=[*out_specs, *[ANY] * nco],
        out_shape=[*out_shape, *comm.out_shape], scratch_shapes=[*scratch_shapes, *comm.sems],
        compiler_params=pltpu.CompilerParams(dimension_semantics=("arbitrary",) * len(grid), vmem_limit_bytes=VMEM_LIMIT,
                                             has_side_effects=True),
    )(*args, *comm.args)
    return list(res[:no]), list(res[no:])


def _all_of(conds):
    out = conds[0]
    for c in conds[1:]:
        out = jnp.logical_and(out, c)
    return out


class _After:
    def __init__(self, token):
        self.args, self.out_shape, self.sems = [token], [], []

    def start(self, *_):
        pass

    def finish(self, *_):
        pass


def _alone(comm, name):
    return _hosted(lambda: None, comm, name=name, grid=(1,), in_specs=[], out_specs=[], out_shape=[], args=[])[1]


HBM = pl.BlockSpec(memory_space=pltpu.HBM)
SEM = pl.BlockSpec(memory_space=pltpu.SEMAPHORE)
DATAFLOW = pltpu.SideEffectType.DATAFLOW_SIDE_EFFECTING


class _SplitGather:
    PER_ARRAY = 8

    def __init__(self, shards):
        self.plan = _Gather(shards)
        self.n = len(shards)

    @staticmethod
    def _tables(sems_of):
        class Table:
            def __init__(self, pick):
                self.pick = pick

            @property
            def at(self):
                return self

            def __getitem__(self, idx):
                return self.pick(idx)

        return [Table(lambda a: sems_of[a][0]), Table(lambda a: sems_of[a][1]),
                Table(lambda am: sems_of[am[0]][2 + am[1]]), Table(lambda am: sems_of[am[0]][5 + am[1]])]

    def start(self, name, after=()):
        n, plan, per, na = self.n, self.plan, self.PER_ARRAY, len(after)

        def body(*refs):
            loc, land = refs[:n], refs[n:2 * n]
            sems_of = {a: refs[2 * n + na + per * a:2 * n + na + per * (a + 1)] for a in range(n)}
            plan.start(loc, land, self._tables(sems_of))
            refs[-1][...] = jnp.zeros_like(refs[-1])

        lands = [pltpu.with_memory_space_constraint(lax.empty(o.shape, o.dtype), pltpu.HBM) for o in plan.out_shape]
        locs = [pltpu.with_memory_space_constraint(a, pltpu.HBM) for a in plan.args]
        res = pl.pallas_call(
            body, name=name,
            out_shape=[*[DMA(())] * (per * n),
                       *[pltpu.HBM(o.shape, o.dtype) for o in plan.out_shape],
                       jax.ShapeDtypeStruct((8, 128), F32)],
            in_specs=[HBM] * (2 * n) + [pl.BlockSpec(memory_space=pl.ANY)] * na,
            out_specs=[SEM] * (per * n) + [HBM] * n + [pl.BlockSpec(memory_space=pltpu.VMEM)],
            input_output_aliases={n + i: per * n + i for i in range(n)},
            compiler_params=pltpu.CompilerParams(has_side_effects=DATAFLOW),
        )(*locs, *lands, *after)
        self.sems = {a: list(res[per * a:per * (a + 1)]) for a in range(n)}
        self.locs = locs
        self.lands = list(res[per * n:per * n + n])
        self.token = res[-1]

    def wait(self, idxs, after, name):
        plan, g, per = self.plan, len(idxs), self.PER_ARRAY

        def body(*refs):
            loc = {a: refs[j] for j, a in enumerate(idxs)}
            land = {a: refs[g + j] for j, a in enumerate(idxs)}
            sems = self._tables({a: refs[2 * g + per * j:2 * g + per * (j + 1)] for j, a in enumerate(idxs)})
            for a in idxs:
                for m in range(1, N_CHIPS):
                    plan._ici(loc, land, sems, a, m, True).wait_recv()
                    plan._ici(loc, land, sems, a, m, False).wait_send()
                plan._own(loc, land, sems, a).wait_recv()
                plan._own(loc, land, sems, a).wait_send()

        res = pl.pallas_call(
            body, name=name,
            out_shape=[pltpu.HBM(self.lands[a].shape, self.lands[a].dtype) for a in idxs],
            in_specs=[HBM] * (2 * g) + [SEM] * (per * g) + [pl.BlockSpec(memory_space=pl.ANY)], out_specs=[HBM] * g,
            input_output_aliases={g + j: j for j in range(g)},
            compiler_params=pltpu.CompilerParams(has_side_effects=DATAFLOW),
        )(*[self.locs[a] for a in idxs], *[self.lands[a] for a in idxs],
          *[s for a in idxs for s in self.sems[a]], after)
        return list(res)


class _SplitExchange:
    PER_ARRAY = 6

    def __init__(self, parts):
        self.plan = _ChipExchange(parts)
        self.n = len(parts)

    @staticmethod
    def _tables(sems_of):
        class Table:
            def __init__(self, pick):
                self.pick = pick

            @property
            def at(self):
                return self

            def __getitem__(self, am):
                return self.pick(am)

        return [Table(lambda am: sems_of[am[0]][am[1]]), Table(lambda am: sems_of[am[0]][3 + am[1]])]

    def start(self, name):
        n, plan, per = self.n, self.plan, self.PER_ARRAY

        def body(*refs):
            p, land = refs[:n], refs[n:2 * n]
            sems_of = {a: refs[2 * n + per * a:2 * n + per * (a + 1)] for a in range(n)}
            plan.start(p, land, self._tables(sems_of))
            refs[-1][...] = jnp.zeros_like(refs[-1])

        lands = [pltpu.with_memory_space_constraint(lax.empty(o.shape, o.dtype), pltpu.HBM) for o in plan.out_shape]
        parts = [pltpu.with_memory_space_constraint(a, pltpu.HBM) for a in plan.args]
        res = pl.pallas_call(
            body, name=name,
            out_shape=[*[DMA(())] * (per * n),
                       *[pltpu.HBM(a.shape, a.dtype) for a in plan.args],
                       *[pltpu.HBM(o.shape, o.dtype) for o in plan.out_shape],
                       jax.ShapeDtypeStruct((8, 128), F32)],
            in_specs=[HBM] * (2 * n),
            out_specs=[SEM] * (per * n) + [HBM] * (2 * n) + [pl.BlockSpec(memory_space=pltpu.VMEM)],
            input_output_aliases={i: per * n + i for i in range(2 * n)},
            compiler_params=pltpu.CompilerParams(has_side_effects=DATAFLOW),
        )(*parts, *lands)
        self.sems = list(res[:per * n])
        self.parts = list(res[per * n:per * n + n])
        self.lands = list(res[per * n + n:per * n + 2 * n])
        return res[-1]

    def wait(self, after, name):
        n, plan, per = self.n, self.plan, self.PER_ARRAY

        def body(*refs):
            p, land = refs[:n], refs[n:2 * n]
            sems_of = {a: refs[2 * n + per * a:2 * n + per * (a + 1)] for a in range(n)}
            plan.finish(p, land, self._tables(sems_of))

        res = pl.pallas_call(
            body, name=name,
            out_shape=[*[pltpu.HBM(a.shape, a.dtype) for a in self.parts], *[pltpu.HBM(a.shape, a.dtype) for a in self.lands]],
            in_specs=[HBM] * (2 * n) + [SEM] * (per * n) + [pl.BlockSpec(memory_space=pl.ANY)] * len(after),
            out_specs=[HBM] * (2 * n), input_output_aliases={i: i for i in range(2 * n)},
            compiler_params=pltpu.CompilerParams(has_side_effects=DATAFLOW),
        )(*self.parts, *self.lands, *self.sems, *after)
        return list(res[:n]), list(res[n:])


PASS_ON_BARRIER = 1


def _sibling_barrier():
    x, y, c, _ = _place()
    barrier = pltpu.get_barrier_semaphore()
    pl.semaphore_signal(barrier, inc=1, device_id=(x, y, 1 - c), device_id_type=MESH)
    pl.semaphore_wait(barrier, 1)


def _pass_on(lands, name):
    n = len(lands)

    def body(*refs):
        out = refs[n:2 * n]
        send_sems, recv_sems = refs[2 * n:]
        x, y, c, k = _place()
        _sibling_barrier()
        cps = []
        for a in range(n):
            for m in range(1, N_CHIPS):
                got = out[a].at[k ^ m, c]
                cp = _remote(got, got, send_sems.at[a, m - 1], recv_sems.at[a, m - 1], (x, y, 1 - c))
                cp.start()
                cps.append(cp)
        for a in range(n):
            for m in range(1, N_CHIPS):
                theirs = out[a].at[k ^ m, 1 - c]
                _remote(theirs, theirs, send_sems.at[a, m - 1], recv_sems.at[a, m - 1], (x, y, 1 - c)).wait_recv()
        for cp in cps:
            cp.wait_send()

    return pl.pallas_call(
        body, name=name, out_shape=[jax.ShapeDtypeStruct(a.shape, a.dtype) for a in lands],
        in_specs=[ANY] * n, out_specs=[ANY] * n, input_output_aliases={a: a for a in range(n)},
        scratch_shapes=[DMA((n, 3)), DMA((n, 3))],
        compiler_params=pltpu.CompilerParams(has_side_effects=True, collective_id=PASS_ON_BARRIER),
    )(*lands)


def _sibling_exchange(grads, small, name, after=None):
    n = len(grads)
    ns = 0 if small is None else 1
    na = 0 if after is None else 1

    def body(*refs):
        g = refs[:n]
        land = refs[n + ns + na:2 * n + ns + na]
        send_sems, recv_sems, own_sem, ssend_sems, srecv_sems = refs[2 * n + 2 * ns + na:]
        x, y, c, k = _place()
        me = 2 * k + c
        cps = []
        for a in range(n):
            cp = _remote(g[a].at[:, pl.ds(1 - c, 1)], land[a], send_sems.at[a], recv_sems.at[a], (x, y, 1 - c))
            cp.start()
            cps.append(cp)
        if small is not None:
            sm, smg = refs[n], refs[2 * n + 1 + na]
            peers = [me ^ m for m in range(1, N_DEV)]
            ids = [(p // 4, (p // 2) % 2, p % 2) for p in peers]
            own = pltpu.make_async_copy(sm, smg.at[me], own_sem)
            own.start()
            for m in range(1, N_DEV):
                cp = _remote(sm, smg.at[me], ssend_sems.at[m - 1], srecv_sems.at[m - 1], ids[m - 1])
                cp.start()
                cps.append(cp)
            for m in range(1, N_DEV):
                _remote(sm, smg.at[peers[m - 1]], ssend_sems.at[m - 1], srecv_sems.at[m - 1], ids[m - 1]).wait_recv()
            own.wait()
        for cp in cps[:n]:
            cp.wait_recv()
        for cp in cps:
            cp.wait_send()

    out_shape = [jax.ShapeDtypeStruct((N_CHIPS, 1) + a.shape[2:], a.dtype) for a in grads]
    ins = list(grads)
    if small is not None:
        out_shape.append(jax.ShapeDtypeStruct((N_DEV,) + small.shape, small.dtype))
        ins.append(small)
    if after is not None:
        ins.append(after)
    return pl.pallas_call(
        body, name=name, out_shape=out_shape, in_specs=[ANY] * (n + ns + na), out_specs=[ANY] * (n + ns),
        scratch_shapes=[DMA((max(n, 1),)), DMA((max(n, 1),)), DMA, DMA((N_DEV - 1,)), DMA((N_DEV - 1,))],
        compiler_params=pltpu.CompilerParams(has_side_effects=True),
    )(*ins)


def _sibling_share(halves, col_half, name, after=()):
    n, na = len(halves), len(after)

    def body(*refs):
        out = refs[n + na:2 * n + na]
        send_sems, recv_sems = refs[2 * n + na:]
        x, y, c, k = _place()

        def half(a, core):
            if not col_half[a]:
                return out[a].at[:, pl.ds(core, 1)]
            cols = out[a].shape[-1] // 2
            return out[a].at[:, :, pl.ds(pl.multiple_of(core * cols, cols), cols)]

        cps = []
        for a in range(n):
            cp = _remote(half(a, c), half(a, c), send_sems.at[a], recv_sems.at[a], (x, y, 1 - c))
            cp.start()
            cps.append(cp)
        for a in range(n):
            _remote(half(a, 1 - c), half(a, 1 - c), send_sems.at[a], recv_sems.at[a], (x, y, 1 - c)).wait_recv()
        for cp in cps:
            cp.wait_send()

    out_shape = [jax.ShapeDtypeStruct(a.shape, a.dtype) for a in halves]
    return pl.pallas_call(
        body, name=name, out_shape=out_shape, in_specs=[ANY] * (n + na), out_specs=[ANY] * n,
        input_output_aliases={a: a for a in range(n)}, scratch_shapes=[DMA((n,)), DMA((n,))],
        compiler_params=pltpu.CompilerParams(has_side_effects=True),
    )(*halves, *after)


def _add_sibling(g, land, core):
    _, _, r, c = g.shape
    rb = _row_block(r, c * (3 * 2 * 2 + 2 * 4))

    def body(core_ref, g_ref, l_ref, o_ref):
        o_ref[...] = (g_ref[...].astype(F32) + l_ref[...].astype(F32)).astype(o_ref.dtype)

    return pl.pallas_call(
        body, name="rs_add_sibling", out_shape=jax.ShapeDtypeStruct((N_CHIPS, r, c), g.dtype),
        grid_spec=pltpu.PrefetchScalarGridSpec(
            num_scalar_prefetch=1, grid=(N_CHIPS, r // rb),
            in_specs=[pl.BlockSpec((None, None, rb, c), lambda j, i, core_ref: (j, core_ref[0], i, 0)),
                      pl.BlockSpec((None, None, rb, c), lambda j, i, core_ref: (j, 0, i, 0))],
            out_specs=pl.BlockSpec((None, rb, c), lambda j, i, core_ref: (j, i, 0))),
        compiler_params=pltpu.CompilerParams(dimension_semantics=("parallel", "parallel"), vmem_limit_bytes=VMEM_LIMIT),
    )(core, g, land)


def _add_chips(part, land, where, layer=0, n_layers=1, into=None, col_half=False, rows=(0, 1)):
    _, r, c = part.shape
    sub, n_sub = rows
    rb = _row_block(r, c * (4 * 2 * 2 + 4 * 2 + 2 * 4))

    def body(where_ref, p_ref, l_ref, *rest):
        acc = p_ref[...].astype(F32)
        for m in range(N_CHIPS - 1):
            acc = acc + l_ref[m].astype(F32)
        rest[-1][...] = acc

    in_specs = [pl.BlockSpec((None, rb, c), lambda i, where_ref: (where_ref[0], i, 0)),
                pl.BlockSpec((N_CHIPS - 1, rb, c), lambda i, where_ref: (0, i, 0))]
    args = [where, part, land]
    if into is not None:
        in_specs.append(ANY)
        args.append(into)
    if col_half:
        out_shape = jax.ShapeDtypeStruct((n_layers, r, 2 * c), F32)
        out_spec = pl.BlockSpec((None, rb, c), lambda i, where_ref: (layer, i, where_ref[1]))
    else:
        out_shape = jax.ShapeDtypeStruct((n_layers, 2, n_sub * r, c), F32)
        out_spec = pl.BlockSpec((None, None, rb, c), lambda i, where_ref: (layer, where_ref[1], sub * (r // rb) + i, 0))
    return pl.pallas_call(
        body, name="rs_add_chips", out_shape=out_shape,
        grid_spec=pltpu.PrefetchScalarGridSpec(
            num_scalar_prefetch=1, grid=(r // rb,), in_specs=in_specs, out_specs=out_spec),
        input_output_aliases={} if into is None else {3: 0},
        compiler_params=pltpu.CompilerParams(dimension_semantics=("parallel",), vmem_limit_bytes=VMEM_LIMIT),
    )(*args)


def _sum_small(smg):
    def body(s_ref, o_ref):
        acc = s_ref[0]
        for j in range(1, N_DEV):
            acc = acc + s_ref[j]
        o_ref[...] = acc

    return pl.pallas_call(body, name="rs_sum_small", out_shape=jax.ShapeDtypeStruct(smg.shape[1:], F32))(smg)


def _pool_windows(ext_ref, g, gw, tm, first_row):
    w = POOL_WINDOWS[g]
    slab = ext_ref[:, g * gw:(g + 1) * gw]
    p, k = slab, 1
    while k < w:
        p = p + pltpu.roll(p, k, 0)
        k *= 2
    t = first_row + lax.broadcasted_iota(jnp.int32, (tm, 1), 0)
    cnt = jnp.minimum(t + 1, w).astype(F32)
    return p[POOL_HALO:] / cnt - slab[POOL_HALO:]


def _fwd_pool(x, small, scale, poolw, comm=None):
    t, d = x.shape
    tm = _token_tile(t)
    gw = d // len(POOL_WINDOWS)

    def body(x_ref, sm_ref, sc_ref, w_ref, h_ref, ext_ref, mix_ref):
        i = pl.program_id(0)

        @pl.when(i == 0)
        def _():
            ext_ref[0:POOL_HALO, :] = jnp.zeros((POOL_HALO, d), F32)

        @pl.when(i > 0)
        def _():
            ext_ref[0:POOL_HALO, :] = ext_ref[tm:tm + POOL_HALO, :]

        xv = x_ref[...]
        xh, _ = _rms(xv)
        ext_ref[POOL_HALO:, :] = xh * sm_ref[0:1, :]
        for g in range(len(POOL_WINDOWS)):
            pooled = _pool_windows(ext_ref, g, gw, tm, i * tm)
            cols = slice(g * gw, (g + 1) * gw)
            mix_ref[:, cols] = _dot(pooled.astype(BF16), w_ref[g]) * sc_ref[:, cols]
        mh, _ = _rms(mix_ref[...])
        h_ref[...] = xv + mh * sm_ref[1:2, :]

    (h,), got = _hosted(
        body, comm, name="fwd_pool", grid=(t // tm,), out_shape=[jax.ShapeDtypeStruct((t, d), F32)],
        in_specs=[pl.BlockSpec((tm, d), lambda i: (i, 0)), _resident(small.shape, lambda i: (0, 0)),
                  _resident(scale.shape, lambda i: (0, 0)), _resident(poolw.shape, lambda i: (0, 0, 0))],
        out_specs=[pl.BlockSpec((tm, d), lambda i: (i, 0))],
        scratch_shapes=[pltpu.VMEM((POOL_HALO + tm, d), F32), pltpu.VMEM((tm, d), F32)],
        args=[x, small, scale, poolw])
    return h, got


def _fwd_ffn(h, small, wgu, wd, layer, comm=None, target=None):
    t, d = h.shape
    tm = _token_tile(t)
    steps = t // tm
    fc = wgu.shape[-1]
    f = 2 * fc
    g_in, g_out = 4 * layer + 2, 4 * layer + 3
    with_loss = target is not None

    def body(h_ref, *refs):
        if with_loss:
            t_ref, sm_ref, wgu_ref, wd_ref, o_ref, gu_ref, ff_ref, n_ref, l_ref, acc_ref = refs
        else:
            sm_ref, wgu_ref, wd_ref, o_ref, gu_ref, ff_ref, n_ref = refs
        hv = h_ref[...]
        hh, _ = _rms(hv)
        n = (hh * sm_ref[g_in:g_in + 1, :]).astype(BF16)
        n_ref[...] = n
        ff = None
        for j in range(2):
            gate = _dot(n, wgu_ref[j])
            up = _dot(n, wgu_ref[2 + j])
            gu_ref[:, j * fc:(j + 1) * fc] = gate.astype(BF16)
            gu_ref[:, f + j * fc:f + (j + 1) * fc] = up.astype(BF16)
            act = (gate * jax.nn.sigmoid(gate) * up).astype(BF16)
            part = _dot(act, wd_ref[j * fc:(j + 1) * fc, :])
            ff = part if ff is None else ff + part
        ff_ref[...] = ff
        fh, _ = _rms(ff)
        out = hv + fh * sm_ref[g_out:g_out + 1, :]
        if not with_loss:
            o_ref[...] = out
            return
        i = pl.program_id(0)
        e = out - t_ref[...]
        o_ref[...] = e * (1.0 / d)

        @pl.when(i == 0)
        def _():
            acc_ref[...] = jnp.zeros_like(acc_ref)

        acc_ref[...] += _colsum(e * e)

        @pl.when(i == steps - 1)
        def _():
            l_ref[...] = jnp.full(l_ref.shape, 0.5 / d, F32) * jnp.sum(acc_ref[...])

    row = lambda i: (i, 0)
    out_shape = [jax.ShapeDtypeStruct((t, d), F32), jax.ShapeDtypeStruct((t, 2 * f), BF16),
                 jax.ShapeDtypeStruct((t, d), F32), jax.ShapeDtypeStruct((t, d), BF16)]
    out_specs = [pl.BlockSpec((tm, d), row), pl.BlockSpec((tm, 2 * f), row), pl.BlockSpec((tm, d), row),
                 pl.BlockSpec((tm, d), row)]
    weight_specs = [_resident(small.shape, lambda i: (0, 0)), _resident(wgu.shape, lambda i: (0, 0, 0)),
                    _resident(wd.shape, lambda i: (0, 0))]
    if with_loss:
        return _hosted(
            body, comm, name=f"fwd_ffn{layer}_loss", grid=(steps,),
            out_shape=out_shape + [jax.ShapeDtypeStruct((8, 128), F32)],
            in_specs=[pl.BlockSpec((tm, d), row), pl.BlockSpec((tm, d), row)] + weight_specs,
            out_specs=out_specs + [pl.BlockSpec((8, 128), lambda i: (0, 0))],
            scratch_shapes=[pltpu.VMEM((1, d), F32)], args=[h, target, small, wgu, wd])
    return _hosted(
        body, comm, name=f"fwd_ffn{layer}", grid=(steps,), out_shape=out_shape,
        in_specs=[pl.BlockSpec((tm, d), row)] + weight_specs, out_specs=out_specs, args=[h, small, wgu, wd])


def _fwd_conv(h, small, win, wout, comm=None):
    t, d = h.shape
    tm = _token_tile(t)
    pc = win.shape[-1]

    def body(h_ref, sm_ref, win_ref, wout_ref, o_ref, proj_ref, y_ref, n_ref, pj_ref, uext_ref):
        i = pl.program_id(0)

        @pl.when(i == 0)
        def _():
            uext_ref[0:CONV_HALO, :] = jnp.zeros((CONV_HALO, d), F32)

        @pl.when(i > 0)
        def _():
            uext_ref[0:CONV_HALO, :] = uext_ref[tm:tm + CONV_HALO, :]

        hv = h_ref[...]
        hh, _ = _rms(hv)
        n = (hh * sm_ref[4:5, :]).astype(BF16)
        n_ref[...] = n
        for k in range(N_CHIPS):
            pj_ref[:, k * pc:(k + 1) * pc] = _dot(n, win_ref[k])
        proj_ref[...] = pj_ref[...].astype(BF16)
        uext_ref[CONV_HALO:, :] = pj_ref[:, d:2 * d] * pj_ref[:, 2 * d:]
        taps = [sm_ref[8 + j:9 + j, :] for j in range(3)]
        full = uext_ref[...]
        conv = (full[CONV_HALO:] * taps[2] + pltpu.roll(full, 1, 0)[CONV_HALO:] * taps[1]
                + pltpu.roll(full, 2, 0)[CONV_HALO:] * taps[0])
        y = _dot((pj_ref[:, 0:d] * conv).astype(BF16), wout_ref[...])
        y_ref[...] = y
        yh, _ = _rms(y)
        o_ref[...] = hv + yh * sm_ref[5:6, :]

    row = lambda i: (i, 0)
    return _hosted(
        body, comm, name="fwd_conv", grid=(t // tm,),
        out_shape=[jax.ShapeDtypeStruct((t, d), F32), jax.ShapeDtypeStruct((t, 3 * d), BF16),
                   jax.ShapeDtypeStruct((t, d), F32), jax.ShapeDtypeStruct((t, d), BF16)],
        in_specs=[pl.BlockSpec((tm, d), row), _resident(small.shape, lambda i: (0, 0)),
                  _resident(win.shape, lambda i: (0, 0, 0)), _resident(wout.shape, lambda i: (0, 0))],
        out_specs=[pl.BlockSpec((tm, d), row), pl.BlockSpec((tm, 3 * d), row), pl.BlockSpec((tm, d), row),
                   pl.BlockSpec((tm, d), row)],
        scratch_shapes=[pltpu.VMEM((tm, 3 * d), F32), pltpu.VMEM((CONV_HALO + tm, d), F32)],
        args=[h, small, win, wout])


def _bwd_ffn(dh, h, ff, gu, small, wgu, wd, layer, comm=None):
    t, d = h.shape
    tm = _token_tile(t, 256)
    fc = wgu.shape[-1]
    f = 2 * fc
    g_in, g_out = 4 * layer + 2, 4 * layer + 3

    def body(dh_ref, h_ref, ff_ref, gu_ref, sm_ref, wgu_ref, wd_ref, o_ref, dgu_ref, dff_ref, act_ref, sg_ref):
        i = pl.program_id(0)

        @pl.when(i == 0)
        def _():
            sg_ref[...] = jnp.zeros_like(sg_ref)

        dy = dh_ref[...]
        fh, r3 = _rms(ff_ref[...])
        sg_ref[1:2, :] += _colsum(dy * fh)
        dff = _rms_bwd(dy, fh, r3, sm_ref[g_out:g_out + 1, :]).astype(BF16)
        dff_ref[...] = dff
        for j in range(2):
            dact = _dot_nt(dff, wd_ref[j * fc:(j + 1) * fc, :])
            gate = gu_ref[:, j * fc:(j + 1) * fc].astype(F32)
            up = gu_ref[:, f + j * fc:f + (j + 1) * fc].astype(F32)
            sig = jax.nn.sigmoid(gate)
            silu = gate * sig
            act_ref[:, j * fc:(j + 1) * fc] = (silu * up).astype(BF16)
            dgu_ref[:, j * fc:(j + 1) * fc] = (dact * up * (sig * (1.0 + gate * (1.0 - sig)))).astype(BF16)
            dgu_ref[:, f + j * fc:f + (j + 1) * fc] = (dact * silu).astype(BF16)
        dn = None
        for k in range(N_CHIPS):
            part = _dot_nt(dgu_ref[:, k * fc:(k + 1) * fc], wgu_ref[k])
            dn = part if dn is None else dn + part
        hh, r2 = _rms(h_ref[...])
        sg_ref[0:1, :] += _colsum(dn * hh)
        o_ref[...] = dy + _rms_bwd(dn, hh, r2, sm_ref[g_in:g_in + 1, :])

    row = lambda i: (i, 0)
    return _hosted(
        body, comm, name=f"bwd_ffn{layer}", grid=(t // tm,),
        out_shape=[jax.ShapeDtypeStruct((t, d), F32), jax.ShapeDtypeStruct((t, 2 * f), BF16),
                   jax.ShapeDtypeStruct((t, d), BF16), jax.ShapeDtypeStruct((t, f), BF16),
                   jax.ShapeDtypeStruct((8, d), F32)],
        in_specs=[pl.BlockSpec((tm, d), row), pl.BlockSpec((tm, d), row), pl.BlockSpec((tm, d), row),
                  pl.BlockSpec((tm, 2 * f), row), _resident(small.shape, lambda i: (0, 0)),
                  _resident(wgu.shape, lambda i: (0, 0, 0)), _resident(wd.shape, lambda i: (0, 0))],
        out_specs=[pl.BlockSpec((tm, d), row), pl.BlockSpec((tm, 2 * f), row), pl.BlockSpec((tm, d), row),
                   pl.BlockSpec((tm, f), row), pl.BlockSpec((8, d), lambda i: (0, 0))],
        args=[dh, h, ff, gu, small, wgu, wd])


def _bwd_conv(dh, h, y, proj, small, win, wout, comm=None):
    t, d = h.shape
    tm = _token_tile(t)
    steps = t // tm
    pc = win.shape[-1]
    halo_blocks = tm // 16

    def body(dh_ref, h_ref, y_ref, proj_ref, halo_ref, sm_ref, win_ref, wout_ref,
             o_ref, dproj_ref, dy_ref, bc_ref, sg_ref, uext_ref, dcext_ref, carry_ref):
        i = pl.program_id(0)
        tile = steps - 1 - i

        @pl.when(i == 0)
        def _():
            sg_ref[...] = jnp.zeros_like(sg_ref)
            carry_ref[...] = jnp.zeros_like(carry_ref)

        dy = dh_ref[...]
        yh, r1 = _rms(y_ref[...])
        sg_ref[1:2, :] += _colsum(dy * yh)
        dyv = _rms_bwd(dy, yh, r1, sm_ref[5:6, :]).astype(BF16)
        dy_ref[...] = dyv
        dbc = _dot_nt(dyv, wout_ref[...])
        b = proj_ref[:, 0:d].astype(F32)
        cg = proj_ref[:, d:2 * d].astype(F32)
        v = proj_ref[:, 2 * d:].astype(F32)
        halo = halo_ref[...].astype(F32)[16 - CONV_HALO:]
        uh = halo[:, d:2 * d] * halo[:, 2 * d:]
        uext_ref[0:CONV_HALO, :] = jnp.where(tile > 0, uh, jnp.zeros_like(uh))
        uext_ref[CONV_HALO:, :] = cg * v
        taps = [sm_ref[8 + j:9 + j, :] for j in range(3)]
        full = uext_ref[...]
        u0 = full[CONV_HALO:]
        u1 = pltpu.roll(full, 1, 0)[CONV_HALO:]
        u2 = pltpu.roll(full, 2, 0)[CONV_HALO:]
        conv = u0 * taps[2] + u1 * taps[1] + u2 * taps[0]
        bc_ref[...] = (b * conv).astype(BF16)
        dconv = dbc * b
        sg_ref[4:5, :] += _colsum(dconv * u0)
        sg_ref[3:4, :] += _colsum(dconv * u1)
        sg_ref[2:3, :] += _colsum(dconv * u2)
        dcext_ref[0:tm, :] = dconv
        dcext_ref[tm:, :] = carry_ref[...]
        carry_ref[...] = dconv[0:CONV_HALO]
        dfull = dcext_ref[...]
        n8 = tm + CONV_HALO
        du = (dfull[0:tm] * taps[2] + pltpu.roll(dfull, n8 - 1, 0)[0:tm] * taps[1]
              + pltpu.roll(dfull, n8 - 2, 0)[0:tm] * taps[0])
        dproj_ref[:, 0:d] = (dbc * conv).astype(BF16)
        dproj_ref[:, d:2 * d] = (du * v).astype(BF16)
        dproj_ref[:, 2 * d:] = (du * cg).astype(BF16)
        dn = None
        for k in range(N_CHIPS):
            part = _dot_nt(dproj_ref[:, k * pc:(k + 1) * pc], win_ref[k])
            dn = part if dn is None else dn + part
        hh, r0 = _rms(h_ref[...])
        sg_ref[0:1, :] += _colsum(dn * hh)
        o_ref[...] = dy + _rms_bwd(dn, hh, r0, sm_ref[4:5, :])

    rev = lambda i: (steps - 1 - i, 0)
    before = lambda i: (jnp.maximum((steps - 1 - i) * halo_blocks - 1, 0), 0)
    return _hosted(
        body, comm, name="bwd_conv", grid=(steps,),
        out_shape=[jax.ShapeDtypeStruct((t, d), F32), jax.ShapeDtypeStruct((t, 3 * d), BF16),
                   jax.ShapeDtypeStruct((t, d), BF16), jax.ShapeDtypeStruct((t, d), BF16),
                   jax.ShapeDtypeStruct((8, d), F32)],
        in_specs=[pl.BlockSpec((tm, d), rev), pl.BlockSpec((tm, d), rev), pl.BlockSpec((tm, d), rev),
                  pl.BlockSpec((tm, 3 * d), rev), pl.BlockSpec((16, 3 * d), before),
                  _resident(small.shape, lambda i: (0, 0)), _resident(win.shape, lambda i: (0, 0, 0)),
                  _resident(wout.shape, lambda i: (0, 0))],
        out_specs=[pl.BlockSpec((tm, d), rev), pl.BlockSpec((tm, 3 * d), rev), pl.BlockSpec((tm, d), rev),
                   pl.BlockSpec((tm, d), rev), pl.BlockSpec((8, d), lambda i: (0, 0))],
        scratch_shapes=[pltpu.VMEM((CONV_HALO + tm, d), F32), pltpu.VMEM((tm + CONV_HALO, d), F32),
                        pltpu.VMEM((CONV_HALO, d), F32)],
        args=[dh, h, y, proj, proj, small, win, wout])


def _bwd_pool(dh, x, small, scale, poolw, comm=None):
    t, d = x.shape
    tm = _token_tile(t)
    steps = t // tm
    ng = len(POOL_WINDOWS)
    gw = d // ng
    halo_blocks = tm // POOL_HALO

    def body(dh_ref, x_ref, halo_ref, sm_ref, sc_ref, w_ref, o_ref, dw_ref, sg_ref,
             ext_ref, mix_ref, mm_ref, pb_ref, qext_ref, dhn_ref, carry_ref):
        i = pl.program_id(0)
        tile = steps - 1 - i

        @pl.when(i == 0)
        def _():
            sg_ref[...] = jnp.zeros_like(sg_ref)
            dw_ref[...] = jnp.zeros_like(dw_ref)
            carry_ref[...] = jnp.zeros_like(carry_ref)

        g0 = sm_ref[0:1, :]
        xv = x_ref[...]
        xh, r0 = _rms(xv)
        hx, _ = _rms(halo_ref[...])
        ext_ref[0:POOL_HALO, :] = jnp.where(tile > 0, hx * g0, jnp.zeros_like(hx))
        ext_ref[POOL_HALO:, :] = xh * g0
        for g in range(ng):
            pooled = _pool_windows(ext_ref, g, gw, tm, tile * tm)
            cols = slice(g * gw, (g + 1) * gw)
            pb = pooled.astype(BF16)
            pb_ref[:, cols] = pb
            mm = _dot(pb, w_ref[g])
            mm_ref[:, cols] = mm
            mix_ref[:, cols] = mm * sc_ref[:, cols]
        dy = dh_ref[...]
        mh, r1 = _rms(mix_ref[...])
        sg_ref[1:2, :] += _colsum(dy * mh)
        dmix = _rms_bwd(dy, mh, r1, sm_ref[1:2, :])
        sg_ref[2:3, :] += _colsum(dmix * mm_ref[...])
        mix_ref[...] = dmix * sc_ref[...]
        n16 = tm + POOL_HALO
        for g in range(ng):
            w = POOL_WINDOWS[g]
            cols = slice(g * gw, (g + 1) * gw)
            dmm = mix_ref[:, cols].astype(BF16)
            dpooled = _dot_nt(dmm, w_ref[g])
            dw_ref[g] += _dot_tn(pb_ref[:, cols], dmm)
            trow = tile * tm + lax.broadcasted_iota(jnp.int32, (tm, 1), 0)
            q = dpooled / jnp.minimum(trow + 1, w).astype(F32)
            qext_ref[0:tm, cols] = q
            qext_ref[tm:, cols] = carry_ref[:, cols]
            carry_ref[:, cols] = q[0:POOL_HALO]
            p, k = qext_ref[:, cols], 1
            while k < w:
                p = p + pltpu.roll(p, n16 - k, 0)
                k *= 2
            dhn_ref[:, cols] = p[0:tm] - dpooled
        dhn = dhn_ref[...]
        sg_ref[0:1, :] += _colsum(dhn * xh)
        o_ref[...] = dy + _rms_bwd(dhn, xh, r0, g0)

    rev = lambda i: (steps - 1 - i, 0)
    before = lambda i: (jnp.maximum((steps - 1 - i) * halo_blocks - 1, 0), 0)
    return _hosted(
        body, comm, name="bwd_pool", grid=(steps,),
        out_shape=[jax.ShapeDtypeStruct((t, d), F32), jax.ShapeDtypeStruct((ng, gw, gw), F32),
                   jax.ShapeDtypeStruct((8, d), F32)],
        in_specs=[pl.BlockSpec((tm, d), rev), pl.BlockSpec((tm, d), rev), pl.BlockSpec((POOL_HALO, d), before),
                  _resident(small.shape, lambda i: (0, 0)), _resident(scale.shape, lambda i: (0, 0)),
                  _resident(poolw.shape, lambda i: (0, 0, 0))],
        out_specs=[pl.BlockSpec((tm, d), rev), pl.BlockSpec((ng, gw, gw), lambda i: (0, 0, 0)),
                   pl.BlockSpec((8, d), lambda i: (0, 0))],
        scratch_shapes=[pltpu.VMEM((POOL_HALO + tm, d), F32), pltpu.VMEM((tm, d), F32), pltpu.VMEM((tm, d), F32),
                        pltpu.VMEM((tm, d), BF16), pltpu.VMEM((tm + POOL_HALO, d), F32), pltpu.VMEM((tm, d), F32),
                        pltpu.VMEM((POOL_HALO, d), F32)],
        args=[dh, x, x, small, scale, poolw])


def _weight_grad(a, b, bm, bn, half_on, name, comm=None, rows=(0, 1)):
    t, m = a.shape
    _, n = b.shape
    if half_on == "a":
        a_cols, b_cols = 2 * bm, bn
    else:
        a_cols, b_cols = bm, 2 * bn
    steps = max(m // a_cols, n // b_cols)
    sub, n_sub = rows
    tr = bm // n_sub

    def spec(cols, total):
        if cols == total:
            return _resident((t, cols), lambda p, j: (0, 0))
        return pl.BlockSpec((t, cols), lambda p, j: (0, j))

    def tile(a_ref, b_ref, half):
        if half_on == "a":
            first = half * bm + sub * tr
            return _dot_tn(a_ref[:, first:first + tr], b_ref[...])
        return _dot_tn(a_ref[...], b_ref[:, half * bn:(half + 1) * bn])

    def body(a_ref, b_ref, parts_ref, land_ref, acc_ref, stage_ref, got_ref, send_sems, recv_sems, got_sem):
        p, j = pl.program_id(0), pl.program_id(1)
        x, y, c, _ = _place()
        half = jnp.where(p == 0, 1 - c, c)

        def send(jj):
            return _remote(stage_ref.at[jj % 2], land_ref.at[jj], send_sems.at[jj], recv_sems.at[jj], (x, y, 1 - c))

        def fetch():
            return pltpu.make_async_copy(land_ref.at[j], got_ref, got_sem)

        @pl.when(p == 1)
        def _():
            @pl.when(j == 0)
            def _():
                for jj in range(max(steps - 2, 0), steps):
                    send(jj).wait_send()

            send(j).wait_recv()
            fetch().start()

        for hv in range(2):
            @pl.when(half == hv)
            def _():
                acc_ref[...] = tile(a_ref, b_ref, hv)

        @pl.when(p == 0)
        def _():
            @pl.when(j >= 2)
            def _():
                send(j - 2).wait_send()

            stage_ref[j % 2] = acc_ref[...].astype(BF16)
            send(j).start()

        @pl.when(p == 1)
        def _():
            fetch().wait()
            parts_ref[...] = (acc_ref[...] + got_ref[...].astype(F32)).astype(BF16)

    (parts, _), got = _hosted(
        body, comm, name=name, grid=(2, steps),
        out_shape=[jax.ShapeDtypeStruct((steps, tr, bn), BF16), jax.ShapeDtypeStruct((steps, tr, bn), BF16)],
        in_specs=[spec(a_cols, m), spec(b_cols, n)],
        out_specs=[pl.BlockSpec((None, tr, bn), lambda p, j: (p * j, 0, 0)), ANY],
        scratch_shapes=[pltpu.VMEM((tr, bn), F32), pltpu.VMEM((2, tr, bn), BF16), pltpu.VMEM((tr, bn), BF16),
                        DMA((steps,)), DMA((steps,)), DMA],
        args=[a, b])
    return parts, got


def _adamw_math(w, g, m, v):
    bc1 = 1.0 - ADAM_B1 ** ADAM_STEP
    bc2 = 1.0 - ADAM_B2 ** ADAM_STEP
    nm = ADAM_B1 * m + (1.0 - ADAM_B1) * g
    nv = ADAM_B2 * v + (1.0 - ADAM_B2) * (g * g)
    return -ADAM_LR * ((nm / bc1) / (jnp.sqrt(nv / bc2) + ADAM_EPS) + ADAM_WD * w), nm, nv


def _adamw_small(small_sum, where, gains, taps, scale):
    dq = gains[0].shape[-1]
    d = small_sum.shape[-1]

    def body(where_ref, mine_ref, all_ref, gw, gm, gv, tw, tm_, tv, sw, sm, sv,
             gg, gd, gnm, gnv, tg, td, tnm, tnv, sg, sd, snm, snv):
        for layer in range(gw.shape[0]):
            g = mine_ref[4 * layer:4 * layer + 4, :]
            gg[layer] = g
            gd[layer], gnm[layer], gnv[layer] = _adamw_math(gw[layer], g, gm[layer], gv[layer])
        g = mine_ref[8:8 + tw.shape[1], :]
        tg[0] = g
        td[0], tnm[0], tnv[0] = _adamw_math(tw[0], g, tm_[0], tv[0])
        g = all_ref[11:12, :]
        sg[...] = g
        sd[...], snm[...], snv[...] = _adamw_math(sw[...], g, sm[...], sv[...])

    full = lambda a: pl.BlockSpec(a.shape, lambda i, where_ref: (0,) * a.ndim)
    params = [*gains, *taps, *scale]
    outs = [gains[0]] * 4 + [taps[0]] * 4 + [scale[0]] * 4
    res = pl.pallas_call(
        body, name="adamw_small", out_shape=[jax.ShapeDtypeStruct(a.shape, F32) for a in outs],
        grid_spec=pltpu.PrefetchScalarGridSpec(
            num_scalar_prefetch=1, grid=(1,),
            in_specs=[pl.BlockSpec((16, dq), lambda i, where_ref: (0, where_ref[0])), pl.BlockSpec((16, d), lambda i, where_ref: (0, 0)),
                      *[full(a) for a in params]],
            out_specs=[full(a) for a in outs]),
    )(where, small_sum, small_sum, *params)
    return tuple(res[0:4]), tuple(res[4:8]), tuple(res[8:12])


def _adamw(w, g, m, v, name):
    r, c = w.shape
    rb = _row_block(r, c * (8 * 4 * 2 + 4 * 4))

    def body(w_ref, g_ref, m_ref, v_ref, d_ref, nm_ref, nv_ref, go_ref):
        gv = g_ref[...]
        go_ref[...] = gv
        d_ref[...], nm_ref[...], nv_ref[...] = _adamw_math(w_ref[...], gv, m_ref[...], v_ref[...])

    spec = pl.BlockSpec((rb, c), lambda i: (i, 0))
    return pl.pallas_call(
        body, name=name, grid=(r // rb,), out_shape=[jax.ShapeDtypeStruct((r, c), F32)] * 4,
        in_specs=[spec] * 4, out_specs=[spec] * 4,
        compiler_params=pltpu.CompilerParams(dimension_semantics=("parallel",), vmem_limit_bytes=VMEM_LIMIT),
    )(w, g, m, v)


def kernel(x, norm_gains, pool_w, pool_scale, conv_in_w, conv_w, conv_out_w, ffn_gate_up_w, ffn_down_w, loss_target, m_norm_gains, m_pool_w, m_pool_scale, m_conv_in_w, m_conv_w, m_conv_out_w, m_ffn_gate_up_w, m_ffn_down_w, v_norm_gains, v_pool_w, v_pool_scale, v_conv_in_w, v_conv_w, v_conv_out_w, v_ffn_gate_up_w, v_ffn_down_w):
    _, t, d = x.shape
    dq = d // N_CHIPS
    ng = len(POOL_WINDOWS)
    gw = d // ng
    fq = ffn_down_w.shape[1]
    f = N_CHIPS * fq
    fc = f // 2
    core = lax.axis_index("c")
    chip = 2 * lax.axis_index("x") + lax.axis_index("y")
    core_arr = jnp.reshape(core, (1,)).astype(jnp.int32)
    where_arr = jnp.stack([chip, core]).astype(jnp.int32)
    x2, target = x[0], loss_target[0]

    small_loc = jnp.concatenate(
        [norm_gains.reshape(8, dq), conv_w[0], jnp.zeros((5, dq), F32)], axis=0).reshape(1, 2, 8, dq)
    pool_loc = pool_w.astype(BF16).reshape(1, 2, ng // 2 * (gw // N_CHIPS), gw)
    wgu_loc = [ffn_gate_up_w[l:l + 1].astype(BF16).reshape(1, 2, d // 2, fc) for l in range(2)]
    wd_loc = [ffn_down_w[l:l + 1].astype(BF16).reshape(1, 2, fq // 2, d) for l in range(2)]
    win_loc = conv_in_w.astype(BF16).reshape(1, 2, d // 2, -1)
    wout_loc = conv_out_w.astype(BF16).reshape(1, 2, dq // 2, d)

    def ffn_weights(wgu_f, wd_f):
        return wgu_f.reshape(N_CHIPS, d, fc), wd_f.reshape(f, d)

    ag0 = _SplitGather([(pool_loc, 0), (small_loc, 0), (wgu_loc[0], 0), (wd_loc[0], 0)])
    ag0.start("ag_start_layer0")
    ag1 = _SplitGather([(win_loc, 0), (wout_loc, 0), (wgu_loc[1], 0), (wd_loc[1], 0)])
    ag1.start("ag_start_layer1", [ag0.token])
    pool_f, small_f = _pass_on(ag0.wait([0, 1], ag1.token, "ag_wait_first"), "ag_pass_first")
    poolw = pool_f.reshape(N_CHIPS, ng, gw // N_CHIPS, gw).transpose(1, 0, 2, 3).reshape(ng, gw, gw)
    small = small_f.transpose(1, 2, 0, 3).reshape(16, d)
    h1, _ = _fwd_pool(x2, small, pool_scale, poolw)
    wgu0, wd0 = ffn_weights(*_pass_on(ag0.wait([2, 3], h1, "ag_wait_ffn0"), "ag_pass_ffn0"))
    (h2, gu0, ff0, n0), _ = _fwd_ffn(h1, small, wgu0, wd0, 0)
    win_f, wout_f = _pass_on(ag1.wait([0, 1], h2, "ag_wait_conv"), "ag_pass_conv")
    win_f, wout_f = win_f.reshape(N_CHIPS, d, -1), wout_f.reshape(d, d)
    (h3, proj, y, nc), _ = _fwd_conv(h2, small, win_f, wout_f)
    wgu1, wd1 = ffn_weights(*_pass_on(ag1.wait([2, 3], h3, "ag_wait_ffn1"), "ag_pass_ffn1"))
    (dh4, gu1, ff1, n1, loss_blk), _ = _fwd_ffn(h3, small, wgu1, wd1, 1, target=target)

    (dh3, dgu1, dff1, act1, sg_f1), _ = _bwd_ffn(dh4, h3, ff1, gu1, small, wgu1, wd1, 1)
    parts_d1, _ = _weight_grad(act1, dff1, fc, d // 2, "b", "dw_down1")
    parts_gu1, _ = _weight_grad(n1, dgu1, d // 2, fc, "a", "dw_gate_up1")
    ex_ffn1 = _SplitExchange([parts_d1.reshape(N_CHIPS, fq, d // 2), parts_gu1])
    started = ex_ffn1.start("rs_start_ffn1")
    (dh2, dproj, dyv, bcv, sg_c), _ = _bwd_conv(dh3, h2, y, proj, small, win_f, wout_f, _After(started))
    parts_in, _ = _weight_grad(nc, dproj, d // 2, 3 * d // N_CHIPS, "a", "dw_conv_in")
    parts_out, _ = _weight_grad(bcv, dyv, dq // 2, d, "a", "dw_conv_out")
    ex_conv = _SplitExchange([parts_in, parts_out])
    started = ex_conv.start("rs_start_conv")
    (dh1, dgu0, dff0, act0, sg_f0), _ = _bwd_ffn(dh2, h1, ff0, gu0, small, wgu0, wd0, 0, _After(started))
    parts_d0, _ = _weight_grad(act0, dff0, fc, d // 2, "b", "dw_down0")
    parts_gu0, _ = _weight_grad(n0, dgu0, d // 2, fc, "a", "dw_gate_up0")
    ex_ffn0 = _SplitExchange([parts_d0.reshape(N_CHIPS, fq, d // 2), parts_gu0])
    started = ex_ffn0.start("rs_start_ffn0")
    (grad_x, dpool, sg_p), _ = _bwd_pool(dh1, x2, small, pool_scale, poolw, _After(started))
    g_pool = dpool.astype(BF16).reshape(2, ng // 2, N_CHIPS, gw // N_CHIPS, gw).transpose(2, 0, 1, 3, 4).reshape(
        N_CHIPS, 2, ng // 2 * (gw // N_CHIPS), gw)
    small_g = jnp.concatenate(
        [sg_p[0:2], sg_f0[0:2], sg_c[0:2], sg_f1[0:2], sg_c[2:5], sg_p[2:3],
         jnp.broadcast_to(loss_blk[0:1, 0:1], (1, d)), jnp.zeros((3, d), F32)], axis=0)
    (land_p,) = _sibling_exchange([g_pool], None, "rs_sibling_pool")
    ex_pool = _SplitExchange([_add_sibling(g_pool, land_p, core_arr)])
    started = ex_pool.start("rs_start_pool")

    def update(w, g, m, v, name):
        flat = (-1, w.shape[-1])
        dl, m2, v2, g2 = _adamw(w.reshape(flat), g.reshape(flat), m.reshape(flat), v.reshape(flat), "adamw_" + name)
        return tuple(o.reshape(w.shape) for o in (g2, dl, m2, v2))

    (parts_d1, parts_gu1), (recv_d1, recv_gu1) = ex_ffn1.wait([started], "rs_wait_ffn1")
    (parts_in, parts_out), (recv_in, recv_out) = ex_conv.wait([started], "rs_wait_conv")
    gs_gu = _add_chips(parts_gu1, recv_gu1, where_arr, 1, 2)
    gs_d = _add_chips(parts_d1, recv_d1, where_arr, 1, 2, col_half=True)
    gs_in = _add_chips(parts_in, recv_in, where_arr)
    gs_out = _add_chips(parts_out, recv_out, where_arr)
    full_in, full_out = _sibling_share([gs_in, gs_out], [False, False], "rs_share_conv")
    up_in = update(conv_in_w, full_in.reshape(1, d, -1), m_conv_in_w, v_conv_in_w, "conv_in")
    up_out = update(conv_out_w, full_out.reshape(1, dq, d), m_conv_out_w, v_conv_out_w, "conv_out")
    done_first = [up_in[1], up_out[1], gs_gu, gs_d]
    (parts_d0, parts_gu0), (recv_d0, recv_gu0) = ex_ffn0.wait(done_first, "rs_wait_ffn0")
    (parts_p,), (recv_p,) = ex_pool.wait(done_first, "rs_wait_pool")
    (small_all,) = _sibling_exchange([], small_g, "rs_small_gather", after=recv_gu0)
    gs_gu = _add_chips(parts_gu0, recv_gu0, where_arr, 0, 2, gs_gu)
    gs_d = _add_chips(parts_d0, recv_d0, where_arr, 0, 2, gs_d, col_half=True)
    gs_pool = _add_chips(parts_p, recv_p, where_arr)
    full_gu, full_d, full_pool = _sibling_share([gs_gu, gs_d, gs_pool], [False, True, False], "rs_share_ffn", [small_all])
    small_sum = _sum_small(small_all)
    loss = small_sum[12, 0]
    up_gains, up_taps, up_scale = _adamw_small(
        small_sum, where_arr, (norm_gains, m_norm_gains, v_norm_gains), (conv_w, m_conv_w, v_conv_w),
        (pool_scale, m_pool_scale, v_pool_scale))

    ups = [
        up_gains,
        update(pool_w, full_pool.reshape(1, ng, gw // N_CHIPS, gw), m_pool_w, v_pool_w, "pool_w"),
        up_scale,
        up_in,
        up_taps,
        up_out,
        update(ffn_gate_up_w, full_gu.reshape(2, d, fc), m_ffn_gate_up_w, v_ffn_gate_up_w, "gate_up"),
        update(ffn_down_w, full_d.reshape(2, fq, d), m_ffn_down_w, v_ffn_down_w, "down"),
    ]
    grads_out, deltas, new_ms, new_vs = (list(col) for col in zip(*ups))
    return (loss, grad_x[None], *grads_out, *deltas, *new_ms, *new_vs)
```

```python
import jax
import jax.numpy as jnp
from jax import lax
from jax.experimental import pallas as pl
from jax.experimental.pallas import tpu as pltpu

RMS_EPS = 1e-6
POOL_WINDOWS = (2, 4, 8, 16)
POOL_HALO = 16
CONV_HALO = 8
N_CHIPS = 4
N_DEV = 8
ADAM_LR = 0.001
ADAM_B1 = 0.9
ADAM_B2 = 0.999
ADAM_EPS = 1e-08
ADAM_WD = 0.01
ADAM_STEP = 10
VMEM_LIMIT = 56 * 2**20
STREAM_BUDGET = 24 * 2**20
MESH = pl.DeviceIdType.MESH
ANY = pl.BlockSpec(memory_space=pl.ANY)
DMA = pltpu.SemaphoreType.DMA
BF16 = jnp.bfloat16
F32 = jnp.float32


def _token_tile(t, rows=512):
    return min(rows, t)


def _rms(x):
    r = lax.rsqrt(jnp.mean(x * x, axis=-1, keepdims=True) + RMS_EPS)
    return x * r, r


def _rms_bwd(dy, xh, r, g):
    a = dy * g
    return r * (a - xh * jnp.mean(a * xh, axis=-1, keepdims=True))


def _dot(a, b):
    return jnp.dot(a, b, preferred_element_type=F32)


def _dot_nt(a, b):
    return lax.dot_general(a, b, (((1,), (1,)), ((), ())), preferred_element_type=F32)


def _dot_tn(a, b):
    return lax.dot_general(a, b, (((0,), (0,)), ((), ())), preferred_element_type=F32)


def _colsum(a):
    return jnp.sum(a, axis=0, keepdims=True)


def _resident(block, index_map):
    return pl.BlockSpec(block, index_map, pipeline_mode=pl.Buffered(1))


def _row_block(r, row_bytes):
    best = None
    for rb in range(16, r + 1, 16):
        if r % rb == 0 and rb * row_bytes <= STREAM_BUDGET:
            best = rb
    return best if best is not None else r


def _place():
    x, y, c = lax.axis_index("x"), lax.axis_index("y"), lax.axis_index("c")
    return x, y, c, 2 * x + y


def _dev(chip, core):
    return (chip // 2, chip % 2, core)


def _remote(src, dst, send_sem, recv_sem, device):
    return pltpu.make_async_remote_copy(src_ref=src, dst_ref=dst, send_sem=send_sem, recv_sem=recv_sem,
                                        device_id=device, device_id_type=MESH)


class _Gather:
    def __init__(self, shards):
        n = len(shards)
        self.args = [s for s, _ in shards]
        self.layers = [l for _, l in shards]
        self.out_shape = [jax.ShapeDtypeStruct((N_CHIPS,) + s.shape[1:], s.dtype) for s in self.args]
        self.sems = [DMA((n,)), DMA((n,)), DMA((n, 3)), DMA((n, 3)), DMA((n, 3)), DMA((n, 3))]

    def _own(self, loc, out, sems, a):
        x, y, c, k = _place()
        return _remote(loc[a].at[self.layers[a]], out[a].at[k], sems[0].at[a], sems[1].at[a], (x, y, 1 - c))

    def _ici(self, loc, out, sems, a, m, arrival):
        x, y, c, k = _place()
        dst = out[a].at[k ^ m, c] if arrival else out[a].at[k, c]
        return _remote(loc[a].at[self.layers[a], c], dst, sems[2].at[a, m - 1], sems[3].at[a, m - 1], _dev(k ^ m, c))

    def _forward(self, out, sems, a, m, arrival):
        x, y, c, k = _place()
        got = out[a].at[k ^ m, 1 - c] if arrival else out[a].at[k ^ m, c]
        return _remote(got, got, sems[4].at[a, m - 1], sems[5].at[a, m - 1], (x, y, 1 - c))

    def start(self, loc, out, sems):
        for a in range(len(self.args)):
            for m in range(1, N_CHIPS):
                self._ici(loc, out, sems, a, m, False).start()
            self._own(loc, out, sems, a).start()

    def finish(self, loc, out, sems):
        n = len(self.args)
        for a in range(n):
            for m in range(1, N_CHIPS):
                self._ici(loc, out, sems, a, m, True).wait_recv()
                self._forward(out, sems, a, m, False).start()
        for a in range(n):
            for m in range(1, N_CHIPS):
                self._forward(out, sems, a, m, True).wait_recv()
            self._own(loc, out, sems, a).wait_recv()
        for a in range(n):
            for m in range(1, N_CHIPS):
                self._ici(loc, out, sems, a, m, False).wait_send()
                self._forward(out, sems, a, m, False).wait_send()
            self._own(loc, out, sems, a).wait_send()


class _SmallGather:
    PEERS = N_DEV - 1

    def __init__(self, small):
        self.args = [small]
        self.out_shape = [jax.ShapeDtypeStruct((N_DEV,) + small.shape, small.dtype)]

    def _copy(self, sm, land, sems, m, arrival):
        x, y, c, k = _place()
        me = 2 * k + c
        peer = me ^ m
        return _remote(sm[0], land[0].at[peer if arrival else me], sems[0].at[0, m - 1], sems[1].at[0, m - 1],
                       (peer // 4, (peer // 2) % 2, peer % 2))

    def start(self, sm, land, sems):
        for m in range(1, N_DEV):
            self._copy(sm, land, sems, m, False).start()

    def finish(self, sm, land, sems):
        for m in range(1, N_DEV):
            self._copy(sm, land, sems, m, True).wait_recv()
        for m in range(1, N_DEV):
            self._copy(sm, land, sems, m, False).wait_send()


class _ChipExchange:
    PEERS = N_CHIPS - 1

    def __init__(self, parts):
        n = len(parts)
        self.args = list(parts)
        self.out_shape = [jax.ShapeDtypeStruct((N_CHIPS - 1,) + p.shape[1:], p.dtype) for p in parts]
        self.sems = [DMA((n, 3)), DMA((n, 3))]

    def _copy(self, p, land, sems, a, m):
        x, y, c, k = _place()
        return _remote(p[a].at[k ^ m], land[a].at[m - 1], sems[0].at[a, m - 1], sems[1].at[a, m - 1], _dev(k ^ m, c))

    def start(self, p, land, sems):
        for a in range(len(self.args)):
            for m in range(1, N_CHIPS):
                self._copy(p, land, sems, a, m).start()

    def finish(self, p, land, sems):
        for a in range(len(self.args)):
            for m in range(1, N_CHIPS):
                self._copy(p, land, sems, a, m).wait_recv()
        for a in range(len(self.args)):
            for m in range(1, N_CHIPS):
                self._copy(p, land, sems, a, m).wait_send()


def _hosted(body, comm, *, name, grid, in_specs, out_specs, out_shape, args, scratch_shapes=()):
    ni, no, ns = len(in_specs), len(out_shape), len(scratch_shapes)
    if comm is None:
        res = pl.pallas_call(
            body, name=name, grid=grid, in_specs=list(in_specs), out_specs=list(out_specs), out_shape=list(out_shape),
            scratch_shapes=list(scratch_shapes),
            compiler_params=pltpu.CompilerParams(dimension_semantics=("arbitrary",) * len(grid), vmem_limit_bytes=VMEM_LIMIT),
        )(*args)
        return list(res), []
    nc, nco = len(comm.args), len(comm.out_shape)

    def full(*refs):
        cin = refs[ni:ni + nc]
        outs = refs[ni + nc:ni + nc + no]
        cout = refs[ni + nc + no:ni + nc + no + nco]
        scratch = refs[ni + nc + no + nco:ni + nc + no + nco + ns]
        csems = refs[ni + nc + no + nco + ns:]
        first = _all_of([pl.program_id(ax) == 0 for ax in range(len(grid))])
        last = _all_of([pl.program_id(ax) == grid[ax] - 1 for ax in range(len(grid))])

        @pl.when(first)
        def _():
            comm.start(cin, cout, csems)

        body(*refs[:ni], *outs, *scratch)

        @pl.when(last)
        def _():
            comm.finish(cin, cout, csems)

    res = pl.pallas_call(
        full, name=name, grid=grid, in_specs=[*in_specs, *[ANY] * nc], out_specs=[*out_specs, *[ANY] * nco],
        out_shape=[*out_shape, *comm.out_shape], scratch_shapes=[*scratch_shapes, *comm.sems],
        compiler_params=pltpu.CompilerParams(dimension_semantics=("arbitrary",) * len(grid), vmem_limit_bytes=VMEM_LIMIT,
                                             has_side_effects=True),
    )(*args, *comm.args)
    return list(res[:no]), list(res[no:])


def _all_of(conds):
    out = conds[0]
    for c in conds[1:]:
        out = jnp.logical_and(out, c)
    return out


class _After:
    def __init__(self, token):
        self.args, self.out_shape, self.sems = [token], [], []

    def start(self, *_):
        pass

    def finish(self, *_):
        pass


def _alone(comm, name):
    return _hosted(lambda: None, comm, name=name, grid=(1,), in_specs=[], out_specs=[], out_shape=[], args=[])[1]


HBM = pl.BlockSpec(memory_space=pltpu.HBM)
SEM = pl.BlockSpec(memory_space=pltpu.SEMAPHORE)
DATAFLOW = pltpu.SideEffectType.DATAFLOW_SIDE_EFFECTING


class _SplitGather:
    PER_ARRAY = 8

    def __init__(self, shards):
        self.plan = _Gather(shards)
        self.n = len(shards)

    @staticmethod
    def _tables(sems_of):
        class Table:
            def __init__(self, pick):
                self.pick = pick

            @property
            def at(self):
                return self

            def __getitem__(self, idx):
                return self.pick(idx)

        return [Table(lambda a: sems_of[a][0]), Table(lambda a: sems_of[a][1]),
                Table(lambda am: sems_of[am[0]][2 + am[1]]), Table(lambda am: sems_of[am[0]][5 + am[1]])]

    def start(self, name, after=()):
        n, plan, per, na = self.n, self.plan, self.PER_ARRAY, len(after)

        def body(*refs):
            loc, land = refs[:n], refs[n:2 * n]
            sems_of = {a: refs[2 * n + na + per * a:2 * n + na + per * (a + 1)] for a in range(n)}
            plan.start(loc, land, self._tables(sems_of))
            refs[-1][...] = jnp.zeros_like(refs[-1])

        lands = [pltpu.with_memory_space_constraint(lax.empty(o.shape, o.dtype), pltpu.HBM) for o in plan.out_shape]
        locs = [pltpu.with_memory_space_constraint(a, pltpu.HBM) for a in plan.args]
        res = pl.pallas_call(
            body, name=name,
            out_shape=[*[DMA(())] * (per * n),
                       *[pltpu.HBM(o.shape, o.dtype) for o in plan.out_shape],
                       jax.ShapeDtypeStruct((8, 128), F32)],
            in_specs=[HBM] * (2 * n) + [pl.BlockSpec(memory_space=pl.ANY)] * na,
            out_specs=[SEM] * (per * n) + [HBM] * n + [pl.BlockSpec(memory_space=pltpu.VMEM)],
            input_output_aliases={n + i: per * n + i for i in range(n)},
            compiler_params=pltpu.CompilerParams(has_side_effects=DATAFLOW),
        )(*locs, *lands, *after)
        self.sems = {a: list(res[per * a:per * (a + 1)]) for a in range(n)}
        self.locs = locs
        self.lands = list(res[per * n:per * n + n])
        self.token = res[-1]

    def wait(self, idxs, after, name):
        plan, g, per = self.plan, len(idxs), self.PER_ARRAY

        def body(*refs):
            loc = {a: refs[j] for j, a in enumerate(idxs)}
            land = {a: refs[g + j] for j, a in enumerate(idxs)}
            sems = self._tables({a: refs[2 * g + per * j:2 * g + per * (j + 1)] for j, a in enumerate(idxs)})
            for a in idxs:
                for m in range(1, N_CHIPS):
                    plan._ici(loc, land, sems, a, m, True).wait_recv()
                    plan._ici(loc, land, sems, a, m, False).wait_send()
                plan._own(loc, land, sems, a).wait_recv()
                plan._own(loc, land, sems, a).wait_send()

        res = pl.pallas_call(
            body, name=name,
            out_shape=[pltpu.HBM(self.lands[a].shape, self.lands[a].dtype) for a in idxs],
            in_specs=[HBM] * (2 * g) + [SEM] * (per * g) + [pl.BlockSpec(memory_space=pl.ANY)], out_specs=[HBM] * g,
            input_output_aliases={g + j: j for j in range(g)},
            compiler_params=pltpu.CompilerParams(has_side_effects=DATAFLOW),
        )(*[self.locs[a] for a in idxs], *[self.lands[a] for a in idxs],
          *[s for a in idxs for s in self.sems[a]], after)
        return list(res)


class _SplitExchange:
    def __init__(self, parts, plan=None):
        self.plan = _ChipExchange(parts) if plan is None else plan
        self.n = len(parts)
        self.PER_ARRAY = 2 * self.plan.PEERS

    def _tables(self, sems_of):
        class Table:
            def __init__(self, pick):
                self.pick = pick

            @property
            def at(self):
                return self

            def __getitem__(self, am):
                return self.pick(am)

        peers = self.plan.PEERS
        return [Table(lambda am: sems_of[am[0]][am[1]]), Table(lambda am: sems_of[am[0]][peers + am[1]])]

    def start(self, name, after=()):
        n, plan, per, na = self.n, self.plan, self.PER_ARRAY, len(after)

        def body(*refs):
            p, land = refs[:n], refs[n:2 * n]
            sems_of = {a: refs[2 * n + na + per * a:2 * n + na + per * (a + 1)] for a in range(n)}
            plan.start(p, land, self._tables(sems_of))
            refs[-1][...] = jnp.zeros_like(refs[-1])

        lands = [pltpu.with_memory_space_constraint(lax.empty(o.shape, o.dtype), pltpu.HBM) for o in plan.out_shape]
        parts = [pltpu.with_memory_space_constraint(a, pltpu.HBM) for a in plan.args]
        res = pl.pallas_call(
            body, name=name,
            out_shape=[*[DMA(())] * (per * n),
                       *[pltpu.HBM(a.shape, a.dtype) for a in plan.args],
                       *[pltpu.HBM(o.shape, o.dtype) for o in plan.out_shape],
                       jax.ShapeDtypeStruct((8, 128), F32)],
            in_specs=[HBM] * (2 * n) + [pl.BlockSpec(memory_space=pl.ANY)] * na,
            out_specs=[SEM] * (per * n) + [HBM] * (2 * n) + [pl.BlockSpec(memory_space=pltpu.VMEM)],
            input_output_aliases={i: per * n + i for i in range(2 * n)},
            compiler_params=pltpu.CompilerParams(has_side_effects=DATAFLOW),
        )(*parts, *lands, *after)
        self.sems = list(res[:per * n])
        self.parts = list(res[per * n:per * n + n])
        self.lands = list(res[per * n + n:per * n + 2 * n])
        return res[-1]

    def wait(self, after, name):
        n, plan, per = self.n, self.plan, self.PER_ARRAY

        def body(*refs):
            p, land = refs[:n], refs[n:2 * n]
            sems_of = {a: refs[2 * n + per * a:2 * n + per * (a + 1)] for a in range(n)}
            plan.finish(p, land, self._tables(sems_of))

        res = pl.pallas_call(
            body, name=name,
            out_shape=[*[pltpu.HBM(a.shape, a.dtype) for a in self.parts], *[pltpu.HBM(a.shape, a.dtype) for a in self.lands]],
            in_specs=[HBM] * (2 * n) + [SEM] * (per * n) + [pl.BlockSpec(memory_space=pl.ANY)] * len(after),
            out_specs=[HBM] * (2 * n), input_output_aliases={i: i for i in range(2 * n)},
            compiler_params=pltpu.CompilerParams(has_side_effects=DATAFLOW),
        )(*self.parts, *self.lands, *self.sems, *after)
        return list(res[:n]), list(res[n:])


PASS_ON_BARRIER = 1


def _sibling_barrier():
    x, y, c, _ = _place()
    barrier = pltpu.get_barrier_semaphore()
    pl.semaphore_signal(barrier, inc=1, device_id=(x, y, 1 - c), device_id_type=MESH)
    pl.semaphore_wait(barrier, 1)


def _pass_on(lands, name):
    n = len(lands)

    def body(*refs):
        out = refs[n:2 * n]
        send_sems, recv_sems = refs[2 * n:]
        x, y, c, k = _place()
        _sibling_barrier()
        cps = []
        for a in range(n):
            for m in range(1, N_CHIPS):
                got = out[a].at[k ^ m, c]
                cp = _remote(got, got, send_sems.at[a, m - 1], recv_sems.at[a, m - 1], (x, y, 1 - c))
                cp.start()
                cps.append(cp)
        for a in range(n):
            for m in range(1, N_CHIPS):
                theirs = out[a].at[k ^ m, 1 - c]
                _remote(theirs, theirs, send_sems.at[a, m - 1], recv_sems.at[a, m - 1], (x, y, 1 - c)).wait_recv()
        for cp in cps:
            cp.wait_send()

    return pl.pallas_call(
        body, name=name, out_shape=[jax.ShapeDtypeStruct(a.shape, a.dtype) for a in lands],
        in_specs=[ANY] * n, out_specs=[ANY] * n, input_output_aliases={a: a for a in range(n)},
        scratch_shapes=[DMA((n, 3)), DMA((n, 3))],
        compiler_params=pltpu.CompilerParams(has_side_effects=True, collective_id=PASS_ON_BARRIER),
    )(*lands)


def _sibling_exchange(grads, small, name, after=None):
    n = len(grads)
    ns = 0 if small is None else 1
    na = 0 if after is None else 1

    def body(*refs):
        g = refs[:n]
        land = refs[n + ns + na:2 * n + ns + na]
        send_sems, recv_sems, own_sem, ssend_sems, srecv_sems = refs[2 * n + 2 * ns + na:]
        x, y, c, k = _place()
        me = 2 * k + c
        cps = []
        for a in range(n):
            cp = _remote(g[a].at[:, pl.ds(1 - c, 1)], land[a], send_sems.at[a], recv_sems.at[a], (x, y, 1 - c))
            cp.start()
            cps.append(cp)
        if small is not None:
            sm, smg = refs[n], refs[2 * n + 1 + na]
            peers = [me ^ m for m in range(1, N_DEV)]
            ids = [(p // 4, (p // 2) % 2, p % 2) for p in peers]
            own = pltpu.make_async_copy(sm, smg.at[me], own_sem)
            own.start()
            for m in range(1, N_DEV):
                cp = _remote(sm, smg.at[me], ssend_sems.at[m - 1], srecv_sems.at[m - 1], ids[m - 1])
                cp.start()
                cps.append(cp)
            for m in range(1, N_DEV):
                _remote(sm, smg.at[peers[m - 1]], ssend_sems.at[m - 1], srecv_sems.at[m - 1], ids[m - 1]).wait_recv()
            own.wait()
        for cp in cps[:n]:
            cp.wait_recv()
        for cp in cps:
            cp.wait_send()

    out_shape = [jax.ShapeDtypeStruct((N_CHIPS, 1) + a.shape[2:], a.dtype) for a in grads]
    ins = list(grads)
    if small is not None:
        out_shape.append(jax.ShapeDtypeStruct((N_DEV,) + small.shape, small.dtype))
        ins.append(small)
    if after is not None:
        ins.append(after)
    return pl.pallas_call(
        body, name=name, out_shape=out_shape, in_specs=[ANY] * (n + ns + na), out_specs=[ANY] * (n + ns),
        scratch_shapes=[DMA((max(n, 1),)), DMA((max(n, 1),)), DMA, DMA((N_DEV - 1,)), DMA((N_DEV - 1,))],
        compiler_params=pltpu.CompilerParams(has_side_effects=True),
    )(*ins)


def _sibling_share(halves, col_half, name, after=()):
    n, na = len(halves), len(after)

    def body(*refs):
        out = refs[n + na:2 * n + na]
        send_sems, recv_sems = refs[2 * n + na:]
        x, y, c, k = _place()

        def half(a, core):
            if not col_half[a]:
                return out[a].at[:, pl.ds(core, 1)]
            cols = out[a].shape[-1] // 2
            return out[a].at[:, :, pl.ds(pl.multiple_of(core * cols, cols), cols)]

        cps = []
        for a in range(n):
            cp = _remote(half(a, c), half(a, c), send_sems.at[a], recv_sems.at[a], (x, y, 1 - c))
            cp.start()
            cps.append(cp)
        for a in range(n):
            _remote(half(a, 1 - c), half(a, 1 - c), send_sems.at[a], recv_sems.at[a], (x, y, 1 - c)).wait_recv()
        for cp in cps:
            cp.wait_send()

    out_shape = [jax.ShapeDtypeStruct(a.shape, a.dtype) for a in halves]
    return pl.pallas_call(
        body, name=name, out_shape=out_shape, in_specs=[ANY] * (n + na), out_specs=[ANY] * n,
        input_output_aliases={a: a for a in range(n)}, scratch_shapes=[DMA((n,)), DMA((n,))],
        compiler_params=pltpu.CompilerParams(has_side_effects=True),
    )(*halves, *after)


def _add_sibling(g, land, core):
    _, _, r, c = g.shape
    rb = _row_block(r, c * (3 * 2 * 2 + 2 * 4))

    def body(core_ref, g_ref, l_ref, o_ref):
        o_ref[...] = (g_ref[...].astype(F32) + l_ref[...].astype(F32)).astype(o_ref.dtype)

    return pl.pallas_call(
        body, name="rs_add_sibling", out_shape=jax.ShapeDtypeStruct((N_CHIPS, r, c), g.dtype),
        grid_spec=pltpu.PrefetchScalarGridSpec(
            num_scalar_prefetch=1, grid=(N_CHIPS, r // rb),
            in_specs=[pl.BlockSpec((None, None, rb, c), lambda j, i, core_ref: (j, core_ref[0], i, 0)),
                      pl.BlockSpec((None, None, rb, c), lambda j, i, core_ref: (j, 0, i, 0))],
            out_specs=pl.BlockSpec((None, rb, c), lambda j, i, core_ref: (j, i, 0))),
        compiler_params=pltpu.CompilerParams(dimension_semantics=("parallel", "parallel"), vmem_limit_bytes=VMEM_LIMIT),
    )(core, g, land)


def _add_chips(part, land, where, layer=0, n_layers=1, into=None, col_half=False, rows=(0, 1)):
    _, r, c = part.shape
    sub, n_sub = rows
    rb = _row_block(r, c * (4 * 2 * 2 + 4 * 2 + 2 * 4))

    def body(where_ref, p_ref, l_ref, *rest):
        acc = p_ref[...].astype(F32)
        for m in range(N_CHIPS - 1):
            acc = acc + l_ref[m].astype(F32)
        rest[-1][...] = acc

    in_specs = [pl.BlockSpec((None, rb, c), lambda i, where_ref: (where_ref[0], i, 0)),
                pl.BlockSpec((N_CHIPS - 1, rb, c), lambda i, where_ref: (0, i, 0))]
    args = [where, part, land]
    if into is not None:
        in_specs.append(ANY)
        args.append(into)
    if col_half:
        out_shape = jax.ShapeDtypeStruct((n_layers, r, 2 * c), F32)
        out_spec = pl.BlockSpec((None, rb, c), lambda i, where_ref: (layer, i, where_ref[1]))
    else:
        out_shape = jax.ShapeDtypeStruct((n_layers, 2, n_sub * r, c), F32)
        out_spec = pl.BlockSpec((None, None, rb, c), lambda i, where_ref: (layer, where_ref[1], sub * (r // rb) + i, 0))
    return pl.pallas_call(
        body, name="rs_add_chips", out_shape=out_shape,
        grid_spec=pltpu.PrefetchScalarGridSpec(
            num_scalar_prefetch=1, grid=(r // rb,), in_specs=in_specs, out_specs=out_spec),
        input_output_aliases={} if into is None else {3: 0},
        compiler_params=pltpu.CompilerParams(dimension_semantics=("parallel",), vmem_limit_bytes=VMEM_LIMIT),
    )(*args)


def _sum_small(smg, own, me):
    def body(me_ref, s_ref, own_ref, o_ref):
        o_ref[...] = jnp.zeros_like(o_ref)
        for j in range(N_DEV):
            @pl.when(me_ref[0] == j)
            def _():
                o_ref[...] += own_ref[...]

            @pl.when(me_ref[0] != j)
            def _():
                o_ref[...] += s_ref[j]

    return pl.pallas_call(
        body, name="rs_sum_small", out_shape=jax.ShapeDtypeStruct(smg.shape[1:], F32),
        grid_spec=pltpu.PrefetchScalarGridSpec(
            num_scalar_prefetch=1, grid=(1,),
            in_specs=[pl.BlockSpec(smg.shape, lambda i, me_ref: (0, 0, 0)), pl.BlockSpec(own.shape, lambda i, me_ref: (0, 0))],
            out_specs=pl.BlockSpec(own.shape, lambda i, me_ref: (0, 0))),
    )(me, smg, own)


def _pool_windows(ext_ref, g, gw, tm, first_row):
    w = POOL_WINDOWS[g]
    slab = ext_ref[:, g * gw:(g + 1) * gw]
    p, k = slab, 1
    while k < w:
        p = p + pltpu.roll(p, k, 0)
        k *= 2
    t = first_row + lax.broadcasted_iota(jnp.int32, (tm, 1), 0)
    cnt = jnp.minimum(t + 1, w).astype(F32)
    return p[POOL_HALO:] / cnt - slab[POOL_HALO:]


def _fwd_pool(x, small, scale, poolw, comm=None):
    t, d = x.shape
    tm = _token_tile(t)
    gw = d // len(POOL_WINDOWS)

    def body(x_ref, sm_ref, sc_ref, w_ref, h_ref, ext_ref, mix_ref):
        i = pl.program_id(0)

        @pl.when(i == 0)
        def _():
            ext_ref[0:POOL_HALO, :] = jnp.zeros((POOL_HALO, d), F32)

        @pl.when(i > 0)
        def _():
            ext_ref[0:POOL_HALO, :] = ext_ref[tm:tm + POOL_HALO, :]

        xv = x_ref[...]
        xh, _ = _rms(xv)
        ext_ref[POOL_HALO:, :] = xh * sm_ref[0:1, :]
        for g in range(len(POOL_WINDOWS)):
            pooled = _pool_windows(ext_ref, g, gw, tm, i * tm)
            cols = slice(g * gw, (g + 1) * gw)
            mix_ref[:, cols] = _dot(pooled.astype(BF16), w_ref[g]) * sc_ref[:, cols]
        mh, _ = _rms(mix_ref[...])
        h_ref[...] = xv + mh * sm_ref[1:2, :]

    (h,), got = _hosted(
        body, comm, name="fwd_pool", grid=(t // tm,), out_shape=[jax.ShapeDtypeStruct((t, d), F32)],
        in_specs=[pl.BlockSpec((tm, d), lambda i: (i, 0)), _resident(small.shape, lambda i: (0, 0)),
                  _resident(scale.shape, lambda i: (0, 0)), _resident(poolw.shape, lambda i: (0, 0, 0))],
        out_specs=[pl.BlockSpec((tm, d), lambda i: (i, 0))],
        scratch_shapes=[pltpu.VMEM((POOL_HALO + tm, d), F32), pltpu.VMEM((tm, d), F32)],
        args=[x, small, scale, poolw])
    return h, got


def _fwd_ffn(h, small, wgu, wd, layer, comm=None, target=None):
    t, d = h.shape
    tm = _token_tile(t)
    steps = t // tm
    fc = wgu.shape[-1]
    f = 2 * fc
    g_in, g_out = 4 * layer + 2, 4 * layer + 3
    with_loss = target is not None

    def body(h_ref, *refs):
        if with_loss:
            t_ref, sm_ref, wgu_ref, wd_ref, o_ref, gu_ref, ff_ref, n_ref, l_ref, acc_ref = refs
        else:
            sm_ref, wgu_ref, wd_ref, o_ref, gu_ref, ff_ref, n_ref = refs
        hv = h_ref[...]
        hh, _ = _rms(hv)
        n = (hh * sm_ref[g_in:g_in + 1, :]).astype(BF16)
        n_ref[...] = n
        ff = None
        for j in range(2):
            gate = _dot(n, wgu_ref[j])
            up = _dot(n, wgu_ref[2 + j])
            gu_ref[:, j * fc:(j + 1) * fc] = gate.astype(BF16)
            gu_ref[:, f + j * fc:f + (j + 1) * fc] = up.astype(BF16)
            act = (gate * jax.nn.sigmoid(gate) * up).astype(BF16)
            part = _dot(act, wd_ref[j * fc:(j + 1) * fc, :])
            ff = part if ff is None else ff + part
        ff_ref[...] = ff
        fh, _ = _rms(ff)
        out = hv + fh * sm_ref[g_out:g_out + 1, :]
        if not with_loss:
            o_ref[...] = out
            return
        i = pl.program_id(0)
        e = out - t_ref[...]
        o_ref[...] = e * (1.0 / d)

        @pl.when(i == 0)
        def _():
            acc_ref[...] = jnp.zeros_like(acc_ref)

        acc_ref[...] += _colsum(e * e)

        @pl.when(i == steps - 1)
        def _():
            l_ref[...] = jnp.full(l_ref.shape, 0.5 / d, F32) * jnp.sum(acc_ref[...])

    row = lambda i: (i, 0)
    out_shape = [jax.ShapeDtypeStruct((t, d), F32), jax.ShapeDtypeStruct((t, 2 * f), BF16),
                 jax.ShapeDtypeStruct((t, d), F32), jax.ShapeDtypeStruct((t, d), BF16)]
    out_specs = [pl.BlockSpec((tm, d), row), pl.BlockSpec((tm, 2 * f), row), pl.BlockSpec((tm, d), row),
                 pl.BlockSpec((tm, d), row)]
    weight_specs = [_resident(small.shape, lambda i: (0, 0)), _resident(wgu.shape, lambda i: (0, 0, 0)),
                    _resident(wd.shape, lambda i: (0, 0))]
    if with_loss:
        return _hosted(
            body, comm, name=f"fwd_ffn{layer}_loss", grid=(steps,),
            out_shape=out_shape + [jax.ShapeDtypeStruct((8, 128), F32)],
            in_specs=[pl.BlockSpec((tm, d), row), pl.BlockSpec((tm, d), row)] + weight_specs,
            out_specs=out_specs + [pl.BlockSpec((8, 128), lambda i: (0, 0))],
            scratch_shapes=[pltpu.VMEM((1, d), F32)], args=[h, target, small, wgu, wd])
    return _hosted(
        body, comm, name=f"fwd_ffn{layer}", grid=(steps,), out_shape=out_shape,
        in_specs=[pl.BlockSpec((tm, d), row)] + weight_specs, out_specs=out_specs, args=[h, small, wgu, wd])


def _fwd_conv(h, small, win, wout, comm=None):
    t, d = h.shape
    tm = _token_tile(t)
    pc = win.shape[-1]

    def body(h_ref, sm_ref, win_ref, wout_ref, o_ref, proj_ref, y_ref, n_ref, pj_ref, uext_ref):
        i = pl.program_id(0)

        @pl.when(i == 0)
        def _():
            uext_ref[0:CONV_HALO, :] = jnp.zeros((CONV_HALO, d), F32)

        @pl.when(i > 0)
        def _():
            uext_ref[0:CONV_HALO, :] = uext_ref[tm:tm + CONV_HALO, :]

        hv = h_ref[...]
        hh, _ = _rms(hv)
        n = (hh * sm_ref[4:5, :]).astype(BF16)
        n_ref[...] = n
        for k in range(N_CHIPS):
            pj_ref[:, k * pc:(k + 1) * pc] = _dot(n, win_ref[k])
        proj_ref[...] = pj_ref[...].astype(BF16)
        uext_ref[CONV_HALO:, :] = pj_ref[:, d:2 * d] * pj_ref[:, 2 * d:]
        taps = [sm_ref[8 + j:9 + j, :] for j in range(3)]
        full = uext_ref[...]
        conv = (full[CONV_HALO:] * taps[2] + pltpu.roll(full, 1, 0)[CONV_HALO:] * taps[1]
                + pltpu.roll(full, 2, 0)[CONV_HALO:] * taps[0])
        y = _dot((pj_ref[:, 0:d] * conv).astype(BF16), wout_ref[...])
        y_ref[...] = y
        yh, _ = _rms(y)
        o_ref[...] = hv + yh * sm_ref[5:6, :]

    row = lambda i: (i, 0)
    return _hosted(
        body, comm, name="fwd_conv", grid=(t // tm,),
        out_shape=[jax.ShapeDtypeStruct((t, d), F32), jax.ShapeDtypeStruct((t, 3 * d), BF16),
                   jax.ShapeDtypeStruct((t, d), F32), jax.ShapeDtypeStruct((t, d), BF16)],
        in_specs=[pl.BlockSpec((tm, d), row), _resident(small.shape, lambda i: (0, 0)),
                  _resident(win.shape, lambda i: (0, 0, 0)), _resident(wout.shape, lambda i: (0, 0))],
        out_specs=[pl.BlockSpec((tm, d), row), pl.BlockSpec((tm, 3 * d), row), pl.BlockSpec((tm, d), row),
                   pl.BlockSpec((tm, d), row)],
        scratch_shapes=[pltpu.VMEM((tm, 3 * d), F32), pltpu.VMEM((CONV_HALO + tm, d), F32)],
        args=[h, small, win, wout])


def _bwd_ffn(dh, h, ff, gu, small, wgu, wd, layer, comm=None):
    t, d = h.shape
    tm = _token_tile(t, 256)
    fc = wgu.shape[-1]
    f = 2 * fc
    g_in, g_out = 4 * layer + 2, 4 * layer + 3

    def body(dh_ref, h_ref, ff_ref, gu_ref, sm_ref, wgu_ref, wd_ref, o_ref, dgu_ref, dff_ref, act_ref, sg_ref):
        i = pl.program_id(0)

        @pl.when(i == 0)
        def _():
            sg_ref[...] = jnp.zeros_like(sg_ref)

        dy = dh_ref[...]
        fh, r3 = _rms(ff_ref[...])
        sg_ref[1:2, :] += _colsum(dy * fh)
        dff = _rms_bwd(dy, fh, r3, sm_ref[g_out:g_out + 1, :]).astype(BF16)
        dff_ref[...] = dff
        for j in range(2):
            dact = _dot_nt(dff, wd_ref[j * fc:(j + 1) * fc, :])
            gate = gu_ref[:, j * fc:(j + 1) * fc].astype(F32)
            up = gu_ref[:, f + j * fc:f + (j + 1) * fc].astype(F32)
            sig = jax.nn.sigmoid(gate)
            silu = gate * sig
            act_ref[:, j * fc:(j + 1) * fc] = (silu * up).astype(BF16)
            dgu_ref[:, j * fc:(j + 1) * fc] = (dact * up * (sig * (1.0 + gate * (1.0 - sig)))).astype(BF16)
            dgu_ref[:, f + j * fc:f + (j + 1) * fc] = (dact * silu).astype(BF16)
        dn = None
        for k in range(N_CHIPS):
            part = _dot_nt(dgu_ref[:, k * fc:(k + 1) * fc], wgu_ref[k])
            dn = part if dn is None else dn + part
        hh, r2 = _rms(h_ref[...])
        sg_ref[0:1, :] += _colsum(dn * hh)
        o_ref[...] = dy + _rms_bwd(dn, hh, r2, sm_ref[g_in:g_in + 1, :])

    row = lambda i: (i, 0)
    return _hosted(
        body, comm, name=f"bwd_ffn{layer}", grid=(t // tm,),
        out_shape=[jax.ShapeDtypeStruct((t, d), F32), jax.ShapeDtypeStruct((t, 2 * f), BF16),
                   jax.ShapeDtypeStruct((t, d), BF16), jax.ShapeDtypeStruct((t, f), BF16),
                   jax.ShapeDtypeStruct((8, d), F32)],
        in_specs=[pl.BlockSpec((tm, d), row), pl.BlockSpec((tm, d), row), pl.BlockSpec((tm, d), row),
                  pl.BlockSpec((tm, 2 * f), row), _resident(small.shape, lambda i: (0, 0)),
                  _resident(wgu.shape, lambda i: (0, 0, 0)), _resident(wd.shape, lambda i: (0, 0))],
        out_specs=[pl.BlockSpec((tm, d), row), pl.BlockSpec((tm, 2 * f), row), pl.BlockSpec((tm, d), row),
                   pl.BlockSpec((tm, f), row), pl.BlockSpec((8, d), lambda i: (0, 0))],
        args=[dh, h, ff, gu, small, wgu, wd])


def _bwd_conv(dh, h, y, proj, small, win, wout, comm=None):
    t, d = h.shape
    tm = _token_tile(t)
    steps = t // tm
    pc = win.shape[-1]
    halo_blocks = tm // 16

    def body(dh_ref, h_ref, y_ref, proj_ref, halo_ref, sm_ref, win_ref, wout_ref,
             o_ref, dproj_ref, dy_ref, bc_ref, sg_ref, uext_ref, dcext_ref, carry_ref):
        i = pl.program_id(0)
        tile = steps - 1 - i

        @pl.when(i == 0)
        def _():
            sg_ref[...] = jnp.zeros_like(sg_ref)
            carry_ref[...] = jnp.zeros_like(carry_ref)

        dy = dh_ref[...]
        yh, r1 = _rms(y_ref[...])
        sg_ref[1:2, :] += _colsum(dy * yh)
        dyv = _rms_bwd(dy, yh, r1, sm_ref[5:6, :]).astype(BF16)
        dy_ref[...] = dyv
        dbc = _dot_nt(dyv, wout_ref[...])
        b = proj_ref[:, 0:d].astype(F32)
        cg = proj_ref[:, d:2 * d].astype(F32)
        v = proj_ref[:, 2 * d:].astype(F32)
        halo = halo_ref[...].astype(F32)[16 - CONV_HALO:]
        uh = halo[:, d:2 * d] * halo[:, 2 * d:]
        uext_ref[0:CONV_HALO, :] = jnp.where(tile > 0, uh, jnp.zeros_like(uh))
        uext_ref[CONV_HALO:, :] = cg * v
        taps = [sm_ref[8 + j:9 + j, :] for j in range(3)]
        full = uext_ref[...]
        u0 = full[CONV_HALO:]
        u1 = pltpu.roll(full, 1, 0)[CONV_HALO:]
        u2 = pltpu.roll(full, 2, 0)[CONV_HALO:]
        conv = u0 * taps[2] + u1 * taps[1] + u2 * taps[0]
        bc_ref[...] = (b * conv).astype(BF16)
        dconv = dbc * b
        sg_ref[4:5, :] += _colsum(dconv * u0)
        sg_ref[3:4, :] += _colsum(dconv * u1)
        sg_ref[2:3, :] += _colsum(dconv * u2)
        dcext_ref[0:tm, :] = dconv
        dcext_ref[tm:, :] = carry_ref[...]
        carry_ref[...] = dconv[0:CONV_HALO]
        dfull = dcext_ref[...]
        n8 = tm + CONV_HALO
        du = (dfull[0:tm] * taps[2] + pltpu.roll(dfull, n8 - 1, 0)[0:tm] * taps[1]
              + pltpu.roll(dfull, n8 - 2, 0)[0:tm] * taps[0])
        dproj_ref[:, 0:d] = (dbc * conv).astype(BF16)
        dproj_ref[:, d:2 * d] = (du * v).astype(BF16)
        dproj_ref[:, 2 * d:] = (du * cg).astype(BF16)
        dn = None
        for k in range(N_CHIPS):
            part = _dot_nt(dproj_ref[:, k * pc:(k + 1) * pc], win_ref[k])
            dn = part if dn is None else dn + part
        hh, r0 = _rms(h_ref[...])
        sg_ref[0:1, :] += _colsum(dn * hh)
        o_ref[...] = dy + _rms_bwd(dn, hh, r0, sm_ref[4:5, :])

    rev = lambda i: (steps - 1 - i, 0)
    before = lambda i: (jnp.maximum((steps - 1 - i) * halo_blocks - 1, 0), 0)
    return _hosted(
        body, comm, name="bwd_conv", grid=(steps,),
        out_shape=[jax.ShapeDtypeStruct((t, d), F32), jax.ShapeDtypeStruct((t, 3 * d), BF16),
                   jax.ShapeDtypeStruct((t, d), BF16), jax.ShapeDtypeStruct((t, d), BF16),
                   jax.ShapeDtypeStruct((8, d), F32)],
        in_specs=[pl.BlockSpec((tm, d), rev), pl.BlockSpec((tm, d), rev), pl.BlockSpec((tm, d), rev),
                  pl.BlockSpec((tm, 3 * d), rev), pl.BlockSpec((16, 3 * d), before),
                  _resident(small.shape, lambda i: (0, 0)), _resident(win.shape, lambda i: (0, 0, 0)),
                  _resident(wout.shape, lambda i: (0, 0))],
        out_specs=[pl.BlockSpec((tm, d), rev), pl.BlockSpec((tm, 3 * d), rev), pl.BlockSpec((tm, d), rev),
                   pl.BlockSpec((tm, d), rev), pl.BlockSpec((8, d), lambda i: (0, 0))],
        scratch_shapes=[pltpu.VMEM((CONV_HALO + tm, d), F32), pltpu.VMEM((tm + CONV_HALO, d), F32),
                        pltpu.VMEM((CONV_HALO, d), F32)],
        args=[dh, h, y, proj, proj, small, win, wout])


def _bwd_pool(dh, x, small, scale, poolw, comm=None):
    t, d = x.shape
    tm = _token_tile(t)
    steps = t // tm
    ng = len(POOL_WINDOWS)
    gw = d // ng
    halo_blocks = tm // POOL_HALO

    def body(dh_ref, x_ref, halo_ref, sm_ref, sc_ref, w_ref, o_ref, dw_ref, sg_ref,
             ext_ref, mix_ref, mm_ref, pb_ref, qext_ref, dhn_ref, carry_ref):
        i = pl.program_id(0)
        tile = steps - 1 - i

        @pl.when(i == 0)
        def _():
            sg_ref[...] = jnp.zeros_like(sg_ref)
            dw_ref[...] = jnp.zeros_like(dw_ref)
            carry_ref[...] = jnp.zeros_like(carry_ref)

        g0 = sm_ref[0:1, :]
        xv = x_ref[...]
        xh, r0 = _rms(xv)
        hx, _ = _rms(halo_ref[...])
        ext_ref[0:POOL_HALO, :] = jnp.where(tile > 0, hx * g0, jnp.zeros_like(hx))
        ext_ref[POOL_HALO:, :] = xh * g0
        for g in range(ng):
            pooled = _pool_windows(ext_ref, g, gw, tm, tile * tm)
            cols = slice(g * gw, (g + 1) * gw)
            pb = pooled.astype(BF16)
            pb_ref[:, cols] = pb
            mm = _dot(pb, w_ref[g])
            mm_ref[:, cols] = mm
            mix_ref[:, cols] = mm * sc_ref[:, cols]
        dy = dh_ref[...]
        mh, r1 = _rms(mix_ref[...])
        sg_ref[1:2, :] += _colsum(dy * mh)
        dmix = _rms_bwd(dy, mh, r1, sm_ref[1:2, :])
        sg_ref[2:3, :] += _colsum(dmix * mm_ref[...])
        mix_ref[...] = dmix * sc_ref[...]
        n16 = tm + POOL_HALO
        for g in range(ng):
            w = POOL_WINDOWS[g]
            cols = slice(g * gw, (g + 1) * gw)
            dmm = mix_ref[:, cols].astype(BF16)
            dpooled = _dot_nt(dmm, w_ref[g])
            dw_ref[g] += _dot_tn(pb_ref[:, cols], dmm)
            trow = tile * tm + lax.broadcasted_iota(jnp.int32, (tm, 1), 0)
            q = dpooled / jnp.minimum(trow + 1, w).astype(F32)
            qext_ref[0:tm, cols] = q
            qext_ref[tm:, cols] = carry_ref[:, cols]
            carry_ref[:, cols] = q[0:POOL_HALO]
            p, k = qext_ref[:, cols], 1
            while k < w:
                p = p + pltpu.roll(p, n16 - k, 0)
                k *= 2
            dhn_ref[:, cols] = p[0:tm] - dpooled
        dhn = dhn_ref[...]
        sg_ref[0:1, :] += _colsum(dhn * xh)
        o_ref[...] = dy + _rms_bwd(dhn, xh, r0, g0)

    rev = lambda i: (steps - 1 - i, 0)
    before = lambda i: (jnp.maximum((steps - 1 - i) * halo_blocks - 1, 0), 0)
    return _hosted(
        body, comm, name="bwd_pool", grid=(steps,),
        out_shape=[jax.ShapeDtypeStruct((t, d), F32), jax.ShapeDtypeStruct((ng, gw, gw), F32),
                   jax.ShapeDtypeStruct((8, d), F32)],
        in_specs=[pl.BlockSpec((tm, d), rev), pl.BlockSpec((tm, d), rev), pl.BlockSpec((POOL_HALO, d), before),
                  _resident(small.shape, lambda i: (0, 0)), _resident(scale.shape, lambda i: (0, 0)),
                  _resident(poolw.shape, lambda i: (0, 0, 0))],
        out_specs=[pl.BlockSpec((tm, d), rev), pl.BlockSpec((ng, gw, gw), lambda i: (0, 0, 0)),
                   pl.BlockSpec((8, d), lambda i: (0, 0))],
        scratch_shapes=[pltpu.VMEM((POOL_HALO + tm, d), F32), pltpu.VMEM((tm, d), F32), pltpu.VMEM((tm, d), F32),
                        pltpu.VMEM((tm, d), BF16), pltpu.VMEM((tm + POOL_HALO, d), F32), pltpu.VMEM((tm, d), F32),
                        pltpu.VMEM((POOL_HALO, d), F32)],
        args=[dh, x, x, small, scale, poolw])


def _weight_grad(a, b, bm, bn, half_on, name, comm=None, rows=(0, 1)):
    t, m = a.shape
    _, n = b.shape
    if half_on == "a":
        a_cols, b_cols = 2 * bm, bn
    else:
        a_cols, b_cols = bm, 2 * bn
    steps = max(m // a_cols, n // b_cols)
    sub, n_sub = rows
    tr = bm // n_sub

    def spec(cols, total):
        if cols == total:
            return _resident((t, cols), lambda p, j: (0, 0))
        return pl.BlockSpec((t, cols), lambda p, j: (0, j))

    def tile(a_ref, b_ref, half):
        if half_on == "a":
            first = half * bm + sub * tr
            return _dot_tn(a_ref[:, first:first + tr], b_ref[...])
        return _dot_tn(a_ref[...], b_ref[:, half * bn:(half + 1) * bn])

    def body(a_ref, b_ref, parts_ref, land_ref, acc_ref, stage_ref, got_ref, send_sems, recv_sems, got_sem):
        p, j = pl.program_id(0), pl.program_id(1)
        x, y, c, _ = _place()
        half = jnp.where(p == 0, 1 - c, c)

        def send(jj):
            return _remote(stage_ref.at[jj % 2], land_ref.at[jj], send_sems.at[jj], recv_sems.at[jj], (x, y, 1 - c))

        def fetch():
            return pltpu.make_async_copy(land_ref.at[j], got_ref, got_sem)

        @pl.when(p == 1)
        def _():
            @pl.when(j == 0)
            def _():
                for jj in range(max(steps - 2, 0), steps):
                    send(jj).wait_send()

            send(j).wait_recv()
            fetch().start()

        for hv in range(2):
            @pl.when(half == hv)
            def _():
                acc_ref[...] = tile(a_ref, b_ref, hv)

        @pl.when(p == 0)
        def _():
            @pl.when(j >= 2)
            def _():
                send(j - 2).wait_send()

            stage_ref[j % 2] = acc_ref[...].astype(BF16)
            send(j).start()

        @pl.when(p == 1)
        def _():
            fetch().wait()
            parts_ref[...] = (acc_ref[...] + got_ref[...].astype(F32)).astype(BF16)

    (parts, _), got = _hosted(
        body, comm, name=name, grid=(2, steps),
        out_shape=[jax.ShapeDtypeStruct((steps, tr, bn), BF16), jax.ShapeDtypeStruct((steps, tr, bn), BF16)],
        in_specs=[spec(a_cols, m), spec(b_cols, n)],
        out_specs=[pl.BlockSpec((None, tr, bn), lambda p, j: (p * j, 0, 0)), ANY],
        scratch_shapes=[pltpu.VMEM((tr, bn), F32), pltpu.VMEM((2, tr, bn), BF16), pltpu.VMEM((tr, bn), BF16),
                        DMA((steps,)), DMA((steps,)), DMA],
        args=[a, b])
    return parts, got


def _cast_layer(w, layer, name):
    _, r, c = w.shape
    rb = _row_block(r, c * (4 + 2) * 2)

    def body(w_ref, o_ref):
        o_ref[...] = w_ref[...].astype(BF16)

    return pl.pallas_call(
        body, name=name, grid=(r // rb,), out_shape=jax.ShapeDtypeStruct((r, c), BF16),
        in_specs=[pl.BlockSpec((None, rb, c), lambda i: (layer, i, 0))], out_specs=pl.BlockSpec((rb, c), lambda i: (i, 0)),
        compiler_params=pltpu.CompilerParams(dimension_semantics=("parallel",), vmem_limit_bytes=VMEM_LIMIT),
    )(w)


def _adamw_math(w, g, m, v):
    bc1 = 1.0 - ADAM_B1 ** ADAM_STEP
    bc2 = 1.0 - ADAM_B2 ** ADAM_STEP
    nm = ADAM_B1 * m + (1.0 - ADAM_B1) * g
    nv = ADAM_B2 * v + (1.0 - ADAM_B2) * (g * g)
    return -ADAM_LR * ((nm / bc1) / (jnp.sqrt(nv / bc2) + ADAM_EPS) + ADAM_WD * w), nm, nv


def _adamw_small(small_sum, where, gains, taps, scale):
    dq = gains[0].shape[-1]
    d = small_sum.shape[-1]

    def body(where_ref, mine_ref, all_ref, gw, gm, gv, tw, tm_, tv, sw, sm, sv,
             gg, gd, gnm, gnv, tg, td, tnm, tnv, sg, sd, snm, snv):
        for layer in range(gw.shape[0]):
            g = mine_ref[4 * layer:4 * layer + 4, :]
            gg[layer] = g
            gd[layer], gnm[layer], gnv[layer] = _adamw_math(gw[layer], g, gm[layer], gv[layer])
        g = mine_ref[8:8 + tw.shape[1], :]
        tg[0] = g
        td[0], tnm[0], tnv[0] = _adamw_math(tw[0], g, tm_[0], tv[0])
        g = all_ref[11:12, :]
        sg[...] = g
        sd[...], snm[...], snv[...] = _adamw_math(sw[...], g, sm[...], sv[...])

    full = lambda a: pl.BlockSpec(a.shape, lambda i, where_ref: (0,) * a.ndim)
    params = [*gains, *taps, *scale]
    outs = [gains[0]] * 4 + [taps[0]] * 4 + [scale[0]] * 4
    res = pl.pallas_call(
        body, name="adamw_small", out_shape=[jax.ShapeDtypeStruct(a.shape, F32) for a in outs],
        grid_spec=pltpu.PrefetchScalarGridSpec(
            num_scalar_prefetch=1, grid=(1,),
            in_specs=[pl.BlockSpec((16, dq), lambda i, where_ref: (0, where_ref[0])), pl.BlockSpec((16, d), lambda i, where_ref: (0, 0)),
                      *[full(a) for a in params]],
            out_specs=[full(a) for a in outs]),
    )(where, small_sum, small_sum, *params)
    return tuple(res[0:4]), tuple(res[4:8]), tuple(res[8:12])


def _adamw(w, g, m, v, name):
    r, c = w.shape
    rb = _row_block(r, c * (8 * 4 * 2 + 4 * 4))

    def body(w_ref, g_ref, m_ref, v_ref, d_ref, nm_ref, nv_ref, go_ref):
        gv = g_ref[...]
        go_ref[...] = gv
        d_ref[...], nm_ref[...], nv_ref[...] = _adamw_math(w_ref[...], gv, m_ref[...], v_ref[...])

    spec = pl.BlockSpec((rb, c), lambda i: (i, 0))
    return pl.pallas_call(
        body, name=name, grid=(r // rb,), out_shape=[jax.ShapeDtypeStruct((r, c), F32)] * 4,
        in_specs=[spec] * 4, out_specs=[spec] * 4,
        compiler_params=pltpu.CompilerParams(dimension_semantics=("parallel",), vmem_limit_bytes=VMEM_LIMIT),
    )(w, g, m, v)


def kernel(x, norm_gains, pool_w, pool_scale, conv_in_w, conv_w, conv_out_w, ffn_gate_up_w, ffn_down_w, loss_target, m_norm_gains, m_pool_w, m_pool_scale, m_conv_in_w, m_conv_w, m_conv_out_w, m_ffn_gate_up_w, m_ffn_down_w, v_norm_gains, v_pool_w, v_pool_scale, v_conv_in_w, v_conv_w, v_conv_out_w, v_ffn_gate_up_w, v_ffn_down_w):
    _, t, d = x.shape
    dq = d // N_CHIPS
    ng = len(POOL_WINDOWS)
    gw = d // ng
    fq = ffn_down_w.shape[1]
    f = N_CHIPS * fq
    fc = f // 2
    core = lax.axis_index("c")
    chip = 2 * lax.axis_index("x") + lax.axis_index("y")
    core_arr = jnp.reshape(core, (1,)).astype(jnp.int32)
    where_arr = jnp.stack([chip, core]).astype(jnp.int32)
    x2, target = x[0], loss_target[0]

    small_loc = jnp.concatenate(
        [norm_gains.reshape(8, dq), conv_w[0], jnp.zeros((5, dq), F32)], axis=0).reshape(1, 2, 8, dq)
    pool_loc = pool_w.astype(BF16).reshape(1, 2, ng // 2 * (gw // N_CHIPS), gw)
    wgu_loc = [_cast_layer(ffn_gate_up_w, 0, "cast_gate_up0").reshape(1, 2, d // 2, fc),
               ffn_gate_up_w[1:2].astype(BF16).reshape(1, 2, d // 2, fc)]
    wd_loc = [_cast_layer(ffn_down_w, 0, "cast_down0").reshape(1, 2, fq // 2, d),
              ffn_down_w[1:2].astype(BF16).reshape(1, 2, fq // 2, d)]
    win_loc = conv_in_w.astype(BF16).reshape(1, 2, d // 2, -1)
    wout_loc = conv_out_w.astype(BF16).reshape(1, 2, dq // 2, d)

    def ffn_weights(wgu_f, wd_f):
        return wgu_f.reshape(N_CHIPS, d, fc), wd_f.reshape(f, d)

    ag0 = _SplitGather([(pool_loc, 0), (small_loc, 0), (wgu_loc[0], 0), (wd_loc[0], 0)])
    ag0.start("ag_start_layer0")
    ag1 = _SplitGather([(win_loc, 0), (wout_loc, 0), (wgu_loc[1], 0), (wd_loc[1], 0)])
    ag1.start("ag_start_layer1", [ag0.token])
    pool_f, small_f = _pass_on(ag0.wait([0, 1], ag1.token, "ag_wait_first"), "ag_pass_first")
    poolw = pool_f.reshape(N_CHIPS, ng, gw // N_CHIPS, gw).transpose(1, 0, 2, 3).reshape(ng, gw, gw)
    small = small_f.transpose(1, 2, 0, 3).reshape(16, d)
    h1, _ = _fwd_pool(x2, small, pool_scale, poolw)
    wgu0, wd0 = ffn_weights(*_pass_on(ag0.wait([2, 3], h1, "ag_wait_ffn0"), "ag_pass_ffn0"))
    (h2, gu0, ff0, n0), _ = _fwd_ffn(h1, small, wgu0, wd0, 0)
    win_f, wout_f = _pass_on(ag1.wait([0, 1], h2, "ag_wait_conv"), "ag_pass_conv")
    win_f, wout_f = win_f.reshape(N_CHIPS, d, -1), wout_f.reshape(d, d)
    (h3, proj, y, nc), _ = _fwd_conv(h2, small, win_f, wout_f)
    wgu1, wd1 = ffn_weights(*_pass_on(ag1.wait([2, 3], h3, "ag_wait_ffn1"), "ag_pass_ffn1"))
    (dh4, gu1, ff1, n1, loss_blk), _ = _fwd_ffn(h3, small, wgu1, wd1, 1, target=target)

    (dh3, dgu1, dff1, act1, sg_f1), _ = _bwd_ffn(dh4, h3, ff1, gu1, small, wgu1, wd1, 1)
    parts_d1, _ = _weight_grad(act1, dff1, fc, d // 2, "b", "dw_down1")
    parts_gu1, _ = _weight_grad(n1, dgu1, d // 2, fc, "a", "dw_gate_up1")
    ex_ffn1 = _SplitExchange([parts_d1.reshape(N_CHIPS, fq, d // 2), parts_gu1])
    started = ex_ffn1.start("rs_start_ffn1")
    (dh2, dproj, dyv, bcv, sg_c), _ = _bwd_conv(dh3, h2, y, proj, small, win_f, wout_f, _After(started))
    parts_in, _ = _weight_grad(nc, dproj, d // 2, 3 * d // N_CHIPS, "a", "dw_conv_in")
    parts_out, _ = _weight_grad(bcv, dyv, dq // 2, d, "a", "dw_conv_out")
    ex_conv = _SplitExchange([parts_in, parts_out])
    started = ex_conv.start("rs_start_conv")
    (dh1, dgu0, dff0, act0, sg_f0), _ = _bwd_ffn(dh2, h1, ff0, gu0, small, wgu0, wd0, 0, _After(started))
    parts_d0, _ = _weight_grad(act0, dff0, fc, d // 2, "b", "dw_down0")
    parts_gu0, _ = _weight_grad(n0, dgu0, d // 2, fc, "a", "dw_gate_up0")
    ex_ffn0 = _SplitExchange([parts_d0.reshape(N_CHIPS, fq, d // 2), parts_gu0])
    started = ex_ffn0.start("rs_start_ffn0")
    (grad_x, dpool, sg_p), _ = _bwd_pool(dh1, x2, small, pool_scale, poolw, _After(started))
    g_pool = dpool.astype(BF16).reshape(2, ng // 2, N_CHIPS, gw // N_CHIPS, gw).transpose(2, 0, 1, 3, 4).reshape(
        N_CHIPS, 2, ng // 2 * (gw // N_CHIPS), gw)
    small_g = jnp.concatenate(
        [sg_p[0:2], sg_f0[0:2], sg_c[0:2], sg_f1[0:2], sg_c[2:5], sg_p[2:3],
         jnp.broadcast_to(loss_blk[0:1, 0:1], (1, d)), jnp.zeros((3, d), F32)], axis=0)
    (land_p,) = _sibling_exchange([g_pool], None, "rs_sibling_pool")
    ex_pool = _SplitExchange([_add_sibling(g_pool, land_p, core_arr)])
    started = ex_pool.start("rs_start_pool")

    def update(w, g, m, v, name):
        flat = (-1, w.shape[-1])
        dl, m2, v2, g2 = _adamw(w.reshape(flat), g.reshape(flat), m.reshape(flat), v.reshape(flat), "adamw_" + name)
        return tuple(o.reshape(w.shape) for o in (g2, dl, m2, v2))

    (parts_d1, parts_gu1), (recv_d1, recv_gu1) = ex_ffn1.wait([started], "rs_wait_ffn1")
    (parts_in, parts_out), (recv_in, recv_out) = ex_conv.wait([started], "rs_wait_conv")
    gs_gu = _add_chips(parts_gu1, recv_gu1, where_arr, 1, 2)
    gs_d = _add_chips(parts_d1, recv_d1, where_arr, 1, 2, col_half=True)
    gs_in = _add_chips(parts_in, recv_in, where_arr)
    gs_out = _add_chips(parts_out, recv_out, where_arr)
    full_in, full_out = _sibling_share([gs_in, gs_out], [False, False], "rs_share_conv")
    up_in = update(conv_in_w, full_in.reshape(1, d, -1), m_conv_in_w, v_conv_in_w, "conv_in")
    up_out = update(conv_out_w, full_out.reshape(1, dq, d), m_conv_out_w, v_conv_out_w, "conv_out")
    done_first = [up_in[1], up_out[1], gs_gu, gs_d]
    (parts_d0, parts_gu0), (recv_d0, recv_gu0) = ex_ffn0.wait(done_first, "rs_wait_ffn0")
    (parts_p,), (recv_p,) = ex_pool.wait(done_first, "rs_wait_pool")
    ex_small = _SplitExchange([small_g], _SmallGather(small_g))
    ex_small.start("rs_start_small", [recv_gu0])
    gs_gu = _add_chips(parts_gu0, recv_gu0, where_arr, 0, 2, gs_gu)
    gs_d = _add_chips(parts_d0, recv_d0, where_arr, 0, 2, gs_d, col_half=True)
    gs_pool = _add_chips(parts_p, recv_p, where_arr)
    full_gu, full_d, full_pool = _sibling_share([gs_gu, gs_d, gs_pool], [False, True, False], "rs_share_ffn")
    up_gu = update(ffn_gate_up_w, full_gu.reshape(2, d, fc), m_ffn_gate_up_w, v_ffn_gate_up_w, "gate_up")
    up_d = update(ffn_down_w, full_d.reshape(2, fq, d), m_ffn_down_w, v_ffn_down_w, "down")
    (small_own,), (small_all,) = ex_small.wait([up_gu[1], up_d[1]], "rs_wait_small")
    small_sum = _sum_small(small_all, small_own, (2 * chip + core).reshape(1).astype(jnp.int32))
    loss = small_sum[12, 0]
    up_gains, up_taps, up_scale = _adamw_small(
        small_sum, where_arr, (norm_gains, m_norm_gains, v_norm_gains), (conv_w, m_conv_w, v_conv_w),
        (pool_scale, m_pool_scale, v_pool_scale))

    ups = [
        up_gains,
        update(pool_w, full_pool.reshape(1, ng, gw // N_CHIPS, gw), m_pool_w, v_pool_w, "pool_w"),
        up_scale,
        up_in,
        up_taps,
        up_out,
        up_gu,
        up_d,
    ]
    grads_out, deltas, new_ms, new_vs = (list(col) for col in zip(*ups))
    return (loss, grad_x[None], *grads_out, *deltas, *new_ms, *new_vs)
```

```python
import jax
import jax.numpy as jnp
from jax import lax
from jax.experimental import pallas as pl
from jax.experimental.pallas import tpu as pltpu

RMS_EPS = 1e-6
POOL_WINDOWS = (2, 4, 8, 16)
POOL_HALO = 16
CONV_HALO = 8
N_CHIPS = 4
N_DEV = 8
ADAM_LR = 0.001
ADAM_B1 = 0.9
ADAM_B2 = 0.999
ADAM_EPS = 1e-08
ADAM_WD = 0.01
ADAM_STEP = 10
VMEM_LIMIT = 56 * 2**20
STREAM_BUDGET = 24 * 2**20
MESH = pl.DeviceIdType.MESH
ANY = pl.BlockSpec(memory_space=pl.ANY)
DMA = pltpu.SemaphoreType.DMA
BF16 = jnp.bfloat16
F32 = jnp.float32


def _token_tile(t, rows=512):
    return min(rows, t)


def _rms(x):
    r = lax.rsqrt(jnp.mean(x * x, axis=-1, keepdims=True) + RMS_EPS)
    return x * r, r


def _rms_bwd(dy, xh, r, g):
    a = dy * g
    return r * (a - xh * jnp.mean(a * xh, axis=-1, keepdims=True))


def _dot(a, b):
    return jnp.dot(a, b, preferred_element_type=F32)


def _dot_nt(a, b):
    return lax.dot_general(a, b, (((1,), (1,)), ((), ())), preferred_element_type=F32)


def _dot_tn(a, b):
    return lax.dot_general(a, b, (((0,), (0,)), ((), ())), preferred_element_type=F32)


def _colsum(a):
    return jnp.sum(a, axis=0, keepdims=True)


def _resident(block, index_map):
    return pl.BlockSpec(block, index_map, pipeline_mode=pl.Buffered(1))


def _row_block(r, row_bytes):
    best = None
    for rb in range(16, r + 1, 16):
        if r % rb == 0 and rb * row_bytes <= STREAM_BUDGET:
            best = rb
    return best if best is not None else r


def _place():
    x, y, c = lax.axis_index("x"), lax.axis_index("y"), lax.axis_index("c")
    return x, y, c, 2 * x + y


def _dev(chip, core):
    return (chip // 2, chip % 2, core)


def _remote(src, dst, send_sem, recv_sem, device):
    return pltpu.make_async_remote_copy(src_ref=src, dst_ref=dst, send_sem=send_sem, recv_sem=recv_sem,
                                        device_id=device, device_id_type=MESH)


class _Gather:
    def __init__(self, shards):
        n = len(shards)
        self.args = [s for s, _ in shards]
        self.layers = [l for _, l in shards]
        self.out_shape = [jax.ShapeDtypeStruct((N_CHIPS,) + s.shape[1:], s.dtype) for s in self.args]
        self.sems = [DMA((n,)), DMA((n,)), DMA((n, 3)), DMA((n, 3)), DMA((n, 3)), DMA((n, 3))]

    def _own(self, loc, out, sems, a):
        x, y, c, k = _place()
        return _remote(loc[a].at[self.layers[a]], out[a].at[k], sems[0].at[a], sems[1].at[a], (x, y, 1 - c))

    def _ici(self, loc, out, sems, a, m, arrival):
        x, y, c, k = _place()
        dst = out[a].at[k ^ m, c] if arrival else out[a].at[k, c]
        return _remote(loc[a].at[self.layers[a], c], dst, sems[2].at[a, m - 1], sems[3].at[a, m - 1], _dev(k ^ m, c))

    def _forward(self, out, sems, a, m, arrival):
        x, y, c, k = _place()
        got = out[a].at[k ^ m, 1 - c] if arrival else out[a].at[k ^ m, c]
        return _remote(got, got, sems[4].at[a, m - 1], sems[5].at[a, m - 1], (x, y, 1 - c))

    def start(self, loc, out, sems):
        for a in range(len(self.args)):
            for m in range(1, N_CHIPS):
                self._ici(loc, out, sems, a, m, False).start()
            self._own(loc, out, sems, a).start()

    def finish(self, loc, out, sems):
        n = len(self.args)
        for a in range(n):
            for m in range(1, N_CHIPS):
                self._ici(loc, out, sems, a, m, True).wait_recv()
                self._forward(out, sems, a, m, False).start()
        for a in range(n):
            for m in range(1, N_CHIPS):
                self._forward(out, sems, a, m, True).wait_recv()
            self._own(loc, out, sems, a).wait_recv()
        for a in range(n):
            for m in range(1, N_CHIPS):
                self._ici(loc, out, sems, a, m, False).wait_send()
                self._forward(out, sems, a, m, False).wait_send()
            self._own(loc, out, sems, a).wait_send()


class _SmallGather:
    PEERS = N_DEV - 1

    def __init__(self, small):
        self.args = [small]
        self.out_shape = [jax.ShapeDtypeStruct((N_DEV,) + small.shape, small.dtype)]

    def _copy(self, sm, land, sems, m, arrival):
        x, y, c, k = _place()
        me = 2 * k + c
        peer = me ^ m
        return _remote(sm[0], land[0].at[peer if arrival else me], sems[0].at[0, m - 1], sems[1].at[0, m - 1],
                       (peer // 4, (peer // 2) % 2, peer % 2))

    def start(self, sm, land, sems):
        for m in range(1, N_DEV):
            self._copy(sm, land, sems, m, False).start()

    def finish(self, sm, land, sems):
        for m in range(1, N_DEV):
            self._copy(sm, land, sems, m, True).wait_recv()
        for m in range(1, N_DEV):
            self._copy(sm, land, sems, m, False).wait_send()


class _ChipExchange:
    PEERS = N_CHIPS - 1

    def __init__(self, parts):
        n = len(parts)
        self.args = list(parts)
        self.out_shape = [jax.ShapeDtypeStruct((N_CHIPS - 1,) + p.shape[1:], p.dtype) for p in parts]
        self.sems = [DMA((n, 3)), DMA((n, 3))]

    def _copy(self, p, land, sems, a, m):
        x, y, c, k = _place()
        return _remote(p[a].at[k ^ m], land[a].at[m - 1], sems[0].at[a, m - 1], sems[1].at[a, m - 1], _dev(k ^ m, c))

    def start(self, p, land, sems):
        for a in range(len(self.args)):
            for m in range(1, N_CHIPS):
                self._copy(p, land, sems, a, m).start()

    def finish(self, p, land, sems):
        for a in range(len(self.args)):
            for m in range(1, N_CHIPS):
                self._copy(p, land, sems, a, m).wait_recv()
        for a in range(len(self.args)):
            for m in range(1, N_CHIPS):
                self._copy(p, land, sems, a, m).wait_send()


def _hosted(body, comm, *, name, grid, in_specs, out_specs, out_shape, args, scratch_shapes=()):
    ni, no, ns = len(in_specs), len(out_shape), len(scratch_shapes)
    if comm is None:
        res = pl.pallas_call(
            body, name=name, grid=grid, in_specs=list(in_specs), out_specs=list(out_specs), out_shape=list(out_shape),
            scratch_shapes=list(scratch_shapes),
            compiler_params=pltpu.CompilerParams(dimension_semantics=("arbitrary",) * len(grid), vmem_limit_bytes=VMEM_LIMIT),
        )(*args)
        return list(res), []
    nc, nco = len(comm.args), len(comm.out_shape)

    def full(*refs):
        cin = refs[ni:ni + nc]
        outs = refs[ni + nc:ni + nc + no]
        cout = refs[ni + nc + no:ni + nc + no + nco]
        scratch = refs[ni + nc + no + nco:ni + nc + no + nco + ns]
        csems = refs[ni + nc + no + nco + ns:]
        first = _all_of([pl.program_id(ax) == 0 for ax in range(len(grid))])
        last = _all_of([pl.program_id(ax) == grid[ax] - 1 for ax in range(len(grid))])

        @pl.when(first)
        def _():
            comm.start(cin, cout, csems)

        body(*refs[:ni], *outs, *scratch)

        @pl.when(last)
        def _():
            comm.finish(cin, cout, csems)

    res = pl.pallas_call(
        full, name=name, grid=grid, in_specs=[*in_specs, *[ANY] * nc], out_specs=[*out_specs, *[ANY] * nco],
        out_shape=[*out_shape, *comm.out_shape], scratch_shapes=[*scratch_shapes, *comm.sems],
        compiler_params=pltpu.CompilerParams(dimension_semantics=("arbitrary",) * len(grid), vmem_limit_bytes=VMEM_LIMIT,
                                             has_side_effects=True),
    )(*args, *comm.args)
    return list(res[:no]), list(res[no:])


def _all_of(conds):
    out = conds[0]
    for c in conds[1:]:
        out = jnp.logical_and(out, c)
    return out


class _After:
    def __init__(self, token):
        self.args, self.out_shape, self.sems = [token], [], []

    def start(self, *_):
        pass

    def finish(self, *_):
        pass


def _alone(comm, name):
    return _hosted(lambda: None, comm, name=name, grid=(1,), in_specs=[], out_specs=[], out_shape=[], args=[])[1]


HBM = pl.BlockSpec(memory_space=pltpu.HBM)
SEM = pl.BlockSpec(memory_space=pltpu.SEMAPHORE)
DATAFLOW = pltpu.SideEffectType.DATAFLOW_SIDE_EFFECTING


class _SplitGather:
    PER_ARRAY = 8

    def __init__(self, shards):
        self.plan = _Gather(shards)
        self.n = len(shards)

    @staticmethod
    def _tables(sems_of):
        class Table:
            def __init__(self, pick):
                self.pick = pick

            @property
            def at(self):
                return self

            def __getitem__(self, idx):
                return self.pick(idx)

        return [Table(lambda a: sems_of[a][0]), Table(lambda a: sems_of[a][1]),
                Table(lambda am: sems_of[am[0]][2 + am[1]]), Table(lambda am: sems_of[am[0]][5 + am[1]])]

    def start(self, name, after=()):
        n, plan, per, na = self.n, self.plan, self.PER_ARRAY, len(after)

        def body(*refs):
            loc, land = refs[:n], refs[n:2 * n]
            sems_of = {a: refs[2 * n + na + per * a:2 * n + na + per * (a + 1)] for a in range(n)}
            plan.start(loc, land, self._tables(sems_of))
            refs[-1][...] = jnp.zeros_like(refs[-1])

        lands = [pltpu.with_memory_space_constraint(lax.empty(o.shape, o.dtype), pltpu.HBM) for o in plan.out_shape]
        locs = [pltpu.with_memory_space_constraint(a, pltpu.HBM) for a in plan.args]
        res = pl.pallas_call(
            body, name=name,
            out_shape=[*[DMA(())] * (per * n),
                       *[pltpu.HBM(o.shape, o.dtype) for o in plan.out_shape],
                       jax.ShapeDtypeStruct((8, 128), F32)],
            in_specs=[HBM] * (2 * n) + [pl.BlockSpec(memory_space=pl.ANY)] * na,
            out_specs=[SEM] * (per * n) + [HBM] * n + [pl.BlockSpec(memory_space=pltpu.VMEM)],
            input_output_aliases={n + i: per * n + i for i in range(n)},
            compiler_params=pltpu.CompilerParams(has_side_effects=DATAFLOW),
        )(*locs, *lands, *after)
        self.sems = {a: list(res[per * a:per * (a + 1)]) for a in range(n)}
        self.locs = locs
        self.lands = list(res[per * n:per * n + n])
        self.token = res[-1]

    def wait(self, idxs, after, name):
        plan, g, per = self.plan, len(idxs), self.PER_ARRAY

        def body(*refs):
            loc = {a: refs[j] for j, a in enumerate(idxs)}
            land = {a: refs[g + j] for j, a in enumerate(idxs)}
            sems = self._tables({a: refs[2 * g + per * j:2 * g + per * (j + 1)] for j, a in enumerate(idxs)})
            for a in idxs:
                for m in range(1, N_CHIPS):
                    plan._ici(loc, land, sems, a, m, True).wait_recv()
                    plan._ici(loc, land, sems, a, m, False).wait_send()
                plan._own(loc, land, sems, a).wait_recv()
                plan._own(loc, land, sems, a).wait_send()

        res = pl.pallas_call(
            body, name=name,
            out_shape=[pltpu.HBM(self.lands[a].shape, self.lands[a].dtype) for a in idxs],
            in_specs=[HBM] * (2 * g) + [SEM] * (per * g) + [pl.BlockSpec(memory_space=pl.ANY)], out_specs=[HBM] * g,
            input_output_aliases={g + j: j for j in range(g)},
            compiler_params=pltpu.CompilerParams(has_side_effects=DATAFLOW),
        )(*[self.locs[a] for a in idxs], *[self.lands[a] for a in idxs],
          *[s for a in idxs for s in self.sems[a]], after)
        return list(res)


class _SplitExchange:
    def __init__(self, parts, plan=None):
        self.plan = _ChipExchange(parts) if plan is None else plan
        self.n = len(parts)
        self.PER_ARRAY = 2 * self.plan.PEERS

    def _tables(self, sems_of):
        class Table:
            def __init__(self, pick):
                self.pick = pick

            @property
            def at(self):
                return self

            def __getitem__(self, am):
                return self.pick(am)

        peers = self.plan.PEERS
        return [Table(lambda am: sems_of[am[0]][am[1]]), Table(lambda am: sems_of[am[0]][peers + am[1]])]

    def start(self, name, after=()):
        n, plan, per, na = self.n, self.plan, self.PER_ARRAY, len(after)

        def body(*refs):
            p, land = refs[:n], refs[n:2 * n]
            sems_of = {a: refs[2 * n + na + per * a:2 * n + na + per * (a + 1)] for a in range(n)}
            plan.start(p, land, self._tables(sems_of))
            refs[-1][...] = jnp.zeros_like(refs[-1])

        lands = [pltpu.with_memory_space_constraint(lax.empty(o.shape, o.dtype), pltpu.HBM) for o in plan.out_shape]
        parts = [pltpu.with_memory_space_constraint(a, pltpu.HBM) for a in plan.args]
        res = pl.pallas_call(
            body, name=name,
            out_shape=[*[DMA(())] * (per * n),
                       *[pltpu.HBM(a.shape, a.dtype) for a in plan.args],
                       *[pltpu.HBM(o.shape, o.dtype) for o in plan.out_shape],
                       jax.ShapeDtypeStruct((8, 128), F32)],
            in_specs=[HBM] * (2 * n) + [pl.BlockSpec(memory_space=pl.ANY)] * na,
            out_specs=[SEM] * (per * n) + [HBM] * (2 * n) + [pl.BlockSpec(memory_space=pltpu.VMEM)],
            input_output_aliases={i: per * n + i for i in range(2 * n)},
            compiler_params=pltpu.CompilerParams(has_side_effects=DATAFLOW),
        )(*parts, *lands, *after)
        self.sems = list(res[:per * n])
        self.parts = list(res[per * n:per * n + n])
        self.lands = list(res[per * n + n:per * n + 2 * n])
        return res[-1]

    def wait(self, after, name):
        n, plan, per = self.n, self.plan, self.PER_ARRAY

        def body(*refs):
            p, land = refs[:n], refs[n:2 * n]
            sems_of = {a: refs[2 * n + per * a:2 * n + per * (a + 1)] for a in range(n)}
            plan.finish(p, land, self._tables(sems_of))

        res = pl.pallas_call(
            body, name=name,
            out_shape=[*[pltpu.HBM(a.shape, a.dtype) for a in self.parts], *[pltpu.HBM(a.shape, a.dtype) for a in self.lands]],
            in_specs=[HBM] * (2 * n) + [SEM] * (per * n) + [pl.BlockSpec(memory_space=pl.ANY)] * len(after),
            out_specs=[HBM] * (2 * n), input_output_aliases={i: i for i in range(2 * n)},
            compiler_params=pltpu.CompilerParams(has_side_effects=DATAFLOW),
        )(*self.parts, *self.lands, *self.sems, *after)
        return list(res[:n]), list(res[n:])


PASS_ON_BARRIER = 1


def _sibling_barrier():
    x, y, c, _ = _place()
    barrier = pltpu.get_barrier_semaphore()
    pl.semaphore_signal(barrier, inc=1, device_id=(x, y, 1 - c), device_id_type=MESH)
    pl.semaphore_wait(barrier, 1)


def _pass_on(lands, name):
    n = len(lands)

    def body(*refs):
        out = refs[n:2 * n]
        send_sems, recv_sems = refs[2 * n:]
        x, y, c, k = _place()
        _sibling_barrier()
        cps = []
        for a in range(n):
            for m in range(1, N_CHIPS):
                got = out[a].at[k ^ m, c]
                cp = _remote(got, got, send_sems.at[a, m - 1], recv_sems.at[a, m - 1], (x, y, 1 - c))
                cp.start()
                cps.append(cp)
        for a in range(n):
            for m in range(1, N_CHIPS):
                theirs = out[a].at[k ^ m, 1 - c]
                _remote(theirs, theirs, send_sems.at[a, m - 1], recv_sems.at[a, m - 1], (x, y, 1 - c)).wait_recv()
        for cp in cps:
            cp.wait_send()

    return pl.pallas_call(
        body, name=name, out_shape=[jax.ShapeDtypeStruct(a.shape, a.dtype) for a in lands],
        in_specs=[ANY] * n, out_specs=[ANY] * n, input_output_aliases={a: a for a in range(n)},
        scratch_shapes=[DMA((n, 3)), DMA((n, 3))],
        compiler_params=pltpu.CompilerParams(has_side_effects=True, collective_id=PASS_ON_BARRIER),
    )(*lands)


def _sibling_exchange(grads, small, name, after=None):
    n = len(grads)
    ns = 0 if small is None else 1
    na = 0 if after is None else 1

    def body(*refs):
        g = refs[:n]
        land = refs[n + ns + na:2 * n + ns + na]
        send_sems, recv_sems, own_sem, ssend_sems, srecv_sems = refs[2 * n + 2 * ns + na:]
        x, y, c, k = _place()
        me = 2 * k + c
        cps = []
        for a in range(n):
            cp = _remote(g[a].at[:, pl.ds(1 - c, 1)], land[a], send_sems.at[a], recv_sems.at[a], (x, y, 1 - c))
            cp.start()
            cps.append(cp)
        if small is not None:
            sm, smg = refs[n], refs[2 * n + 1 + na]
            peers = [me ^ m for m in range(1, N_DEV)]
            ids = [(p // 4, (p // 2) % 2, p % 2) for p in peers]
            own = pltpu.make_async_copy(sm, smg.at[me], own_sem)
            own.start()
            for m in range(1, N_DEV):
                cp = _remote(sm, smg.at[me], ssend_sems.at[m - 1], srecv_sems.at[m - 1], ids[m - 1])
                cp.start()
                cps.append(cp)
            for m in range(1, N_DEV):
                _remote(sm, smg.at[peers[m - 1]], ssend_sems.at[m - 1], srecv_sems.at[m - 1], ids[m - 1]).wait_recv()
            own.wait()
        for cp in cps[:n]:
            cp.wait_recv()
        for cp in cps:
            cp.wait_send()

    out_shape = [jax.ShapeDtypeStruct((N_CHIPS, 1) + a.shape[2:], a.dtype) for a in grads]
    ins = list(grads)
    if small is not None:
        out_shape.append(jax.ShapeDtypeStruct((N_DEV,) + small.shape, small.dtype))
        ins.append(small)
    if after is not None:
        ins.append(after)
    return pl.pallas_call(
        body, name=name, out_shape=out_shape, in_specs=[ANY] * (n + ns + na), out_specs=[ANY] * (n + ns),
        scratch_shapes=[DMA((max(n, 1),)), DMA((max(n, 1),)), DMA, DMA((N_DEV - 1,)), DMA((N_DEV - 1,))],
        compiler_params=pltpu.CompilerParams(has_side_effects=True),
    )(*ins)


def _sibling_share(halves, col_half, name, after=()):
    n, na = len(halves), len(after)

    def body(*refs):
        out = refs[n + na:2 * n + na]
        send_sems, recv_sems = refs[2 * n + na:]
        x, y, c, k = _place()

        def half(a, core):
            if not col_half[a]:
                return out[a].at[:, pl.ds(core, 1)]
            cols = out[a].shape[-1] // 2
            return out[a].at[:, :, pl.ds(pl.multiple_of(core * cols, cols), cols)]

        cps = []
        for a in range(n):
            cp = _remote(half(a, c), half(a, c), send_sems.at[a], recv_sems.at[a], (x, y, 1 - c))
            cp.start()
            cps.append(cp)
        for a in range(n):
            _remote(half(a, 1 - c), half(a, 1 - c), send_sems.at[a], recv_sems.at[a], (x, y, 1 - c)).wait_recv()
        for cp in cps:
            cp.wait_send()

    out_shape = [jax.ShapeDtypeStruct(a.shape, a.dtype) for a in halves]
    return pl.pallas_call(
        body, name=name, out_shape=out_shape, in_specs=[ANY] * (n + na), out_specs=[ANY] * n,
        input_output_aliases={a: a for a in range(n)}, scratch_shapes=[DMA((n,)), DMA((n,))],
        compiler_params=pltpu.CompilerParams(has_side_effects=True),
    )(*halves, *after)


def _add_sibling(g, land, core):
    _, _, r, c = g.shape
    rb = _row_block(r, c * (3 * 2 * 2 + 2 * 4))

    def body(core_ref, g_ref, l_ref, o_ref):
        o_ref[...] = (g_ref[...].astype(F32) + l_ref[...].astype(F32)).astype(o_ref.dtype)

    return pl.pallas_call(
        body, name="rs_add_sibling", out_shape=jax.ShapeDtypeStruct((N_CHIPS, r, c), g.dtype),
        grid_spec=pltpu.PrefetchScalarGridSpec(
            num_scalar_prefetch=1, grid=(N_CHIPS, r // rb),
            in_specs=[pl.BlockSpec((None, None, rb, c), lambda j, i, core_ref: (j, core_ref[0], i, 0)),
                      pl.BlockSpec((None, None, rb, c), lambda j, i, core_ref: (j, 0, i, 0))],
            out_specs=pl.BlockSpec((None, rb, c), lambda j, i, core_ref: (j, i, 0))),
        compiler_params=pltpu.CompilerParams(dimension_semantics=("parallel", "parallel"), vmem_limit_bytes=VMEM_LIMIT),
    )(core, g, land)


def _add_chips(part, land, where, layer=0, n_layers=1, into=None, col_half=False, rows=(0, 1)):
    _, r, c = part.shape
    sub, n_sub = rows
    rb = _row_block(r, c * (4 * 2 * 2 + 4 * 2 + 2 * 4))

    def body(where_ref, p_ref, l_ref, *rest):
        acc = p_ref[...].astype(F32)
        for m in range(N_CHIPS - 1):
            acc = acc + l_ref[m].astype(F32)
        rest[-1][...] = acc

    in_specs = [pl.BlockSpec((None, rb, c), lambda i, where_ref: (where_ref[0], i, 0)),
                pl.BlockSpec((N_CHIPS - 1, rb, c), lambda i, where_ref: (0, i, 0))]
    args = [where, part, land]
    if into is not None:
        in_specs.append(ANY)
        args.append(into)
    if col_half:
        out_shape = jax.ShapeDtypeStruct((n_layers, r, 2 * c), F32)
        out_spec = pl.BlockSpec((None, rb, c), lambda i, where_ref: (layer, i, where_ref[1]))
    else:
        out_shape = jax.ShapeDtypeStruct((n_layers, 2, n_sub * r, c), F32)
        out_spec = pl.BlockSpec((None, None, rb, c), lambda i, where_ref: (layer, where_ref[1], sub * (r // rb) + i, 0))
    return pl.pallas_call(
        body, name="rs_add_chips", out_shape=out_shape,
        grid_spec=pltpu.PrefetchScalarGridSpec(
            num_scalar_prefetch=1, grid=(r // rb,), in_specs=in_specs, out_specs=out_spec),
        input_output_aliases={} if into is None else {3: 0},
        compiler_params=pltpu.CompilerParams(dimension_semantics=("parallel",), vmem_limit_bytes=VMEM_LIMIT),
    )(*args)


def _sum_small(smg, own, me):
    def body(me_ref, s_ref, own_ref, o_ref):
        o_ref[...] = jnp.zeros_like(o_ref)
        for j in range(N_DEV):
            @pl.when(me_ref[0] == j)
            def _():
                o_ref[...] += own_ref[...]

            @pl.when(me_ref[0] != j)
            def _():
                o_ref[...] += s_ref[j]

    return pl.pallas_call(
        body, name="rs_sum_small", out_shape=jax.ShapeDtypeStruct(smg.shape[1:], F32),
        grid_spec=pltpu.PrefetchScalarGridSpec(
            num_scalar_prefetch=1, grid=(1,),
            in_specs=[pl.BlockSpec(smg.shape, lambda i, me_ref: (0, 0, 0)), pl.BlockSpec(own.shape, lambda i, me_ref: (0, 0))],
            out_specs=pl.BlockSpec(own.shape, lambda i, me_ref: (0, 0))),
    )(me, smg, own)


def _pool_windows(ext_ref, g, gw, tm, first_row):
    w = POOL_WINDOWS[g]
    slab = ext_ref[:, g * gw:(g + 1) * gw]
    p, k = slab, 1
    while k < w:
        p = p + pltpu.roll(p, k, 0)
        k *= 2
    t = first_row + lax.broadcasted_iota(jnp.int32, (tm, 1), 0)
    cnt = jnp.minimum(t + 1, w).astype(F32)
    return p[POOL_HALO:] / cnt - slab[POOL_HALO:]


def _fwd_pool(x, small, scale, poolw, comm=None):
    t, d = x.shape
    tm = _token_tile(t)
    gw = d // len(POOL_WINDOWS)

    def body(x_ref, sm_ref, sc_ref, w_ref, h_ref, ext_ref, mix_ref):
        i = pl.program_id(0)

        @pl.when(i == 0)
        def _():
            ext_ref[0:POOL_HALO, :] = jnp.zeros((POOL_HALO, d), F32)

        @pl.when(i > 0)
        def _():
            ext_ref[0:POOL_HALO, :] = ext_ref[tm:tm + POOL_HALO, :]

        xv = x_ref[...]
        xh, _ = _rms(xv)
        ext_ref[POOL_HALO:, :] = xh * sm_ref[0:1, :]
        for g in range(len(POOL_WINDOWS)):
            pooled = _pool_windows(ext_ref, g, gw, tm, i * tm)
            cols = slice(g * gw, (g + 1) * gw)
            mix_ref[:, cols] = _dot(pooled.astype(BF16), w_ref[g]) * sc_ref[:, cols]
        mh, _ = _rms(mix_ref[...])
        h_ref[...] = xv + mh * sm_ref[1:2, :]

    (h,), got = _hosted(
        body, comm, name="fwd_pool", grid=(t // tm,), out_shape=[jax.ShapeDtypeStruct((t, d), F32)],
        in_specs=[pl.BlockSpec((tm, d), lambda i: (i, 0)), _resident(small.shape, lambda i: (0, 0)),
                  _resident(scale.shape, lambda i: (0, 0)), _resident(poolw.shape, lambda i: (0, 0, 0))],
        out_specs=[pl.BlockSpec((tm, d), lambda i: (i, 0))],
        scratch_shapes=[pltpu.VMEM((POOL_HALO + tm, d), F32), pltpu.VMEM((tm, d), F32)],
        args=[x, small, scale, poolw])
    return h, got


def _fwd_ffn(h, small, wgu, wd, layer, comm=None, target=None):
    t, d = h.shape
    tm = _token_tile(t)
    steps = t // tm
    fc = wgu.shape[-1]
    f = 2 * fc
    g_in, g_out = 4 * layer + 2, 4 * layer + 3
    with_loss = target is not None

    def body(h_ref, *refs):
        if with_loss:
            t_ref, sm_ref, wgu_ref, wd_ref, o_ref, gu_ref, ff_ref, n_ref, l_ref, acc_ref = refs
        else:
            sm_ref, wgu_ref, wd_ref, o_ref, gu_ref, ff_ref, n_ref = refs
        hv = h_ref[...]
        hh, _ = _rms(hv)
        n = (hh * sm_ref[g_in:g_in + 1, :]).astype(BF16)
        n_ref[...] = n
        ff = None
        for j in range(2):
            gate = _dot(n, wgu_ref[j])
            up = _dot(n, wgu_ref[2 + j])
            gu_ref[:, j * fc:(j + 1) * fc] = gate.astype(BF16)
            gu_ref[:, f + j * fc:f + (j + 1) * fc] = up.astype(BF16)
            act = (gate * jax.nn.sigmoid(gate) * up).astype(BF16)
            part = _dot(act, wd_ref[j * fc:(j + 1) * fc, :])
            ff = part if ff is None else ff + part
        ff_ref[...] = ff
        fh, _ = _rms(ff)
        out = hv + fh * sm_ref[g_out:g_out + 1, :]
        if not with_loss:
            o_ref[...] = out
            return
        i = pl.program_id(0)
        e = out - t_ref[...]
        o_ref[...] = e * (1.0 / d)

        @pl.when(i == 0)
        def _():
            acc_ref[...] = jnp.zeros_like(acc_ref)

        acc_ref[...] += _colsum(e * e)

        @pl.when(i == steps - 1)
        def _():
            l_ref[...] = jnp.full(l_ref.shape, 0.5 / d, F32) * jnp.sum(acc_ref[...])

    row = lambda i: (i, 0)
    out_shape = [jax.ShapeDtypeStruct((t, d), F32), jax.ShapeDtypeStruct((t, 2 * f), BF16),
                 jax.ShapeDtypeStruct((t, d), F32), jax.ShapeDtypeStruct((t, d), BF16)]
    out_specs = [pl.BlockSpec((tm, d), row), pl.BlockSpec((tm, 2 * f), row), pl.BlockSpec((tm, d), row),
                 pl.BlockSpec((tm, d), row)]
    weight_specs = [_resident(small.shape, lambda i: (0, 0)), _resident(wgu.shape, lambda i: (0, 0, 0)),
                    _resident(wd.shape, lambda i: (0, 0))]
    if with_loss:
        return _hosted(
            body, comm, name=f"fwd_ffn{layer}_loss", grid=(steps,),
            out_shape=out_shape + [jax.ShapeDtypeStruct((8, 128), F32)],
            in_specs=[pl.BlockSpec((tm, d), row), pl.BlockSpec((tm, d), row)] + weight_specs,
            out_specs=out_specs + [pl.BlockSpec((8, 128), lambda i: (0, 0))],
            scratch_shapes=[pltpu.VMEM((1, d), F32)], args=[h, target, small, wgu, wd])
    return _hosted(
        body, comm, name=f"fwd_ffn{layer}", grid=(steps,), out_shape=out_shape,
        in_specs=[pl.BlockSpec((tm, d), row)] + weight_specs, out_specs=out_specs, args=[h, small, wgu, wd])


def _fwd_conv(h, small, win, wout, comm=None):
    t, d = h.shape
    tm = _token_tile(t)
    pc = win.shape[-1]

    def body(h_ref, sm_ref, win_ref, wout_ref, o_ref, proj_ref, y_ref, n_ref, pj_ref, uext_ref):
        i = pl.program_id(0)

        @pl.when(i == 0)
        def _():
            uext_ref[0:CONV_HALO, :] = jnp.zeros((CONV_HALO, d), F32)

        @pl.when(i > 0)
        def _():
            uext_ref[0:CONV_HALO, :] = uext_ref[tm:tm + CONV_HALO, :]

        hv = h_ref[...]
        hh, _ = _rms(hv)
        n = (hh * sm_ref[4:5, :]).astype(BF16)
        n_ref[...] = n
        for k in range(N_CHIPS):
            pj_ref[:, k * pc:(k + 1) * pc] = _dot(n, win_ref[k])
        proj_ref[...] = pj_ref[...].astype(BF16)
        uext_ref[CONV_HALO:, :] = pj_ref[:, d:2 * d] * pj_ref[:, 2 * d:]
        taps = [sm_ref[8 + j:9 + j, :] for j in range(3)]
        full = uext_ref[...]
        conv = (full[CONV_HALO:] * taps[2] + pltpu.roll(full, 1, 0)[CONV_HALO:] * taps[1]
                + pltpu.roll(full, 2, 0)[CONV_HALO:] * taps[0])
        y = _dot((pj_ref[:, 0:d] * conv).astype(BF16), wout_ref[...])
        y_ref[...] = y
        yh, _ = _rms(y)
        o_ref[...] = hv + yh * sm_ref[5:6, :]

    row = lambda i: (i, 0)
    return _hosted(
        body, comm, name="fwd_conv", grid=(t // tm,),
        out_shape=[jax.ShapeDtypeStruct((t, d), F32), jax.ShapeDtypeStruct((t, 3 * d), BF16),
                   jax.ShapeDtypeStruct((t, d), F32), jax.ShapeDtypeStruct((t, d), BF16)],
        in_specs=[pl.BlockSpec((tm, d), row), _resident(small.shape, lambda i: (0, 0)),
                  _resident(win.shape, lambda i: (0, 0, 0)), _resident(wout.shape, lambda i: (0, 0))],
        out_specs=[pl.BlockSpec((tm, d), row), pl.BlockSpec((tm, 3 * d), row), pl.BlockSpec((tm, d), row),
                   pl.BlockSpec((tm, d), row)],
        scratch_shapes=[pltpu.VMEM((tm, 3 * d), F32), pltpu.VMEM((CONV_HALO + tm, d), F32)],
        args=[h, small, win, wout])


def _bwd_ffn(dh, h, ff, gu, small, wgu, wd, layer, comm=None):
    t, d = h.shape
    tm = _token_tile(t, 256)
    fc = wgu.shape[-1]
    f = 2 * fc
    g_in, g_out = 4 * layer + 2, 4 * layer + 3

    def body(dh_ref, h_ref, ff_ref, gu_ref, sm_ref, wgu_ref, wd_ref, o_ref, dgu_ref, dff_ref, act_ref, sg_ref):
        i = pl.program_id(0)

        @pl.when(i == 0)
        def _():
            sg_ref[...] = jnp.zeros_like(sg_ref)

        dy = dh_ref[...]
        fh, r3 = _rms(ff_ref[...])
        sg_ref[1:2, :] += _colsum(dy * fh)
        dff = _rms_bwd(dy, fh, r3, sm_ref[g_out:g_out + 1, :]).astype(BF16)
        dff_ref[...] = dff
        for j in range(2):
            dact = _dot_nt(dff, wd_ref[j * fc:(j + 1) * fc, :])
            gate = gu_ref[:, j * fc:(j + 1) * fc].astype(F32)
            up = gu_ref[:, f + j * fc:f + (j + 1) * fc].astype(F32)
            sig = jax.nn.sigmoid(gate)
            silu = gate * sig
            act_ref[:, j * fc:(j + 1) * fc] = (silu * up).astype(BF16)
            dgu_ref[:, j * fc:(j + 1) * fc] = (dact * up * (sig * (1.0 + gate * (1.0 - sig)))).astype(BF16)
            dgu_ref[:, f + j * fc:f + (j + 1) * fc] = (dact * silu).astype(BF16)
        dn = None
        for k in range(N_CHIPS):
            part = _dot_nt(dgu_ref[:, k * fc:(k + 1) * fc], wgu_ref[k])
            dn = part if dn is None else dn + part
        hh, r2 = _rms(h_ref[...])
        sg_ref[0:1, :] += _colsum(dn * hh)
        o_ref[...] = dy + _rms_bwd(dn, hh, r2, sm_ref[g_in:g_in + 1, :])

    row = lambda i: (i, 0)
    return _hosted(
        body, comm, name=f"bwd_ffn{layer}", grid=(t // tm,),
        out_shape=[jax.ShapeDtypeStruct((t, d), F32), jax.ShapeDtypeStruct((t, 2 * f), BF16),
                   jax.ShapeDtypeStruct((t, d), BF16), jax.ShapeDtypeStruct((t, f), BF16),
                   jax.ShapeDtypeStruct((8, d), F32)],
        in_specs=[pl.BlockSpec((tm, d), row), pl.BlockSpec((tm, d), row), pl.BlockSpec((tm, d), row),
                  pl.BlockSpec((tm, 2 * f), row), _resident(small.shape, lambda i: (0, 0)),
                  _resident(wgu.shape, lambda i: (0, 0, 0)), _resident(wd.shape, lambda i: (0, 0))],
        out_specs=[pl.BlockSpec((tm, d), row), pl.BlockSpec((tm, 2 * f), row), pl.BlockSpec((tm, d), row),
                   pl.BlockSpec((tm, f), row), pl.BlockSpec((8, d), lambda i: (0, 0))],
        args=[dh, h, ff, gu, small, wgu, wd])


def _bwd_conv(dh, h, y, proj, small, win, wout, comm=None):
    t, d = h.shape
    tm = _token_tile(t)
    steps = t // tm
    pc = win.shape[-1]
    halo_blocks = tm // 16

    def body(dh_ref, h_ref, y_ref, proj_ref, halo_ref, sm_ref, win_ref, wout_ref,
             o_ref, dproj_ref, dy_ref, bc_ref, sg_ref, uext_ref, dcext_ref, carry_ref):
        i = pl.program_id(0)
        tile = steps - 1 - i

        @pl.when(i == 0)
        def _():
            sg_ref[...] = jnp.zeros_like(sg_ref)
            carry_ref[...] = jnp.zeros_like(carry_ref)

        dy = dh_ref[...]
        yh, r1 = _rms(y_ref[...])
        sg_ref[1:2, :] += _colsum(dy * yh)
        dyv = _rms_bwd(dy, yh, r1, sm_ref[5:6, :]).astype(BF16)
        dy_ref[...] = dyv
        dbc = _dot_nt(dyv, wout_ref[...])
        b = proj_ref[:, 0:d].astype(F32)
        cg = proj_ref[:, d:2 * d].astype(F32)
        v = proj_ref[:, 2 * d:].astype(F32)
        halo = halo_ref[...].astype(F32)[16 - CONV_HALO:]
        uh = halo[:, d:2 * d] * halo[:, 2 * d:]
        uext_ref[0:CONV_HALO, :] = jnp.where(tile > 0, uh, jnp.zeros_like(uh))
        uext_ref[CONV_HALO:, :] = cg * v
        taps = [sm_ref[8 + j:9 + j, :] for j in range(3)]
        full = uext_ref[...]
        u0 = full[CONV_HALO:]
        u1 = pltpu.roll(full, 1, 0)[CONV_HALO:]
        u2 = pltpu.roll(full, 2, 0)[CONV_HALO:]
        conv = u0 * taps[2] + u1 * taps[1] + u2 * taps[0]
        bc_ref[...] = (b * conv).astype(BF16)
        dconv = dbc * b
        sg_ref[4:5, :] += _colsum(dconv * u0)
        sg_ref[3:4, :] += _colsum(dconv * u1)
        sg_ref[2:3, :] += _colsum(dconv * u2)
        dcext_ref[0:tm, :] = dconv
        dcext_ref[tm:, :] = carry_ref[...]
        carry_ref[...] = dconv[0:CONV_HALO]
        dfull = dcext_ref[...]
        n8 = tm + CONV_HALO
        du = (dfull[0:tm] * taps[2] + pltpu.roll(dfull, n8 - 1, 0)[0:tm] * taps[1]
              + pltpu.roll(dfull, n8 - 2, 0)[0:tm] * taps[0])
        dproj_ref[:, 0:d] = (dbc * conv).astype(BF16)
        dproj_ref[:, d:2 * d] = (du * v).astype(BF16)
        dproj_ref[:, 2 * d:] = (du * cg).astype(BF16)
        dn = None
        for k in range(N_CHIPS):
            part = _dot_nt(dproj_ref[:, k * pc:(k + 1) * pc], win_ref[k])
            dn = part if dn is None else dn + part
        hh, r0 = _rms(h_ref[...])
        sg_ref[0:1, :] += _colsum(dn * hh)
        o_ref[...] = dy + _rms_bwd(dn, hh, r0, sm_ref[4:5, :])

    rev = lambda i: (steps - 1 - i, 0)
    before = lambda i: (jnp.maximum((steps - 1 - i) * halo_blocks - 1, 0), 0)
    return _hosted(
        body, comm, name="bwd_conv", grid=(steps,),
        out_shape=[jax.ShapeDtypeStruct((t, d), F32), jax.ShapeDtypeStruct((t, 3 * d), BF16),
                   jax.ShapeDtypeStruct((t, d), BF16), jax.ShapeDtypeStruct((t, d), BF16),
                   jax.ShapeDtypeStruct((8, d), F32)],
        in_specs=[pl.BlockSpec((tm, d), rev), pl.BlockSpec((tm, d), rev), pl.BlockSpec((tm, d), rev),
                  pl.BlockSpec((tm, 3 * d), rev), pl.BlockSpec((16, 3 * d), before),
                  _resident(small.shape, lambda i: (0, 0)), _resident(win.shape, lambda i: (0, 0, 0)),
                  _resident(wout.shape, lambda i: (0, 0))],
        out_specs=[pl.BlockSpec((tm, d), rev), pl.BlockSpec((tm, 3 * d), rev), pl.BlockSpec((tm, d), rev),
                   pl.BlockSpec((tm, d), rev), pl.BlockSpec((8, d), lambda i: (0, 0))],
        scratch_shapes=[pltpu.VMEM((CONV_HALO + tm, d), F32), pltpu.VMEM((tm + CONV_HALO, d), F32),
                        pltpu.VMEM((CONV_HALO, d), F32)],
        args=[dh, h, y, proj, proj, small, win, wout])


def _bwd_pool(dh, x, small, scale, poolw, comm=None):
    t, d = x.shape
    tm = _token_tile(t)
    steps = t // tm
    ng = len(POOL_WINDOWS)
    gw = d // ng
    halo_blocks = tm // POOL_HALO

    def body(dh_ref, x_ref, halo_ref, sm_ref, sc_ref, w_ref, o_ref, dw_ref, sg_ref,
             ext_ref, mix_ref, mm_ref, pb_ref, qext_ref, dhn_ref, carry_ref):
        i = pl.program_id(0)
        tile = steps - 1 - i

        @pl.when(i == 0)
        def _():
            sg_ref[...] = jnp.zeros_like(sg_ref)
            dw_ref[...] = jnp.zeros_like(dw_ref)
            carry_ref[...] = jnp.zeros_like(carry_ref)

        g0 = sm_ref[0:1, :]
        xv = x_ref[...]
        xh, r0 = _rms(xv)
        hx, _ = _rms(halo_ref[...])
        ext_ref[0:POOL_HALO, :] = jnp.where(tile > 0, hx * g0, jnp.zeros_like(hx))
        ext_ref[POOL_HALO:, :] = xh * g0
        for g in range(ng):
            pooled = _pool_windows(ext_ref, g, gw, tm, tile * tm)
            cols = slice(g * gw, (g + 1) * gw)
            pb = pooled.astype(BF16)
            pb_ref[:, cols] = pb
            mm = _dot(pb, w_ref[g])
            mm_ref[:, cols] = mm
            mix_ref[:, cols] = mm * sc_ref[:, cols]
        dy = dh_ref[...]
        mh, r1 = _rms(mix_ref[...])
        sg_ref[1:2, :] += _colsum(dy * mh)
        dmix = _rms_bwd(dy, mh, r1, sm_ref[1:2, :])
        sg_ref[2:3, :] += _colsum(dmix * mm_ref[...])
        mix_ref[...] = dmix * sc_ref[...]
        n16 = tm + POOL_HALO
        for g in range(ng):
            w = POOL_WINDOWS[g]
            cols = slice(g * gw, (g + 1) * gw)
            dmm = mix_ref[:, cols].astype(BF16)
            dpooled = _dot_nt(dmm, w_ref[g])
            dw_ref[g] += _dot_tn(pb_ref[:, cols], dmm)
            trow = tile * tm + lax.broadcasted_iota(jnp.int32, (tm, 1), 0)
            q = dpooled / jnp.minimum(trow + 1, w).astype(F32)
            qext_ref[0:tm, cols] = q
            qext_ref[tm:, cols] = carry_ref[:, cols]
            carry_ref[:, cols] = q[0:POOL_HALO]
            p, k = qext_ref[:, cols], 1
            while k < w:
                p = p + pltpu.roll(p, n16 - k, 0)
                k *= 2
            dhn_ref[:, cols] = p[0:tm] - dpooled
        dhn = dhn_ref[...]
        sg_ref[0:1, :] += _colsum(dhn * xh)
        o_ref[...] = dy + _rms_bwd(dhn, xh, r0, g0)

    rev = lambda i: (steps - 1 - i, 0)
    before = lambda i: (jnp.maximum((steps - 1 - i) * halo_blocks - 1, 0), 0)
    return _hosted(
        body, comm, name="bwd_pool", grid=(steps,),
        out_shape=[jax.ShapeDtypeStruct((t, d), F32), jax.ShapeDtypeStruct((ng, gw, gw), F32),
                   jax.ShapeDtypeStruct((8, d), F32)],
        in_specs=[pl.BlockSpec((tm, d), rev), pl.BlockSpec((tm, d), rev), pl.BlockSpec((POOL_HALO, d), before),
                  _resident(small.shape, lambda i: (0, 0)), _resident(scale.shape, lambda i: (0, 0)),
                  _resident(poolw.shape, lambda i: (0, 0, 0))],
        out_specs=[pl.BlockSpec((tm, d), rev), pl.BlockSpec((ng, gw, gw), lambda i: (0, 0, 0)),
                   pl.BlockSpec((8, d), lambda i: (0, 0))],
        scratch_shapes=[pltpu.VMEM((POOL_HALO + tm, d), F32), pltpu.VMEM((tm, d), F32), pltpu.VMEM((tm, d), F32),
                        pltpu.VMEM((tm, d), BF16), pltpu.VMEM((tm + POOL_HALO, d), F32), pltpu.VMEM((tm, d), F32),
                        pltpu.VMEM((POOL_HALO, d), F32)],
        args=[dh, x, x, small, scale, poolw])


def _weight_grad(a, b, bm, bn, half_on, name, comm=None, rows=(0, 1)):
    t, m = a.shape
    _, n = b.shape
    if half_on == "a":
        a_cols, b_cols = 2 * bm, bn
    else:
        a_cols, b_cols = bm, 2 * bn
    steps = max(m // a_cols, n // b_cols)
    sub, n_sub = rows
    tr = bm // n_sub

    def spec(cols, total):
        if cols == total:
            return _resident((t, cols), lambda p, j: (0, 0))
        return pl.BlockSpec((t, cols), lambda p, j: (0, j))

    def tile(a_ref, b_ref, half):
        if half_on == "a":
            first = half * bm + sub * tr
            return _dot_tn(a_ref[:, first:first + tr], b_ref[...])
        return _dot_tn(a_ref[...], b_ref[:, half * bn:(half + 1) * bn])

    def body(a_ref, b_ref, parts_ref, land_ref, acc_ref, stage_ref, got_ref, send_sems, recv_sems, got_sem):
        p, j = pl.program_id(0), pl.program_id(1)
        x, y, c, _ = _place()
        half = jnp.where(p == 0, 1 - c, c)

        def send(jj):
            return _remote(stage_ref.at[jj % 2], land_ref.at[jj], send_sems.at[jj], recv_sems.at[jj], (x, y, 1 - c))

        def fetch():
            return pltpu.make_async_copy(land_ref.at[j], got_ref, got_sem)

        @pl.when(p == 1)
        def _():
            @pl.when(j == 0)
            def _():
                for jj in range(max(steps - 2, 0), steps):
                    send(jj).wait_send()

            send(j).wait_recv()
            fetch().start()

        for hv in range(2):
            @pl.when(half == hv)
            def _():
                acc_ref[...] = tile(a_ref, b_ref, hv)

        @pl.when(p == 0)
        def _():
            @pl.when(j >= 2)
            def _():
                send(j - 2).wait_send()

            stage_ref[j % 2] = acc_ref[...].astype(BF16)
            send(j).start()

        @pl.when(p == 1)
        def _():
            fetch().wait()
            parts_ref[...] = (acc_ref[...] + got_ref[...].astype(F32)).astype(BF16)

    (parts, _), got = _hosted(
        body, comm, name=name, grid=(2, steps),
        out_shape=[jax.ShapeDtypeStruct((steps, tr, bn), BF16), jax.ShapeDtypeStruct((steps, tr, bn), BF16)],
        in_specs=[spec(a_cols, m), spec(b_cols, n)],
        out_specs=[pl.BlockSpec((None, tr, bn), lambda p, j: (p * j, 0, 0)), ANY],
        scratch_shapes=[pltpu.VMEM((tr, bn), F32), pltpu.VMEM((2, tr, bn), BF16), pltpu.VMEM((tr, bn), BF16),
                        DMA((steps,)), DMA((steps,)), DMA],
        args=[a, b])
    return parts, got


def _cast_layer(w, layer, name):
    _, r, c = w.shape
    rb = _row_block(r, c * (4 + 2) * 2)

    def body(w_ref, o_ref):
        o_ref[...] = w_ref[...].astype(BF16)

    return pl.pallas_call(
        body, name=name, grid=(r // rb,), out_shape=jax.ShapeDtypeStruct((r, c), BF16),
        in_specs=[pl.BlockSpec((None, rb, c), lambda i: (layer, i, 0))], out_specs=pl.BlockSpec((rb, c), lambda i: (i, 0)),
        compiler_params=pltpu.CompilerParams(dimension_semantics=("parallel",), vmem_limit_bytes=VMEM_LIMIT),
    )(w)


def _adamw_math(w, g, m, v):
    bc1 = 1.0 - ADAM_B1 ** ADAM_STEP
    bc2 = 1.0 - ADAM_B2 ** ADAM_STEP
    nm = ADAM_B1 * m + (1.0 - ADAM_B1) * g
    nv = ADAM_B2 * v + (1.0 - ADAM_B2) * (g * g)
    return -ADAM_LR * ((nm / bc1) / (jnp.sqrt(nv / bc2) + ADAM_EPS) + ADAM_WD * w), nm, nv


def _adamw_small(small_sum, where, gains, taps, scale):
    dq = gains[0].shape[-1]
    d = small_sum.shape[-1]

    def body(where_ref, mine_ref, all_ref, gw, gm, gv, tw, tm_, tv, sw, sm, sv,
             gg, gd, gnm, gnv, tg, td, tnm, tnv, sg, sd, snm, snv):
        for layer in range(gw.shape[0]):
            g = mine_ref[4 * layer:4 * layer + 4, :]
            gg[layer] = g
            gd[layer], gnm[layer], gnv[layer] = _adamw_math(gw[layer], g, gm[layer], gv[layer])
        g = mine_ref[8:8 + tw.shape[1], :]
        tg[0] = g
        td[0], tnm[0], tnv[0] = _adamw_math(tw[0], g, tm_[0], tv[0])
        g = all_ref[11:12, :]
        sg[...] = g
        sd[...], snm[...], snv[...] = _adamw_math(sw[...], g, sm[...], sv[...])

    full = lambda a: pl.BlockSpec(a.shape, lambda i, where_ref: (0,) * a.ndim)
    params = [*gains, *taps, *scale]
    outs = [gains[0]] * 4 + [taps[0]] * 4 + [scale[0]] * 4
    res = pl.pallas_call(
        body, name="adamw_small", out_shape=[jax.ShapeDtypeStruct(a.shape, F32) for a in outs],
        grid_spec=pltpu.PrefetchScalarGridSpec(
            num_scalar_prefetch=1, grid=(1,),
            in_specs=[pl.BlockSpec((16, dq), lambda i, where_ref: (0, where_ref[0])), pl.BlockSpec((16, d), lambda i, where_ref: (0, 0)),
                      *[full(a) for a in params]],
            out_specs=[full(a) for a in outs]),
    )(where, small_sum, small_sum, *params)
    return tuple(res[0:4]), tuple(res[4:8]), tuple(res[8:12])


def _adamw(w, g, m, v, name):
    r, c = w.shape
    rb = _row_block(r, c * (8 * 4 * 2 + 4 * 4))

    def body(w_ref, g_ref, m_ref, v_ref, d_ref, nm_ref, nv_ref, go_ref):
        gv = g_ref[...]
        go_ref[...] = gv
        d_ref[...], nm_ref[...], nv_ref[...] = _adamw_math(w_ref[...], gv, m_ref[...], v_ref[...])

    spec = pl.BlockSpec((rb, c), lambda i: (i, 0))
    return pl.pallas_call(
        body, name=name, grid=(r // rb,), out_shape=[jax.ShapeDtypeStruct((r, c), F32)] * 4,
        in_specs=[spec] * 4, out_specs=[spec] * 4,
        compiler_params=pltpu.CompilerParams(dimension_semantics=("parallel",), vmem_limit_bytes=VMEM_LIMIT),
    )(w, g, m, v)


def kernel(x, norm_gains, pool_w, pool_scale, conv_in_w, conv_w, conv_out_w, ffn_gate_up_w, ffn_down_w, loss_target, m_norm_gains, m_pool_w, m_pool_scale, m_conv_in_w, m_conv_w, m_conv_out_w, m_ffn_gate_up_w, m_ffn_down_w, v_norm_gains, v_pool_w, v_pool_scale, v_conv_in_w, v_conv_w, v_conv_out_w, v_ffn_gate_up_w, v_ffn_down_w):
    _, t, d = x.shape
    dq = d // N_CHIPS
    ng = len(POOL_WINDOWS)
    gw = d // ng
    fq = ffn_down_w.shape[1]
    f = N_CHIPS * fq
    fc = f // 2
    core = lax.axis_index("c")
    chip = 2 * lax.axis_index("x") + lax.axis_index("y")
    core_arr = jnp.reshape(core, (1,)).astype(jnp.int32)
    where_arr = jnp.stack([chip, core]).astype(jnp.int32)
    x2, target = x[0], loss_target[0]

    small_loc = jnp.concatenate(
        [norm_gains.reshape(8, dq), conv_w[0], jnp.zeros((5, dq), F32)], axis=0).reshape(1, 2, 8, dq)
    pool_loc = pool_w.astype(BF16).reshape(1, 2, ng // 2 * (gw // N_CHIPS), gw)
    wgu_loc = [_cast_layer(ffn_gate_up_w, 0, "cast_gate_up0").reshape(1, 2, d // 2, fc),
               ffn_gate_up_w[1:2].astype(BF16).reshape(1, 2, d // 2, fc)]
    wd_loc = [_cast_layer(ffn_down_w, 0, "cast_down0").reshape(1, 2, fq // 2, d),
              ffn_down_w[1:2].astype(BF16).reshape(1, 2, fq // 2, d)]
    win_loc = conv_in_w.astype(BF16).reshape(1, 2, d // 2, -1)
    wout_loc = conv_out_w.astype(BF16).reshape(1, 2, dq // 2, d)

    def ffn_weights(wgu_f, wd_f):
        return wgu_f.reshape(N_CHIPS, d, fc), wd_f.reshape(f, d)

    ag0 = _SplitGather([(pool_loc, 0), (small_loc, 0), (wgu_loc[0], 0), (wd_loc[0], 0)])
    ag0.start("ag_start_layer0")
    ag1 = _SplitGather([(win_loc, 0), (wout_loc, 0), (wgu_loc[1], 0), (wd_loc[1], 0)])
    ag1.start("ag_start_layer1", [ag0.token])
    pool_f, small_f = _pass_on(ag0.wait([0, 1], ag1.token, "ag_wait_first"), "ag_pass_first")
    poolw = pool_f.reshape(N_CHIPS, ng, gw // N_CHIPS, gw).transpose(1, 0, 2, 3).reshape(ng, gw, gw)
    small = small_f.transpose(1, 2, 0, 3).reshape(16, d)
    h1, _ = _fwd_pool(x2, small, pool_scale, poolw)
    wgu0, wd0 = ffn_weights(*_pass_on(ag0.wait([2, 3], h1, "ag_wait_ffn0"), "ag_pass_ffn0"))
    (h2, gu0, ff0, n0), _ = _fwd_ffn(h1, small, wgu0, wd0, 0)
    win_f, wout_f = _pass_on(ag1.wait([0, 1], h2, "ag_wait_conv"), "ag_pass_conv")
    win_f, wout_f = win_f.reshape(N_CHIPS, d, -1), wout_f.reshape(d, d)
    (h3, proj, y, nc), _ = _fwd_conv(h2, small, win_f, wout_f)
    wgu1, wd1 = ffn_weights(*_pass_on(ag1.wait([2, 3], h3, "ag_wait_ffn1"), "ag_pass_ffn1"))
    (dh4, gu1, ff1, n1, loss_blk), _ = _fwd_ffn(h3, small, wgu1, wd1, 1, target=target)

    (dh3, dgu1, dff1, act1, sg_f1), _ = _bwd_ffn(dh4, h3, ff1, gu1, small, wgu1, wd1, 1)
    parts_d1, _ = _weight_grad(act1, dff1, fc, d // 2, "b", "dw_down1")
    parts_gu1, _ = _weight_grad(n1, dgu1, d // 2, fc, "a", "dw_gate_up1")
    ex_ffn1 = _SplitExchange([parts_d1.reshape(N_CHIPS, fq, d // 2), parts_gu1])
    started = ex_ffn1.start("rs_start_ffn1")
    (dh2, dproj, dyv, bcv, sg_c), _ = _bwd_conv(dh3, h2, y, proj, small, win_f, wout_f, _After(started))
    parts_in, _ = _weight_grad(nc, dproj, d // 2, 3 * d // N_CHIPS, "a", "dw_conv_in")
    parts_out, _ = _weight_grad(bcv, dyv, dq // 2, d, "a", "dw_conv_out")
    ex_conv = _SplitExchange([parts_in, parts_out])
    started = ex_conv.start("rs_start_conv")
    (dh1, dgu0, dff0, act0, sg_f0), _ = _bwd_ffn(dh2, h1, ff0, gu0, small, wgu0, wd0, 0, _After(started))
    parts_d0, _ = _weight_grad(act0, dff0, fc, d // 2, "b", "dw_down0")
    parts_gu0, _ = _weight_grad(n0, dgu0, d // 2, fc, "a", "dw_gate_up0")
    ex_ffn0 = _SplitExchange([parts_d0.reshape(N_CHIPS, fq, d // 2), parts_gu0])
    started = ex_ffn0.start("rs_start_ffn0")
    (grad_x, dpool, sg_p), _ = _bwd_pool(dh1, x2, small, pool_scale, poolw, _After(started))
    g_pool = dpool.astype(BF16).reshape(2, ng // 2, N_CHIPS, gw // N_CHIPS, gw).transpose(2, 0, 1, 3, 4).reshape(
        N_CHIPS, 2, ng // 2 * (gw // N_CHIPS), gw)
    small_g = jnp.concatenate(
        [sg_p[0:2], sg_f0[0:2], sg_c[0:2], sg_f1[0:2], sg_c[2:5], sg_p[2:3],
         jnp.broadcast_to(loss_blk[0:1, 0:1], (1, d)), jnp.zeros((3, d), F32)], axis=0)
    (land_p,) = _sibling_exchange([g_pool], None, "rs_sibling_pool")
    ex_pool = _SplitExchange([_add_sibling(g_pool, land_p, core_arr)])
    started = ex_pool.start("rs_start_pool")

    def update(w, g, m, v, name):
        if w.size * 4 * 8 <= STREAM_BUDGET // 4:
            def body(w_ref, g_ref, m_ref, v_ref, go_ref, d_ref, nm_ref, nv_ref):
                go_ref[...] = g_ref[...]
                d_ref[...], nm_ref[...], nv_ref[...] = _adamw_math(w_ref[...], g_ref[...], m_ref[...], v_ref[...])

            return tuple(pl.pallas_call(body, name="adamw_" + name, out_shape=[jax.ShapeDtypeStruct(w.shape, F32)] * 4)(
                w, g.reshape(w.shape), m, v))
        flat = (-1, w.shape[-1])
        dl, m2, v2, g2 = _adamw(w.reshape(flat), g.reshape(flat), m.reshape(flat), v.reshape(flat), "adamw_" + name)
        return tuple(o.reshape(w.shape) for o in (g2, dl, m2, v2))

    (parts_d1, parts_gu1), (recv_d1, recv_gu1) = ex_ffn1.wait([started], "rs_wait_ffn1")
    (parts_in, parts_out), (recv_in, recv_out) = ex_conv.wait([started], "rs_wait_conv")
    gs_gu = _add_chips(parts_gu1, recv_gu1, where_arr, 1, 2)
    gs_d = _add_chips(parts_d1, recv_d1, where_arr, 1, 2, col_half=True)
    gs_in = _add_chips(parts_in, recv_in, where_arr)
    gs_out = _add_chips(parts_out, recv_out, where_arr)
    full_in, full_out = _sibling_share([gs_in, gs_out], [False, False], "rs_share_conv")
    up_in = update(conv_in_w, full_in.reshape(1, d, -1), m_conv_in_w, v_conv_in_w, "conv_in")
    up_out = update(conv_out_w, full_out.reshape(1, dq, d), m_conv_out_w, v_conv_out_w, "conv_out")
    done_first = [up_in[1], up_out[1], gs_gu, gs_d]
    (parts_d0, parts_gu0), (recv_d0, recv_gu0) = ex_ffn0.wait(done_first, "rs_wait_ffn0")
    (parts_p,), (recv_p,) = ex_pool.wait(done_first, "rs_wait_pool")
    ex_small = _SplitExchange([small_g], _SmallGather(small_g))
    started = ex_small.start("rs_start_small", [recv_gu0])
    gs_gu = _add_chips(parts_gu0, recv_gu0, where_arr, 0, 2, gs_gu)
    gs_d = _add_chips(parts_d0, recv_d0, where_arr, 0, 2, gs_d, col_half=True)
    gs_pool = _add_chips(parts_p, recv_p, where_arr)
    full_gu, full_d, full_pool = _sibling_share([gs_gu, gs_d, gs_pool], [False, True, False], "rs_share_ffn", [started])
    up_gu = update(ffn_gate_up_w, full_gu.reshape(2, d, fc), m_ffn_gate_up_w, v_ffn_gate_up_w, "gate_up")
    up_d = update(ffn_down_w, full_d.reshape(2, fq, d), m_ffn_down_w, v_ffn_down_w, "down")
    (small_own,), (small_all,) = ex_small.wait([up_gu[1], up_d[1]], "rs_wait_small")
    small_sum = _sum_small(small_all, small_own, (2 * chip + core).reshape(1).astype(jnp.int32))
    loss = small_sum[12, 0]
    up_gains, up_taps, up_scale = _adamw_small(
        small_sum, where_arr, (norm_gains, m_norm_gains, v_norm_gains), (conv_w, m_conv_w, v_conv_w),
        (pool_scale, m_pool_scale, v_pool_scale))

    ups = [
        up_gains,
        update(pool_w, full_pool.reshape(1, ng, gw // N_CHIPS, gw), m_pool_w, v_pool_w, "pool_w"),
        up_scale,
        up_in,
        up_taps,
        up_out,
        up_gu,
        up_d,
    ]
    grads_out, deltas, new_ms, new_vs = (list(col) for col in zip(*ups))
    return (loss, grad_x[None], *grads_out, *deltas, *new_ms, *new_vs)
```

```python
import jax
import jax.numpy as jnp
from jax import lax
from jax.experimental import pallas as pl
from jax.experimental.pallas import tpu as pltpu

RMS_EPS = 1e-6
POOL_WINDOWS = (2, 4, 8, 16)
POOL_HALO = 16
CONV_HALO = 8
N_CHIPS = 4
N_DEV = 8
ADAM_LR = 0.001
ADAM_B1 = 0.9
ADAM_B2 = 0.999
ADAM_EPS = 1e-08
ADAM_WD = 0.01
ADAM_STEP = 10
VMEM_LIMIT = 56 * 2**20
STREAM_BUDGET = 24 * 2**20
MESH = pl.DeviceIdType.MESH
ANY = pl.BlockSpec(memory_space=pl.ANY)
DMA = pltpu.SemaphoreType.DMA
BF16 = jnp.bfloat16
F32 = jnp.float32


TOKEN_TILE = 512
FFN_BWD_TOKEN_TILE = 256


def _token_tile(t, rows=TOKEN_TILE):
    return min(rows, t)


def _rms(x):
    r = lax.rsqrt(jnp.mean(x * x, axis=-1, keepdims=True) + RMS_EPS)
    return x * r, r


def _rms_bwd(dy, xh, r, g):
    a = dy * g
    return r * (a - xh * jnp.mean(a * xh, axis=-1, keepdims=True))


def _dot(a, b):
    return jnp.dot(a, b, preferred_element_type=F32)


def _dot_nt(a, b):
    return lax.dot_general(a, b, (((1,), (1,)), ((), ())), preferred_element_type=F32)


def _dot_tn(a, b):
    return lax.dot_general(a, b, (((0,), (0,)), ((), ())), preferred_element_type=F32)


def _colsum(a):
    return jnp.sum(a, axis=0, keepdims=True)


def _resident(block, index_map):
    return pl.BlockSpec(block, index_map, pipeline_mode=pl.Buffered(1))


def _row_block(r, row_bytes):
    best = None
    for rb in range(16, r + 1, 16):
        if r % rb == 0 and rb * row_bytes <= STREAM_BUDGET:
            best = rb
    return best if best is not None else r


def _place():
    x, y, c = lax.axis_index("x"), lax.axis_index("y"), lax.axis_index("c")
    return x, y, c, 2 * x + y


def _dev(chip, core):
    return (chip // 2, chip % 2, core)


def _remote(src, dst, send_sem, recv_sem, device):
    return pltpu.make_async_remote_copy(src_ref=src, dst_ref=dst, send_sem=send_sem, recv_sem=recv_sem,
                                        device_id=device, device_id_type=MESH)


class _Gather:
    def __init__(self, shards):
        self.args = [s for s, _ in shards]
        self.layers = [l for _, l in shards]
        self.out_shape = [jax.ShapeDtypeStruct((N_CHIPS,) + s.shape[1:], s.dtype) for s in self.args]

    def _own(self, loc, out, sems, a):
        x, y, c, k = _place()
        return _remote(loc[a].at[self.layers[a]], out[a].at[k], sems[0].at[a], sems[1].at[a], (x, y, 1 - c))

    def _ici(self, loc, out, sems, a, m, arrival):
        x, y, c, k = _place()
        dst = out[a].at[k ^ m, c] if arrival else out[a].at[k, c]
        return _remote(loc[a].at[self.layers[a], c], dst, sems[2].at[a, m - 1], sems[3].at[a, m - 1], _dev(k ^ m, c))

    def start(self, loc, out, sems):
        for a in range(len(self.args)):
            for m in range(1, N_CHIPS):
                self._ici(loc, out, sems, a, m, False).start()
            self._own(loc, out, sems, a).start()


class _SmallGather:
    PEERS = N_DEV - 1

    def __init__(self, small):
        self.args = [small]
        self.out_shape = [jax.ShapeDtypeStruct((N_DEV,) + small.shape, small.dtype)]

    def _copy(self, sm, land, sems, m, arrival):
        x, y, c, k = _place()
        me = 2 * k + c
        peer = me ^ m
        return _remote(sm[0], land[0].at[peer if arrival else me], sems[0].at[0, m - 1], sems[1].at[0, m - 1],
                       (peer // 4, (peer // 2) % 2, peer % 2))

    def start(self, sm, land, sems):
        for m in range(1, N_DEV):
            self._copy(sm, land, sems, m, False).start()

    def finish(self, sm, land, sems):
        for m in range(1, N_DEV):
            self._copy(sm, land, sems, m, True).wait_recv()
        for m in range(1, N_DEV):
            self._copy(sm, land, sems, m, False).wait_send()


class _ChipExchange:
    PEERS = N_CHIPS - 1

    def __init__(self, parts):
        self.args = list(parts)
        self.out_shape = [jax.ShapeDtypeStruct((N_CHIPS - 1,) + p.shape[1:], p.dtype) for p in parts]

    def _copy(self, p, land, sems, a, m):
        x, y, c, k = _place()
        return _remote(p[a].at[k ^ m], land[a].at[m - 1], sems[0].at[a, m - 1], sems[1].at[a, m - 1], _dev(k ^ m, c))

    def start(self, p, land, sems):
        for a in range(len(self.args)):
            for m in range(1, N_CHIPS):
                self._copy(p, land, sems, a, m).start()

    def finish(self, p, land, sems):
        for a in range(len(self.args)):
            for m in range(1, N_CHIPS):
                self._copy(p, land, sems, a, m).wait_recv()
        for a in range(len(self.args)):
            for m in range(1, N_CHIPS):
                self._copy(p, land, sems, a, m).wait_send()


def _gridded(body, after, *, name, grid, in_specs, out_specs, out_shape, args, scratch_shapes=()):
    ni, na = len(in_specs), len(after)

    def full(*refs):
        body(*refs[:ni], *refs[ni + na:])

    return list(pl.pallas_call(
        full, name=name, grid=grid, in_specs=[*in_specs, *[ANY] * na], out_specs=list(out_specs),
        out_shape=list(out_shape), scratch_shapes=list(scratch_shapes),
        compiler_params=pltpu.CompilerParams(dimension_semantics=("arbitrary",) * len(grid), vmem_limit_bytes=VMEM_LIMIT),
    )(*args, *after))


HBM = pl.BlockSpec(memory_space=pltpu.HBM)
SEM = pl.BlockSpec(memory_space=pltpu.SEMAPHORE)
DATAFLOW = pltpu.SideEffectType.DATAFLOW_SIDE_EFFECTING


class _SplitGather:
    PER_ARRAY = 8

    def __init__(self, shards):
        self.plan = _Gather(shards)
        self.n = len(shards)

    @staticmethod
    def _tables(sems_of):
        class Table:
            def __init__(self, pick):
                self.pick = pick

            @property
            def at(self):
                return self

            def __getitem__(self, idx):
                return self.pick(idx)

        return [Table(lambda a: sems_of[a][0]), Table(lambda a: sems_of[a][1]),
                Table(lambda am: sems_of[am[0]][2 + am[1]]), Table(lambda am: sems_of[am[0]][5 + am[1]])]

    def start(self, name, after=()):
        n, plan, per, na = self.n, self.plan, self.PER_ARRAY, len(after)

        def body(*refs):
            loc, land = refs[:n], refs[n:2 * n]
            sems_of = {a: refs[2 * n + na + per * a:2 * n + na + per * (a + 1)] for a in range(n)}
            plan.start(loc, land, self._tables(sems_of))
            refs[-1][...] = jnp.zeros_like(refs[-1])

        lands = [pltpu.with_memory_space_constraint(lax.empty(o.shape, o.dtype), pltpu.HBM) for o in plan.out_shape]
        locs = [pltpu.with_memory_space_constraint(a, pltpu.HBM) for a in plan.args]
        res = pl.pallas_call(
            body, name=name,
            out_shape=[*[DMA(())] * (per * n),
                       *[pltpu.HBM(o.shape, o.dtype) for o in plan.out_shape],
                       jax.ShapeDtypeStruct((8, 128), F32)],
            in_specs=[HBM] * (2 * n) + [pl.BlockSpec(memory_space=pl.ANY)] * na,
            out_specs=[SEM] * (per * n) + [HBM] * n + [pl.BlockSpec(memory_space=pltpu.VMEM)],
            input_output_aliases={n + i: per * n + i for i in range(n)},
            compiler_params=pltpu.CompilerParams(has_side_effects=DATAFLOW),
        )(*locs, *lands, *after)
        self.sems = {a: list(res[per * a:per * (a + 1)]) for a in range(n)}
        self.locs = locs
        self.lands = list(res[per * n:per * n + n])
        self.token = res[-1]

    def wait(self, idxs, after, name):
        plan, g, per = self.plan, len(idxs), self.PER_ARRAY

        def body(*refs):
            loc = {a: refs[j] for j, a in enumerate(idxs)}
            land = {a: refs[g + j] for j, a in enumerate(idxs)}
            sems = self._tables({a: refs[2 * g + per * j:2 * g + per * (j + 1)] for j, a in enumerate(idxs)})
            for a in idxs:
                for m in range(1, N_CHIPS):
                    plan._ici(loc, land, sems, a, m, True).wait_recv()
                    plan._ici(loc, land, sems, a, m, False).wait_send()
                plan._own(loc, land, sems, a).wait_recv()
                plan._own(loc, land, sems, a).wait_send()

        res = pl.pallas_call(
            body, name=name,
            out_shape=[pltpu.HBM(self.lands[a].shape, self.lands[a].dtype) for a in idxs],
            in_specs=[HBM] * (2 * g) + [SEM] * (per * g) + [pl.BlockSpec(memory_space=pl.ANY)], out_specs=[HBM] * g,
            input_output_aliases={g + j: j for j in range(g)},
            compiler_params=pltpu.CompilerParams(has_side_effects=DATAFLOW),
        )(*[self.locs[a] for a in idxs], *[self.lands[a] for a in idxs],
          *[s for a in idxs for s in self.sems[a]], after)
        return list(res)


class _SplitExchange:
    def __init__(self, parts, plan=None):
        self.plan = _ChipExchange(parts) if plan is None else plan
        self.n = len(parts)
        self.PER_ARRAY = 2 * self.plan.PEERS

    def _tables(self, sems_of):
        class Table:
            def __init__(self, pick):
                self.pick = pick

            @property
            def at(self):
                return self

            def __getitem__(self, am):
                return self.pick(am)

        peers = self.plan.PEERS
        return [Table(lambda am: sems_of[am[0]][am[1]]), Table(lambda am: sems_of[am[0]][peers + am[1]])]

    def start(self, name, after=()):
        n, plan, per, na = self.n, self.plan, self.PER_ARRAY, len(after)

        def body(*refs):
            p, land = refs[:n], refs[n:2 * n]
            sems_of = {a: refs[2 * n + na + per * a:2 * n + na + per * (a + 1)] for a in range(n)}
            plan.start(p, land, self._tables(sems_of))
            refs[-1][...] = jnp.zeros_like(refs[-1])

        lands = [pltpu.with_memory_space_constraint(lax.empty(o.shape, o.dtype), pltpu.HBM) for o in plan.out_shape]
        parts = [pltpu.with_memory_space_constraint(a, pltpu.HBM) for a in plan.args]
        res = pl.pallas_call(
            body, name=name,
            out_shape=[*[DMA(())] * (per * n),
                       *[pltpu.HBM(a.shape, a.dtype) for a in plan.args],
                       *[pltpu.HBM(o.shape, o.dtype) for o in plan.out_shape],
                       jax.ShapeDtypeStruct((8, 128), F32)],
            in_specs=[HBM] * (2 * n) + [pl.BlockSpec(memory_space=pl.ANY)] * na,
            out_specs=[SEM] * (per * n) + [HBM] * (2 * n) + [pl.BlockSpec(memory_space=pltpu.VMEM)],
            input_output_aliases={i: per * n + i for i in range(2 * n)},
            compiler_params=pltpu.CompilerParams(has_side_effects=DATAFLOW),
        )(*parts, *lands, *after)
        self.sems = list(res[:per * n])
        self.parts = list(res[per * n:per * n + n])
        self.lands = list(res[per * n + n:per * n + 2 * n])
        return res[-1]

    def wait(self, after, name):
        n, plan, per = self.n, self.plan, self.PER_ARRAY

        def body(*refs):
            p, land = refs[:n], refs[n:2 * n]
            sems_of = {a: refs[2 * n + per * a:2 * n + per * (a + 1)] for a in range(n)}
            plan.finish(p, land, self._tables(sems_of))

        res = pl.pallas_call(
            body, name=name,
            out_shape=[*[pltpu.HBM(a.shape, a.dtype) for a in self.parts], *[pltpu.HBM(a.shape, a.dtype) for a in self.lands]],
            in_specs=[HBM] * (2 * n) + [SEM] * (per * n) + [pl.BlockSpec(memory_space=pl.ANY)] * len(after),
            out_specs=[HBM] * (2 * n), input_output_aliases={i: i for i in range(2 * n)},
            compiler_params=pltpu.CompilerParams(has_side_effects=DATAFLOW),
        )(*self.parts, *self.lands, *self.sems, *after)
        return list(res[:n]), list(res[n:])


PASS_ON_BARRIER = 1


def _sibling_barrier():
    x, y, c, _ = _place()
    barrier = pltpu.get_barrier_semaphore()
    pl.semaphore_signal(barrier, inc=1, device_id=(x, y, 1 - c), device_id_type=MESH)
    pl.semaphore_wait(barrier, 1)


def _pass_on(lands, name):
    n = len(lands)

    def body(*refs):
        out = refs[n:2 * n]
        send_sems, recv_sems = refs[2 * n:]
        x, y, c, k = _place()
        _sibling_barrier()
        cps = []
        for a in range(n):
            for m in range(1, N_CHIPS):
                got = out[a].at[k ^ m, c]
                cp = _remote(got, got, send_sems.at[a, m - 1], recv_sems.at[a, m - 1], (x, y, 1 - c))
                cp.start()
                cps.append(cp)
        for a in range(n):
            for m in range(1, N_CHIPS):
                theirs = out[a].at[k ^ m, 1 - c]
                _remote(theirs, theirs, send_sems.at[a, m - 1], recv_sems.at[a, m - 1], (x, y, 1 - c)).wait_recv()
        for cp in cps:
            cp.wait_send()

    return pl.pallas_call(
        body, name=name, out_shape=[jax.ShapeDtypeStruct(a.shape, a.dtype) for a in lands],
        in_specs=[ANY] * n, out_specs=[ANY] * n, input_output_aliases={a: a for a in range(n)},
        scratch_shapes=[DMA((n, 3)), DMA((n, 3))],
        compiler_params=pltpu.CompilerParams(has_side_effects=True, collective_id=PASS_ON_BARRIER),
    )(*lands)


def _sibling_exchange(g, name):
    def body(g_ref, land_ref, send_sem, recv_sem):
        x, y, c, _ = _place()
        cp = _remote(g_ref.at[:, pl.ds(1 - c, 1)], land_ref, send_sem, recv_sem, (x, y, 1 - c))
        cp.start()
        cp.wait_recv()
        cp.wait_send()

    return pl.pallas_call(
        body, name=name, out_shape=jax.ShapeDtypeStruct((N_CHIPS, 1) + g.shape[2:], g.dtype), in_specs=[ANY],
        out_specs=ANY, scratch_shapes=[DMA, DMA], compiler_params=pltpu.CompilerParams(has_side_effects=True),
    )(g)


def _sibling_share(halves, col_half, name, after=()):
    n, na = len(halves), len(after)

    def body(*refs):
        out = refs[n + na:2 * n + na]
        send_sems, recv_sems = refs[2 * n + na:]
        x, y, c, k = _place()

        def half(a, core):
            if not col_half[a]:
                return out[a].at[:, pl.ds(core, 1)]
            cols = out[a].shape[-1] // 2
            return out[a].at[:, :, pl.ds(pl.multiple_of(core * cols, cols), cols)]

        cps = []
        for a in range(n):
            cp = _remote(half(a, c), half(a, c), send_sems.at[a], recv_sems.at[a], (x, y, 1 - c))
            cp.start()
            cps.append(cp)
        for a in range(n):
            _remote(half(a, 1 - c), half(a, 1 - c), send_sems.at[a], recv_sems.at[a], (x, y, 1 - c)).wait_recv()
        for cp in cps:
            cp.wait_send()

    out_shape = [jax.ShapeDtypeStruct(a.shape, a.dtype) for a in halves]
    return pl.pallas_call(
        body, name=name, out_shape=out_shape, in_specs=[ANY] * (n + na), out_specs=[ANY] * n,
        input_output_aliases={a: a for a in range(n)}, scratch_shapes=[DMA((n,)), DMA((n,))],
        compiler_params=pltpu.CompilerParams(has_side_effects=True),
    )(*halves, *after)


def _add_sibling(g, land, core):
    _, _, r, c = g.shape
    rb = _row_block(r, c * (3 * 2 * 2 + 2 * 4))

    def body(core_ref, g_ref, l_ref, o_ref):
        o_ref[...] = (g_ref[...].astype(F32) + l_ref[...].astype(F32)).astype(o_ref.dtype)

    return pl.pallas_call(
        body, name="rs_add_sibling", out_shape=jax.ShapeDtypeStruct((N_CHIPS, r, c), g.dtype),
        grid_spec=pltpu.PrefetchScalarGridSpec(
            num_scalar_prefetch=1, grid=(N_CHIPS, r // rb),
            in_specs=[pl.BlockSpec((None, None, rb, c), lambda j, i, core_ref: (j, core_ref[0], i, 0)),
                      pl.BlockSpec((None, None, rb, c), lambda j, i, core_ref: (j, 0, i, 0))],
            out_specs=pl.BlockSpec((None, rb, c), lambda j, i, core_ref: (j, i, 0))),
        compiler_params=pltpu.CompilerParams(dimension_semantics=("parallel", "parallel"), vmem_limit_bytes=VMEM_LIMIT),
    )(core, g, land)


def _add_chips(part, land, where, layer=0, n_layers=1, into=None, col_half=False):
    _, r, c = part.shape
    rb = _row_block(r, c * (4 * 2 * 2 + 4 * 2 + 2 * 4))

    def body(where_ref, p_ref, l_ref, *rest):
        acc = p_ref[...].astype(F32)
        for m in range(N_CHIPS - 1):
            acc = acc + l_ref[m].astype(F32)
        rest[-1][...] = acc

    in_specs = [pl.BlockSpec((None, rb, c), lambda i, where_ref: (where_ref[0], i, 0)),
                pl.BlockSpec((N_CHIPS - 1, rb, c), lambda i, where_ref: (0, i, 0))]
    args = [where, part, land]
    if into is not None:
        in_specs.append(ANY)
        args.append(into)
    if col_half:
        out_shape = jax.ShapeDtypeStruct((n_layers, r, 2 * c), F32)
        out_spec = pl.BlockSpec((None, rb, c), lambda i, where_ref: (layer, i, where_ref[1]))
    else:
        out_shape = jax.ShapeDtypeStruct((n_layers, 2, r, c), F32)
        out_spec = pl.BlockSpec((None, None, rb, c), lambda i, where_ref: (layer, where_ref[1], i, 0))
    return pl.pallas_call(
        body, name="rs_add_chips", out_shape=out_shape,
        grid_spec=pltpu.PrefetchScalarGridSpec(
            num_scalar_prefetch=1, grid=(r // rb,), in_specs=in_specs, out_specs=out_spec),
        input_output_aliases={} if into is None else {3: 0},
        compiler_params=pltpu.CompilerParams(dimension_semantics=("parallel",), vmem_limit_bytes=VMEM_LIMIT),
    )(*args)


def _sum_small(smg, own, me):
    def body(me_ref, s_ref, own_ref, o_ref):
        o_ref[...] = jnp.zeros_like(o_ref)
        for j in range(N_DEV):
            @pl.when(me_ref[0] == j)
            def _():
                o_ref[...] += own_ref[...]

            @pl.when(me_ref[0] != j)
            def _():
                o_ref[...] += s_ref[j]

    return pl.pallas_call(
        body, name="rs_sum_small", out_shape=jax.ShapeDtypeStruct(smg.shape[1:], F32),
        grid_spec=pltpu.PrefetchScalarGridSpec(
            num_scalar_prefetch=1, grid=(1,),
            in_specs=[pl.BlockSpec(smg.shape, lambda i, me_ref: (0, 0, 0)), pl.BlockSpec(own.shape, lambda i, me_ref: (0, 0))],
            out_specs=pl.BlockSpec(own.shape, lambda i, me_ref: (0, 0))),
    )(me, smg, own)


def _pool_windows(ext_ref, g, gw, tm, first_row):
    w = POOL_WINDOWS[g]
    slab = ext_ref[:, g * gw:(g + 1) * gw]
    p, k = slab, 1
    while k < w:
        p = p + pltpu.roll(p, k, 0)
        k *= 2
    t = first_row + lax.broadcasted_iota(jnp.int32, (tm, 1), 0)
    cnt = jnp.minimum(t + 1, w).astype(F32)
    return p[POOL_HALO:] / cnt - slab[POOL_HALO:]


def _fwd_pool(x, small, scale, poolw, after=()):
    t, d = x.shape
    tm = _token_tile(t)
    gw = d // len(POOL_WINDOWS)

    def body(x_ref, sm_ref, sc_ref, w_ref, h_ref, ext_ref, mix_ref):
        i = pl.program_id(0)

        @pl.when(i == 0)
        def _():
            ext_ref[0:POOL_HALO, :] = jnp.zeros((POOL_HALO, d), F32)

        @pl.when(i > 0)
        def _():
            ext_ref[0:POOL_HALO, :] = ext_ref[tm:tm + POOL_HALO, :]

        xv = x_ref[...]
        xh, _ = _rms(xv)
        ext_ref[POOL_HALO:, :] = xh * sm_ref[0:1, :]
        for g in range(len(POOL_WINDOWS)):
            pooled = _pool_windows(ext_ref, g, gw, tm, i * tm)
            cols = slice(g * gw, (g + 1) * gw)
            mix_ref[:, cols] = _dot(pooled.astype(BF16), w_ref[g]) * sc_ref[:, cols]
        mh, _ = _rms(mix_ref[...])
        h_ref[...] = xv + mh * sm_ref[1:2, :]

    return _gridded(
        body, after, name="fwd_pool", grid=(t // tm,), out_shape=[jax.ShapeDtypeStruct((t, d), F32)],
        in_specs=[pl.BlockSpec((tm, d), lambda i: (i, 0)), _resident(small.shape, lambda i: (0, 0)),
                  _resident(scale.shape, lambda i: (0, 0)), _resident(poolw.shape, lambda i: (0, 0, 0))],
        out_specs=[pl.BlockSpec((tm, d), lambda i: (i, 0))],
        scratch_shapes=[pltpu.VMEM((POOL_HALO + tm, d), F32), pltpu.VMEM((tm, d), F32)],
        args=[x, small, scale, poolw])[0]


def _fwd_ffn(h, small, wgu, wd, layer, after=(), target=None):
    t, d = h.shape
    tm = _token_tile(t)
    steps = t // tm
    fc = wgu.shape[-1]
    f = 2 * fc
    g_in, g_out = 4 * layer + 2, 4 * layer + 3
    with_loss = target is not None

    def body(h_ref, *refs):
        if with_loss:
            t_ref, sm_ref, wgu_ref, wd_ref, o_ref, gu_ref, ff_ref, n_ref, l_ref, acc_ref = refs
        else:
            sm_ref, wgu_ref, wd_ref, o_ref, gu_ref, ff_ref, n_ref = refs
        hv = h_ref[...]
        hh, _ = _rms(hv)
        n = (hh * sm_ref[g_in:g_in + 1, :]).astype(BF16)
        n_ref[...] = n
        ff = None
        for j in range(2):
            gate = _dot(n, wgu_ref[j])
            up = _dot(n, wgu_ref[2 + j])
            gu_ref[:, j * fc:(j + 1) * fc] = gate.astype(BF16)
            gu_ref[:, f + j * fc:f + (j + 1) * fc] = up.astype(BF16)
            act = (gate * jax.nn.sigmoid(gate) * up).astype(BF16)
            part = _dot(act, wd_ref[j * fc:(j + 1) * fc, :])
            ff = part if ff is None else ff + part
        ff_ref[...] = ff
        fh, _ = _rms(ff)
        out = hv + fh * sm_ref[g_out:g_out + 1, :]
        if not with_loss:
            o_ref[...] = out
            return
        i = pl.program_id(0)
        e = out - t_ref[...]
        o_ref[...] = e * (1.0 / d)

        @pl.when(i == 0)
        def _():
            acc_ref[...] = jnp.zeros_like(acc_ref)

        acc_ref[...] += _colsum(e * e)

        @pl.when(i == steps - 1)
        def _():
            l_ref[...] = jnp.full(l_ref.shape, 0.5 / d, F32) * jnp.sum(acc_ref[...])

    row = lambda i: (i, 0)
    out_shape = [jax.ShapeDtypeStruct((t, d), F32), jax.ShapeDtypeStruct((t, 2 * f), BF16),
                 jax.ShapeDtypeStruct((t, d), F32), jax.ShapeDtypeStruct((t, d), BF16)]
    out_specs = [pl.BlockSpec((tm, d), row), pl.BlockSpec((tm, 2 * f), row), pl.BlockSpec((tm, d), row),
                 pl.BlockSpec((tm, d), row)]
    weight_specs = [_resident(small.shape, lambda i: (0, 0)), _resident(wgu.shape, lambda i: (0, 0, 0)),
                    _resident(wd.shape, lambda i: (0, 0))]
    if with_loss:
        return _gridded(
            body, after, name=f"fwd_ffn{layer}_loss", grid=(steps,),
            out_shape=out_shape + [jax.ShapeDtypeStruct((8, 128), F32)],
            in_specs=[pl.BlockSpec((tm, d), row), pl.BlockSpec((tm, d), row)] + weight_specs,
            out_specs=out_specs + [pl.BlockSpec((8, 128), lambda i: (0, 0))],
            scratch_shapes=[pltpu.VMEM((1, d), F32)], args=[h, target, small, wgu, wd])
    return _gridded(
        body, after, name=f"fwd_ffn{layer}", grid=(steps,), out_shape=out_shape,
        in_specs=[pl.BlockSpec((tm, d), row)] + weight_specs, out_specs=out_specs, args=[h, small, wgu, wd])


def _fwd_conv(h, small, win, wout, after=()):
    t, d = h.shape
    tm = _token_tile(t)
    pc = win.shape[-1]

    def body(h_ref, sm_ref, win_ref, wout_ref, o_ref, proj_ref, y_ref, n_ref, pj_ref, uext_ref):
        i = pl.program_id(0)

        @pl.when(i == 0)
        def _():
            uext_ref[0:CONV_HALO, :] = jnp.zeros((CONV_HALO, d), F32)

        @pl.when(i > 0)
        def _():
            uext_ref[0:CONV_HALO, :] = uext_ref[tm:tm + CONV_HALO, :]

        hv = h_ref[...]
        hh, _ = _rms(hv)
        n = (hh * sm_ref[4:5, :]).astype(BF16)
        n_ref[...] = n
        for k in range(N_CHIPS):
            pj_ref[:, k * pc:(k + 1) * pc] = _dot(n, win_ref[k])
        proj_ref[...] = pj_ref[...].astype(BF16)
        uext_ref[CONV_HALO:, :] = pj_ref[:, d:2 * d] * pj_ref[:, 2 * d:]
        taps = [sm_ref[8 + j:9 + j, :] for j in range(3)]
        full = uext_ref[...]
        conv = (full[CONV_HALO:] * taps[2] + pltpu.roll(full, 1, 0)[CONV_HALO:] * taps[1]
                + pltpu.roll(full, 2, 0)[CONV_HALO:] * taps[0])
        y = _dot((pj_ref[:, 0:d] * conv).astype(BF16), wout_ref[...])
        y_ref[...] = y
        yh, _ = _rms(y)
        o_ref[...] = hv + yh * sm_ref[5:6, :]

    row = lambda i: (i, 0)
    return _gridded(
        body, after, name="fwd_conv", grid=(t // tm,),
        out_shape=[jax.ShapeDtypeStruct((t, d), F32), jax.ShapeDtypeStruct((t, 3 * d), BF16),
                   jax.ShapeDtypeStruct((t, d), F32), jax.ShapeDtypeStruct((t, d), BF16)],
        in_specs=[pl.BlockSpec((tm, d), row), _resident(small.shape, lambda i: (0, 0)),
                  _resident(win.shape, lambda i: (0, 0, 0)), _resident(wout.shape, lambda i: (0, 0))],
        out_specs=[pl.BlockSpec((tm, d), row), pl.BlockSpec((tm, 3 * d), row), pl.BlockSpec((tm, d), row),
                   pl.BlockSpec((tm, d), row)],
        scratch_shapes=[pltpu.VMEM((tm, 3 * d), F32), pltpu.VMEM((CONV_HALO + tm, d), F32)],
        args=[h, small, win, wout])


def _bwd_ffn(dh, h, ff, gu, small, wgu, wd, layer, after=()):
    t, d = h.shape
    tm = _token_tile(t, FFN_BWD_TOKEN_TILE)
    fc = wgu.shape[-1]
    f = 2 * fc
    g_in, g_out = 4 * layer + 2, 4 * layer + 3

    def body(dh_ref, h_ref, ff_ref, gu_ref, sm_ref, wgu_ref, wd_ref, o_ref, dgu_ref, dff_ref, act_ref, sg_ref):
        i = pl.program_id(0)

        @pl.when(i == 0)
        def _():
            sg_ref[...] = jnp.zeros_like(sg_ref)

        dy = dh_ref[...]
        fh, r3 = _rms(ff_ref[...])
        sg_ref[1:2, :] += _colsum(dy * fh)
        dff = _rms_bwd(dy, fh, r3, sm_ref[g_out:g_out + 1, :]).astype(BF16)
        dff_ref[...] = dff
        for j in range(2):
            dact = _dot_nt(dff, wd_ref[j * fc:(j + 1) * fc, :])
            gate = gu_ref[:, j * fc:(j + 1) * fc].astype(F32)
            up = gu_ref[:, f + j * fc:f + (j + 1) * fc].astype(F32)
            sig = jax.nn.sigmoid(gate)
            silu = gate * sig
            act_ref[:, j * fc:(j + 1) * fc] = (silu * up).astype(BF16)
            dgu_ref[:, j * fc:(j + 1) * fc] = (dact * up * (sig * (1.0 + gate * (1.0 - sig)))).astype(BF16)
            dgu_ref[:, f + j * fc:f + (j + 1) * fc] = (dact * silu).astype(BF16)
        dn = None
        for k in range(N_CHIPS):
            part = _dot_nt(dgu_ref[:, k * fc:(k + 1) * fc], wgu_ref[k])
            dn = part if dn is None else dn + part
        hh, r2 = _rms(h_ref[...])
        sg_ref[0:1, :] += _colsum(dn * hh)
        o_ref[...] = dy + _rms_bwd(dn, hh, r2, sm_ref[g_in:g_in + 1, :])

    row = lambda i: (i, 0)
    return _gridded(
        body, after, name=f"bwd_ffn{layer}", grid=(t // tm,),
        out_shape=[jax.ShapeDtypeStruct((t, d), F32), jax.ShapeDtypeStruct((t, 2 * f), BF16),
                   jax.ShapeDtypeStruct((t, d), BF16), jax.ShapeDtypeStruct((t, f), BF16),
                   jax.ShapeDtypeStruct((8, d), F32)],
        in_specs=[pl.BlockSpec((tm, d), row), pl.BlockSpec((tm, d), row), pl.BlockSpec((tm, d), row),
                  pl.BlockSpec((tm, 2 * f), row), _resident(small.shape, lambda i: (0, 0)),
                  _resident(wgu.shape, lambda i: (0, 0, 0)), _resident(wd.shape, lambda i: (0, 0))],
        out_specs=[pl.BlockSpec((tm, d), row), pl.BlockSpec((tm, 2 * f), row), pl.BlockSpec((tm, d), row),
                   pl.BlockSpec((tm, f), row), pl.BlockSpec((8, d), lambda i: (0, 0))],
        args=[dh, h, ff, gu, small, wgu, wd])


def _bwd_conv(dh, h, y, proj, small, win, wout, after=()):
    t, d = h.shape
    tm = _token_tile(t)
    steps = t // tm
    pc = win.shape[-1]
    halo_blocks = tm // 16

    def body(dh_ref, h_ref, y_ref, proj_ref, halo_ref, sm_ref, win_ref, wout_ref,
             o_ref, dproj_ref, dy_ref, bc_ref, sg_ref, uext_ref, dcext_ref, carry_ref):
        i = pl.program_id(0)
        tile = steps - 1 - i

        @pl.when(i == 0)
        def _():
            sg_ref[...] = jnp.zeros_like(sg_ref)
            carry_ref[...] = jnp.zeros_like(carry_ref)

        dy = dh_ref[...]
        yh, r1 = _rms(y_ref[...])
        sg_ref[1:2, :] += _colsum(dy * yh)
        dyv = _rms_bwd(dy, yh, r1, sm_ref[5:6, :]).astype(BF16)
        dy_ref[...] = dyv
        dbc = _dot_nt(dyv, wout_ref[...])
        b = proj_ref[:, 0:d].astype(F32)
        cg = proj_ref[:, d:2 * d].astype(F32)
        v = proj_ref[:, 2 * d:].astype(F32)
        halo = halo_ref[...].astype(F32)[16 - CONV_HALO:]
        uh = halo[:, d:2 * d] * halo[:, 2 * d:]
        uext_ref[0:CONV_HALO, :] = jnp.where(tile > 0, uh, jnp.zeros_like(uh))
        uext_ref[CONV_HALO:, :] = cg * v
        taps = [sm_ref[8 + j:9 + j, :] for j in range(3)]
        full = uext_ref[...]
        u0 = full[CONV_HALO:]
        u1 = pltpu.roll(full, 1, 0)[CONV_HALO:]
        u2 = pltpu.roll(full, 2, 0)[CONV_HALO:]
        conv = u0 * taps[2] + u1 * taps[1] + u2 * taps[0]
        bc_ref[...] = (b * conv).astype(BF16)
        dconv = dbc * b
        sg_ref[4:5, :] += _colsum(dconv * u0)
        sg_ref[3:4, :] += _colsum(dconv * u1)
        sg_ref[2:3, :] += _colsum(dconv * u2)
        dcext_ref[0:tm, :] = dconv
        dcext_ref[tm:, :] = carry_ref[...]
        carry_ref[...] = dconv[0:CONV_HALO]
        dfull = dcext_ref[...]
        n8 = tm + CONV_HALO
        du = (dfull[0:tm] * taps[2] + pltpu.roll(dfull, n8 - 1, 0)[0:tm] * taps[1]
              + pltpu.roll(dfull, n8 - 2, 0)[0:tm] * taps[0])
        dproj_ref[:, 0:d] = (dbc * conv).astype(BF16)
        dproj_ref[:, d:2 * d] = (du * v).astype(BF16)
        dproj_ref[:, 2 * d:] = (du * cg).astype(BF16)
        dn = None
        for k in range(N_CHIPS):
            part = _dot_nt(dproj_ref[:, k * pc:(k + 1) * pc], win_ref[k])
            dn = part if dn is None else dn + part
        hh, r0 = _rms(h_ref[...])
        sg_ref[0:1, :] += _colsum(dn * hh)
        o_ref[...] = dy + _rms_bwd(dn, hh, r0, sm_ref[4:5, :])

    rev = lambda i: (steps - 1 - i, 0)
    before = lambda i: (jnp.maximum((steps - 1 - i) * halo_blocks - 1, 0), 0)
    return _gridded(
        body, after, name="bwd_conv", grid=(steps,),
        out_shape=[jax.ShapeDtypeStruct((t, d), F32), jax.ShapeDtypeStruct((t, 3 * d), BF16),
                   jax.ShapeDtypeStruct((t, d), BF16), jax.ShapeDtypeStruct((t, d), BF16),
                   jax.ShapeDtypeStruct((8, d), F32)],
        in_specs=[pl.BlockSpec((tm, d), rev), pl.BlockSpec((tm, d), rev), pl.BlockSpec((tm, d), rev),
                  pl.BlockSpec((tm, 3 * d), rev), pl.BlockSpec((16, 3 * d), before),
                  _resident(small.shape, lambda i: (0, 0)), _resident(win.shape, lambda i: (0, 0, 0)),
                  _resident(wout.shape, lambda i: (0, 0))],
        out_specs=[pl.BlockSpec((tm, d), rev), pl.BlockSpec((tm, 3 * d), rev), pl.BlockSpec((tm, d), rev),
                   pl.BlockSpec((tm, d), rev), pl.BlockSpec((8, d), lambda i: (0, 0))],
        scratch_shapes=[pltpu.VMEM((CONV_HALO + tm, d), F32), pltpu.VMEM((tm + CONV_HALO, d), F32),
                        pltpu.VMEM((CONV_HALO, d), F32)],
        args=[dh, h, y, proj, proj, small, win, wout])


def _bwd_pool(dh, x, small, scale, poolw, after=()):
    t, d = x.shape
    tm = _token_tile(t)
    steps = t // tm
    ng = len(POOL_WINDOWS)
    gw = d // ng
    halo_blocks = tm // POOL_HALO

    def body(dh_ref, x_ref, halo_ref, sm_ref, sc_ref, w_ref, o_ref, dw_ref, sg_ref,
             ext_ref, mix_ref, mm_ref, pb_ref, qext_ref, dhn_ref, carry_ref):
        i = pl.program_id(0)
        tile = steps - 1 - i

        @pl.when(i == 0)
        def _():
            sg_ref[...] = jnp.zeros_like(sg_ref)
            dw_ref[...] = jnp.zeros_like(dw_ref)
            carry_ref[...] = jnp.zeros_like(carry_ref)

        g0 = sm_ref[0:1, :]
        xv = x_ref[...]
        xh, r0 = _rms(xv)
        hx, _ = _rms(halo_ref[...])
        ext_ref[0:POOL_HALO, :] = jnp.where(tile > 0, hx * g0, jnp.zeros_like(hx))
        ext_ref[POOL_HALO:, :] = xh * g0
        for g in range(ng):
            pooled = _pool_windows(ext_ref, g, gw, tm, tile * tm)
            cols = slice(g * gw, (g + 1) * gw)
            pb = pooled.astype(BF16)
            pb_ref[:, cols] = pb
            mm = _dot(pb, w_ref[g])
            mm_ref[:, cols] = mm
            mix_ref[:, cols] = mm * sc_ref[:, cols]
        dy = dh_ref[...]
        mh, r1 = _rms(mix_ref[...])
        sg_ref[1:2, :] += _colsum(dy * mh)
        dmix = _rms_bwd(dy, mh, r1, sm_ref[1:2, :])
        sg_ref[2:3, :] += _colsum(dmix * mm_ref[...])
        mix_ref[...] = dmix * sc_ref[...]
        n16 = tm + POOL_HALO
        for g in range(ng):
            w = POOL_WINDOWS[g]
            cols = slice(g * gw, (g + 1) * gw)
            dmm = mix_ref[:, cols].astype(BF16)
            dpooled = _dot_nt(dmm, w_ref[g])
            dw_ref[g] += _dot_tn(pb_ref[:, cols], dmm)
            trow = tile * tm + lax.broadcasted_iota(jnp.int32, (tm, 1), 0)
            q = dpooled / jnp.minimum(trow + 1, w).astype(F32)
            qext_ref[0:tm, cols] = q
            qext_ref[tm:, cols] = carry_ref[:, cols]
            carry_ref[:, cols] = q[0:POOL_HALO]
            p, k = qext_ref[:, cols], 1
            while k < w:
                p = p + pltpu.roll(p, n16 - k, 0)
                k *= 2
            dhn_ref[:, cols] = p[0:tm] - dpooled
        dhn = dhn_ref[...]
        sg_ref[0:1, :] += _colsum(dhn * xh)
        o_ref[...] = dy + _rms_bwd(dhn, xh, r0, g0)

    rev = lambda i: (steps - 1 - i, 0)
    before = lambda i: (jnp.maximum((steps - 1 - i) * halo_blocks - 1, 0), 0)
    return _gridded(
        body, after, name="bwd_pool", grid=(steps,),
        out_shape=[jax.ShapeDtypeStruct((t, d), F32), jax.ShapeDtypeStruct((ng, gw, gw), F32),
                   jax.ShapeDtypeStruct((8, d), F32)],
        in_specs=[pl.BlockSpec((tm, d), rev), pl.BlockSpec((tm, d), rev), pl.BlockSpec((POOL_HALO, d), before),
                  _resident(small.shape, lambda i: (0, 0)), _resident(scale.shape, lambda i: (0, 0)),
                  _resident(poolw.shape, lambda i: (0, 0, 0))],
        out_specs=[pl.BlockSpec((tm, d), rev), pl.BlockSpec((ng, gw, gw), lambda i: (0, 0, 0)),
                   pl.BlockSpec((8, d), lambda i: (0, 0))],
        scratch_shapes=[pltpu.VMEM((POOL_HALO + tm, d), F32), pltpu.VMEM((tm, d), F32), pltpu.VMEM((tm, d), F32),
                        pltpu.VMEM((tm, d), BF16), pltpu.VMEM((tm + POOL_HALO, d), F32), pltpu.VMEM((tm, d), F32),
                        pltpu.VMEM((POOL_HALO, d), F32)],
        args=[dh, x, x, small, scale, poolw])


def _weight_grad(a, b, bm, bn, half_on, name):
    t, m = a.shape
    _, n = b.shape
    if half_on == "a":
        a_cols, b_cols = 2 * bm, bn
    else:
        a_cols, b_cols = bm, 2 * bn
    steps = max(m // a_cols, n // b_cols)

    def spec(cols, total):
        if cols == total:
            return _resident((t, cols), lambda p, j: (0, 0))
        return pl.BlockSpec((t, cols), lambda p, j: (0, j))

    def tile(a_ref, b_ref, half):
        if half_on == "a":
            return _dot_tn(a_ref[:, half * bm:(half + 1) * bm], b_ref[...])
        return _dot_tn(a_ref[...], b_ref[:, half * bn:(half + 1) * bn])

    def body(a_ref, b_ref, parts_ref, land_ref, acc_ref, stage_ref, got_ref, send_sems, recv_sems, got_sem):
        p, j = pl.program_id(0), pl.program_id(1)
        x, y, c, _ = _place()
        half = jnp.where(p == 0, 1 - c, c)

        def send(jj):
            return _remote(stage_ref.at[jj % 2], land_ref.at[jj], send_sems.at[jj], recv_sems.at[jj], (x, y, 1 - c))

        def fetch():
            return pltpu.make_async_copy(land_ref.at[j], got_ref, got_sem)

        @pl.when(p == 1)
        def _():
            @pl.when(j == 0)
            def _():
                for jj in range(max(steps - 2, 0), steps):
                    send(jj).wait_send()

            send(j).wait_recv()
            fetch().start()

        for hv in range(2):
            @pl.when(half == hv)
            def _():
                acc_ref[...] = tile(a_ref, b_ref, hv)

        @pl.when(p == 0)
        def _():
            @pl.when(j >= 2)
            def _():
                send(j - 2).wait_send()

            stage_ref[j % 2] = acc_ref[...].astype(BF16)
            send(j).start()

        @pl.when(p == 1)
        def _():
            fetch().wait()
            parts_ref[...] = (acc_ref[...] + got_ref[...].astype(F32)).astype(BF16)

    return _gridded(
        body, (), name=name, grid=(2, steps),
        out_shape=[jax.ShapeDtypeStruct((steps, bm, bn), BF16), jax.ShapeDtypeStruct((steps, bm, bn), BF16)],
        in_specs=[spec(a_cols, m), spec(b_cols, n)],
        out_specs=[pl.BlockSpec((None, bm, bn), lambda p, j: (p * j, 0, 0)), ANY],
        scratch_shapes=[pltpu.VMEM((bm, bn), F32), pltpu.VMEM((2, bm, bn), BF16), pltpu.VMEM((bm, bn), BF16),
                        DMA((steps,)), DMA((steps,)), DMA],
        args=[a, b])[0]


def _cast_layer(w, layer, name):
    _, r, c = w.shape
    rb = _row_block(r, c * (4 + 2) * 2)

    def body(w_ref, o_ref):
        o_ref[...] = w_ref[...].astype(BF16)

    return pl.pallas_call(
        body, name=name, grid=(r // rb,), out_shape=jax.ShapeDtypeStruct((r, c), BF16),
        in_specs=[pl.BlockSpec((None, rb, c), lambda i: (layer, i, 0))], out_specs=pl.BlockSpec((rb, c), lambda i: (i, 0)),
        compiler_params=pltpu.CompilerParams(dimension_semantics=("parallel",), vmem_limit_bytes=VMEM_LIMIT),
    )(w)


def _adamw_math(w, g, m, v):
    bc1 = 1.0 - ADAM_B1 ** ADAM_STEP
    bc2 = 1.0 - ADAM_B2 ** ADAM_STEP
    nm = ADAM_B1 * m + (1.0 - ADAM_B1) * g
    nv = ADAM_B2 * v + (1.0 - ADAM_B2) * (g * g)
    return -ADAM_LR * ((nm / bc1) / (jnp.sqrt(nv / bc2) + ADAM_EPS) + ADAM_WD * w), nm, nv


def _adamw_small(small_sum, where, gains, taps, scale):
    dq = gains[0].shape[-1]
    d = small_sum.shape[-1]

    def body(where_ref, mine_ref, all_ref, gw, gm, gv, tw, tm_, tv, sw, sm, sv,
             gg, gd, gnm, gnv, tg, td, tnm, tnv, sg, sd, snm, snv):
        for layer in range(gw.shape[0]):
            g = mine_ref[4 * layer:4 * layer + 4, :]
            gg[layer] = g
            gd[layer], gnm[layer], gnv[layer] = _adamw_math(gw[layer], g, gm[layer], gv[layer])
        g = mine_ref[8:8 + tw.shape[1], :]
        tg[0] = g
        td[0], tnm[0], tnv[0] = _adamw_math(tw[0], g, tm_[0], tv[0])
        g = all_ref[11:12, :]
        sg[...] = g
        sd[...], snm[...], snv[...] = _adamw_math(sw[...], g, sm[...], sv[...])

    full = lambda a: pl.BlockSpec(a.shape, lambda i, where_ref: (0,) * a.ndim)
    params = [*gains, *taps, *scale]
    outs = [gains[0]] * 4 + [taps[0]] * 4 + [scale[0]] * 4
    res = pl.pallas_call(
        body, name="adamw_small", out_shape=[jax.ShapeDtypeStruct(a.shape, F32) for a in outs],
        grid_spec=pltpu.PrefetchScalarGridSpec(
            num_scalar_prefetch=1, grid=(1,),
            in_specs=[pl.BlockSpec((16, dq), lambda i, where_ref: (0, where_ref[0])), pl.BlockSpec((16, d), lambda i, where_ref: (0, 0)),
                      *[full(a) for a in params]],
            out_specs=[full(a) for a in outs]),
    )(where, small_sum, small_sum, *params)
    return tuple(res[0:4]), tuple(res[4:8]), tuple(res[8:12])


def _adamw(w, g, m, v, name):
    r, c = w.shape
    rb = _row_block(r, c * (8 * 4 * 2 + 4 * 4))

    def body(w_ref, g_ref, m_ref, v_ref, d_ref, nm_ref, nv_ref, go_ref):
        gv = g_ref[...]
        go_ref[...] = gv
        d_ref[...], nm_ref[...], nv_ref[...] = _adamw_math(w_ref[...], gv, m_ref[...], v_ref[...])

    spec = pl.BlockSpec((rb, c), lambda i: (i, 0))
    return pl.pallas_call(
        body, name=name, grid=(r // rb,), out_shape=[jax.ShapeDtypeStruct((r, c), F32)] * 4,
        in_specs=[spec] * 4, out_specs=[spec] * 4,
        compiler_params=pltpu.CompilerParams(dimension_semantics=("parallel",), vmem_limit_bytes=VMEM_LIMIT),
    )(w, g, m, v)


def kernel(x, norm_gains, pool_w, pool_scale, conv_in_w, conv_w, conv_out_w, ffn_gate_up_w, ffn_down_w, loss_target, m_norm_gains, m_pool_w, m_pool_scale, m_conv_in_w, m_conv_w, m_conv_out_w, m_ffn_gate_up_w, m_ffn_down_w, v_norm_gains, v_pool_w, v_pool_scale, v_conv_in_w, v_conv_w, v_conv_out_w, v_ffn_gate_up_w, v_ffn_down_w):
    _, t, d = x.shape
    dq = d // N_CHIPS
    ng = len(POOL_WINDOWS)
    gw = d // ng
    fq = ffn_down_w.shape[1]
    f = N_CHIPS * fq
    fc = f // 2
    core = lax.axis_index("c")
    chip = 2 * lax.axis_index("x") + lax.axis_index("y")
    core_arr = jnp.reshape(core, (1,)).astype(jnp.int32)
    where_arr = jnp.stack([chip, core]).astype(jnp.int32)
    x2, target = x[0], loss_target[0]

    small_loc = jnp.concatenate(
        [norm_gains.reshape(8, dq), conv_w[0], jnp.zeros((5, dq), F32)], axis=0).reshape(1, 2, 8, dq)
    pool_loc = pool_w.astype(BF16).reshape(1, 2, ng // 2 * (gw // N_CHIPS), gw)
    wgu_loc = [_cast_layer(ffn_gate_up_w, 0, "cast_gate_up0").reshape(1, 2, d // 2, fc),
               ffn_gate_up_w[1:2].astype(BF16).reshape(1, 2, d // 2, fc)]
    wd_loc = [_cast_layer(ffn_down_w, 0, "cast_down0").reshape(1, 2, fq // 2, d),
              ffn_down_w[1:2].astype(BF16).reshape(1, 2, fq // 2, d)]
    win_loc = conv_in_w.astype(BF16).reshape(1, 2, d // 2, -1)
    wout_loc = conv_out_w.astype(BF16).reshape(1, 2, dq // 2, d)

    def ffn_weights(wgu_f, wd_f):
        return wgu_f.reshape(N_CHIPS, d, fc), wd_f.reshape(f, d)

    ag0 = _SplitGather([(pool_loc, 0), (small_loc, 0), (wgu_loc[0], 0), (wd_loc[0], 0)])
    ag0.start("ag_start_layer0")
    ag1 = _SplitGather([(win_loc, 0), (wout_loc, 0), (wgu_loc[1], 0), (wd_loc[1], 0)])
    ag1.start("ag_start_layer1", [ag0.token])
    pool_f, small_f = _pass_on(ag0.wait([0, 1], ag1.token, "ag_wait_first"), "ag_pass_first")
    poolw = pool_f.reshape(N_CHIPS, ng, gw // N_CHIPS, gw).transpose(1, 0, 2, 3).reshape(ng, gw, gw)
    small = small_f.transpose(1, 2, 0, 3).reshape(16, d)
    h1 = _fwd_pool(x2, small, pool_scale, poolw)
    wgu0, wd0 = ffn_weights(*_pass_on(ag0.wait([2, 3], h1, "ag_wait_ffn0"), "ag_pass_ffn0"))
    h2, gu0, ff0, n0 = _fwd_ffn(h1, small, wgu0, wd0, 0)
    win_f, wout_f = _pass_on(ag1.wait([0, 1], h2, "ag_wait_conv"), "ag_pass_conv")
    win_f, wout_f = win_f.reshape(N_CHIPS, d, -1), wout_f.reshape(d, d)
    h3, proj, y, nc = _fwd_conv(h2, small, win_f, wout_f)
    wgu1, wd1 = ffn_weights(*_pass_on(ag1.wait([2, 3], h3, "ag_wait_ffn1"), "ag_pass_ffn1"))
    dh4, gu1, ff1, n1, loss_blk = _fwd_ffn(h3, small, wgu1, wd1, 1, target=target)

    dh3, dgu1, dff1, act1, sg_f1 = _bwd_ffn(dh4, h3, ff1, gu1, small, wgu1, wd1, 1)
    parts_d1 = _weight_grad(act1, dff1, fc, d // 2, "b", "dw_down1")
    parts_gu1 = _weight_grad(n1, dgu1, d // 2, fc, "a", "dw_gate_up1")
    ex_ffn1 = _SplitExchange([parts_d1.reshape(N_CHIPS, fq, d // 2), parts_gu1])
    started = ex_ffn1.start("rs_start_ffn1")
    dh2, dproj, dyv, bcv, sg_c = _bwd_conv(dh3, h2, y, proj, small, win_f, wout_f, [started])
    parts_in = _weight_grad(nc, dproj, d // 2, 3 * d // N_CHIPS, "a", "dw_conv_in")
    parts_out = _weight_grad(bcv, dyv, dq // 2, d, "a", "dw_conv_out")
    ex_conv = _SplitExchange([parts_in, parts_out])
    started = ex_conv.start("rs_start_conv")
    dh1, dgu0, dff0, act0, sg_f0 = _bwd_ffn(dh2, h1, ff0, gu0, small, wgu0, wd0, 0, [started])
    parts_d0 = _weight_grad(act0, dff0, fc, d // 2, "b", "dw_down0")
    parts_gu0 = _weight_grad(n0, dgu0, d // 2, fc, "a", "dw_gate_up0")
    ex_ffn0 = _SplitExchange([parts_d0.reshape(N_CHIPS, fq, d // 2), parts_gu0])
    started = ex_ffn0.start("rs_start_ffn0")
    grad_x, dpool, sg_p = _bwd_pool(dh1, x2, small, pool_scale, poolw, [started])
    g_pool = dpool.astype(BF16).reshape(2, ng // 2, N_CHIPS, gw // N_CHIPS, gw).transpose(2, 0, 1, 3, 4).reshape(
        N_CHIPS, 2, ng // 2 * (gw // N_CHIPS), gw)
    small_g = jnp.concatenate(
        [sg_p[0:2], sg_f0[0:2], sg_c[0:2], sg_f1[0:2], sg_c[2:5], sg_p[2:3],
         jnp.broadcast_to(loss_blk[0:1, 0:1], (1, d)), jnp.zeros((3, d), F32)], axis=0)
    ex_pool = _SplitExchange([_add_sibling(g_pool, _sibling_exchange(g_pool, "rs_sibling_pool"), core_arr)])
    started = ex_pool.start("rs_start_pool")

    def update(w, g, m, v, name):
        if w.size * 4 * 8 <= STREAM_BUDGET // 4:
            def body(w_ref, g_ref, m_ref, v_ref, go_ref, d_ref, nm_ref, nv_ref):
                go_ref[...] = g_ref[...]
                d_ref[...], nm_ref[...], nv_ref[...] = _adamw_math(w_ref[...], g_ref[...], m_ref[...], v_ref[...])

            return tuple(pl.pallas_call(body, name="adamw_" + name, out_shape=[jax.ShapeDtypeStruct(w.shape, F32)] * 4)(
                w, g.reshape(w.shape), m, v))
        flat = (-1, w.shape[-1])
        dl, m2, v2, g2 = _adamw(w.reshape(flat), g.reshape(flat), m.reshape(flat), v.reshape(flat), "adamw_" + name)
        return tuple(o.reshape(w.shape) for o in (g2, dl, m2, v2))

    (parts_d1, parts_gu1), (recv_d1, recv_gu1) = ex_ffn1.wait([started], "rs_wait_ffn1")
    (parts_in, parts_out), (recv_in, recv_out) = ex_conv.wait([started], "rs_wait_conv")
    gs_gu = _add_chips(parts_gu1, recv_gu1, where_arr, 1, 2)
    gs_d = _add_chips(parts_d1, recv_d1, where_arr, 1, 2, col_half=True)
    gs_in = _add_chips(parts_in, recv_in, where_arr)
    gs_out = _add_chips(parts_out, recv_out, where_arr)
    full_in, full_out = _sibling_share([gs_in, gs_out], [False, False], "rs_share_conv")
    up_in = update(conv_in_w, full_in.reshape(1, d, -1), m_conv_in_w, v_conv_in_w, "conv_in")
    up_out = update(conv_out_w, full_out.reshape(1, dq, d), m_conv_out_w, v_conv_out_w, "conv_out")
    done_first = [up_in[1], up_out[1], gs_gu, gs_d]
    (parts_d0, parts_gu0), (recv_d0, recv_gu0) = ex_ffn0.wait(done_first, "rs_wait_ffn0")
    (parts_p,), (recv_p,) = ex_pool.wait(done_first, "rs_wait_pool")
    ex_small = _SplitExchange([small_g], _SmallGather(small_g))
    started = ex_small.start("rs_start_small", [recv_gu0])
    gs_gu = _add_chips(parts_gu0, recv_gu0, where_arr, 0, 2, gs_gu)
    gs_d = _add_chips(parts_d0, recv_d0, where_arr, 0, 2, gs_d, col_half=True)
    gs_pool = _add_chips(parts_p, recv_p, where_arr)
    full_gu, full_d, full_pool = _sibling_share([gs_gu, gs_d, gs_pool], [False, True, False], "rs_share_ffn", [started])
    up_gu = update(ffn_gate_up_w, full_gu.reshape(2, d, fc), m_ffn_gate_up_w, v_ffn_gate_up_w, "gate_up")
    up_d = update(ffn_down_w, full_d.reshape(2, fq, d), m_ffn_down_w, v_ffn_down_w, "down")
    (small_own,), (small_all,) = ex_small.wait([up_gu[1], up_d[1]], "rs_wait_small")
    small_sum = _sum_small(small_all, small_own, (2 * chip + core).reshape(1).astype(jnp.int32))
    loss = small_sum[12, 0]
    up_gains, up_taps, up_scale = _adamw_small(
        small_sum, where_arr, (norm_gains, m_norm_gains, v_norm_gains), (conv_w, m_conv_w, v_conv_w),
        (pool_scale, m_pool_scale, v_pool_scale))

    ups = [
        up_gains,
        update(pool_w, full_pool.reshape(1, ng, gw // N_CHIPS, gw), m_pool_w, v_pool_w, "pool_w"),
        up_scale,
        up_in,
        up_taps,
        up_out,
        up_gu,
        up_d,
    ]
    grads_out, deltas, new_ms, new_vs = (list(col) for col in zip(*ups))
    return (loss, grad_x[None], *grads_out, *deltas, *new_ms, *new_vs)
```

```python
import jax
import jax.numpy as jnp
from jax import lax
from jax.experimental import pallas as pl
from jax.experimental.pallas import tpu as pltpu

RMS_EPS = 1e-6
POOL_WINDOWS = (2, 4, 8, 16)
POOL_HALO = 16
CONV_HALO = 8
N_CHIPS = 4
N_DEV = 8
ADAM_LR = 0.001
ADAM_B1 = 0.9
ADAM_B2 = 0.999
ADAM_EPS = 1e-08
ADAM_WD = 0.01
ADAM_STEP = 10
VMEM_LIMIT = 56 * 2**20
STREAM_BUDGET = 24 * 2**20
MESH = pl.DeviceIdType.MESH
ANY = pl.BlockSpec(memory_space=pl.ANY)
DMA = pltpu.SemaphoreType.DMA
BF16 = jnp.bfloat16
F32 = jnp.float32


TOKEN_TILE = 512
FFN_BWD_TOKEN_TILE = 256


def _token_tile(t, rows=TOKEN_TILE):
    return min(rows, t)


def _rms(x):
    r = lax.rsqrt(jnp.mean(x * x, axis=-1, keepdims=True) + RMS_EPS)
    return x * r, r


def _rms_bwd(dy, xh, r, g):
    a = dy * g
    return r * (a - xh * jnp.mean(a * xh, axis=-1, keepdims=True))


def _dot(a, b):
    return jnp.dot(a, b, preferred_element_type=F32)


def _dot_nt(a, b):
    return lax.dot_general(a, b, (((1,), (1,)), ((), ())), preferred_element_type=F32)


def _dot_tn(a, b):
    return lax.dot_general(a, b, (((0,), (0,)), ((), ())), preferred_element_type=F32)


def _colsum(a):
    return jnp.sum(a, axis=0, keepdims=True)


def _resident(block, index_map):
    return pl.BlockSpec(block, index_map, pipeline_mode=pl.Buffered(1))


def _row_block(r, row_bytes):
    best = None
    for rb in range(16, r + 1, 16):
        if r % rb == 0 and rb * row_bytes <= STREAM_BUDGET:
            best = rb
    return best if best is not None else r


def _place():
    x, y, c = lax.axis_index("x"), lax.axis_index("y"), lax.axis_index("c")
    return x, y, c, 2 * x + y


def _dev(chip, core):
    return (chip // 2, chip % 2, core)


def _remote(src, dst, send_sem, recv_sem, device):
    return pltpu.make_async_remote_copy(src_ref=src, dst_ref=dst, send_sem=send_sem, recv_sem=recv_sem,
                                        device_id=device, device_id_type=MESH)


class _Gather:
    def __init__(self, shards):
        self.args = [s for s, _ in shards]
        self.layers = [l for _, l in shards]
        self.out_shape = [jax.ShapeDtypeStruct((N_CHIPS,) + s.shape[1:], s.dtype) for s in self.args]

    def _own(self, loc, out, sems, a):
        x, y, c, k = _place()
        return _remote(loc[a].at[self.layers[a]], out[a].at[k], sems[0].at[a], sems[1].at[a], (x, y, 1 - c))

    def _ici(self, loc, out, sems, a, m, arrival):
        x, y, c, k = _place()
        dst = out[a].at[k ^ m, c] if arrival else out[a].at[k, c]
        return _remote(loc[a].at[self.layers[a], c], dst, sems[2].at[a, m - 1], sems[3].at[a, m - 1], _dev(k ^ m, c))

    def start(self, loc, out, sems):
        for a in range(len(self.args)):
            for m in range(1, N_CHIPS):
                self._ici(loc, out, sems, a, m, False).start()
            self._own(loc, out, sems, a).start()


class _SmallGather:
    PEERS = N_DEV - 1

    def __init__(self, small):
        self.args = [small]
        self.out_shape = [jax.ShapeDtypeStruct((N_DEV,) + small.shape, small.dtype)]

    def _copy(self, sm, land, sems, m, arrival):
        x, y, c, k = _place()
        me = 2 * k + c
        peer = me ^ m
        return _remote(sm[0], land[0].at[peer if arrival else me], sems[0].at[0, m - 1], sems[1].at[0, m - 1],
                       (peer // 4, (peer // 2) % 2, peer % 2))

    def start(self, sm, land, sems):
        for m in range(1, N_DEV):
            self._copy(sm, land, sems, m, False).start()

    def finish(self, sm, land, sems):
        for m in range(1, N_DEV):
            self._copy(sm, land, sems, m, True).wait_recv()
        for m in range(1, N_DEV):
            self._copy(sm, land, sems, m, False).wait_send()


class _ChipExchange:
    PEERS = N_CHIPS - 1

    def __init__(self, parts):
        self.args = list(parts)
        self.out_shape = [jax.ShapeDtypeStruct((N_CHIPS - 1,) + p.shape[1:], p.dtype) for p in parts]

    def _copy(self, p, land, sems, a, m):
        x, y, c, k = _place()
        return _remote(p[a].at[k ^ m], land[a].at[m - 1], sems[0].at[a, m - 1], sems[1].at[a, m - 1], _dev(k ^ m, c))

    def start(self, p, land, sems):
        for a in range(len(self.args)):
            for m in range(1, N_CHIPS):
                self._copy(p, land, sems, a, m).start()

    def finish(self, p, land, sems):
        for a in range(len(self.args)):
            for m in range(1, N_CHIPS):
                self._copy(p, land, sems, a, m).wait_recv()
        for a in range(len(self.args)):
            for m in range(1, N_CHIPS):
                self._copy(p, land, sems, a, m).wait_send()


def _gridded(body, after, *, name, grid, in_specs, out_specs, out_shape, args, scratch_shapes=()):
    ni, na = len(in_specs), len(after)

    def full(*refs):
        body(*refs[:ni], *refs[ni + na:])

    return list(pl.pallas_call(
        full, name=name, grid=grid, in_specs=[*in_specs, *[ANY] * na], out_specs=list(out_specs),
        out_shape=list(out_shape), scratch_shapes=list(scratch_shapes),
        compiler_params=pltpu.CompilerParams(dimension_semantics=("arbitrary",) * len(grid), vmem_limit_bytes=VMEM_LIMIT),
    )(*args, *after))


HBM = pl.BlockSpec(memory_space=pltpu.HBM)
SEM = pl.BlockSpec(memory_space=pltpu.SEMAPHORE)
DATAFLOW = pltpu.SideEffectType.DATAFLOW_SIDE_EFFECTING


class _SplitGather:
    PER_ARRAY = 8

    def __init__(self, shards):
        self.plan = _Gather(shards)
        self.n = len(shards)

    @staticmethod
    def _tables(sems_of):
        class Table:
            def __init__(self, pick):
                self.pick = pick

            @property
            def at(self):
                return self

            def __getitem__(self, idx):
                return self.pick(idx)

        return [Table(lambda a: sems_of[a][0]), Table(lambda a: sems_of[a][1]),
                Table(lambda am: sems_of[am[0]][2 + am[1]]), Table(lambda am: sems_of[am[0]][5 + am[1]])]

    def start(self, name, after=()):
        n, plan, per, na = self.n, self.plan, self.PER_ARRAY, len(after)

        def body(*refs):
            loc, land = refs[:n], refs[n:2 * n]
            sems_of = {a: refs[2 * n + na + per * a:2 * n + na + per * (a + 1)] for a in range(n)}
            plan.start(loc, land, self._tables(sems_of))
            refs[-1][...] = jnp.zeros_like(refs[-1])

        lands = [pltpu.with_memory_space_constraint(lax.empty(o.shape, o.dtype), pltpu.HBM) for o in plan.out_shape]
        locs = [pltpu.with_memory_space_constraint(a, pltpu.HBM) for a in plan.args]
        res = pl.pallas_call(
            body, name=name,
            out_shape=[*[DMA(())] * (per * n),
                       *[pltpu.HBM(o.shape, o.dtype) for o in plan.out_shape],
                       jax.ShapeDtypeStruct((8, 128), F32)],
            in_specs=[HBM] * (2 * n) + [pl.BlockSpec(memory_space=pl.ANY)] * na,
            out_specs=[SEM] * (per * n) + [HBM] * n + [pl.BlockSpec(memory_space=pltpu.VMEM)],
            input_output_aliases={n + i: per * n + i for i in range(n)},
            compiler_params=pltpu.CompilerParams(has_side_effects=DATAFLOW),
        )(*locs, *lands, *after)
        self.sems = {a: list(res[per * a:per * (a + 1)]) for a in range(n)}
        self.locs = locs
        self.lands = list(res[per * n:per * n + n])
        self.token = res[-1]

    def wait(self, idxs, after, name):
        plan, g, per = self.plan, len(idxs), self.PER_ARRAY

        def body(*refs):
            loc = {a: refs[j] for j, a in enumerate(idxs)}
            land = {a: refs[g + j] for j, a in enumerate(idxs)}
            sems = self._tables({a: refs[2 * g + per * j:2 * g + per * (j + 1)] for j, a in enumerate(idxs)})
            for a in idxs:
                for m in range(1, N_CHIPS):
                    plan._ici(loc, land, sems, a, m, True).wait_recv()
                    plan._ici(loc, land, sems, a, m, False).wait_send()
                plan._own(loc, land, sems, a).wait_recv()
                plan._own(loc, land, sems, a).wait_send()

        res = pl.pallas_call(
            body, name=name,
            out_shape=[pltpu.HBM(self.lands[a].shape, self.lands[a].dtype) for a in idxs],
            in_specs=[HBM] * (2 * g) + [SEM] * (per * g) + [pl.BlockSpec(memory_space=pl.ANY)], out_specs=[HBM] * g,
            input_output_aliases={g + j: j for j in range(g)},
            compiler_params=pltpu.CompilerParams(has_side_effects=DATAFLOW),
        )(*[self.locs[a] for a in idxs], *[self.lands[a] for a in idxs],
          *[s for a in idxs for s in self.sems[a]], after)
        return list(res)


class _SplitExchange:
    def __init__(self, parts, plan=None):
        self.plan = _ChipExchange(parts) if plan is None else plan
        self.n = len(parts)
        self.PER_ARRAY = 2 * self.plan.PEERS

    def _tables(self, sems_of):
        class Table:
            def __init__(self, pick):
                self.pick = pick

            @property
            def at(self):
                return self

            def __getitem__(self, am):
                return self.pick(am)

        peers = self.plan.PEERS
        return [Table(lambda am: sems_of[am[0]][am[1]]), Table(lambda am: sems_of[am[0]][peers + am[1]])]

    def start(self, name, after=()):
        n, plan, per, na = self.n, self.plan, self.PER_ARRAY, len(after)

        def body(*refs):
            p, land = refs[:n], refs[n:2 * n]
            sems_of = {a: refs[2 * n + na + per * a:2 * n + na + per * (a + 1)] for a in range(n)}
            plan.start(p, land, self._tables(sems_of))
            refs[-1][...] = jnp.zeros_like(refs[-1])

        lands = [pltpu.with_memory_space_constraint(lax.empty(o.shape, o.dtype), pltpu.HBM) for o in plan.out_shape]
        parts = [pltpu.with_memory_space_constraint(a, pltpu.HBM) for a in plan.args]
        res = pl.pallas_call(
            body, name=name,
            out_shape=[*[DMA(())] * (per * n),
                       *[pltpu.HBM(a.shape, a.dtype) for a in plan.args],
                       *[pltpu.HBM(o.shape, o.dtype) for o in plan.out_shape],
                       jax.ShapeDtypeStruct((8, 128), F32)],
            in_specs=[HBM] * (2 * n) + [pl.BlockSpec(memory_space=pl.ANY)] * na,
            out_specs=[SEM] * (per * n) + [HBM] * (2 * n) + [pl.BlockSpec(memory_space=pltpu.VMEM)],
            input_output_aliases={i: per * n + i for i in range(2 * n)},
            compiler_params=pltpu.CompilerParams(has_side_effects=DATAFLOW),
        )(*parts, *lands, *after)
        self.sems = list(res[:per * n])
        self.parts = list(res[per * n:per * n + n])
        self.lands = list(res[per * n + n:per * n + 2 * n])
        return res[-1]

    def wait(self, after, name):
        n, plan, per = self.n, self.plan, self.PER_ARRAY

        def body(*refs):
            p, land = refs[:n], refs[n:2 * n]
            sems_of = {a: refs[2 * n + per * a:2 * n + per * (a + 1)] for a in range(n)}
            plan.finish(p, land, self._tables(sems_of))

        res = pl.pallas_call(
            body, name=name,
            out_shape=[*[pltpu.HBM(a.shape, a.dtype) for a in self.parts], *[pltpu.HBM(a.shape, a.dtype) for a in self.lands]],
            in_specs=[HBM] * (2 * n) + [SEM] * (per * n) + [pl.BlockSpec(memory_space=pl.ANY)] * len(after),
            out_specs=[HBM] * (2 * n), input_output_aliases={i: i for i in range(2 * n)},
            compiler_params=pltpu.CompilerParams(has_side_effects=DATAFLOW),
        )(*self.parts, *self.lands, *self.sems, *after)
        return list(res[:n]), list(res[n:])


PASS_ON_BARRIER = 1


def _sibling_barrier():
    x, y, c, _ = _place()
    barrier = pltpu.get_barrier_semaphore()
    pl.semaphore_signal(barrier, inc=1, device_id=(x, y, 1 - c), device_id_type=MESH)
    pl.semaphore_wait(barrier, 1)


def _pass_on(lands, name):
    n = len(lands)

    def body(*refs):
        out = refs[n:2 * n]
        send_sems, recv_sems = refs[2 * n:]
        x, y, c, k = _place()
        _sibling_barrier()
        cps = []
        for a in range(n):
            for m in range(1, N_CHIPS):
                got = out[a].at[k ^ m, c]
                cp = _remote(got, got, send_sems.at[a, m - 1], recv_sems.at[a, m - 1], (x, y, 1 - c))
                cp.start()
                cps.append(cp)
        for a in range(n):
            for m in range(1, N_CHIPS):
                theirs = out[a].at[k ^ m, 1 - c]
                _remote(theirs, theirs, send_sems.at[a, m - 1], recv_sems.at[a, m - 1], (x, y, 1 - c)).wait_recv()
        for cp in cps:
            cp.wait_send()

    return pl.pallas_call(
        body, name=name, out_shape=[jax.ShapeDtypeStruct(a.shape, a.dtype) for a in lands],
        in_specs=[ANY] * n, out_specs=[ANY] * n, input_output_aliases={a: a for a in range(n)},
        scratch_shapes=[DMA((n, 3)), DMA((n, 3))],
        compiler_params=pltpu.CompilerParams(has_side_effects=True, collective_id=PASS_ON_BARRIER),
    )(*lands)


def _sibling_exchange(g, name):
    def body(g_ref, land_ref, send_sem, recv_sem):
        x, y, c, _ = _place()
        cp = _remote(g_ref.at[:, pl.ds(1 - c, 1)], land_ref, send_sem, recv_sem, (x, y, 1 - c))
        cp.start()
        cp.wait_recv()
        cp.wait_send()

    return pl.pallas_call(
        body, name=name, out_shape=jax.ShapeDtypeStruct((N_CHIPS, 1) + g.shape[2:], g.dtype), in_specs=[ANY],
        out_specs=ANY, scratch_shapes=[DMA, DMA], compiler_params=pltpu.CompilerParams(has_side_effects=True),
    )(g)


def _sibling_share(halves, col_half, name, after=()):
    n, na = len(halves), len(after)

    def body(*refs):
        out = refs[n + na:2 * n + na]
        send_sems, recv_sems = refs[2 * n + na:]
        x, y, c, k = _place()

        def half(a, core):
            if not col_half[a]:
                return out[a].at[:, pl.ds(core, 1)]
            cols = out[a].shape[-1] // 2
            return out[a].at[:, :, pl.ds(pl.multiple_of(core * cols, cols), cols)]

        cps = []
        for a in range(n):
            cp = _remote(half(a, c), half(a, c), send_sems.at[a], recv_sems.at[a], (x, y, 1 - c))
            cp.start()
            cps.append(cp)
        for a in range(n):
            _remote(half(a, 1 - c), half(a, 1 - c), send_sems.at[a], recv_sems.at[a], (x, y, 1 - c)).wait_recv()
        for cp in cps:
            cp.wait_send()

    out_shape = [jax.ShapeDtypeStruct(a.shape, a.dtype) for a in halves]
    return pl.pallas_call(
        body, name=name, out_shape=out_shape, in_specs=[ANY] * (n + na), out_specs=[ANY] * n,
        input_output_aliases={a: a for a in range(n)}, scratch_shapes=[DMA((n,)), DMA((n,))],
        compiler_params=pltpu.CompilerParams(has_side_effects=True),
    )(*halves, *after)


def _add_sibling(g, land, core):
    _, _, r, c = g.shape
    rb = _row_block(r, c * (3 * 2 * 2 + 2 * 4))

    def body(core_ref, g_ref, l_ref, o_ref):
        o_ref[...] = (g_ref[...].astype(F32) + l_ref[...].astype(F32)).astype(o_ref.dtype)

    return pl.pallas_call(
        body, name="rs_add_sibling", out_shape=jax.ShapeDtypeStruct((N_CHIPS, r, c), g.dtype),
        grid_spec=pltpu.PrefetchScalarGridSpec(
            num_scalar_prefetch=1, grid=(N_CHIPS, r // rb),
            in_specs=[pl.BlockSpec((None, None, rb, c), lambda j, i, core_ref: (j, core_ref[0], i, 0)),
                      pl.BlockSpec((None, None, rb, c), lambda j, i, core_ref: (j, 0, i, 0))],
            out_specs=pl.BlockSpec((None, rb, c), lambda j, i, core_ref: (j, i, 0))),
        compiler_params=pltpu.CompilerParams(dimension_semantics=("parallel", "parallel"), vmem_limit_bytes=VMEM_LIMIT),
    )(core, g, land)


def _add_chips(part, land, where, layer=0, n_layers=1, into=None, col_half=False):
    _, r, c = part.shape
    rb = _row_block(r, c * (4 * 2 * 2 + 4 * 2 + 2 * 4))

    def body(where_ref, p_ref, l_ref, *rest):
        acc = p_ref[...].astype(F32)
        for m in range(N_CHIPS - 1):
            acc = acc + l_ref[m].astype(F32)
        rest[-1][...] = acc

    in_specs = [pl.BlockSpec((None, rb, c), lambda i, where_ref: (where_ref[0], i, 0)),
                pl.BlockSpec((N_CHIPS - 1, rb, c), lambda i, where_ref: (0, i, 0))]
    args = [where, part, land]
    if into is not None:
        in_specs.append(ANY)
        args.append(into)
    if col_half:
        out_shape = jax.ShapeDtypeStruct((n_layers, r, 2 * c), F32)
        out_spec = pl.BlockSpec((None, rb, c), lambda i, where_ref: (layer, i, where_ref[1]))
    else:
        out_shape = jax.ShapeDtypeStruct((n_layers, 2, r, c), F32)
        out_spec = pl.BlockSpec((None, None, rb, c), lambda i, where_ref: (layer, where_ref[1], i, 0))
    return pl.pallas_call(
        body, name="rs_add_chips", out_shape=out_shape,
        grid_spec=pltpu.PrefetchScalarGridSpec(
            num_scalar_prefetch=1, grid=(r // rb,), in_specs=in_specs, out_specs=out_spec),
        input_output_aliases={} if into is None else {3: 0},
        compiler_params=pltpu.CompilerParams(dimension_semantics=("parallel",), vmem_limit_bytes=VMEM_LIMIT),
    )(*args)


def _sum_small(smg, own, me):
    def body(me_ref, s_ref, own_ref, o_ref):
        o_ref[...] = jnp.zeros_like(o_ref)
        for j in range(N_DEV):
            @pl.when(me_ref[0] == j)
            def _():
                o_ref[...] += own_ref[...]

            @pl.when(me_ref[0] != j)
            def _():
                o_ref[...] += s_ref[j]

    return pl.pallas_call(
        body, name="rs_sum_small", out_shape=jax.ShapeDtypeStruct(smg.shape[1:], F32),
        grid_spec=pltpu.PrefetchScalarGridSpec(
            num_scalar_prefetch=1, grid=(1,),
            in_specs=[pl.BlockSpec(smg.shape, lambda i, me_ref: (0, 0, 0)), pl.BlockSpec(own.shape, lambda i, me_ref: (0, 0))],
            out_specs=pl.BlockSpec(own.shape, lambda i, me_ref: (0, 0))),
    )(me, smg, own)


def _pool_windows(ext_ref, g, gw, tm, first_row):
    w = POOL_WINDOWS[g]
    slab = ext_ref[:, g * gw:(g + 1) * gw]
    p, k = slab, 1
    while k < w:
        p = p + pltpu.roll(p, k, 0)
        k *= 2
    t = first_row + lax.broadcasted_iota(jnp.int32, (tm, 1), 0)
    cnt = jnp.minimum(t + 1, w).astype(F32)
    return p[POOL_HALO:] / cnt - slab[POOL_HALO:]


def _fwd_pool(x, small, scale, poolw, after=()):
    t, d = x.shape
    tm = _token_tile(t)
    gw = d // len(POOL_WINDOWS)

    def body(x_ref, sm_ref, sc_ref, w_ref, h_ref, ext_ref, mix_ref):
        i = pl.program_id(0)

        @pl.when(i == 0)
        def _():
            ext_ref[0:POOL_HALO, :] = jnp.zeros((POOL_HALO, d), F32)

        @pl.when(i > 0)
        def _():
            ext_ref[0:POOL_HALO, :] = ext_ref[tm:tm + POOL_HALO, :]

        xv = x_ref[...]
        xh, _ = _rms(xv)
        ext_ref[POOL_HALO:, :] = xh * sm_ref[0:1, :]
        for g in range(len(POOL_WINDOWS)):
            pooled = _pool_windows(ext_ref, g, gw, tm, i * tm)
            cols = slice(g * gw, (g + 1) * gw)
            mix_ref[:, cols] = _dot(pooled.astype(BF16), w_ref[g]) * sc_ref[:, cols]
        mh, _ = _rms(mix_ref[...])
        h_ref[...] = xv + mh * sm_ref[1:2, :]

    return _gridded(
        body, after, name="fwd_pool", grid=(t // tm,), out_shape=[jax.ShapeDtypeStruct((t, d), F32)],
        in_specs=[pl.BlockSpec((tm, d), lambda i: (i, 0)), _resident(small.shape, lambda i: (0, 0)),
                  _resident(scale.shape, lambda i: (0, 0)), _resident(poolw.shape, lambda i: (0, 0, 0))],
        out_specs=[pl.BlockSpec((tm, d), lambda i: (i, 0))],
        scratch_shapes=[pltpu.VMEM((POOL_HALO + tm, d), F32), pltpu.VMEM((tm, d), F32)],
        args=[x, small, scale, poolw])[0]


def _fwd_ffn(h, small, wgu, wd, layer, after=(), target=None):
    t, d = h.shape
    tm = _token_tile(t)
    steps = t // tm
    fc = wgu.shape[-1]
    f = 2 * fc
    g_in, g_out = 4 * layer + 2, 4 * layer + 3
    with_loss = target is not None

    def body(h_ref, *refs):
        if with_loss:
            t_ref, sm_ref, wgu_ref, wd_ref, o_ref, gu_ref, ff_ref, n_ref, l_ref, acc_ref = refs
        else:
            sm_ref, wgu_ref, wd_ref, o_ref, gu_ref, ff_ref, n_ref = refs
        hv = h_ref[...]
        hh, _ = _rms(hv)
        n = (hh * sm_ref[g_in:g_in + 1, :]).astype(BF16)
        n_ref[...] = n
        ff = None
        for j in range(2):
            gate = _dot(n, wgu_ref[j])
            up = _dot(n, wgu_ref[2 + j])
            gu_ref[:, j * fc:(j + 1) * fc] = gate.astype(BF16)
            gu_ref[:, f + j * fc:f + (j + 1) * fc] = up.astype(BF16)
            act = (gate * jax.nn.sigmoid(gate) * up).astype(BF16)
            part = _dot(act, wd_ref[j * fc:(j + 1) * fc, :])
            ff = part if ff is None else ff + part
        ff_ref[...] = ff
        fh, _ = _rms(ff)
        out = hv + fh * sm_ref[g_out:g_out + 1, :]
        if not with_loss:
            o_ref[...] = out
            return
        i = pl.program_id(0)
        e = out - t_ref[...]
        o_ref[...] = e * (1.0 / d)

        @pl.when(i == 0)
        def _():
            acc_ref[...] = jnp.zeros_like(acc_ref)

        acc_ref[...] += _colsum(e * e)

        @pl.when(i == steps - 1)
        def _():
            l_ref[...] = jnp.full(l_ref.shape, 0.5 / d, F32) * jnp.sum(acc_ref[...])

    row = lambda i: (i, 0)
    out_shape = [jax.ShapeDtypeStruct((t, d), F32), jax.ShapeDtypeStruct((t, 2 * f), BF16),
                 jax.ShapeDtypeStruct((t, d), F32), jax.ShapeDtypeStruct((t, d), BF16)]
    out_specs = [pl.BlockSpec((tm, d), row), pl.BlockSpec((tm, 2 * f), row), pl.BlockSpec((tm, d), row),
                 pl.BlockSpec((tm, d), row)]
    weight_specs = [_resident(small.shape, lambda i: (0, 0)), _resident(wgu.shape, lambda i: (0, 0, 0)),
                    _resident(wd.shape, lambda i: (0, 0))]
    if with_loss:
        return _gridded(
            body, after, name=f"fwd_ffn{layer}_loss", grid=(steps,),
            out_shape=out_shape + [jax.ShapeDtypeStruct((8, 128), F32)],
            in_specs=[pl.BlockSpec((tm, d), row), pl.BlockSpec((tm, d), row)] + weight_specs,
            out_specs=out_specs + [pl.BlockSpec((8, 128), lambda i: (0, 0))],
            scratch_shapes=[pltpu.VMEM((1, d), F32)], args=[h, target, small, wgu, wd])
    return _gridded(
        body, after, name=f"fwd_ffn{layer}", grid=(steps,), out_shape=out_shape,
        in_specs=[pl.BlockSpec((tm, d), row)] + weight_specs, out_specs=out_specs, args=[h, small, wgu, wd])


def _fwd_ffn_up(h, small, wgu, layer):
    t, d = h.shape
    tm = _token_tile(t)
    fc = wgu.shape[-1]
    f = 2 * fc
    g_in = 4 * layer + 2

    def body(h_ref, sm_ref, wgu_ref, gu_ref, n_ref):
        hh, _ = _rms(h_ref[...])
        n = (hh * sm_ref[g_in:g_in + 1, :]).astype(BF16)
        n_ref[...] = n
        for j in range(2):
            gu_ref[:, j * fc:(j + 1) * fc] = _dot(n, wgu_ref[j]).astype(BF16)
            gu_ref[:, f + j * fc:f + (j + 1) * fc] = _dot(n, wgu_ref[2 + j]).astype(BF16)

    row = lambda i: (i, 0)
    return _gridded(
        body, (), name=f"fwd_ffn{layer}_up", grid=(t // tm,),
        out_shape=[jax.ShapeDtypeStruct((t, 2 * f), BF16), jax.ShapeDtypeStruct((t, d), BF16)],
        in_specs=[pl.BlockSpec((tm, d), row), _resident(small.shape, lambda i: (0, 0)),
                  _resident(wgu.shape, lambda i: (0, 0, 0))],
        out_specs=[pl.BlockSpec((tm, 2 * f), row), pl.BlockSpec((tm, d), row)], args=[h, small, wgu])


def _fwd_ffn_down(h, gu, small, wd, layer):
    t, d = h.shape
    tm = _token_tile(t)
    f = wd.shape[0]
    fc = f // 2
    g_out = 4 * layer + 3

    def body(h_ref, gu_ref, sm_ref, wd_ref, o_ref, ff_ref):
        ff = None
        for j in range(2):
            gate = gu_ref[:, j * fc:(j + 1) * fc].astype(F32)
            up = gu_ref[:, f + j * fc:f + (j + 1) * fc].astype(F32)
            act = (gate * jax.nn.sigmoid(gate) * up).astype(BF16)
            part = _dot(act, wd_ref[j * fc:(j + 1) * fc, :])
            ff = part if ff is None else ff + part
        ff_ref[...] = ff
        fh, _ = _rms(ff)
        o_ref[...] = h_ref[...] + fh * sm_ref[g_out:g_out + 1, :]

    row = lambda i: (i, 0)
    return _gridded(
        body, (), name=f"fwd_ffn{layer}_down", grid=(t // tm,),
        out_shape=[jax.ShapeDtypeStruct((t, d), F32), jax.ShapeDtypeStruct((t, d), F32)],
        in_specs=[pl.BlockSpec((tm, d), row), pl.BlockSpec((tm, 2 * f), row), _resident(small.shape, lambda i: (0, 0)),
                  _resident(wd.shape, lambda i: (0, 0))],
        out_specs=[pl.BlockSpec((tm, d), row), pl.BlockSpec((tm, d), row)], args=[h, gu, small, wd])


def _fwd_conv(h, small, win, wout, after=()):
    t, d = h.shape
    tm = _token_tile(t)
    pc = win.shape[-1]

    def body(h_ref, sm_ref, win_ref, wout_ref, o_ref, proj_ref, y_ref, n_ref, pj_ref, uext_ref):
        i = pl.program_id(0)

        @pl.when(i == 0)
        def _():
            uext_ref[0:CONV_HALO, :] = jnp.zeros((CONV_HALO, d), F32)

        @pl.when(i > 0)
        def _():
            uext_ref[0:CONV_HALO, :] = uext_ref[tm:tm + CONV_HALO, :]

        hv = h_ref[...]
        hh, _ = _rms(hv)
        n = (hh * sm_ref[4:5, :]).astype(BF16)
        n_ref[...] = n
        for k in range(N_CHIPS):
            pj_ref[:, k * pc:(k + 1) * pc] = _dot(n, win_ref[k])
        proj_ref[...] = pj_ref[...].astype(BF16)
        uext_ref[CONV_HALO:, :] = pj_ref[:, d:2 * d] * pj_ref[:, 2 * d:]
        taps = [sm_ref[8 + j:9 + j, :] for j in range(3)]
        full = uext_ref[...]
        conv = (full[CONV_HALO:] * taps[2] + pltpu.roll(full, 1, 0)[CONV_HALO:] * taps[1]
                + pltpu.roll(full, 2, 0)[CONV_HALO:] * taps[0])
        y = _dot((pj_ref[:, 0:d] * conv).astype(BF16), wout_ref[...])
        y_ref[...] = y
        yh, _ = _rms(y)
        o_ref[...] = hv + yh * sm_ref[5:6, :]

    row = lambda i: (i, 0)
    return _gridded(
        body, after, name="fwd_conv", grid=(t // tm,),
        out_shape=[jax.ShapeDtypeStruct((t, d), F32), jax.ShapeDtypeStruct((t, 3 * d), BF16),
                   jax.ShapeDtypeStruct((t, d), F32), jax.ShapeDtypeStruct((t, d), BF16)],
        in_specs=[pl.BlockSpec((tm, d), row), _resident(small.shape, lambda i: (0, 0)),
                  _resident(win.shape, lambda i: (0, 0, 0)), _resident(wout.shape, lambda i: (0, 0))],
        out_specs=[pl.BlockSpec((tm, d), row), pl.BlockSpec((tm, 3 * d), row), pl.BlockSpec((tm, d), row),
                   pl.BlockSpec((tm, d), row)],
        scratch_shapes=[pltpu.VMEM((tm, 3 * d), F32), pltpu.VMEM((CONV_HALO + tm, d), F32)],
        args=[h, small, win, wout])


def _bwd_ffn(dh, h, ff, gu, small, wgu, wd, layer, after=()):
    t, d = h.shape
    tm = _token_tile(t, FFN_BWD_TOKEN_TILE)
    fc = wgu.shape[-1]
    f = 2 * fc
    g_in, g_out = 4 * layer + 2, 4 * layer + 3

    def body(dh_ref, h_ref, ff_ref, gu_ref, sm_ref, wgu_ref, wd_ref, o_ref, dgu_ref, dff_ref, act_ref, sg_ref):
        i = pl.program_id(0)

        @pl.when(i == 0)
        def _():
            sg_ref[...] = jnp.zeros_like(sg_ref)

        dy = dh_ref[...]
        fh, r3 = _rms(ff_ref[...])
        sg_ref[1:2, :] += _colsum(dy * fh)
        dff = _rms_bwd(dy, fh, r3, sm_ref[g_out:g_out + 1, :]).astype(BF16)
        dff_ref[...] = dff
        for j in range(2):
            dact = _dot_nt(dff, wd_ref[j * fc:(j + 1) * fc, :])
            gate = gu_ref[:, j * fc:(j + 1) * fc].astype(F32)
            up = gu_ref[:, f + j * fc:f + (j + 1) * fc].astype(F32)
            sig = jax.nn.sigmoid(gate)
            silu = gate * sig
            act_ref[:, j * fc:(j + 1) * fc] = (silu * up).astype(BF16)
            dgu_ref[:, j * fc:(j + 1) * fc] = (dact * up * (sig * (1.0 + gate * (1.0 - sig)))).astype(BF16)
            dgu_ref[:, f + j * fc:f + (j + 1) * fc] = (dact * silu).astype(BF16)
        dn = None
        for k in range(N_CHIPS):
            part = _dot_nt(dgu_ref[:, k * fc:(k + 1) * fc], wgu_ref[k])
            dn = part if dn is None else dn + part
        hh, r2 = _rms(h_ref[...])
        sg_ref[0:1, :] += _colsum(dn * hh)
        o_ref[...] = dy + _rms_bwd(dn, hh, r2, sm_ref[g_in:g_in + 1, :])

    row = lambda i: (i, 0)
    return _gridded(
        body, after, name=f"bwd_ffn{layer}", grid=(t // tm,),
        out_shape=[jax.ShapeDtypeStruct((t, d), F32), jax.ShapeDtypeStruct((t, 2 * f), BF16),
                   jax.ShapeDtypeStruct((t, d), BF16), jax.ShapeDtypeStruct((t, f), BF16),
                   jax.ShapeDtypeStruct((8, d), F32)],
        in_specs=[pl.BlockSpec((tm, d), row), pl.BlockSpec((tm, d), row), pl.BlockSpec((tm, d), row),
                  pl.BlockSpec((tm, 2 * f), row), _resident(small.shape, lambda i: (0, 0)),
                  _resident(wgu.shape, lambda i: (0, 0, 0)), _resident(wd.shape, lambda i: (0, 0))],
        out_specs=[pl.BlockSpec((tm, d), row), pl.BlockSpec((tm, 2 * f), row), pl.BlockSpec((tm, d), row),
                   pl.BlockSpec((tm, f), row), pl.BlockSpec((8, d), lambda i: (0, 0))],
        args=[dh, h, ff, gu, small, wgu, wd])


def _bwd_conv(dh, h, y, proj, small, win, wout, after=()):
    t, d = h.shape
    tm = _token_tile(t)
    steps = t // tm
    pc = win.shape[-1]
    halo_blocks = tm // 16

    def body(dh_ref, h_ref, y_ref, proj_ref, halo_ref, sm_ref, win_ref, wout_ref,
             o_ref, dproj_ref, dy_ref, bc_ref, sg_ref, uext_ref, dcext_ref, carry_ref):
        i = pl.program_id(0)
        tile = steps - 1 - i

        @pl.when(i == 0)
        def _():
            sg_ref[...] = jnp.zeros_like(sg_ref)
            carry_ref[...] = jnp.zeros_like(carry_ref)

        dy = dh_ref[...]
        yh, r1 = _rms(y_ref[...])
        sg_ref[1:2, :] += _colsum(dy * yh)
        dyv = _rms_bwd(dy, yh, r1, sm_ref[5:6, :]).astype(BF16)
        dy_ref[...] = dyv
        dbc = _dot_nt(dyv, wout_ref[...])
        b = proj_ref[:, 0:d].astype(F32)
        cg = proj_ref[:, d:2 * d].astype(F32)
        v = proj_ref[:, 2 * d:].astype(F32)
        halo = halo_ref[...].astype(F32)[16 - CONV_HALO:]
        uh = halo[:, d:2 * d] * halo[:, 2 * d:]
        uext_ref[0:CONV_HALO, :] = jnp.where(tile > 0, uh, jnp.zeros_like(uh))
        uext_ref[CONV_HALO:, :] = cg * v
        taps = [sm_ref[8 + j:9 + j, :] for j in range(3)]
        full = uext_ref[...]
        u0 = full[CONV_HALO:]
        u1 = pltpu.roll(full, 1, 0)[CONV_HALO:]
        u2 = pltpu.roll(full, 2, 0)[CONV_HALO:]
        conv = u0 * taps[2] + u1 * taps[1] + u2 * taps[0]
        bc_ref[...] = (b * conv).astype(BF16)
        dconv = dbc * b
        sg_ref[4:5, :] += _colsum(dconv * u0)
        sg_ref[3:4, :] += _colsum(dconv * u1)
        sg_ref[2:3, :] += _colsum(dconv * u2)
        dcext_ref[0:tm, :] = dconv
        dcext_ref[tm:, :] = carry_ref[...]
        carry_ref[...] = dconv[0:CONV_HALO]
        dfull = dcext_ref[...]
        n8 = tm + CONV_HALO
        du = (dfull[0:tm] * taps[2] + pltpu.roll(dfull, n8 - 1, 0)[0:tm] * taps[1]
              + pltpu.roll(dfull, n8 - 2, 0)[0:tm] * taps[0])
        dproj_ref[:, 0:d] = (dbc * conv).astype(BF16)
        dproj_ref[:, d:2 * d] = (du * v).astype(BF16)
        dproj_ref[:, 2 * d:] = (du * cg).astype(BF16)
        dn = None
        for k in range(N_CHIPS):
            part = _dot_nt(dproj_ref[:, k * pc:(k + 1) * pc], win_ref[k])
            dn = part if dn is None else dn + part
        hh, r0 = _rms(h_ref[...])
        sg_ref[0:1, :] += _colsum(dn * hh)
        o_ref[...] = dy + _rms_bwd(dn, hh, r0, sm_ref[4:5, :])

    rev = lambda i: (steps - 1 - i, 0)
    before = lambda i: (jnp.maximum((steps - 1 - i) * halo_blocks - 1, 0), 0)
    return _gridded(
        body, after, name="bwd_conv", grid=(steps,),
        out_shape=[jax.ShapeDtypeStruct((t, d), F32), jax.ShapeDtypeStruct((t, 3 * d), BF16),
                   jax.ShapeDtypeStruct((t, d), BF16), jax.ShapeDtypeStruct((t, d), BF16),
                   jax.ShapeDtypeStruct((8, d), F32)],
        in_specs=[pl.BlockSpec((tm, d), rev), pl.BlockSpec((tm, d), rev), pl.BlockSpec((tm, d), rev),
                  pl.BlockSpec((tm, 3 * d), rev), pl.BlockSpec((16, 3 * d), before),
                  _resident(small.shape, lambda i: (0, 0)), _resident(win.shape, lambda i: (0, 0, 0)),
                  _resident(wout.shape, lambda i: (0, 0))],
        out_specs=[pl.BlockSpec((tm, d), rev), pl.BlockSpec((tm, 3 * d), rev), pl.BlockSpec((tm, d), rev),
                   pl.BlockSpec((tm, d), rev), pl.BlockSpec((8, d), lambda i: (0, 0))],
        scratch_shapes=[pltpu.VMEM((CONV_HALO + tm, d), F32), pltpu.VMEM((tm + CONV_HALO, d), F32),
                        pltpu.VMEM((CONV_HALO, d), F32)],
        args=[dh, h, y, proj, proj, small, win, wout])


def _bwd_pool(dh, x, small, scale, poolw, after=()):
    t, d = x.shape
    tm = _token_tile(t)
    steps = t // tm
    ng = len(POOL_WINDOWS)
    gw = d // ng
    halo_blocks = tm // POOL_HALO

    def body(dh_ref, x_ref, halo_ref, sm_ref, sc_ref, w_ref, o_ref, dw_ref, sg_ref,
             ext_ref, mix_ref, mm_ref, pb_ref, qext_ref, dhn_ref, carry_ref):
        i = pl.program_id(0)
        tile = steps - 1 - i

        @pl.when(i == 0)
        def _():
            sg_ref[...] = jnp.zeros_like(sg_ref)
            dw_ref[...] = jnp.zeros_like(dw_ref)
            carry_ref[...] = jnp.zeros_like(carry_ref)

        g0 = sm_ref[0:1, :]
        xv = x_ref[...]
        xh, r0 = _rms(xv)
        hx, _ = _rms(halo_ref[...])
        ext_ref[0:POOL_HALO, :] = jnp.where(tile > 0, hx * g0, jnp.zeros_like(hx))
        ext_ref[POOL_HALO:, :] = xh * g0
        for g in range(ng):
            pooled = _pool_windows(ext_ref, g, gw, tm, tile * tm)
            cols = slice(g * gw, (g + 1) * gw)
            pb = pooled.astype(BF16)
            pb_ref[:, cols] = pb
            mm = _dot(pb, w_ref[g])
            mm_ref[:, cols] = mm
            mix_ref[:, cols] = mm * sc_ref[:, cols]
        dy = dh_ref[...]
        mh, r1 = _rms(mix_ref[...])
        sg_ref[1:2, :] += _colsum(dy * mh)
        dmix = _rms_bwd(dy, mh, r1, sm_ref[1:2, :])
        sg_ref[2:3, :] += _colsum(dmix * mm_ref[...])
        mix_ref[...] = dmix * sc_ref[...]
        n16 = tm + POOL_HALO
        for g in range(ng):
            w = POOL_WINDOWS[g]
            cols = slice(g * gw, (g + 1) * gw)
            dmm = mix_ref[:, cols].astype(BF16)
            dpooled = _dot_nt(dmm, w_ref[g])
            dw_ref[g] += _dot_tn(pb_ref[:, cols], dmm)
            trow = tile * tm + lax.broadcasted_iota(jnp.int32, (tm, 1), 0)
            q = dpooled / jnp.minimum(trow + 1, w).astype(F32)
            qext_ref[0:tm, cols] = q
            qext_ref[tm:, cols] = carry_ref[:, cols]
            carry_ref[:, cols] = q[0:POOL_HALO]
            p, k = qext_ref[:, cols], 1
            while k < w:
                p = p + pltpu.roll(p, n16 - k, 0)
                k *= 2
            dhn_ref[:, cols] = p[0:tm] - dpooled
        dhn = dhn_ref[...]
        sg_ref[0:1, :] += _colsum(dhn * xh)
        o_ref[...] = dy + _rms_bwd(dhn, xh, r0, g0)

    rev = lambda i: (steps - 1 - i, 0)
    before = lambda i: (jnp.maximum((steps - 1 - i) * halo_blocks - 1, 0), 0)
    return _gridded(
        body, after, name="bwd_pool", grid=(steps,),
        out_shape=[jax.ShapeDtypeStruct((t, d), F32), jax.ShapeDtypeStruct((ng, gw, gw), F32),
                   jax.ShapeDtypeStruct((8, d), F32)],
        in_specs=[pl.BlockSpec((tm, d), rev), pl.BlockSpec((tm, d), rev), pl.BlockSpec((POOL_HALO, d), before),
                  _resident(small.shape, lambda i: (0, 0)), _resident(scale.shape, lambda i: (0, 0)),
                  _resident(poolw.shape, lambda i: (0, 0, 0))],
        out_specs=[pl.BlockSpec((tm, d), rev), pl.BlockSpec((ng, gw, gw), lambda i: (0, 0, 0)),
                   pl.BlockSpec((8, d), lambda i: (0, 0))],
        scratch_shapes=[pltpu.VMEM((POOL_HALO + tm, d), F32), pltpu.VMEM((tm, d), F32), pltpu.VMEM((tm, d), F32),
                        pltpu.VMEM((tm, d), BF16), pltpu.VMEM((tm + POOL_HALO, d), F32), pltpu.VMEM((tm, d), F32),
                        pltpu.VMEM((POOL_HALO, d), F32)],
        args=[dh, x, x, small, scale, poolw])


def _weight_grad(a, b, bm, bn, half_on, name):
    t, m = a.shape
    _, n = b.shape
    if half_on == "a":
        a_cols, b_cols = 2 * bm, bn
    else:
        a_cols, b_cols = bm, 2 * bn
    steps = max(m // a_cols, n // b_cols)

    def spec(cols, total):
        if cols == total:
            return _resident((t, cols), lambda p, j: (0, 0))
        return pl.BlockSpec((t, cols), lambda p, j: (0, j))

    def tile(a_ref, b_ref, half):
        if half_on == "a":
            return _dot_tn(a_ref[:, half * bm:(half + 1) * bm], b_ref[...])
        return _dot_tn(a_ref[...], b_ref[:, half * bn:(half + 1) * bn])

    def body(a_ref, b_ref, parts_ref, land_ref, acc_ref, stage_ref, got_ref, send_sems, recv_sems, got_sem):
        p, j = pl.program_id(0), pl.program_id(1)
        x, y, c, _ = _place()
        half = jnp.where(p == 0, 1 - c, c)

        def send(jj):
            return _remote(stage_ref.at[jj % 2], land_ref.at[jj], send_sems.at[jj], recv_sems.at[jj], (x, y, 1 - c))

        def fetch():
            return pltpu.make_async_copy(land_ref.at[j], got_ref, got_sem)

        @pl.when(p == 1)
        def _():
            @pl.when(j == 0)
            def _():
                for jj in range(max(steps - 2, 0), steps):
                    send(jj).wait_send()

            send(j).wait_recv()
            fetch().start()

        for hv in range(2):
            @pl.when(half == hv)
            def _():
                acc_ref[...] = tile(a_ref, b_ref, hv)

        @pl.when(p == 0)
        def _():
            @pl.when(j >= 2)
            def _():
                send(j - 2).wait_send()

            stage_ref[j % 2] = acc_ref[...].astype(BF16)
            send(j).start()

        @pl.when(p == 1)
        def _():
            fetch().wait()
            parts_ref[...] = (acc_ref[...] + got_ref[...].astype(F32)).astype(BF16)

    return _gridded(
        body, (), name=name, grid=(2, steps),
        out_shape=[jax.ShapeDtypeStruct((steps, bm, bn), BF16), jax.ShapeDtypeStruct((steps, bm, bn), BF16)],
        in_specs=[spec(a_cols, m), spec(b_cols, n)],
        out_specs=[pl.BlockSpec((None, bm, bn), lambda p, j: (p * j, 0, 0)), ANY],
        scratch_shapes=[pltpu.VMEM((bm, bn), F32), pltpu.VMEM((2, bm, bn), BF16), pltpu.VMEM((bm, bn), BF16),
                        DMA((steps,)), DMA((steps,)), DMA],
        args=[a, b])[0]


def _cast_layer(w, layer, name):
    _, r, c = w.shape
    rb = _row_block(r, c * (4 + 2) * 2)

    def body(w_ref, o_ref):
        o_ref[...] = w_ref[...].astype(BF16)

    return pl.pallas_call(
        body, name=name, grid=(r // rb,), out_shape=jax.ShapeDtypeStruct((r, c), BF16),
        in_specs=[pl.BlockSpec((None, rb, c), lambda i: (layer, i, 0))], out_specs=pl.BlockSpec((rb, c), lambda i: (i, 0)),
        compiler_params=pltpu.CompilerParams(dimension_semantics=("parallel",), vmem_limit_bytes=VMEM_LIMIT),
    )(w)


def _adamw_math(w, g, m, v):
    bc1 = 1.0 - ADAM_B1 ** ADAM_STEP
    bc2 = 1.0 - ADAM_B2 ** ADAM_STEP
    nm = ADAM_B1 * m + (1.0 - ADAM_B1) * g
    nv = ADAM_B2 * v + (1.0 - ADAM_B2) * (g * g)
    return -ADAM_LR * ((nm / bc1) / (jnp.sqrt(nv / bc2) + ADAM_EPS) + ADAM_WD * w), nm, nv


def _adamw_small(small_sum, where, gains, taps, scale):
    dq = gains[0].shape[-1]
    d = small_sum.shape[-1]

    def body(where_ref, mine_ref, all_ref, gw, gm, gv, tw, tm_, tv, sw, sm, sv,
             gg, gd, gnm, gnv, tg, td, tnm, tnv, sg, sd, snm, snv):
        for layer in range(gw.shape[0]):
            g = mine_ref[4 * layer:4 * layer + 4, :]
            gg[layer] = g
            gd[layer], gnm[layer], gnv[layer] = _adamw_math(gw[layer], g, gm[layer], gv[layer])
        g = mine_ref[8:8 + tw.shape[1], :]
        tg[0] = g
        td[0], tnm[0], tnv[0] = _adamw_math(tw[0], g, tm_[0], tv[0])
        g = all_ref[11:12, :]
        sg[...] = g
        sd[...], snm[...], snv[...] = _adamw_math(sw[...], g, sm[...], sv[...])

    full = lambda a: pl.BlockSpec(a.shape, lambda i, where_ref: (0,) * a.ndim)
    params = [*gains, *taps, *scale]
    outs = [gains[0]] * 4 + [taps[0]] * 4 + [scale[0]] * 4
    res = pl.pallas_call(
        body, name="adamw_small", out_shape=[jax.ShapeDtypeStruct(a.shape, F32) for a in outs],
        grid_spec=pltpu.PrefetchScalarGridSpec(
            num_scalar_prefetch=1, grid=(1,),
            in_specs=[pl.BlockSpec((16, dq), lambda i, where_ref: (0, where_ref[0])), pl.BlockSpec((16, d), lambda i, where_ref: (0, 0)),
                      *[full(a) for a in params]],
            out_specs=[full(a) for a in outs]),
    )(where, small_sum, small_sum, *params)
    return tuple(res[0:4]), tuple(res[4:8]), tuple(res[8:12])


def _adamw(w, g, m, v, name):
    r, c = w.shape
    rb = _row_block(r, c * (8 * 4 * 2 + 4 * 4))

    def body(w_ref, g_ref, m_ref, v_ref, d_ref, nm_ref, nv_ref, go_ref):
        gv = g_ref[...]
        go_ref[...] = gv
        d_ref[...], nm_ref[...], nv_ref[...] = _adamw_math(w_ref[...], gv, m_ref[...], v_ref[...])

    spec = pl.BlockSpec((rb, c), lambda i: (i, 0))
    return pl.pallas_call(
        body, name=name, grid=(r // rb,), out_shape=[jax.ShapeDtypeStruct((r, c), F32)] * 4,
        in_specs=[spec] * 4, out_specs=[spec] * 4,
        compiler_params=pltpu.CompilerParams(dimension_semantics=("parallel",), vmem_limit_bytes=VMEM_LIMIT),
    )(w, g, m, v)


def kernel(x, norm_gains, pool_w, pool_scale, conv_in_w, conv_w, conv_out_w, ffn_gate_up_w, ffn_down_w, loss_target, m_norm_gains, m_pool_w, m_pool_scale, m_conv_in_w, m_conv_w, m_conv_out_w, m_ffn_gate_up_w, m_ffn_down_w, v_norm_gains, v_pool_w, v_pool_scale, v_conv_in_w, v_conv_w, v_conv_out_w, v_ffn_gate_up_w, v_ffn_down_w):
    _, t, d = x.shape
    dq = d // N_CHIPS
    ng = len(POOL_WINDOWS)
    gw = d // ng
    fq = ffn_down_w.shape[1]
    f = N_CHIPS * fq
    fc = f // 2
    core = lax.axis_index("c")
    chip = 2 * lax.axis_index("x") + lax.axis_index("y")
    core_arr = jnp.reshape(core, (1,)).astype(jnp.int32)
    where_arr = jnp.stack([chip, core]).astype(jnp.int32)
    x2, target = x[0], loss_target[0]

    small_loc = jnp.concatenate(
        [norm_gains.reshape(8, dq), conv_w[0], jnp.zeros((5, dq), F32)], axis=0).reshape(1, 2, 8, dq)
    pool_loc = pool_w.astype(BF16).reshape(1, 2, ng // 2 * (gw // N_CHIPS), gw)
    wgu_loc = [_cast_layer(ffn_gate_up_w, 0, "cast_gate_up0").reshape(1, 2, d // 2, fc),
               ffn_gate_up_w[1:2].astype(BF16).reshape(1, 2, d // 2, fc)]
    wd_loc = [_cast_layer(ffn_down_w, 0, "cast_down0").reshape(1, 2, fq // 2, d),
              ffn_down_w[1:2].astype(BF16).reshape(1, 2, fq // 2, d)]
    win_loc = conv_in_w.astype(BF16).reshape(1, 2, d // 2, -1)
    wout_loc = conv_out_w.astype(BF16).reshape(1, 2, dq // 2, d)

    def ffn_weights(wgu_f, wd_f):
        return wgu_f.reshape(N_CHIPS, d, fc), wd_f.reshape(f, d)

    ag0 = _SplitGather([(pool_loc, 0), (small_loc, 0), (wgu_loc[0], 0), (wd_loc[0], 0)])
    ag0.start("ag_start_layer0")
    ag1 = _SplitGather([(win_loc, 0), (wout_loc, 0), (wgu_loc[1], 0), (wd_loc[1], 0)])
    ag1.start("ag_start_layer1", [ag0.token])
    pool_f, small_f = _pass_on(ag0.wait([0, 1], ag1.token, "ag_wait_first"), "ag_pass_first")
    poolw = pool_f.reshape(N_CHIPS, ng, gw // N_CHIPS, gw).transpose(1, 0, 2, 3).reshape(ng, gw, gw)
    small = small_f.transpose(1, 2, 0, 3).reshape(16, d)
    h1 = _fwd_pool(x2, small, pool_scale, poolw)
    (wgu0,) = _pass_on(ag0.wait([2], h1, "ag_wait_gate_up0"), "ag_pass_gate_up0")
    wgu0 = wgu0.reshape(N_CHIPS, d, fc)
    gu0, n0 = _fwd_ffn_up(h1, small, wgu0, 0)
    (wd0,) = _pass_on(ag0.wait([3], gu0, "ag_wait_down0"), "ag_pass_down0")
    wd0 = wd0.reshape(f, d)
    h2, ff0 = _fwd_ffn_down(h1, gu0, small, wd0, 0)
    win_f, wout_f = _pass_on(ag1.wait([0, 1], h2, "ag_wait_conv"), "ag_pass_conv")
    win_f, wout_f = win_f.reshape(N_CHIPS, d, -1), wout_f.reshape(d, d)
    h3, proj, y, nc = _fwd_conv(h2, small, win_f, wout_f)
    wgu1, wd1 = ffn_weights(*_pass_on(ag1.wait([2, 3], h3, "ag_wait_ffn1"), "ag_pass_ffn1"))
    dh4, gu1, ff1, n1, loss_blk = _fwd_ffn(h3, small, wgu1, wd1, 1, target=target)

    dh3, dgu1, dff1, act1, sg_f1 = _bwd_ffn(dh4, h3, ff1, gu1, small, wgu1, wd1, 1)
    parts_d1 = _weight_grad(act1, dff1, fc, d // 2, "b", "dw_down1")
    parts_gu1 = _weight_grad(n1, dgu1, d // 2, fc, "a", "dw_gate_up1")
    ex_ffn1 = _SplitExchange([parts_d1.reshape(N_CHIPS, fq, d // 2), parts_gu1])
    started = ex_ffn1.start("rs_start_ffn1")
    dh2, dproj, dyv, bcv, sg_c = _bwd_conv(dh3, h2, y, proj, small, win_f, wout_f, [started])
    parts_in = _weight_grad(nc, dproj, d // 2, 3 * d // N_CHIPS, "a", "dw_conv_in")
    parts_out = _weight_grad(bcv, dyv, dq // 2, d, "a", "dw_conv_out")
    ex_conv = _SplitExchange([parts_in, parts_out])
    started = ex_conv.start("rs_start_conv")
    dh1, dgu0, dff0, act0, sg_f0 = _bwd_ffn(dh2, h1, ff0, gu0, small, wgu0, wd0, 0, [started])
    parts_d0 = _weight_grad(act0, dff0, fc, d // 2, "b", "dw_down0")
    parts_gu0 = _weight_grad(n0, dgu0, d // 2, fc, "a", "dw_gate_up0")
    ex_ffn0 = _SplitExchange([parts_d0.reshape(N_CHIPS, fq, d // 2), parts_gu0])
    started = ex_ffn0.start("rs_start_ffn0")
    grad_x, dpool, sg_p = _bwd_pool(dh1, x2, small, pool_scale, poolw, [started])
    g_pool = dpool.astype(BF16).reshape(2, ng // 2, N_CHIPS, gw // N_CHIPS, gw).transpose(2, 0, 1, 3, 4).reshape(
        N_CHIPS, 2, ng // 2 * (gw // N_CHIPS), gw)
    small_g = jnp.concatenate(
        [sg_p[0:2], sg_f0[0:2], sg_c[0:2], sg_f1[0:2], sg_c[2:5], sg_p[2:3],
         jnp.broadcast_to(loss_blk[0:1, 0:1], (1, d)), jnp.zeros((3, d), F32)], axis=0)
    ex_pool = _SplitExchange([_add_sibling(g_pool, _sibling_exchange(g_pool, "rs_sibling_pool"), core_arr)])
    started = ex_pool.start("rs_start_pool")

    def update(w, g, m, v, name):
        if w.size * 4 * 8 <= STREAM_BUDGET // 4:
            def body(w_ref, g_ref, m_ref, v_ref, go_ref, d_ref, nm_ref, nv_ref):
                go_ref[...] = g_ref[...]
                d_ref[...], nm_ref[...], nv_ref[...] = _adamw_math(w_ref[...], g_ref[...], m_ref[...], v_ref[...])

            return tuple(pl.pallas_call(body, name="adamw_" + name, out_shape=[jax.ShapeDtypeStruct(w.shape, F32)] * 4)(
                w, g.reshape(w.shape), m, v))
        flat = (-1, w.shape[-1])
        dl, m2, v2, g2 = _adamw(w.reshape(flat), g.reshape(flat), m.reshape(flat), v.reshape(flat), "adamw_" + name)
        return tuple(o.reshape(w.shape) for o in (g2, dl, m2, v2))

    (parts_d1, parts_gu1), (recv_d1, recv_gu1) = ex_ffn1.wait([started], "rs_wait_ffn1")
    (parts_in, parts_out), (recv_in, recv_out) = ex_conv.wait([started], "rs_wait_conv")
    gs_gu = _add_chips(parts_gu1, recv_gu1, where_arr, 1, 2)
    gs_d = _add_chips(parts_d1, recv_d1, where_arr, 1, 2, col_half=True)
    gs_in = _add_chips(parts_in, recv_in, where_arr)
    gs_out = _add_chips(parts_out, recv_out, where_arr)
    full_in, full_out = _sibling_share([gs_in, gs_out], [False, False], "rs_share_conv")
    up_in = update(conv_in_w, full_in.reshape(1, d, -1), m_conv_in_w, v_conv_in_w, "conv_in")
    up_out = update(conv_out_w, full_out.reshape(1, dq, d), m_conv_out_w, v_conv_out_w, "conv_out")
    done_first = [up_in[1], up_out[1], gs_gu, gs_d]
    (parts_d0, parts_gu0), (recv_d0, recv_gu0) = ex_ffn0.wait(done_first, "rs_wait_ffn0")
    (parts_p,), (recv_p,) = ex_pool.wait(done_first, "rs_wait_pool")
    ex_small = _SplitExchange([small_g], _SmallGather(small_g))
    started = ex_small.start("rs_start_small", [recv_gu0])
    gs_gu = _add_chips(parts_gu0, recv_gu0, where_arr, 0, 2, gs_gu)
    gs_d = _add_chips(parts_d0, recv_d0, where_arr, 0, 2, gs_d, col_half=True)
    gs_pool = _add_chips(parts_p, recv_p, where_arr)
    full_gu, full_d, full_pool = _sibling_share([gs_gu, gs_d, gs_pool], [False, True, False], "rs_share_ffn", [started])
    up_gu = update(ffn_gate_up_w, full_gu.reshape(2, d, fc), m_ffn_gate_up_w, v_ffn_gate_up_w, "gate_up")
    up_d = update(ffn_down_w, full_d.reshape(2, fq, d), m_ffn_down_w, v_ffn_down_w, "down")
    (small_own,), (small_all,) = ex_small.wait([up_gu[1], up_d[1]], "rs_wait_small")
    small_sum = _sum_small(small_all, small_own, (2 * chip + core).reshape(1).astype(jnp.int32))
    loss = small_sum[12, 0]
    up_gains, up_taps, up_scale = _adamw_small(
        small_sum, where_arr, (norm_gains, m_norm_gains, v_norm_gains), (conv_w, m_conv_w, v_conv_w),
        (pool_scale, m_pool_scale, v_pool_scale))

    ups = [
        up_gains,
        update(pool_w, full_pool.reshape(1, ng, gw // N_CHIPS, gw), m_pool_w, v_pool_w, "pool_w"),
        up_scale,
        up_in,
        up_taps,
        up_out,
        up_gu,
        up_d,
    ]
    grads_out, deltas, new_ms, new_vs = (list(col) for col in zip(*ups))
    return (loss, grad_x[None], *grads_out, *deltas, *new_ms, *new_vs)
```

```python
import jax
import jax.numpy as jnp
from jax import lax
from jax.experimental import pallas as pl
from jax.experimental.pallas import tpu as pltpu

RMS_EPS = 1e-6
POOL_WINDOWS = (2, 4, 8, 16)
POOL_HALO = 16
CONV_HALO = 8
N_CHIPS = 4
N_DEV = 8
ADAM_LR = 0.001
ADAM_B1 = 0.9
ADAM_B2 = 0.999
ADAM_EPS = 1e-08
ADAM_WD = 0.01
ADAM_STEP = 10
VMEM_LIMIT = 56 * 2**20
STREAM_BUDGET = 24 * 2**20
MESH = pl.DeviceIdType.MESH
ANY = pl.BlockSpec(memory_space=pl.ANY)
DMA = pltpu.SemaphoreType.DMA
BF16 = jnp.bfloat16
F32 = jnp.float32


TOKEN_TILE = 512
FFN_BWD_TOKEN_TILE = 256


def _token_tile(t, rows=TOKEN_TILE):
    return min(rows, t)


def _rms(x):
    r = lax.rsqrt(jnp.mean(x * x, axis=-1, keepdims=True) + RMS_EPS)
    return x * r, r


def _rms_bwd(dy, xh, r, g):
    a = dy * g
    return r * (a - xh * jnp.mean(a * xh, axis=-1, keepdims=True))


def _dot(a, b):
    return jnp.dot(a, b, preferred_element_type=F32)


def _dot_nt(a, b):
    return lax.dot_general(a, b, (((1,), (1,)), ((), ())), preferred_element_type=F32)


def _dot_tn(a, b):
    return lax.dot_general(a, b, (((0,), (0,)), ((), ())), preferred_element_type=F32)


def _colsum(a):
    return jnp.sum(a, axis=0, keepdims=True)


def _resident(block, index_map):
    return pl.BlockSpec(block, index_map, pipeline_mode=pl.Buffered(1))


def _row_block(r, row_bytes):
    best = None
    for rb in range(16, r + 1, 16):
        if r % rb == 0 and rb * row_bytes <= STREAM_BUDGET:
            best = rb
    return best if best is not None else r


def _place():
    x, y, c = lax.axis_index("x"), lax.axis_index("y"), lax.axis_index("c")
    return x, y, c, 2 * x + y


def _dev(chip, core):
    return (chip // 2, chip % 2, core)


def _remote(src, dst, send_sem, recv_sem, device):
    return pltpu.make_async_remote_copy(src_ref=src, dst_ref=dst, send_sem=send_sem, recv_sem=recv_sem,
                                        device_id=device, device_id_type=MESH)


class _Gather:
    def __init__(self, shards):
        self.args = [s for s, _ in shards]
        self.layers = [l for _, l in shards]
        self.out_shape = [jax.ShapeDtypeStruct((N_CHIPS,) + s.shape[1:], s.dtype) for s in self.args]

    def _own(self, loc, out, sems, a):
        x, y, c, k = _place()
        return _remote(loc[a].at[self.layers[a]], out[a].at[k], sems[0].at[a], sems[1].at[a], (x, y, 1 - c))

    def _ici(self, loc, out, sems, a, m, arrival):
        x, y, c, k = _place()
        dst = out[a].at[k ^ m, c] if arrival else out[a].at[k, c]
        return _remote(loc[a].at[self.layers[a], c], dst, sems[2].at[a, m - 1], sems[3].at[a, m - 1], _dev(k ^ m, c))

    def start(self, loc, out, sems):
        for a in range(len(self.args)):
            for m in range(1, N_CHIPS):
                self._ici(loc, out, sems, a, m, False).start()
            self._own(loc, out, sems, a).start()


class _SmallGather:
    PEERS = N_DEV - 1

    def __init__(self, small):
        self.args = [small]
        self.out_shape = [jax.ShapeDtypeStruct((N_DEV,) + small.shape, small.dtype)]

    def _copy(self, sm, land, sems, m, arrival):
        x, y, c, k = _place()
        me = 2 * k + c
        peer = me ^ m
        return _remote(sm[0], land[0].at[peer if arrival else me], sems[0].at[0, m - 1], sems[1].at[0, m - 1],
                       (peer // 4, (peer // 2) % 2, peer % 2))

    def start(self, sm, land, sems):
        for m in range(1, N_DEV):
            self._copy(sm, land, sems, m, False).start()

    def finish(self, sm, land, sems):
        for m in range(1, N_DEV):
            self._copy(sm, land, sems, m, True).wait_recv()
        for m in range(1, N_DEV):
            self._copy(sm, land, sems, m, False).wait_send()


class _ChipExchange:
    PEERS = N_CHIPS - 1

    def __init__(self, parts):
        self.args = list(parts)
        self.out_shape = [jax.ShapeDtypeStruct((N_CHIPS - 1,) + p.shape[1:], p.dtype) for p in parts]

    def _copy(self, p, land, sems, a, m):
        x, y, c, k = _place()
        return _remote(p[a].at[k ^ m], land[a].at[m - 1], sems[0].at[a, m - 1], sems[1].at[a, m - 1], _dev(k ^ m, c))

    def start(self, p, land, sems):
        for a in range(len(self.args)):
            for m in range(1, N_CHIPS):
                self._copy(p, land, sems, a, m).start()

    def finish(self, p, land, sems):
        for a in range(len(self.args)):
            for m in range(1, N_CHIPS):
                self._copy(p, land, sems, a, m).wait_recv()
        for a in range(len(self.args)):
            for m in range(1, N_CHIPS):
                self._copy(p, land, sems, a, m).wait_send()


def _gridded(body, after, *, name, grid, in_specs, out_specs, out_shape, args, scratch_shapes=()):
    ni, na = len(in_specs), len(after)

    def full(*refs):
        body(*refs[:ni], *refs[ni + na:])

    return list(pl.pallas_call(
        full, name=name, grid=grid, in_specs=[*in_specs, *[ANY] * na], out_specs=list(out_specs),
        out_shape=list(out_shape), scratch_shapes=list(scratch_shapes),
        compiler_params=pltpu.CompilerParams(dimension_semantics=("arbitrary",) * len(grid), vmem_limit_bytes=VMEM_LIMIT),
    )(*args, *after))


HBM = pl.BlockSpec(memory_space=pltpu.HBM)
SEM = pl.BlockSpec(memory_space=pltpu.SEMAPHORE)
DATAFLOW = pltpu.SideEffectType.DATAFLOW_SIDE_EFFECTING


class _SplitGather:
    PER_ARRAY = 8

    def __init__(self, shards):
        self.plan = _Gather(shards)
        self.n = len(shards)

    @staticmethod
    def _tables(sems_of):
        class Table:
            def __init__(self, pick):
                self.pick = pick

            @property
            def at(self):
                return self

            def __getitem__(self, idx):
                return self.pick(idx)

        return [Table(lambda a: sems_of[a][0]), Table(lambda a: sems_of[a][1]),
                Table(lambda am: sems_of[am[0]][2 + am[1]]), Table(lambda am: sems_of[am[0]][5 + am[1]])]

    def start(self, name, after=()):
        n, plan, per, na = self.n, self.plan, self.PER_ARRAY, len(after)

        def body(*refs):
            loc, land = refs[:n], refs[n:2 * n]
            sems_of = {a: refs[2 * n + na + per * a:2 * n + na + per * (a + 1)] for a in range(n)}
            plan.start(loc, land, self._tables(sems_of))
            refs[-1][...] = jnp.zeros_like(refs[-1])

        lands = [pltpu.with_memory_space_constraint(lax.empty(o.shape, o.dtype), pltpu.HBM) for o in plan.out_shape]
        locs = [pltpu.with_memory_space_constraint(a, pltpu.HBM) for a in plan.args]
        res = pl.pallas_call(
            body, name=name,
            out_shape=[*[DMA(())] * (per * n),
                       *[pltpu.HBM(o.shape, o.dtype) for o in plan.out_shape],
                       jax.ShapeDtypeStruct((8, 128), F32)],
            in_specs=[HBM] * (2 * n) + [pl.BlockSpec(memory_space=pl.ANY)] * na,
            out_specs=[SEM] * (per * n) + [HBM] * n + [pl.BlockSpec(memory_space=pltpu.VMEM)],
            input_output_aliases={n + i: per * n + i for i in range(n)},
            compiler_params=pltpu.CompilerParams(has_side_effects=DATAFLOW),
        )(*locs, *lands, *after)
        self.sems = {a: list(res[per * a:per * (a + 1)]) for a in range(n)}
        self.locs = locs
        self.lands = list(res[per * n:per * n + n])
        self.token = res[-1]

    def wait(self, idxs, after, name):
        plan, g, per = self.plan, len(idxs), self.PER_ARRAY

        def body(*refs):
            loc = {a: refs[j] for j, a in enumerate(idxs)}
            land = {a: refs[g + j] for j, a in enumerate(idxs)}
            sems = self._tables({a: refs[2 * g + per * j:2 * g + per * (j + 1)] for j, a in enumerate(idxs)})
            for a in idxs:
                for m in range(1, N_CHIPS):
                    plan._ici(loc, land, sems, a, m, True).wait_recv()
                    plan._ici(loc, land, sems, a, m, False).wait_send()
                plan._own(loc, land, sems, a).wait_recv()
                plan._own(loc, land, sems, a).wait_send()

        res = pl.pallas_call(
            body, name=name,
            out_shape=[pltpu.HBM(self.lands[a].shape, self.lands[a].dtype) for a in idxs],
            in_specs=[HBM] * (2 * g) + [SEM] * (per * g) + [pl.BlockSpec(memory_space=pl.ANY)], out_specs=[HBM] * g,
            input_output_aliases={g + j: j for j in range(g)},
            compiler_params=pltpu.CompilerParams(has_side_effects=DATAFLOW),
        )(*[self.locs[a] for a in idxs], *[self.lands[a] for a in idxs],
          *[s for a in idxs for s in self.sems[a]], after)
        return list(res)


class _SplitExchange:
    def __init__(self, parts, plan=None):
        self.plan = _ChipExchange(parts) if plan is None else plan
        self.n = len(parts)
        self.PER_ARRAY = 2 * self.plan.PEERS

    def _tables(self, sems_of):
        class Table:
            def __init__(self, pick):
                self.pick = pick

            @property
            def at(self):
                return self

            def __getitem__(self, am):
                return self.pick(am)

        peers = self.plan.PEERS
        return [Table(lambda am: sems_of[am[0]][am[1]]), Table(lambda am: sems_of[am[0]][peers + am[1]])]

    def start(self, name, after=()):
        n, plan, per, na = self.n, self.plan, self.PER_ARRAY, len(after)

        def body(*refs):
            p, land = refs[:n], refs[n:2 * n]
            sems_of = {a: refs[2 * n + na + per * a:2 * n + na + per * (a + 1)] for a in range(n)}
            plan.start(p, land, self._tables(sems_of))
            refs[-1][...] = jnp.zeros_like(refs[-1])

        lands = [pltpu.with_memory_space_constraint(lax.empty(o.shape, o.dtype), pltpu.HBM) for o in plan.out_shape]
        parts = [pltpu.with_memory_space_constraint(a, pltpu.HBM) for a in plan.args]
        res = pl.pallas_call(
            body, name=name,
            out_shape=[*[DMA(())] * (per * n),
                       *[pltpu.HBM(a.shape, a.dtype) for a in plan.args],
                       *[pltpu.HBM(o.shape, o.dtype) for o in plan.out_shape],
                       jax.ShapeDtypeStruct((8, 128), F32)],
            in_specs=[HBM] * (2 * n) + [pl.BlockSpec(memory_space=pl.ANY)] * na,
            out_specs=[SEM] * (per * n) + [HBM] * (2 * n) + [pl.BlockSpec(memory_space=pltpu.VMEM)],
            input_output_aliases={i: per * n + i for i in range(2 * n)},
            compiler_params=pltpu.CompilerParams(has_side_effects=DATAFLOW),
        )(*parts, *lands, *after)
        self.sems = list(res[:per * n])
        self.parts = list(res[per * n:per * n + n])
        self.lands = list(res[per * n + n:per * n + 2 * n])
        return res[-1]

    def wait(self, after, name):
        n, plan, per = self.n, self.plan, self.PER_ARRAY

        def body(*refs):
            p, land = refs[:n], refs[n:2 * n]
            sems_of = {a: refs[2 * n + per * a:2 * n + per * (a + 1)] for a in range(n)}
            plan.finish(p, land, self._tables(sems_of))

        res = pl.pallas_call(
            body, name=name,
            out_shape=[*[pltpu.HBM(a.shape, a.dtype) for a in self.parts], *[pltpu.HBM(a.shape, a.dtype) for a in self.lands]],
            in_specs=[HBM] * (2 * n) + [SEM] * (per * n) + [pl.BlockSpec(memory_space=pl.ANY)] * len(after),
            out_specs=[HBM] * (2 * n), input_output_aliases={i: i for i in range(2 * n)},
            compiler_params=pltpu.CompilerParams(has_side_effects=DATAFLOW),
        )(*self.parts, *self.lands, *self.sems, *after)
        return list(res[:n]), list(res[n:])


PASS_ON_BARRIER = 1


def _sibling_barrier():
    x, y, c, _ = _place()
    barrier = pltpu.get_barrier_semaphore()
    pl.semaphore_signal(barrier, inc=1, device_id=(x, y, 1 - c), device_id_type=MESH)
    pl.semaphore_wait(barrier, 1)


def _pass_on(lands, name):
    n = len(lands)

    def body(*refs):
        out = refs[n:2 * n]
        send_sems, recv_sems = refs[2 * n:]
        x, y, c, k = _place()
        _sibling_barrier()
        cps = []
        for a in range(n):
            for m in range(1, N_CHIPS):
                got = out[a].at[k ^ m, c]
                cp = _remote(got, got, send_sems.at[a, m - 1], recv_sems.at[a, m - 1], (x, y, 1 - c))
                cp.start()
                cps.append(cp)
        for a in range(n):
            for m in range(1, N_CHIPS):
                theirs = out[a].at[k ^ m, 1 - c]
                _remote(theirs, theirs, send_sems.at[a, m - 1], recv_sems.at[a, m - 1], (x, y, 1 - c)).wait_recv()
        for cp in cps:
            cp.wait_send()

    return pl.pallas_call(
        body, name=name, out_shape=[jax.ShapeDtypeStruct(a.shape, a.dtype) for a in lands],
        in_specs=[ANY] * n, out_specs=[ANY] * n, input_output_aliases={a: a for a in range(n)},
        scratch_shapes=[DMA((n, 3)), DMA((n, 3))],
        compiler_params=pltpu.CompilerParams(has_side_effects=True, collective_id=PASS_ON_BARRIER),
    )(*lands)


def _sibling_exchange(g, name):
    def body(g_ref, land_ref, send_sem, recv_sem):
        x, y, c, _ = _place()
        cp = _remote(g_ref.at[:, pl.ds(1 - c, 1)], land_ref, send_sem, recv_sem, (x, y, 1 - c))
        cp.start()
        cp.wait_recv()
        cp.wait_send()

    return pl.pallas_call(
        body, name=name, out_shape=jax.ShapeDtypeStruct((N_CHIPS, 1) + g.shape[2:], g.dtype), in_specs=[ANY],
        out_specs=ANY, scratch_shapes=[DMA, DMA], compiler_params=pltpu.CompilerParams(has_side_effects=True),
    )(g)


def _sibling_share(halves, col_half, name, after=()):
    n, na = len(halves), len(after)

    def body(*refs):
        out = refs[n + na:2 * n + na]
        send_sems, recv_sems = refs[2 * n + na:]
        x, y, c, k = _place()

        def half(a, core):
            if not col_half[a]:
                return out[a].at[:, pl.ds(core, 1)]
            cols = out[a].shape[-1] // 2
            return out[a].at[:, :, pl.ds(pl.multiple_of(core * cols, cols), cols)]

        cps = []
        for a in range(n):
            cp = _remote(half(a, c), half(a, c), send_sems.at[a], recv_sems.at[a], (x, y, 1 - c))
            cp.start()
            cps.append(cp)
        for a in range(n):
            _remote(half(a, 1 - c), half(a, 1 - c), send_sems.at[a], recv_sems.at[a], (x, y, 1 - c)).wait_recv()
        for cp in cps:
            cp.wait_send()

    out_shape = [jax.ShapeDtypeStruct(a.shape, a.dtype) for a in halves]
    return pl.pallas_call(
        body, name=name, out_shape=out_shape, in_specs=[ANY] * (n + na), out_specs=[ANY] * n,
        input_output_aliases={a: a for a in range(n)}, scratch_shapes=[DMA((n,)), DMA((n,))],
        compiler_params=pltpu.CompilerParams(has_side_effects=True),
    )(*halves, *after)


def _add_sibling(g, land, core):
    _, _, r, c = g.shape
    rb = _row_block(r, c * (3 * 2 * 2 + 2 * 4))

    def body(core_ref, g_ref, l_ref, o_ref):
        o_ref[...] = (g_ref[...].astype(F32) + l_ref[...].astype(F32)).astype(o_ref.dtype)

    return pl.pallas_call(
        body, name="rs_add_sibling", out_shape=jax.ShapeDtypeStruct((N_CHIPS, r, c), g.dtype),
        grid_spec=pltpu.PrefetchScalarGridSpec(
            num_scalar_prefetch=1, grid=(N_CHIPS, r // rb),
            in_specs=[pl.BlockSpec((None, None, rb, c), lambda j, i, core_ref: (j, core_ref[0], i, 0)),
                      pl.BlockSpec((None, None, rb, c), lambda j, i, core_ref: (j, 0, i, 0))],
            out_specs=pl.BlockSpec((None, rb, c), lambda j, i, core_ref: (j, i, 0))),
        compiler_params=pltpu.CompilerParams(dimension_semantics=("parallel", "parallel"), vmem_limit_bytes=VMEM_LIMIT),
    )(core, g, land)


def _add_chips(part, land, where, layer=0, n_layers=1, into=None, col_half=False):
    _, r, c = part.shape
    rb = _row_block(r, c * (4 * 2 * 2 + 4 * 2 + 2 * 4))

    def body(where_ref, p_ref, l_ref, *rest):
        acc = p_ref[...].astype(F32)
        for m in range(N_CHIPS - 1):
            acc = acc + l_ref[m].astype(F32)
        rest[-1][...] = acc

    in_specs = [pl.BlockSpec((None, rb, c), lambda i, where_ref: (where_ref[0], i, 0)),
                pl.BlockSpec((N_CHIPS - 1, rb, c), lambda i, where_ref: (0, i, 0))]
    args = [where, part, land]
    if into is not None:
        in_specs.append(ANY)
        args.append(into)
    if col_half:
        out_shape = jax.ShapeDtypeStruct((n_layers, r, 2 * c), F32)
        out_spec = pl.BlockSpec((None, rb, c), lambda i, where_ref: (layer, i, where_ref[1]))
    else:
        out_shape = jax.ShapeDtypeStruct((n_layers, 2, r, c), F32)
        out_spec = pl.BlockSpec((None, None, rb, c), lambda i, where_ref: (layer, where_ref[1], i, 0))
    return pl.pallas_call(
        body, name="rs_add_chips", out_shape=out_shape,
        grid_spec=pltpu.PrefetchScalarGridSpec(
            num_scalar_prefetch=1, grid=(r // rb,), in_specs=in_specs, out_specs=out_spec),
        input_output_aliases={} if into is None else {3: 0},
        compiler_params=pltpu.CompilerParams(dimension_semantics=("parallel",), vmem_limit_bytes=VMEM_LIMIT),
    )(*args)


def _sum_small(smg, own, me):
    def body(me_ref, s_ref, own_ref, o_ref):
        o_ref[...] = jnp.zeros_like(o_ref)
        for j in range(N_DEV):
            @pl.when(me_ref[0] == j)
            def _():
                o_ref[...] += own_ref[...]

            @pl.when(me_ref[0] != j)
            def _():
                o_ref[...] += s_ref[j]

    return pl.pallas_call(
        body, name="rs_sum_small", out_shape=jax.ShapeDtypeStruct(smg.shape[1:], F32),
        grid_spec=pltpu.PrefetchScalarGridSpec(
            num_scalar_prefetch=1, grid=(1,),
            in_specs=[pl.BlockSpec(smg.shape, lambda i, me_ref: (0, 0, 0)), pl.BlockSpec(own.shape, lambda i, me_ref: (0, 0))],
            out_specs=pl.BlockSpec(own.shape, lambda i, me_ref: (0, 0))),
    )(me, smg, own)


def _pool_windows(ext_ref, g, gw, tm, first_row):
    w = POOL_WINDOWS[g]
    slab = ext_ref[:, g * gw:(g + 1) * gw]
    p, k = slab, 1
    while k < w:
        p = p + pltpu.roll(p, k, 0)
        k *= 2
    t = first_row + lax.broadcasted_iota(jnp.int32, (tm, 1), 0)
    cnt = jnp.minimum(t + 1, w).astype(F32)
    return p[POOL_HALO:] / cnt - slab[POOL_HALO:]


def _fwd_pool(x, small, scale, poolw, after=()):
    t, d = x.shape
    tm = _token_tile(t)
    gw = d // len(POOL_WINDOWS)

    def body(x_ref, sm_ref, sc_ref, w_ref, h_ref, ext_ref, mix_ref):
        i = pl.program_id(0)

        @pl.when(i == 0)
        def _():
            ext_ref[0:POOL_HALO, :] = jnp.zeros((POOL_HALO, d), F32)

        @pl.when(i > 0)
        def _():
            ext_ref[0:POOL_HALO, :] = ext_ref[tm:tm + POOL_HALO, :]

        xv = x_ref[...]
        xh, _ = _rms(xv)
        ext_ref[POOL_HALO:, :] = xh * sm_ref[0:1, :]
        for g in range(len(POOL_WINDOWS)):
            pooled = _pool_windows(ext_ref, g, gw, tm, i * tm)
            cols = slice(g * gw, (g + 1) * gw)
            mix_ref[:, cols] = _dot(pooled.astype(BF16), w_ref[g]) * sc_ref[:, cols]
        mh, _ = _rms(mix_ref[...])
        h_ref[...] = xv + mh * sm_ref[1:2, :]

    return _gridded(
        body, after, name="fwd_pool", grid=(t // tm,), out_shape=[jax.ShapeDtypeStruct((t, d), F32)],
        in_specs=[pl.BlockSpec((tm, d), lambda i: (i, 0)), _resident(small.shape, lambda i: (0, 0)),
                  _resident(scale.shape, lambda i: (0, 0)), _resident(poolw.shape, lambda i: (0, 0, 0))],
        out_specs=[pl.BlockSpec((tm, d), lambda i: (i, 0))],
        scratch_shapes=[pltpu.VMEM((POOL_HALO + tm, d), F32), pltpu.VMEM((tm, d), F32)],
        args=[x, small, scale, poolw])[0]


def _fwd_ffn(h, small, wgu, wd, layer, after=(), target=None):
    t, d = h.shape
    tm = _token_tile(t)
    steps = t // tm
    fc = wgu.shape[-1]
    f = 2 * fc
    g_in, g_out = 4 * layer + 2, 4 * layer + 3
    with_loss = target is not None

    def body(h_ref, *refs):
        if with_loss:
            t_ref, sm_ref, wgu_ref, wd_ref, o_ref, gu_ref, ff_ref, n_ref, l_ref, acc_ref = refs
        else:
            sm_ref, wgu_ref, wd_ref, o_ref, gu_ref, ff_ref, n_ref = refs
        hv = h_ref[...]
        hh, _ = _rms(hv)
        n = (hh * sm_ref[g_in:g_in + 1, :]).astype(BF16)
        n_ref[...] = n
        ff = None
        for j in range(2):
            gate = _dot(n, wgu_ref[j])
            up = _dot(n, wgu_ref[2 + j])
            gu_ref[:, j * fc:(j + 1) * fc] = gate.astype(BF16)
            gu_ref[:, f + j * fc:f + (j + 1) * fc] = up.astype(BF16)
            act = (gate * jax.nn.sigmoid(gate) * up).astype(BF16)
            part = _dot(act, wd_ref[j * fc:(j + 1) * fc, :])
            ff = part if ff is None else ff + part
        ff_ref[...] = ff
        fh, _ = _rms(ff)
        out = hv + fh * sm_ref[g_out:g_out + 1, :]
        if not with_loss:
            o_ref[...] = out
            return
        i = pl.program_id(0)
        e = out - t_ref[...]
        o_ref[...] = e * (1.0 / d)

        @pl.when(i == 0)
        def _():
            acc_ref[...] = jnp.zeros_like(acc_ref)

        acc_ref[...] += _colsum(e * e)

        @pl.when(i == steps - 1)
        def _():
            l_ref[...] = jnp.full(l_ref.shape, 0.5 / d, F32) * jnp.sum(acc_ref[...])

    row = lambda i: (i, 0)
    out_shape = [jax.ShapeDtypeStruct((t, d), F32), jax.ShapeDtypeStruct((t, 2 * f), BF16),
                 jax.ShapeDtypeStruct((t, d), F32), jax.ShapeDtypeStruct((t, d), BF16)]
    out_specs = [pl.BlockSpec((tm, d), row), pl.BlockSpec((tm, 2 * f), row), pl.BlockSpec((tm, d), row),
                 pl.BlockSpec((tm, d), row)]
    weight_specs = [_resident(small.shape, lambda i: (0, 0)), _resident(wgu.shape, lambda i: (0, 0, 0)),
                    _resident(wd.shape, lambda i: (0, 0))]
    if with_loss:
        return _gridded(
            body, after, name=f"fwd_ffn{layer}_loss", grid=(steps,),
            out_shape=out_shape + [jax.ShapeDtypeStruct((8, 128), F32)],
            in_specs=[pl.BlockSpec((tm, d), row), pl.BlockSpec((tm, d), row)] + weight_specs,
            out_specs=out_specs + [pl.BlockSpec((8, 128), lambda i: (0, 0))],
            scratch_shapes=[pltpu.VMEM((1, d), F32)], args=[h, target, small, wgu, wd])
    return _gridded(
        body, after, name=f"fwd_ffn{layer}", grid=(steps,), out_shape=out_shape,
        in_specs=[pl.BlockSpec((tm, d), row)] + weight_specs, out_specs=out_specs, args=[h, small, wgu, wd])


def _fwd_ffn_up(h, small, wgu, layer):
    t, d = h.shape
    tm = _token_tile(t)
    fc = wgu.shape[-1]
    f = 2 * fc
    g_in = 4 * layer + 2

    def body(h_ref, sm_ref, wgu_ref, gu_ref, n_ref, act_ref):
        hh, _ = _rms(h_ref[...])
        n = (hh * sm_ref[g_in:g_in + 1, :]).astype(BF16)
        n_ref[...] = n
        for j in range(2):
            gate = _dot(n, wgu_ref[j])
            up = _dot(n, wgu_ref[2 + j])
            gu_ref[:, j * fc:(j + 1) * fc] = gate.astype(BF16)
            gu_ref[:, f + j * fc:f + (j + 1) * fc] = up.astype(BF16)
            act_ref[:, j * fc:(j + 1) * fc] = (gate * jax.nn.sigmoid(gate) * up).astype(BF16)

    row = lambda i: (i, 0)
    return _gridded(
        body, (), name=f"fwd_ffn{layer}_up", grid=(t // tm,),
        out_shape=[jax.ShapeDtypeStruct((t, 2 * f), BF16), jax.ShapeDtypeStruct((t, d), BF16),
                   jax.ShapeDtypeStruct((t, f), BF16)],
        in_specs=[pl.BlockSpec((tm, d), row), _resident(small.shape, lambda i: (0, 0)),
                  _resident(wgu.shape, lambda i: (0, 0, 0))],
        out_specs=[pl.BlockSpec((tm, 2 * f), row), pl.BlockSpec((tm, d), row), pl.BlockSpec((tm, f), row)],
        args=[h, small, wgu])


def _fwd_ffn_down(h, act, small, wd, layer):
    t, d = h.shape
    tm = _token_tile(t)
    f = wd.shape[0]
    g_out = 4 * layer + 3

    def body(h_ref, act_ref, sm_ref, wd_ref, o_ref, ff_ref):
        ff = _dot(act_ref[...], wd_ref[...])
        ff_ref[...] = ff
        fh, _ = _rms(ff)
        o_ref[...] = h_ref[...] + fh * sm_ref[g_out:g_out + 1, :]

    row = lambda i: (i, 0)
    return _gridded(
        body, (), name=f"fwd_ffn{layer}_down", grid=(t // tm,),
        out_shape=[jax.ShapeDtypeStruct((t, d), F32), jax.ShapeDtypeStruct((t, d), F32)],
        in_specs=[pl.BlockSpec((tm, d), row), pl.BlockSpec((tm, f), row), _resident(small.shape, lambda i: (0, 0)),
                  _resident(wd.shape, lambda i: (0, 0))],
        out_specs=[pl.BlockSpec((tm, d), row), pl.BlockSpec((tm, d), row)], args=[h, act, small, wd])


def _fwd_conv(h, small, win, wout, after=()):
    t, d = h.shape
    tm = _token_tile(t)
    pc = win.shape[-1]

    def body(h_ref, sm_ref, win_ref, wout_ref, o_ref, proj_ref, y_ref, n_ref, pj_ref, uext_ref):
        i = pl.program_id(0)

        @pl.when(i == 0)
        def _():
            uext_ref[0:CONV_HALO, :] = jnp.zeros((CONV_HALO, d), F32)

        @pl.when(i > 0)
        def _():
            uext_ref[0:CONV_HALO, :] = uext_ref[tm:tm + CONV_HALO, :]

        hv = h_ref[...]
        hh, _ = _rms(hv)
        n = (hh * sm_ref[4:5, :]).astype(BF16)
        n_ref[...] = n
        for k in range(N_CHIPS):
            pj_ref[:, k * pc:(k + 1) * pc] = _dot(n, win_ref[k])
        proj_ref[...] = pj_ref[...].astype(BF16)
        uext_ref[CONV_HALO:, :] = pj_ref[:, d:2 * d] * pj_ref[:, 2 * d:]
        taps = [sm_ref[8 + j:9 + j, :] for j in range(3)]
        full = uext_ref[...]
        conv = (full[CONV_HALO:] * taps[2] + pltpu.roll(full, 1, 0)[CONV_HALO:] * taps[1]
                + pltpu.roll(full, 2, 0)[CONV_HALO:] * taps[0])
        y = _dot((pj_ref[:, 0:d] * conv).astype(BF16), wout_ref[...])
        y_ref[...] = y
        yh, _ = _rms(y)
        o_ref[...] = hv + yh * sm_ref[5:6, :]

    row = lambda i: (i, 0)
    return _gridded(
        body, after, name="fwd_conv", grid=(t // tm,),
        out_shape=[jax.ShapeDtypeStruct((t, d), F32), jax.ShapeDtypeStruct((t, 3 * d), BF16),
                   jax.ShapeDtypeStruct((t, d), F32), jax.ShapeDtypeStruct((t, d), BF16)],
        in_specs=[pl.BlockSpec((tm, d), row), _resident(small.shape, lambda i: (0, 0)),
                  _resident(win.shape, lambda i: (0, 0, 0)), _resident(wout.shape, lambda i: (0, 0))],
        out_specs=[pl.BlockSpec((tm, d), row), pl.BlockSpec((tm, 3 * d), row), pl.BlockSpec((tm, d), row),
                   pl.BlockSpec((tm, d), row)],
        scratch_shapes=[pltpu.VMEM((tm, 3 * d), F32), pltpu.VMEM((CONV_HALO + tm, d), F32)],
        args=[h, small, win, wout])


def _bwd_ffn(dh, h, ff, gu, small, wgu, wd, layer, after=()):
    t, d = h.shape
    tm = _token_tile(t, FFN_BWD_TOKEN_TILE)
    fc = wgu.shape[-1]
    f = 2 * fc
    g_in, g_out = 4 * layer + 2, 4 * layer + 3

    def body(dh_ref, h_ref, ff_ref, gu_ref, sm_ref, wgu_ref, wd_ref, o_ref, dgu_ref, dff_ref, act_ref, sg_ref):
        i = pl.program_id(0)

        @pl.when(i == 0)
        def _():
            sg_ref[...] = jnp.zeros_like(sg_ref)

        dy = dh_ref[...]
        fh, r3 = _rms(ff_ref[...])
        sg_ref[1:2, :] += _colsum(dy * fh)
        dff = _rms_bwd(dy, fh, r3, sm_ref[g_out:g_out + 1, :]).astype(BF16)
        dff_ref[...] = dff
        for j in range(2):
            dact = _dot_nt(dff, wd_ref[j * fc:(j + 1) * fc, :])
            gate = gu_ref[:, j * fc:(j + 1) * fc].astype(F32)
            up = gu_ref[:, f + j * fc:f + (j + 1) * fc].astype(F32)
            sig = jax.nn.sigmoid(gate)
            silu = gate * sig
            act_ref[:, j * fc:(j + 1) * fc] = (silu * up).astype(BF16)
            dgu_ref[:, j * fc:(j + 1) * fc] = (dact * up * (sig * (1.0 + gate * (1.0 - sig)))).astype(BF16)
            dgu_ref[:, f + j * fc:f + (j + 1) * fc] = (dact * silu).astype(BF16)
        dn = None
        for k in range(N_CHIPS):
            part = _dot_nt(dgu_ref[:, k * fc:(k + 1) * fc], wgu_ref[k])
            dn = part if dn is None else dn + part
        hh, r2 = _rms(h_ref[...])
        sg_ref[0:1, :] += _colsum(dn * hh)
        o_ref[...] = dy + _rms_bwd(dn, hh, r2, sm_ref[g_in:g_in + 1, :])

    row = lambda i: (i, 0)
    return _gridded(
        body, after, name=f"bwd_ffn{layer}", grid=(t // tm,),
        out_shape=[jax.ShapeDtypeStruct((t, d), F32), jax.ShapeDtypeStruct((t, 2 * f), BF16),
                   jax.ShapeDtypeStruct((t, d), BF16), jax.ShapeDtypeStruct((t, f), BF16),
                   jax.ShapeDtypeStruct((8, d), F32)],
        in_specs=[pl.BlockSpec((tm, d), row), pl.BlockSpec((tm, d), row), pl.BlockSpec((tm, d), row),
                  pl.BlockSpec((tm, 2 * f), row), _resident(small.shape, lambda i: (0, 0)),
                  _resident(wgu.shape, lambda i: (0, 0, 0)), _resident(wd.shape, lambda i: (0, 0))],
        out_specs=[pl.BlockSpec((tm, d), row), pl.BlockSpec((tm, 2 * f), row), pl.BlockSpec((tm, d), row),
                   pl.BlockSpec((tm, f), row), pl.BlockSpec((8, d), lambda i: (0, 0))],
        args=[dh, h, ff, gu, small, wgu, wd])


def _bwd_conv(dh, h, y, proj, small, win, wout, after=()):
    t, d = h.shape
    tm = _token_tile(t)
    steps = t // tm
    pc = win.shape[-1]
    halo_blocks = tm // 16

    def body(dh_ref, h_ref, y_ref, proj_ref, halo_ref, sm_ref, win_ref, wout_ref,
             o_ref, dproj_ref, dy_ref, bc_ref, sg_ref, uext_ref, dcext_ref, carry_ref):
        i = pl.program_id(0)
        tile = steps - 1 - i

        @pl.when(i == 0)
        def _():
            sg_ref[...] = jnp.zeros_like(sg_ref)
            carry_ref[...] = jnp.zeros_like(carry_ref)

        dy = dh_ref[...]
        yh, r1 = _rms(y_ref[...])
        sg_ref[1:2, :] += _colsum(dy * yh)
        dyv = _rms_bwd(dy, yh, r1, sm_ref[5:6, :]).astype(BF16)
        dy_ref[...] = dyv
        dbc = _dot_nt(dyv, wout_ref[...])
        b = proj_ref[:, 0:d].astype(F32)
        cg = proj_ref[:, d:2 * d].astype(F32)
        v = proj_ref[:, 2 * d:].astype(F32)
        halo = halo_ref[...].astype(F32)[16 - CONV_HALO:]
        uh = halo[:, d:2 * d] * halo[:, 2 * d:]
        uext_ref[0:CONV_HALO, :] = jnp.where(tile > 0, uh, jnp.zeros_like(uh))
        uext_ref[CONV_HALO:, :] = cg * v
        taps = [sm_ref[8 + j:9 + j, :] for j in range(3)]
        full = uext_ref[...]
        u0 = full[CONV_HALO:]
        u1 = pltpu.roll(full, 1, 0)[CONV_HALO:]
        u2 = pltpu.roll(full, 2, 0)[CONV_HALO:]
        conv = u0 * taps[2] + u1 * taps[1] + u2 * taps[0]
        bc_ref[...] = (b * conv).astype(BF16)
        dconv = dbc * b
        sg_ref[4:5, :] += _colsum(dconv * u0)
        sg_ref[3:4, :] += _colsum(dconv * u1)
        sg_ref[2:3, :] += _colsum(dconv * u2)
        dcext_ref[0:tm, :] = dconv
        dcext_ref[tm:, :] = carry_ref[...]
        carry_ref[...] = dconv[0:CONV_HALO]
        dfull = dcext_ref[...]
        n8 = tm + CONV_HALO
        du = (dfull[0:tm] * taps[2] + pltpu.roll(dfull, n8 - 1, 0)[0:tm] * taps[1]
              + pltpu.roll(dfull, n8 - 2, 0)[0:tm] * taps[0])
        dproj_ref[:, 0:d] = (dbc * conv).astype(BF16)
        dproj_ref[:, d:2 * d] = (du * v).astype(BF16)
        dproj_ref[:, 2 * d:] = (du * cg).astype(BF16)
        dn = None
        for k in range(N_CHIPS):
            part = _dot_nt(dproj_ref[:, k * pc:(k + 1) * pc], win_ref[k])
            dn = part if dn is None else dn + part
        hh, r0 = _rms(h_ref[...])
        sg_ref[0:1, :] += _colsum(dn * hh)
        o_ref[...] = dy + _rms_bwd(dn, hh, r0, sm_ref[4:5, :])

    rev = lambda i: (steps - 1 - i, 0)
    before = lambda i: (jnp.maximum((steps - 1 - i) * halo_blocks - 1, 0), 0)
    return _gridded(
        body, after, name="bwd_conv", grid=(steps,),
        out_shape=[jax.ShapeDtypeStruct((t, d), F32), jax.ShapeDtypeStruct((t, 3 * d), BF16),
                   jax.ShapeDtypeStruct((t, d), BF16), jax.ShapeDtypeStruct((t, d), BF16),
                   jax.ShapeDtypeStruct((8, d), F32)],
        in_specs=[pl.BlockSpec((tm, d), rev), pl.BlockSpec((tm, d), rev), pl.BlockSpec((tm, d), rev),
                  pl.BlockSpec((tm, 3 * d), rev), pl.BlockSpec((16, 3 * d), before),
                  _resident(small.shape, lambda i: (0, 0)), _resident(win.shape, lambda i: (0, 0, 0)),
                  _resident(wout.shape, lambda i: (0, 0))],
        out_specs=[pl.BlockSpec((tm, d), rev), pl.BlockSpec((tm, 3 * d), rev), pl.BlockSpec((tm, d), rev),
                   pl.BlockSpec((tm, d), rev), pl.BlockSpec((8, d), lambda i: (0, 0))],
        scratch_shapes=[pltpu.VMEM((CONV_HALO + tm, d), F32), pltpu.VMEM((tm + CONV_HALO, d), F32),
                        pltpu.VMEM((CONV_HALO, d), F32)],
        args=[dh, h, y, proj, proj, small, win, wout])


def _bwd_pool(dh, x, small, scale, poolw, after=()):
    t, d = x.shape
    tm = _token_tile(t)
    steps = t // tm
    ng = len(POOL_WINDOWS)
    gw = d // ng
    halo_blocks = tm // POOL_HALO

    def body(dh_ref, x_ref, halo_ref, sm_ref, sc_ref, w_ref, o_ref, dw_ref, sg_ref,
             ext_ref, mix_ref, mm_ref, pb_ref, qext_ref, dhn_ref, carry_ref):
        i = pl.program_id(0)
        tile = steps - 1 - i

        @pl.when(i == 0)
        def _():
            sg_ref[...] = jnp.zeros_like(sg_ref)
            dw_ref[...] = jnp.zeros_like(dw_ref)
            carry_ref[...] = jnp.zeros_like(carry_ref)

        g0 = sm_ref[0:1, :]
        xv = x_ref[...]
        xh, r0 = _rms(xv)
        hx, _ = _rms(halo_ref[...])
        ext_ref[0:POOL_HALO, :] = jnp.where(tile > 0, hx * g0, jnp.zeros_like(hx))
        ext_ref[POOL_HALO:, :] = xh * g0
        for g in range(ng):
            pooled = _pool_windows(ext_ref, g, gw, tm, tile * tm)
            cols = slice(g * gw, (g + 1) * gw)
            pb = pooled.astype(BF16)
            pb_ref[:, cols] = pb
            mm = _dot(pb, w_ref[g])
            mm_ref[:, cols] = mm
            mix_ref[:, cols] = mm * sc_ref[:, cols]
        dy = dh_ref[...]
        mh, r1 = _rms(mix_ref[...])
        sg_ref[1:2, :] += _colsum(dy * mh)
        dmix = _rms_bwd(dy, mh, r1, sm_ref[1:2, :])
        sg_ref[2:3, :] += _colsum(dmix * mm_ref[...])
        mix_ref[...] = dmix * sc_ref[...]
        n16 = tm + POOL_HALO
        for g in range(ng):
            w = POOL_WINDOWS[g]
            cols = slice(g * gw, (g + 1) * gw)
            dmm = mix_ref[:, cols].astype(BF16)
            dpooled = _dot_nt(dmm, w_ref[g])
            dw_ref[g] += _dot_tn(pb_ref[:, cols], dmm)
            trow = tile * tm + lax.broadcasted_iota(jnp.int32, (tm, 1), 0)
            q = dpooled / jnp.minimum(trow + 1, w).astype(F32)
            qext_ref[0:tm, cols] = q
            qext_ref[tm:, cols] = carry_ref[:, cols]
            carry_ref[:, cols] = q[0:POOL_HALO]
            p, k = qext_ref[:, cols], 1
            while k < w:
                p = p + pltpu.roll(p, n16 - k, 0)
                k *= 2
            dhn_ref[:, cols] = p[0:tm] - dpooled
        dhn = dhn_ref[...]
        sg_ref[0:1, :] += _colsum(dhn * xh)
        o_ref[...] = dy + _rms_bwd(dhn, xh, r0, g0)

    rev = lambda i: (steps - 1 - i, 0)
    before = lambda i: (jnp.maximum((steps - 1 - i) * halo_blocks - 1, 0), 0)
    return _gridded(
        body, after, name="bwd_pool", grid=(steps,),
        out_shape=[jax.ShapeDtypeStruct((t, d), F32), jax.ShapeDtypeStruct((ng, gw, gw), F32),
                   jax.ShapeDtypeStruct((8, d), F32)],
        in_specs=[pl.BlockSpec((tm, d), rev), pl.BlockSpec((tm, d), rev), pl.BlockSpec((POOL_HALO, d), before),
                  _resident(small.shape, lambda i: (0, 0)), _resident(scale.shape, lambda i: (0, 0)),
                  _resident(poolw.shape, lambda i: (0, 0, 0))],
        out_specs=[pl.BlockSpec((tm, d), rev), pl.BlockSpec((ng, gw, gw), lambda i: (0, 0, 0)),
                   pl.BlockSpec((8, d), lambda i: (0, 0))],
        scratch_shapes=[pltpu.VMEM((POOL_HALO + tm, d), F32), pltpu.VMEM((tm, d), F32), pltpu.VMEM((tm, d), F32),
                        pltpu.VMEM((tm, d), BF16), pltpu.VMEM((tm + POOL_HALO, d), F32), pltpu.VMEM((tm, d), F32),
                        pltpu.VMEM((POOL_HALO, d), F32)],
        args=[dh, x, x, small, scale, poolw])


def _weight_grad(a, b, bm, bn, half_on, name):
    t, m = a.shape
    _, n = b.shape
    if half_on == "a":
        a_cols, b_cols = 2 * bm, bn
    else:
        a_cols, b_cols = bm, 2 * bn
    steps = max(m // a_cols, n // b_cols)

    def spec(cols, total):
        if cols == total:
            return _resident((t, cols), lambda p, j: (0, 0))
        return pl.BlockSpec((t, cols), lambda p, j: (0, j))

    def tile(a_ref, b_ref, half):
        if half_on == "a":
            return _dot_tn(a_ref[:, half * bm:(half + 1) * bm], b_ref[...])
        return _dot_tn(a_ref[...], b_ref[:, half * bn:(half + 1) * bn])

    def body(a_ref, b_ref, parts_ref, land_ref, acc_ref, stage_ref, got_ref, send_sems, recv_sems, got_sem):
        p, j = pl.program_id(0), pl.program_id(1)
        x, y, c, _ = _place()
        half = jnp.where(p == 0, 1 - c, c)

        def send(jj):
            return _remote(stage_ref.at[jj % 2], land_ref.at[jj], send_sems.at[jj], recv_sems.at[jj], (x, y, 1 - c))

        def fetch():
            return pltpu.make_async_copy(land_ref.at[j], got_ref, got_sem)

        @pl.when(p == 1)
        def _():
            @pl.when(j == 0)
            def _():
                for jj in range(max(steps - 2, 0), steps):
                    send(jj).wait_send()

            send(j).wait_recv()
            fetch().start()

        for hv in range(2):
            @pl.when(half == hv)
            def _():
                acc_ref[...] = tile(a_ref, b_ref, hv)

        @pl.when(p == 0)
        def _():
            @pl.when(j >= 2)
            def _():
                send(j - 2).wait_send()

            stage_ref[j % 2] = acc_ref[...].astype(BF16)
            send(j).start()

        @pl.when(p == 1)
        def _():
            fetch().wait()
            parts_ref[...] = (acc_ref[...] + got_ref[...].astype(F32)).astype(BF16)

    return _gridded(
        body, (), name=name, grid=(2, steps),
        out_shape=[jax.ShapeDtypeStruct((steps, bm, bn), BF16), jax.ShapeDtypeStruct((steps, bm, bn), BF16)],
        in_specs=[spec(a_cols, m), spec(b_cols, n)],
        out_specs=[pl.BlockSpec((None, bm, bn), lambda p, j: (p * j, 0, 0)), ANY],
        scratch_shapes=[pltpu.VMEM((bm, bn), F32), pltpu.VMEM((2, bm, bn), BF16), pltpu.VMEM((bm, bn), BF16),
                        DMA((steps,)), DMA((steps,)), DMA],
        args=[a, b])[0]


def _cast_layer(w, layer, name):
    _, r, c = w.shape
    rb = _row_block(r, c * (4 + 2) * 2)

    def body(w_ref, o_ref):
        o_ref[...] = w_ref[...].astype(BF16)

    return pl.pallas_call(
        body, name=name, grid=(r // rb,), out_shape=jax.ShapeDtypeStruct((r, c), BF16),
        in_specs=[pl.BlockSpec((None, rb, c), lambda i: (layer, i, 0))], out_specs=pl.BlockSpec((rb, c), lambda i: (i, 0)),
        compiler_params=pltpu.CompilerParams(dimension_semantics=("parallel",), vmem_limit_bytes=VMEM_LIMIT),
    )(w)


def _adamw_math(w, g, m, v):
    bc1 = 1.0 - ADAM_B1 ** ADAM_STEP
    bc2 = 1.0 - ADAM_B2 ** ADAM_STEP
    nm = ADAM_B1 * m + (1.0 - ADAM_B1) * g
    nv = ADAM_B2 * v + (1.0 - ADAM_B2) * (g * g)
    return -ADAM_LR * ((nm / bc1) / (jnp.sqrt(nv / bc2) + ADAM_EPS) + ADAM_WD * w), nm, nv


def _adamw_small(small_sum, where, gains, taps, scale):
    dq = gains[0].shape[-1]
    d = small_sum.shape[-1]

    def body(where_ref, mine_ref, all_ref, gw, gm, gv, tw, tm_, tv, sw, sm, sv,
             gg, gd, gnm, gnv, tg, td, tnm, tnv, sg, sd, snm, snv):
        for layer in range(gw.shape[0]):
            g = mine_ref[4 * layer:4 * layer + 4, :]
            gg[layer] = g
            gd[layer], gnm[layer], gnv[layer] = _adamw_math(gw[layer], g, gm[layer], gv[layer])
        g = mine_ref[8:8 + tw.shape[1], :]
        tg[0] = g
        td[0], tnm[0], tnv[0] = _adamw_math(tw[0], g, tm_[0], tv[0])
        g = all_ref[11:12, :]
        sg[...] = g
        sd[...], snm[...], snv[...] = _adamw_math(sw[...], g, sm[...], sv[...])

    full = lambda a: pl.BlockSpec(a.shape, lambda i, where_ref: (0,) * a.ndim)
    params = [*gains, *taps, *scale]
    outs = [gains[0]] * 4 + [taps[0]] * 4 + [scale[0]] * 4
    res = pl.pallas_call(
        body, name="adamw_small", out_shape=[jax.ShapeDtypeStruct(a.shape, F32) for a in outs],
        grid_spec=pltpu.PrefetchScalarGridSpec(
            num_scalar_prefetch=1, grid=(1,),
            in_specs=[pl.BlockSpec((16, dq), lambda i, where_ref: (0, where_ref[0])), pl.BlockSpec((16, d), lambda i, where_ref: (0, 0)),
                      *[full(a) for a in params]],
            out_specs=[full(a) for a in outs]),
    )(where, small_sum, small_sum, *params)
    return tuple(res[0:4]), tuple(res[4:8]), tuple(res[8:12])


def _adamw(w, g, m, v, name):
    r, c = w.shape
    rb = _row_block(r, c * (8 * 4 * 2 + 4 * 4))

    def body(w_ref, g_ref, m_ref, v_ref, d_ref, nm_ref, nv_ref, go_ref):
        gv = g_ref[...]
        go_ref[...] = gv
        d_ref[...], nm_ref[...], nv_ref[...] = _adamw_math(w_ref[...], gv, m_ref[...], v_ref[...])

    spec = pl.BlockSpec((rb, c), lambda i: (i, 0))
    return pl.pallas_call(
        body, name=name, grid=(r // rb,), out_shape=[jax.ShapeDtypeStruct((r, c), F32)] * 4,
        in_specs=[spec] * 4, out_specs=[spec] * 4,
        compiler_params=pltpu.CompilerParams(dimension_semantics=("parallel",), vmem_limit_bytes=VMEM_LIMIT),
    )(w, g, m, v)


def kernel(x, norm_gains, pool_w, pool_scale, conv_in_w, conv_w, conv_out_w, ffn_gate_up_w, ffn_down_w, loss_target, m_norm_gains, m_pool_w, m_pool_scale, m_conv_in_w, m_conv_w, m_conv_out_w, m_ffn_gate_up_w, m_ffn_down_w, v_norm_gains, v_pool_w, v_pool_scale, v_conv_in_w, v_conv_w, v_conv_out_w, v_ffn_gate_up_w, v_ffn_down_w):
    _, t, d = x.shape
    dq = d // N_CHIPS
    ng = len(POOL_WINDOWS)
    gw = d // ng
    fq = ffn_down_w.shape[1]
    f = N_CHIPS * fq
    fc = f // 2
    core = lax.axis_index("c")
    chip = 2 * lax.axis_index("x") + lax.axis_index("y")
    core_arr = jnp.reshape(core, (1,)).astype(jnp.int32)
    where_arr = jnp.stack([chip, core]).astype(jnp.int32)
    x2, target = x[0], loss_target[0]

    small_loc = jnp.concatenate(
        [norm_gains.reshape(8, dq), conv_w[0], jnp.zeros((5, dq), F32)], axis=0).reshape(1, 2, 8, dq)
    pool_loc = pool_w.astype(BF16).reshape(1, 2, ng // 2 * (gw // N_CHIPS), gw)
    wgu_loc = [_cast_layer(ffn_gate_up_w, 0, "cast_gate_up0").reshape(1, 2, d // 2, fc),
               ffn_gate_up_w[1:2].astype(BF16).reshape(1, 2, d // 2, fc)]
    wd_loc = [_cast_layer(ffn_down_w, 0, "cast_down0").reshape(1, 2, fq // 2, d),
              ffn_down_w[1:2].astype(BF16).reshape(1, 2, fq // 2, d)]
    win_loc = conv_in_w.astype(BF16).reshape(1, 2, d // 2, -1)
    wout_loc = conv_out_w.astype(BF16).reshape(1, 2, dq // 2, d)

    def ffn_weights(wgu_f, wd_f):
        return wgu_f.reshape(N_CHIPS, d, fc), wd_f.reshape(f, d)

    ag0 = _SplitGather([(pool_loc, 0), (small_loc, 0), (wgu_loc[0], 0), (wd_loc[0], 0)])
    ag0.start("ag_start_layer0")
    ag1 = _SplitGather([(win_loc, 0), (wout_loc, 0), (wgu_loc[1], 0), (wd_loc[1], 0)])
    ag1.start("ag_start_layer1", [ag0.token])
    pool_f, small_f = _pass_on(ag0.wait([0, 1], ag1.token, "ag_wait_first"), "ag_pass_first")
    poolw = pool_f.reshape(N_CHIPS, ng, gw // N_CHIPS, gw).transpose(1, 0, 2, 3).reshape(ng, gw, gw)
    small = small_f.transpose(1, 2, 0, 3).reshape(16, d)
    h1 = _fwd_pool(x2, small, pool_scale, poolw)
    (wgu0,) = _pass_on(ag0.wait([2], h1, "ag_wait_gate_up0"), "ag_pass_gate_up0")
    wgu0 = wgu0.reshape(N_CHIPS, d, fc)
    gu0, n0, act_fwd0 = _fwd_ffn_up(h1, small, wgu0, 0)
    (wd0,) = _pass_on(ag0.wait([3], gu0, "ag_wait_down0"), "ag_pass_down0")
    wd0 = wd0.reshape(f, d)
    h2, ff0 = _fwd_ffn_down(h1, act_fwd0, small, wd0, 0)
    win_f, wout_f = _pass_on(ag1.wait([0, 1], h2, "ag_wait_conv"), "ag_pass_conv")
    win_f, wout_f = win_f.reshape(N_CHIPS, d, -1), wout_f.reshape(d, d)
    h3, proj, y, nc = _fwd_conv(h2, small, win_f, wout_f)
    wgu1, wd1 = ffn_weights(*_pass_on(ag1.wait([2, 3], h3, "ag_wait_ffn1"), "ag_pass_ffn1"))
    dh4, gu1, ff1, n1, loss_blk = _fwd_ffn(h3, small, wgu1, wd1, 1, target=target)

    dh3, dgu1, dff1, act1, sg_f1 = _bwd_ffn(dh4, h3, ff1, gu1, small, wgu1, wd1, 1)
    parts_d1 = _weight_grad(act1, dff1, fc, d // 2, "b", "dw_down1")
    parts_gu1 = _weight_grad(n1, dgu1, d // 2, fc, "a", "dw_gate_up1")
    ex_ffn1 = _SplitExchange([parts_d1.reshape(N_CHIPS, fq, d // 2), parts_gu1])
    started = ex_ffn1.start("rs_start_ffn1")
    dh2, dproj, dyv, bcv, sg_c = _bwd_conv(dh3, h2, y, proj, small, win_f, wout_f, [started])
    parts_in = _weight_grad(nc, dproj, d // 2, 3 * d // N_CHIPS, "a", "dw_conv_in")
    parts_out = _weight_grad(bcv, dyv, dq // 2, d, "a", "dw_conv_out")
    ex_conv = _SplitExchange([parts_in, parts_out])
    started = ex_conv.start("rs_start_conv")
    dh1, dgu0, dff0, act0, sg_f0 = _bwd_ffn(dh2, h1, ff0, gu0, small, wgu0, wd0, 0, [started])
    parts_d0 = _weight_grad(act0, dff0, fc, d // 2, "b", "dw_down0")
    parts_gu0 = _weight_grad(n0, dgu0, d // 2, fc, "a", "dw_gate_up0")
    ex_ffn0 = _SplitExchange([parts_d0.reshape(N_CHIPS, fq, d // 2), parts_gu0])
    started = ex_ffn0.start("rs_start_ffn0")
    grad_x, dpool, sg_p = _bwd_pool(dh1, x2, small, pool_scale, poolw, [started])
    g_pool = dpool.astype(BF16).reshape(2, ng // 2, N_CHIPS, gw // N_CHIPS, gw).transpose(2, 0, 1, 3, 4).reshape(
        N_CHIPS, 2, ng // 2 * (gw // N_CHIPS), gw)
    small_g = jnp.concatenate(
        [sg_p[0:2], sg_f0[0:2], sg_c[0:2], sg_f1[0:2], sg_c[2:5], sg_p[2:3],
         jnp.broadcast_to(loss_blk[0:1, 0:1], (1, d)), jnp.zeros((3, d), F32)], axis=0)
    ex_pool = _SplitExchange([_add_sibling(g_pool, _sibling_exchange(g_pool, "rs_sibling_pool"), core_arr)])
    started = ex_pool.start("rs_start_pool")

    def update(w, g, m, v, name):
        if w.size * 4 * 8 <= STREAM_BUDGET // 4:
            def body(w_ref, g_ref, m_ref, v_ref, go_ref, d_ref, nm_ref, nv_ref):
                go_ref[...] = g_ref[...]
                d_ref[...], nm_ref[...], nv_ref[...] = _adamw_math(w_ref[...], g_ref[...], m_ref[...], v_ref[...])

            return tuple(pl.pallas_call(body, name="adamw_" + name, out_shape=[jax.ShapeDtypeStruct(w.shape, F32)] * 4)(
                w, g.reshape(w.shape), m, v))
        flat = (-1, w.shape[-1])
        dl, m2, v2, g2 = _adamw(w.reshape(flat), g.reshape(flat), m.reshape(flat), v.reshape(flat), "adamw_" + name)
        return tuple(o.reshape(w.shape) for o in (g2, dl, m2, v2))

    (parts_d1, parts_gu1), (recv_d1, recv_gu1) = ex_ffn1.wait([started], "rs_wait_ffn1")
    (parts_in, parts_out), (recv_in, recv_out) = ex_conv.wait([started], "rs_wait_conv")
    gs_gu = _add_chips(parts_gu1, recv_gu1, where_arr, 1, 2)
    gs_d = _add_chips(parts_d1, recv_d1, where_arr, 1, 2, col_half=True)
    gs_in = _add_chips(parts_in, recv_in, where_arr)
    gs_out = _add_chips(parts_out, recv_out, where_arr)
    full_in, full_out = _sibling_share([gs_in, gs_out], [False, False], "rs_share_conv")
    up_in = update(conv_in_w, full_in.reshape(1, d, -1), m_conv_in_w, v_conv_in_w, "conv_in")
    up_out = update(conv_out_w, full_out.reshape(1, dq, d), m_conv_out_w, v_conv_out_w, "conv_out")
    done_first = [up_in[1], up_out[1], gs_gu, gs_d]
    (parts_d0, parts_gu0), (recv_d0, recv_gu0) = ex_ffn0.wait(done_first, "rs_wait_ffn0")
    (parts_p,), (recv_p,) = ex_pool.wait(done_first, "rs_wait_pool")
    ex_small = _SplitExchange([small_g], _SmallGather(small_g))
    started = ex_small.start("rs_start_small", [recv_gu0])
    gs_gu = _add_chips(parts_gu0, recv_gu0, where_arr, 0, 2, gs_gu)
    gs_d = _add_chips(parts_d0, recv_d0, where_arr, 0, 2, gs_d, col_half=True)
    gs_pool = _add_chips(parts_p, recv_p, where_arr)
    full_gu, full_d, full_pool = _sibling_share([gs_gu, gs_d, gs_pool], [False, True, False], "rs_share_ffn", [started])
    up_gu = update(ffn_gate_up_w, full_gu.reshape(2, d, fc), m_ffn_gate_up_w, v_ffn_gate_up_w, "gate_up")
    up_d = update(ffn_down_w, full_d.reshape(2, fq, d), m_ffn_down_w, v_ffn_down_w, "down")
    (small_own,), (small_all,) = ex_small.wait([up_gu[1], up_d[1]], "rs_wait_small")
    small_sum = _sum_small(small_all, small_own, (2 * chip + core).reshape(1).astype(jnp.int32))
    loss = small_sum[12, 0]
    up_gains, up_taps, up_scale = _adamw_small(
        small_sum, where_arr, (norm_gains, m_norm_gains, v_norm_gains), (conv_w, m_conv_w, v_conv_w),
        (pool_scale, m_pool_scale, v_pool_scale))

    ups = [
        up_gains,
        update(pool_w, full_pool.reshape(1, ng, gw // N_CHIPS, gw), m_pool_w, v_pool_w, "pool_w"),
        up_scale,
        up_in,
        up_taps,
        up_out,
        up_gu,
        up_d,
    ]
    grads_out, deltas, new_ms, new_vs = (list(col) for col in zip(*ups))
    return (loss, grad_x[None], *grads_out, *deltas, *new_ms, *new_vs)
```

```python
import jax
import jax.numpy as jnp
from jax import lax
from jax.experimental import pallas as pl
from jax.experimental.pallas import tpu as pltpu

RMS_EPS = 1e-6
POOL_WINDOWS = (2, 4, 8, 16)
POOL_HALO = 16
CONV_HALO = 8
N_CHIPS = 4
N_DEV = 8
ADAM_LR = 0.001
ADAM_B1 = 0.9
ADAM_B2 = 0.999
ADAM_EPS = 1e-08
ADAM_WD = 0.01
ADAM_STEP = 10
VMEM_LIMIT = 56 * 2**20
STREAM_BUDGET = 24 * 2**20
MESH = pl.DeviceIdType.MESH
ANY = pl.BlockSpec(memory_space=pl.ANY)
DMA = pltpu.SemaphoreType.DMA
BF16 = jnp.bfloat16
F32 = jnp.float32


TOKEN_TILE = 512
FFN_BWD_TOKEN_TILE = 256


def _token_tile(t, rows=TOKEN_TILE):
    return min(rows, t)


def _rms(x):
    r = lax.rsqrt(jnp.mean(x * x, axis=-1, keepdims=True) + RMS_EPS)
    return x * r, r


def _rms_bwd(dy, xh, r, g):
    a = dy * g
    return r * (a - xh * jnp.mean(a * xh, axis=-1, keepdims=True))


def _dot(a, b):
    return jnp.dot(a, b, preferred_element_type=F32)


def _dot_nt(a, b):
    return lax.dot_general(a, b, (((1,), (1,)), ((), ())), preferred_element_type=F32)


def _dot_tn(a, b):
    return lax.dot_general(a, b, (((0,), (0,)), ((), ())), preferred_element_type=F32)


def _colsum(a):
    return jnp.sum(a, axis=0, keepdims=True)


def _resident(block, index_map):
    return pl.BlockSpec(block, index_map, pipeline_mode=pl.Buffered(1))


def _row_block(r, row_bytes):
    best = None
    for rb in range(16, r + 1, 16):
        if r % rb == 0 and rb * row_bytes <= STREAM_BUDGET:
            best = rb
    return best if best is not None else r


def _place():
    x, y, c = lax.axis_index("x"), lax.axis_index("y"), lax.axis_index("c")
    return x, y, c, 2 * x + y


def _dev(chip, core):
    return (chip // 2, chip % 2, core)


def _remote(src, dst, send_sem, recv_sem, device):
    return pltpu.make_async_remote_copy(src_ref=src, dst_ref=dst, send_sem=send_sem, recv_sem=recv_sem,
                                        device_id=device, device_id_type=MESH)


class _Gather:
    def __init__(self, shards):
        self.args = [s for s, _ in shards]
        self.layers = [l for _, l in shards]
        self.out_shape = [jax.ShapeDtypeStruct((N_CHIPS,) + s.shape[1:], s.dtype) for s in self.args]

    def _own(self, loc, out, sems, a):
        x, y, c, k = _place()
        return _remote(loc[a].at[self.layers[a]], out[a].at[k], sems[0].at[a], sems[1].at[a], (x, y, 1 - c))

    def _ici(self, loc, out, sems, a, m, arrival):
        x, y, c, k = _place()
        dst = out[a].at[k ^ m, c] if arrival else out[a].at[k, c]
        return _remote(loc[a].at[self.layers[a], c], dst, sems[2].at[a, m - 1], sems[3].at[a, m - 1], _dev(k ^ m, c))

    def start(self, loc, out, sems):
        for a in range(len(self.args)):
            for m in range(1, N_CHIPS):
                self._ici(loc, out, sems, a, m, False).start()
            self._own(loc, out, sems, a).start()


class _SmallGather:
    PEERS = N_DEV - 1

    def __init__(self, small):
        self.args = [small]
        self.out_shape = [jax.ShapeDtypeStruct((N_DEV,) + small.shape, small.dtype)]

    def _copy(self, sm, land, sems, m, arrival):
        x, y, c, k = _place()
        me = 2 * k + c
        peer = me ^ m
        return _remote(sm[0], land[0].at[peer if arrival else me], sems[0].at[0, m - 1], sems[1].at[0, m - 1],
                       (peer // 4, (peer // 2) % 2, peer % 2))

    def start(self, sm, land, sems):
        for m in range(1, N_DEV):
            self._copy(sm, land, sems, m, False).start()

    def finish(self, sm, land, sems):
        for m in range(1, N_DEV):
            self._copy(sm, land, sems, m, True).wait_recv()
        for m in range(1, N_DEV):
            self._copy(sm, land, sems, m, False).wait_send()


class _ChipExchange:
    PEERS = N_CHIPS - 1

    def __init__(self, parts):
        self.args = list(parts)
        self.out_shape = [jax.ShapeDtypeStruct((N_CHIPS - 1,) + p.shape[1:], p.dtype) for p in parts]

    def _copy(self, p, land, sems, a, m):
        x, y, c, k = _place()
        return _remote(p[a].at[k ^ m], land[a].at[m - 1], sems[0].at[a, m - 1], sems[1].at[a, m - 1], _dev(k ^ m, c))

    def start(self, p, land, sems):
        for a in range(len(self.args)):
            for m in range(1, N_CHIPS):
                self._copy(p, land, sems, a, m).start()

    def finish(self, p, land, sems):
        for a in range(len(self.args)):
            for m in range(1, N_CHIPS):
                self._copy(p, land, sems, a, m).wait_recv()
        for a in range(len(self.args)):
            for m in range(1, N_CHIPS):
                self._copy(p, land, sems, a, m).wait_send()


def _gridded(body, after, *, name, grid, in_specs, out_specs, out_shape, args, scratch_shapes=()):
    ni, na = len(in_specs), len(after)

    def full(*refs):
        body(*refs[:ni], *refs[ni + na:])

    return list(pl.pallas_call(
        full, name=name, grid=grid, in_specs=[*in_specs, *[ANY] * na], out_specs=list(out_specs),
        out_shape=list(out_shape), scratch_shapes=list(scratch_shapes),
        compiler_params=pltpu.CompilerParams(dimension_semantics=("arbitrary",) * len(grid), vmem_limit_bytes=VMEM_LIMIT),
    )(*args, *after))


HBM = pl.BlockSpec(memory_space=pltpu.HBM)
SEM = pl.BlockSpec(memory_space=pltpu.SEMAPHORE)
DATAFLOW = pltpu.SideEffectType.DATAFLOW_SIDE_EFFECTING


class _SplitGather:
    PER_ARRAY = 8

    def __init__(self, shards):
        self.plan = _Gather(shards)
        self.n = len(shards)

    @staticmethod
    def _tables(sems_of):
        class Table:
            def __init__(self, pick):
                self.pick = pick

            @property
            def at(self):
                return self

            def __getitem__(self, idx):
                return self.pick(idx)

        return [Table(lambda a: sems_of[a][0]), Table(lambda a: sems_of[a][1]),
                Table(lambda am: sems_of[am[0]][2 + am[1]]), Table(lambda am: sems_of[am[0]][5 + am[1]])]

    def start(self, name, after=()):
        n, plan, per, na = self.n, self.plan, self.PER_ARRAY, len(after)

        def body(*refs):
            loc, land = refs[:n], refs[n:2 * n]
            sems_of = {a: refs[2 * n + na + per * a:2 * n + na + per * (a + 1)] for a in range(n)}
            plan.start(loc, land, self._tables(sems_of))
            refs[-1][...] = jnp.zeros_like(refs[-1])

        lands = [pltpu.with_memory_space_constraint(lax.empty(o.shape, o.dtype), pltpu.HBM) for o in plan.out_shape]
        locs = [pltpu.with_memory_space_constraint(a, pltpu.HBM) for a in plan.args]
        res = pl.pallas_call(
            body, name=name,
            out_shape=[*[DMA(())] * (per * n),
                       *[pltpu.HBM(o.shape, o.dtype) for o in plan.out_shape],
                       jax.ShapeDtypeStruct((8, 128), F32)],
            in_specs=[HBM] * (2 * n) + [pl.BlockSpec(memory_space=pl.ANY)] * na,
            out_specs=[SEM] * (per * n) + [HBM] * n + [pl.BlockSpec(memory_space=pltpu.VMEM)],
            input_output_aliases={n + i: per * n + i for i in range(n)},
            compiler_params=pltpu.CompilerParams(has_side_effects=DATAFLOW),
        )(*locs, *lands, *after)
        self.sems = {a: list(res[per * a:per * (a + 1)]) for a in range(n)}
        self.locs = locs
        self.lands = list(res[per * n:per * n + n])
        self.token = res[-1]

    def wait(self, idxs, after, name):
        plan, g, per = self.plan, len(idxs), self.PER_ARRAY

        def body(*refs):
            loc = {a: refs[j] for j, a in enumerate(idxs)}
            land = {a: refs[g + j] for j, a in enumerate(idxs)}
            sems = self._tables({a: refs[2 * g + per * j:2 * g + per * (j + 1)] for j, a in enumerate(idxs)})
            for a in idxs:
                for m in range(1, N_CHIPS):
                    plan._ici(loc, land, sems, a, m, True).wait_recv()
                    plan._ici(loc, land, sems, a, m, False).wait_send()
                plan._own(loc, land, sems, a).wait_recv()
                plan._own(loc, land, sems, a).wait_send()

        res = pl.pallas_call(
            body, name=name,
            out_shape=[pltpu.HBM(self.lands[a].shape, self.lands[a].dtype) for a in idxs],
            in_specs=[HBM] * (2 * g) + [SEM] * (per * g) + [pl.BlockSpec(memory_space=pl.ANY)], out_specs=[HBM] * g,
            input_output_aliases={g + j: j for j in range(g)},
            compiler_params=pltpu.CompilerParams(has_side_effects=DATAFLOW),
        )(*[self.locs[a] for a in idxs], *[self.lands[a] for a in idxs],
          *[s for a in idxs for s in self.sems[a]], after)
        return list(res)


class _SplitExchange:
    def __init__(self, parts, plan=None):
        self.plan = _ChipExchange(parts) if plan is None else plan
        self.n = len(parts)
        self.PER_ARRAY = 2 * self.plan.PEERS

    def _tables(self, sems_of):
        class Table:
            def __init__(self, pick):
                self.pick = pick

            @property
            def at(self):
                return self

            def __getitem__(self, am):
                return self.pick(am)

        peers = self.plan.PEERS
        return [Table(lambda am: sems_of[am[0]][am[1]]), Table(lambda am: sems_of[am[0]][peers + am[1]])]

    def start(self, name, after=()):
        n, plan, per, na = self.n, self.plan, self.PER_ARRAY, len(after)

        def body(*refs):
            p, land = refs[:n], refs[n:2 * n]
            sems_of = {a: refs[2 * n + na + per * a:2 * n + na + per * (a + 1)] for a in range(n)}
            plan.start(p, land, self._tables(sems_of))
            refs[-1][...] = jnp.zeros_like(refs[-1])

        lands = [pltpu.with_memory_space_constraint(lax.empty(o.shape, o.dtype), pltpu.HBM) for o in plan.out_shape]
        parts = [pltpu.with_memory_space_constraint(a, pltpu.HBM) for a in plan.args]
        res = pl.pallas_call(
            body, name=name,
            out_shape=[*[DMA(())] * (per * n),
                       *[pltpu.HBM(a.shape, a.dtype) for a in plan.args],
                       *[pltpu.HBM(o.shape, o.dtype) for o in plan.out_shape],
                       jax.ShapeDtypeStruct((8, 128), F32)],
            in_specs=[HBM] * (2 * n) + [pl.BlockSpec(memory_space=pl.ANY)] * na,
            out_specs=[SEM] * (per * n) + [HBM] * (2 * n) + [pl.BlockSpec(memory_space=pltpu.VMEM)],
            input_output_aliases={i: per * n + i for i in range(2 * n)},
            compiler_params=pltpu.CompilerParams(has_side_effects=DATAFLOW),
        )(*parts, *lands, *after)
        self.sems = list(res[:per * n])
        self.parts = list(res[per * n:per * n + n])
        self.lands = list(res[per * n + n:per * n + 2 * n])
        return res[-1]

    def wait(self, after, name):
        n, plan, per = self.n, self.plan, self.PER_ARRAY

        def body(*refs):
            p, land = refs[:n], refs[n:2 * n]
            sems_of = {a: refs[2 * n + per * a:2 * n + per * (a + 1)] for a in range(n)}
            plan.finish(p, land, self._tables(sems_of))

        res = pl.pallas_call(
            body, name=name,
            out_shape=[*[pltpu.HBM(a.shape, a.dtype) for a in self.parts], *[pltpu.HBM(a.shape, a.dtype) for a in self.lands]],
            in_specs=[HBM] * (2 * n) + [SEM] * (per * n) + [pl.BlockSpec(memory_space=pl.ANY)] * len(after),
            out_specs=[HBM] * (2 * n), input_output_aliases={i: i for i in range(2 * n)},
            compiler_params=pltpu.CompilerParams(has_side_effects=DATAFLOW),
        )(*self.parts, *self.lands, *self.sems, *after)
        return list(res[:n]), list(res[n:])


PASS_ON_BARRIER = 1


def _sibling_barrier():
    x, y, c, _ = _place()
    barrier = pltpu.get_barrier_semaphore()
    pl.semaphore_signal(barrier, inc=1, device_id=(x, y, 1 - c), device_id_type=MESH)
    pl.semaphore_wait(barrier, 1)


def _pass_on(lands, name):
    n = len(lands)

    def body(*refs):
        out = refs[n:2 * n]
        send_sems, recv_sems = refs[2 * n:]
        x, y, c, k = _place()
        _sibling_barrier()
        cps = []
        for a in range(n):
            for m in range(1, N_CHIPS):
                got = out[a].at[k ^ m, c]
                cp = _remote(got, got, send_sems.at[a, m - 1], recv_sems.at[a, m - 1], (x, y, 1 - c))
                cp.start()
                cps.append(cp)
        for a in range(n):
            for m in range(1, N_CHIPS):
                theirs = out[a].at[k ^ m, 1 - c]
                _remote(theirs, theirs, send_sems.at[a, m - 1], recv_sems.at[a, m - 1], (x, y, 1 - c)).wait_recv()
        for cp in cps:
            cp.wait_send()

    return pl.pallas_call(
        body, name=name, out_shape=[jax.ShapeDtypeStruct(a.shape, a.dtype) for a in lands],
        in_specs=[ANY] * n, out_specs=[ANY] * n, input_output_aliases={a: a for a in range(n)},
        scratch_shapes=[DMA((n, 3)), DMA((n, 3))],
        compiler_params=pltpu.CompilerParams(has_side_effects=True, collective_id=PASS_ON_BARRIER),
    )(*lands)


def _sibling_exchange(g, name):
    def body(g_ref, land_ref, send_sem, recv_sem):
        x, y, c, _ = _place()
        cp = _remote(g_ref.at[:, pl.ds(1 - c, 1)], land_ref, send_sem, recv_sem, (x, y, 1 - c))
        cp.start()
        cp.wait_recv()
        cp.wait_send()

    return pl.pallas_call(
        body, name=name, out_shape=jax.ShapeDtypeStruct((N_CHIPS, 1) + g.shape[2:], g.dtype), in_specs=[ANY],
        out_specs=ANY, scratch_shapes=[DMA, DMA], compiler_params=pltpu.CompilerParams(has_side_effects=True),
    )(g)


def _sibling_share(halves, col_half, name, after=()):
    n, na = len(halves), len(after)

    def body(*refs):
        out = refs[n + na:2 * n + na]
        send_sems, recv_sems = refs[2 * n + na:]
        x, y, c, k = _place()

        def half(a, core):
            if not col_half[a]:
                return out[a].at[:, pl.ds(core, 1)]
            cols = out[a].shape[-1] // 2
            return out[a].at[:, :, pl.ds(pl.multiple_of(core * cols, cols), cols)]

        cps = []
        for a in range(n):
            cp = _remote(half(a, c), half(a, c), send_sems.at[a], recv_sems.at[a], (x, y, 1 - c))
            cp.start()
            cps.append(cp)
        for a in range(n):
            _remote(half(a, 1 - c), half(a, 1 - c), send_sems.at[a], recv_sems.at[a], (x, y, 1 - c)).wait_recv()
        for cp in cps:
            cp.wait_send()

    out_shape = [jax.ShapeDtypeStruct(a.shape, a.dtype) for a in halves]
    return pl.pallas_call(
        body, name=name, out_shape=out_shape, in_specs=[ANY] * (n + na), out_specs=[ANY] * n,
        input_output_aliases={a: a for a in range(n)}, scratch_shapes=[DMA((n,)), DMA((n,))],
        compiler_params=pltpu.CompilerParams(has_side_effects=True),
    )(*halves, *after)


def _add_sibling(g, land, core):
    _, _, r, c = g.shape
    rb = _row_block(r, c * (3 * 2 * 2 + 2 * 4))

    def body(core_ref, g_ref, l_ref, o_ref):
        o_ref[...] = (g_ref[...].astype(F32) + l_ref[...].astype(F32)).astype(o_ref.dtype)

    return pl.pallas_call(
        body, name="rs_add_sibling", out_shape=jax.ShapeDtypeStruct((N_CHIPS, r, c), g.dtype),
        grid_spec=pltpu.PrefetchScalarGridSpec(
            num_scalar_prefetch=1, grid=(N_CHIPS, r // rb),
            in_specs=[pl.BlockSpec((None, None, rb, c), lambda j, i, core_ref: (j, core_ref[0], i, 0)),
                      pl.BlockSpec((None, None, rb, c), lambda j, i, core_ref: (j, 0, i, 0))],
            out_specs=pl.BlockSpec((None, rb, c), lambda j, i, core_ref: (j, i, 0))),
        compiler_params=pltpu.CompilerParams(dimension_semantics=("parallel", "parallel"), vmem_limit_bytes=VMEM_LIMIT),
    )(core, g, land)


def _add_chips(part, land, where, layer=0, n_layers=1, into=None, col_half=False):
    _, r, c = part.shape
    rb = _row_block(r, c * (4 * 2 * 2 + 4 * 2 + 2 * 4))

    def body(where_ref, p_ref, l_ref, *rest):
        acc = p_ref[...].astype(F32)
        for m in range(N_CHIPS - 1):
            acc = acc + l_ref[m].astype(F32)
        rest[-1][...] = acc

    in_specs = [pl.BlockSpec((None, rb, c), lambda i, where_ref: (where_ref[0], i, 0)),
                pl.BlockSpec((N_CHIPS - 1, rb, c), lambda i, where_ref: (0, i, 0))]
    args = [where, part, land]
    if into is not None:
        in_specs.append(ANY)
        args.append(into)
    if col_half:
        out_shape = jax.ShapeDtypeStruct((n_layers, r, 2 * c), F32)
        out_spec = pl.BlockSpec((None, rb, c), lambda i, where_ref: (layer, i, where_ref[1]))
    else:
        out_shape = jax.ShapeDtypeStruct((n_layers, 2, r, c), F32)
        out_spec = pl.BlockSpec((None, None, rb, c), lambda i, where_ref: (layer, where_ref[1], i, 0))
    return pl.pallas_call(
        body, name="rs_add_chips", out_shape=out_shape,
        grid_spec=pltpu.PrefetchScalarGridSpec(
            num_scalar_prefetch=1, grid=(r // rb,), in_specs=in_specs, out_specs=out_spec),
        input_output_aliases={} if into is None else {3: 0},
        compiler_params=pltpu.CompilerParams(dimension_semantics=("parallel",), vmem_limit_bytes=VMEM_LIMIT),
    )(*args)


def _sum_small(smg, own, me):
    def body(me_ref, s_ref, own_ref, o_ref):
        o_ref[...] = jnp.zeros_like(o_ref)
        for j in range(N_DEV):
            @pl.when(me_ref[0] == j)
            def _():
                o_ref[...] += own_ref[...]

            @pl.when(me_ref[0] != j)
            def _():
                o_ref[...] += s_ref[j]

    return pl.pallas_call(
        body, name="rs_sum_small", out_shape=jax.ShapeDtypeStruct(smg.shape[1:], F32),
        grid_spec=pltpu.PrefetchScalarGridSpec(
            num_scalar_prefetch=1, grid=(1,),
            in_specs=[pl.BlockSpec(smg.shape, lambda i, me_ref: (0, 0, 0)), pl.BlockSpec(own.shape, lambda i, me_ref: (0, 0))],
            out_specs=pl.BlockSpec(own.shape, lambda i, me_ref: (0, 0))),
    )(me, smg, own)


def _pool_windows(ext_ref, g, gw, tm, first_row):
    w = POOL_WINDOWS[g]
    slab = ext_ref[:, g * gw:(g + 1) * gw]
    p, k = slab, 1
    while k < w:
        p = p + pltpu.roll(p, k, 0)
        k *= 2
    t = first_row + lax.broadcasted_iota(jnp.int32, (tm, 1), 0)
    cnt = jnp.minimum(t + 1, w).astype(F32)
    return p[POOL_HALO:] / cnt - slab[POOL_HALO:]


def _fwd_pool(x, small, scale, poolw, after=()):
    t, d = x.shape
    tm = _token_tile(t)
    gw = d // len(POOL_WINDOWS)

    def body(x_ref, sm_ref, sc_ref, w_ref, h_ref, ext_ref, mix_ref):
        i = pl.program_id(0)

        @pl.when(i == 0)
        def _():
            ext_ref[0:POOL_HALO, :] = jnp.zeros((POOL_HALO, d), F32)

        @pl.when(i > 0)
        def _():
            ext_ref[0:POOL_HALO, :] = ext_ref[tm:tm + POOL_HALO, :]

        xv = x_ref[...]
        xh, _ = _rms(xv)
        ext_ref[POOL_HALO:, :] = xh * sm_ref[0:1, :]
        for g in range(len(POOL_WINDOWS)):
            pooled = _pool_windows(ext_ref, g, gw, tm, i * tm)
            cols = slice(g * gw, (g + 1) * gw)
            mix_ref[:, cols] = _dot(pooled.astype(BF16), w_ref[g]) * sc_ref[:, cols]
        mh, _ = _rms(mix_ref[...])
        h_ref[...] = xv + mh * sm_ref[1:2, :]

    return _gridded(
        body, after, name="fwd_pool", grid=(t // tm,), out_shape=[jax.ShapeDtypeStruct((t, d), F32)],
        in_specs=[pl.BlockSpec((tm, d), lambda i: (i, 0)), _resident(small.shape, lambda i: (0, 0)),
                  _resident(scale.shape, lambda i: (0, 0)), _resident(poolw.shape, lambda i: (0, 0, 0))],
        out_specs=[pl.BlockSpec((tm, d), lambda i: (i, 0))],
        scratch_shapes=[pltpu.VMEM((POOL_HALO + tm, d), F32), pltpu.VMEM((tm, d), F32)],
        args=[x, small, scale, poolw])[0]


def _fwd_ffn(h, small, wgu, wd, layer, after=(), target=None):
    t, d = h.shape
    tm = _token_tile(t)
    steps = t // tm
    fc = wgu.shape[-1]
    f = 2 * fc
    g_in, g_out = 4 * layer + 2, 4 * layer + 3
    with_loss = target is not None

    def body(h_ref, *refs):
        if with_loss:
            t_ref, sm_ref, wgu_ref, wd_ref, o_ref, gu_ref, ff_ref, n_ref, l_ref, acc_ref = refs
        else:
            sm_ref, wgu_ref, wd_ref, o_ref, gu_ref, ff_ref, n_ref = refs
        hv = h_ref[...]
        hh, _ = _rms(hv)
        n = (hh * sm_ref[g_in:g_in + 1, :]).astype(BF16)
        n_ref[...] = n
        ff = None
        for j in range(2):
            gate = _dot(n, wgu_ref[j])
            up = _dot(n, wgu_ref[2 + j])
            gu_ref[:, j * fc:(j + 1) * fc] = gate.astype(BF16)
            gu_ref[:, f + j * fc:f + (j + 1) * fc] = up.astype(BF16)
            act = (gate * jax.nn.sigmoid(gate) * up).astype(BF16)
            part = _dot(act, wd_ref[j * fc:(j + 1) * fc, :])
            ff = part if ff is None else ff + part
        ff_ref[...] = ff
        fh, _ = _rms(ff)
        out = hv + fh * sm_ref[g_out:g_out + 1, :]
        if not with_loss:
            o_ref[...] = out
            return
        i = pl.program_id(0)
        e = out - t_ref[...]
        o_ref[...] = e * (1.0 / d)

        @pl.when(i == 0)
        def _():
            acc_ref[...] = jnp.zeros_like(acc_ref)

        acc_ref[...] += _colsum(e * e)

        @pl.when(i == steps - 1)
        def _():
            l_ref[...] = jnp.full(l_ref.shape, 0.5 / d, F32) * jnp.sum(acc_ref[...])

    row = lambda i: (i, 0)
    out_shape = [jax.ShapeDtypeStruct((t, d), F32), jax.ShapeDtypeStruct((t, 2 * f), BF16),
                 jax.ShapeDtypeStruct((t, d), F32), jax.ShapeDtypeStruct((t, d), BF16)]
    out_specs = [pl.BlockSpec((tm, d), row), pl.BlockSpec((tm, 2 * f), row), pl.BlockSpec((tm, d), row),
                 pl.BlockSpec((tm, d), row)]
    weight_specs = [_resident(small.shape, lambda i: (0, 0)), _resident(wgu.shape, lambda i: (0, 0, 0)),
                    _resident(wd.shape, lambda i: (0, 0))]
    if with_loss:
        return _gridded(
            body, after, name=f"fwd_ffn{layer}_loss", grid=(steps,),
            out_shape=out_shape + [jax.ShapeDtypeStruct((8, 128), F32)],
            in_specs=[pl.BlockSpec((tm, d), row), pl.BlockSpec((tm, d), row)] + weight_specs,
            out_specs=out_specs + [pl.BlockSpec((8, 128), lambda i: (0, 0))],
            scratch_shapes=[pltpu.VMEM((1, d), F32)], args=[h, target, small, wgu, wd])
    return _gridded(
        body, after, name=f"fwd_ffn{layer}", grid=(steps,), out_shape=out_shape,
        in_specs=[pl.BlockSpec((tm, d), row)] + weight_specs, out_specs=out_specs, args=[h, small, wgu, wd])


def _fwd_ffn_up(h, small, wgu, layer):
    t, d = h.shape
    tm = _token_tile(t)
    fc = wgu.shape[-1]
    f = 2 * fc
    g_in = 4 * layer + 2

    def body(h_ref, sm_ref, wgu_ref, gu_ref, n_ref, act_ref):
        hh, _ = _rms(h_ref[...])
        n = (hh * sm_ref[g_in:g_in + 1, :]).astype(BF16)
        n_ref[...] = n
        for j in range(2):
            gate = _dot(n, wgu_ref[j])
            up = _dot(n, wgu_ref[2 + j])
            gu_ref[:, j * fc:(j + 1) * fc] = gate.astype(BF16)
            gu_ref[:, f + j * fc:f + (j + 1) * fc] = up.astype(BF16)
            act_ref[:, j * fc:(j + 1) * fc] = (gate * jax.nn.sigmoid(gate) * up).astype(BF16)

    row = lambda i: (i, 0)
    return _gridded(
        body, (), name=f"fwd_ffn{layer}_up", grid=(t // tm,),
        out_shape=[jax.ShapeDtypeStruct((t, 2 * f), BF16), jax.ShapeDtypeStruct((t, d), BF16),
                   jax.ShapeDtypeStruct((t, f), BF16)],
        in_specs=[pl.BlockSpec((tm, d), row), _resident(small.shape, lambda i: (0, 0)),
                  _resident(wgu.shape, lambda i: (0, 0, 0))],
        out_specs=[pl.BlockSpec((tm, 2 * f), row), pl.BlockSpec((tm, d), row), pl.BlockSpec((tm, f), row)],
        args=[h, small, wgu])


def _fwd_ffn_down(h, act, small, wd, layer):
    t, d = h.shape
    tm = _token_tile(t)
    f = wd.shape[0]
    g_out = 4 * layer + 3

    def body(h_ref, act_ref, sm_ref, wd_ref, o_ref, ff_ref):
        ff = _dot(act_ref[...], wd_ref[...])
        ff_ref[...] = ff
        fh, _ = _rms(ff)
        o_ref[...] = h_ref[...] + fh * sm_ref[g_out:g_out + 1, :]

    row = lambda i: (i, 0)
    return _gridded(
        body, (), name=f"fwd_ffn{layer}_down", grid=(t // tm,),
        out_shape=[jax.ShapeDtypeStruct((t, d), F32), jax.ShapeDtypeStruct((t, d), F32)],
        in_specs=[pl.BlockSpec((tm, d), row), pl.BlockSpec((tm, f), row), _resident(small.shape, lambda i: (0, 0)),
                  _resident(wd.shape, lambda i: (0, 0))],
        out_specs=[pl.BlockSpec((tm, d), row), pl.BlockSpec((tm, d), row)], args=[h, act, small, wd])


def _fwd_conv(h, small, win, wout, after=()):
    t, d = h.shape
    tm = _token_tile(t)
    pc = win.shape[-1]

    def body(h_ref, sm_ref, win_ref, wout_ref, o_ref, proj_ref, y_ref, n_ref, pj_ref, uext_ref):
        i = pl.program_id(0)

        @pl.when(i == 0)
        def _():
            uext_ref[0:CONV_HALO, :] = jnp.zeros((CONV_HALO, d), F32)

        @pl.when(i > 0)
        def _():
            uext_ref[0:CONV_HALO, :] = uext_ref[tm:tm + CONV_HALO, :]

        hv = h_ref[...]
        hh, _ = _rms(hv)
        n = (hh * sm_ref[4:5, :]).astype(BF16)
        n_ref[...] = n
        for k in range(N_CHIPS):
            pj_ref[:, k * pc:(k + 1) * pc] = _dot(n, win_ref[k])
        proj_ref[...] = pj_ref[...].astype(BF16)
        uext_ref[CONV_HALO:, :] = pj_ref[:, d:2 * d] * pj_ref[:, 2 * d:]
        taps = [sm_ref[8 + j:9 + j, :] for j in range(3)]
        full = uext_ref[...]
        conv = (full[CONV_HALO:] * taps[2] + pltpu.roll(full, 1, 0)[CONV_HALO:] * taps[1]
                + pltpu.roll(full, 2, 0)[CONV_HALO:] * taps[0])
        y = _dot((pj_ref[:, 0:d] * conv).astype(BF16), wout_ref[...])
        y_ref[...] = y
        yh, _ = _rms(y)
        o_ref[...] = hv + yh * sm_ref[5:6, :]

    row = lambda i: (i, 0)
    return _gridded(
        body, after, name="fwd_conv", grid=(t // tm,),
        out_shape=[jax.ShapeDtypeStruct((t, d), F32), jax.ShapeDtypeStruct((t, 3 * d), BF16),
                   jax.ShapeDtypeStruct((t, d), F32), jax.ShapeDtypeStruct((t, d), BF16)],
        in_specs=[pl.BlockSpec((tm, d), row), _resident(small.shape, lambda i: (0, 0)),
                  _resident(win.shape, lambda i: (0, 0, 0)), _resident(wout.shape, lambda i: (0, 0))],
        out_specs=[pl.BlockSpec((tm, d), row), pl.BlockSpec((tm, 3 * d), row), pl.BlockSpec((tm, d), row),
                   pl.BlockSpec((tm, d), row)],
        scratch_shapes=[pltpu.VMEM((tm, 3 * d), F32), pltpu.VMEM((CONV_HALO + tm, d), F32)],
        args=[h, small, win, wout])


def _bwd_ffn(dh, h, ff, gu, small, wgu, wd, layer, after=()):
    t, d = h.shape
    tm = _token_tile(t, FFN_BWD_TOKEN_TILE)
    fc = wgu.shape[-1]
    f = 2 * fc
    g_in, g_out = 4 * layer + 2, 4 * layer + 3

    def body(dh_ref, h_ref, ff_ref, gu_ref, sm_ref, wgu_ref, wd_ref, o_ref, dgu_ref, dff_ref, act_ref, sg_ref):
        i = pl.program_id(0)

        @pl.when(i == 0)
        def _():
            sg_ref[...] = jnp.zeros_like(sg_ref)

        dy = dh_ref[...]
        fh, r3 = _rms(ff_ref[...])
        sg_ref[1:2, :] += _colsum(dy * fh)
        dff = _rms_bwd(dy, fh, r3, sm_ref[g_out:g_out + 1, :]).astype(BF16)
        dff_ref[...] = dff
        for j in range(2):
            dact = _dot_nt(dff, wd_ref[j * fc:(j + 1) * fc, :])
            gate = gu_ref[:, j * fc:(j + 1) * fc].astype(F32)
            up = gu_ref[:, f + j * fc:f + (j + 1) * fc].astype(F32)
            sig = jax.nn.sigmoid(gate)
            silu = gate * sig
            act_ref[:, j * fc:(j + 1) * fc] = (silu * up).astype(BF16)
            dgu_ref[:, j * fc:(j + 1) * fc] = (dact * up * (sig * (1.0 + gate * (1.0 - sig)))).astype(BF16)
            dgu_ref[:, f + j * fc:f + (j + 1) * fc] = (dact * silu).astype(BF16)
        dn = None
        for k in range(N_CHIPS):
            part = _dot_nt(dgu_ref[:, k * fc:(k + 1) * fc], wgu_ref[k])
            dn = part if dn is None else dn + part
        hh, r2 = _rms(h_ref[...])
        sg_ref[0:1, :] += _colsum(dn * hh)
        o_ref[...] = dy + _rms_bwd(dn, hh, r2, sm_ref[g_in:g_in + 1, :])

    row = lambda i: (i, 0)
    return _gridded(
        body, after, name=f"bwd_ffn{layer}", grid=(t // tm,),
        out_shape=[jax.ShapeDtypeStruct((t, d), F32), jax.ShapeDtypeStruct((t, 2 * f), BF16),
                   jax.ShapeDtypeStruct((t, d), BF16), jax.ShapeDtypeStruct((t, f), BF16),
                   jax.ShapeDtypeStruct((8, d), F32)],
        in_specs=[pl.BlockSpec((tm, d), row), pl.BlockSpec((tm, d), row), pl.BlockSpec((tm, d), row),
                  pl.BlockSpec((tm, 2 * f), row), _resident(small.shape, lambda i: (0, 0)),
                  _resident(wgu.shape, lambda i: (0, 0, 0)), _resident(wd.shape, lambda i: (0, 0))],
        out_specs=[pl.BlockSpec((tm, d), row), pl.BlockSpec((tm, 2 * f), row), pl.BlockSpec((tm, d), row),
                   pl.BlockSpec((tm, f), row), pl.BlockSpec((8, d), lambda i: (0, 0))],
        args=[dh, h, ff, gu, small, wgu, wd])


def _bwd_conv(dh, h, y, proj, small, win, wout, after=()):
    t, d = h.shape
    tm = _token_tile(t)
    steps = t // tm
    pc = win.shape[-1]
    halo_blocks = tm // 16

    def body(dh_ref, h_ref, y_ref, proj_ref, halo_ref, sm_ref, win_ref, wout_ref,
             o_ref, dproj_ref, dy_ref, bc_ref, sg_ref, uext_ref, dcext_ref, carry_ref):
        i = pl.program_id(0)
        tile = steps - 1 - i

        @pl.when(i == 0)
        def _():
            sg_ref[...] = jnp.zeros_like(sg_ref)
            carry_ref[...] = jnp.zeros_like(carry_ref)

        dy = dh_ref[...]
        yh, r1 = _rms(y_ref[...])
        sg_ref[1:2, :] += _colsum(dy * yh)
        dyv = _rms_bwd(dy, yh, r1, sm_ref[5:6, :]).astype(BF16)
        dy_ref[...] = dyv
        dbc = _dot_nt(dyv, wout_ref[...])
        b = proj_ref[:, 0:d].astype(F32)
        cg = proj_ref[:, d:2 * d].astype(F32)
        v = proj_ref[:, 2 * d:].astype(F32)
        halo = halo_ref[...].astype(F32)[16 - CONV_HALO:]
        uh = halo[:, d:2 * d] * halo[:, 2 * d:]
        uext_ref[0:CONV_HALO, :] = jnp.where(tile > 0, uh, jnp.zeros_like(uh))
        uext_ref[CONV_HALO:, :] = cg * v
        taps = [sm_ref[8 + j:9 + j, :] for j in range(3)]
        full = uext_ref[...]
        u0 = full[CONV_HALO:]
        u1 = pltpu.roll(full, 1, 0)[CONV_HALO:]
        u2 = pltpu.roll(full, 2, 0)[CONV_HALO:]
        conv = u0 * taps[2] + u1 * taps[1] + u2 * taps[0]
        bc_ref[...] = (b * conv).astype(BF16)
        dconv = dbc * b
        sg_ref[4:5, :] += _colsum(dconv * u0)
        sg_ref[3:4, :] += _colsum(dconv * u1)
        sg_ref[2:3, :] += _colsum(dconv * u2)
        dcext_ref[0:tm, :] = dconv
        dcext_ref[tm:, :] = carry_ref[...]
        carry_ref[...] = dconv[0:CONV_HALO]
        dfull = dcext_ref[...]
        n8 = tm + CONV_HALO
        du = (dfull[0:tm] * taps[2] + pltpu.roll(dfull, n8 - 1, 0)[0:tm] * taps[1]
              + pltpu.roll(dfull, n8 - 2, 0)[0:tm] * taps[0])
        dproj_ref[:, 0:d] = (dbc * conv).astype(BF16)
        dproj_ref[:, d:2 * d] = (du * v).astype(BF16)
        dproj_ref[:, 2 * d:] = (du * cg).astype(BF16)
        dn = None
        for k in range(N_CHIPS):
            part = _dot_nt(dproj_ref[:, k * pc:(k + 1) * pc], win_ref[k])
            dn = part if dn is None else dn + part
        hh, r0 = _rms(h_ref[...])
        sg_ref[0:1, :] += _colsum(dn * hh)
        o_ref[...] = dy + _rms_bwd(dn, hh, r0, sm_ref[4:5, :])

    rev = lambda i: (steps - 1 - i, 0)
    before = lambda i: (jnp.maximum((steps - 1 - i) * halo_blocks - 1, 0), 0)
    return _gridded(
        body, after, name="bwd_conv", grid=(steps,),
        out_shape=[jax.ShapeDtypeStruct((t, d), F32), jax.ShapeDtypeStruct((t, 3 * d), BF16),
                   jax.ShapeDtypeStruct((t, d), BF16), jax.ShapeDtypeStruct((t, d), BF16),
                   jax.ShapeDtypeStruct((8, d), F32)],
        in_specs=[pl.BlockSpec((tm, d), rev), pl.BlockSpec((tm, d), rev), pl.BlockSpec((tm, d), rev),
                  pl.BlockSpec((tm, 3 * d), rev), pl.BlockSpec((16, 3 * d), before),
                  _resident(small.shape, lambda i: (0, 0)), _resident(win.shape, lambda i: (0, 0, 0)),
                  _resident(wout.shape, lambda i: (0, 0))],
        out_specs=[pl.BlockSpec((tm, d), rev), pl.BlockSpec((tm, 3 * d), rev), pl.BlockSpec((tm, d), rev),
                   pl.BlockSpec((tm, d), rev), pl.BlockSpec((8, d), lambda i: (0, 0))],
        scratch_shapes=[pltpu.VMEM((CONV_HALO + tm, d), F32), pltpu.VMEM((tm + CONV_HALO, d), F32),
                        pltpu.VMEM((CONV_HALO, d), F32)],
        args=[dh, h, y, proj, proj, small, win, wout])


def _bwd_pool(dh, x, small, scale, poolw, after=()):
    t, d = x.shape
    tm = _token_tile(t)
    steps = t // tm
    ng = len(POOL_WINDOWS)
    gw = d // ng
    halo_blocks = tm // POOL_HALO

    def body(dh_ref, x_ref, halo_ref, sm_ref, sc_ref, w_ref, o_ref, dw_ref, sg_ref,
             ext_ref, mix_ref, mm_ref, pb_ref, qext_ref, dhn_ref, carry_ref):
        i = pl.program_id(0)
        tile = steps - 1 - i

        @pl.when(i == 0)
        def _():
            sg_ref[...] = jnp.zeros_like(sg_ref)
            dw_ref[...] = jnp.zeros_like(dw_ref)
            carry_ref[...] = jnp.zeros_like(carry_ref)

        g0 = sm_ref[0:1, :]
        xv = x_ref[...]
        xh, r0 = _rms(xv)
        hx, _ = _rms(halo_ref[...])
        ext_ref[0:POOL_HALO, :] = jnp.where(tile > 0, hx * g0, jnp.zeros_like(hx))
        ext_ref[POOL_HALO:, :] = xh * g0
        for g in range(ng):
            pooled = _pool_windows(ext_ref, g, gw, tm, tile * tm)
            cols = slice(g * gw, (g + 1) * gw)
            pb = pooled.astype(BF16)
            pb_ref[:, cols] = pb
            mm = _dot(pb, w_ref[g])
            mm_ref[:, cols] = mm
            mix_ref[:, cols] = mm * sc_ref[:, cols]
        dy = dh_ref[...]
        mh, r1 = _rms(mix_ref[...])
        sg_ref[1:2, :] += _colsum(dy * mh)
        dmix = _rms_bwd(dy, mh, r1, sm_ref[1:2, :])
        sg_ref[2:3, :] += _colsum(dmix * mm_ref[...])
        mix_ref[...] = dmix * sc_ref[...]
        n16 = tm + POOL_HALO
        for g in range(ng):
            w = POOL_WINDOWS[g]
            cols = slice(g * gw, (g + 1) * gw)
            dmm = mix_ref[:, cols].astype(BF16)
            dpooled = _dot_nt(dmm, w_ref[g])
            dw_ref[g] += _dot_tn(pb_ref[:, cols], dmm)
            trow = tile * tm + lax.broadcasted_iota(jnp.int32, (tm, 1), 0)
            q = dpooled / jnp.minimum(trow + 1, w).astype(F32)
            qext_ref[0:tm, cols] = q
            qext_ref[tm:, cols] = carry_ref[:, cols]
            carry_ref[:, cols] = q[0:POOL_HALO]
            p, k = qext_ref[:, cols], 1
            while k < w:
                p = p + pltpu.roll(p, n16 - k, 0)
                k *= 2
            dhn_ref[:, cols] = p[0:tm] - dpooled
        dhn = dhn_ref[...]
        sg_ref[0:1, :] += _colsum(dhn * xh)
        o_ref[...] = dy + _rms_bwd(dhn, xh, r0, g0)

    rev = lambda i: (steps - 1 - i, 0)
    before = lambda i: (jnp.maximum((steps - 1 - i) * halo_blocks - 1, 0), 0)
    return _gridded(
        body, after, name="bwd_pool", grid=(steps,),
        out_shape=[jax.ShapeDtypeStruct((t, d), F32), jax.ShapeDtypeStruct((ng, gw, gw), F32),
                   jax.ShapeDtypeStruct((8, d), F32)],
        in_specs=[pl.BlockSpec((tm, d), rev), pl.BlockSpec((tm, d), rev), pl.BlockSpec((POOL_HALO, d), before),
                  _resident(small.shape, lambda i: (0, 0)), _resident(scale.shape, lambda i: (0, 0)),
                  _resident(poolw.shape, lambda i: (0, 0, 0))],
        out_specs=[pl.BlockSpec((tm, d), rev), pl.BlockSpec((ng, gw, gw), lambda i: (0, 0, 0)),
                   pl.BlockSpec((8, d), lambda i: (0, 0))],
        scratch_shapes=[pltpu.VMEM((POOL_HALO + tm, d), F32), pltpu.VMEM((tm, d), F32), pltpu.VMEM((tm, d), F32),
                        pltpu.VMEM((tm, d), BF16), pltpu.VMEM((tm + POOL_HALO, d), F32), pltpu.VMEM((tm, d), F32),
                        pltpu.VMEM((POOL_HALO, d), F32)],
        args=[dh, x, x, small, scale, poolw])


def _weight_grad(a, b, bm, bn, half_on, name):
    t, m = a.shape
    _, n = b.shape
    if half_on == "a":
        a_cols, b_cols = 2 * bm, bn
    else:
        a_cols, b_cols = bm, 2 * bn
    steps = max(m // a_cols, n // b_cols)

    def spec(cols, total):
        if cols == total:
            return _resident((t, cols), lambda p, j: (0, 0))
        return pl.BlockSpec((t, cols), lambda p, j: (0, j))

    def tile(a_ref, b_ref, half):
        if half_on == "a":
            return _dot_tn(a_ref[:, half * bm:(half + 1) * bm], b_ref[...])
        return _dot_tn(a_ref[...], b_ref[:, half * bn:(half + 1) * bn])

    def body(a_ref, b_ref, parts_ref, land_ref, acc_ref, stage_ref, got_ref, send_sems, recv_sems, got_sem):
        p, j = pl.program_id(0), pl.program_id(1)
        x, y, c, _ = _place()
        half = jnp.where(p == 0, 1 - c, c)

        def send(jj):
            return _remote(stage_ref.at[jj % 2], land_ref.at[jj], send_sems.at[jj], recv_sems.at[jj], (x, y, 1 - c))

        def fetch():
            return pltpu.make_async_copy(land_ref.at[j], got_ref, got_sem)

        @pl.when(p == 1)
        def _():
            @pl.when(j == 0)
            def _():
                for jj in range(max(steps - 2, 0), steps):
                    send(jj).wait_send()

            send(j).wait_recv()
            fetch().start()

        for hv in range(2):
            @pl.when(half == hv)
            def _():
                acc_ref[...] = tile(a_ref, b_ref, hv)

        @pl.when(p == 0)
        def _():
            @pl.when(j >= 2)
            def _():
                send(j - 2).wait_send()

            stage_ref[j % 2] = acc_ref[...].astype(BF16)
            send(j).start()

        @pl.when(p == 1)
        def _():
            fetch().wait()
            parts_ref[...] = (acc_ref[...] + got_ref[...].astype(F32)).astype(BF16)

    return _gridded(
        body, (), name=name, grid=(2, steps),
        out_shape=[jax.ShapeDtypeStruct((steps, bm, bn), BF16), jax.ShapeDtypeStruct((steps, bm, bn), BF16)],
        in_specs=[spec(a_cols, m), spec(b_cols, n)],
        out_specs=[pl.BlockSpec((None, bm, bn), lambda p, j: (p * j, 0, 0)), ANY],
        scratch_shapes=[pltpu.VMEM((bm, bn), F32), pltpu.VMEM((2, bm, bn), BF16), pltpu.VMEM((bm, bn), BF16),
                        DMA((steps,)), DMA((steps,)), DMA],
        args=[a, b])[0]


def _cast_layer(w, layer, name):
    _, r, c = w.shape
    rb = _row_block(r, c * (4 + 2) * 2)

    def body(w_ref, o_ref):
        o_ref[...] = w_ref[...].astype(BF16)

    return pl.pallas_call(
        body, name=name, grid=(r // rb,), out_shape=jax.ShapeDtypeStruct((r, c), BF16),
        in_specs=[pl.BlockSpec((None, rb, c), lambda i: (layer, i, 0))], out_specs=pl.BlockSpec((rb, c), lambda i: (i, 0)),
        compiler_params=pltpu.CompilerParams(dimension_semantics=("parallel",), vmem_limit_bytes=VMEM_LIMIT),
    )(w)


def _adamw_math(w, g, m, v):
    bc1 = 1.0 - ADAM_B1 ** ADAM_STEP
    bc2 = 1.0 - ADAM_B2 ** ADAM_STEP
    nm = ADAM_B1 * m + (1.0 - ADAM_B1) * g
    nv = ADAM_B2 * v + (1.0 - ADAM_B2) * (g * g)
    return -ADAM_LR * ((nm / bc1) / (jnp.sqrt(nv / bc2) + ADAM_EPS) + ADAM_WD * w), nm, nv


def _adamw_small(small_sum, where, gains, taps, scale):
    dq = gains[0].shape[-1]
    d = small_sum.shape[-1]

    def body(where_ref, mine_ref, all_ref, gw, gm, gv, tw, tm_, tv, sw, sm, sv,
             gg, gd, gnm, gnv, tg, td, tnm, tnv, sg, sd, snm, snv):
        for layer in range(gw.shape[0]):
            g = mine_ref[4 * layer:4 * layer + 4, :]
            gg[layer] = g
            gd[layer], gnm[layer], gnv[layer] = _adamw_math(gw[layer], g, gm[layer], gv[layer])
        g = mine_ref[8:8 + tw.shape[1], :]
        tg[0] = g
        td[0], tnm[0], tnv[0] = _adamw_math(tw[0], g, tm_[0], tv[0])
        g = all_ref[11:12, :]
        sg[...] = g
        sd[...], snm[...], snv[...] = _adamw_math(sw[...], g, sm[...], sv[...])

    full = lambda a: pl.BlockSpec(a.shape, lambda i, where_ref: (0,) * a.ndim)
    params = [*gains, *taps, *scale]
    outs = [gains[0]] * 4 + [taps[0]] * 4 + [scale[0]] * 4
    res = pl.pallas_call(
        body, name="adamw_small", out_shape=[jax.ShapeDtypeStruct(a.shape, F32) for a in outs],
        grid_spec=pltpu.PrefetchScalarGridSpec(
            num_scalar_prefetch=1, grid=(1,),
            in_specs=[pl.BlockSpec((16, dq), lambda i, where_ref: (0, where_ref[0])), pl.BlockSpec((16, d), lambda i, where_ref: (0, 0)),
                      *[full(a) for a in params]],
            out_specs=[full(a) for a in outs]),
    )(where, small_sum, small_sum, *params)
    return tuple(res[0:4]), tuple(res[4:8]), tuple(res[8:12])


def _adamw(w, g, m, v, name):
    r, c = w.shape
    rb = _row_block(r, c * (8 * 4 * 2 + 4 * 4))

    def body(w_ref, g_ref, m_ref, v_ref, d_ref, nm_ref, nv_ref, go_ref):
        gv = g_ref[...]
        go_ref[...] = gv
        d_ref[...], nm_ref[...], nv_ref[...] = _adamw_math(w_ref[...], gv, m_ref[...], v_ref[...])

    spec = pl.BlockSpec((rb, c), lambda i: (i, 0))
    return pl.pallas_call(
        body, name=name, grid=(r // rb,), out_shape=[jax.ShapeDtypeStruct((r, c), F32)] * 4,
        in_specs=[spec] * 4, out_specs=[spec] * 4,
        compiler_params=pltpu.CompilerParams(dimension_semantics=("parallel",), vmem_limit_bytes=VMEM_LIMIT),
    )(w, g, m, v)


def kernel(x, norm_gains, pool_w, pool_scale, conv_in_w, conv_w, conv_out_w, ffn_gate_up_w, ffn_down_w, loss_target, m_norm_gains, m_pool_w, m_pool_scale, m_conv_in_w, m_conv_w, m_conv_out_w, m_ffn_gate_up_w, m_ffn_down_w, v_norm_gains, v_pool_w, v_pool_scale, v_conv_in_w, v_conv_w, v_conv_out_w, v_ffn_gate_up_w, v_ffn_down_w):
    _, t, d = x.shape
    dq = d // N_CHIPS
    ng = len(POOL_WINDOWS)
    gw = d // ng
    fq = ffn_down_w.shape[1]
    f = N_CHIPS * fq
    fc = f // 2
    core = lax.axis_index("c")
    chip = 2 * lax.axis_index("x") + lax.axis_index("y")
    core_arr = jnp.reshape(core, (1,)).astype(jnp.int32)
    where_arr = jnp.stack([chip, core]).astype(jnp.int32)
    x2, target = x[0], loss_target[0]

    small_loc = jnp.concatenate(
        [norm_gains.reshape(8, dq), conv_w[0], jnp.zeros((5, dq), F32)], axis=0).reshape(1, 2, 8, dq)
    pool_loc = pool_w.astype(BF16).reshape(1, 2, ng // 2 * (gw // N_CHIPS), gw)
    wgu_loc = [_cast_layer(ffn_gate_up_w, 0, "cast_gate_up0").reshape(1, 2, d // 2, fc),
               ffn_gate_up_w[1:2].astype(BF16).reshape(1, 2, d // 2, fc)]
    wd_loc = [_cast_layer(ffn_down_w, 0, "cast_down0").reshape(1, 2, fq // 2, d),
              ffn_down_w[1:2].astype(BF16).reshape(1, 2, fq // 2, d)]
    win_loc = conv_in_w.astype(BF16).reshape(1, 2, d // 2, -1)
    wout_loc = conv_out_w.astype(BF16).reshape(1, 2, dq // 2, d)

    def ffn_weights(wgu_f, wd_f):
        return wgu_f.reshape(N_CHIPS, d, fc), wd_f.reshape(f, d)

    ag0 = _SplitGather([(pool_loc, 0), (small_loc, 0), (wgu_loc[0], 0), (wd_loc[0], 0)])
    ag0.start("ag_start_layer0")
    ag1 = _SplitGather([(win_loc, 0), (wout_loc, 0), (wgu_loc[1], 0), (wd_loc[1], 0)])
    ag1.start("ag_start_layer1", [ag0.token])
    pool_f, small_f = _pass_on(ag0.wait([0, 1], ag1.token, "ag_wait_first"), "ag_pass_first")
    poolw = pool_f.reshape(N_CHIPS, ng, gw // N_CHIPS, gw).transpose(1, 0, 2, 3).reshape(ng, gw, gw)
    small = small_f.transpose(1, 2, 0, 3).reshape(16, d)
    h1 = _fwd_pool(x2, small, pool_scale, poolw)
    (wgu0,) = _pass_on(ag0.wait([2], h1, "ag_wait_gate_up0"), "ag_pass_gate_up0")
    wgu0 = wgu0.reshape(N_CHIPS, d, fc)
    gu0, n0, act_fwd0 = _fwd_ffn_up(h1, small, wgu0, 0)
    (wd0,) = _pass_on(ag0.wait([3], gu0, "ag_wait_down0"), "ag_pass_down0")
    wd0 = wd0.reshape(f, d)
    h2, ff0 = _fwd_ffn_down(h1, act_fwd0, small, wd0, 0)
    win_f, wout_f = _pass_on(ag1.wait([0, 1], h2, "ag_wait_conv"), "ag_pass_conv")
    win_f, wout_f = win_f.reshape(N_CHIPS, d, -1), wout_f.reshape(d, d)
    h3, proj, y, nc = _fwd_conv(h2, small, win_f, wout_f)
    wgu1, wd1 = ffn_weights(*_pass_on(ag1.wait([2, 3], h3, "ag_wait_ffn1"), "ag_pass_ffn1"))
    dh4, gu1, ff1, n1, loss_blk = _fwd_ffn(h3, small, wgu1, wd1, 1, target=target)

    dh3, dgu1, dff1, act1, sg_f1 = _bwd_ffn(dh4, h3, ff1, gu1, small, wgu1, wd1, 1)
    parts_d1 = _weight_grad(act1, dff1, fc, d // 2, "b", "dw_down1")
    parts_gu1 = _weight_grad(n1, dgu1, d // 2, fc, "a", "dw_gate_up1")
    ex_ffn1 = _SplitExchange([parts_d1.reshape(N_CHIPS, fq, d // 2), parts_gu1])
    started = ex_ffn1.start("rs_start_ffn1")
    dh2, dproj, dyv, bcv, sg_c = _bwd_conv(dh3, h2, y, proj, small, win_f, wout_f, [started])
    parts_in = _weight_grad(nc, dproj, d // 2, 3 * d // N_CHIPS, "a", "dw_conv_in")
    parts_out = _weight_grad(bcv, dyv, d // 2, d // 2, "b", "dw_conv_out")
    ex_conv = _SplitExchange([parts_in, parts_out.reshape(N_CHIPS, dq, d // 2)])
    started = ex_conv.start("rs_start_conv")
    dh1, dgu0, dff0, act0, sg_f0 = _bwd_ffn(dh2, h1, ff0, gu0, small, wgu0, wd0, 0, [started])
    parts_d0 = _weight_grad(act0, dff0, fc, d // 2, "b", "dw_down0")
    parts_gu0 = _weight_grad(n0, dgu0, d // 2, fc, "a", "dw_gate_up0")
    ex_ffn0 = _SplitExchange([parts_d0.reshape(N_CHIPS, fq, d // 2), parts_gu0])
    started = ex_ffn0.start("rs_start_ffn0")
    grad_x, dpool, sg_p = _bwd_pool(dh1, x2, small, pool_scale, poolw, [started])
    g_pool = dpool.astype(BF16).reshape(2, ng // 2, N_CHIPS, gw // N_CHIPS, gw).transpose(2, 0, 1, 3, 4).reshape(
        N_CHIPS, 2, ng // 2 * (gw // N_CHIPS), gw)
    small_g = jnp.concatenate(
        [sg_p[0:2], sg_f0[0:2], sg_c[0:2], sg_f1[0:2], sg_c[2:5], sg_p[2:3],
         jnp.broadcast_to(loss_blk[0:1, 0:1], (1, d)), jnp.zeros((3, d), F32)], axis=0)
    ex_pool = _SplitExchange([_add_sibling(g_pool, _sibling_exchange(g_pool, "rs_sibling_pool"), core_arr)])
    started = ex_pool.start("rs_start_pool")

    def update(w, g, m, v, name):
        if w.size * 4 * 8 <= STREAM_BUDGET // 4:
            def body(w_ref, g_ref, m_ref, v_ref, go_ref, d_ref, nm_ref, nv_ref):
                go_ref[...] = g_ref[...]
                d_ref[...], nm_ref[...], nv_ref[...] = _adamw_math(w_ref[...], g_ref[...], m_ref[...], v_ref[...])

            return tuple(pl.pallas_call(body, name="adamw_" + name, out_shape=[jax.ShapeDtypeStruct(w.shape, F32)] * 4)(
                w, g.reshape(w.shape), m, v))
        flat = (-1, w.shape[-1])
        dl, m2, v2, g2 = _adamw(w.reshape(flat), g.reshape(flat), m.reshape(flat), v.reshape(flat), "adamw_" + name)
        return tuple(o.reshape(w.shape) for o in (g2, dl, m2, v2))

    (parts_d1, parts_gu1), (recv_d1, recv_gu1) = ex_ffn1.wait([started], "rs_wait_ffn1")
    (parts_in, parts_out), (recv_in, recv_out) = ex_conv.wait([started], "rs_wait_conv")
    gs_gu = _add_chips(parts_gu1, recv_gu1, where_arr, 1, 2)
    gs_d = _add_chips(parts_d1, recv_d1, where_arr, 1, 2, col_half=True)
    gs_in = _add_chips(parts_in, recv_in, where_arr)
    gs_out = _add_chips(parts_out, recv_out, where_arr, col_half=True)
    full_in, full_out = _sibling_share([gs_in, gs_out], [False, True], "rs_share_conv")
    up_in = update(conv_in_w, full_in.reshape(1, d, -1), m_conv_in_w, v_conv_in_w, "conv_in")
    up_out = update(conv_out_w, full_out.reshape(1, dq, d), m_conv_out_w, v_conv_out_w, "conv_out")
    done_first = [up_in[1], up_out[1], gs_gu, gs_d]
    (parts_d0, parts_gu0), (recv_d0, recv_gu0) = ex_ffn0.wait(done_first, "rs_wait_ffn0")
    (parts_p,), (recv_p,) = ex_pool.wait(done_first, "rs_wait_pool")
    ex_small = _SplitExchange([small_g], _SmallGather(small_g))
    started = ex_small.start("rs_start_small", [recv_gu0])
    gs_gu = _add_chips(parts_gu0, recv_gu0, where_arr, 0, 2, gs_gu)
    gs_d = _add_chips(parts_d0, recv_d0, where_arr, 0, 2, gs_d, col_half=True)
    gs_pool = _add_chips(parts_p, recv_p, where_arr)
    full_gu, full_d, full_pool = _sibling_share([gs_gu, gs_d, gs_pool], [False, True, False], "rs_share_ffn", [started])
    up_gu = update(ffn_gate_up_w, full_gu.reshape(2, d, fc), m_ffn_gate_up_w, v_ffn_gate_up_w, "gate_up")
    up_d = update(ffn_down_w, full_d.reshape(2, fq, d), m_ffn_down_w, v_ffn_down_w, "down")
    (small_own,), (small_all,) = ex_small.wait([up_gu[1], up_d[1]], "rs_wait_small")
    small_sum = _sum_small(small_all, small_own, (2 * chip + core).reshape(1).astype(jnp.int32))
    loss = small_sum[12, 0]
    up_gains, up_taps, up_scale = _adamw_small(
        small_sum, where_arr, (norm_gains, m_norm_gains, v_norm_gains), (conv_w, m_conv_w, v_conv_w),
        (pool_scale, m_pool_scale, v_pool_scale))

    ups = [
        up_gains,
        update(pool_w, full_pool.reshape(1, ng, gw // N_CHIPS, gw), m_pool_w, v_pool_w, "pool_w"),
        up_scale,
        up_in,
        up_taps,
        up_out,
        up_gu,
        up_d,
    ]
    grads_out, deltas, new_ms, new_vs = (list(col) for col in zip(*ups))
    return (loss, grad_x[None], *grads_out, *deltas, *new_ms, *new_vs)
```

```python
import jax
import jax.numpy as jnp
from jax import lax
from jax.experimental import pallas as pl
from jax.experimental.pallas import tpu as pltpu

RMS_EPS = 1e-6
POOL_WINDOWS = (2, 4, 8, 16)
POOL_HALO = 16
CONV_HALO = 8
N_CHIPS = 4
N_DEV = 8
ADAM_LR = 0.001
ADAM_B1 = 0.9
ADAM_B2 = 0.999
ADAM_EPS = 1e-08
ADAM_WD = 0.01
ADAM_STEP = 10
VMEM_LIMIT = 56 * 2**20
STREAM_BUDGET = 24 * 2**20
MESH = pl.DeviceIdType.MESH
ANY = pl.BlockSpec(memory_space=pl.ANY)
DMA = pltpu.SemaphoreType.DMA
BF16 = jnp.bfloat16
F32 = jnp.float32


TOKEN_TILE = 512
FFN_BWD_TOKEN_TILE = 256


def _token_tile(t, rows=TOKEN_TILE):
    return min(rows, t)


def _rms(x):
    r = lax.rsqrt(jnp.mean(x * x, axis=-1, keepdims=True) + RMS_EPS)
    return x * r, r


def _rms_bwd(dy, xh, r, g):
    a = dy * g
    return r * (a - xh * jnp.mean(a * xh, axis=-1, keepdims=True))


def _dot(a, b):
    return jnp.dot(a, b, preferred_element_type=F32)


def _dot_nt(a, b):
    return lax.dot_general(a, b, (((1,), (1,)), ((), ())), preferred_element_type=F32)


def _dot_tn(a, b):
    return lax.dot_general(a, b, (((0,), (0,)), ((), ())), preferred_element_type=F32)


def _colsum(a):
    return jnp.sum(a, axis=0, keepdims=True)


def _resident(block, index_map):
    return pl.BlockSpec(block, index_map, pipeline_mode=pl.Buffered(1))


def _row_block(r, row_bytes):
    best = None
    for rb in range(16, r + 1, 16):
        if r % rb == 0 and rb * row_bytes <= STREAM_BUDGET:
            best = rb
    return best if best is not None else r


def _place():
    x, y, c = lax.axis_index("x"), lax.axis_index("y"), lax.axis_index("c")
    return x, y, c, 2 * x + y


def _dev(chip, core):
    return (chip // 2, chip % 2, core)


def _remote(src, dst, send_sem, recv_sem, device):
    return pltpu.make_async_remote_copy(src_ref=src, dst_ref=dst, send_sem=send_sem, recv_sem=recv_sem,
                                        device_id=device, device_id_type=MESH)


class _Gather:
    def __init__(self, shards):
        self.args = [s for s, _ in shards]
        self.layers = [l for _, l in shards]
        self.out_shape = [jax.ShapeDtypeStruct((N_CHIPS,) + s.shape[1:], s.dtype) for s in self.args]

    def _own(self, loc, out, sems, a):
        x, y, c, k = _place()
        return _remote(loc[a].at[self.layers[a]], out[a].at[k], sems[0].at[a], sems[1].at[a], (x, y, 1 - c))

    def _ici(self, loc, out, sems, a, m, arrival):
        x, y, c, k = _place()
        dst = out[a].at[k ^ m, c] if arrival else out[a].at[k, c]
        return _remote(loc[a].at[self.layers[a], c], dst, sems[2].at[a, m - 1], sems[3].at[a, m - 1], _dev(k ^ m, c))

    def start(self, loc, out, sems):
        for a in range(len(self.args)):
            for m in range(1, N_CHIPS):
                self._ici(loc, out, sems, a, m, False).start()
            self._own(loc, out, sems, a).start()


class _SmallGather:
    PEERS = N_DEV - 1

    def __init__(self, small):
        self.args = [small]
        self.out_shape = [jax.ShapeDtypeStruct((N_DEV,) + small.shape, small.dtype)]

    def _copy(self, sm, land, sems, m, arrival):
        x, y, c, k = _place()
        me = 2 * k + c
        peer = me ^ m
        return _remote(sm[0], land[0].at[peer if arrival else me], sems[0].at[0, m - 1], sems[1].at[0, m - 1],
                       (peer // 4, (peer // 2) % 2, peer % 2))

    def start(self, sm, land, sems):
        for m in range(1, N_DEV):
            self._copy(sm, land, sems, m, False).start()

    def finish(self, sm, land, sems):
        for m in range(1, N_DEV):
            self._copy(sm, land, sems, m, True).wait_recv()
        for m in range(1, N_DEV):
            self._copy(sm, land, sems, m, False).wait_send()


class _ChipExchange:
    PEERS = N_CHIPS - 1

    def __init__(self, parts):
        self.args = list(parts)
        self.out_shape = [jax.ShapeDtypeStruct((N_CHIPS - 1,) + p.shape[1:], p.dtype) for p in parts]

    def _copy(self, p, land, sems, a, m):
        x, y, c, k = _place()
        return _remote(p[a].at[k ^ m], land[a].at[m - 1], sems[0].at[a, m - 1], sems[1].at[a, m - 1], _dev(k ^ m, c))

    def start(self, p, land, sems):
        for a in range(len(self.args)):
            for m in range(1, N_CHIPS):
                self._copy(p, land, sems, a, m).start()

    def finish(self, p, land, sems):
        for a in range(len(self.args)):
            for m in range(1, N_CHIPS):
                self._copy(p, land, sems, a, m).wait_recv()
        for a in range(len(self.args)):
            for m in range(1, N_CHIPS):
                self._copy(p, land, sems, a, m).wait_send()


def _gridded(body, after, *, name, grid, in_specs, out_specs, out_shape, args, scratch_shapes=()):
    ni, na = len(in_specs), len(after)

    def full(*refs):
        body(*refs[:ni], *refs[ni + na:])

    return list(pl.pallas_call(
        full, name=name, grid=grid, in_specs=[*in_specs, *[ANY] * na], out_specs=list(out_specs),
        out_shape=list(out_shape), scratch_shapes=list(scratch_shapes),
        compiler_params=pltpu.CompilerParams(dimension_semantics=("arbitrary",) * len(grid), vmem_limit_bytes=VMEM_LIMIT),
    )(*args, *after))


HBM = pl.BlockSpec(memory_space=pltpu.HBM)
SEM = pl.BlockSpec(memory_space=pltpu.SEMAPHORE)
DATAFLOW = pltpu.SideEffectType.DATAFLOW_SIDE_EFFECTING


class _SplitGather:
    PER_ARRAY = 8

    def __init__(self, shards):
        self.plan = _Gather(shards)
        self.n = len(shards)

    @staticmethod
    def _tables(sems_of):
        class Table:
            def __init__(self, pick):
                self.pick = pick

            @property
            def at(self):
                return self

            def __getitem__(self, idx):
                return self.pick(idx)

        return [Table(lambda a: sems_of[a][0]), Table(lambda a: sems_of[a][1]),
                Table(lambda am: sems_of[am[0]][2 + am[1]]), Table(lambda am: sems_of[am[0]][5 + am[1]])]

    def start(self, name, after=()):
        n, plan, per, na = self.n, self.plan, self.PER_ARRAY, len(after)

        def body(*refs):
            loc, land = refs[:n], refs[n:2 * n]
            sems_of = {a: refs[2 * n + na + per * a:2 * n + na + per * (a + 1)] for a in range(n)}
            plan.start(loc, land, self._tables(sems_of))
            refs[-1][...] = jnp.zeros_like(refs[-1])

        lands = [pltpu.with_memory_space_constraint(lax.empty(o.shape, o.dtype), pltpu.HBM) for o in plan.out_shape]
        locs = [pltpu.with_memory_space_constraint(a, pltpu.HBM) for a in plan.args]
        res = pl.pallas_call(
            body, name=name,
            out_shape=[*[DMA(())] * (per * n),
                       *[pltpu.HBM(o.shape, o.dtype) for o in plan.out_shape],
                       jax.ShapeDtypeStruct((8, 128), F32)],
            in_specs=[HBM] * (2 * n) + [pl.BlockSpec(memory_space=pl.ANY)] * na,
            out_specs=[SEM] * (per * n) + [HBM] * n + [pl.BlockSpec(memory_space=pltpu.VMEM)],
            input_output_aliases={n + i: per * n + i for i in range(n)},
            compiler_params=pltpu.CompilerParams(has_side_effects=DATAFLOW),
        )(*locs, *lands, *after)
        self.sems = {a: list(res[per * a:per * (a + 1)]) for a in range(n)}
        self.locs = locs
        self.lands = list(res[per * n:per * n + n])
        self.token = res[-1]

    def wait(self, idxs, after, name):
        plan, g, per = self.plan, len(idxs), self.PER_ARRAY

        def body(*refs):
            loc = {a: refs[j] for j, a in enumerate(idxs)}
            land = {a: refs[g + j] for j, a in enumerate(idxs)}
            sems = self._tables({a: refs[2 * g + per * j:2 * g + per * (j + 1)] for j, a in enumerate(idxs)})
            for a in idxs:
                for m in range(1, N_CHIPS):
                    plan._ici(loc, land, sems, a, m, True).wait_recv()
                    plan._ici(loc, land, sems, a, m, False).wait_send()
                plan._own(loc, land, sems, a).wait_recv()
                plan._own(loc, land, sems, a).wait_send()

        res = pl.pallas_call(
            body, name=name,
            out_shape=[pltpu.HBM(self.lands[a].shape, self.lands[a].dtype) for a in idxs],
            in_specs=[HBM] * (2 * g) + [SEM] * (per * g) + [pl.BlockSpec(memory_space=pl.ANY)], out_specs=[HBM] * g,
            input_output_aliases={g + j: j for j in range(g)},
            compiler_params=pltpu.CompilerParams(has_side_effects=DATAFLOW),
        )(*[self.locs[a] for a in idxs], *[self.lands[a] for a in idxs],
          *[s for a in idxs for s in self.sems[a]], after)
        return list(res)


class _SplitExchange:
    def __init__(self, parts, plan=None):
        self.plan = _ChipExchange(parts) if plan is None else plan
        self.n = len(parts)
        self.PER_ARRAY = 2 * self.plan.PEERS

    def _tables(self, sems_of):
        class Table:
            def __init__(self, pick):
                self.pick = pick

            @property
            def at(self):
                return self

            def __getitem__(self, am):
                return self.pick(am)

        peers = self.plan.PEERS
        return [Table(lambda am: sems_of[am[0]][am[1]]), Table(lambda am: sems_of[am[0]][peers + am[1]])]

    def start(self, name, after=()):
        n, plan, per, na = self.n, self.plan, self.PER_ARRAY, len(after)

        def body(*refs):
            p, land = refs[:n], refs[n:2 * n]
            sems_of = {a: refs[2 * n + na + per * a:2 * n + na + per * (a + 1)] for a in range(n)}
            plan.start(p, land, self._tables(sems_of))
            refs[-1][...] = jnp.zeros_like(refs[-1])

        lands = [pltpu.with_memory_space_constraint(lax.empty(o.shape, o.dtype), pltpu.HBM) for o in plan.out_shape]
        parts = [pltpu.with_memory_space_constraint(a, pltpu.HBM) for a in plan.args]
        res = pl.pallas_call(
            body, name=name,
            out_shape=[*[DMA(())] * (per * n),
                       *[pltpu.HBM(a.shape, a.dtype) for a in plan.args],
                       *[pltpu.HBM(o.shape, o.dtype) for o in plan.out_shape],
                       jax.ShapeDtypeStruct((8, 128), F32)],
            in_specs=[HBM] * (2 * n) + [pl.BlockSpec(memory_space=pl.ANY)] * na,
            out_specs=[SEM] * (per * n) + [HBM] * (2 * n) + [pl.BlockSpec(memory_space=pltpu.VMEM)],
            input_output_aliases={i: per * n + i for i in range(2 * n)},
            compiler_params=pltpu.CompilerParams(has_side_effects=DATAFLOW),
        )(*parts, *lands, *after)
        self.sems = list(res[:per * n])
        self.parts = list(res[per * n:per * n + n])
        self.lands = list(res[per * n + n:per * n + 2 * n])
        return res[-1]

    def wait(self, after, name):
        n, plan, per = self.n, self.plan, self.PER_ARRAY

        def body(*refs):
            p, land = refs[:n], refs[n:2 * n]
            sems_of = {a: refs[2 * n + per * a:2 * n + per * (a + 1)] for a in range(n)}
            plan.finish(p, land, self._tables(sems_of))

        res = pl.pallas_call(
            body, name=name,
            out_shape=[*[pltpu.HBM(a.shape, a.dtype) for a in self.parts], *[pltpu.HBM(a.shape, a.dtype) for a in self.lands]],
            in_specs=[HBM] * (2 * n) + [SEM] * (per * n) + [pl.BlockSpec(memory_space=pl.ANY)] * len(after),
            out_specs=[HBM] * (2 * n), input_output_aliases={i: i for i in range(2 * n)},
            compiler_params=pltpu.CompilerParams(has_side_effects=DATAFLOW),
        )(*self.parts, *self.lands, *self.sems, *after)
        return list(res[:n]), list(res[n:])


PASS_ON_BARRIER = 1


def _sibling_barrier():
    x, y, c, _ = _place()
    barrier = pltpu.get_barrier_semaphore()
    pl.semaphore_signal(barrier, inc=1, device_id=(x, y, 1 - c), device_id_type=MESH)
    pl.semaphore_wait(barrier, 1)


def _pass_on(lands, name):
    n = len(lands)

    def body(*refs):
        out = refs[n:2 * n]
        send_sems, recv_sems = refs[2 * n:]
        x, y, c, k = _place()
        _sibling_barrier()
        cps = []
        for a in range(n):
            for m in range(1, N_CHIPS):
                got = out[a].at[k ^ m, c]
                cp = _remote(got, got, send_sems.at[a, m - 1], recv_sems.at[a, m - 1], (x, y, 1 - c))
                cp.start()
                cps.append(cp)
        for a in range(n):
            for m in range(1, N_CHIPS):
                theirs = out[a].at[k ^ m, 1 - c]
                _remote(theirs, theirs, send_sems.at[a, m - 1], recv_sems.at[a, m - 1], (x, y, 1 - c)).wait_recv()
        for cp in cps:
            cp.wait_send()

    return pl.pallas_call(
        body, name=name, out_shape=[jax.ShapeDtypeStruct(a.shape, a.dtype) for a in lands],
        in_specs=[ANY] * n, out_specs=[ANY] * n, input_output_aliases={a: a for a in range(n)},
        scratch_shapes=[DMA((n, 3)), DMA((n, 3))],
        compiler_params=pltpu.CompilerParams(has_side_effects=True, collective_id=PASS_ON_BARRIER),
    )(*lands)


def _sibling_exchange(g, name):
    def body(g_ref, land_ref, send_sem, recv_sem):
        x, y, c, _ = _place()
        cp = _remote(g_ref.at[:, pl.ds(1 - c, 1)], land_ref, send_sem, recv_sem, (x, y, 1 - c))
        cp.start()
        cp.wait_recv()
        cp.wait_send()

    return pl.pallas_call(
        body, name=name, out_shape=jax.ShapeDtypeStruct((N_CHIPS, 1) + g.shape[2:], g.dtype), in_specs=[ANY],
        out_specs=ANY, scratch_shapes=[DMA, DMA], compiler_params=pltpu.CompilerParams(has_side_effects=True),
    )(g)


def _sibling_share(halves, col_half, name, after=()):
    n, na = len(halves), len(after)

    def body(*refs):
        out = refs[n + na:2 * n + na]
        send_sems, recv_sems = refs[2 * n + na:]
        x, y, c, k = _place()

        def half(a, core):
            if not col_half[a]:
                return out[a].at[:, pl.ds(core, 1)]
            cols = out[a].shape[-1] // 2
            return out[a].at[:, :, pl.ds(pl.multiple_of(core * cols, cols), cols)]

        cps = []
        for a in range(n):
            cp = _remote(half(a, c), half(a, c), send_sems.at[a], recv_sems.at[a], (x, y, 1 - c))
            cp.start()
            cps.append(cp)
        for a in range(n):
            _remote(half(a, 1 - c), half(a, 1 - c), send_sems.at[a], recv_sems.at[a], (x, y, 1 - c)).wait_recv()
        for cp in cps:
            cp.wait_send()

    out_shape = [jax.ShapeDtypeStruct(a.shape, a.dtype) for a in halves]
    return pl.pallas_call(
        body, name=name, out_shape=out_shape, in_specs=[ANY] * (n + na), out_specs=[ANY] * n,
        input_output_aliases={a: a for a in range(n)}, scratch_shapes=[DMA((n,)), DMA((n,))],
        compiler_params=pltpu.CompilerParams(has_side_effects=True),
    )(*halves, *after)


def _add_sibling(g, land, core):
    _, _, r, c = g.shape
    rb = _row_block(r, c * (3 * 2 * 2 + 2 * 4))

    def body(core_ref, g_ref, l_ref, o_ref):
        o_ref[...] = (g_ref[...].astype(F32) + l_ref[...].astype(F32)).astype(o_ref.dtype)

    return pl.pallas_call(
        body, name="rs_add_sibling", out_shape=jax.ShapeDtypeStruct((N_CHIPS, r, c), g.dtype),
        grid_spec=pltpu.PrefetchScalarGridSpec(
            num_scalar_prefetch=1, grid=(N_CHIPS, r // rb),
            in_specs=[pl.BlockSpec((None, None, rb, c), lambda j, i, core_ref: (j, core_ref[0], i, 0)),
                      pl.BlockSpec((None, None, rb, c), lambda j, i, core_ref: (j, 0, i, 0))],
            out_specs=pl.BlockSpec((None, rb, c), lambda j, i, core_ref: (j, i, 0))),
        compiler_params=pltpu.CompilerParams(dimension_semantics=("parallel", "parallel"), vmem_limit_bytes=VMEM_LIMIT),
    )(core, g, land)


def _add_chips(part, land, where, layer=0, n_layers=1, into=None, col_half=False):
    _, r, c = part.shape
    rb = _row_block(r, c * (4 * 2 * 2 + 4 * 2 + 2 * 4))

    def body(where_ref, p_ref, l_ref, *rest):
        acc = p_ref[...].astype(F32)
        for m in range(N_CHIPS - 1):
            acc = acc + l_ref[m].astype(F32)
        rest[-1][...] = acc

    in_specs = [pl.BlockSpec((None, rb, c), lambda i, where_ref: (where_ref[0], i, 0)),
                pl.BlockSpec((N_CHIPS - 1, rb, c), lambda i, where_ref: (0, i, 0))]
    args = [where, part, land]
    if into is not None:
        in_specs.append(ANY)
        args.append(into)
    if col_half:
        out_shape = jax.ShapeDtypeStruct((n_layers, r, 2 * c), F32)
        out_spec = pl.BlockSpec((None, rb, c), lambda i, where_ref: (layer, i, where_ref[1]))
    else:
        out_shape = jax.ShapeDtypeStruct((n_layers, 2, r, c), F32)
        out_spec = pl.BlockSpec((None, None, rb, c), lambda i, where_ref: (layer, where_ref[1], i, 0))
    return pl.pallas_call(
        body, name="rs_add_chips", out_shape=out_shape,
        grid_spec=pltpu.PrefetchScalarGridSpec(
            num_scalar_prefetch=1, grid=(r // rb,), in_specs=in_specs, out_specs=out_spec),
        input_output_aliases={} if into is None else {3: 0},
        compiler_params=pltpu.CompilerParams(dimension_semantics=("parallel",), vmem_limit_bytes=VMEM_LIMIT),
    )(*args)


def _sum_small(smg, own, me):
    def body(me_ref, s_ref, own_ref, o_ref):
        o_ref[...] = jnp.zeros_like(o_ref)
        for j in range(N_DEV):
            @pl.when(me_ref[0] == j)
            def _():
                o_ref[...] += own_ref[...]

            @pl.when(me_ref[0] != j)
            def _():
                o_ref[...] += s_ref[j]

    return pl.pallas_call(
        body, name="rs_sum_small", out_shape=jax.ShapeDtypeStruct(smg.shape[1:], F32),
        grid_spec=pltpu.PrefetchScalarGridSpec(
            num_scalar_prefetch=1, grid=(1,),
            in_specs=[pl.BlockSpec(smg.shape, lambda i, me_ref: (0, 0, 0)), pl.BlockSpec(own.shape, lambda i, me_ref: (0, 0))],
            out_specs=pl.BlockSpec(own.shape, lambda i, me_ref: (0, 0))),
    )(me, smg, own)


def _pool_windows(ext_ref, g, gw, tm, first_row):
    w = POOL_WINDOWS[g]
    slab = ext_ref[:, g * gw:(g + 1) * gw]
    p, k = slab, 1
    while k < w:
        p = p + pltpu.roll(p, k, 0)
        k *= 2
    t = first_row + lax.broadcasted_iota(jnp.int32, (tm, 1), 0)
    cnt = jnp.minimum(t + 1, w).astype(F32)
    return p[POOL_HALO:] / cnt - slab[POOL_HALO:]


def _fwd_pool(x, small, scale, poolw, after=()):
    t, d = x.shape
    tm = _token_tile(t)
    gw = d // len(POOL_WINDOWS)

    def body(x_ref, sm_ref, sc_ref, w_ref, h_ref, ext_ref, mix_ref):
        i = pl.program_id(0)

        @pl.when(i == 0)
        def _():
            ext_ref[0:POOL_HALO, :] = jnp.zeros((POOL_HALO, d), F32)

        @pl.when(i > 0)
        def _():
            ext_ref[0:POOL_HALO, :] = ext_ref[tm:tm + POOL_HALO, :]

        xv = x_ref[...]
        xh, _ = _rms(xv)
        ext_ref[POOL_HALO:, :] = xh * sm_ref[0:1, :]
        for g in range(len(POOL_WINDOWS)):
            pooled = _pool_windows(ext_ref, g, gw, tm, i * tm)
            cols = slice(g * gw, (g + 1) * gw)
            mix_ref[:, cols] = _dot(pooled.astype(BF16), w_ref[g]) * sc_ref[:, cols]
        mh, _ = _rms(mix_ref[...])
        h_ref[...] = xv + mh * sm_ref[1:2, :]

    return _gridded(
        body, after, name="fwd_pool", grid=(t // tm,), out_shape=[jax.ShapeDtypeStruct((t, d), F32)],
        in_specs=[pl.BlockSpec((tm, d), lambda i: (i, 0)), _resident(small.shape, lambda i: (0, 0)),
                  _resident(scale.shape, lambda i: (0, 0)), _resident(poolw.shape, lambda i: (0, 0, 0))],
        out_specs=[pl.BlockSpec((tm, d), lambda i: (i, 0))],
        scratch_shapes=[pltpu.VMEM((POOL_HALO + tm, d), F32), pltpu.VMEM((tm, d), F32)],
        args=[x, small, scale, poolw])[0]


def _fwd_ffn(h, small, wgu, wd, layer, after=(), target=None):
    t, d = h.shape
    tm = _token_tile(t)
    steps = t // tm
    fc = wgu.shape[-1]
    f = 2 * fc
    g_in, g_out = 4 * layer + 2, 4 * layer + 3
    with_loss = target is not None

    def body(h_ref, *refs):
        if with_loss:
            t_ref, sm_ref, wgu_ref, wd_ref, o_ref, gu_ref, ff_ref, n_ref, l_ref, acc_ref = refs
        else:
            sm_ref, wgu_ref, wd_ref, o_ref, gu_ref, ff_ref, n_ref = refs
        hv = h_ref[...]
        hh, _ = _rms(hv)
        n = (hh * sm_ref[g_in:g_in + 1, :]).astype(BF16)
        n_ref[...] = n
        ff = None
        for j in range(2):
            gate = _dot(n, wgu_ref[j])
            up = _dot(n, wgu_ref[2 + j])
            gu_ref[:, j * fc:(j + 1) * fc] = gate.astype(BF16)
            gu_ref[:, f + j * fc:f + (j + 1) * fc] = up.astype(BF16)
            act = (gate * jax.nn.sigmoid(gate) * up).astype(BF16)
            part = _dot(act, wd_ref[j * fc:(j + 1) * fc, :])
            ff = part if ff is None else ff + part
        ff_ref[...] = ff
        fh, _ = _rms(ff)
        out = hv + fh * sm_ref[g_out:g_out + 1, :]
        if not with_loss:
            o_ref[...] = out
            return
        i = pl.program_id(0)
        e = out - t_ref[...]
        o_ref[...] = e * (1.0 / d)

        @pl.when(i == 0)
        def _():
            acc_ref[...] = jnp.zeros_like(acc_ref)

        acc_ref[...] += _colsum(e * e)

        @pl.when(i == steps - 1)
        def _():
            l_ref[...] = jnp.full(l_ref.shape, 0.5 / d, F32) * jnp.sum(acc_ref[...])

    row = lambda i: (i, 0)
    out_shape = [jax.ShapeDtypeStruct((t, d), F32), jax.ShapeDtypeStruct((t, 2 * f), BF16),
                 jax.ShapeDtypeStruct((t, d), F32), jax.ShapeDtypeStruct((t, d), BF16)]
    out_specs = [pl.BlockSpec((tm, d), row), pl.BlockSpec((tm, 2 * f), row), pl.BlockSpec((tm, d), row),
                 pl.BlockSpec((tm, d), row)]
    weight_specs = [_resident(small.shape, lambda i: (0, 0)), _resident(wgu.shape, lambda i: (0, 0, 0)),
                    _resident(wd.shape, lambda i: (0, 0))]
    if with_loss:
        return _gridded(
            body, after, name=f"fwd_ffn{layer}_loss", grid=(steps,),
            out_shape=out_shape + [jax.ShapeDtypeStruct((8, 128), F32)],
            in_specs=[pl.BlockSpec((tm, d), row), pl.BlockSpec((tm, d), row)] + weight_specs,
            out_specs=out_specs + [pl.BlockSpec((8, 128), lambda i: (0, 0))],
            scratch_shapes=[pltpu.VMEM((1, d), F32)], args=[h, target, small, wgu, wd])
    return _gridded(
        body, after, name=f"fwd_ffn{layer}", grid=(steps,), out_shape=out_shape,
        in_specs=[pl.BlockSpec((tm, d), row)] + weight_specs, out_specs=out_specs, args=[h, small, wgu, wd])


def _fwd_ffn_up(h, small, wgu, layer):
    t, d = h.shape
    tm = _token_tile(t)
    fc = wgu.shape[-1]
    f = 2 * fc
    g_in = 4 * layer + 2

    def body(h_ref, sm_ref, wgu_ref, gu_ref, n_ref, act_ref):
        hh, _ = _rms(h_ref[...])
        n = (hh * sm_ref[g_in:g_in + 1, :]).astype(BF16)
        n_ref[...] = n
        for j in range(2):
            gate = _dot(n, wgu_ref[j])
            up = _dot(n, wgu_ref[2 + j])
            gu_ref[:, j * fc:(j + 1) * fc] = gate.astype(BF16)
            gu_ref[:, f + j * fc:f + (j + 1) * fc] = up.astype(BF16)
            act_ref[:, j * fc:(j + 1) * fc] = (gate * jax.nn.sigmoid(gate) * up).astype(BF16)

    row = lambda i: (i, 0)
    return _gridded(
        body, (), name=f"fwd_ffn{layer}_up", grid=(t // tm,),
        out_shape=[jax.ShapeDtypeStruct((t, 2 * f), BF16), jax.ShapeDtypeStruct((t, d), BF16),
                   jax.ShapeDtypeStruct((t, f), BF16)],
        in_specs=[pl.BlockSpec((tm, d), row), _resident(small.shape, lambda i: (0, 0)),
                  _resident(wgu.shape, lambda i: (0, 0, 0))],
        out_specs=[pl.BlockSpec((tm, 2 * f), row), pl.BlockSpec((tm, d), row), pl.BlockSpec((tm, f), row)],
        args=[h, small, wgu])


def _fwd_ffn_down(h, act, small, wd, layer):
    t, d = h.shape
    tm = _token_tile(t)
    f = wd.shape[0]
    g_out = 4 * layer + 3

    def body(h_ref, act_ref, sm_ref, wd_ref, o_ref, ff_ref):
        ff = _dot(act_ref[...], wd_ref[...])
        ff_ref[...] = ff
        fh, _ = _rms(ff)
        o_ref[...] = h_ref[...] + fh * sm_ref[g_out:g_out + 1, :]

    row = lambda i: (i, 0)
    return _gridded(
        body, (), name=f"fwd_ffn{layer}_down", grid=(t // tm,),
        out_shape=[jax.ShapeDtypeStruct((t, d), F32), jax.ShapeDtypeStruct((t, d), F32)],
        in_specs=[pl.BlockSpec((tm, d), row), pl.BlockSpec((tm, f), row), _resident(small.shape, lambda i: (0, 0)),
                  _resident(wd.shape, lambda i: (0, 0))],
        out_specs=[pl.BlockSpec((tm, d), row), pl.BlockSpec((tm, d), row)], args=[h, act, small, wd])


def _fwd_conv(h, small, win, wout, after=()):
    t, d = h.shape
    tm = _token_tile(t)
    pc = win.shape[-1]

    def body(h_ref, sm_ref, win_ref, wout_ref, o_ref, proj_ref, y_ref, n_ref, pj_ref, uext_ref):
        i = pl.program_id(0)

        @pl.when(i == 0)
        def _():
            uext_ref[0:CONV_HALO, :] = jnp.zeros((CONV_HALO, d), F32)

        @pl.when(i > 0)
        def _():
            uext_ref[0:CONV_HALO, :] = uext_ref[tm:tm + CONV_HALO, :]

        hv = h_ref[...]
        hh, _ = _rms(hv)
        n = (hh * sm_ref[4:5, :]).astype(BF16)
        n_ref[...] = n
        for k in range(N_CHIPS):
            pj_ref[:, k * pc:(k + 1) * pc] = _dot(n, win_ref[k])
        proj_ref[...] = pj_ref[...].astype(BF16)
        uext_ref[CONV_HALO:, :] = pj_ref[:, d:2 * d] * pj_ref[:, 2 * d:]
        taps = [sm_ref[8 + j:9 + j, :] for j in range(3)]
        full = uext_ref[...]
        conv = (full[CONV_HALO:] * taps[2] + pltpu.roll(full, 1, 0)[CONV_HALO:] * taps[1]
                + pltpu.roll(full, 2, 0)[CONV_HALO:] * taps[0])
        y = _dot((pj_ref[:, 0:d] * conv).astype(BF16), wout_ref[...])
        y_ref[...] = y
        yh, _ = _rms(y)
        o_ref[...] = hv + yh * sm_ref[5:6, :]

    row = lambda i: (i, 0)
    return _gridded(
        body, after, name="fwd_conv", grid=(t // tm,),
        out_shape=[jax.ShapeDtypeStruct((t, d), F32), jax.ShapeDtypeStruct((t, 3 * d), BF16),
                   jax.ShapeDtypeStruct((t, d), F32), jax.ShapeDtypeStruct((t, d), BF16)],
        in_specs=[pl.BlockSpec((tm, d), row), _resident(small.shape, lambda i: (0, 0)),
                  _resident(win.shape, lambda i: (0, 0, 0)), _resident(wout.shape, lambda i: (0, 0))],
        out_specs=[pl.BlockSpec((tm, d), row), pl.BlockSpec((tm, 3 * d), row), pl.BlockSpec((tm, d), row),
                   pl.BlockSpec((tm, d), row)],
        scratch_shapes=[pltpu.VMEM((tm, 3 * d), F32), pltpu.VMEM((CONV_HALO + tm, d), F32)],
        args=[h, small, win, wout])


def _bwd_ffn(dh, h, ff, gu, small, wgu, wd, layer, after=()):
    t, d = h.shape
    tm = _token_tile(t, FFN_BWD_TOKEN_TILE)
    fc = wgu.shape[-1]
    f = 2 * fc
    g_in, g_out = 4 * layer + 2, 4 * layer + 3

    def body(dh_ref, h_ref, ff_ref, gu_ref, sm_ref, wgu_ref, wd_ref, o_ref, dgu_ref, dff_ref, act_ref, sg_ref):
        i = pl.program_id(0)

        @pl.when(i == 0)
        def _():
            sg_ref[...] = jnp.zeros_like(sg_ref)

        dy = dh_ref[...]
        fh, r3 = _rms(ff_ref[...])
        sg_ref[1:2, :] += _colsum(dy * fh)
        dff = _rms_bwd(dy, fh, r3, sm_ref[g_out:g_out + 1, :]).astype(BF16)
        dff_ref[...] = dff
        for j in range(2):
            dact = _dot_nt(dff, wd_ref[j * fc:(j + 1) * fc, :])
            gate = gu_ref[:, j * fc:(j + 1) * fc].astype(F32)
            up = gu_ref[:, f + j * fc:f + (j + 1) * fc].astype(F32)
            sig = jax.nn.sigmoid(gate)
            silu = gate * sig
            act_ref[:, j * fc:(j + 1) * fc] = (silu * up).astype(BF16)
            dgu_ref[:, j * fc:(j + 1) * fc] = (dact * up * (sig * (1.0 + gate * (1.0 - sig)))).astype(BF16)
            dgu_ref[:, f + j * fc:f + (j + 1) * fc] = (dact * silu).astype(BF16)
        dn = None
        for k in range(N_CHIPS):
            part = _dot_nt(dgu_ref[:, k * fc:(k + 1) * fc], wgu_ref[k])
            dn = part if dn is None else dn + part
        hh, r2 = _rms(h_ref[...])
        sg_ref[0:1, :] += _colsum(dn * hh)
        o_ref[...] = dy + _rms_bwd(dn, hh, r2, sm_ref[g_in:g_in + 1, :])

    row = lambda i: (i, 0)
    return _gridded(
        body, after, name=f"bwd_ffn{layer}", grid=(t // tm,),
        out_shape=[jax.ShapeDtypeStruct((t, d), F32), jax.ShapeDtypeStruct((t, 2 * f), BF16),
                   jax.ShapeDtypeStruct((t, d), BF16), jax.ShapeDtypeStruct((t, f), BF16),
                   jax.ShapeDtypeStruct((8, d), F32)],
        in_specs=[pl.BlockSpec((tm, d), row), pl.BlockSpec((tm, d), row), pl.BlockSpec((tm, d), row),
                  pl.BlockSpec((tm, 2 * f), row), _resident(small.shape, lambda i: (0, 0)),
                  _resident(wgu.shape, lambda i: (0, 0, 0)), _resident(wd.shape, lambda i: (0, 0))],
        out_specs=[pl.BlockSpec((tm, d), row), pl.BlockSpec((tm, 2 * f), row), pl.BlockSpec((tm, d), row),
                   pl.BlockSpec((tm, f), row), pl.BlockSpec((8, d), lambda i: (0, 0))],
        args=[dh, h, ff, gu, small, wgu, wd])


def _bwd_conv(dh, h, y, proj, small, win, wout, after=()):
    t, d = h.shape
    tm = _token_tile(t)
    steps = t // tm
    pc = win.shape[-1]
    halo_blocks = tm // 16

    def body(dh_ref, h_ref, y_ref, proj_ref, halo_ref, sm_ref, win_ref, wout_ref,
             o_ref, dproj_ref, dy_ref, bc_ref, sg_ref, uext_ref, dcext_ref, carry_ref):
        i = pl.program_id(0)
        tile = steps - 1 - i

        @pl.when(i == 0)
        def _():
            sg_ref[...] = jnp.zeros_like(sg_ref)
            carry_ref[...] = jnp.zeros_like(carry_ref)

        dy = dh_ref[...]
        yh, r1 = _rms(y_ref[...])
        sg_ref[1:2, :] += _colsum(dy * yh)
        dyv = _rms_bwd(dy, yh, r1, sm_ref[5:6, :]).astype(BF16)
        dy_ref[...] = dyv
        dbc = _dot_nt(dyv, wout_ref[...])
        b = proj_ref[:, 0:d].astype(F32)
        cg = proj_ref[:, d:2 * d].astype(F32)
        v = proj_ref[:, 2 * d:].astype(F32)
        halo = halo_ref[...].astype(F32)[16 - CONV_HALO:]
        uh = halo[:, d:2 * d] * halo[:, 2 * d:]
        uext_ref[0:CONV_HALO, :] = jnp.where(tile > 0, uh, jnp.zeros_like(uh))
        uext_ref[CONV_HALO:, :] = cg * v
        taps = [sm_ref[8 + j:9 + j, :] for j in range(3)]
        full = uext_ref[...]
        u0 = full[CONV_HALO:]
        u1 = pltpu.roll(full, 1, 0)[CONV_HALO:]
        u2 = pltpu.roll(full, 2, 0)[CONV_HALO:]
        conv = u0 * taps[2] + u1 * taps[1] + u2 * taps[0]
        bc_ref[...] = (b * conv).astype(BF16)
        dconv = dbc * b
        sg_ref[4:5, :] += _colsum(dconv * u0)
        sg_ref[3:4, :] += _colsum(dconv * u1)
        sg_ref[2:3, :] += _colsum(dconv * u2)
        dcext_ref[0:tm, :] = dconv
        dcext_ref[tm:, :] = carry_ref[...]
        carry_ref[...] = dconv[0:CONV_HALO]
        dfull = dcext_ref[...]
        n8 = tm + CONV_HALO
        du = (dfull[0:tm] * taps[2] + pltpu.roll(dfull, n8 - 1, 0)[0:tm] * taps[1]
              + pltpu.roll(dfull, n8 - 2, 0)[0:tm] * taps[0])
        dproj_ref[:, 0:d] = (dbc * conv).astype(BF16)
        dproj_ref[:, d:2 * d] = (du * v).astype(BF16)
        dproj_ref[:, 2 * d:] = (du * cg).astype(BF16)
        dn = None
        for k in range(N_CHIPS):
            part = _dot_nt(dproj_ref[:, k * pc:(k + 1) * pc], win_ref[k])
            dn = part if dn is None else dn + part
        hh, r0 = _rms(h_ref[...])
        sg_ref[0:1, :] += _colsum(dn * hh)
        o_ref[...] = dy + _rms_bwd(dn, hh, r0, sm_ref[4:5, :])

    rev = lambda i: (steps - 1 - i, 0)
    before = lambda i: (jnp.maximum((steps - 1 - i) * halo_blocks - 1, 0), 0)
    return _gridded(
        body, after, name="bwd_conv", grid=(steps,),
        out_shape=[jax.ShapeDtypeStruct((t, d), F32), jax.ShapeDtypeStruct((t, 3 * d), BF16),
                   jax.ShapeDtypeStruct((t, d), BF16), jax.ShapeDtypeStruct((t, d), BF16),
                   jax.ShapeDtypeStruct((8, d), F32)],
        in_specs=[pl.BlockSpec((tm, d), rev), pl.BlockSpec((tm, d), rev), pl.BlockSpec((tm, d), rev),
                  pl.BlockSpec((tm, 3 * d), rev), pl.BlockSpec((16, 3 * d), before),
                  _resident(small.shape, lambda i: (0, 0)), _resident(win.shape, lambda i: (0, 0, 0)),
                  _resident(wout.shape, lambda i: (0, 0))],
        out_specs=[pl.BlockSpec((tm, d), rev), pl.BlockSpec((tm, 3 * d), rev), pl.BlockSpec((tm, d), rev),
                   pl.BlockSpec((tm, d), rev), pl.BlockSpec((8, d), lambda i: (0, 0))],
        scratch_shapes=[pltpu.VMEM((CONV_HALO + tm, d), F32), pltpu.VMEM((tm + CONV_HALO, d), F32),
                        pltpu.VMEM((CONV_HALO, d), F32)],
        args=[dh, h, y, proj, proj, small, win, wout])


def _bwd_pool(dh, x, small, scale, poolw, after=()):
    t, d = x.shape
    tm = _token_tile(t)
    steps = t // tm
    ng = len(POOL_WINDOWS)
    gw = d // ng
    halo_blocks = tm // POOL_HALO

    def body(dh_ref, x_ref, halo_ref, sm_ref, sc_ref, w_ref, o_ref, dw_ref, sg_ref,
             ext_ref, mix_ref, mm_ref, pb_ref, qext_ref, dhn_ref, carry_ref):
        i = pl.program_id(0)
        tile = steps - 1 - i

        @pl.when(i == 0)
        def _():
            sg_ref[...] = jnp.zeros_like(sg_ref)
            dw_ref[...] = jnp.zeros_like(dw_ref)
            carry_ref[...] = jnp.zeros_like(carry_ref)

        g0 = sm_ref[0:1, :]
        xv = x_ref[...]
        xh, r0 = _rms(xv)
        hx, _ = _rms(halo_ref[...])
        ext_ref[0:POOL_HALO, :] = jnp.where(tile > 0, hx * g0, jnp.zeros_like(hx))
        ext_ref[POOL_HALO:, :] = xh * g0
        for g in range(ng):
            pooled = _pool_windows(ext_ref, g, gw, tm, tile * tm)
            cols = slice(g * gw, (g + 1) * gw)
            pb = pooled.astype(BF16)
            pb_ref[:, cols] = pb
            mm = _dot(pb, w_ref[g])
            mm_ref[:, cols] = mm
            mix_ref[:, cols] = mm * sc_ref[:, cols]
        dy = dh_ref[...]
        mh, r1 = _rms(mix_ref[...])
        sg_ref[1:2, :] += _colsum(dy * mh)
        dmix = _rms_bwd(dy, mh, r1, sm_ref[1:2, :])
        sg_ref[2:3, :] += _colsum(dmix * mm_ref[...])
        mix_ref[...] = dmix * sc_ref[...]
        n16 = tm + POOL_HALO
        for g in range(ng):
            w = POOL_WINDOWS[g]
            cols = slice(g * gw, (g + 1) * gw)
            dmm = mix_ref[:, cols].astype(BF16)
            dpooled = _dot_nt(dmm, w_ref[g])
            dw_ref[g] += _dot_tn(pb_ref[:, cols], dmm)
            trow = tile * tm + lax.broadcasted_iota(jnp.int32, (tm, 1), 0)
            q = dpooled / jnp.minimum(trow + 1, w).astype(F32)
            qext_ref[0:tm, cols] = q
            qext_ref[tm:, cols] = carry_ref[:, cols]
            carry_ref[:, cols] = q[0:POOL_HALO]
            p, k = qext_ref[:, cols], 1
            while k < w:
                p = p + pltpu.roll(p, n16 - k, 0)
                k *= 2
            dhn_ref[:, cols] = p[0:tm] - dpooled
        dhn = dhn_ref[...]
        sg_ref[0:1, :] += _colsum(dhn * xh)
        o_ref[...] = dy + _rms_bwd(dhn, xh, r0, g0)

    rev = lambda i: (steps - 1 - i, 0)
    before = lambda i: (jnp.maximum((steps - 1 - i) * halo_blocks - 1, 0), 0)
    return _gridded(
        body, after, name="bwd_pool", grid=(steps,),
        out_shape=[jax.ShapeDtypeStruct((t, d), F32), jax.ShapeDtypeStruct((ng, gw, gw), F32),
                   jax.ShapeDtypeStruct((8, d), F32)],
        in_specs=[pl.BlockSpec((tm, d), rev), pl.BlockSpec((tm, d), rev), pl.BlockSpec((POOL_HALO, d), before),
                  _resident(small.shape, lambda i: (0, 0)), _resident(scale.shape, lambda i: (0, 0)),
                  _resident(poolw.shape, lambda i: (0, 0, 0))],
        out_specs=[pl.BlockSpec((tm, d), rev), pl.BlockSpec((ng, gw, gw), lambda i: (0, 0, 0)),
                   pl.BlockSpec((8, d), lambda i: (0, 0))],
        scratch_shapes=[pltpu.VMEM((POOL_HALO + tm, d), F32), pltpu.VMEM((tm, d), F32), pltpu.VMEM((tm, d), F32),
                        pltpu.VMEM((tm, d), BF16), pltpu.VMEM((tm + POOL_HALO, d), F32), pltpu.VMEM((tm, d), F32),
                        pltpu.VMEM((POOL_HALO, d), F32)],
        args=[dh, x, x, small, scale, poolw])


def _weight_grad(a, b, bm, bn, half_on, name):
    t, m = a.shape
    _, n = b.shape
    if half_on == "a":
        a_cols, b_cols = 2 * bm, bn
    else:
        a_cols, b_cols = bm, 2 * bn
    steps = max(m // a_cols, n // b_cols)

    def spec(cols, total):
        if cols == total:
            return _resident((t, cols), lambda p, j: (0, 0))
        return pl.BlockSpec((t, cols), lambda p, j: (0, j))

    def tile(a_ref, b_ref, half):
        if half_on == "a":
            return _dot_tn(a_ref[:, half * bm:(half + 1) * bm], b_ref[...])
        return _dot_tn(a_ref[...], b_ref[:, half * bn:(half + 1) * bn])

    def body(a_ref, b_ref, parts_ref, land_ref, acc_ref, stage_ref, got_ref, send_sems, recv_sems, got_sem):
        p, j = pl.program_id(0), pl.program_id(1)
        x, y, c, _ = _place()
        half = jnp.where(p == 0, 1 - c, c)

        def send(jj):
            return _remote(stage_ref.at[jj % 2], land_ref.at[jj], send_sems.at[jj], recv_sems.at[jj], (x, y, 1 - c))

        def fetch():
            return pltpu.make_async_copy(land_ref.at[j], got_ref, got_sem)

        @pl.when(p == 1)
        def _():
            @pl.when(j == 0)
            def _():
                for jj in range(max(steps - 2, 0), steps):
                    send(jj).wait_send()

            send(j).wait_recv()
            fetch().start()

        for hv in range(2):
            @pl.when(half == hv)
            def _():
                acc_ref[...] = tile(a_ref, b_ref, hv)

        @pl.when(p == 0)
        def _():
            @pl.when(j >= 2)
            def _():
                send(j - 2).wait_send()

            stage_ref[j % 2] = acc_ref[...].astype(BF16)
            send(j).start()

        @pl.when(p == 1)
        def _():
            fetch().wait()
            parts_ref[...] = (acc_ref[...] + got_ref[...].astype(F32)).astype(BF16)

    return _gridded(
        body, (), name=name, grid=(2, steps),
        out_shape=[jax.ShapeDtypeStruct((steps, bm, bn), BF16), jax.ShapeDtypeStruct((steps, bm, bn), BF16)],
        in_specs=[spec(a_cols, m), spec(b_cols, n)],
        out_specs=[pl.BlockSpec((None, bm, bn), lambda p, j: (p * j, 0, 0)), ANY],
        scratch_shapes=[pltpu.VMEM((bm, bn), F32), pltpu.VMEM((2, bm, bn), BF16), pltpu.VMEM((bm, bn), BF16),
                        DMA((steps,)), DMA((steps,)), DMA],
        args=[a, b])[0]


def _cast_layer(w, layer, name, after=()):
    _, r, c = w.shape
    rb = _row_block(r, c * (4 + 2) * 2)

    def body(w_ref, *rest):
        rest[-1][...] = w_ref[...].astype(BF16)

    return pl.pallas_call(
        body, name=name, grid=(r // rb,), out_shape=jax.ShapeDtypeStruct((r, c), BF16),
        in_specs=[pl.BlockSpec((None, rb, c), lambda i: (layer, i, 0))] + [ANY] * len(after),
        out_specs=pl.BlockSpec((rb, c), lambda i: (i, 0)),
        compiler_params=pltpu.CompilerParams(dimension_semantics=("parallel",), vmem_limit_bytes=VMEM_LIMIT),
    )(w, *after)


def _adamw_math(w, g, m, v):
    bc1 = 1.0 - ADAM_B1 ** ADAM_STEP
    bc2 = 1.0 - ADAM_B2 ** ADAM_STEP
    nm = ADAM_B1 * m + (1.0 - ADAM_B1) * g
    nv = ADAM_B2 * v + (1.0 - ADAM_B2) * (g * g)
    return -ADAM_LR * ((nm / bc1) / (jnp.sqrt(nv / bc2) + ADAM_EPS) + ADAM_WD * w), nm, nv


def _adamw_small(small_sum, where, gains, taps, scale):
    dq = gains[0].shape[-1]
    d = small_sum.shape[-1]

    def body(where_ref, mine_ref, all_ref, gw, gm, gv, tw, tm_, tv, sw, sm, sv,
             gg, gd, gnm, gnv, tg, td, tnm, tnv, sg, sd, snm, snv):
        for layer in range(gw.shape[0]):
            g = mine_ref[4 * layer:4 * layer + 4, :]
            gg[layer] = g
            gd[layer], gnm[layer], gnv[layer] = _adamw_math(gw[layer], g, gm[layer], gv[layer])
        g = mine_ref[8:8 + tw.shape[1], :]
        tg[0] = g
        td[0], tnm[0], tnv[0] = _adamw_math(tw[0], g, tm_[0], tv[0])
        g = all_ref[11:12, :]
        sg[...] = g
        sd[...], snm[...], snv[...] = _adamw_math(sw[...], g, sm[...], sv[...])

    full = lambda a: pl.BlockSpec(a.shape, lambda i, where_ref: (0,) * a.ndim)
    params = [*gains, *taps, *scale]
    outs = [gains[0]] * 4 + [taps[0]] * 4 + [scale[0]] * 4
    res = pl.pallas_call(
        body, name="adamw_small", out_shape=[jax.ShapeDtypeStruct(a.shape, F32) for a in outs],
        grid_spec=pltpu.PrefetchScalarGridSpec(
            num_scalar_prefetch=1, grid=(1,),
            in_specs=[pl.BlockSpec((16, dq), lambda i, where_ref: (0, where_ref[0])), pl.BlockSpec((16, d), lambda i, where_ref: (0, 0)),
                      *[full(a) for a in params]],
            out_specs=[full(a) for a in outs]),
    )(where, small_sum, small_sum, *params)
    return tuple(res[0:4]), tuple(res[4:8]), tuple(res[8:12])


def _adamw(w, g, m, v, name):
    r, c = w.shape
    rb = _row_block(r, c * (8 * 4 * 2 + 4 * 4))

    def body(w_ref, g_ref, m_ref, v_ref, d_ref, nm_ref, nv_ref, go_ref):
        gv = g_ref[...]
        go_ref[...] = gv
        d_ref[...], nm_ref[...], nv_ref[...] = _adamw_math(w_ref[...], gv, m_ref[...], v_ref[...])

    spec = pl.BlockSpec((rb, c), lambda i: (i, 0))
    return pl.pallas_call(
        body, name=name, grid=(r // rb,), out_shape=[jax.ShapeDtypeStruct((r, c), F32)] * 4,
        in_specs=[spec] * 4, out_specs=[spec] * 4,
        compiler_params=pltpu.CompilerParams(dimension_semantics=("parallel",), vmem_limit_bytes=VMEM_LIMIT),
    )(w, g, m, v)


def kernel(x, norm_gains, pool_w, pool_scale, conv_in_w, conv_w, conv_out_w, ffn_gate_up_w, ffn_down_w, loss_target, m_norm_gains, m_pool_w, m_pool_scale, m_conv_in_w, m_conv_w, m_conv_out_w, m_ffn_gate_up_w, m_ffn_down_w, v_norm_gains, v_pool_w, v_pool_scale, v_conv_in_w, v_conv_w, v_conv_out_w, v_ffn_gate_up_w, v_ffn_down_w):
    _, t, d = x.shape
    dq = d // N_CHIPS
    ng = len(POOL_WINDOWS)
    gw = d // ng
    fq = ffn_down_w.shape[1]
    f = N_CHIPS * fq
    fc = f // 2
    core = lax.axis_index("c")
    chip = 2 * lax.axis_index("x") + lax.axis_index("y")
    core_arr = jnp.reshape(core, (1,)).astype(jnp.int32)
    where_arr = jnp.stack([chip, core]).astype(jnp.int32)
    x2, target = x[0], loss_target[0]

    small_loc = jnp.concatenate(
        [norm_gains.reshape(8, dq), conv_w[0], jnp.zeros((5, dq), F32)], axis=0).reshape(1, 2, 8, dq)
    pool_loc = pool_w.astype(BF16).reshape(1, 2, ng // 2 * (gw // N_CHIPS), gw)
    wgu_loc = [_cast_layer(ffn_gate_up_w, 0, "cast_gate_up0").reshape(1, 2, d // 2, fc),
               ffn_gate_up_w[1:2].astype(BF16).reshape(1, 2, d // 2, fc)]
    wd1_loc = ffn_down_w[1:2].astype(BF16).reshape(1, 2, fq // 2, d)
    win_loc = conv_in_w.astype(BF16).reshape(1, 2, d // 2, -1)
    wout_loc = conv_out_w.astype(BF16).reshape(1, 2, dq // 2, d)

    def ffn_weights(wgu_f, wd_f):
        return wgu_f.reshape(N_CHIPS, d, fc), wd_f.reshape(f, d)

    ag0 = _SplitGather([(pool_loc, 0), (small_loc, 0), (wgu_loc[0], 0)])
    ag0.start("ag_start_gate_up0")
    wd0_loc = _cast_layer(ffn_down_w, 0, "cast_down0", [ag0.token]).reshape(1, 2, fq // 2, d)
    ag0d = _SplitGather([(wd0_loc, 0)])
    ag0d.start("ag_start_down0", [ag0.token])
    ag1 = _SplitGather([(win_loc, 0), (wout_loc, 0), (wgu_loc[1], 0), (wd1_loc, 0)])
    ag1.start("ag_start_layer1", [ag0d.token])
    pool_f, small_f = _pass_on(ag0.wait([0, 1], ag1.token, "ag_wait_first"), "ag_pass_first")
    poolw = pool_f.reshape(N_CHIPS, ng, gw // N_CHIPS, gw).transpose(1, 0, 2, 3).reshape(ng, gw, gw)
    small = small_f.transpose(1, 2, 0, 3).reshape(16, d)
    h1 = _fwd_pool(x2, small, pool_scale, poolw)
    (wgu0,) = _pass_on(ag0.wait([2], h1, "ag_wait_gate_up0"), "ag_pass_gate_up0")
    wgu0 = wgu0.reshape(N_CHIPS, d, fc)
    gu0, n0, act_fwd0 = _fwd_ffn_up(h1, small, wgu0, 0)
    (wd0,) = _pass_on(ag0d.wait([0], gu0, "ag_wait_down0"), "ag_pass_down0")
    wd0 = wd0.reshape(f, d)
    h2, ff0 = _fwd_ffn_down(h1, act_fwd0, small, wd0, 0)
    win_f, wout_f = _pass_on(ag1.wait([0, 1], h2, "ag_wait_conv"), "ag_pass_conv")
    win_f, wout_f = win_f.reshape(N_CHIPS, d, -1), wout_f.reshape(d, d)
    h3, proj, y, nc = _fwd_conv(h2, small, win_f, wout_f)
    wgu1, wd1 = ffn_weights(*_pass_on(ag1.wait([2, 3], h3, "ag_wait_ffn1"), "ag_pass_ffn1"))
    dh4, gu1, ff1, n1, loss_blk = _fwd_ffn(h3, small, wgu1, wd1, 1, target=target)

    dh3, dgu1, dff1, act1, sg_f1 = _bwd_ffn(dh4, h3, ff1, gu1, small, wgu1, wd1, 1)
    parts_d1 = _weight_grad(act1, dff1, fc, d // 2, "b", "dw_down1")
    parts_gu1 = _weight_grad(n1, dgu1, d // 2, fc, "a", "dw_gate_up1")
    ex_ffn1 = _SplitExchange([parts_d1.reshape(N_CHIPS, fq, d // 2), parts_gu1])
    started = ex_ffn1.start("rs_start_ffn1")
    dh2, dproj, dyv, bcv, sg_c = _bwd_conv(dh3, h2, y, proj, small, win_f, wout_f, [started])
    parts_in = _weight_grad(nc, dproj, d // 2, 3 * d // N_CHIPS, "a", "dw_conv_in")
    parts_out = _weight_grad(bcv, dyv, dq // 2, d, "a", "dw_conv_out")
    ex_conv = _SplitExchange([parts_in, parts_out])
    started = ex_conv.start("rs_start_conv")
    dh1, dgu0, dff0, act0, sg_f0 = _bwd_ffn(dh2, h1, ff0, gu0, small, wgu0, wd0, 0, [started])
    parts_d0 = _weight_grad(act0, dff0, fc, d // 2, "b", "dw_down0")
    parts_gu0 = _weight_grad(n0, dgu0, d // 2, fc, "a", "dw_gate_up0")
    ex_ffn0 = _SplitExchange([parts_d0.reshape(N_CHIPS, fq, d // 2), parts_gu0])
    started = ex_ffn0.start("rs_start_ffn0")
    grad_x, dpool, sg_p = _bwd_pool(dh1, x2, small, pool_scale, poolw, [started])
    g_pool = dpool.astype(BF16).reshape(2, ng // 2, N_CHIPS, gw // N_CHIPS, gw).transpose(2, 0, 1, 3, 4).reshape(
        N_CHIPS, 2, ng // 2 * (gw // N_CHIPS), gw)
    small_g = jnp.concatenate(
        [sg_p[0:2], sg_f0[0:2], sg_c[0:2], sg_f1[0:2], sg_c[2:5], sg_p[2:3],
         jnp.broadcast_to(loss_blk[0:1, 0:1], (1, d)), jnp.zeros((3, d), F32)], axis=0)
    ex_pool = _SplitExchange([_add_sibling(g_pool, _sibling_exchange(g_pool, "rs_sibling_pool"), core_arr)])
    started = ex_pool.start("rs_start_pool")

    def update(w, g, m, v, name):
        if w.size * 4 * 8 <= STREAM_BUDGET // 4:
            def body(w_ref, g_ref, m_ref, v_ref, go_ref, d_ref, nm_ref, nv_ref):
                go_ref[...] = g_ref[...]
                d_ref[...], nm_ref[...], nv_ref[...] = _adamw_math(w_ref[...], g_ref[...], m_ref[...], v_ref[...])

            return tuple(pl.pallas_call(body, name="adamw_" + name, out_shape=[jax.ShapeDtypeStruct(w.shape, F32)] * 4)(
                w, g.reshape(w.shape), m, v))
        flat = (-1, w.shape[-1])
        dl, m2, v2, g2 = _adamw(w.reshape(flat), g.reshape(flat), m.reshape(flat), v.reshape(flat), "adamw_" + name)
        return tuple(o.reshape(w.shape) for o in (g2, dl, m2, v2))

    (parts_d1, parts_gu1), (recv_d1, recv_gu1) = ex_ffn1.wait([started], "rs_wait_ffn1")
    (parts_in, parts_out), (recv_in, recv_out) = ex_conv.wait([started], "rs_wait_conv")
    gs_gu = _add_chips(parts_gu1, recv_gu1, where_arr, 1, 2)
    gs_d = _add_chips(parts_d1, recv_d1, where_arr, 1, 2, col_half=True)
    gs_in = _add_chips(parts_in, recv_in, where_arr)
    gs_out = _add_chips(parts_out, recv_out, where_arr)
    full_in, full_out = _sibling_share([gs_in, gs_out], [False, False], "rs_share_conv")
    up_in = update(conv_in_w, full_in.reshape(1, d, -1), m_conv_in_w, v_conv_in_w, "conv_in")
    up_out = update(conv_out_w, full_out.reshape(1, dq, d), m_conv_out_w, v_conv_out_w, "conv_out")
    done_first = [up_in[1], up_out[1], gs_gu, gs_d]
    (parts_d0, parts_gu0), (recv_d0, recv_gu0) = ex_ffn0.wait(done_first, "rs_wait_ffn0")
    (parts_p,), (recv_p,) = ex_pool.wait(done_first, "rs_wait_pool")
    ex_small = _SplitExchange([small_g], _SmallGather(small_g))
    started = ex_small.start("rs_start_small", [recv_gu0])
    gs_gu = _add_chips(parts_gu0, recv_gu0, where_arr, 0, 2, gs_gu)
    gs_d = _add_chips(parts_d0, recv_d0, where_arr, 0, 2, gs_d, col_half=True)
    gs_pool = _add_chips(parts_p, recv_p, where_arr)
    full_gu, full_d, full_pool = _sibling_share([gs_gu, gs_d, gs_pool], [False, True, False], "rs_share_ffn", [started])
    up_gu = update(ffn_gate_up_w, full_gu.reshape(2, d, fc), m_ffn_gate_up_w, v_ffn_gate_up_w, "gate_up")
    up_d = update(ffn_down_w, full_d.reshape(2, fq, d), m_ffn_down_w, v_ffn_down_w, "down")
    (small_own,), (small_all,) = ex_small.wait([up_gu[1], up_d[1]], "rs_wait_small")
    small_sum = _sum_small(small_all, small_own, (2 * chip + core).reshape(1).astype(jnp.int32))
    loss = small_sum[12, 0]
    up_gains, up_taps, up_scale = _adamw_small(
        small_sum, where_arr, (norm_gains, m_norm_gains, v_norm_gains), (conv_w, m_conv_w, v_conv_w),
        (pool_scale, m_pool_scale, v_pool_scale))

    ups = [
        up_gains,
        update(pool_w, full_pool.reshape(1, ng, gw // N_CHIPS, gw), m_pool_w, v_pool_w, "pool_w"),
        up_scale,
        up_in,
        up_taps,
        up_out,
        up_gu,
        up_d,
    ]
    grads_out, deltas, new_ms, new_vs = (list(col) for col in zip(*ups))
    return (loss, grad_x[None], *grads_out, *deltas, *new_ms, *new_vs)
```

```python
import jax
import jax.numpy as jnp
from jax import lax
from jax.experimental import pallas as pl
from jax.experimental.pallas import tpu as pltpu

RMS_EPS = 1e-6
POOL_WINDOWS = (2, 4, 8, 16)
POOL_HALO = 16
CONV_HALO = 8
N_CHIPS = 4
N_DEV = 8
ADAM_LR = 0.001
ADAM_B1 = 0.9
ADAM_B2 = 0.999
ADAM_EPS = 1e-08
ADAM_WD = 0.01
ADAM_STEP = 10
VMEM_LIMIT = 56 * 2**20
STREAM_BUDGET = 24 * 2**20
MESH = pl.DeviceIdType.MESH
ANY = pl.BlockSpec(memory_space=pl.ANY)
DMA = pltpu.SemaphoreType.DMA
BF16 = jnp.bfloat16
F32 = jnp.float32


TOKEN_TILE = 512
FFN_BWD_TOKEN_TILE = 256


def _token_tile(t, rows=TOKEN_TILE):
    return min(rows, t)


def _rms(x):
    r = lax.rsqrt(jnp.mean(x * x, axis=-1, keepdims=True) + RMS_EPS)
    return x * r, r


def _rms_bwd(dy, xh, r, g):
    a = dy * g
    return r * (a - xh * jnp.mean(a * xh, axis=-1, keepdims=True))


def _dot(a, b):
    return jnp.dot(a, b, preferred_element_type=F32)


def _dot_nt(a, b):
    return lax.dot_general(a, b, (((1,), (1,)), ((), ())), preferred_element_type=F32)


def _dot_tn(a, b):
    return lax.dot_general(a, b, (((0,), (0,)), ((), ())), preferred_element_type=F32)


def _colsum(a):
    return jnp.sum(a, axis=0, keepdims=True)


def _resident(block, index_map):
    return pl.BlockSpec(block, index_map, pipeline_mode=pl.Buffered(1))


def _row_block(r, row_bytes):
    best = None
    for rb in range(16, r + 1, 16):
        if r % rb == 0 and rb * row_bytes <= STREAM_BUDGET:
            best = rb
    return best if best is not None else r


def _place():
    x, y, c = lax.axis_index("x"), lax.axis_index("y"), lax.axis_index("c")
    return x, y, c, 2 * x + y


def _dev(chip, core):
    return (chip // 2, chip % 2, core)


def _remote(src, dst, send_sem, recv_sem, device):
    return pltpu.make_async_remote_copy(src_ref=src, dst_ref=dst, send_sem=send_sem, recv_sem=recv_sem,
                                        device_id=device, device_id_type=MESH)


class _Gather:
    def __init__(self, shards):
        self.args = [s for s, _ in shards]
        self.layers = [l for _, l in shards]
        self.out_shape = [jax.ShapeDtypeStruct((N_CHIPS,) + s.shape[1:], s.dtype) for s in self.args]

    def _own(self, loc, out, sems, a):
        x, y, c, k = _place()
        return _remote(loc[a].at[self.layers[a]], out[a].at[k], sems[0].at[a], sems[1].at[a], (x, y, 1 - c))

    def _ici(self, loc, out, sems, a, m, arrival):
        x, y, c, k = _place()
        dst = out[a].at[k ^ m, c] if arrival else out[a].at[k, c]
        return _remote(loc[a].at[self.layers[a], c], dst, sems[2].at[a, m - 1], sems[3].at[a, m - 1], _dev(k ^ m, c))

    def start(self, loc, out, sems):
        for a in range(len(self.args)):
            for m in range(1, N_CHIPS):
                self._ici(loc, out, sems, a, m, False).start()
            self._own(loc, out, sems, a).start()


class _SmallGather:
    PEERS = N_DEV - 1

    def __init__(self, small):
        self.args = [small]
        self.out_shape = [jax.ShapeDtypeStruct((N_DEV,) + small.shape, small.dtype)]

    def _copy(self, sm, land, sems, m, arrival):
        x, y, c, k = _place()
        me = 2 * k + c
        peer = me ^ m
        return _remote(sm[0], land[0].at[peer if arrival else me], sems[0].at[0, m - 1], sems[1].at[0, m - 1],
                       (peer // 4, (peer // 2) % 2, peer % 2))

    def start(self, sm, land, sems):
        for m in range(1, N_DEV):
            self._copy(sm, land, sems, m, False).start()

    def finish(self, sm, land, sems):
        for m in range(1, N_DEV):
            self._copy(sm, land, sems, m, True).wait_recv()
        for m in range(1, N_DEV):
            self._copy(sm, land, sems, m, False).wait_send()


class _ChipExchange:
    PEERS = N_CHIPS - 1

    def __init__(self, parts):
        self.args = list(parts)
        self.out_shape = [jax.ShapeDtypeStruct((N_CHIPS - 1,) + p.shape[1:], p.dtype) for p in parts]

    def _copy(self, p, land, sems, a, m):
        x, y, c, k = _place()
        return _remote(p[a].at[k ^ m], land[a].at[m - 1], sems[0].at[a, m - 1], sems[1].at[a, m - 1], _dev(k ^ m, c))

    def start(self, p, land, sems):
        for a in range(len(self.args)):
            for m in range(1, N_CHIPS):
                self._copy(p, land, sems, a, m).start()

    def finish(self, p, land, sems):
        for a in range(len(self.args)):
            for m in range(1, N_CHIPS):
                self._copy(p, land, sems, a, m).wait_recv()
        for a in range(len(self.args)):
            for m in range(1, N_CHIPS):
                self._copy(p, land, sems, a, m).wait_send()


def _gridded(body, after, *, name, grid, in_specs, out_specs, out_shape, args, scratch_shapes=()):
    ni, na = len(in_specs), len(after)

    def full(*refs):
        body(*refs[:ni], *refs[ni + na:])

    return list(pl.pallas_call(
        full, name=name, grid=grid, in_specs=[*in_specs, *[ANY] * na], out_specs=list(out_specs),
        out_shape=list(out_shape), scratch_shapes=list(scratch_shapes),
        compiler_params=pltpu.CompilerParams(dimension_semantics=("arbitrary",) * len(grid), vmem_limit_bytes=VMEM_LIMIT),
    )(*args, *after))


HBM = pl.BlockSpec(memory_space=pltpu.HBM)
SEM = pl.BlockSpec(memory_space=pltpu.SEMAPHORE)
DATAFLOW = pltpu.SideEffectType.DATAFLOW_SIDE_EFFECTING


class _SplitGather:
    PER_ARRAY = 8

    def __init__(self, shards):
        self.plan = _Gather(shards)
        self.n = len(shards)

    @staticmethod
    def _tables(sems_of):
        class Table:
            def __init__(self, pick):
                self.pick = pick

            @property
            def at(self):
                return self

            def __getitem__(self, idx):
                return self.pick(idx)

        return [Table(lambda a: sems_of[a][0]), Table(lambda a: sems_of[a][1]),
                Table(lambda am: sems_of[am[0]][2 + am[1]]), Table(lambda am: sems_of[am[0]][5 + am[1]])]

    def start(self, name, after=()):
        n, plan, per, na = self.n, self.plan, self.PER_ARRAY, len(after)

        def body(*refs):
            loc, land = refs[:n], refs[n:2 * n]
            sems_of = {a: refs[2 * n + na + per * a:2 * n + na + per * (a + 1)] for a in range(n)}
            plan.start(loc, land, self._tables(sems_of))
            refs[-1][...] = jnp.zeros_like(refs[-1])

        lands = [pltpu.with_memory_space_constraint(lax.empty(o.shape, o.dtype), pltpu.HBM) for o in plan.out_shape]
        locs = [pltpu.with_memory_space_constraint(a, pltpu.HBM) for a in plan.args]
        res = pl.pallas_call(
            body, name=name,
            out_shape=[*[DMA(())] * (per * n),
                       *[pltpu.HBM(o.shape, o.dtype) for o in plan.out_shape],
                       jax.ShapeDtypeStruct((8, 128), F32)],
            in_specs=[HBM] * (2 * n) + [pl.BlockSpec(memory_space=pl.ANY)] * na,
            out_specs=[SEM] * (per * n) + [HBM] * n + [pl.BlockSpec(memory_space=pltpu.VMEM)],
            input_output_aliases={n + i: per * n + i for i in range(n)},
            compiler_params=pltpu.CompilerParams(has_side_effects=DATAFLOW),
        )(*locs, *lands, *after)
        self.sems = {a: list(res[per * a:per * (a + 1)]) for a in range(n)}
        self.locs = locs
        self.lands = list(res[per * n:per * n + n])
        self.token = res[-1]

    def wait(self, idxs, after, name):
        plan, g, per = self.plan, len(idxs), self.PER_ARRAY

        def body(*refs):
            loc = {a: refs[j] for j, a in enumerate(idxs)}
            land = {a: refs[g + j] for j, a in enumerate(idxs)}
            sems = self._tables({a: refs[2 * g + per * j:2 * g + per * (j + 1)] for j, a in enumerate(idxs)})
            for a in idxs:
                for m in range(1, N_CHIPS):
                    plan._ici(loc, land, sems, a, m, True).wait_recv()
                    plan._ici(loc, land, sems, a, m, False).wait_send()
                plan._own(loc, land, sems, a).wait_recv()
                plan._own(loc, land, sems, a).wait_send()

        res = pl.pallas_call(
            body, name=name,
            out_shape=[pltpu.HBM(self.lands[a].shape, self.lands[a].dtype) for a in idxs],
            in_specs=[HBM] * (2 * g) + [SEM] * (per * g) + [pl.BlockSpec(memory_space=pl.ANY)], out_specs=[HBM] * g,
            input_output_aliases={g + j: j for j in range(g)},
            compiler_params=pltpu.CompilerParams(has_side_effects=DATAFLOW),
        )(*[self.locs[a] for a in idxs], *[self.lands[a] for a in idxs],
          *[s for a in idxs for s in self.sems[a]], after)
        return list(res)


class _SplitExchange:
    def __init__(self, parts, plan=None):
        self.plan = _ChipExchange(parts) if plan is None else plan
        self.n = len(parts)
        self.PER_ARRAY = 2 * self.plan.PEERS

    def _tables(self, sems_of):
        class Table:
            def __init__(self, pick):
                self.pick = pick

            @property
            def at(self):
                return self

            def __getitem__(self, am):
                return self.pick(am)

        peers = self.plan.PEERS
        return [Table(lambda am: sems_of[am[0]][am[1]]), Table(lambda am: sems_of[am[0]][peers + am[1]])]

    def start(self, name, after=()):
        n, plan, per, na = self.n, self.plan, self.PER_ARRAY, len(after)

        def body(*refs):
            p, land = refs[:n], refs[n:2 * n]
            sems_of = {a: refs[2 * n + na + per * a:2 * n + na + per * (a + 1)] for a in range(n)}
            plan.start(p, land, self._tables(sems_of))
            refs[-1][...] = jnp.zeros_like(refs[-1])

        lands = [pltpu.with_memory_space_constraint(lax.empty(o.shape, o.dtype), pltpu.HBM) for o in plan.out_shape]
        parts = [pltpu.with_memory_space_constraint(a, pltpu.HBM) for a in plan.args]
        res = pl.pallas_call(
            body, name=name,
            out_shape=[*[DMA(())] * (per * n),
                       *[pltpu.HBM(a.shape, a.dtype) for a in plan.args],
                       *[pltpu.HBM(o.shape, o.dtype) for o in plan.out_shape],
                       jax.ShapeDtypeStruct((8, 128), F32)],
            in_specs=[HBM] * (2 * n) + [pl.BlockSpec(memory_space=pl.ANY)] * na,
            out_specs=[SEM] * (per * n) + [HBM] * (2 * n) + [pl.BlockSpec(memory_space=pltpu.VMEM)],
            input_output_aliases={i: per * n + i for i in range(2 * n)},
            compiler_params=pltpu.CompilerParams(has_side_effects=DATAFLOW),
        )(*parts, *lands, *after)
        self.sems = list(res[:per * n])
        self.parts = list(res[per * n:per * n + n])
        self.lands = list(res[per * n + n:per * n + 2 * n])
        return res[-1]

    def wait(self, after, name):
        n, plan, per = self.n, self.plan, self.PER_ARRAY

        def body(*refs):
            p, land = refs[:n], refs[n:2 * n]
            sems_of = {a: refs[2 * n + per * a:2 * n + per * (a + 1)] for a in range(n)}
            plan.finish(p, land, self._tables(sems_of))

        res = pl.pallas_call(
            body, name=name,
            out_shape=[*[pltpu.HBM(a.shape, a.dtype) for a in self.parts], *[pltpu.HBM(a.shape, a.dtype) for a in self.lands]],
            in_specs=[HBM] * (2 * n) + [SEM] * (per * n) + [pl.BlockSpec(memory_space=pl.ANY)] * len(after),
            out_specs=[HBM] * (2 * n), input_output_aliases={i: i for i in range(2 * n)},
            compiler_params=pltpu.CompilerParams(has_side_effects=DATAFLOW),
        )(*self.parts, *self.lands, *self.sems, *after)
        return list(res[:n]), list(res[n:])


PASS_ON_BARRIER = 1


def _sibling_barrier():
    x, y, c, _ = _place()
    barrier = pltpu.get_barrier_semaphore()
    pl.semaphore_signal(barrier, inc=1, device_id=(x, y, 1 - c), device_id_type=MESH)
    pl.semaphore_wait(barrier, 1)


def _pass_on(lands, name):
    n = len(lands)

    def body(*refs):
        out = refs[n:2 * n]
        send_sems, recv_sems = refs[2 * n:]
        x, y, c, k = _place()
        _sibling_barrier()
        cps = []
        for a in range(n):
            for m in range(1, N_CHIPS):
                got = out[a].at[k ^ m, c]
                cp = _remote(got, got, send_sems.at[a, m - 1], recv_sems.at[a, m - 1], (x, y, 1 - c))
                cp.start()
                cps.append(cp)
        for a in range(n):
            for m in range(1, N_CHIPS):
                theirs = out[a].at[k ^ m, 1 - c]
                _remote(theirs, theirs, send_sems.at[a, m - 1], recv_sems.at[a, m - 1], (x, y, 1 - c)).wait_recv()
        for cp in cps:
            cp.wait_send()

    return pl.pallas_call(
        body, name=name, out_shape=[jax.ShapeDtypeStruct(a.shape, a.dtype) for a in lands],
        in_specs=[ANY] * n, out_specs=[ANY] * n, input_output_aliases={a: a for a in range(n)},
        scratch_shapes=[DMA((n, 3)), DMA((n, 3))],
        compiler_params=pltpu.CompilerParams(has_side_effects=True, collective_id=PASS_ON_BARRIER),
    )(*lands)


def _sibling_exchange(g, name):
    def body(g_ref, land_ref, send_sem, recv_sem):
        x, y, c, _ = _place()
        cp = _remote(g_ref.at[:, pl.ds(1 - c, 1)], land_ref, send_sem, recv_sem, (x, y, 1 - c))
        cp.start()
        cp.wait_recv()
        cp.wait_send()

    return pl.pallas_call(
        body, name=name, out_shape=jax.ShapeDtypeStruct((N_CHIPS, 1) + g.shape[2:], g.dtype), in_specs=[ANY],
        out_specs=ANY, scratch_shapes=[DMA, DMA], compiler_params=pltpu.CompilerParams(has_side_effects=True),
    )(g)


def _sibling_share(halves, col_half, name, after=()):
    n, na = len(halves), len(after)

    def body(*refs):
        out = refs[n + na:2 * n + na]
        send_sems, recv_sems = refs[2 * n + na:]
        x, y, c, k = _place()

        def half(a, core):
            if not col_half[a]:
                return out[a].at[:, pl.ds(core, 1)]
            cols = out[a].shape[-1] // 2
            return out[a].at[:, :, pl.ds(pl.multiple_of(core * cols, cols), cols)]

        cps = []
        for a in range(n):
            cp = _remote(half(a, c), half(a, c), send_sems.at[a], recv_sems.at[a], (x, y, 1 - c))
            cp.start()
            cps.append(cp)
        for a in range(n):
            _remote(half(a, 1 - c), half(a, 1 - c), send_sems.at[a], recv_sems.at[a], (x, y, 1 - c)).wait_recv()
        for cp in cps:
            cp.wait_send()

    out_shape = [jax.ShapeDtypeStruct(a.shape, a.dtype) for a in halves]
    return pl.pallas_call(
        body, name=name, out_shape=out_shape, in_specs=[ANY] * (n + na), out_specs=[ANY] * n,
        input_output_aliases={a: a for a in range(n)}, scratch_shapes=[DMA((n,)), DMA((n,))],
        compiler_params=pltpu.CompilerParams(has_side_effects=True),
    )(*halves, *after)


def _add_sibling(g, land, core):
    _, _, r, c = g.shape
    rb = _row_block(r, c * (3 * 2 * 2 + 2 * 4))

    def body(core_ref, g_ref, l_ref, o_ref):
        o_ref[...] = (g_ref[...].astype(F32) + l_ref[...].astype(F32)).astype(o_ref.dtype)

    return pl.pallas_call(
        body, name="rs_add_sibling", out_shape=jax.ShapeDtypeStruct((N_CHIPS, r, c), g.dtype),
        grid_spec=pltpu.PrefetchScalarGridSpec(
            num_scalar_prefetch=1, grid=(N_CHIPS, r // rb),
            in_specs=[pl.BlockSpec((None, None, rb, c), lambda j, i, core_ref: (j, core_ref[0], i, 0)),
                      pl.BlockSpec((None, None, rb, c), lambda j, i, core_ref: (j, 0, i, 0))],
            out_specs=pl.BlockSpec((None, rb, c), lambda j, i, core_ref: (j, i, 0))),
        compiler_params=pltpu.CompilerParams(dimension_semantics=("parallel", "parallel"), vmem_limit_bytes=VMEM_LIMIT),
    )(core, g, land)


def _add_chips(part, land, where, layer=0, n_layers=1, into=None, col_half=False):
    _, r, c = part.shape
    rb = _row_block(r, c * (4 * 2 * 2 + 4 * 2 + 2 * 4))

    def body(where_ref, p_ref, l_ref, *rest):
        acc = p_ref[...].astype(F32)
        for m in range(N_CHIPS - 1):
            acc = acc + l_ref[m].astype(F32)
        rest[-1][...] = acc

    in_specs = [pl.BlockSpec((None, rb, c), lambda i, where_ref: (where_ref[0], i, 0)),
                pl.BlockSpec((N_CHIPS - 1, rb, c), lambda i, where_ref: (0, i, 0))]
    args = [where, part, land]
    if into is not None:
        in_specs.append(ANY)
        args.append(into)
    if col_half:
        out_shape = jax.ShapeDtypeStruct((n_layers, r, 2 * c), F32)
        out_spec = pl.BlockSpec((None, rb, c), lambda i, where_ref: (layer, i, where_ref[1]))
    else:
        out_shape = jax.ShapeDtypeStruct((n_layers, 2, r, c), F32)
        out_spec = pl.BlockSpec((None, None, rb, c), lambda i, where_ref: (layer, where_ref[1], i, 0))
    return pl.pallas_call(
        body, name="rs_add_chips", out_shape=out_shape,
        grid_spec=pltpu.PrefetchScalarGridSpec(
            num_scalar_prefetch=1, grid=(r // rb,), in_specs=in_specs, out_specs=out_spec),
        input_output_aliases={} if into is None else {3: 0},
        compiler_params=pltpu.CompilerParams(dimension_semantics=("parallel",), vmem_limit_bytes=VMEM_LIMIT),
    )(*args)


def _sum_small(smg, own, me):
    def body(me_ref, s_ref, own_ref, o_ref):
        o_ref[...] = jnp.zeros_like(o_ref)
        for j in range(N_DEV):
            @pl.when(me_ref[0] == j)
            def _():
                o_ref[...] += own_ref[...]

            @pl.when(me_ref[0] != j)
            def _():
                o_ref[...] += s_ref[j]

    return pl.pallas_call(
        body, name="rs_sum_small", out_shape=jax.ShapeDtypeStruct(smg.shape[1:], F32),
        grid_spec=pltpu.PrefetchScalarGridSpec(
            num_scalar_prefetch=1, grid=(1,),
            in_specs=[pl.BlockSpec(smg.shape, lambda i, me_ref: (0, 0, 0)), pl.BlockSpec(own.shape, lambda i, me_ref: (0, 0))],
            out_specs=pl.BlockSpec(own.shape, lambda i, me_ref: (0, 0))),
    )(me, smg, own)


def _pool_windows(ext_ref, g, gw, tm, first_row):
    w = POOL_WINDOWS[g]
    slab = ext_ref[:, g * gw:(g + 1) * gw]
    p, k = slab, 1
    while k < w:
        p = p + pltpu.roll(p, k, 0)
        k *= 2
    t = first_row + lax.broadcasted_iota(jnp.int32, (tm, 1), 0)
    cnt = jnp.minimum(t + 1, w).astype(F32)
    return p[POOL_HALO:] / cnt - slab[POOL_HALO:]


def _fwd_pool(x, small, scale, poolw, after=()):
    t, d = x.shape
    tm = _token_tile(t)
    gw = d // len(POOL_WINDOWS)

    def body(x_ref, sm_ref, sc_ref, w_ref, h_ref, ext_ref, mix_ref):
        i = pl.program_id(0)

        @pl.when(i == 0)
        def _():
            ext_ref[0:POOL_HALO, :] = jnp.zeros((POOL_HALO, d), F32)

        @pl.when(i > 0)
        def _():
            ext_ref[0:POOL_HALO, :] = ext_ref[tm:tm + POOL_HALO, :]

        xv = x_ref[...]
        xh, _ = _rms(xv)
        ext_ref[POOL_HALO:, :] = xh * sm_ref[0:1, :]
        for g in range(len(POOL_WINDOWS)):
            pooled = _pool_windows(ext_ref, g, gw, tm, i * tm)
            cols = slice(g * gw, (g + 1) * gw)
            mix_ref[:, cols] = _dot(pooled.astype(BF16), w_ref[g]) * sc_ref[:, cols]
        mh, _ = _rms(mix_ref[...])
        h_ref[...] = xv + mh * sm_ref[1:2, :]

    return _gridded(
        body, after, name="fwd_pool", grid=(t // tm,), out_shape=[jax.ShapeDtypeStruct((t, d), F32)],
        in_specs=[pl.BlockSpec((tm, d), lambda i: (i, 0)), _resident(small.shape, lambda i: (0, 0)),
                  _resident(scale.shape, lambda i: (0, 0)), _resident(poolw.shape, lambda i: (0, 0, 0))],
        out_specs=[pl.BlockSpec((tm, d), lambda i: (i, 0))],
        scratch_shapes=[pltpu.VMEM((POOL_HALO + tm, d), F32), pltpu.VMEM((tm, d), F32)],
        args=[x, small, scale, poolw])[0]


def _fwd_ffn(h, small, wgu, wd, layer, after=(), target=None):
    t, d = h.shape
    tm = _token_tile(t)
    steps = t // tm
    fc = wgu.shape[-1]
    f = 2 * fc
    g_in, g_out = 4 * layer + 2, 4 * layer + 3
    with_loss = target is not None

    def body(h_ref, *refs):
        if with_loss:
            t_ref, sm_ref, wgu_ref, wd_ref, o_ref, gu_ref, ff_ref, n_ref, l_ref, acc_ref = refs
        else:
            sm_ref, wgu_ref, wd_ref, o_ref, gu_ref, ff_ref, n_ref = refs
        hv = h_ref[...]
        hh, _ = _rms(hv)
        n = (hh * sm_ref[g_in:g_in + 1, :]).astype(BF16)
        n_ref[...] = n
        ff = None
        for j in range(2):
            gate = _dot(n, wgu_ref[j])
            up = _dot(n, wgu_ref[2 + j])
            gu_ref[:, j * fc:(j + 1) * fc] = gate.astype(BF16)
            gu_ref[:, f + j * fc:f + (j + 1) * fc] = up.astype(BF16)
            act = (gate * jax.nn.sigmoid(gate) * up).astype(BF16)
            part = _dot(act, wd_ref[j * fc:(j + 1) * fc, :])
            ff = part if ff is None else ff + part
        ff_ref[...] = ff
        fh, _ = _rms(ff)
        out = hv + fh * sm_ref[g_out:g_out + 1, :]
        if not with_loss:
            o_ref[...] = out
            return
        i = pl.program_id(0)
        e = out - t_ref[...]
        o_ref[...] = e * (1.0 / d)

        @pl.when(i == 0)
        def _():
            acc_ref[...] = jnp.zeros_like(acc_ref)

        acc_ref[...] += _colsum(e * e)

        @pl.when(i == steps - 1)
        def _():
            l_ref[...] = jnp.full(l_ref.shape, 0.5 / d, F32) * jnp.sum(acc_ref[...])

    row = lambda i: (i, 0)
    out_shape = [jax.ShapeDtypeStruct((t, d), F32), jax.ShapeDtypeStruct((t, 2 * f), BF16),
                 jax.ShapeDtypeStruct((t, d), F32), jax.ShapeDtypeStruct((t, d), BF16)]
    out_specs = [pl.BlockSpec((tm, d), row), pl.BlockSpec((tm, 2 * f), row), pl.BlockSpec((tm, d), row),
                 pl.BlockSpec((tm, d), row)]
    weight_specs = [_resident(small.shape, lambda i: (0, 0)), _resident(wgu.shape, lambda i: (0, 0, 0)),
                    _resident(wd.shape, lambda i: (0, 0))]
    if with_loss:
        return _gridded(
            body, after, name=f"fwd_ffn{layer}_loss", grid=(steps,),
            out_shape=out_shape + [jax.ShapeDtypeStruct((8, 128), F32)],
            in_specs=[pl.BlockSpec((tm, d), row), pl.BlockSpec((tm, d), row)] + weight_specs,
            out_specs=out_specs + [pl.BlockSpec((8, 128), lambda i: (0, 0))],
            scratch_shapes=[pltpu.VMEM((1, d), F32)], args=[h, target, small, wgu, wd])
    return _gridded(
        body, after, name=f"fwd_ffn{layer}", grid=(steps,), out_shape=out_shape,
        in_specs=[pl.BlockSpec((tm, d), row)] + weight_specs, out_specs=out_specs, args=[h, small, wgu, wd])


def _fwd_ffn_up(h, small, wgu, layer):
    t, d = h.shape
    tm = _token_tile(t)
    fc = wgu.shape[-1]
    f = 2 * fc
    g_in = 4 * layer + 2

    def body(h_ref, sm_ref, wgu_ref, gu_ref, n_ref, act_ref):
        hh, _ = _rms(h_ref[...])
        n = (hh * sm_ref[g_in:g_in + 1, :]).astype(BF16)
        n_ref[...] = n
        for j in range(2):
            gate = _dot(n, wgu_ref[j])
            up = _dot(n, wgu_ref[2 + j])
            gu_ref[:, j * fc:(j + 1) * fc] = gate.astype(BF16)
            gu_ref[:, f + j * fc:f + (j + 1) * fc] = up.astype(BF16)
            act_ref[:, j * fc:(j + 1) * fc] = (gate * jax.nn.sigmoid(gate) * up).astype(BF16)

    row = lambda i: (i, 0)
    return _gridded(
        body, (), name=f"fwd_ffn{layer}_up", grid=(t // tm,),
        out_shape=[jax.ShapeDtypeStruct((t, 2 * f), BF16), jax.ShapeDtypeStruct((t, d), BF16),
                   jax.ShapeDtypeStruct((t, f), BF16)],
        in_specs=[pl.BlockSpec((tm, d), row), _resident(small.shape, lambda i: (0, 0)),
                  _resident(wgu.shape, lambda i: (0, 0, 0))],
        out_specs=[pl.BlockSpec((tm, 2 * f), row), pl.BlockSpec((tm, d), row), pl.BlockSpec((tm, f), row)],
        args=[h, small, wgu])


def _fwd_ffn_down(h, act, small, wd, layer):
    t, d = h.shape
    tm = _token_tile(t)
    f = wd.shape[0]
    g_out = 4 * layer + 3

    def body(h_ref, act_ref, sm_ref, wd_ref, o_ref, ff_ref):
        ff = _dot(act_ref[...], wd_ref[...])
        ff_ref[...] = ff
        fh, _ = _rms(ff)
        o_ref[...] = h_ref[...] + fh * sm_ref[g_out:g_out + 1, :]

    row = lambda i: (i, 0)
    return _gridded(
        body, (), name=f"fwd_ffn{layer}_down", grid=(t // tm,),
        out_shape=[jax.ShapeDtypeStruct((t, d), F32), jax.ShapeDtypeStruct((t, d), F32)],
        in_specs=[pl.BlockSpec((tm, d), row), pl.BlockSpec((tm, f), row), _resident(small.shape, lambda i: (0, 0)),
                  _resident(wd.shape, lambda i: (0, 0))],
        out_specs=[pl.BlockSpec((tm, d), row), pl.BlockSpec((tm, d), row)], args=[h, act, small, wd])


def _fwd_conv(h, small, win, wout, after=()):
    t, d = h.shape
    tm = _token_tile(t)
    pc = win.shape[-1]

    def body(h_ref, sm_ref, win_ref, wout_ref, o_ref, proj_ref, y_ref, n_ref, pj_ref, uext_ref):
        i = pl.program_id(0)

        @pl.when(i == 0)
        def _():
            uext_ref[0:CONV_HALO, :] = jnp.zeros((CONV_HALO, d), F32)

        @pl.when(i > 0)
        def _():
            uext_ref[0:CONV_HALO, :] = uext_ref[tm:tm + CONV_HALO, :]

        hv = h_ref[...]
        hh, _ = _rms(hv)
        n = (hh * sm_ref[4:5, :]).astype(BF16)
        n_ref[...] = n
        for k in range(N_CHIPS):
            pj_ref[:, k * pc:(k + 1) * pc] = _dot(n, win_ref[k])
        proj_ref[...] = pj_ref[...].astype(BF16)
        uext_ref[CONV_HALO:, :] = pj_ref[:, d:2 * d] * pj_ref[:, 2 * d:]
        taps = [sm_ref[8 + j:9 + j, :] for j in range(3)]
        full = uext_ref[...]
        conv = (full[CONV_HALO:] * taps[2] + pltpu.roll(full, 1, 0)[CONV_HALO:] * taps[1]
                + pltpu.roll(full, 2, 0)[CONV_HALO:] * taps[0])
        y = _dot((pj_ref[:, 0:d] * conv).astype(BF16), wout_ref[...])
        y_ref[...] = y
        yh, _ = _rms(y)
        o_ref[...] = hv + yh * sm_ref[5:6, :]

    row = lambda i: (i, 0)
    return _gridded(
        body, after, name="fwd_conv", grid=(t // tm,),
        out_shape=[jax.ShapeDtypeStruct((t, d), F32), jax.ShapeDtypeStruct((t, 3 * d), BF16),
                   jax.ShapeDtypeStruct((t, d), F32), jax.ShapeDtypeStruct((t, d), BF16)],
        in_specs=[pl.BlockSpec((tm, d), row), _resident(small.shape, lambda i: (0, 0)),
                  _resident(win.shape, lambda i: (0, 0, 0)), _resident(wout.shape, lambda i: (0, 0))],
        out_specs=[pl.BlockSpec((tm, d), row), pl.BlockSpec((tm, 3 * d), row), pl.BlockSpec((tm, d), row),
                   pl.BlockSpec((tm, d), row)],
        scratch_shapes=[pltpu.VMEM((tm, 3 * d), F32), pltpu.VMEM((CONV_HALO + tm, d), F32)],
        args=[h, small, win, wout])


def _bwd_ffn(dh, h, ff, gu, small, wgu, wd, layer, after=()):
    t, d = h.shape
    tm = _token_tile(t, FFN_BWD_TOKEN_TILE)
    fc = wgu.shape[-1]
    f = 2 * fc
    g_in, g_out = 4 * layer + 2, 4 * layer + 3

    def body(dh_ref, h_ref, ff_ref, gu_ref, sm_ref, wgu_ref, wd_ref, o_ref, dgu_ref, dff_ref, act_ref, sg_ref):
        i = pl.program_id(0)

        @pl.when(i == 0)
        def _():
            sg_ref[...] = jnp.zeros_like(sg_ref)

        dy = dh_ref[...]
        fh, r3 = _rms(ff_ref[...])
        sg_ref[1:2, :] += _colsum(dy * fh)
        dff = _rms_bwd(dy, fh, r3, sm_ref[g_out:g_out + 1, :]).astype(BF16)
        dff_ref[...] = dff
        for j in range(2):
            dact = _dot_nt(dff, wd_ref[j * fc:(j + 1) * fc, :])
            gate = gu_ref[:, j * fc:(j + 1) * fc].astype(F32)
            up = gu_ref[:, f + j * fc:f + (j + 1) * fc].astype(F32)
            sig = jax.nn.sigmoid(gate)
            silu = gate * sig
            act_ref[:, j * fc:(j + 1) * fc] = (silu * up).astype(BF16)
            dgu_ref[:, j * fc:(j + 1) * fc] = (dact * up * (sig * (1.0 + gate * (1.0 - sig)))).astype(BF16)
            dgu_ref[:, f + j * fc:f + (j + 1) * fc] = (dact * silu).astype(BF16)
        dn = None
        for k in range(N_CHIPS):
            part = _dot_nt(dgu_ref[:, k * fc:(k + 1) * fc], wgu_ref[k])
            dn = part if dn is None else dn + part
        hh, r2 = _rms(h_ref[...])
        sg_ref[0:1, :] += _colsum(dn * hh)
        o_ref[...] = dy + _rms_bwd(dn, hh, r2, sm_ref[g_in:g_in + 1, :])

    row = lambda i: (i, 0)
    return _gridded(
        body, after, name=f"bwd_ffn{layer}", grid=(t // tm,),
        out_shape=[jax.ShapeDtypeStruct((t, d), F32), jax.ShapeDtypeStruct((t, 2 * f), BF16),
                   jax.ShapeDtypeStruct((t, d), BF16), jax.ShapeDtypeStruct((t, f), BF16),
                   jax.ShapeDtypeStruct((8, d), F32)],
        in_specs=[pl.BlockSpec((tm, d), row), pl.BlockSpec((tm, d), row), pl.BlockSpec((tm, d), row),
                  pl.BlockSpec((tm, 2 * f), row), _resident(small.shape, lambda i: (0, 0)),
                  _resident(wgu.shape, lambda i: (0, 0, 0)), _resident(wd.shape, lambda i: (0, 0))],
        out_specs=[pl.BlockSpec((tm, d), row), pl.BlockSpec((tm, 2 * f), row), pl.BlockSpec((tm, d), row),
                   pl.BlockSpec((tm, f), row), pl.BlockSpec((8, d), lambda i: (0, 0))],
        args=[dh, h, ff, gu, small, wgu, wd])


def _bwd_conv(dh, h, y, proj, small, win, wout, after=()):
    t, d = h.shape
    tm = _token_tile(t)
    steps = t // tm
    pc = win.shape[-1]
    halo_blocks = tm // 16

    def body(dh_ref, h_ref, y_ref, proj_ref, halo_ref, sm_ref, win_ref, wout_ref,
             o_ref, dproj_ref, dy_ref, bc_ref, sg_ref, uext_ref, dcext_ref, carry_ref):
        i = pl.program_id(0)
        tile = steps - 1 - i

        @pl.when(i == 0)
        def _():
            sg_ref[...] = jnp.zeros_like(sg_ref)
            carry_ref[...] = jnp.zeros_like(carry_ref)

        dy = dh_ref[...]
        yh, r1 = _rms(y_ref[...])
        sg_ref[1:2, :] += _colsum(dy * yh)
        dyv = _rms_bwd(dy, yh, r1, sm_ref[5:6, :]).astype(BF16)
        dy_ref[...] = dyv
        dbc = _dot_nt(dyv, wout_ref[...])
        b = proj_ref[:, 0:d].astype(F32)
        cg = proj_ref[:, d:2 * d].astype(F32)
        v = proj_ref[:, 2 * d:].astype(F32)
        halo = halo_ref[...].astype(F32)[16 - CONV_HALO:]
        uh = halo[:, d:2 * d] * halo[:, 2 * d:]
        uext_ref[0:CONV_HALO, :] = jnp.where(tile > 0, uh, jnp.zeros_like(uh))
        uext_ref[CONV_HALO:, :] = cg * v
        taps = [sm_ref[8 + j:9 + j, :] for j in range(3)]
        full = uext_ref[...]
        u0 = full[CONV_HALO:]
        u1 = pltpu.roll(full, 1, 0)[CONV_HALO:]
        u2 = pltpu.roll(full, 2, 0)[CONV_HALO:]
        conv = u0 * taps[2] + u1 * taps[1] + u2 * taps[0]
        bc_ref[...] = (b * conv).astype(BF16)
        dconv = dbc * b
        sg_ref[4:5, :] += _colsum(dconv * u0)
        sg_ref[3:4, :] += _colsum(dconv * u1)
        sg_ref[2:3, :] += _colsum(dconv * u2)
        dcext_ref[0:tm, :] = dconv
        dcext_ref[tm:, :] = carry_ref[...]
        carry_ref[...] = dconv[0:CONV_HALO]
        dfull = dcext_ref[...]
        n8 = tm + CONV_HALO
        du = (dfull[0:tm] * taps[2] + pltpu.roll(dfull, n8 - 1, 0)[0:tm] * taps[1]
              + pltpu.roll(dfull, n8 - 2, 0)[0:tm] * taps[0])
        dproj_ref[:, 0:d] = (dbc * conv).astype(BF16)
        dproj_ref[:, d:2 * d] = (du * v).astype(BF16)
        dproj_ref[:, 2 * d:] = (du * cg).astype(BF16)
        dn = None
        for k in range(N_CHIPS):
            part = _dot_nt(dproj_ref[:, k * pc:(k + 1) * pc], win_ref[k])
            dn = part if dn is None else dn + part
        hh, r0 = _rms(h_ref[...])
        sg_ref[0:1, :] += _colsum(dn * hh)
        o_ref[...] = dy + _rms_bwd(dn, hh, r0, sm_ref[4:5, :])

    rev = lambda i: (steps - 1 - i, 0)
    before = lambda i: (jnp.maximum((steps - 1 - i) * halo_blocks - 1, 0), 0)
    return _gridded(
        body, after, name="bwd_conv", grid=(steps,),
        out_shape=[jax.ShapeDtypeStruct((t, d), F32), jax.ShapeDtypeStruct((t, 3 * d), BF16),
                   jax.ShapeDtypeStruct((t, d), BF16), jax.ShapeDtypeStruct((t, d), BF16),
                   jax.ShapeDtypeStruct((8, d), F32)],
        in_specs=[pl.BlockSpec((tm, d), rev), pl.BlockSpec((tm, d), rev), pl.BlockSpec((tm, d), rev),
                  pl.BlockSpec((tm, 3 * d), rev), pl.BlockSpec((16, 3 * d), before),
                  _resident(small.shape, lambda i: (0, 0)), _resident(win.shape, lambda i: (0, 0, 0)),
                  _resident(wout.shape, lambda i: (0, 0))],
        out_specs=[pl.BlockSpec((tm, d), rev), pl.BlockSpec((tm, 3 * d), rev), pl.BlockSpec((tm, d), rev),
                   pl.BlockSpec((tm, d), rev), pl.BlockSpec((8, d), lambda i: (0, 0))],
        scratch_shapes=[pltpu.VMEM((CONV_HALO + tm, d), F32), pltpu.VMEM((tm + CONV_HALO, d), F32),
                        pltpu.VMEM((CONV_HALO, d), F32)],
        args=[dh, h, y, proj, proj, small, win, wout])


def _bwd_pool(dh, x, small, scale, poolw, after=()):
    t, d = x.shape
    tm = _token_tile(t)
    steps = t // tm
    ng = len(POOL_WINDOWS)
    gw = d // ng
    halo_blocks = tm // POOL_HALO

    def body(dh_ref, x_ref, halo_ref, sm_ref, sc_ref, w_ref, o_ref, dw_ref, sg_ref,
             ext_ref, mix_ref, mm_ref, pb_ref, qext_ref, dhn_ref, carry_ref):
        i = pl.program_id(0)
        tile = steps - 1 - i

        @pl.when(i == 0)
        def _():
            sg_ref[...] = jnp.zeros_like(sg_ref)
            dw_ref[...] = jnp.zeros_like(dw_ref)
            carry_ref[...] = jnp.zeros_like(carry_ref)

        g0 = sm_ref[0:1, :]
        xv = x_ref[...]
        xh, r0 = _rms(xv)
        hx, _ = _rms(halo_ref[...])
        ext_ref[0:POOL_HALO, :] = jnp.where(tile > 0, hx * g0, jnp.zeros_like(hx))
        ext_ref[POOL_HALO:, :] = xh * g0
        for g in range(ng):
            pooled = _pool_windows(ext_ref, g, gw, tm, tile * tm)
            cols = slice(g * gw, (g + 1) * gw)
            pb = pooled.astype(BF16)
            pb_ref[:, cols] = pb
            mm = _dot(pb, w_ref[g])
            mm_ref[:, cols] = mm
            mix_ref[:, cols] = mm * sc_ref[:, cols]
        dy = dh_ref[...]
        mh, r1 = _rms(mix_ref[...])
        sg_ref[1:2, :] += _colsum(dy * mh)
        dmix = _rms_bwd(dy, mh, r1, sm_ref[1:2, :])
        sg_ref[2:3, :] += _colsum(dmix * mm_ref[...])
        mix_ref[...] = dmix * sc_ref[...]
        n16 = tm + POOL_HALO
        for g in range(ng):
            w = POOL_WINDOWS[g]
            cols = slice(g * gw, (g + 1) * gw)
            dmm = mix_ref[:, cols].astype(BF16)
            dpooled = _dot_nt(dmm, w_ref[g])
            dw_ref[g] += _dot_tn(pb_ref[:, cols], dmm)
            trow = tile * tm + lax.broadcasted_iota(jnp.int32, (tm, 1), 0)
            q = dpooled / jnp.minimum(trow + 1, w).astype(F32)
            qext_ref[0:tm, cols] = q
            qext_ref[tm:, cols] = carry_ref[:, cols]
            carry_ref[:, cols] = q[0:POOL_HALO]
            p, k = qext_ref[:, cols], 1
            while k < w:
                p = p + pltpu.roll(p, n16 - k, 0)
                k *= 2
            dhn_ref[:, cols] = p[0:tm] - dpooled
        dhn = dhn_ref[...]
        sg_ref[0:1, :] += _colsum(dhn * xh)
        o_ref[...] = dy + _rms_bwd(dhn, xh, r0, g0)

    rev = lambda i: (steps - 1 - i, 0)
    before = lambda i: (jnp.maximum((steps - 1 - i) * halo_blocks - 1, 0), 0)
    return _gridded(
        body, after, name="bwd_pool", grid=(steps,),
        out_shape=[jax.ShapeDtypeStruct((t, d), F32), jax.ShapeDtypeStruct((ng, gw, gw), F32),
                   jax.ShapeDtypeStruct((8, d), F32)],
        in_specs=[pl.BlockSpec((tm, d), rev), pl.BlockSpec((tm, d), rev), pl.BlockSpec((POOL_HALO, d), before),
                  _resident(small.shape, lambda i: (0, 0)), _resident(scale.shape, lambda i: (0, 0)),
                  _resident(poolw.shape, lambda i: (0, 0, 0))],
        out_specs=[pl.BlockSpec((tm, d), rev), pl.BlockSpec((ng, gw, gw), lambda i: (0, 0, 0)),
                   pl.BlockSpec((8, d), lambda i: (0, 0))],
        scratch_shapes=[pltpu.VMEM((POOL_HALO + tm, d), F32), pltpu.VMEM((tm, d), F32), pltpu.VMEM((tm, d), F32),
                        pltpu.VMEM((tm, d), BF16), pltpu.VMEM((tm + POOL_HALO, d), F32), pltpu.VMEM((tm, d), F32),
                        pltpu.VMEM((POOL_HALO, d), F32)],
        args=[dh, x, x, small, scale, poolw])


def _weight_grad(a, b, bm, bn, half_on, name):
    t, m = a.shape
    _, n = b.shape
    if half_on == "a":
        a_cols, b_cols = 2 * bm, bn
    else:
        a_cols, b_cols = bm, 2 * bn
    steps = max(m // a_cols, n // b_cols)

    def spec(cols, total):
        if cols == total:
            return _resident((t, cols), lambda p, j: (0, 0))
        return pl.BlockSpec((t, cols), lambda p, j: (0, j))

    def tile(a_ref, b_ref, half):
        if half_on == "a":
            return _dot_tn(a_ref[:, half * bm:(half + 1) * bm], b_ref[...])
        return _dot_tn(a_ref[...], b_ref[:, half * bn:(half + 1) * bn])

    def body(a_ref, b_ref, parts_ref, land_ref, acc_ref, stage_ref, got_ref, send_sems, recv_sems, got_sem):
        p, j = pl.program_id(0), pl.program_id(1)
        x, y, c, _ = _place()
        half = jnp.where(p == 0, 1 - c, c)

        def send(jj):
            return _remote(stage_ref.at[jj % 2], land_ref.at[jj], send_sems.at[jj], recv_sems.at[jj], (x, y, 1 - c))

        def fetch():
            return pltpu.make_async_copy(land_ref.at[j], got_ref, got_sem)

        @pl.when(p == 1)
        def _():
            @pl.when(j == 0)
            def _():
                for jj in range(max(steps - 2, 0), steps):
                    send(jj).wait_send()

            send(j).wait_recv()
            fetch().start()

        for hv in range(2):
            @pl.when(half == hv)
            def _():
                acc_ref[...] = tile(a_ref, b_ref, hv)

        @pl.when(p == 0)
        def _():
            @pl.when(j >= 2)
            def _():
                send(j - 2).wait_send()

            stage_ref[j % 2] = acc_ref[...].astype(BF16)
            send(j).start()

        @pl.when(p == 1)
        def _():
            fetch().wait()
            parts_ref[...] = (acc_ref[...] + got_ref[...].astype(F32)).astype(BF16)

    return _gridded(
        body, (), name=name, grid=(2, steps),
        out_shape=[jax.ShapeDtypeStruct((steps, bm, bn), BF16), jax.ShapeDtypeStruct((steps, bm, bn), BF16)],
        in_specs=[spec(a_cols, m), spec(b_cols, n)],
        out_specs=[pl.BlockSpec((None, bm, bn), lambda p, j: (p * j, 0, 0)), ANY],
        scratch_shapes=[pltpu.VMEM((bm, bn), F32), pltpu.VMEM((2, bm, bn), BF16), pltpu.VMEM((bm, bn), BF16),
                        DMA((steps,)), DMA((steps,)), DMA],
        args=[a, b])[0]


def _cast_layer(w, layer, name, after=()):
    _, r, c = w.shape
    rb = _row_block(r, c * (4 + 2) * 2)

    def body(w_ref, *rest):
        rest[-1][...] = w_ref[...].astype(BF16)

    return pl.pallas_call(
        body, name=name, grid=(r // rb,), out_shape=jax.ShapeDtypeStruct((r, c), BF16),
        in_specs=[pl.BlockSpec((None, rb, c), lambda i: (layer, i, 0))] + [ANY] * len(after),
        out_specs=pl.BlockSpec((rb, c), lambda i: (i, 0)),
        compiler_params=pltpu.CompilerParams(dimension_semantics=("parallel",), vmem_limit_bytes=VMEM_LIMIT),
    )(w, *after)


def _adamw_math(w, g, m, v):
    bc1 = 1.0 - ADAM_B1 ** ADAM_STEP
    bc2 = 1.0 - ADAM_B2 ** ADAM_STEP
    nm = ADAM_B1 * m + (1.0 - ADAM_B1) * g
    nv = ADAM_B2 * v + (1.0 - ADAM_B2) * (g * g)
    return -ADAM_LR * ((nm / bc1) / (jnp.sqrt(nv / bc2) + ADAM_EPS) + ADAM_WD * w), nm, nv


def _adamw_small(small_sum, where, gains, taps, scale):
    dq = gains[0].shape[-1]
    d = small_sum.shape[-1]

    def body(where_ref, mine_ref, all_ref, gw, gm, gv, tw, tm_, tv, sw, sm, sv,
             gg, gd, gnm, gnv, tg, td, tnm, tnv, sg, sd, snm, snv):
        for layer in range(gw.shape[0]):
            g = mine_ref[4 * layer:4 * layer + 4, :]
            gg[layer] = g
            gd[layer], gnm[layer], gnv[layer] = _adamw_math(gw[layer], g, gm[layer], gv[layer])
        g = mine_ref[8:8 + tw.shape[1], :]
        tg[0] = g
        td[0], tnm[0], tnv[0] = _adamw_math(tw[0], g, tm_[0], tv[0])
        g = all_ref[11:12, :]
        sg[...] = g
        sd[...], snm[...], snv[...] = _adamw_math(sw[...], g, sm[...], sv[...])

    full = lambda a: pl.BlockSpec(a.shape, lambda i, where_ref: (0,) * a.ndim)
    params = [*gains, *taps, *scale]
    outs = [gains[0]] * 4 + [taps[0]] * 4 + [scale[0]] * 4
    res = pl.pallas_call(
        body, name="adamw_small", out_shape=[jax.ShapeDtypeStruct(a.shape, F32) for a in outs],
        grid_spec=pltpu.PrefetchScalarGridSpec(
            num_scalar_prefetch=1, grid=(1,),
            in_specs=[pl.BlockSpec((16, dq), lambda i, where_ref: (0, where_ref[0])), pl.BlockSpec((16, d), lambda i, where_ref: (0, 0)),
                      *[full(a) for a in params]],
            out_specs=[full(a) for a in outs]),
    )(where, small_sum, small_sum, *params)
    return tuple(res[0:4]), tuple(res[4:8]), tuple(res[8:12])


def _adamw(w, g, m, v, name):
    r, c = w.shape
    rb = _row_block(r, c * (8 * 4 * 2 + 4 * 4))

    def body(w_ref, g_ref, m_ref, v_ref, d_ref, nm_ref, nv_ref, go_ref):
        gv = g_ref[...]
        go_ref[...] = gv
        d_ref[...], nm_ref[...], nv_ref[...] = _adamw_math(w_ref[...], gv, m_ref[...], v_ref[...])

    spec = pl.BlockSpec((rb, c), lambda i: (i, 0))
    return pl.pallas_call(
        body, name=name, grid=(r // rb,), out_shape=[jax.ShapeDtypeStruct((r, c), F32)] * 4,
        in_specs=[spec] * 4, out_specs=[spec] * 4,
        compiler_params=pltpu.CompilerParams(dimension_semantics=("parallel",), vmem_limit_bytes=VMEM_LIMIT),
    )(w, g, m, v)


def kernel(x, norm_gains, pool_w, pool_scale, conv_in_w, conv_w, conv_out_w, ffn_gate_up_w, ffn_down_w, loss_target, m_norm_gains, m_pool_w, m_pool_scale, m_conv_in_w, m_conv_w, m_conv_out_w, m_ffn_gate_up_w, m_ffn_down_w, v_norm_gains, v_pool_w, v_pool_scale, v_conv_in_w, v_conv_w, v_conv_out_w, v_ffn_gate_up_w, v_ffn_down_w):
    _, t, d = x.shape
    dq = d // N_CHIPS
    ng = len(POOL_WINDOWS)
    gw = d // ng
    fq = ffn_down_w.shape[1]
    f = N_CHIPS * fq
    fc = f // 2
    core = lax.axis_index("c")
    chip = 2 * lax.axis_index("x") + lax.axis_index("y")
    core_arr = jnp.reshape(core, (1,)).astype(jnp.int32)
    where_arr = jnp.stack([chip, core]).astype(jnp.int32)
    x2, target = x[0], loss_target[0]

    small_loc = jnp.concatenate(
        [norm_gains.reshape(8, dq), conv_w[0], jnp.zeros((5, dq), F32)], axis=0).reshape(1, 2, 8, dq)
    pool_loc = pool_w.astype(BF16).reshape(1, 2, ng // 2 * (gw // N_CHIPS), gw)
    wgu_loc = [_cast_layer(ffn_gate_up_w, 0, "cast_gate_up0").reshape(1, 2, d // 2, fc),
               ffn_gate_up_w[1:2].astype(BF16).reshape(1, 2, d // 2, fc)]
    wd1_loc = ffn_down_w[1:2].astype(BF16).reshape(1, 2, fq // 2, d)
    win_loc = conv_in_w.astype(BF16).reshape(1, 2, d // 2, -1)
    wout_loc = conv_out_w.astype(BF16).reshape(1, 2, dq // 2, d)

    def ffn_weights(wgu_f, wd_f):
        return wgu_f.reshape(N_CHIPS, d, fc), wd_f.reshape(f, d)

    ag0 = _SplitGather([(pool_loc, 0), (small_loc, 0), (wgu_loc[0], 0)])
    ag0.start("ag_start_gate_up0")
    wd0_loc = _cast_layer(ffn_down_w, 0, "cast_down0", [ag0.token]).reshape(1, 2, fq // 2, d)
    ag0d = _SplitGather([(wd0_loc, 0)])
    ag0d.start("ag_start_down0", [ag0.token])
    ag1 = _SplitGather([(win_loc, 0), (wout_loc, 0), (wgu_loc[1], 0), (wd1_loc, 0)])
    ag1.start("ag_start_layer1", [ag0d.token])
    pool_f, small_f = _pass_on(ag0.wait([0, 1], ag1.token, "ag_wait_first"), "ag_pass_first")
    poolw = pool_f.reshape(N_CHIPS, ng, gw // N_CHIPS, gw).transpose(1, 0, 2, 3).reshape(ng, gw, gw)
    small = small_f.transpose(1, 2, 0, 3).reshape(16, d)
    h1 = _fwd_pool(x2, small, pool_scale, poolw)
    (wgu0,) = _pass_on(ag0.wait([2], h1, "ag_wait_gate_up0"), "ag_pass_gate_up0")
    wgu0 = wgu0.reshape(N_CHIPS, d, fc)
    gu0, n0, act_fwd0 = _fwd_ffn_up(h1, small, wgu0, 0)
    wd0, win_f, wout_f = _pass_on(
        ag0d.wait([0], gu0, "ag_wait_down0") + ag1.wait([0, 1], gu0, "ag_wait_conv"), "ag_pass_down0_conv")
    wd0 = wd0.reshape(f, d)
    h2, ff0 = _fwd_ffn_down(h1, act_fwd0, small, wd0, 0)
    win_f, wout_f = win_f.reshape(N_CHIPS, d, -1), wout_f.reshape(d, d)
    h3, proj, y, nc = _fwd_conv(h2, small, win_f, wout_f)
    wgu1, wd1 = ffn_weights(*_pass_on(ag1.wait([2, 3], h3, "ag_wait_ffn1"), "ag_pass_ffn1"))
    dh4, gu1, ff1, n1, loss_blk = _fwd_ffn(h3, small, wgu1, wd1, 1, target=target)

    dh3, dgu1, dff1, act1, sg_f1 = _bwd_ffn(dh4, h3, ff1, gu1, small, wgu1, wd1, 1)
    parts_d1 = _weight_grad(act1, dff1, fc, d // 2, "b", "dw_down1")
    parts_gu1 = _weight_grad(n1, dgu1, d // 2, fc, "a", "dw_gate_up1")
    ex_ffn1 = _SplitExchange([parts_d1.reshape(N_CHIPS, fq, d // 2), parts_gu1])
    started = ex_ffn1.start("rs_start_ffn1")
    dh2, dproj, dyv, bcv, sg_c = _bwd_conv(dh3, h2, y, proj, small, win_f, wout_f, [started])
    parts_in = _weight_grad(nc, dproj, d // 2, 3 * d // N_CHIPS, "a", "dw_conv_in")
    parts_out = _weight_grad(bcv, dyv, dq // 2, d, "a", "dw_conv_out")
    ex_conv = _SplitExchange([parts_in, parts_out])
    started = ex_conv.start("rs_start_conv")
    dh1, dgu0, dff0, act0, sg_f0 = _bwd_ffn(dh2, h1, ff0, gu0, small, wgu0, wd0, 0, [started])
    parts_d0 = _weight_grad(act0, dff0, fc, d // 2, "b", "dw_down0")
    parts_gu0 = _weight_grad(n0, dgu0, d // 2, fc, "a", "dw_gate_up0")
    ex_ffn0 = _SplitExchange([parts_d0.reshape(N_CHIPS, fq, d // 2), parts_gu0])
    started = ex_ffn0.start("rs_start_ffn0")
    grad_x, dpool, sg_p = _bwd_pool(dh1, x2, small, pool_scale, poolw, [started])
    g_pool = dpool.astype(BF16).reshape(2, ng // 2, N_CHIPS, gw // N_CHIPS, gw).transpose(2, 0, 1, 3, 4).reshape(
        N_CHIPS, 2, ng // 2 * (gw // N_CHIPS), gw)
    small_g = jnp.concatenate(
        [sg_p[0:2], sg_f0[0:2], sg_c[0:2], sg_f1[0:2], sg_c[2:5], sg_p[2:3],
         jnp.broadcast_to(loss_blk[0:1, 0:1], (1, d)), jnp.zeros((3, d), F32)], axis=0)
    ex_pool = _SplitExchange([_add_sibling(g_pool, _sibling_exchange(g_pool, "rs_sibling_pool"), core_arr)])
    started = ex_pool.start("rs_start_pool")

    def update(w, g, m, v, name):
        if w.size * 4 * 8 <= STREAM_BUDGET // 4:
            def body(w_ref, g_ref, m_ref, v_ref, go_ref, d_ref, nm_ref, nv_ref):
                go_ref[...] = g_ref[...]
                d_ref[...], nm_ref[...], nv_ref[...] = _adamw_math(w_ref[...], g_ref[...], m_ref[...], v_ref[...])

            return tuple(pl.pallas_call(body, name="adamw_" + name, out_shape=[jax.ShapeDtypeStruct(w.shape, F32)] * 4)(
                w, g.reshape(w.shape), m, v))
        flat = (-1, w.shape[-1])
        dl, m2, v2, g2 = _adamw(w.reshape(flat), g.reshape(flat), m.reshape(flat), v.reshape(flat), "adamw_" + name)
        return tuple(o.reshape(w.shape) for o in (g2, dl, m2, v2))

    (parts_d1, parts_gu1), (recv_d1, recv_gu1) = ex_ffn1.wait([started], "rs_wait_ffn1")
    (parts_in, parts_out), (recv_in, recv_out) = ex_conv.wait([started], "rs_wait_conv")
    gs_gu = _add_chips(parts_gu1, recv_gu1, where_arr, 1, 2)
    gs_d = _add_chips(parts_d1, recv_d1, where_arr, 1, 2, col_half=True)
    gs_in = _add_chips(parts_in, recv_in, where_arr)
    gs_out = _add_chips(parts_out, recv_out, where_arr)
    full_in, full_out = _sibling_share([gs_in, gs_out], [False, False], "rs_share_conv")
    up_in = update(conv_in_w, full_in.reshape(1, d, -1), m_conv_in_w, v_conv_in_w, "conv_in")
    up_out = update(conv_out_w, full_out.reshape(1, dq, d), m_conv_out_w, v_conv_out_w, "conv_out")
    done_first = [up_in[1], up_out[1], gs_gu, gs_d]
    (parts_d0, parts_gu0), (recv_d0, recv_gu0) = ex_ffn0.wait(done_first, "rs_wait_ffn0")
    (parts_p,), (recv_p,) = ex_pool.wait(done_first, "rs_wait_pool")
    ex_small = _SplitExchange([small_g], _SmallGather(small_g))
    started = ex_small.start("rs_start_small", [recv_gu0])
    gs_gu = _add_chips(parts_gu0, recv_gu0, where_arr, 0, 2, gs_gu)
    gs_d = _add_chips(parts_d0, recv_d0, where_arr, 0, 2, gs_d, col_half=True)
    gs_pool = _add_chips(parts_p, recv_p, where_arr)
    full_gu, full_d, full_pool = _sibling_share([gs_gu, gs_d, gs_pool], [False, True, False], "rs_share_ffn", [started])
    up_gu = update(ffn_gate_up_w, full_gu.reshape(2, d, fc), m_ffn_gate_up_w, v_ffn_gate_up_w, "gate_up")
    up_d = update(ffn_down_w, full_d.reshape(2, fq, d), m_ffn_down_w, v_ffn_down_w, "down")
    (small_own,), (small_all,) = ex_small.wait([up_gu[1], up_d[1]], "rs_wait_small")
    small_sum = _sum_small(small_all, small_own, (2 * chip + core).reshape(1).astype(jnp.int32))
    loss = small_sum[12, 0]
    up_gains, up_taps, up_scale = _adamw_small(
        small_sum, where_arr, (norm_gains, m_norm_gains, v_norm_gains), (conv_w, m_conv_w, v_conv_w),
        (pool_scale, m_pool_scale, v_pool_scale))

    ups = [
        up_gains,
        update(pool_w, full_pool.reshape(1, ng, gw // N_CHIPS, gw), m_pool_w, v_pool_w, "pool_w"),
        up_scale,
        up_in,
        up_taps,
        up_out,
        up_gu,
        up_d,
    ]
    grads_out, deltas, new_ms, new_vs = (list(col) for col in zip(*ups))
    return (loss, grad_x[None], *grads_out, *deltas, *new_ms, *new_vs)
```

```python
import jax
import jax.numpy as jnp
from jax import lax
from jax.experimental import pallas as pl
from jax.experimental.pallas import tpu as pltpu

RMS_EPS = 1e-6
POOL_WINDOWS = (2, 4, 8, 16)
POOL_HALO = 16
CONV_HALO = 8
N_CHIPS = 4
N_DEV = 8
ADAM_LR = 0.001
ADAM_B1 = 0.9
ADAM_B2 = 0.999
ADAM_EPS = 1e-08
ADAM_WD = 0.01
ADAM_STEP = 10
VMEM_LIMIT = 56 * 2**20
STREAM_BUDGET = 24 * 2**20
MESH = pl.DeviceIdType.MESH
ANY = pl.BlockSpec(memory_space=pl.ANY)
DMA = pltpu.SemaphoreType.DMA
BF16 = jnp.bfloat16
F32 = jnp.float32


TOKEN_TILE = 512
FFN_BWD_TOKEN_TILE = 256


def _token_tile(t, rows=TOKEN_TILE):
    return min(rows, t)


def _rms(x):
    r = lax.rsqrt(jnp.mean(x * x, axis=-1, keepdims=True) + RMS_EPS)
    return x * r, r


def _rms_bwd(dy, xh, r, g):
    a = dy * g
    return r * (a - xh * jnp.mean(a * xh, axis=-1, keepdims=True))


def _dot(a, b):
    return jnp.dot(a, b, preferred_element_type=F32)


def _dot_nt(a, b):
    return lax.dot_general(a, b, (((1,), (1,)), ((), ())), preferred_element_type=F32)


def _dot_tn(a, b):
    return lax.dot_general(a, b, (((0,), (0,)), ((), ())), preferred_element_type=F32)


def _colsum(a):
    return jnp.sum(a, axis=0, keepdims=True)


def _resident(block, index_map):
    return pl.BlockSpec(block, index_map, pipeline_mode=pl.Buffered(1))


def _row_block(r, row_bytes):
    best = None
    for rb in range(16, r + 1, 16):
        if r % rb == 0 and rb * row_bytes <= STREAM_BUDGET:
            best = rb
    return best if best is not None else r


def _place():
    x, y, c = lax.axis_index("x"), lax.axis_index("y"), lax.axis_index("c")
    return x, y, c, 2 * x + y


def _dev(chip, core):
    return (chip // 2, chip % 2, core)


def _remote(src, dst, send_sem, recv_sem, device):
    return pltpu.make_async_remote_copy(src_ref=src, dst_ref=dst, send_sem=send_sem, recv_sem=recv_sem,
                                        device_id=device, device_id_type=MESH)


class _Gather:
    def __init__(self, shards):
        self.args = [s for s, _ in shards]
        self.layers = [l for _, l in shards]
        self.out_shape = [jax.ShapeDtypeStruct((N_CHIPS,) + s.shape[1:], s.dtype) for s in self.args]

    def _own(self, loc, out, sems, a):
        x, y, c, k = _place()
        return _remote(loc[a].at[self.layers[a]], out[a].at[k], sems[0].at[a], sems[1].at[a], (x, y, 1 - c))

    def _ici(self, loc, out, sems, a, m, arrival):
        x, y, c, k = _place()
        dst = out[a].at[k ^ m, c] if arrival else out[a].at[k, c]
        return _remote(loc[a].at[self.layers[a], c], dst, sems[2].at[a, m - 1], sems[3].at[a, m - 1], _dev(k ^ m, c))

    def start(self, loc, out, sems):
        for a in range(len(self.args)):
            for m in range(1, N_CHIPS):
                self._ici(loc, out, sems, a, m, False).start()
            self._own(loc, out, sems, a).start()


class _SmallGather:
    PEERS = N_DEV - 1

    def __init__(self, small):
        self.args = [small]
        self.out_shape = [jax.ShapeDtypeStruct((N_DEV,) + small.shape, small.dtype)]

    def _copy(self, sm, land, sems, m, arrival):
        x, y, c, k = _place()
        me = 2 * k + c
        peer = me ^ m
        return _remote(sm[0], land[0].at[peer if arrival else me], sems[0].at[0, m - 1], sems[1].at[0, m - 1],
                       (peer // 4, (peer // 2) % 2, peer % 2))

    def start(self, sm, land, sems):
        for m in range(1, N_DEV):
            self._copy(sm, land, sems, m, False).start()

    def finish(self, sm, land, sems):
        for m in range(1, N_DEV):
            self._copy(sm, land, sems, m, True).wait_recv()
        for m in range(1, N_DEV):
            self._copy(sm, land, sems, m, False).wait_send()


class _ChipExchange:
    PEERS = N_CHIPS - 1

    def __init__(self, parts):
        self.args = list(parts)
        self.out_shape = [jax.ShapeDtypeStruct((N_CHIPS - 1,) + p.shape[1:], p.dtype) for p in parts]

    def _copy(self, p, land, sems, a, m):
        x, y, c, k = _place()
        return _remote(p[a].at[k ^ m], land[a].at[m - 1], sems[0].at[a, m - 1], sems[1].at[a, m - 1], _dev(k ^ m, c))

    def start(self, p, land, sems):
        for a in range(len(self.args)):
            for m in range(1, N_CHIPS):
                self._copy(p, land, sems, a, m).start()

    def finish(self, p, land, sems):
        for a in range(len(self.args)):
            for m in range(1, N_CHIPS):
                self._copy(p, land, sems, a, m).wait_recv()
        for a in range(len(self.args)):
            for m in range(1, N_CHIPS):
                self._copy(p, land, sems, a, m).wait_send()


def _gridded(body, after, *, name, grid, in_specs, out_specs, out_shape, args, scratch_shapes=()):
    ni, na = len(in_specs), len(after)

    def full(*refs):
        body(*refs[:ni], *refs[ni + na:])

    return list(pl.pallas_call(
        full, name=name, grid=grid, in_specs=[*in_specs, *[ANY] * na], out_specs=list(out_specs),
        out_shape=list(out_shape), scratch_shapes=list(scratch_shapes),
        compiler_params=pltpu.CompilerParams(dimension_semantics=("arbitrary",) * len(grid), vmem_limit_bytes=VMEM_LIMIT),
    )(*args, *after))


HBM = pl.BlockSpec(memory_space=pltpu.HBM)
SEM = pl.BlockSpec(memory_space=pltpu.SEMAPHORE)
DATAFLOW = pltpu.SideEffectType.DATAFLOW_SIDE_EFFECTING


class _SplitGather:
    PER_ARRAY = 8

    def __init__(self, shards):
        self.plan = _Gather(shards)
        self.n = len(shards)

    @staticmethod
    def _tables(sems_of):
        class Table:
            def __init__(self, pick):
                self.pick = pick

            @property
            def at(self):
                return self

            def __getitem__(self, idx):
                return self.pick(idx)

        return [Table(lambda a: sems_of[a][0]), Table(lambda a: sems_of[a][1]),
                Table(lambda am: sems_of[am[0]][2 + am[1]]), Table(lambda am: sems_of[am[0]][5 + am[1]])]

    def start(self, name, after=()):
        n, plan, per, na = self.n, self.plan, self.PER_ARRAY, len(after)

        def body(*refs):
            loc, land = refs[:n], refs[n:2 * n]
            sems_of = {a: refs[2 * n + na + per * a:2 * n + na + per * (a + 1)] for a in range(n)}
            plan.start(loc, land, self._tables(sems_of))
            refs[-1][...] = jnp.zeros_like(refs[-1])

        lands = [pltpu.with_memory_space_constraint(lax.empty(o.shape, o.dtype), pltpu.HBM) for o in plan.out_shape]
        locs = [pltpu.with_memory_space_constraint(a, pltpu.HBM) for a in plan.args]
        res = pl.pallas_call(
            body, name=name,
            out_shape=[*[DMA(())] * (per * n),
                       *[pltpu.HBM(o.shape, o.dtype) for o in plan.out_shape],
                       jax.ShapeDtypeStruct((8, 128), F32)],
            in_specs=[HBM] * (2 * n) + [pl.BlockSpec(memory_space=pl.ANY)] * na,
            out_specs=[SEM] * (per * n) + [HBM] * n + [pl.BlockSpec(memory_space=pltpu.VMEM)],
            input_output_aliases={n + i: per * n + i for i in range(n)},
            compiler_params=pltpu.CompilerParams(has_side_effects=DATAFLOW),
        )(*locs, *lands, *after)
        self.sems = {a: list(res[per * a:per * (a + 1)]) for a in range(n)}
        self.locs = locs
        self.lands = list(res[per * n:per * n + n])
        self.token = res[-1]

    def wait(self, idxs, after, name):
        plan, g, per = self.plan, len(idxs), self.PER_ARRAY

        def body(*refs):
            loc = {a: refs[j] for j, a in enumerate(idxs)}
            land = {a: refs[g + j] for j, a in enumerate(idxs)}
            sems = self._tables({a: refs[2 * g + per * j:2 * g + per * (j + 1)] for j, a in enumerate(idxs)})
            for a in idxs:
                for m in range(1, N_CHIPS):
                    plan._ici(loc, land, sems, a, m, True).wait_recv()
                    plan._ici(loc, land, sems, a, m, False).wait_send()
                plan._own(loc, land, sems, a).wait_recv()
                plan._own(loc, land, sems, a).wait_send()

        res = pl.pallas_call(
            body, name=name,
            out_shape=[pltpu.HBM(self.lands[a].shape, self.lands[a].dtype) for a in idxs],
            in_specs=[HBM] * (2 * g) + [SEM] * (per * g) + [pl.BlockSpec(memory_space=pl.ANY)], out_specs=[HBM] * g,
            input_output_aliases={g + j: j for j in range(g)},
            compiler_params=pltpu.CompilerParams(has_side_effects=DATAFLOW),
        )(*[self.locs[a] for a in idxs], *[self.lands[a] for a in idxs],
          *[s for a in idxs for s in self.sems[a]], after)
        return list(res)


class _SplitExchange:
    def __init__(self, parts, plan=None):
        self.plan = _ChipExchange(parts) if plan is None else plan
        self.n = len(parts)
        self.PER_ARRAY = 2 * self.plan.PEERS

    def _tables(self, sems_of):
        class Table:
            def __init__(self, pick):
                self.pick = pick

            @property
            def at(self):
                return self

            def __getitem__(self, am):
                return self.pick(am)

        peers = self.plan.PEERS
        return [Table(lambda am: sems_of[am[0]][am[1]]), Table(lambda am: sems_of[am[0]][peers + am[1]])]

    def start(self, name, after=()):
        n, plan, per, na = self.n, self.plan, self.PER_ARRAY, len(after)

        def body(*refs):
            p, land = refs[:n], refs[n:2 * n]
            sems_of = {a: refs[2 * n + na + per * a:2 * n + na + per * (a + 1)] for a in range(n)}
            plan.start(p, land, self._tables(sems_of))
            refs[-1][...] = jnp.zeros_like(refs[-1])

        lands = [pltpu.with_memory_space_constraint(lax.empty(o.shape, o.dtype), pltpu.HBM) for o in plan.out_shape]
        parts = [pltpu.with_memory_space_constraint(a, pltpu.HBM) for a in plan.args]
        res = pl.pallas_call(
            body, name=name,
            out_shape=[*[DMA(())] * (per * n),
                       *[pltpu.HBM(a.shape, a.dtype) for a in plan.args],
                       *[pltpu.HBM(o.shape, o.dtype) for o in plan.out_shape],
                       jax.ShapeDtypeStruct((8, 128), F32)],
            in_specs=[HBM] * (2 * n) + [pl.BlockSpec(memory_space=pl.ANY)] * na,
            out_specs=[SEM] * (per * n) + [HBM] * (2 * n) + [pl.BlockSpec(memory_space=pltpu.VMEM)],
            input_output_aliases={i: per * n + i for i in range(2 * n)},
            compiler_params=pltpu.CompilerParams(has_side_effects=DATAFLOW),
        )(*parts, *lands, *after)
        self.sems = list(res[:per * n])
        self.parts = list(res[per * n:per * n + n])
        self.lands = list(res[per * n + n:per * n + 2 * n])
        return res[-1]

    def wait(self, after, name):
        n, plan, per = self.n, self.plan, self.PER_ARRAY

        def body(*refs):
            p, land = refs[:n], refs[n:2 * n]
            sems_of = {a: refs[2 * n + per * a:2 * n + per * (a + 1)] for a in range(n)}
            plan.finish(p, land, self._tables(sems_of))

        res = pl.pallas_call(
            body, name=name,
            out_shape=[*[pltpu.HBM(a.shape, a.dtype) for a in self.parts], *[pltpu.HBM(a.shape, a.dtype) for a in self.lands]],
            in_specs=[HBM] * (2 * n) + [SEM] * (per * n) + [pl.BlockSpec(memory_space=pl.ANY)] * len(after),
            out_specs=[HBM] * (2 * n), input_output_aliases={i: i for i in range(2 * n)},
            compiler_params=pltpu.CompilerParams(has_side_effects=DATAFLOW),
        )(*self.parts, *self.lands, *self.sems, *after)
        return list(res[:n]), list(res[n:])


PASS_ON_BARRIER = 1
EXCHANGE_BARRIER = 2
SHARE_BARRIER = 3


def _sibling_barrier():
    x, y, c, _ = _place()
    barrier = pltpu.get_barrier_semaphore()
    pl.semaphore_signal(barrier, inc=1, device_id=(x, y, 1 - c), device_id_type=MESH)
    pl.semaphore_wait(barrier, 1)


def _pass_on(lands, name):
    n = len(lands)

    def body(*refs):
        out = refs[n:2 * n]
        send_sems, recv_sems = refs[2 * n:]
        x, y, c, k = _place()
        _sibling_barrier()
        cps = []
        for a in range(n):
            for m in range(1, N_CHIPS):
                got = out[a].at[k ^ m, c]
                cp = _remote(got, got, send_sems.at[a, m - 1], recv_sems.at[a, m - 1], (x, y, 1 - c))
                cp.start()
                cps.append(cp)
        for a in range(n):
            for m in range(1, N_CHIPS):
                theirs = out[a].at[k ^ m, 1 - c]
                _remote(theirs, theirs, send_sems.at[a, m - 1], recv_sems.at[a, m - 1], (x, y, 1 - c)).wait_recv()
        for cp in cps:
            cp.wait_send()

    return pl.pallas_call(
        body, name=name, out_shape=[jax.ShapeDtypeStruct(a.shape, a.dtype) for a in lands],
        in_specs=[ANY] * n, out_specs=[ANY] * n, input_output_aliases={a: a for a in range(n)},
        scratch_shapes=[DMA((n, 3)), DMA((n, 3))],
        compiler_params=pltpu.CompilerParams(has_side_effects=True, collective_id=PASS_ON_BARRIER),
    )(*lands)


def _sibling_exchange(g, name):
    def body(g_ref, land_ref, send_sem, recv_sem):
        x, y, c, _ = _place()
        _sibling_barrier()
        cp = _remote(g_ref.at[:, pl.ds(1 - c, 1)], land_ref, send_sem, recv_sem, (x, y, 1 - c))
        cp.start()
        cp.wait_recv()
        cp.wait_send()

    return pl.pallas_call(
        body, name=name, out_shape=jax.ShapeDtypeStruct((N_CHIPS, 1) + g.shape[2:], g.dtype), in_specs=[ANY],
        out_specs=ANY, scratch_shapes=[DMA, DMA],
        compiler_params=pltpu.CompilerParams(has_side_effects=True, collective_id=EXCHANGE_BARRIER),
    )(g)


def _sibling_share(halves, col_half, name, after=()):
    n, na = len(halves), len(after)

    def body(*refs):
        out = refs[n + na:2 * n + na]
        send_sems, recv_sems = refs[2 * n + na:]
        x, y, c, k = _place()

        def half(a, core):
            if not col_half[a]:
                return out[a].at[:, pl.ds(core, 1)]
            cols = out[a].shape[-1] // 2
            return out[a].at[:, :, pl.ds(pl.multiple_of(core * cols, cols), cols)]

        _sibling_barrier()
        cps = []
        for a in range(n):
            cp = _remote(half(a, c), half(a, c), send_sems.at[a], recv_sems.at[a], (x, y, 1 - c))
            cp.start()
            cps.append(cp)
        for a in range(n):
            _remote(half(a, 1 - c), half(a, 1 - c), send_sems.at[a], recv_sems.at[a], (x, y, 1 - c)).wait_recv()
        for cp in cps:
            cp.wait_send()

    out_shape = [jax.ShapeDtypeStruct(a.shape, a.dtype) for a in halves]
    return pl.pallas_call(
        body, name=name, out_shape=out_shape, in_specs=[ANY] * (n + na), out_specs=[ANY] * n,
        input_output_aliases={a: a for a in range(n)}, scratch_shapes=[DMA((n,)), DMA((n,))],
        compiler_params=pltpu.CompilerParams(has_side_effects=True, collective_id=SHARE_BARRIER),
    )(*halves, *after)


def _add_sibling(g, land, core):
    _, _, r, c = g.shape
    rb = _row_block(r, c * (3 * 2 * 2 + 2 * 4))

    def body(core_ref, g_ref, l_ref, o_ref):
        o_ref[...] = (g_ref[...].astype(F32) + l_ref[...].astype(F32)).astype(o_ref.dtype)

    return pl.pallas_call(
        body, name="rs_add_sibling", out_shape=jax.ShapeDtypeStruct((N_CHIPS, r, c), g.dtype),
        grid_spec=pltpu.PrefetchScalarGridSpec(
            num_scalar_prefetch=1, grid=(N_CHIPS, r // rb),
            in_specs=[pl.BlockSpec((None, None, rb, c), lambda j, i, core_ref: (j, core_ref[0], i, 0)),
                      pl.BlockSpec((None, None, rb, c), lambda j, i, core_ref: (j, 0, i, 0))],
            out_specs=pl.BlockSpec((None, rb, c), lambda j, i, core_ref: (j, i, 0))),
        compiler_params=pltpu.CompilerParams(dimension_semantics=("parallel", "parallel"), vmem_limit_bytes=VMEM_LIMIT),
    )(core, g, land)


def _add_chips(part, land, where, layer=0, n_layers=1, into=None, col_half=False):
    _, r, c = part.shape
    rb = _row_block(r, c * (4 * 2 * 2 + 4 * 2 + 2 * 4))

    def body(where_ref, p_ref, l_ref, *rest):
        acc = p_ref[...].astype(F32)
        for m in range(N_CHIPS - 1):
            acc = acc + l_ref[m].astype(F32)
        rest[-1][...] = acc

    in_specs = [pl.BlockSpec((None, rb, c), lambda i, where_ref: (where_ref[0], i, 0)),
                pl.BlockSpec((N_CHIPS - 1, rb, c), lambda i, where_ref: (0, i, 0))]
    args = [where, part, land]
    if into is not None:
        in_specs.append(ANY)
        args.append(into)
    if col_half:
        out_shape = jax.ShapeDtypeStruct((n_layers, r, 2 * c), F32)
        out_spec = pl.BlockSpec((None, rb, c), lambda i, where_ref: (layer, i, where_ref[1]))
    else:
        out_shape = jax.ShapeDtypeStruct((n_layers, 2, r, c), F32)
        out_spec = pl.BlockSpec((None, None, rb, c), lambda i, where_ref: (layer, where_ref[1], i, 0))
    return pl.pallas_call(
        body, name="rs_add_chips", out_shape=out_shape,
        grid_spec=pltpu.PrefetchScalarGridSpec(
            num_scalar_prefetch=1, grid=(r // rb,), in_specs=in_specs, out_specs=out_spec),
        input_output_aliases={} if into is None else {3: 0},
        compiler_params=pltpu.CompilerParams(dimension_semantics=("parallel",), vmem_limit_bytes=VMEM_LIMIT),
    )(*args)


def _sum_small(smg, own, me):
    def body(me_ref, s_ref, own_ref, o_ref):
        o_ref[...] = jnp.zeros_like(o_ref)
        for j in range(N_DEV):
            @pl.when(me_ref[0] == j)
            def _():
                o_ref[...] += own_ref[...]

            @pl.when(me_ref[0] != j)
            def _():
                o_ref[...] += s_ref[j]

    return pl.pallas_call(
        body, name="rs_sum_small", out_shape=jax.ShapeDtypeStruct(smg.shape[1:], F32),
        grid_spec=pltpu.PrefetchScalarGridSpec(
            num_scalar_prefetch=1, grid=(1,),
            in_specs=[pl.BlockSpec(smg.shape, lambda i, me_ref: (0, 0, 0)), pl.BlockSpec(own.shape, lambda i, me_ref: (0, 0))],
            out_specs=pl.BlockSpec(own.shape, lambda i, me_ref: (0, 0))),
    )(me, smg, own)


def _pool_windows(ext_ref, g, gw, tm, first_row):
    w = POOL_WINDOWS[g]
    slab = ext_ref[:, g * gw:(g + 1) * gw]
    p, k = slab, 1
    while k < w:
        p = p + pltpu.roll(p, k, 0)
        k *= 2
    t = first_row + lax.broadcasted_iota(jnp.int32, (tm, 1), 0)
    cnt = jnp.minimum(t + 1, w).astype(F32)
    return p[POOL_HALO:] / cnt - slab[POOL_HALO:]


def _fwd_pool(x, small, scale, poolw, after=()):
    t, d = x.shape
    tm = _token_tile(t)
    gw = d // len(POOL_WINDOWS)

    def body(x_ref, sm_ref, sc_ref, w_ref, h_ref, ext_ref, mix_ref):
        i = pl.program_id(0)

        @pl.when(i == 0)
        def _():
            ext_ref[0:POOL_HALO, :] = jnp.zeros((POOL_HALO, d), F32)

        @pl.when(i > 0)
        def _():
            ext_ref[0:POOL_HALO, :] = ext_ref[tm:tm + POOL_HALO, :]

        xv = x_ref[...]
        xh, _ = _rms(xv)
        ext_ref[POOL_HALO:, :] = xh * sm_ref[0:1, :]
        for g in range(len(POOL_WINDOWS)):
            pooled = _pool_windows(ext_ref, g, gw, tm, i * tm)
            cols = slice(g * gw, (g + 1) * gw)
            mix_ref[:, cols] = _dot(pooled.astype(BF16), w_ref[g]) * sc_ref[:, cols]
        mh, _ = _rms(mix_ref[...])
        h_ref[...] = xv + mh * sm_ref[1:2, :]

    return _gridded(
        body, after, name="fwd_pool", grid=(t // tm,), out_shape=[jax.ShapeDtypeStruct((t, d), F32)],
        in_specs=[pl.BlockSpec((tm, d), lambda i: (i, 0)), _resident(small.shape, lambda i: (0, 0)),
                  _resident(scale.shape, lambda i: (0, 0)), _resident(poolw.shape, lambda i: (0, 0, 0))],
        out_specs=[pl.BlockSpec((tm, d), lambda i: (i, 0))],
        scratch_shapes=[pltpu.VMEM((POOL_HALO + tm, d), F32), pltpu.VMEM((tm, d), F32)],
        args=[x, small, scale, poolw])[0]


def _fwd_ffn(h, small, wgu, wd, layer, after=(), target=None):
    t, d = h.shape
    tm = _token_tile(t)
    steps = t // tm
    fc = wgu.shape[-1]
    f = 2 * fc
    g_in, g_out = 4 * layer + 2, 4 * layer + 3
    with_loss = target is not None

    def body(h_ref, *refs):
        if with_loss:
            t_ref, sm_ref, wgu_ref, wd_ref, o_ref, gu_ref, ff_ref, n_ref, l_ref, acc_ref = refs
        else:
            sm_ref, wgu_ref, wd_ref, o_ref, gu_ref, ff_ref, n_ref = refs
        hv = h_ref[...]
        hh, _ = _rms(hv)
        n = (hh * sm_ref[g_in:g_in + 1, :]).astype(BF16)
        n_ref[...] = n
        ff = None
        for j in range(2):
            gate = _dot(n, wgu_ref[j])
            up = _dot(n, wgu_ref[2 + j])
            gu_ref[:, j * fc:(j + 1) * fc] = gate.astype(BF16)
            gu_ref[:, f + j * fc:f + (j + 1) * fc] = up.astype(BF16)
            act = (gate * jax.nn.sigmoid(gate) * up).astype(BF16)
            part = _dot(act, wd_ref[j * fc:(j + 1) * fc, :])
            ff = part if ff is None else ff + part
        ff_ref[...] = ff
        fh, _ = _rms(ff)
        out = hv + fh * sm_ref[g_out:g_out + 1, :]
        if not with_loss:
            o_ref[...] = out
            return
        i = pl.program_id(0)
        e = out - t_ref[...]
        o_ref[...] = e * (1.0 / d)

        @pl.when(i == 0)
        def _():
            acc_ref[...] = jnp.zeros_like(acc_ref)

        acc_ref[...] += _colsum(e * e)

        @pl.when(i == steps - 1)
        def _():
            l_ref[...] = jnp.full(l_ref.shape, 0.5 / d, F32) * jnp.sum(acc_ref[...])

    row = lambda i: (i, 0)
    out_shape = [jax.ShapeDtypeStruct((t, d), F32), jax.ShapeDtypeStruct((t, 2 * f), BF16),
                 jax.ShapeDtypeStruct((t, d), F32), jax.ShapeDtypeStruct((t, d), BF16)]
    out_specs = [pl.BlockSpec((tm, d), row), pl.BlockSpec((tm, 2 * f), row), pl.BlockSpec((tm, d), row),
                 pl.BlockSpec((tm, d), row)]
    weight_specs = [_resident(small.shape, lambda i: (0, 0)), _resident(wgu.shape, lambda i: (0, 0, 0)),
                    _resident(wd.shape, lambda i: (0, 0))]
    if with_loss:
        return _gridded(
            body, after, name=f"fwd_ffn{layer}_loss", grid=(steps,),
            out_shape=out_shape + [jax.ShapeDtypeStruct((8, 128), F32)],
            in_specs=[pl.BlockSpec((tm, d), row), pl.BlockSpec((tm, d), row)] + weight_specs,
            out_specs=out_specs + [pl.BlockSpec((8, 128), lambda i: (0, 0))],
            scratch_shapes=[pltpu.VMEM((1, d), F32)], args=[h, target, small, wgu, wd])
    return _gridded(
        body, after, name=f"fwd_ffn{layer}", grid=(steps,), out_shape=out_shape,
        in_specs=[pl.BlockSpec((tm, d), row)] + weight_specs, out_specs=out_specs, args=[h, small, wgu, wd])


def _fwd_ffn_up(h, small, wgu, layer):
    t, d = h.shape
    tm = _token_tile(t)
    fc = wgu.shape[-1]
    f = 2 * fc
    g_in = 4 * layer + 2

    def body(h_ref, sm_ref, wgu_ref, gu_ref, n_ref, act_ref):
        hh, _ = _rms(h_ref[...])
        n = (hh * sm_ref[g_in:g_in + 1, :]).astype(BF16)
        n_ref[...] = n
        for j in range(2):
            gate = _dot(n, wgu_ref[j])
            up = _dot(n, wgu_ref[2 + j])
            gu_ref[:, j * fc:(j + 1) * fc] = gate.astype(BF16)
            gu_ref[:, f + j * fc:f + (j + 1) * fc] = up.astype(BF16)
            act_ref[:, j * fc:(j + 1) * fc] = (gate * jax.nn.sigmoid(gate) * up).astype(BF16)

    row = lambda i: (i, 0)
    return _gridded(
        body, (), name=f"fwd_ffn{layer}_up", grid=(t // tm,),
        out_shape=[jax.ShapeDtypeStruct((t, 2 * f), BF16), jax.ShapeDtypeStruct((t, d), BF16),
                   jax.ShapeDtypeStruct((t, f), BF16)],
        in_specs=[pl.BlockSpec((tm, d), row), _resident(small.shape, lambda i: (0, 0)),
                  _resident(wgu.shape, lambda i: (0, 0, 0))],
        out_specs=[pl.BlockSpec((tm, 2 * f), row), pl.BlockSpec((tm, d), row), pl.BlockSpec((tm, f), row)],
        args=[h, small, wgu])


def _fwd_ffn_down(h, act, small, wd, layer):
    t, d = h.shape
    tm = _token_tile(t)
    f = wd.shape[0]
    g_out = 4 * layer + 3

    def body(h_ref, act_ref, sm_ref, wd_ref, o_ref, ff_ref):
        ff = _dot(act_ref[...], wd_ref[...])
        ff_ref[...] = ff
        fh, _ = _rms(ff)
        o_ref[...] = h_ref[...] + fh * sm_ref[g_out:g_out + 1, :]

    row = lambda i: (i, 0)
    return _gridded(
        body, (), name=f"fwd_ffn{layer}_down", grid=(t // tm,),
        out_shape=[jax.ShapeDtypeStruct((t, d), F32), jax.ShapeDtypeStruct((t, d), F32)],
        in_specs=[pl.BlockSpec((tm, d), row), pl.BlockSpec((tm, f), row), _resident(small.shape, lambda i: (0, 0)),
                  _resident(wd.shape, lambda i: (0, 0))],
        out_specs=[pl.BlockSpec((tm, d), row), pl.BlockSpec((tm, d), row)], args=[h, act, small, wd])


def _fwd_conv(h, small, win, wout, after=()):
    t, d = h.shape
    tm = _token_tile(t)
    pc = win.shape[-1]

    def body(h_ref, sm_ref, win_ref, wout_ref, o_ref, proj_ref, y_ref, n_ref, pj_ref, uext_ref):
        i = pl.program_id(0)

        @pl.when(i == 0)
        def _():
            uext_ref[0:CONV_HALO, :] = jnp.zeros((CONV_HALO, d), F32)

        @pl.when(i > 0)
        def _():
            uext_ref[0:CONV_HALO, :] = uext_ref[tm:tm + CONV_HALO, :]

        hv = h_ref[...]
        hh, _ = _rms(hv)
        n = (hh * sm_ref[4:5, :]).astype(BF16)
        n_ref[...] = n
        for k in range(N_CHIPS):
            pj_ref[:, k * pc:(k + 1) * pc] = _dot(n, win_ref[k])
        proj_ref[...] = pj_ref[...].astype(BF16)
        uext_ref[CONV_HALO:, :] = pj_ref[:, d:2 * d] * pj_ref[:, 2 * d:]
        taps = [sm_ref[8 + j:9 + j, :] for j in range(3)]
        full = uext_ref[...]
        conv = (full[CONV_HALO:] * taps[2] + pltpu.roll(full, 1, 0)[CONV_HALO:] * taps[1]
                + pltpu.roll(full, 2, 0)[CONV_HALO:] * taps[0])
        y = _dot((pj_ref[:, 0:d] * conv).astype(BF16), wout_ref[...])
        y_ref[...] = y
        yh, _ = _rms(y)
        o_ref[...] = hv + yh * sm_ref[5:6, :]

    row = lambda i: (i, 0)
    return _gridded(
        body, after, name="fwd_conv", grid=(t // tm,),
        out_shape=[jax.ShapeDtypeStruct((t, d), F32), jax.ShapeDtypeStruct((t, 3 * d), BF16),
                   jax.ShapeDtypeStruct((t, d), F32), jax.ShapeDtypeStruct((t, d), BF16)],
        in_specs=[pl.BlockSpec((tm, d), row), _resident(small.shape, lambda i: (0, 0)),
                  _resident(win.shape, lambda i: (0, 0, 0)), _resident(wout.shape, lambda i: (0, 0))],
        out_specs=[pl.BlockSpec((tm, d), row), pl.BlockSpec((tm, 3 * d), row), pl.BlockSpec((tm, d), row),
                   pl.BlockSpec((tm, d), row)],
        scratch_shapes=[pltpu.VMEM((tm, 3 * d), F32), pltpu.VMEM((CONV_HALO + tm, d), F32)],
        args=[h, small, win, wout])


def _bwd_ffn(dh, h, ff, gu, small, wgu, wd, layer, after=()):
    t, d = h.shape
    tm = _token_tile(t, FFN_BWD_TOKEN_TILE)
    fc = wgu.shape[-1]
    f = 2 * fc
    g_in, g_out = 4 * layer + 2, 4 * layer + 3

    def body(dh_ref, h_ref, ff_ref, gu_ref, sm_ref, wgu_ref, wd_ref, o_ref, dgu_ref, dff_ref, act_ref, sg_ref):
        i = pl.program_id(0)

        @pl.when(i == 0)
        def _():
            sg_ref[...] = jnp.zeros_like(sg_ref)

        dy = dh_ref[...]
        fh, r3 = _rms(ff_ref[...])
        sg_ref[1:2, :] += _colsum(dy * fh)
        dff = _rms_bwd(dy, fh, r3, sm_ref[g_out:g_out + 1, :]).astype(BF16)
        dff_ref[...] = dff
        for j in range(2):
            dact = _dot_nt(dff, wd_ref[j * fc:(j + 1) * fc, :])
            gate = gu_ref[:, j * fc:(j + 1) * fc].astype(F32)
            up = gu_ref[:, f + j * fc:f + (j + 1) * fc].astype(F32)
            sig = jax.nn.sigmoid(gate)
            silu = gate * sig
            act_ref[:, j * fc:(j + 1) * fc] = (silu * up).astype(BF16)
            dgu_ref[:, j * fc:(j + 1) * fc] = (dact * up * (sig * (1.0 + gate * (1.0 - sig)))).astype(BF16)
            dgu_ref[:, f + j * fc:f + (j + 1) * fc] = (dact * silu).astype(BF16)
        dn = None
        for k in range(N_CHIPS):
            part = _dot_nt(dgu_ref[:, k * fc:(k + 1) * fc], wgu_ref[k])
            dn = part if dn is None else dn + part
        hh, r2 = _rms(h_ref[...])
        sg_ref[0:1, :] += _colsum(dn * hh)
        o_ref[...] = dy + _rms_bwd(dn, hh, r2, sm_ref[g_in:g_in + 1, :])

    row = lambda i: (i, 0)
    return _gridded(
        body, after, name=f"bwd_ffn{layer}", grid=(t // tm,),
        out_shape=[jax.ShapeDtypeStruct((t, d), F32), jax.ShapeDtypeStruct((t, 2 * f), BF16),
                   jax.ShapeDtypeStruct((t, d), BF16), jax.ShapeDtypeStruct((t, f), BF16),
                   jax.ShapeDtypeStruct((8, d), F32)],
        in_specs=[pl.BlockSpec((tm, d), row), pl.BlockSpec((tm, d), row), pl.BlockSpec((tm, d), row),
                  pl.BlockSpec((tm, 2 * f), row), _resident(small.shape, lambda i: (0, 0)),
                  _resident(wgu.shape, lambda i: (0, 0, 0)), _resident(wd.shape, lambda i: (0, 0))],
        out_specs=[pl.BlockSpec((tm, d), row), pl.BlockSpec((tm, 2 * f), row), pl.BlockSpec((tm, d), row),
                   pl.BlockSpec((tm, f), row), pl.BlockSpec((8, d), lambda i: (0, 0))],
        args=[dh, h, ff, gu, small, wgu, wd])


def _bwd_conv(dh, h, y, proj, small, win, wout, after=()):
    t, d = h.shape
    tm = _token_tile(t)
    steps = t // tm
    pc = win.shape[-1]
    halo_blocks = tm // 16

    def body(dh_ref, h_ref, y_ref, proj_ref, halo_ref, sm_ref, win_ref, wout_ref,
             o_ref, dproj_ref, dy_ref, bc_ref, sg_ref, uext_ref, dcext_ref, carry_ref):
        i = pl.program_id(0)
        tile = steps - 1 - i

        @pl.when(i == 0)
        def _():
            sg_ref[...] = jnp.zeros_like(sg_ref)
            carry_ref[...] = jnp.zeros_like(carry_ref)

        dy = dh_ref[...]
        yh, r1 = _rms(y_ref[...])
        sg_ref[1:2, :] += _colsum(dy * yh)
        dyv = _rms_bwd(dy, yh, r1, sm_ref[5:6, :]).astype(BF16)
        dy_ref[...] = dyv
        dbc = _dot_nt(dyv, wout_ref[...])
        b = proj_ref[:, 0:d].astype(F32)
        cg = proj_ref[:, d:2 * d].astype(F32)
        v = proj_ref[:, 2 * d:].astype(F32)
        halo = halo_ref[...].astype(F32)[16 - CONV_HALO:]
        uh = halo[:, d:2 * d] * halo[:, 2 * d:]
        uext_ref[0:CONV_HALO, :] = jnp.where(tile > 0, uh, jnp.zeros_like(uh))
        uext_ref[CONV_HALO:, :] = cg * v
        taps = [sm_ref[8 + j:9 + j, :] for j in range(3)]
        full = uext_ref[...]
        u0 = full[CONV_HALO:]
        u1 = pltpu.roll(full, 1, 0)[CONV_HALO:]
        u2 = pltpu.roll(full, 2, 0)[CONV_HALO:]
        conv = u0 * taps[2] + u1 * taps[1] + u2 * taps[0]
        bc_ref[...] = (b * conv).astype(BF16)
        dconv = dbc * b
        sg_ref[4:5, :] += _colsum(dconv * u0)
        sg_ref[3:4, :] += _colsum(dconv * u1)
        sg_ref[2:3, :] += _colsum(dconv * u2)
        dcext_ref[0:tm, :] = dconv
        dcext_ref[tm:, :] = carry_ref[...]
        carry_ref[...] = dconv[0:CONV_HALO]
        dfull = dcext_ref[...]
        n8 = tm + CONV_HALO
        du = (dfull[0:tm] * taps[2] + pltpu.roll(dfull, n8 - 1, 0)[0:tm] * taps[1]
              + pltpu.roll(dfull, n8 - 2, 0)[0:tm] * taps[0])
        dproj_ref[:, 0:d] = (dbc * conv).astype(BF16)
        dproj_ref[:, d:2 * d] = (du * v).astype(BF16)
        dproj_ref[:, 2 * d:] = (du * cg).astype(BF16)
        dn = None
        for k in range(N_CHIPS):
            part = _dot_nt(dproj_ref[:, k * pc:(k + 1) * pc], win_ref[k])
            dn = part if dn is None else dn + part
        hh, r0 = _rms(h_ref[...])
        sg_ref[0:1, :] += _colsum(dn * hh)
        o_ref[...] = dy + _rms_bwd(dn, hh, r0, sm_ref[4:5, :])

    rev = lambda i: (steps - 1 - i, 0)
    before = lambda i: (jnp.maximum((steps - 1 - i) * halo_blocks - 1, 0), 0)
    return _gridded(
        body, after, name="bwd_conv", grid=(steps,),
        out_shape=[jax.ShapeDtypeStruct((t, d), F32), jax.ShapeDtypeStruct((t, 3 * d), BF16),
                   jax.ShapeDtypeStruct((t, d), BF16), jax.ShapeDtypeStruct((t, d), BF16),
                   jax.ShapeDtypeStruct((8, d), F32)],
        in_specs=[pl.BlockSpec((tm, d), rev), pl.BlockSpec((tm, d), rev), pl.BlockSpec((tm, d), rev),
                  pl.BlockSpec((tm, 3 * d), rev), pl.BlockSpec((16, 3 * d), before),
                  _resident(small.shape, lambda i: (0, 0)), _resident(win.shape, lambda i: (0, 0, 0)),
                  _resident(wout.shape, lambda i: (0, 0))],
        out_specs=[pl.BlockSpec((tm, d), rev), pl.BlockSpec((tm, 3 * d), rev), pl.BlockSpec((tm, d), rev),
                   pl.BlockSpec((tm, d), rev), pl.BlockSpec((8, d), lambda i: (0, 0))],
        scratch_shapes=[pltpu.VMEM((CONV_HALO + tm, d), F32), pltpu.VMEM((tm + CONV_HALO, d), F32),
                        pltpu.VMEM((CONV_HALO, d), F32)],
        args=[dh, h, y, proj, proj, small, win, wout])


def _bwd_pool(dh, x, small, scale, poolw, after=()):
    t, d = x.shape
    tm = _token_tile(t)
    steps = t // tm
    ng = len(POOL_WINDOWS)
    gw = d // ng
    halo_blocks = tm // POOL_HALO

    def body(dh_ref, x_ref, halo_ref, sm_ref, sc_ref, w_ref, o_ref, dw_ref, sg_ref,
             ext_ref, mix_ref, mm_ref, pb_ref, qext_ref, dhn_ref, carry_ref):
        i = pl.program_id(0)
        tile = steps - 1 - i

        @pl.when(i == 0)
        def _():
            sg_ref[...] = jnp.zeros_like(sg_ref)
            dw_ref[...] = jnp.zeros_like(dw_ref)
            carry_ref[...] = jnp.zeros_like(carry_ref)

        g0 = sm_ref[0:1, :]
        xv = x_ref[...]
        xh, r0 = _rms(xv)
        hx, _ = _rms(halo_ref[...])
        ext_ref[0:POOL_HALO, :] = jnp.where(tile > 0, hx * g0, jnp.zeros_like(hx))
        ext_ref[POOL_HALO:, :] = xh * g0
        for g in range(ng):
            pooled = _pool_windows(ext_ref, g, gw, tm, tile * tm)
            cols = slice(g * gw, (g + 1) * gw)
            pb = pooled.astype(BF16)
            pb_ref[:, cols] = pb
            mm = _dot(pb, w_ref[g])
            mm_ref[:, cols] = mm
            mix_ref[:, cols] = mm * sc_ref[:, cols]
        dy = dh_ref[...]
        mh, r1 = _rms(mix_ref[...])
        sg_ref[1:2, :] += _colsum(dy * mh)
        dmix = _rms_bwd(dy, mh, r1, sm_ref[1:2, :])
        sg_ref[2:3, :] += _colsum(dmix * mm_ref[...])
        mix_ref[...] = dmix * sc_ref[...]
        n16 = tm + POOL_HALO
        for g in range(ng):
            w = POOL_WINDOWS[g]
            cols = slice(g * gw, (g + 1) * gw)
            dmm = mix_ref[:, cols].astype(BF16)
            dpooled = _dot_nt(dmm, w_ref[g])
            dw_ref[g] += _dot_tn(pb_ref[:, cols], dmm)
            trow = tile * tm + lax.broadcasted_iota(jnp.int32, (tm, 1), 0)
            q = dpooled / jnp.minimum(trow + 1, w).astype(F32)
            qext_ref[0:tm, cols] = q
            qext_ref[tm:, cols] = carry_ref[:, cols]
            carry_ref[:, cols] = q[0:POOL_HALO]
            p, k = qext_ref[:, cols], 1
            while k < w:
                p = p + pltpu.roll(p, n16 - k, 0)
                k *= 2
            dhn_ref[:, cols] = p[0:tm] - dpooled
        dhn = dhn_ref[...]
        sg_ref[0:1, :] += _colsum(dhn * xh)
        o_ref[...] = dy + _rms_bwd(dhn, xh, r0, g0)

    rev = lambda i: (steps - 1 - i, 0)
    before = lambda i: (jnp.maximum((steps - 1 - i) * halo_blocks - 1, 0), 0)
    return _gridded(
        body, after, name="bwd_pool", grid=(steps,),
        out_shape=[jax.ShapeDtypeStruct((t, d), F32), jax.ShapeDtypeStruct((ng, gw, gw), F32),
                   jax.ShapeDtypeStruct((8, d), F32)],
        in_specs=[pl.BlockSpec((tm, d), rev), pl.BlockSpec((tm, d), rev), pl.BlockSpec((POOL_HALO, d), before),
                  _resident(small.shape, lambda i: (0, 0)), _resident(scale.shape, lambda i: (0, 0)),
                  _resident(poolw.shape, lambda i: (0, 0, 0))],
        out_specs=[pl.BlockSpec((tm, d), rev), pl.BlockSpec((ng, gw, gw), lambda i: (0, 0, 0)),
                   pl.BlockSpec((8, d), lambda i: (0, 0))],
        scratch_shapes=[pltpu.VMEM((POOL_HALO + tm, d), F32), pltpu.VMEM((tm, d), F32), pltpu.VMEM((tm, d), F32),
                        pltpu.VMEM((tm, d), BF16), pltpu.VMEM((tm + POOL_HALO, d), F32), pltpu.VMEM((tm, d), F32),
                        pltpu.VMEM((POOL_HALO, d), F32)],
        args=[dh, x, x, small, scale, poolw])


def _weight_grad(a, b, bm, bn, half_on, name):
    t, m = a.shape
    _, n = b.shape
    if half_on == "a":
        a_cols, b_cols = 2 * bm, bn
    else:
        a_cols, b_cols = bm, 2 * bn
    steps = max(m // a_cols, n // b_cols)

    def spec(cols, total):
        if cols == total:
            return _resident((t, cols), lambda p, j: (0, 0))
        return pl.BlockSpec((t, cols), lambda p, j: (0, j))

    def tile(a_ref, b_ref, half):
        if half_on == "a":
            return _dot_tn(a_ref[:, half * bm:(half + 1) * bm], b_ref[...])
        return _dot_tn(a_ref[...], b_ref[:, half * bn:(half + 1) * bn])

    def body(a_ref, b_ref, parts_ref, land_ref, acc_ref, stage_ref, got_ref, send_sems, recv_sems, got_sem):
        p, j = pl.program_id(0), pl.program_id(1)
        x, y, c, _ = _place()
        half = jnp.where(p == 0, 1 - c, c)

        def send(jj):
            return _remote(stage_ref.at[jj % 2], land_ref.at[jj], send_sems.at[jj], recv_sems.at[jj], (x, y, 1 - c))

        def fetch():
            return pltpu.make_async_copy(land_ref.at[j], got_ref, got_sem)

        @pl.when(p == 1)
        def _():
            @pl.when(j == 0)
            def _():
                for jj in range(max(steps - 2, 0), steps):
                    send(jj).wait_send()

            send(j).wait_recv()
            fetch().start()

        for hv in range(2):
            @pl.when(half == hv)
            def _():
                acc_ref[...] = tile(a_ref, b_ref, hv)

        @pl.when(p == 0)
        def _():
            @pl.when(j >= 2)
            def _():
                send(j - 2).wait_send()

            stage_ref[j % 2] = acc_ref[...].astype(BF16)
            send(j).start()

        @pl.when(p == 1)
        def _():
            fetch().wait()
            parts_ref[...] = (acc_ref[...] + got_ref[...].astype(F32)).astype(BF16)

    return _gridded(
        body, (), name=name, grid=(2, steps),
        out_shape=[jax.ShapeDtypeStruct((steps, bm, bn), BF16), jax.ShapeDtypeStruct((steps, bm, bn), BF16)],
        in_specs=[spec(a_cols, m), spec(b_cols, n)],
        out_specs=[pl.BlockSpec((None, bm, bn), lambda p, j: (p * j, 0, 0)), ANY],
        scratch_shapes=[pltpu.VMEM((bm, bn), F32), pltpu.VMEM((2, bm, bn), BF16), pltpu.VMEM((bm, bn), BF16),
                        DMA((steps,)), DMA((steps,)), DMA],
        args=[a, b])[0]


def _cast_layer(w, layer, name, after=()):
    _, r, c = w.shape
    rb = _row_block(r, c * (4 + 2) * 2)

    def body(w_ref, *rest):
        rest[-1][...] = w_ref[...].astype(BF16)

    return pl.pallas_call(
        body, name=name, grid=(r // rb,), out_shape=jax.ShapeDtypeStruct((r, c), BF16),
        in_specs=[pl.BlockSpec((None, rb, c), lambda i: (layer, i, 0))] + [ANY] * len(after),
        out_specs=pl.BlockSpec((rb, c), lambda i: (i, 0)),
        compiler_params=pltpu.CompilerParams(dimension_semantics=("parallel",), vmem_limit_bytes=VMEM_LIMIT),
    )(w, *after)


def _adamw_math(w, g, m, v):
    bc1 = 1.0 - ADAM_B1 ** ADAM_STEP
    bc2 = 1.0 - ADAM_B2 ** ADAM_STEP
    nm = ADAM_B1 * m + (1.0 - ADAM_B1) * g
    nv = ADAM_B2 * v + (1.0 - ADAM_B2) * (g * g)
    return -ADAM_LR * ((nm / bc1) / (jnp.sqrt(nv / bc2) + ADAM_EPS) + ADAM_WD * w), nm, nv


def _adamw_small(small_sum, where, gains, taps, scale):
    dq = gains[0].shape[-1]
    d = small_sum.shape[-1]

    def body(where_ref, mine_ref, all_ref, gw, gm, gv, tw, tm_, tv, sw, sm, sv,
             gg, gd, gnm, gnv, tg, td, tnm, tnv, sg, sd, snm, snv):
        for layer in range(gw.shape[0]):
            g = mine_ref[4 * layer:4 * layer + 4, :]
            gg[layer] = g
            gd[layer], gnm[layer], gnv[layer] = _adamw_math(gw[layer], g, gm[layer], gv[layer])
        g = mine_ref[8:8 + tw.shape[1], :]
        tg[0] = g
        td[0], tnm[0], tnv[0] = _adamw_math(tw[0], g, tm_[0], tv[0])
        g = all_ref[11:12, :]
        sg[...] = g
        sd[...], snm[...], snv[...] = _adamw_math(sw[...], g, sm[...], sv[...])

    full = lambda a: pl.BlockSpec(a.shape, lambda i, where_ref: (0,) * a.ndim)
    params = [*gains, *taps, *scale]
    outs = [gains[0]] * 4 + [taps[0]] * 4 + [scale[0]] * 4
    res = pl.pallas_call(
        body, name="adamw_small", out_shape=[jax.ShapeDtypeStruct(a.shape, F32) for a in outs],
        grid_spec=pltpu.PrefetchScalarGridSpec(
            num_scalar_prefetch=1, grid=(1,),
            in_specs=[pl.BlockSpec((16, dq), lambda i, where_ref: (0, where_ref[0])), pl.BlockSpec((16, d), lambda i, where_ref: (0, 0)),
                      *[full(a) for a in params]],
            out_specs=[full(a) for a in outs]),
    )(where, small_sum, small_sum, *params)
    return tuple(res[0:4]), tuple(res[4:8]), tuple(res[8:12])


def _adamw(w, g, m, v, name):
    r, c = w.shape
    rb = _row_block(r, c * (8 * 4 * 2 + 4 * 4))

    def body(w_ref, g_ref, m_ref, v_ref, d_ref, nm_ref, nv_ref, go_ref):
        gv = g_ref[...]
        go_ref[...] = gv
        d_ref[...], nm_ref[...], nv_ref[...] = _adamw_math(w_ref[...], gv, m_ref[...], v_ref[...])

    spec = pl.BlockSpec((rb, c), lambda i: (i, 0))
    return pl.pallas_call(
        body, name=name, grid=(r // rb,), out_shape=[jax.ShapeDtypeStruct((r, c), F32)] * 4,
        in_specs=[spec] * 4, out_specs=[spec] * 4,
        compiler_params=pltpu.CompilerParams(dimension_semantics=("parallel",), vmem_limit_bytes=VMEM_LIMIT),
    )(w, g, m, v)


def kernel(x, norm_gains, pool_w, pool_scale, conv_in_w, conv_w, conv_out_w, ffn_gate_up_w, ffn_down_w, loss_target, m_norm_gains, m_pool_w, m_pool_scale, m_conv_in_w, m_conv_w, m_conv_out_w, m_ffn_gate_up_w, m_ffn_down_w, v_norm_gains, v_pool_w, v_pool_scale, v_conv_in_w, v_conv_w, v_conv_out_w, v_ffn_gate_up_w, v_ffn_down_w):
    _, t, d = x.shape
    dq = d // N_CHIPS
    ng = len(POOL_WINDOWS)
    gw = d // ng
    fq = ffn_down_w.shape[1]
    f = N_CHIPS * fq
    fc = f // 2
    core = lax.axis_index("c")
    chip = 2 * lax.axis_index("x") + lax.axis_index("y")
    core_arr = jnp.reshape(core, (1,)).astype(jnp.int32)
    where_arr = jnp.stack([chip, core]).astype(jnp.int32)
    x2, target = x[0], loss_target[0]

    small_loc = jnp.concatenate(
        [norm_gains.reshape(8, dq), conv_w[0], jnp.zeros((5, dq), F32)], axis=0).reshape(1, 2, 8, dq)
    pool_loc = pool_w.astype(BF16).reshape(1, 2, ng // 2 * (gw // N_CHIPS), gw)
    wgu_loc = [_cast_layer(ffn_gate_up_w, 0, "cast_gate_up0").reshape(1, 2, d // 2, fc),
               ffn_gate_up_w[1:2].astype(BF16).reshape(1, 2, d // 2, fc)]
    wd1_loc = ffn_down_w[1:2].astype(BF16).reshape(1, 2, fq // 2, d)
    win_loc = conv_in_w.astype(BF16).reshape(1, 2, d // 2, -1)
    wout_loc = conv_out_w.astype(BF16).reshape(1, 2, dq // 2, d)

    def ffn_weights(wgu_f, wd_f):
        return wgu_f.reshape(N_CHIPS, d, fc), wd_f.reshape(f, d)

    ag0 = _SplitGather([(pool_loc, 0), (small_loc, 0), (wgu_loc[0], 0)])
    ag0.start("ag_start_gate_up0")
    wd0_loc = _cast_layer(ffn_down_w, 0, "cast_down0", [ag0.token]).reshape(1, 2, fq // 2, d)
    ag0d = _SplitGather([(wd0_loc, 0)])
    ag0d.start("ag_start_down0", [ag0.token])
    ag1 = _SplitGather([(win_loc, 0), (wout_loc, 0), (wgu_loc[1], 0), (wd1_loc, 0)])
    ag1.start("ag_start_layer1", [ag0d.token])
    pool_f, small_f = _pass_on(ag0.wait([0, 1], ag1.token, "ag_wait_first"), "ag_pass_first")
    poolw = pool_f.reshape(N_CHIPS, ng, gw // N_CHIPS, gw).transpose(1, 0, 2, 3).reshape(ng, gw, gw)
    small = small_f.transpose(1, 2, 0, 3).reshape(16, d)
    h1 = _fwd_pool(x2, small, pool_scale, poolw)
    (wgu0,) = _pass_on(ag0.wait([2], h1, "ag_wait_gate_up0"), "ag_pass_gate_up0")
    wgu0 = wgu0.reshape(N_CHIPS, d, fc)
    gu0, n0, act_fwd0 = _fwd_ffn_up(h1, small, wgu0, 0)
    wd0, win_f, wout_f = _pass_on(
        ag0d.wait([0], gu0, "ag_wait_down0") + ag1.wait([0, 1], gu0, "ag_wait_conv"), "ag_pass_down0_conv")
    wd0 = wd0.reshape(f, d)
    h2, ff0 = _fwd_ffn_down(h1, act_fwd0, small, wd0, 0)
    win_f, wout_f = win_f.reshape(N_CHIPS, d, -1), wout_f.reshape(d, d)
    h3, proj, y, nc = _fwd_conv(h2, small, win_f, wout_f)
    wgu1, wd1 = ffn_weights(*_pass_on(ag1.wait([2, 3], h3, "ag_wait_ffn1"), "ag_pass_ffn1"))
    dh4, gu1, ff1, n1, loss_blk = _fwd_ffn(h3, small, wgu1, wd1, 1, target=target)

    dh3, dgu1, dff1, act1, sg_f1 = _bwd_ffn(dh4, h3, ff1, gu1, small, wgu1, wd1, 1)
    parts_d1 = _weight_grad(act1, dff1, fc, d // 2, "b", "dw_down1")
    parts_gu1 = _weight_grad(n1, dgu1, d // 2, fc, "a", "dw_gate_up1")
    ex_ffn1 = _SplitExchange([parts_d1.reshape(N_CHIPS, fq, d // 2), parts_gu1])
    started = ex_ffn1.start("rs_start_ffn1")
    dh2, dproj, dyv, bcv, sg_c = _bwd_conv(dh3, h2, y, proj, small, win_f, wout_f, [started])
    parts_in = _weight_grad(nc, dproj, d // 2, 3 * d // N_CHIPS, "a", "dw_conv_in")
    parts_out = _weight_grad(bcv, dyv, dq // 2, d, "a", "dw_conv_out")
    ex_conv = _SplitExchange([parts_in, parts_out])
    started = ex_conv.start("rs_start_conv")
    dh1, dgu0, dff0, act0, sg_f0 = _bwd_ffn(dh2, h1, ff0, gu0, small, wgu0, wd0, 0, [started])
    parts_d0 = _weight_grad(act0, dff0, fc, d // 2, "b", "dw_down0")
    parts_gu0 = _weight_grad(n0, dgu0, d // 2, fc, "a", "dw_gate_up0")
    ex_ffn0 = _SplitExchange([parts_d0.reshape(N_CHIPS, fq, d // 2), parts_gu0])
    started = ex_ffn0.start("rs_start_ffn0")
    grad_x, dpool, sg_p = _bwd_pool(dh1, x2, small, pool_scale, poolw, [started])
    g_pool = dpool.astype(BF16).reshape(2, ng // 2, N_CHIPS, gw // N_CHIPS, gw).transpose(2, 0, 1, 3, 4).reshape(
        N_CHIPS, 2, ng // 2 * (gw // N_CHIPS), gw)
    small_g = jnp.concatenate(
        [sg_p[0:2], sg_f0[0:2], sg_c[0:2], sg_f1[0:2], sg_c[2:5], sg_p[2:3],
         jnp.broadcast_to(loss_blk[0:1, 0:1], (1, d)), jnp.zeros((3, d), F32)], axis=0)
    ex_pool = _SplitExchange([_add_sibling(g_pool, _sibling_exchange(g_pool, "rs_sibling_pool"), core_arr)])
    started = ex_pool.start("rs_start_pool")

    def update(w, g, m, v, name):
        if w.size * 4 * 8 <= STREAM_BUDGET // 4:
            def body(w_ref, g_ref, m_ref, v_ref, go_ref, d_ref, nm_ref, nv_ref):
                go_ref[...] = g_ref[...]
                d_ref[...], nm_ref[...], nv_ref[...] = _adamw_math(w_ref[...], g_ref[...], m_ref[...], v_ref[...])

            return tuple(pl.pallas_call(body, name="adamw_" + name, out_shape=[jax.ShapeDtypeStruct(w.shape, F32)] * 4)(
                w, g.reshape(w.shape), m, v))
        flat = (-1, w.shape[-1])
        dl, m2, v2, g2 = _adamw(w.reshape(flat), g.reshape(flat), m.reshape(flat), v.reshape(flat), "adamw_" + name)
        return tuple(o.reshape(w.shape) for o in (g2, dl, m2, v2))

    (parts_d1, parts_gu1), (recv_d1, recv_gu1) = ex_ffn1.wait([started], "rs_wait_ffn1")
    (parts_in, parts_out), (recv_in, recv_out) = ex_conv.wait([started], "rs_wait_conv")
    gs_gu = _add_chips(parts_gu1, recv_gu1, where_arr, 1, 2)
    gs_d = _add_chips(parts_d1, recv_d1, where_arr, 1, 2, col_half=True)
    gs_in = _add_chips(parts_in, recv_in, where_arr)
    gs_out = _add_chips(parts_out, recv_out, where_arr)
    full_in, full_out = _sibling_share([gs_in, gs_out], [False, False], "rs_share_conv")
    up_in = update(conv_in_w, full_in.reshape(1, d, -1), m_conv_in_w, v_conv_in_w, "conv_in")
    up_out = update(conv_out_w, full_out.reshape(1, dq, d), m_conv_out_w, v_conv_out_w, "conv_out")
    done_first = [up_in[1], up_out[1], gs_gu, gs_d]
    (parts_d0, parts_gu0), (recv_d0, recv_gu0) = ex_ffn0.wait(done_first, "rs_wait_ffn0")
    (parts_p,), (recv_p,) = ex_pool.wait(done_first, "rs_wait_pool")
    ex_small = _SplitExchange([small_g], _SmallGather(small_g))
    started = ex_small.start("rs_start_small", [recv_gu0])
    gs_gu = _add_chips(parts_gu0, recv_gu0, where_arr, 0, 2, gs_gu)
    gs_d = _add_chips(parts_d0, recv_d0, where_arr, 0, 2, gs_d, col_half=True)
    gs_pool = _add_chips(parts_p, recv_p, where_arr)
    full_gu, full_d, full_pool = _sibling_share([gs_gu, gs_d, gs_pool], [False, True, False], "rs_share_ffn", [started])
    up_gu = update(ffn_gate_up_w, full_gu.reshape(2, d, fc), m_ffn_gate_up_w, v_ffn_gate_up_w, "gate_up")
    up_d = update(ffn_down_w, full_d.reshape(2, fq, d), m_ffn_down_w, v_ffn_down_w, "down")
    (small_own,), (small_all,) = ex_small.wait([up_gu[1], up_d[1]], "rs_wait_small")
    small_sum = _sum_small(small_all, small_own, (2 * chip + core).reshape(1).astype(jnp.int32))
    loss = small_sum[12, 0]
    up_gains, up_taps, up_scale = _adamw_small(
        small_sum, where_arr, (norm_gains, m_norm_gains, v_norm_gains), (conv_w, m_conv_w, v_conv_w),
        (pool_scale, m_pool_scale, v_pool_scale))

    ups = [
        up_gains,
        update(pool_w, full_pool.reshape(1, ng, gw // N_CHIPS, gw), m_pool_w, v_pool_w, "pool_w"),
        up_scale,
        up_in,
        up_taps,
        up_out,
        up_gu,
        up_d,
    ]
    grads_out, deltas, new_ms, new_vs = (list(col) for col in zip(*ups))
    return (loss, grad_x[None], *grads_out, *deltas, *new_ms, *new_vs)
```

```python
import jax
import jax.numpy as jnp
from jax import lax
from jax.experimental import pallas as pl
from jax.experimental.pallas import tpu as pltpu

RMS_EPS = 1e-6
POOL_WINDOWS = (2, 4, 8, 16)
POOL_HALO = 16
CONV_HALO = 8
N_CHIPS = 4
N_DEV = 8
ADAM_LR = 0.001
ADAM_B1 = 0.9
ADAM_B2 = 0.999
ADAM_EPS = 1e-08
ADAM_WD = 0.01
ADAM_STEP = 10
VMEM_LIMIT = 56 * 2**20
STREAM_BUDGET = 24 * 2**20
MESH = pl.DeviceIdType.MESH
ANY = pl.BlockSpec(memory_space=pl.ANY)
DMA = pltpu.SemaphoreType.DMA
BF16 = jnp.bfloat16
F32 = jnp.float32


TOKEN_TILE = 512
FFN_BWD_TOKEN_TILE = 256


def _token_tile(t, rows=TOKEN_TILE):
    return min(rows, t)


def _rms(x):
    r = lax.rsqrt(jnp.mean(x * x, axis=-1, keepdims=True) + RMS_EPS)
    return x * r, r


def _rms_bwd(dy, xh, r, g):
    a = dy * g
    return r * (a - xh * jnp.mean(a * xh, axis=-1, keepdims=True))


def _dot(a, b):
    return jnp.dot(a, b, preferred_element_type=F32)


def _dot_nt(a, b):
    return lax.dot_general(a, b, (((1,), (1,)), ((), ())), preferred_element_type=F32)


def _dot_tn(a, b):
    return lax.dot_general(a, b, (((0,), (0,)), ((), ())), preferred_element_type=F32)


def _colsum(a):
    return jnp.sum(a, axis=0, keepdims=True)


def _resident(block, index_map):
    return pl.BlockSpec(block, index_map, pipeline_mode=pl.Buffered(1))


def _row_block(r, row_bytes):
    best = None
    for rb in range(16, r + 1, 16):
        if r % rb == 0 and rb * row_bytes <= STREAM_BUDGET:
            best = rb
    return best if best is not None else r


def _place():
    x, y, c = lax.axis_index("x"), lax.axis_index("y"), lax.axis_index("c")
    return x, y, c, 2 * x + y


def _dev(chip, core):
    return (chip // 2, chip % 2, core)


def _remote(src, dst, send_sem, recv_sem, device):
    return pltpu.make_async_remote_copy(src_ref=src, dst_ref=dst, send_sem=send_sem, recv_sem=recv_sem,
                                        device_id=device, device_id_type=MESH)


class _Gather:
    def __init__(self, shards):
        self.args = [s for s, _ in shards]
        self.layers = [l for _, l in shards]
        self.out_shape = [jax.ShapeDtypeStruct((N_CHIPS,) + s.shape[1:], s.dtype) for s in self.args]

    def _own(self, loc, out, sems, a):
        x, y, c, k = _place()
        return _remote(loc[a].at[self.layers[a]], out[a].at[k], sems[0].at[a], sems[1].at[a], (x, y, 1 - c))

    def _ici(self, loc, out, sems, a, m, arrival):
        x, y, c, k = _place()
        dst = out[a].at[k ^ m, c] if arrival else out[a].at[k, c]
        return _remote(loc[a].at[self.layers[a], c], dst, sems[2].at[a, m - 1], sems[3].at[a, m - 1], _dev(k ^ m, c))

    def start(self, loc, out, sems):
        for a in range(len(self.args)):
            for m in range(1, N_CHIPS):
                self._ici(loc, out, sems, a, m, False).start()
            self._own(loc, out, sems, a).start()


class _SmallGather:
    PEERS = N_DEV - 1

    def __init__(self, small):
        self.args = [small]
        self.out_shape = [jax.ShapeDtypeStruct((N_DEV,) + small.shape, small.dtype)]

    def _copy(self, sm, land, sems, m, arrival):
        x, y, c, k = _place()
        me = 2 * k + c
        peer = me ^ m
        return _remote(sm[0], land[0].at[peer if arrival else me], sems[0].at[0, m - 1], sems[1].at[0, m - 1],
                       (peer // 4, (peer // 2) % 2, peer % 2))

    def start(self, sm, land, sems):
        for m in range(1, N_DEV):
            self._copy(sm, land, sems, m, False).start()

    def finish(self, sm, land, sems):
        for m in range(1, N_DEV):
            self._copy(sm, land, sems, m, True).wait_recv()
        for m in range(1, N_DEV):
            self._copy(sm, land, sems, m, False).wait_send()


class _ChipExchange:
    PEERS = N_CHIPS - 1

    def __init__(self, parts):
        self.args = list(parts)
        self.out_shape = [jax.ShapeDtypeStruct((N_CHIPS - 1,) + p.shape[1:], p.dtype) for p in parts]

    def _copy(self, p, land, sems, a, m):
        x, y, c, k = _place()
        return _remote(p[a].at[k ^ m], land[a].at[m - 1], sems[0].at[a, m - 1], sems[1].at[a, m - 1], _dev(k ^ m, c))

    def start(self, p, land, sems):
        for a in range(len(self.args)):
            for m in range(1, N_CHIPS):
                self._copy(p, land, sems, a, m).start()

    def finish(self, p, land, sems):
        for a in range(len(self.args)):
            for m in range(1, N_CHIPS):
                self._copy(p, land, sems, a, m).wait_recv()
        for a in range(len(self.args)):
            for m in range(1, N_CHIPS):
                self._copy(p, land, sems, a, m).wait_send()


def _gridded(body, after, *, name, grid, in_specs, out_specs, out_shape, args, scratch_shapes=()):
    ni, na = len(in_specs), len(after)

    def full(*refs):
        body(*refs[:ni], *refs[ni + na:])

    return list(pl.pallas_call(
        full, name=name, grid=grid, in_specs=[*in_specs, *[ANY] * na], out_specs=list(out_specs),
        out_shape=list(out_shape), scratch_shapes=list(scratch_shapes),
        compiler_params=pltpu.CompilerParams(dimension_semantics=("arbitrary",) * len(grid), vmem_limit_bytes=VMEM_LIMIT),
    )(*args, *after))


HBM = pl.BlockSpec(memory_space=pltpu.HBM)
SEM = pl.BlockSpec(memory_space=pltpu.SEMAPHORE)
DATAFLOW = pltpu.SideEffectType.DATAFLOW_SIDE_EFFECTING


class _SplitGather:
    PER_ARRAY = 8

    def __init__(self, shards):
        self.plan = _Gather(shards)
        self.n = len(shards)

    @staticmethod
    def _tables(sems_of):
        class Table:
            def __init__(self, pick):
                self.pick = pick

            @property
            def at(self):
                return self

            def __getitem__(self, idx):
                return self.pick(idx)

        return [Table(lambda a: sems_of[a][0]), Table(lambda a: sems_of[a][1]),
                Table(lambda am: sems_of[am[0]][2 + am[1]]), Table(lambda am: sems_of[am[0]][5 + am[1]])]

    def start(self, name, after=()):
        n, plan, per, na = self.n, self.plan, self.PER_ARRAY, len(after)

        def body(*refs):
            loc, land = refs[:n], refs[n:2 * n]
            sems_of = {a: refs[2 * n + na + per * a:2 * n + na + per * (a + 1)] for a in range(n)}
            plan.start(loc, land, self._tables(sems_of))
            refs[-1][...] = jnp.zeros_like(refs[-1])

        lands = [pltpu.with_memory_space_constraint(lax.empty(o.shape, o.dtype), pltpu.HBM) for o in plan.out_shape]
        locs = [pltpu.with_memory_space_constraint(a, pltpu.HBM) for a in plan.args]
        res = pl.pallas_call(
            body, name=name,
            out_shape=[*[DMA(())] * (per * n),
                       *[pltpu.HBM(o.shape, o.dtype) for o in plan.out_shape],
                       jax.ShapeDtypeStruct((8, 128), F32)],
            in_specs=[HBM] * (2 * n) + [pl.BlockSpec(memory_space=pl.ANY)] * na,
            out_specs=[SEM] * (per * n) + [HBM] * n + [pl.BlockSpec(memory_space=pltpu.VMEM)],
            input_output_aliases={n + i: per * n + i for i in range(n)},
            compiler_params=pltpu.CompilerParams(has_side_effects=DATAFLOW),
        )(*locs, *lands, *after)
        self.sems = {a: list(res[per * a:per * (a + 1)]) for a in range(n)}
        self.locs = locs
        self.lands = list(res[per * n:per * n + n])
        self.token = res[-1]

    def wait(self, idxs, after, name):
        plan, g, per = self.plan, len(idxs), self.PER_ARRAY

        def body(*refs):
            loc = {a: refs[j] for j, a in enumerate(idxs)}
            land = {a: refs[g + j] for j, a in enumerate(idxs)}
            sems = self._tables({a: refs[2 * g + per * j:2 * g + per * (j + 1)] for j, a in enumerate(idxs)})
            for a in idxs:
                for m in range(1, N_CHIPS):
                    plan._ici(loc, land, sems, a, m, True).wait_recv()
                    plan._ici(loc, land, sems, a, m, False).wait_send()
                plan._own(loc, land, sems, a).wait_recv()
                plan._own(loc, land, sems, a).wait_send()

        res = pl.pallas_call(
            body, name=name,
            out_shape=[pltpu.HBM(self.lands[a].shape, self.lands[a].dtype) for a in idxs],
            in_specs=[HBM] * (2 * g) + [SEM] * (per * g) + [pl.BlockSpec(memory_space=pl.ANY)], out_specs=[HBM] * g,
            input_output_aliases={g + j: j for j in range(g)},
            compiler_params=pltpu.CompilerParams(has_side_effects=DATAFLOW),
        )(*[self.locs[a] for a in idxs], *[self.lands[a] for a in idxs],
          *[s for a in idxs for s in self.sems[a]], after)
        return list(res)


class _SplitExchange:
    def __init__(self, parts, plan=None):
        self.plan = _ChipExchange(parts) if plan is None else plan
        self.n = len(parts)
        self.PER_ARRAY = 2 * self.plan.PEERS

    def _tables(self, sems_of):
        class Table:
            def __init__(self, pick):
                self.pick = pick

            @property
            def at(self):
                return self

            def __getitem__(self, am):
                return self.pick(am)

        peers = self.plan.PEERS
        return [Table(lambda am: sems_of[am[0]][am[1]]), Table(lambda am: sems_of[am[0]][peers + am[1]])]

    def start(self, name, after=()):
        n, plan, per, na = self.n, self.plan, self.PER_ARRAY, len(after)

        def body(*refs):
            p, land = refs[:n], refs[n:2 * n]
            sems_of = {a: refs[2 * n + na + per * a:2 * n + na + per * (a + 1)] for a in range(n)}
            plan.start(p, land, self._tables(sems_of))
            refs[-1][...] = jnp.zeros_like(refs[-1])

        lands = [pltpu.with_memory_space_constraint(lax.empty(o.shape, o.dtype), pltpu.HBM) for o in plan.out_shape]
        parts = [pltpu.with_memory_space_constraint(a, pltpu.HBM) for a in plan.args]
        res = pl.pallas_call(
            body, name=name,
            out_shape=[*[DMA(())] * (per * n),
                       *[pltpu.HBM(a.shape, a.dtype) for a in plan.args],
                       *[pltpu.HBM(o.shape, o.dtype) for o in plan.out_shape],
                       jax.ShapeDtypeStruct((8, 128), F32)],
            in_specs=[HBM] * (2 * n) + [pl.BlockSpec(memory_space=pl.ANY)] * na,
            out_specs=[SEM] * (per * n) + [HBM] * (2 * n) + [pl.BlockSpec(memory_space=pltpu.VMEM)],
            input_output_aliases={i: per * n + i for i in range(2 * n)},
            compiler_params=pltpu.CompilerParams(has_side_effects=DATAFLOW),
        )(*parts, *lands, *after)
        self.sems = list(res[:per * n])
        self.parts = list(res[per * n:per * n + n])
        self.lands = list(res[per * n + n:per * n + 2 * n])
        return res[-1]

    def wait(self, after, name):
        n, plan, per = self.n, self.plan, self.PER_ARRAY

        def body(*refs):
            p, land = refs[:n], refs[n:2 * n]
            sems_of = {a: refs[2 * n + per * a:2 * n + per * (a + 1)] for a in range(n)}
            plan.finish(p, land, self._tables(sems_of))

        res = pl.pallas_call(
            body, name=name,
            out_shape=[*[pltpu.HBM(a.shape, a.dtype) for a in self.parts], *[pltpu.HBM(a.shape, a.dtype) for a in self.lands]],
            in_specs=[HBM] * (2 * n) + [SEM] * (per * n) + [pl.BlockSpec(memory_space=pl.ANY)] * len(after),
            out_specs=[HBM] * (2 * n), input_output_aliases={i: i for i in range(2 * n)},
            compiler_params=pltpu.CompilerParams(has_side_effects=DATAFLOW),
        )(*self.parts, *self.lands, *self.sems, *after)
        return list(res[:n]), list(res[n:])


PASS_ON_BARRIER = 1
EXCHANGE_BARRIER = 2
SHARE_BARRIER = 3


def _sibling_barrier():
    x, y, c, _ = _place()
    barrier = pltpu.get_barrier_semaphore()
    pl.semaphore_signal(barrier, inc=1, device_id=(x, y, 1 - c), device_id_type=MESH)
    pl.semaphore_wait(barrier, 1)


def _pass_on(lands, name):
    n = len(lands)

    def body(*refs):
        out = refs[n:2 * n]
        send_sems, recv_sems = refs[2 * n:]
        x, y, c, k = _place()
        _sibling_barrier()
        cps = []
        for a in range(n):
            for m in range(1, N_CHIPS):
                got = out[a].at[k ^ m, c]
                cp = _remote(got, got, send_sems.at[a, m - 1], recv_sems.at[a, m - 1], (x, y, 1 - c))
                cp.start()
                cps.append(cp)
        for a in range(n):
            for m in range(1, N_CHIPS):
                theirs = out[a].at[k ^ m, 1 - c]
                _remote(theirs, theirs, send_sems.at[a, m - 1], recv_sems.at[a, m - 1], (x, y, 1 - c)).wait_recv()
        for cp in cps:
            cp.wait_send()

    return pl.pallas_call(
        body, name=name, out_shape=[jax.ShapeDtypeStruct(a.shape, a.dtype) for a in lands],
        in_specs=[ANY] * n, out_specs=[ANY] * n, input_output_aliases={a: a for a in range(n)},
        scratch_shapes=[DMA((n, 3)), DMA((n, 3))],
        compiler_params=pltpu.CompilerParams(has_side_effects=True, collective_id=PASS_ON_BARRIER),
    )(*lands)


def _sibling_exchange(g, name):
    def body(g_ref, land_ref, send_sem, recv_sem):
        x, y, c, _ = _place()
        _sibling_barrier()
        cp = _remote(g_ref.at[:, pl.ds(1 - c, 1)], land_ref, send_sem, recv_sem, (x, y, 1 - c))
        cp.start()
        cp.wait_recv()
        cp.wait_send()

    return pl.pallas_call(
        body, name=name, out_shape=jax.ShapeDtypeStruct((N_CHIPS, 1) + g.shape[2:], g.dtype), in_specs=[ANY],
        out_specs=ANY, scratch_shapes=[DMA, DMA],
        compiler_params=pltpu.CompilerParams(has_side_effects=True, collective_id=EXCHANGE_BARRIER),
    )(g)


def _sibling_share(halves, col_half, name, after=()):
    n, na = len(halves), len(after)

    def body(*refs):
        out = refs[n + na:2 * n + na]
        send_sems, recv_sems = refs[2 * n + na:]
        x, y, c, k = _place()

        def half(a, core):
            if not col_half[a]:
                return out[a].at[:, pl.ds(core, 1)]
            cols = out[a].shape[-1] // 2
            return out[a].at[:, :, pl.ds(pl.multiple_of(core * cols, cols), cols)]

        _sibling_barrier()
        cps = []
        for a in range(n):
            cp = _remote(half(a, c), half(a, c), send_sems.at[a], recv_sems.at[a], (x, y, 1 - c))
            cp.start()
            cps.append(cp)
        for a in range(n):
            _remote(half(a, 1 - c), half(a, 1 - c), send_sems.at[a], recv_sems.at[a], (x, y, 1 - c)).wait_recv()
        for cp in cps:
            cp.wait_send()

    out_shape = [jax.ShapeDtypeStruct(a.shape, a.dtype) for a in halves]
    return pl.pallas_call(
        body, name=name, out_shape=out_shape, in_specs=[ANY] * (n + na), out_specs=[ANY] * n,
        input_output_aliases={a: a for a in range(n)}, scratch_shapes=[DMA((n,)), DMA((n,))],
        compiler_params=pltpu.CompilerParams(has_side_effects=True, collective_id=SHARE_BARRIER),
    )(*halves, *after)


def _add_sibling(g, land, core):
    _, _, r, c = g.shape
    rb = _row_block(r, c * (3 * 2 * 2 + 2 * 4))

    def body(core_ref, g_ref, l_ref, o_ref):
        o_ref[...] = (g_ref[...].astype(F32) + l_ref[...].astype(F32)).astype(o_ref.dtype)

    return pl.pallas_call(
        body, name="rs_add_sibling", out_shape=jax.ShapeDtypeStruct((N_CHIPS, r, c), g.dtype),
        grid_spec=pltpu.PrefetchScalarGridSpec(
            num_scalar_prefetch=1, grid=(N_CHIPS, r // rb),
            in_specs=[pl.BlockSpec((None, None, rb, c), lambda j, i, core_ref: (j, core_ref[0], i, 0)),
                      pl.BlockSpec((None, None, rb, c), lambda j, i, core_ref: (j, 0, i, 0))],
            out_specs=pl.BlockSpec((None, rb, c), lambda j, i, core_ref: (j, i, 0))),
        compiler_params=pltpu.CompilerParams(dimension_semantics=("parallel", "parallel"), vmem_limit_bytes=VMEM_LIMIT),
    )(core, g, land)


def _add_chips(part, land, where, layer=0, n_layers=1, into=None, col_half=False):
    _, r, c = part.shape
    rb = _row_block(r, c * (4 * 2 * 2 + 4 * 2 + 2 * 4))

    def body(where_ref, p_ref, l_ref, *rest):
        acc = p_ref[...].astype(F32)
        for m in range(N_CHIPS - 1):
            acc = acc + l_ref[m].astype(F32)
        rest[-1][...] = acc

    in_specs = [pl.BlockSpec((None, rb, c), lambda i, where_ref: (where_ref[0], i, 0)),
                pl.BlockSpec((N_CHIPS - 1, rb, c), lambda i, where_ref: (0, i, 0))]
    args = [where, part, land]
    if into is not None:
        in_specs.append(ANY)
        args.append(into)
    if col_half:
        out_shape = jax.ShapeDtypeStruct((n_layers, r, 2 * c), F32)
        out_spec = pl.BlockSpec((None, rb, c), lambda i, where_ref: (layer, i, where_ref[1]))
    else:
        out_shape = jax.ShapeDtypeStruct((n_layers, 2, r, c), F32)
        out_spec = pl.BlockSpec((None, None, rb, c), lambda i, where_ref: (layer, where_ref[1], i, 0))
    return pl.pallas_call(
        body, name="rs_add_chips", out_shape=out_shape,
        grid_spec=pltpu.PrefetchScalarGridSpec(
            num_scalar_prefetch=1, grid=(r // rb,), in_specs=in_specs, out_specs=out_spec),
        input_output_aliases={} if into is None else {3: 0},
        compiler_params=pltpu.CompilerParams(dimension_semantics=("parallel",), vmem_limit_bytes=VMEM_LIMIT),
    )(*args)


def _sum_small(smg, own, me):
    def body(me_ref, s_ref, own_ref, o_ref):
        o_ref[...] = jnp.zeros_like(o_ref)
        for j in range(N_DEV):
            @pl.when(me_ref[0] == j)
            def _():
                o_ref[...] += own_ref[...]

            @pl.when(me_ref[0] != j)
            def _():
                o_ref[...] += s_ref[j]

    return pl.pallas_call(
        body, name="rs_sum_small", out_shape=jax.ShapeDtypeStruct(smg.shape[1:], F32),
        grid_spec=pltpu.PrefetchScalarGridSpec(
            num_scalar_prefetch=1, grid=(1,),
            in_specs=[pl.BlockSpec(smg.shape, lambda i, me_ref: (0, 0, 0)), pl.BlockSpec(own.shape, lambda i, me_ref: (0, 0))],
            out_specs=pl.BlockSpec(own.shape, lambda i, me_ref: (0, 0))),
    )(me, smg, own)


def _pool_windows(ext_ref, g, gw, tm, first_row):
    w = POOL_WINDOWS[g]
    slab = ext_ref[:, g * gw:(g + 1) * gw]
    p, k = slab, 1
    while k < w:
        p = p + pltpu.roll(p, k, 0)
        k *= 2
    t = first_row + lax.broadcasted_iota(jnp.int32, (tm, 1), 0)
    cnt = jnp.minimum(t + 1, w).astype(F32)
    return p[POOL_HALO:] / cnt - slab[POOL_HALO:]


def _fwd_pool(x, small, scale, poolw, after=()):
    t, d = x.shape
    tm = _token_tile(t)
    gw = d // len(POOL_WINDOWS)

    def body(x_ref, sm_ref, sc_ref, w_ref, h_ref, ext_ref, mix_ref):
        i = pl.program_id(0)

        @pl.when(i == 0)
        def _():
            ext_ref[0:POOL_HALO, :] = jnp.zeros((POOL_HALO, d), F32)

        @pl.when(i > 0)
        def _():
            ext_ref[0:POOL_HALO, :] = ext_ref[tm:tm + POOL_HALO, :]

        xv = x_ref[...]
        xh, _ = _rms(xv)
        ext_ref[POOL_HALO:, :] = xh * sm_ref[0:1, :]
        for g in range(len(POOL_WINDOWS)):
            pooled = _pool_windows(ext_ref, g, gw, tm, i * tm)
            cols = slice(g * gw, (g + 1) * gw)
            mix_ref[:, cols] = _dot(pooled.astype(BF16), w_ref[g]) * sc_ref[:, cols]
        mh, _ = _rms(mix_ref[...])
        h_ref[...] = xv + mh * sm_ref[1:2, :]

    return _gridded(
        body, after, name="fwd_pool", grid=(t // tm,), out_shape=[jax.ShapeDtypeStruct((t, d), F32)],
        in_specs=[pl.BlockSpec((tm, d), lambda i: (i, 0)), _resident(small.shape, lambda i: (0, 0)),
                  _resident(scale.shape, lambda i: (0, 0)), _resident(poolw.shape, lambda i: (0, 0, 0))],
        out_specs=[pl.BlockSpec((tm, d), lambda i: (i, 0))],
        scratch_shapes=[pltpu.VMEM((POOL_HALO + tm, d), F32), pltpu.VMEM((tm, d), F32)],
        args=[x, small, scale, poolw])[0]


def _fwd_ffn(h, small, wgu, wd, layer, after=(), target=None):
    t, d = h.shape
    tm = _token_tile(t)
    steps = t // tm
    fc = wgu.shape[-1]
    f = 2 * fc
    g_in, g_out = 4 * layer + 2, 4 * layer + 3
    with_loss = target is not None

    def body(h_ref, *refs):
        if with_loss:
            t_ref, sm_ref, wgu_ref, wd_ref, o_ref, gu_ref, ff_ref, n_ref, l_ref, acc_ref = refs
        else:
            sm_ref, wgu_ref, wd_ref, o_ref, gu_ref, ff_ref, n_ref = refs
        hv = h_ref[...]
        hh, _ = _rms(hv)
        n = (hh * sm_ref[g_in:g_in + 1, :]).astype(BF16)
        n_ref[...] = n
        ff = None
        for j in range(2):
            gate = _dot(n, wgu_ref[j])
            up = _dot(n, wgu_ref[2 + j])
            gu_ref[:, j * fc:(j + 1) * fc] = gate.astype(BF16)
            gu_ref[:, f + j * fc:f + (j + 1) * fc] = up.astype(BF16)
            act = (gate * jax.nn.sigmoid(gate) * up).astype(BF16)
            part = _dot(act, wd_ref[j * fc:(j + 1) * fc, :])
            ff = part if ff is None else ff + part
        ff_ref[...] = ff
        fh, _ = _rms(ff)
        out = hv + fh * sm_ref[g_out:g_out + 1, :]
        if not with_loss:
            o_ref[...] = out
            return
        i = pl.program_id(0)
        e = out - t_ref[...]
        o_ref[...] = e * (1.0 / d)

        @pl.when(i == 0)
        def _():
            acc_ref[...] = jnp.zeros_like(acc_ref)

        acc_ref[...] += _colsum(e * e)

        @pl.when(i == steps - 1)
        def _():
            l_ref[...] = jnp.full(l_ref.shape, 0.5 / d, F32) * jnp.sum(acc_ref[...])

    row = lambda i: (i, 0)
    out_shape = [jax.ShapeDtypeStruct((t, d), F32), jax.ShapeDtypeStruct((t, 2 * f), BF16),
                 jax.ShapeDtypeStruct((t, d), F32), jax.ShapeDtypeStruct((t, d), BF16)]
    out_specs = [pl.BlockSpec((tm, d), row), pl.BlockSpec((tm, 2 * f), row), pl.BlockSpec((tm, d), row),
                 pl.BlockSpec((tm, d), row)]
    weight_specs = [_resident(small.shape, lambda i: (0, 0)), _resident(wgu.shape, lambda i: (0, 0, 0)),
                    _resident(wd.shape, lambda i: (0, 0))]
    if with_loss:
        return _gridded(
            body, after, name=f"fwd_ffn{layer}_loss", grid=(steps,),
            out_shape=out_shape + [jax.ShapeDtypeStruct((8, 128), F32)],
            in_specs=[pl.BlockSpec((tm, d), row), pl.BlockSpec((tm, d), row)] + weight_specs,
            out_specs=out_specs + [pl.BlockSpec((8, 128), lambda i: (0, 0))],
            scratch_shapes=[pltpu.VMEM((1, d), F32)], args=[h, target, small, wgu, wd])
    return _gridded(
        body, after, name=f"fwd_ffn{layer}", grid=(steps,), out_shape=out_shape,
        in_specs=[pl.BlockSpec((tm, d), row)] + weight_specs, out_specs=out_specs, args=[h, small, wgu, wd])


def _fwd_ffn_up(h, small, wgu, layer):
    t, d = h.shape
    tm = _token_tile(t)
    fc = wgu.shape[-1]
    f = 2 * fc
    g_in = 4 * layer + 2

    def body(h_ref, sm_ref, wgu_ref, gu_ref, n_ref, act_ref):
        hh, _ = _rms(h_ref[...])
        n = (hh * sm_ref[g_in:g_in + 1, :]).astype(BF16)
        n_ref[...] = n
        for j in range(2):
            gate = _dot(n, wgu_ref[j])
            up = _dot(n, wgu_ref[2 + j])
            gu_ref[:, j * fc:(j + 1) * fc] = gate.astype(BF16)
            gu_ref[:, f + j * fc:f + (j + 1) * fc] = up.astype(BF16)
            act_ref[:, j * fc:(j + 1) * fc] = (gate * jax.nn.sigmoid(gate) * up).astype(BF16)

    row = lambda i: (i, 0)
    return _gridded(
        body, (), name=f"fwd_ffn{layer}_up", grid=(t // tm,),
        out_shape=[jax.ShapeDtypeStruct((t, 2 * f), BF16), jax.ShapeDtypeStruct((t, d), BF16),
                   jax.ShapeDtypeStruct((t, f), BF16)],
        in_specs=[pl.BlockSpec((tm, d), row), _resident(small.shape, lambda i: (0, 0)),
                  _resident(wgu.shape, lambda i: (0, 0, 0))],
        out_specs=[pl.BlockSpec((tm, 2 * f), row), pl.BlockSpec((tm, d), row), pl.BlockSpec((tm, f), row)],
        args=[h, small, wgu])


def _fwd_ffn_down(h, act, small, wd, layer):
    t, d = h.shape
    tm = _token_tile(t)
    f = wd.shape[0]
    g_out = 4 * layer + 3

    def body(h_ref, act_ref, sm_ref, wd_ref, o_ref, ff_ref):
        ff = _dot(act_ref[...], wd_ref[...])
        ff_ref[...] = ff
        fh, _ = _rms(ff)
        o_ref[...] = h_ref[...] + fh * sm_ref[g_out:g_out + 1, :]

    row = lambda i: (i, 0)
    return _gridded(
        body, (), name=f"fwd_ffn{layer}_down", grid=(t // tm,),
        out_shape=[jax.ShapeDtypeStruct((t, d), F32), jax.ShapeDtypeStruct((t, d), F32)],
        in_specs=[pl.BlockSpec((tm, d), row), pl.BlockSpec((tm, f), row), _resident(small.shape, lambda i: (0, 0)),
                  _resident(wd.shape, lambda i: (0, 0))],
        out_specs=[pl.BlockSpec((tm, d), row), pl.BlockSpec((tm, d), row)], args=[h, act, small, wd])


def _fwd_conv(h, small, win, wout, after=()):
    t, d = h.shape
    tm = _token_tile(t)
    pc = win.shape[-1]

    def body(h_ref, sm_ref, win_ref, wout_ref, o_ref, proj_ref, y_ref, n_ref, pj_ref, uext_ref):
        i = pl.program_id(0)

        @pl.when(i == 0)
        def _():
            uext_ref[0:CONV_HALO, :] = jnp.zeros((CONV_HALO, d), F32)

        @pl.when(i > 0)
        def _():
            uext_ref[0:CONV_HALO, :] = uext_ref[tm:tm + CONV_HALO, :]

        hv = h_ref[...]
        hh, _ = _rms(hv)
        n = (hh * sm_ref[4:5, :]).astype(BF16)
        n_ref[...] = n
        for k in range(N_CHIPS):
            pj_ref[:, k * pc:(k + 1) * pc] = _dot(n, win_ref[k])
        proj_ref[...] = pj_ref[...].astype(BF16)
        uext_ref[CONV_HALO:, :] = pj_ref[:, d:2 * d] * pj_ref[:, 2 * d:]
        taps = [sm_ref[8 + j:9 + j, :] for j in range(3)]
        full = uext_ref[...]
        conv = (full[CONV_HALO:] * taps[2] + pltpu.roll(full, 1, 0)[CONV_HALO:] * taps[1]
                + pltpu.roll(full, 2, 0)[CONV_HALO:] * taps[0])
        y = _dot((pj_ref[:, 0:d] * conv).astype(BF16), wout_ref[...])
        y_ref[...] = y
        yh, _ = _rms(y)
        o_ref[...] = hv + yh * sm_ref[5:6, :]

    row = lambda i: (i, 0)
    return _gridded(
        body, after, name="fwd_conv", grid=(t // tm,),
        out_shape=[jax.ShapeDtypeStruct((t, d), F32), jax.ShapeDtypeStruct((t, 3 * d), BF16),
                   jax.ShapeDtypeStruct((t, d), F32), jax.ShapeDtypeStruct((t, d), BF16)],
        in_specs=[pl.BlockSpec((tm, d), row), _resident(small.shape, lambda i: (0, 0)),
                  _resident(win.shape, lambda i: (0, 0, 0)), _resident(wout.shape, lambda i: (0, 0))],
        out_specs=[pl.BlockSpec((tm, d), row), pl.BlockSpec((tm, 3 * d), row), pl.BlockSpec((tm, d), row),
                   pl.BlockSpec((tm, d), row)],
        scratch_shapes=[pltpu.VMEM((tm, 3 * d), F32), pltpu.VMEM((CONV_HALO + tm, d), F32)],
        args=[h, small, win, wout])


def _bwd_ffn(dh, h, ff, gu, small, wgu, wd, layer, after=()):
    t, d = h.shape
    tm = _token_tile(t, FFN_BWD_TOKEN_TILE)
    fc = wgu.shape[-1]
    f = 2 * fc
    g_in, g_out = 4 * layer + 2, 4 * layer + 3

    def body(dh_ref, h_ref, ff_ref, gu_ref, sm_ref, wgu_ref, wd_ref, o_ref, dgu_ref, dff_ref, act_ref, sg_ref):
        i = pl.program_id(0)

        @pl.when(i == 0)
        def _():
            sg_ref[...] = jnp.zeros_like(sg_ref)

        dy = dh_ref[...]
        fh, r3 = _rms(ff_ref[...])
        sg_ref[1:2, :] += _colsum(dy * fh)
        dff = _rms_bwd(dy, fh, r3, sm_ref[g_out:g_out + 1, :]).astype(BF16)
        dff_ref[...] = dff
        for j in range(2):
            dact = _dot_nt(dff, wd_ref[j * fc:(j + 1) * fc, :])
            gate = gu_ref[:, j * fc:(j + 1) * fc].astype(F32)
            up = gu_ref[:, f + j * fc:f + (j + 1) * fc].astype(F32)
            sig = jax.nn.sigmoid(gate)
            silu = gate * sig
            act_ref[:, j * fc:(j + 1) * fc] = (silu * up).astype(BF16)
            dgu_ref[:, j * fc:(j + 1) * fc] = (dact * up * (sig * (1.0 + gate * (1.0 - sig)))).astype(BF16)
            dgu_ref[:, f + j * fc:f + (j + 1) * fc] = (dact * silu).astype(BF16)
        dn = None
        for k in range(N_CHIPS):
            part = _dot_nt(dgu_ref[:, k * fc:(k + 1) * fc], wgu_ref[k])
            dn = part if dn is None else dn + part
        hh, r2 = _rms(h_ref[...])
        sg_ref[0:1, :] += _colsum(dn * hh)
        o_ref[...] = dy + _rms_bwd(dn, hh, r2, sm_ref[g_in:g_in + 1, :])

    row = lambda i: (i, 0)
    return _gridded(
        body, after, name=f"bwd_ffn{layer}", grid=(t // tm,),
        out_shape=[jax.ShapeDtypeStruct((t, d), F32), jax.ShapeDtypeStruct((t, 2 * f), BF16),
                   jax.ShapeDtypeStruct((t, d), BF16), jax.ShapeDtypeStruct((t, f), BF16),
                   jax.ShapeDtypeStruct((8, d), F32)],
        in_specs=[pl.BlockSpec((tm, d), row), pl.BlockSpec((tm, d), row), pl.BlockSpec((tm, d), row),
                  pl.BlockSpec((tm, 2 * f), row), _resident(small.shape, lambda i: (0, 0)),
                  _resident(wgu.shape, lambda i: (0, 0, 0)), _resident(wd.shape, lambda i: (0, 0))],
        out_specs=[pl.BlockSpec((tm, d), row), pl.BlockSpec((tm, 2 * f), row), pl.BlockSpec((tm, d), row),
                   pl.BlockSpec((tm, f), row), pl.BlockSpec((8, d), lambda i: (0, 0))],
        args=[dh, h, ff, gu, small, wgu, wd])


def _bwd_conv(dh, h, y, proj, small, win, wout, after=()):
    t, d = h.shape
    tm = _token_tile(t)
    steps = t // tm
    pc = win.shape[-1]
    halo_blocks = tm // 16

    def body(dh_ref, h_ref, y_ref, proj_ref, halo_ref, sm_ref, win_ref, wout_ref,
             o_ref, dproj_ref, dy_ref, bc_ref, sg_ref, uext_ref, dcext_ref, carry_ref):
        i = pl.program_id(0)
        tile = steps - 1 - i

        @pl.when(i == 0)
        def _():
            sg_ref[...] = jnp.zeros_like(sg_ref)
            carry_ref[...] = jnp.zeros_like(carry_ref)

        dy = dh_ref[...]
        yh, r1 = _rms(y_ref[...])
        sg_ref[1:2, :] += _colsum(dy * yh)
        dyv = _rms_bwd(dy, yh, r1, sm_ref[5:6, :]).astype(BF16)
        dy_ref[...] = dyv
        dbc = _dot_nt(dyv, wout_ref[...])
        b = proj_ref[:, 0:d].astype(F32)
        cg = proj_ref[:, d:2 * d].astype(F32)
        v = proj_ref[:, 2 * d:].astype(F32)
        halo = halo_ref[...].astype(F32)[16 - CONV_HALO:]
        uh = halo[:, d:2 * d] * halo[:, 2 * d:]
        uext_ref[0:CONV_HALO, :] = jnp.where(tile > 0, uh, jnp.zeros_like(uh))
        uext_ref[CONV_HALO:, :] = cg * v
        taps = [sm_ref[8 + j:9 + j, :] for j in range(3)]
        full = uext_ref[...]
        u0 = full[CONV_HALO:]
        u1 = pltpu.roll(full, 1, 0)[CONV_HALO:]
        u2 = pltpu.roll(full, 2, 0)[CONV_HALO:]
        conv = u0 * taps[2] + u1 * taps[1] + u2 * taps[0]
        bc_ref[...] = (b * conv).astype(BF16)
        dconv = dbc * b
        sg_ref[4:5, :] += _colsum(dconv * u0)
        sg_ref[3:4, :] += _colsum(dconv * u1)
        sg_ref[2:3, :] += _colsum(dconv * u2)
        dcext_ref[0:tm, :] = dconv
        dcext_ref[tm:, :] = carry_ref[...]
        carry_ref[...] = dconv[0:CONV_HALO]
        dfull = dcext_ref[...]
        n8 = tm + CONV_HALO
        du = (dfull[0:tm] * taps[2] + pltpu.roll(dfull, n8 - 1, 0)[0:tm] * taps[1]
              + pltpu.roll(dfull, n8 - 2, 0)[0:tm] * taps[0])
        dproj_ref[:, 0:d] = (dbc * conv).astype(BF16)
        dproj_ref[:, d:2 * d] = (du * v).astype(BF16)
        dproj_ref[:, 2 * d:] = (du * cg).astype(BF16)
        dn = None
        for k in range(N_CHIPS):
            part = _dot_nt(dproj_ref[:, k * pc:(k + 1) * pc], win_ref[k])
            dn = part if dn is None else dn + part
        hh, r0 = _rms(h_ref[...])
        sg_ref[0:1, :] += _colsum(dn * hh)
        o_ref[...] = dy + _rms_bwd(dn, hh, r0, sm_ref[4:5, :])

    rev = lambda i: (steps - 1 - i, 0)
    before = lambda i: (jnp.maximum((steps - 1 - i) * halo_blocks - 1, 0), 0)
    return _gridded(
        body, after, name="bwd_conv", grid=(steps,),
        out_shape=[jax.ShapeDtypeStruct((t, d), F32), jax.ShapeDtypeStruct((t, 3 * d), BF16),
                   jax.ShapeDtypeStruct((t, d), BF16), jax.ShapeDtypeStruct((t, d), BF16),
                   jax.ShapeDtypeStruct((8, d), F32)],
        in_specs=[pl.BlockSpec((tm, d), rev), pl.BlockSpec((tm, d), rev), pl.BlockSpec((tm, d), rev),
                  pl.BlockSpec((tm, 3 * d), rev), pl.BlockSpec((16, 3 * d), before),
                  _resident(small.shape, lambda i: (0, 0)), _resident(win.shape, lambda i: (0, 0, 0)),
                  _resident(wout.shape, lambda i: (0, 0))],
        out_specs=[pl.BlockSpec((tm, d), rev), pl.BlockSpec((tm, 3 * d), rev), pl.BlockSpec((tm, d), rev),
                   pl.BlockSpec((tm, d), rev), pl.BlockSpec((8, d), lambda i: (0, 0))],
        scratch_shapes=[pltpu.VMEM((CONV_HALO + tm, d), F32), pltpu.VMEM((tm + CONV_HALO, d), F32),
                        pltpu.VMEM((CONV_HALO, d), F32)],
        args=[dh, h, y, proj, proj, small, win, wout])


def _bwd_pool(dh, x, small, scale, poolw, after=()):
    t, d = x.shape
    tm = _token_tile(t)
    steps = t // tm
    ng = len(POOL_WINDOWS)
    gw = d // ng
    halo_blocks = tm // POOL_HALO

    def body(dh_ref, x_ref, halo_ref, sm_ref, sc_ref, w_ref, o_ref, dw_ref, sg_ref,
             ext_ref, mix_ref, mm_ref, pb_ref, qext_ref, dhn_ref, carry_ref):
        i = pl.program_id(0)
        tile = steps - 1 - i

        @pl.when(i == 0)
        def _():
            sg_ref[...] = jnp.zeros_like(sg_ref)
            dw_ref[...] = jnp.zeros_like(dw_ref)
            carry_ref[...] = jnp.zeros_like(carry_ref)

        g0 = sm_ref[0:1, :]
        xv = x_ref[...]
        xh, r0 = _rms(xv)
        hx, _ = _rms(halo_ref[...])
        ext_ref[0:POOL_HALO, :] = jnp.where(tile > 0, hx * g0, jnp.zeros_like(hx))
        ext_ref[POOL_HALO:, :] = xh * g0
        for g in range(ng):
            pooled = _pool_windows(ext_ref, g, gw, tm, tile * tm)
            cols = slice(g * gw, (g + 1) * gw)
            pb = pooled.astype(BF16)
            pb_ref[:, cols] = pb
            mm = _dot(pb, w_ref[g])
            mm_ref[:, cols] = mm
            mix_ref[:, cols] = mm * sc_ref[:, cols]
        dy = dh_ref[...]
        mh, r1 = _rms(mix_ref[...])
        sg_ref[1:2, :] += _colsum(dy * mh)
        dmix = _rms_bwd(dy, mh, r1, sm_ref[1:2, :])
        sg_ref[2:3, :] += _colsum(dmix * mm_ref[...])
        mix_ref[...] = dmix * sc_ref[...]
        n16 = tm + POOL_HALO
        for g in range(ng):
            w = POOL_WINDOWS[g]
            cols = slice(g * gw, (g + 1) * gw)
            dmm = mix_ref[:, cols].astype(BF16)
            dpooled = _dot_nt(dmm, w_ref[g])
            dw_ref[g] += _dot_tn(pb_ref[:, cols], dmm)
            trow = tile * tm + lax.broadcasted_iota(jnp.int32, (tm, 1), 0)
            q = dpooled / jnp.minimum(trow + 1, w).astype(F32)
            qext_ref[0:tm, cols] = q
            qext_ref[tm:, cols] = carry_ref[:, cols]
            carry_ref[:, cols] = q[0:POOL_HALO]
            p, k = qext_ref[:, cols], 1
            while k < w:
                p = p + pltpu.roll(p, n16 - k, 0)
                k *= 2
            dhn_ref[:, cols] = p[0:tm] - dpooled
        dhn = dhn_ref[...]
        sg_ref[0:1, :] += _colsum(dhn * xh)
        o_ref[...] = dy + _rms_bwd(dhn, xh, r0, g0)

    rev = lambda i: (steps - 1 - i, 0)
    before = lambda i: (jnp.maximum((steps - 1 - i) * halo_blocks - 1, 0), 0)
    return _gridded(
        body, after, name="bwd_pool", grid=(steps,),
        out_shape=[jax.ShapeDtypeStruct((t, d), F32), jax.ShapeDtypeStruct((ng, gw, gw), F32),
                   jax.ShapeDtypeStruct((8, d), F32)],
        in_specs=[pl.BlockSpec((tm, d), rev), pl.BlockSpec((tm, d), rev), pl.BlockSpec((POOL_HALO, d), before),
                  _resident(small.shape, lambda i: (0, 0)), _resident(scale.shape, lambda i: (0, 0)),
                  _resident(poolw.shape, lambda i: (0, 0, 0))],
        out_specs=[pl.BlockSpec((tm, d), rev), pl.BlockSpec((ng, gw, gw), lambda i: (0, 0, 0)),
                   pl.BlockSpec((8, d), lambda i: (0, 0))],
        scratch_shapes=[pltpu.VMEM((POOL_HALO + tm, d), F32), pltpu.VMEM((tm, d), F32), pltpu.VMEM((tm, d), F32),
                        pltpu.VMEM((tm, d), BF16), pltpu.VMEM((tm + POOL_HALO, d), F32), pltpu.VMEM((tm, d), F32),
                        pltpu.VMEM((POOL_HALO, d), F32)],
        args=[dh, x, x, small, scale, poolw])


def _weight_grad(a, b, bm, bn, half_on, name, after=()):
    t, m = a.shape
    _, n = b.shape
    if half_on == "a":
        a_cols, b_cols = 2 * bm, bn
    else:
        a_cols, b_cols = bm, 2 * bn
    steps = max(m // a_cols, n // b_cols)

    def spec(cols, total):
        if cols == total:
            return _resident((t, cols), lambda p, j: (0, 0))
        return pl.BlockSpec((t, cols), lambda p, j: (0, j))

    def tile(a_ref, b_ref, half):
        if half_on == "a":
            return _dot_tn(a_ref[:, half * bm:(half + 1) * bm], b_ref[...])
        return _dot_tn(a_ref[...], b_ref[:, half * bn:(half + 1) * bn])

    def body(a_ref, b_ref, parts_ref, land_ref, acc_ref, stage_ref, got_ref, send_sems, recv_sems, got_sem):
        p, j = pl.program_id(0), pl.program_id(1)
        x, y, c, _ = _place()
        half = jnp.where(p == 0, 1 - c, c)

        def send(jj):
            return _remote(stage_ref.at[jj % 2], land_ref.at[jj], send_sems.at[jj], recv_sems.at[jj], (x, y, 1 - c))

        def fetch():
            return pltpu.make_async_copy(land_ref.at[j], got_ref, got_sem)

        @pl.when(p == 1)
        def _():
            @pl.when(j == 0)
            def _():
                for jj in range(max(steps - 2, 0), steps):
                    send(jj).wait_send()

            send(j).wait_recv()
            fetch().start()

        for hv in range(2):
            @pl.when(half == hv)
            def _():
                acc_ref[...] = tile(a_ref, b_ref, hv)

        @pl.when(p == 0)
        def _():
            @pl.when(j >= 2)
            def _():
                send(j - 2).wait_send()

            stage_ref[j % 2] = acc_ref[...].astype(BF16)
            send(j).start()

        @pl.when(p == 1)
        def _():
            fetch().wait()
            parts_ref[...] = (acc_ref[...] + got_ref[...].astype(F32)).astype(BF16)

    return _gridded(
        body, after, name=name, grid=(2, steps),
        out_shape=[jax.ShapeDtypeStruct((steps, bm, bn), BF16), jax.ShapeDtypeStruct((steps, bm, bn), BF16)],
        in_specs=[spec(a_cols, m), spec(b_cols, n)],
        out_specs=[pl.BlockSpec((None, bm, bn), lambda p, j: (p * j, 0, 0)), ANY],
        scratch_shapes=[pltpu.VMEM((bm, bn), F32), pltpu.VMEM((2, bm, bn), BF16), pltpu.VMEM((bm, bn), BF16),
                        DMA((steps,)), DMA((steps,)), DMA],
        args=[a, b])[0]


def _cast_layer(w, layer, name, after=()):
    _, r, c = w.shape
    rb = _row_block(r, c * (4 + 2) * 2)

    def body(w_ref, *rest):
        rest[-1][...] = w_ref[...].astype(BF16)

    return pl.pallas_call(
        body, name=name, grid=(r // rb,), out_shape=jax.ShapeDtypeStruct((r, c), BF16),
        in_specs=[pl.BlockSpec((None, rb, c), lambda i: (layer, i, 0))] + [ANY] * len(after),
        out_specs=pl.BlockSpec((rb, c), lambda i: (i, 0)),
        compiler_params=pltpu.CompilerParams(dimension_semantics=("parallel",), vmem_limit_bytes=VMEM_LIMIT),
    )(w, *after)


def _adamw_math(w, g, m, v):
    bc1 = 1.0 - ADAM_B1 ** ADAM_STEP
    bc2 = 1.0 - ADAM_B2 ** ADAM_STEP
    nm = ADAM_B1 * m + (1.0 - ADAM_B1) * g
    nv = ADAM_B2 * v + (1.0 - ADAM_B2) * (g * g)
    return -ADAM_LR * ((nm / bc1) / (jnp.sqrt(nv / bc2) + ADAM_EPS) + ADAM_WD * w), nm, nv


def _adamw_small(small_sum, where, gains, taps, scale):
    dq = gains[0].shape[-1]
    d = small_sum.shape[-1]

    def body(where_ref, mine_ref, all_ref, gw, gm, gv, tw, tm_, tv, sw, sm, sv,
             gg, gd, gnm, gnv, tg, td, tnm, tnv, sg, sd, snm, snv):
        for layer in range(gw.shape[0]):
            g = mine_ref[4 * layer:4 * layer + 4, :]
            gg[layer] = g
            gd[layer], gnm[layer], gnv[layer] = _adamw_math(gw[layer], g, gm[layer], gv[layer])
        g = mine_ref[8:8 + tw.shape[1], :]
        tg[0] = g
        td[0], tnm[0], tnv[0] = _adamw_math(tw[0], g, tm_[0], tv[0])
        g = all_ref[11:12, :]
        sg[...] = g
        sd[...], snm[...], snv[...] = _adamw_math(sw[...], g, sm[...], sv[...])

    full = lambda a: pl.BlockSpec(a.shape, lambda i, where_ref: (0,) * a.ndim)
    params = [*gains, *taps, *scale]
    outs = [gains[0]] * 4 + [taps[0]] * 4 + [scale[0]] * 4
    res = pl.pallas_call(
        body, name="adamw_small", out_shape=[jax.ShapeDtypeStruct(a.shape, F32) for a in outs],
        grid_spec=pltpu.PrefetchScalarGridSpec(
            num_scalar_prefetch=1, grid=(1,),
            in_specs=[pl.BlockSpec((16, dq), lambda i, where_ref: (0, where_ref[0])), pl.BlockSpec((16, d), lambda i, where_ref: (0, 0)),
                      *[full(a) for a in params]],
            out_specs=[full(a) for a in outs]),
    )(where, small_sum, small_sum, *params)
    return tuple(res[0:4]), tuple(res[4:8]), tuple(res[8:12])


def _adamw(w, g, m, v, name):
    r, c = w.shape
    rb = _row_block(r, c * (8 * 4 * 2 + 4 * 4))

    def body(w_ref, g_ref, m_ref, v_ref, d_ref, nm_ref, nv_ref, go_ref):
        gv = g_ref[...]
        go_ref[...] = gv
        d_ref[...], nm_ref[...], nv_ref[...] = _adamw_math(w_ref[...], gv, m_ref[...], v_ref[...])

    spec = pl.BlockSpec((rb, c), lambda i: (i, 0))
    return pl.pallas_call(
        body, name=name, grid=(r // rb,), out_shape=[jax.ShapeDtypeStruct((r, c), F32)] * 4,
        in_specs=[spec] * 4, out_specs=[spec] * 4,
        compiler_params=pltpu.CompilerParams(dimension_semantics=("parallel",), vmem_limit_bytes=VMEM_LIMIT),
    )(w, g, m, v)


def kernel(x, norm_gains, pool_w, pool_scale, conv_in_w, conv_w, conv_out_w, ffn_gate_up_w, ffn_down_w, loss_target, m_norm_gains, m_pool_w, m_pool_scale, m_conv_in_w, m_conv_w, m_conv_out_w, m_ffn_gate_up_w, m_ffn_down_w, v_norm_gains, v_pool_w, v_pool_scale, v_conv_in_w, v_conv_w, v_conv_out_w, v_ffn_gate_up_w, v_ffn_down_w):
    _, t, d = x.shape
    dq = d // N_CHIPS
    ng = len(POOL_WINDOWS)
    gw = d // ng
    fq = ffn_down_w.shape[1]
    f = N_CHIPS * fq
    fc = f // 2
    core = lax.axis_index("c")
    chip = 2 * lax.axis_index("x") + lax.axis_index("y")
    core_arr = jnp.reshape(core, (1,)).astype(jnp.int32)
    where_arr = jnp.stack([chip, core]).astype(jnp.int32)
    x2, target = x[0], loss_target[0]

    small_loc = jnp.concatenate(
        [norm_gains.reshape(8, dq), conv_w[0], jnp.zeros((5, dq), F32)], axis=0).reshape(1, 2, 8, dq)
    pool_loc = pool_w.astype(BF16).reshape(1, 2, ng // 2 * (gw // N_CHIPS), gw)
    wgu_loc = [_cast_layer(ffn_gate_up_w, 0, "cast_gate_up0").reshape(1, 2, d // 2, fc),
               ffn_gate_up_w[1:2].astype(BF16).reshape(1, 2, d // 2, fc)]
    wd1_loc = ffn_down_w[1:2].astype(BF16).reshape(1, 2, fq // 2, d)
    win_loc = conv_in_w.astype(BF16).reshape(1, 2, d // 2, -1)
    wout_loc = conv_out_w.astype(BF16).reshape(1, 2, dq // 2, d)

    def ffn_weights(wgu_f, wd_f):
        return wgu_f.reshape(N_CHIPS, d, fc), wd_f.reshape(f, d)

    ag0 = _SplitGather([(pool_loc, 0), (small_loc, 0), (wgu_loc[0], 0)])
    ag0.start("ag_start_gate_up0")
    wd0_loc = _cast_layer(ffn_down_w, 0, "cast_down0", [ag0.token]).reshape(1, 2, fq // 2, d)
    ag0d = _SplitGather([(wd0_loc, 0)])
    ag0d.start("ag_start_down0", [ag0.token])
    ag1 = _SplitGather([(win_loc, 0), (wout_loc, 0), (wgu_loc[1], 0), (wd1_loc, 0)])
    ag1.start("ag_start_layer1", [ag0d.token])
    pool_f, small_f = _pass_on(ag0.wait([0, 1], ag1.token, "ag_wait_first"), "ag_pass_first")
    poolw = pool_f.reshape(N_CHIPS, ng, gw // N_CHIPS, gw).transpose(1, 0, 2, 3).reshape(ng, gw, gw)
    small = small_f.transpose(1, 2, 0, 3).reshape(16, d)
    h1 = _fwd_pool(x2, small, pool_scale, poolw)
    (wgu0,) = _pass_on(ag0.wait([2], h1, "ag_wait_gate_up0"), "ag_pass_gate_up0")
    wgu0 = wgu0.reshape(N_CHIPS, d, fc)
    gu0, n0, act_fwd0 = _fwd_ffn_up(h1, small, wgu0, 0)
    wd0, win_f, wout_f = _pass_on(
        ag0d.wait([0], gu0, "ag_wait_down0") + ag1.wait([0, 1], gu0, "ag_wait_conv"), "ag_pass_down0_conv")
    wd0 = wd0.reshape(f, d)
    h2, ff0 = _fwd_ffn_down(h1, act_fwd0, small, wd0, 0)
    win_f, wout_f = win_f.reshape(N_CHIPS, d, -1), wout_f.reshape(d, d)
    h3, proj, y, nc = _fwd_conv(h2, small, win_f, wout_f)
    wgu1, wd1 = ffn_weights(*_pass_on(ag1.wait([2, 3], h3, "ag_wait_ffn1"), "ag_pass_ffn1"))
    dh4, gu1, ff1, n1, loss_blk = _fwd_ffn(h3, small, wgu1, wd1, 1, target=target)

    dh3, dgu1, dff1, act1, sg_f1 = _bwd_ffn(dh4, h3, ff1, gu1, small, wgu1, wd1, 1)
    parts_d1 = _weight_grad(act1, dff1, fc, d // 2, "b", "dw_down1")
    parts_gu1 = _weight_grad(n1, dgu1, d // 2, fc, "a", "dw_gate_up1")
    ex_ffn1 = _SplitExchange([parts_d1.reshape(N_CHIPS, fq, d // 2), parts_gu1])
    started = ex_ffn1.start("rs_start_ffn1")
    dh2, dproj, dyv, bcv, sg_c = _bwd_conv(dh3, h2, y, proj, small, win_f, wout_f, [started])
    parts_in = _weight_grad(nc, dproj, d // 2, 3 * d // N_CHIPS, "a", "dw_conv_in")
    parts_out = _weight_grad(bcv, dyv, dq // 2, d, "a", "dw_conv_out")
    ex_conv = _SplitExchange([parts_in, parts_out])
    started = ex_conv.start("rs_start_conv")
    dh1, dgu0, dff0, act0, sg_f0 = _bwd_ffn(dh2, h1, ff0, gu0, small, wgu0, wd0, 0, [started])
    parts_d0 = _weight_grad(act0, dff0, fc, d // 2, "b", "dw_down0")
    ex_d0 = _SplitExchange([parts_d0.reshape(N_CHIPS, fq, d // 2)])
    started = ex_d0.start("rs_start_down0")
    parts_gu0 = _weight_grad(n0, dgu0, d // 2, fc, "a", "dw_gate_up0", [started])
    ex_ffn0 = _SplitExchange([parts_gu0])
    started = ex_ffn0.start("rs_start_gate_up0")
    grad_x, dpool, sg_p = _bwd_pool(dh1, x2, small, pool_scale, poolw, [started])
    g_pool = dpool.astype(BF16).reshape(2, ng // 2, N_CHIPS, gw // N_CHIPS, gw).transpose(2, 0, 1, 3, 4).reshape(
        N_CHIPS, 2, ng // 2 * (gw // N_CHIPS), gw)
    small_g = jnp.concatenate(
        [sg_p[0:2], sg_f0[0:2], sg_c[0:2], sg_f1[0:2], sg_c[2:5], sg_p[2:3],
         jnp.broadcast_to(loss_blk[0:1, 0:1], (1, d)), jnp.zeros((3, d), F32)], axis=0)
    ex_pool = _SplitExchange([_add_sibling(g_pool, _sibling_exchange(g_pool, "rs_sibling_pool"), core_arr)])
    started = ex_pool.start("rs_start_pool")

    def update(w, g, m, v, name):
        if w.size * 4 * 8 <= STREAM_BUDGET // 4:
            def body(w_ref, g_ref, m_ref, v_ref, go_ref, d_ref, nm_ref, nv_ref):
                go_ref[...] = g_ref[...]
                d_ref[...], nm_ref[...], nv_ref[...] = _adamw_math(w_ref[...], g_ref[...], m_ref[...], v_ref[...])

            return tuple(pl.pallas_call(body, name="adamw_" + name, out_shape=[jax.ShapeDtypeStruct(w.shape, F32)] * 4)(
                w, g.reshape(w.shape), m, v))
        flat = (-1, w.shape[-1])
        dl, m2, v2, g2 = _adamw(w.reshape(flat), g.reshape(flat), m.reshape(flat), v.reshape(flat), "adamw_" + name)
        return tuple(o.reshape(w.shape) for o in (g2, dl, m2, v2))

    (parts_d1, parts_gu1), (recv_d1, recv_gu1) = ex_ffn1.wait([started], "rs_wait_ffn1")
    (parts_in, parts_out), (recv_in, recv_out) = ex_conv.wait([started], "rs_wait_conv")
    gs_gu = _add_chips(parts_gu1, recv_gu1, where_arr, 1, 2)
    gs_d = _add_chips(parts_d1, recv_d1, where_arr, 1, 2, col_half=True)
    gs_in = _add_chips(parts_in, recv_in, where_arr)
    gs_out = _add_chips(parts_out, recv_out, where_arr)
    full_in, full_out = _sibling_share([gs_in, gs_out], [False, False], "rs_share_conv")
    up_in = update(conv_in_w, full_in.reshape(1, d, -1), m_conv_in_w, v_conv_in_w, "conv_in")
    up_out = update(conv_out_w, full_out.reshape(1, dq, d), m_conv_out_w, v_conv_out_w, "conv_out")
    done_first = [up_in[1], up_out[1], gs_gu, gs_d]
    (parts_d0,), (recv_d0,) = ex_d0.wait(done_first, "rs_wait_down0")
    (parts_gu0,), (recv_gu0,) = ex_ffn0.wait(done_first, "rs_wait_gate_up0")
    (parts_p,), (recv_p,) = ex_pool.wait(done_first, "rs_wait_pool")
    ex_small = _SplitExchange([small_g], _SmallGather(small_g))
    started = ex_small.start("rs_start_small", [recv_gu0])
    gs_gu = _add_chips(parts_gu0, recv_gu0, where_arr, 0, 2, gs_gu)
    gs_d = _add_chips(parts_d0, recv_d0, where_arr, 0, 2, gs_d, col_half=True)
    gs_pool = _add_chips(parts_p, recv_p, where_arr)
    full_gu, full_d, full_pool = _sibling_share([gs_gu, gs_d, gs_pool], [False, True, False], "rs_share_ffn", [started])
    up_gu = update(ffn_gate_up_w, full_gu.reshape(2, d, fc), m_ffn_gate_up_w, v_ffn_gate_up_w, "gate_up")
    up_d = update(ffn_down_w, full_d.reshape(2, fq, d), m_ffn_down_w, v_ffn_down_w, "down")
    (small_own,), (small_all,) = ex_small.wait([up_gu[1], up_d[1]], "rs_wait_small")
    small_sum = _sum_small(small_all, small_own, (2 * chip + core).reshape(1).astype(jnp.int32))
    loss = small_sum[12, 0]
    up_gains, up_taps, up_scale = _adamw_small(
        small_sum, where_arr, (norm_gains, m_norm_gains, v_norm_gains), (conv_w, m_conv_w, v_conv_w),
        (pool_scale, m_pool_scale, v_pool_scale))

    ups = [
        up_gains,
        update(pool_w, full_pool.reshape(1, ng, gw // N_CHIPS, gw), m_pool_w, v_pool_w, "pool_w"),
        up_scale,
        up_in,
        up_taps,
        up_out,
        up_gu,
        up_d,
    ]
    grads_out, deltas, new_ms, new_vs = (list(col) for col in zip(*ups))
    return (loss, grad_x[None], *grads_out, *deltas, *new_ms, *new_vs)
```
